```python
import math
import jax
import jax.numpy as jnp
from jax import lax
import numpy as np

D_MODEL = 1024
BATCH = 8
SEQ = 8192
DEPTH = 1

N_MOD = 6
EPS = 1e-6
GM_CHUNK = 128
GM_HEAD = 128
GM_GROUPS = D_MODEL // GM_HEAD
GM_WIDTH = GM_GROUPS * GM_HEAD
SSM_INNER = 2 * D_MODEL
SSM_HEAD_DIM = 64
SSM_HEADS = SSM_INNER // SSM_HEAD_DIM
SSM_GROUPS = 8
SSM_STATE = 128
SSM_CONV = 4
SSM_CHUNK = 128
CONV_DIM = SSM_INNER + 2 * SSM_GROUPS * SSM_STATE
D_FF = 4 * D_MODEL
IN_SIZES = (GM_WIDTH, GM_WIDTH, SSM_INNER, CONV_DIM, SSM_HEADS, D_MODEL, D_MODEL)
IN_WIDTH = sum(IN_SIZES)

kernel_name = 'hybrid_sgu_ssd_block'


def _split_offsets(sizes):
    offs, acc = [], 0
    for s in sizes[:-1]:
        acc += s
        offs.append(acc)
    return offs


def rms_norm(x, w=None):
    xf = x.astype(jnp.float32)
    y = xf * lax.rsqrt(jnp.mean(jnp.square(xf), axis=-1, keepdims=True) + EPS)
    if w is not None:
        y = y * w.astype(jnp.float32)
    return y.astype(x.dtype)


def gated_group_rms_norm(y, z, w, groups):
    g = y.astype(jnp.float32) * jax.nn.silu(z.astype(jnp.float32))
    gs = g.reshape(*g.shape[:-1], groups, -1)
    gs = gs * lax.rsqrt(jnp.mean(jnp.square(gs), axis=-1, keepdims=True) + EPS)
    return (gs.reshape(g.shape) * w.astype(jnp.float32)).astype(y.dtype)


def sgu_branch(u, v, norm_w, ws, bs):
    b, s, _ = v.shape
    u = jax.nn.gelu(u)
    v = rms_norm(jax.nn.gelu(v), norm_w)
    vc = v.reshape(b, s // GM_CHUNK, GM_CHUNK, GM_GROUPS, GM_HEAD)
    causal = jnp.tril(jnp.ones((GM_CHUNK, GM_CHUNK), dtype=bool))
    ws_c = jnp.where(causal[None], ws, jnp.zeros_like(ws))
    sv = jnp.einsum('gij,bnjgd->bnigd', ws_c, vc) + bs.T[None, None, :, :, None]
    return u * sv.reshape(b, s, GM_WIDTH)


def ssd_scan(xs, dt, A, Bm, Cm):
    b, s, h, p = xs.shape
    g, n = Bm.shape[-2], Bm.shape[-1]
    hpg = h // g
    q = SSM_CHUNK
    nc = s // q

    def to_chunks(t):
        return jnp.swapaxes(t.reshape(b, nc, q, *t.shape[2:]), 0, 1)

    xdt = (xs.astype(jnp.float32) * dt[..., None]).reshape(b, s, g, hpg, p)
    a = (dt * A).reshape(b, s, g, hpg)
    causal = jnp.tril(jnp.ones((q, q), dtype=bool))

    def step(state, inp):
        x_c, a_c, B_c, C_c = inp
        cum = jnp.cumsum(a_c, axis=1)
        cum_t = jnp.moveaxis(cum, 1, -1)
        seg = cum_t[..., :, None] - cum_t[..., None, :]
        decay = jnp.exp(jnp.where(causal, seg, -jnp.inf))
        cb = jnp.einsum('bign,bjgn->bgij', C_c, B_c)
        y_intra = jnp.einsum('bgij,bghij,bjghp->bighp', cb, decay, x_c)
        y_inter = jnp.einsum('bign,bghpn->bighp', C_c, state) * jnp.exp(cum)[..., None]
        last = cum[:, -1]
        w_end = jnp.exp(last[:, None] - cum)
        new_state = state * jnp.exp(last)[..., None, None] + jnp.einsum(
            'bjgn,bjgh,bjghp->bghpn', B_c, w_end, x_c)
        return new_state, y_intra + y_inter

    state0 = jnp.zeros((b, g, hpg, p, n), jnp.float32)
    _, ys = lax.scan(step, state0, (to_chunks(xdt), to_chunks(a),
                                    to_chunks(Bm.astype(jnp.float32)),
                                    to_chunks(Cm.astype(jnp.float32))))
    return jnp.swapaxes(ys, 0, 1).reshape(b, s, h, p)


def ssd_branch(z, xbc, dt_raw, conv_w, conv_b, dt_bias, a_log, d_skip, norm_w):
    b, s, _ = xbc.shape
    xbc = lax.conv_general_dilated(
        xbc, conv_w[:, None, :], window_strides=(1,), padding=[(SSM_CONV - 1, 0)],
        dimension_numbers=('NWC', 'WIO', 'NWC'), feature_group_count=CONV_DIM) + conv_b
    xbc = jax.nn.silu(xbc)
    gn = SSM_GROUPS * SSM_STATE
    xs = xbc[..., :SSM_INNER].reshape(b, s, SSM_HEADS, SSM_HEAD_DIM)
    Bm = xbc[..., SSM_INNER:SSM_INNER + gn].reshape(b, s, SSM_GROUPS, SSM_STATE)
    Cm = xbc[..., SSM_INNER + gn:].reshape(b, s, SSM_GROUPS, SSM_STATE)
    dt = jax.nn.softplus(dt_raw.astype(jnp.float32) + dt_bias.astype(jnp.float32))
    A = -jnp.exp(a_log.astype(jnp.float32))
    y = ssd_scan(xs, dt, A, Bm, Cm)
    y = y + xs.astype(jnp.float32) * d_skip.astype(jnp.float32)[:, None]
    y = y.reshape(b, s, SSM_INNER).astype(z.dtype)
    return gated_group_rms_norm(y, z, norm_w, SSM_GROUPS)


def _fwd_setup_inputs(seed: int = 0) -> dict:
    key = jax.random.key(seed)
    ks = jax.random.split(key, 22)
    f = jnp.float32
    L = DEPTH

    def nrm(k, shape, fan_in):
        return jax.random.normal(k, shape, f) * (fan_in ** -0.5)

    x = jax.random.normal(ks[0], (BATCH, SEQ, D_MODEL), f)
    c = jax.random.normal(ks[1], (BATCH, D_MODEL), f)
    w_mod = 0.5 * nrm(ks[2], (L, D_MODEL, N_MOD * D_MODEL), D_MODEL)
    b_mod = 0.01 * jax.random.normal(ks[3], (L, N_MOD * D_MODEL), f)
    w_in = nrm(ks[4], (L, D_MODEL, IN_WIDTH), D_MODEL)
    gm_norm_w = 1.0 + 0.05 * jax.random.normal(ks[5], (L, GM_WIDTH), f)
    gm_ws = nrm(ks[6], (L, GM_GROUPS, GM_CHUNK, GM_CHUNK), GM_CHUNK)
    gm_bs = 1.0 + 0.1 * jax.random.normal(ks[7], (L, GM_GROUPS, GM_CHUNK), f)
    conv_w = nrm(ks[8], (L, SSM_CONV, CONV_DIM), SSM_CONV)
    conv_b = 0.01 * jax.random.normal(ks[9], (L, CONV_DIM), f)
    dt0 = jnp.exp(jax.random.uniform(ks[10], (L, SSM_HEADS), f,
                                     minval=math.log(1e-3), maxval=math.log(1e-1)))
    dt_bias = dt0 + jnp.log(-jnp.expm1(-dt0))
    a_log = jnp.log(jax.random.uniform(ks[11], (L, SSM_HEADS), f, minval=1.0, maxval=16.0))
    d_skip = 1.0 + 0.1 * jax.random.normal(ks[12], (L, SSM_HEADS), f)
    ssm_norm_w = 1.0 + 0.05 * jax.random.normal(ks[13], (L, SSM_INNER), f)
    w_branch_gm = nrm(ks[14], (L, GM_WIDTH, D_MODEL), GM_WIDTH)
    w_branch_ssm = nrm(ks[15], (L, SSM_INNER, D_MODEL), SSM_INNER)
    w_out = nrm(ks[16], (L, D_MODEL, D_MODEL), D_MODEL)
    w_ff1 = nrm(ks[17], (L, D_MODEL, D_FF), D_MODEL)
    w_ff2 = nrm(ks[18], (L, D_FF, D_MODEL), D_FF)
    final_norm_w = 1.0 + 0.05 * jax.random.normal(ks[19], (D_MODEL,), f)
    return {'x': x, 'c': c, 'w_mod': w_mod, 'b_mod': b_mod, 'w_in': w_in,
            'gm_norm_w': gm_norm_w, 'gm_ws': gm_ws, 'gm_bs': gm_bs,
            'conv_w': conv_w, 'conv_b': conv_b, 'dt_bias': dt_bias, 'a_log': a_log,
            'd_skip': d_skip, 'ssm_norm_w': ssm_norm_w, 'w_branch_gm': w_branch_gm,
            'w_branch_ssm': w_branch_ssm, 'w_out': w_out, 'w_ff1': w_ff1,
            'w_ff2': w_ff2, 'final_norm_w': final_norm_w}


def _fwd_reference(x, c, w_mod, b_mod, w_in, gm_norm_w, gm_ws, gm_bs, conv_w, conv_b,
              dt_bias, a_log, d_skip, ssm_norm_w, w_branch_gm, w_branch_ssm, w_out,
              w_ff1, w_ff2, final_norm_w):
    c_act = jax.nn.silu(c)
    offs = _split_offsets(IN_SIZES)
    for l in range(DEPTH):
        mod = (c_act @ w_mod[l] + b_mod[l])[:, None, :]
        sh1, sc1, g1, sh2, sc2, g2 = jnp.split(mod, N_MOD, axis=-1)
        h = rms_norm(x) * (1.0 + sc1) + sh1
        proj = h @ w_in[l]
        u, v, z, xbc, dt_raw, gate_a, gate_b = jnp.split(proj, offs, axis=-1)
        y_a = sgu_branch(u, v, gm_norm_w[l], gm_ws[l], gm_bs[l])
        y_b = ssd_branch(z, xbc, dt_raw, conv_w[l], conv_b[l], dt_bias[l], a_log[l],
                         d_skip[l], ssm_norm_w[l])
        mixed = (jax.nn.sigmoid(gate_a) * (y_a @ w_branch_gm[l])
                 + jax.nn.sigmoid(gate_b) * (y_b @ w_branch_ssm[l]))
        x = x + g1 * (mixed @ w_out[l])
        h2 = rms_norm(x) * (1.0 + sc2) + sh2
        x = x + g2 * (jnp.square(jax.nn.relu(h2 @ w_ff1[l])) @ w_ff2[l])
    return rms_norm(x, final_norm_w)


import jax as _jax
import jax.numpy as _jnp

TWIN_FORMAT = 'train_step'
FWD_PARAMS = ['x', 'c', 'w_mod', 'b_mod', 'w_in', 'gm_norm_w', 'gm_ws', 'gm_bs', 'conv_w', 'conv_b', 'dt_bias', 'a_log', 'd_skip', 'ssm_norm_w', 'w_branch_gm', 'w_branch_ssm', 'w_out', 'w_ff1', 'w_ff2', 'final_norm_w']
TWIN_WEIGHTS = ['w_mod', 'b_mod', 'w_in', 'gm_norm_w', 'gm_ws', 'gm_bs', 'conv_w', 'conv_b', 'dt_bias', 'a_log', 'd_skip', 'ssm_norm_w', 'w_branch_gm', 'w_branch_ssm', 'w_out', 'w_ff1', 'w_ff2', 'final_norm_w']
TWIN_DIFF_INPUT = 'x'
TWIN_INPUTS = ['x', 'c', 'w_mod', 'b_mod', 'w_in', 'gm_norm_w', 'gm_ws', 'gm_bs', 'conv_w', 'conv_b', 'dt_bias', 'a_log', 'd_skip', 'ssm_norm_w', 'w_branch_gm', 'w_branch_ssm', 'w_out', 'w_ff1', 'w_ff2', 'final_norm_w', 'loss_target', 'm_w_mod', 'm_b_mod', 'm_w_in', 'm_gm_norm_w', 'm_gm_ws', 'm_gm_bs', 'm_conv_w', 'm_conv_b', 'm_dt_bias', 'm_a_log', 'm_d_skip', 'm_ssm_norm_w', 'm_w_branch_gm', 'm_w_branch_ssm', 'm_w_out', 'm_w_ff1', 'm_w_ff2', 'm_final_norm_w', 'v_w_mod', 'v_b_mod', 'v_w_in', 'v_gm_norm_w', 'v_gm_ws', 'v_gm_bs', 'v_conv_w', 'v_conv_b', 'v_dt_bias', 'v_a_log', 'v_d_skip', 'v_ssm_norm_w', 'v_w_branch_gm', 'v_w_branch_ssm', 'v_w_out', 'v_w_ff1', 'v_w_ff2', 'v_final_norm_w']
TWIN_OUTPUTS = ['loss', 'grad_x', 'grad_w_mod', 'grad_b_mod', 'grad_w_in', 'grad_gm_norm_w', 'grad_gm_ws', 'grad_gm_bs', 'grad_conv_w', 'grad_conv_b', 'grad_dt_bias', 'grad_a_log', 'grad_d_skip', 'grad_ssm_norm_w', 'grad_w_branch_gm', 'grad_w_branch_ssm', 'grad_w_out', 'grad_w_ff1', 'grad_w_ff2', 'grad_final_norm_w', 'delta_w_mod', 'delta_b_mod', 'delta_w_in', 'delta_gm_norm_w', 'delta_gm_ws', 'delta_gm_bs', 'delta_conv_w', 'delta_conv_b', 'delta_dt_bias', 'delta_a_log', 'delta_d_skip', 'delta_ssm_norm_w', 'delta_w_branch_gm', 'delta_w_branch_ssm', 'delta_w_out', 'delta_w_ff1', 'delta_w_ff2', 'delta_final_norm_w', 'new_m_w_mod', 'new_m_b_mod', 'new_m_w_in', 'new_m_gm_norm_w', 'new_m_gm_ws', 'new_m_gm_bs', 'new_m_conv_w', 'new_m_conv_b', 'new_m_dt_bias', 'new_m_a_log', 'new_m_d_skip', 'new_m_ssm_norm_w', 'new_m_w_branch_gm', 'new_m_w_branch_ssm', 'new_m_w_out', 'new_m_w_ff1', 'new_m_w_ff2', 'new_m_final_norm_w', 'new_v_w_mod', 'new_v_b_mod', 'new_v_w_in', 'new_v_gm_norm_w', 'new_v_gm_ws', 'new_v_gm_bs', 'new_v_conv_w', 'new_v_conv_b', 'new_v_dt_bias', 'new_v_a_log', 'new_v_d_skip', 'new_v_ssm_norm_w', 'new_v_w_branch_gm', 'new_v_w_branch_ssm', 'new_v_w_out', 'new_v_w_ff1', 'new_v_w_ff2', 'new_v_final_norm_w']
TWIN_LEAF_KINDS = {'loss': 'loss', 'grad_x': 'grad_x', 'grad_w_mod': 'grad_w', 'grad_b_mod': 'grad_w', 'grad_w_in': 'grad_w', 'grad_gm_norm_w': 'grad_w', 'grad_gm_ws': 'grad_w', 'grad_gm_bs': 'grad_w', 'grad_conv_w': 'grad_w', 'grad_conv_b': 'grad_w', 'grad_dt_bias': 'grad_w', 'grad_a_log': 'grad_w', 'grad_d_skip': 'grad_w', 'grad_ssm_norm_w': 'grad_w', 'grad_w_branch_gm': 'grad_w', 'grad_w_branch_ssm': 'grad_w', 'grad_w_out': 'grad_w', 'grad_w_ff1': 'grad_w', 'grad_w_ff2': 'grad_w', 'grad_final_norm_w': 'grad_w', 'delta_w_mod': 'delta_w', 'delta_b_mod': 'delta_w', 'delta_w_in': 'delta_w', 'delta_gm_norm_w': 'delta_w', 'delta_gm_ws': 'delta_w', 'delta_gm_bs': 'delta_w', 'delta_conv_w': 'delta_w', 'delta_conv_b': 'delta_w', 'delta_dt_bias': 'delta_w', 'delta_a_log': 'delta_w', 'delta_d_skip': 'delta_w', 'delta_ssm_norm_w': 'delta_w', 'delta_w_branch_gm': 'delta_w', 'delta_w_branch_ssm': 'delta_w', 'delta_w_out': 'delta_w', 'delta_w_ff1': 'delta_w', 'delta_w_ff2': 'delta_w', 'delta_final_norm_w': 'delta_w', 'new_m_w_mod': 'new_m', 'new_m_b_mod': 'new_m', 'new_m_w_in': 'new_m', 'new_m_gm_norm_w': 'new_m', 'new_m_gm_ws': 'new_m', 'new_m_gm_bs': 'new_m', 'new_m_conv_w': 'new_m', 'new_m_conv_b': 'new_m', 'new_m_dt_bias': 'new_m', 'new_m_a_log': 'new_m', 'new_m_d_skip': 'new_m', 'new_m_ssm_norm_w': 'new_m', 'new_m_w_branch_gm': 'new_m', 'new_m_w_branch_ssm': 'new_m', 'new_m_w_out': 'new_m', 'new_m_w_ff1': 'new_m', 'new_m_w_ff2': 'new_m', 'new_m_final_norm_w': 'new_m', 'new_v_w_mod': 'new_v', 'new_v_b_mod': 'new_v', 'new_v_w_in': 'new_v', 'new_v_gm_norm_w': 'new_v', 'new_v_gm_ws': 'new_v', 'new_v_gm_bs': 'new_v', 'new_v_conv_w': 'new_v', 'new_v_conv_b': 'new_v', 'new_v_dt_bias': 'new_v', 'new_v_a_log': 'new_v', 'new_v_d_skip': 'new_v', 'new_v_ssm_norm_w': 'new_v', 'new_v_w_branch_gm': 'new_v', 'new_v_w_branch_ssm': 'new_v', 'new_v_w_out': 'new_v', 'new_v_w_ff1': 'new_v', 'new_v_w_ff2': 'new_v', 'new_v_final_norm_w': 'new_v'}


def _forward(args):
    return _fwd_reference(*[args[k] for k in FWD_PARAMS])


def _output_shape():
    out = _jax.eval_shape(lambda: _forward(_fwd_setup_inputs(0)))
    return out.shape, out.dtype

N_MICROBATCH = 1
ADAM_LR = 0.001
ADAM_B1 = 0.9
ADAM_B2 = 0.999
ADAM_EPS = 1e-08
ADAM_WD = 0.01
ADAM_STEP = 10
PER_EXAMPLE_BATCH_AXIS = {'x': 0, 'c': 0, 'loss_target': 0}
SHARED_INPUTS = []
_WEIGHT_DTYPES = {'w_mod': _jnp.float32, 'b_mod': _jnp.float32, 'w_in': _jnp.float32, 'gm_norm_w': _jnp.float32, 'gm_ws': _jnp.float32, 'gm_bs': _jnp.float32, 'conv_w': _jnp.float32, 'conv_b': _jnp.float32, 'dt_bias': _jnp.float32, 'a_log': _jnp.float32, 'd_skip': _jnp.float32, 'ssm_norm_w': _jnp.float32, 'w_branch_gm': _jnp.float32, 'w_branch_ssm': _jnp.float32, 'w_out': _jnp.float32, 'w_ff1': _jnp.float32, 'w_ff2': _jnp.float32, 'final_norm_w': _jnp.float32}
MOMENT_SCALE = {'w_mod': 2.595121e-01, 'b_mod': 5.281226e-01, 'w_in': 2.602395e-02, 'gm_norm_w': 2.243963e-02, 'gm_ws': 2.182490e-02, 'gm_bs': 3.146316e-02, 'conv_w': 2.346620e-02, 'conv_b': 3.501296e-02, 'dt_bias': 8.106647e-02, 'a_log': 2.343082e-01, 'd_skip': 1.464788e-01, 'ssm_norm_w': 3.286985e-02, 'w_branch_gm': 4.072797e-02, 'w_branch_ssm': 4.361928e-02, 'w_out': 6.027274e-02, 'w_ff1': 6.051473e-02, 'w_ff2': 1.430101e-01, 'final_norm_w': 6.444141e+01}


def _to_microbatches(a, axis):
    t = _jnp.moveaxis(a, axis, 0)
    t = t.reshape((N_MICROBATCH, t.shape[0] // N_MICROBATCH) + t.shape[1:])
    return _jnp.moveaxis(t, 1, axis + 1)


def setup_inputs(seed: int = 0) -> dict:
    inp = _fwd_setup_inputs(seed)
    key = _jax.random.fold_in(_jax.random.key(seed), 7919)
    shape, _ = _output_shape()
    out = dict(inp)
    out["loss_target"] = _jax.random.normal(_jax.random.fold_in(key, 0), shape, _jnp.float32)
    for i, name in enumerate(TWIN_WEIGHTS):
        w = inp[name].astype(_jnp.float32)
        if MOMENT_SCALE is None:
            s = _jnp.sqrt(_jnp.mean(_jnp.square(w)) + 1e-30)
        else:
            s = MOMENT_SCALE[name]
        km, kv = _jax.random.split(_jax.random.fold_in(key, i + 1))
        out[name] = w
        out["m_" + name] = s * _jax.random.normal(km, w.shape, _jnp.float32)
        out["v_" + name] = (s * s) * _jax.random.uniform(kv, w.shape, _jnp.float32, 0.5, 1.5)
    if N_MICROBATCH > 1:
        for name, axis in PER_EXAMPLE_BATCH_AXIS.items():
            out[name] = _to_microbatches(out[name], axis)
    return {'x': out['x'], 'c': out['c'], 'w_mod': out['w_mod'], 'b_mod': out['b_mod'], 'w_in': out['w_in'], 'gm_norm_w': out['gm_norm_w'], 'gm_ws': out['gm_ws'], 'gm_bs': out['gm_bs'], 'conv_w': out['conv_w'], 'conv_b': out['conv_b'], 'dt_bias': out['dt_bias'], 'a_log': out['a_log'], 'd_skip': out['d_skip'], 'ssm_norm_w': out['ssm_norm_w'], 'w_branch_gm': out['w_branch_gm'], 'w_branch_ssm': out['w_branch_ssm'], 'w_out': out['w_out'], 'w_ff1': out['w_ff1'], 'w_ff2': out['w_ff2'], 'final_norm_w': out['final_norm_w'], 'loss_target': out['loss_target'], 'm_w_mod': out['m_w_mod'], 'm_b_mod': out['m_b_mod'], 'm_w_in': out['m_w_in'], 'm_gm_norm_w': out['m_gm_norm_w'], 'm_gm_ws': out['m_gm_ws'], 'm_gm_bs': out['m_gm_bs'], 'm_conv_w': out['m_conv_w'], 'm_conv_b': out['m_conv_b'], 'm_dt_bias': out['m_dt_bias'], 'm_a_log': out['m_a_log'], 'm_d_skip': out['m_d_skip'], 'm_ssm_norm_w': out['m_ssm_norm_w'], 'm_w_branch_gm': out['m_w_branch_gm'], 'm_w_branch_ssm': out['m_w_branch_ssm'], 'm_w_out': out['m_w_out'], 'm_w_ff1': out['m_w_ff1'], 'm_w_ff2': out['m_w_ff2'], 'm_final_norm_w': out['m_final_norm_w'], 'v_w_mod': out['v_w_mod'], 'v_b_mod': out['v_b_mod'], 'v_w_in': out['v_w_in'], 'v_gm_norm_w': out['v_gm_norm_w'], 'v_gm_ws': out['v_gm_ws'], 'v_gm_bs': out['v_gm_bs'], 'v_conv_w': out['v_conv_w'], 'v_conv_b': out['v_conv_b'], 'v_dt_bias': out['v_dt_bias'], 'v_a_log': out['v_a_log'], 'v_d_skip': out['v_d_skip'], 'v_ssm_norm_w': out['v_ssm_norm_w'], 'v_w_branch_gm': out['v_w_branch_gm'], 'v_w_branch_ssm': out['v_w_branch_ssm'], 'v_w_out': out['v_w_out'], 'v_w_ff1': out['v_w_ff1'], 'v_w_ff2': out['v_w_ff2'], 'v_final_norm_w': out['v_final_norm_w']}


def _loss(weights, diff, rest, loss_target):
    with _jax.named_scope("forward"):
        args = {**rest, TWIN_DIFF_INPUT: diff, **{k: w.astype(_WEIGHT_DTYPES[k]) for k, w in weights.items()}}
        y = _forward(args)
    with _jax.named_scope("loss_head"):
        err = _jnp.square(y.astype(_jnp.float32) - loss_target)
        return 0.5 * _jnp.sum(_jnp.mean(err, axis=-1)) if err.ndim else 0.5 * err


def _adamw(w, g, m, v):
    m = ADAM_B1 * m + (1.0 - ADAM_B1) * g
    v = ADAM_B2 * v + (1.0 - ADAM_B2) * _jnp.square(g)
    m_hat = m / (1.0 - ADAM_B1 ** ADAM_STEP)
    v_hat = v / (1.0 - ADAM_B2 ** ADAM_STEP)
    delta = -ADAM_LR * (m_hat / (_jnp.sqrt(v_hat) + ADAM_EPS) + ADAM_WD * w)
    return delta, m, v


def reference(x, c, w_mod, b_mod, w_in, gm_norm_w, gm_ws, gm_bs, conv_w, conv_b, dt_bias, a_log, d_skip, ssm_norm_w, w_branch_gm, w_branch_ssm, w_out, w_ff1, w_ff2, final_norm_w, loss_target, m_w_mod, m_b_mod, m_w_in, m_gm_norm_w, m_gm_ws, m_gm_bs, m_conv_w, m_conv_b, m_dt_bias, m_a_log, m_d_skip, m_ssm_norm_w, m_w_branch_gm, m_w_branch_ssm, m_w_out, m_w_ff1, m_w_ff2, m_final_norm_w, v_w_mod, v_b_mod, v_w_in, v_gm_norm_w, v_gm_ws, v_gm_bs, v_conv_w, v_conv_b, v_dt_bias, v_a_log, v_d_skip, v_ssm_norm_w, v_w_branch_gm, v_w_branch_ssm, v_w_out, v_w_ff1, v_w_ff2, v_final_norm_w):
    given = dict(x=x, c=c, w_mod=w_mod, b_mod=b_mod, w_in=w_in, gm_norm_w=gm_norm_w, gm_ws=gm_ws, gm_bs=gm_bs, conv_w=conv_w, conv_b=conv_b, dt_bias=dt_bias, a_log=a_log, d_skip=d_skip, ssm_norm_w=ssm_norm_w, w_branch_gm=w_branch_gm, w_branch_ssm=w_branch_ssm, w_out=w_out, w_ff1=w_ff1, w_ff2=w_ff2, final_norm_w=final_norm_w, loss_target=loss_target, m_w_mod=m_w_mod, m_b_mod=m_b_mod, m_w_in=m_w_in, m_gm_norm_w=m_gm_norm_w, m_gm_ws=m_gm_ws, m_gm_bs=m_gm_bs, m_conv_w=m_conv_w, m_conv_b=m_conv_b, m_dt_bias=m_dt_bias, m_a_log=m_a_log, m_d_skip=m_d_skip, m_ssm_norm_w=m_ssm_norm_w, m_w_branch_gm=m_w_branch_gm, m_w_branch_ssm=m_w_branch_ssm, m_w_out=m_w_out, m_w_ff1=m_w_ff1, m_w_ff2=m_w_ff2, m_final_norm_w=m_final_norm_w, v_w_mod=v_w_mod, v_b_mod=v_b_mod, v_w_in=v_w_in, v_gm_norm_w=v_gm_norm_w, v_gm_ws=v_gm_ws, v_gm_bs=v_gm_bs, v_conv_w=v_conv_w, v_conv_b=v_conv_b, v_dt_bias=v_dt_bias, v_a_log=v_a_log, v_d_skip=v_d_skip, v_ssm_norm_w=v_ssm_norm_w, v_w_branch_gm=v_w_branch_gm, v_w_branch_ssm=v_w_branch_ssm, v_w_out=v_w_out, v_w_ff1=v_w_ff1, v_w_ff2=v_w_ff2, v_final_norm_w=v_final_norm_w)
    weights = {n: given[n] for n in TWIN_WEIGHTS}
    shared = {n: given[n] for n in SHARED_INPUTS}
    per_example = {n: given[n] for n in ['x', 'c']}
    grad_fn = _jax.value_and_grad(_loss, argnums=(0, 1))

    def one_microbatch(ex, loss_target):
        ex = dict(ex)
        diff = ex.pop(TWIN_DIFF_INPUT)
        return grad_fn(weights, diff, {**shared, **ex}, loss_target)

    if N_MICROBATCH == 1:
        loss, (grad_w, grad_x) = one_microbatch(per_example, given["loss_target"])
    else:
        def body(carry, xs):
            loss_sum, grad_sum = carry
            l_k, (gw_k, gx_k) = one_microbatch(xs[0], xs[1])
            with _jax.named_scope("update"):
                return (loss_sum + l_k, _jax.tree.map(_jnp.add, grad_sum, gw_k)), gx_k

        init = (_jnp.zeros((), _jnp.float32), _jax.tree.map(_jnp.zeros_like, weights))
        (loss, grad_w), grad_x = _jax.lax.scan(body, init, (per_example, given["loss_target"]))
    with _jax.named_scope("update"):
        delta_w, new_m, new_v = {}, {}, {}
        for n in TWIN_WEIGHTS:
            delta_w[n], new_m[n], new_v[n] = _adamw(weights[n], grad_w[n], given["m_" + n], given["v_" + n])
    return (loss, grad_x, *[grad_w[n] for n in TWIN_WEIGHTS], *[delta_w[n] for n in TWIN_WEIGHTS],
            *[new_m[n] for n in TWIN_WEIGHTS], *[new_v[n] for n in TWIN_WEIGHTS])
```

```python
import functools

import jax
import jax.numpy as jnp
from jax import lax
from jax.experimental import pallas as pl
from jax.experimental.pallas import tpu as pltpu

F32 = jnp.float32
BF16 = jnp.bfloat16
MESH = pl.DeviceIdType.MESH
HIGHEST = lax.Precision.HIGHEST

D = 1024
EPS = 1e-6
Q = 128
GM_G = 8
NHEAD = 32
NGRP = 8
DI = 2048
CONV = 4096
DFF = 4096
W_IN = 10272
W_IN_R = 10368
OFF_Z, OFF_XBC, OFF_GA, OFF_DT = 2048, 4096, 8192, 10240
N_CHIP = 4
N_DEV = 8

ADAM_LR, ADAM_B1, ADAM_B2, ADAM_EPS, ADAM_WD, ADAM_STEP = 0.001, 0.9, 0.999, 1e-08, 0.01, 10

ANY = pl.BlockSpec(memory_space=pl.ANY)
VMEM = pl.BlockSpec(memory_space=pltpu.VMEM)


def _full(shape):
    return pl.BlockSpec(shape, lambda *_: (0,) * len(shape))


def _pick(n, prefs):
    for p in prefs:
        if n % p == 0:
            return p
    return n


def _rms(x):
    return x * lax.rsqrt(jnp.mean(x * x, axis=-1, keepdims=True) + EPS)


def _modnorm(x, sc, sh):
    return _rms(x) * (1.0 + sc) + sh


def _sgu_pre(u, v, w):
    return jax.nn.gelu(u), _rms(jax.nn.gelu(v)) * w


def _gatenorm(y, z, w):
    g = y * jax.nn.silu(z)
    return _rms(g) * w


def _mix(ga, gb, ba, bb):
    return jax.nn.sigmoid(ga) * ba + jax.nn.sigmoid(gb) * bb


def _loss_tile(x1, fo, g2, fw, t):
    x2 = x1 + g2 * fo
    y = _rms(x2) * fw
    err = jnp.square(y - t)
    return 0.5 * jnp.sum(jnp.mean(err, axis=-1))


def _tril(n):
    r = lax.broadcasted_iota(jnp.int32, (n, n), 0)
    c = lax.broadcasted_iota(jnp.int32, (n, n), 1)
    return r >= c


def _dt_prep(dtr, dtb, alog):
    dt = jax.nn.softplus(dtr + dtb)
    a = dt * (-jnp.exp(alog))
    ones = _tril(Q).astype(F32)
    cum = jnp.dot(ones, a, precision=HIGHEST, preferred_element_type=F32)
    cum_t = lax.dot_general(a, ones, (((0,), (1,)), ((), ())), precision=HIGHEST, preferred_element_type=F32)
    return dt, cum, cum_t


def _ssd_group(x0, x1, s0, s1, bm, cm, dt, cum, cum_t, dsk, grp):
    lane = lax.broadcasted_iota(jnp.int32, (1, 128), 1)
    sub = lax.broadcasted_iota(jnp.int32, (128, 1), 0)
    causal = _tril(Q)
    half = lane < 64
    half_rows = sub < 64
    bmb = bm.astype(BF16)
    cmb = cm.astype(BF16)
    cb = lax.dot_general(cmb, bmb, (((1,), (1,)), ((), ())), preferred_element_type=F32)

    def col(v, h):
        return jnp.sum(jnp.where(lane == h, v, 0.0), axis=1, keepdims=True)

    def row(v, h):
        return jnp.sum(jnp.where(sub == h, v, 0.0), axis=0, keepdims=True)

    def last(c):
        return jnp.sum(jnp.where(sub == Q - 1, c, 0.0), axis=0, keepdims=True)

    outs, states = [], []
    for p, (xp, sp) in enumerate(((x0, s0), (x1, s1))):
        h_a = 4 * grp + 2 * p
        h_b = h_a + 1
        dt_a, dt_b = col(dt, h_a), col(dt, h_b)
        cum_a, cum_b = col(cum, h_a), col(cum, h_b)
        row_a, row_b = row(cum_t, h_a), row(cum_t, h_b)
        last_a, last_b = last(cum_a), last(cum_b)
        xdt = xp * jnp.where(half, dt_a, dt_b)
        xdtb = xdt.astype(BF16)
        m_a = (cb * jnp.exp(jnp.where(causal, cum_a - row_a, -jnp.inf))).astype(BF16)
        m_b = (cb * jnp.exp(jnp.where(causal, cum_b - row_b, -jnp.inf))).astype(BF16)
        y_intra = jnp.where(half, jnp.dot(m_a, xdtb, preferred_element_type=F32),
                            jnp.dot(m_b, xdtb, preferred_element_type=F32))
        y_inter = lax.dot_general(cmb, sp.astype(BF16), (((1,), (1,)), ((), ())), preferred_element_type=F32)
        y_inter = y_inter * jnp.where(half, jnp.exp(cum_a), jnp.exp(cum_b))
        w_end = jnp.where(half, jnp.exp(last_a - cum_a), jnp.exp(last_b - cum_b))
        upd = lax.dot_general((xdt * w_end).astype(BF16), bmb, (((0,), (0,)), ((), ())), preferred_element_type=F32)
        states.append(sp * jnp.where(half_rows, jnp.exp(last_a), jnp.exp(last_b)) + upd)
        outs.append(y_intra + y_inter + xp * jnp.where(half, col(dsk, h_a), col(dsk, h_b)))
    return outs[0], outs[1], states[0], states[1]


def _matmul(a, b, mode, *, name, out_dtypes=(F32,), epi=None, epi_ins=(), tm=1024, tn=1024, tk=1024):
    if mode == "nn":
        (m, k), n = a.shape, b.shape[1]
    elif mode == "nt":
        (m, k), n = a.shape, b.shape[0]
    else:
        (k, m), n = a.shape, b.shape[1]
    tm, tn, tk = _pick(m, (tm, 512, 256, 128)), _pick(n, (tn, 1152, 1024, 512, 384, 256, 128)), _pick(k, (tk, 1152, 1024, 512, 256, 128))
    nk = k // tk
    if mode == "nn":
        a_spec = pl.BlockSpec((tm, tk), lambda i, j, kk: (i, kk))
        b_spec = pl.BlockSpec((tk, tn), lambda i, j, kk: (kk, j))
        dims = (((1,), (0,)), ((), ()))
    elif mode == "nt":
        a_spec = pl.BlockSpec((tm, tk), lambda i, j, kk: (i, kk))
        b_spec = pl.BlockSpec((tn, tk), lambda i, j, kk: (j, kk))
        dims = (((1,), (1,)), ((), ()))
    else:
        a_spec = pl.BlockSpec((tk, tm), lambda i, j, kk: (kk, i))
        b_spec = pl.BlockSpec((tk, tn), lambda i, j, kk: (kk, j))
        dims = (((0,), (0,)), ((), ()))
    o_spec = pl.BlockSpec((tm, tn), lambda i, j, kk: (i, j))
    n_epi, n_out = len(epi_ins), len(out_dtypes)

    def body(*refs):
        a_ref, b_ref = refs[0], refs[1]
        e_refs = refs[2:2 + n_epi]
        o_refs = refs[2 + n_epi:2 + n_epi + n_out]
        acc_ref = refs[-1]

        def finish(acc):
            outs = epi(acc, *[e[...] for e in e_refs]) if epi is not None else (acc,)
            for o_ref, val in zip(o_refs, outs):
                o_ref[...] = val.astype(o_ref.dtype)

        part = lax.dot_general(a_ref[...], b_ref[...], dims, preferred_element_type=F32)
        if nk == 1:
            finish(part)
        else:
            kk = pl.program_id(2)

            @pl.when(kk == 0)
            def _():
                acc_ref[...] = part

            @pl.when(kk > 0)
            def _():
                acc_ref[...] += part

            @pl.when(kk == nk - 1)
            def _():
                finish(acc_ref[...])

    return pl.pallas_call(
        body, name=name, grid=(m // tm, n // tn, nk),
        in_specs=[a_spec, b_spec] + [o_spec] * n_epi,
        out_specs=[o_spec] * n_out,
        out_shape=[jax.ShapeDtypeStruct((m, n), dt) for dt in out_dtypes],
        scratch_shapes=[pltpu.VMEM((tm, tn) if nk > 1 else (8, 128), F32)],
        compiler_params=pltpu.CompilerParams(dimension_semantics=("parallel", "parallel", "arbitrary")),
    )(a, b, *epi_ins)


def _row_tile(s):
    return _pick(s, (512, 256, 128))


def _prenorm(x, sc, sh):
    s = x.shape[0]
    tm = _row_tile(s)

    def body(x_ref, sc_ref, sh_ref, h_ref):
        h_ref[...] = _modnorm(x_ref[...], sc_ref[...], sh_ref[...]).astype(BF16)

    row = pl.BlockSpec((tm, D), lambda i: (i, 0))
    return pl.pallas_call(body, name="prenorm", grid=(s // tm,), in_specs=[row, _full((1, D)), _full((1, D))],
                          out_specs=row, out_shape=jax.ShapeDtypeStruct((s, D), BF16))(x, sc, sh)


def _prenorm_bwd(x, sc, sh, dh, dx_res):
    s = x.shape[0]
    tm = _row_tile(s)

    def body(x_ref, sc_ref, sh_ref, dh_ref, dr_ref, dx_ref, dsc_ref, dsh_ref):
        _, vjp = jax.vjp(_modnorm, x_ref[...], sc_ref[...], sh_ref[...])
        dx, dsc, dsh = vjp(dh_ref[...])
        dx_ref[...] = dr_ref[...] + dx

        @pl.when(pl.program_id(0) == 0)
        def _():
            dsc_ref[...] = jnp.zeros_like(dsc_ref)
            dsh_ref[...] = jnp.zeros_like(dsh_ref)

        dsc_ref[...] += dsc
        dsh_ref[...] += dsh

    row = pl.BlockSpec((tm, D), lambda i: (i, 0))
    vec = _full((1, D))
    return pl.pallas_call(
        body, name="prenorm_bwd", grid=(s // tm,), in_specs=[row, vec, vec, row, row], out_specs=[row, vec, vec],
        out_shape=[jax.ShapeDtypeStruct((s, D), F32), jax.ShapeDtypeStruct((1, D), F32), jax.ShapeDtypeStruct((1, D), F32)],
    )(x, sc, sh, dh, dx_res)


def _resid_norm(x, mo, g1, sc, sh):
    s = x.shape[0]
    tm = _row_tile(s)

    def body(x_ref, mo_ref, g_ref, sc_ref, sh_ref, x1_ref, h_ref):
        x1 = x_ref[...] + g_ref[...] * mo_ref[...]
        x1_ref[...] = x1
        h_ref[...] = _modnorm(x1, sc_ref[...], sh_ref[...]).astype(BF16)

    row = pl.BlockSpec((tm, D), lambda i: (i, 0))
    vec = _full((1, D))
    return pl.pallas_call(
        body, name="resid_norm", grid=(s // tm,), in_specs=[row, row, vec, vec, vec], out_specs=[row, row],
        out_shape=[jax.ShapeDtypeStruct((s, D), F32), jax.ShapeDtypeStruct((s, D), BF16)],
    )(x, mo, g1, sc, sh)


def _resid_norm_bwd(x1, mo, g1, sc, sh, dh, dx2):
    s = x1.shape[0]
    tm = _row_tile(s)

    def body(x1_ref, mo_ref, g_ref, sc_ref, sh_ref, dh_ref, dx2_ref, dx1_ref, dmo_ref, dg_ref, dsc_ref, dsh_ref):
        _, vjp = jax.vjp(_modnorm, x1_ref[...], sc_ref[...], sh_ref[...])
        dx, dsc, dsh = vjp(dh_ref[...])
        dx1 = dx2_ref[...] + dx
        dx1_ref[...] = dx1
        dmo_ref[...] = (dx1 * g_ref[...]).astype(BF16)

        @pl.when(pl.program_id(0) == 0)
        def _():
            dg_ref[...] = jnp.zeros_like(dg_ref)
            dsc_ref[...] = jnp.zeros_like(dsc_ref)
            dsh_ref[...] = jnp.zeros_like(dsh_ref)

        dg_ref[...] += jnp.sum(dx1 * mo_ref[...], axis=0, keepdims=True)
        dsc_ref[...] += dsc
        dsh_ref[...] += dsh

    row = pl.BlockSpec((tm, D), lambda i: (i, 0))
    vec = _full((1, D))
    vshape = jax.ShapeDtypeStruct((1, D), F32)
    return pl.pallas_call(
        body, name="resid_norm_bwd", grid=(s // tm,), in_specs=[row, row, vec, vec, vec, row, row],
        out_specs=[row, row, vec, vec, vec],
        out_shape=[jax.ShapeDtypeStruct((s, D), F32), jax.ShapeDtypeStruct((s, D), BF16), vshape, vshape, vshape],
    )(x1, mo, g1, sc, sh, dh, dx2)


def _loss_head(x1, fo, g2, fw, t):
    s = x1.shape[0]
    tm = _row_tile(s)

    def body(x1_ref, fo_ref, g_ref, fw_ref, t_ref, loss_ref, dx_ref, dfo_ref, dg_ref, dfw_ref):
        loss, (dx1, dfo, dg, dfw) = jax.value_and_grad(_loss_tile, argnums=(0, 1, 2, 3))(
            x1_ref[...], fo_ref[...], g_ref[...], fw_ref[...], t_ref[...])
        dx_ref[...] = dx1
        dfo_ref[...] = dfo.astype(BF16)

        @pl.when(pl.program_id(0) == 0)
        def _():
            loss_ref[...] = jnp.zeros_like(loss_ref)
            dg_ref[...] = jnp.zeros_like(dg_ref)
            dfw_ref[...] = jnp.zeros_like(dfw_ref)

        loss_ref[...] += jnp.full(loss_ref.shape, loss, F32)
        dg_ref[...] += dg
        dfw_ref[...] += dfw

    row = pl.BlockSpec((tm, D), lambda i: (i, 0))
    vec = _full((1, D))
    vshape = jax.ShapeDtypeStruct((1, D), F32)
    return pl.pallas_call(
        body, name="loss_head", grid=(s // tm,), in_specs=[row, row, vec, vec, row],
        out_specs=[_full((8, 128)), row, row, vec, vec],
        out_shape=[jax.ShapeDtypeStruct((8, 128), F32), jax.ShapeDtypeStruct((s, D), F32),
                   jax.ShapeDtypeStruct((s, D), BF16), vshape, vshape],
    )(x1, fo, g2, fw, t)


def _sgu_fwd(proj, norm_w, ws_b, bs_t):
    s = proj.shape[0]
    tm = _pick(s, (256, 128))

    def body(u_ref, v_ref, w_ref, ws_ref, bs_ref, y_ref):
        ug, vn = _sgu_pre(u_ref[...], v_ref[...], w_ref[...])
        vnb = vn.astype(BF16)
        for c in range(tm // Q):
            r = slice(c * Q, (c + 1) * Q)
            for g in range(GM_G):
                cs = slice(g * 128, (g + 1) * 128)
                sv = jnp.dot(ws_ref[g], vnb[r, cs], preferred_element_type=F32) + bs_ref[:, g:g + 1]
                y_ref[r, cs] = (ug[r, cs] * sv).astype(BF16)

    return pl.pallas_call(
        body, name="sgu_fwd", grid=(s // tm,),
        in_specs=[pl.BlockSpec((tm, D), lambda i: (i, 0)), pl.BlockSpec((tm, D), lambda i: (i, 1)),
                  _full((1, D)), _full((GM_G, Q, Q)), _full((Q, GM_G))],
        out_specs=pl.BlockSpec((tm, D), lambda i: (i, 0)),
        out_shape=jax.ShapeDtypeStruct((s, D), BF16),
    )(proj, proj, norm_w, ws_b, bs_t)


def _sgu_bwd(proj, norm_w, ws_b, bs_t, dy, dproj):
    s = proj.shape[0]
    tm = _pick(s, (256, 128))

    def body(u_ref, v_ref, w_ref, ws_ref, bs_ref, dy_ref, _, duv_ref, dw_ref, dws_ref, dbs_ref, dug_scr, dvn_scr):
        @pl.when(pl.program_id(0) == 0)
        def _():
            dw_ref[...] = jnp.zeros_like(dw_ref)
            dws_ref[...] = jnp.zeros_like(dws_ref)
            dbs_ref[...] = jnp.zeros_like(dbs_ref)

        (ug, vn), vjp = jax.vjp(_sgu_pre, u_ref[...], v_ref[...], w_ref[...])
        vnb = vn.astype(BF16)
        dy = dy_ref[...]
        causal = _tril(Q).astype(F32)
        for c in range(tm // Q):
            r = slice(c * Q, (c + 1) * Q)
            for g in range(GM_G):
                cs = slice(g * 128, (g + 1) * 128)
                blk = vnb[r, cs]
                sv = jnp.dot(ws_ref[g], blk, preferred_element_type=F32) + bs_ref[:, g:g + 1]
                dug_scr[r, cs] = dy[r, cs] * sv
                dsv = dy[r, cs] * ug[r, cs]
                dsvb = dsv.astype(BF16)
                dws_ref[g] += causal * lax.dot_general(dsvb, blk, (((1,), (1,)), ((), ())), preferred_element_type=F32)
                dbs_ref[:, g:g + 1] += jnp.sum(dsv, axis=1, keepdims=True)
                dvn_scr[r, cs] = lax.dot_general(ws_ref[g], dsvb, (((0,), (0,)), ((), ())), preferred_element_type=F32)
        du, dv, dw = vjp((dug_scr[...], dvn_scr[...]))
        duv_ref[:, :D] = du.astype(BF16)
        duv_ref[:, D:] = dv.astype(BF16)
        dw_ref[...] += dw

    return pl.pallas_call(
        body, name="sgu_bwd", grid=(s // tm,),
        in_specs=[pl.BlockSpec((tm, D), lambda i: (i, 0)), pl.BlockSpec((tm, D), lambda i: (i, 1)),
                  _full((1, D)), _full((GM_G, Q, Q)), _full((Q, GM_G)), pl.BlockSpec((tm, D), lambda i: (i, 0)), ANY],
        out_specs=[pl.BlockSpec((tm, 2 * D), lambda i: (i, 0)), _full((1, D)), _full((GM_G, Q, Q)), _full((Q, GM_G))],
        out_shape=[jax.ShapeDtypeStruct(dproj.shape, BF16), jax.ShapeDtypeStruct((1, D), F32),
                   jax.ShapeDtypeStruct((GM_G, Q, Q), F32), jax.ShapeDtypeStruct((Q, GM_G), F32)],
        scratch_shapes=[pltpu.VMEM((tm, D), F32), pltpu.VMEM((tm, D), F32)],
        input_output_aliases={6: 0},
    )(proj, proj, norm_w, ws_b, bs_t, dy, dproj)


def _gatenorm_fwd(y_ssd, proj, norm_w):
    s = y_ssd.shape[0]
    tm = _row_tile(s)
    zb = OFF_Z // 256

    def body(y_ref, z_ref, w_ref, o_ref):
        o_ref[...] = _gatenorm(y_ref[...], z_ref[...], w_ref[...]).astype(BF16)

    return pl.pallas_call(
        body, name="gatenorm_fwd", grid=(s // tm, NGRP),
        in_specs=[pl.BlockSpec((tm, 256), lambda i, g: (i, g)), pl.BlockSpec((tm, 256), lambda i, g: (i, zb + g)),
                  pl.BlockSpec((1, 256), lambda i, g: (0, g))],
        out_specs=pl.BlockSpec((tm, 256), lambda i, g: (i, g)),
        out_shape=jax.ShapeDtypeStruct((s, DI), BF16),
    )(y_ssd, proj, norm_w)


def _gatenorm_bwd(y_ssd, proj, norm_w, dyb, dproj):
    s = y_ssd.shape[0]
    tm = _row_tile(s)
    zb = OFF_Z // 256

    def body(y_ref, z_ref, w_ref, d_ref, _, dy_ref, dz_ref, dw_ref):
        _, vjp = jax.vjp(_gatenorm, y_ref[...], z_ref[...], w_ref[...])
        dy, dz, dw = vjp(d_ref[...])
        dy_ref[...] = dy
        dz_ref[...] = dz.astype(BF16)

        @pl.when(pl.program_id(1) == 0)
        def _():
            dw_ref[...] = jnp.zeros_like(dw_ref)

        dw_ref[...] += dw

    blk = pl.BlockSpec((tm, 256), lambda g, i: (i, g))
    return pl.pallas_call(
        body, name="gatenorm_bwd", grid=(NGRP, s // tm),
        in_specs=[blk, pl.BlockSpec((tm, 256), lambda g, i: (i, zb + g)), pl.BlockSpec((1, 256), lambda g, i: (0, g)), blk, ANY],
        out_specs=[blk, pl.BlockSpec((tm, 256), lambda g, i: (i, zb + g)), pl.BlockSpec((1, 256), lambda g, i: (0, g))],
        out_shape=[jax.ShapeDtypeStruct((s, DI), F32), jax.ShapeDtypeStruct(dproj.shape, BF16), jax.ShapeDtypeStruct((1, DI), F32)],
        input_output_aliases={4: 1},
    )(y_ssd, proj, norm_w, dyb, dproj)


def _mix_fwd(proj, ba, bb):
    s = proj.shape[0]
    tm = _row_tile(s)
    gb0 = OFF_GA // D

    def body(ga_ref, gb_ref, ba_ref, bb_ref, o_ref):
        o_ref[...] = _mix(ga_ref[...], gb_ref[...], ba_ref[...], bb_ref[...]).astype(BF16)

    row = pl.BlockSpec((tm, D), lambda i: (i, 0))
    return pl.pallas_call(
        body, name="mix_fwd", grid=(s // tm,),
        in_specs=[pl.BlockSpec((tm, D), lambda i: (i, gb0)), pl.BlockSpec((tm, D), lambda i: (i, gb0 + 1)), row, row],
        out_specs=row, out_shape=jax.ShapeDtypeStruct((s, D), BF16),
    )(proj, proj, ba, bb)


def _mix_bwd(proj, ba, bb, dmixed):
    s = proj.shape[0]
    tm = _row_tile(s)
    gb0 = OFF_GA // D

    def body(ga_ref, gb_ref, ba_ref, bb_ref, d_ref, dg_ref, dba_ref, dbb_ref):
        _, vjp = jax.vjp(_mix, ga_ref[...], gb_ref[...], ba_ref[...], bb_ref[...])
        dga, dgb, dba, dbb = vjp(d_ref[...])
        dg_ref[:, :D] = dga.astype(BF16)
        dg_ref[:, D:] = dgb.astype(BF16)
        dba_ref[...] = dba.astype(BF16)
        dbb_ref[...] = dbb.astype(BF16)

    row = pl.BlockSpec((tm, D), lambda i: (i, 0))
    return pl.pallas_call(
        body, name="mix_bwd", grid=(s // tm,),
        in_specs=[pl.BlockSpec((tm, D), lambda i: (i, gb0)), pl.BlockSpec((tm, D), lambda i: (i, gb0 + 1)), row, row, row],
        out_specs=[pl.BlockSpec((tm, 2 * D), lambda i: (i, OFF_GA // (2 * D))), row, row],
        out_shape=[jax.ShapeDtypeStruct((s, W_IN_R), BF16), jax.ShapeDtypeStruct((s, D), BF16), jax.ShapeDtypeStruct((s, D), BF16)],
    )(proj, proj, ba, bb, dmixed)


CONV_TC = 1024


def _conv_fwd(proj, conv_w, conv_b):
    s = proj.shape[0]
    tm = _row_tile(s)
    cb0 = OFF_XBC // CONV_TC

    def body(x_ref, halo_ref, w_ref, b_ref, o_ref):
        halo = jnp.where(pl.program_id(0) > 0, halo_ref[...], 0.0)
        ext = jnp.concatenate([halo, x_ref[...]], axis=0)
        acc = jnp.broadcast_to(b_ref[...], (tm, CONV_TC))
        for k in range(4):
            shifted = ext if k == 3 else pltpu.roll(ext, 3 - k, 0)
            acc = acc + w_ref[k:k + 1, :] * shifted[8:, :]
        o_ref[...] = jax.nn.silu(acc)

    return pl.pallas_call(
        body, name="conv_fwd", grid=(s // tm, CONV // CONV_TC),
        in_specs=[pl.BlockSpec((tm, CONV_TC), lambda i, j: (i, cb0 + j)),
                  pl.BlockSpec((8, CONV_TC), lambda i, j: (jnp.maximum(i * (tm // 8) - 1, 0), cb0 + j)),
                  pl.BlockSpec((4, CONV_TC), lambda i, j: (0, j)), pl.BlockSpec((1, CONV_TC), lambda i, j: (0, j))],
        out_specs=pl.BlockSpec((tm, CONV_TC), lambda i, j: (i, j)),
        out_shape=jax.ShapeDtypeStruct((s, CONV), F32),
    )(proj, proj, conv_w, conv_b)


def _conv_bwd(proj, conv_w, conv_b, dact, dproj):
    s = proj.shape[0]
    tm = _row_tile(s)
    nt = s // tm
    cb0 = OFF_XBC // CONV_TC

    def body(x_ref, prev_ref, next_ref, d_ref, dnext_ref, w_ref, b_ref, _, dx_ref, dw_ref, db_ref):
        i = pl.program_id(1)
        prev = jnp.where(i > 0, prev_ref[...], 0.0)
        ext = jnp.concatenate([prev, x_ref[...], next_ref[...]], axis=0)
        dext = jnp.concatenate([d_ref[...], jnp.where(i < nt - 1, dnext_ref[...], 0.0)], axis=0)
        pre = jnp.broadcast_to(b_ref[...], (tm + 8, CONV_TC))
        taps = []
        for k in range(4):
            shifted = (ext if k == 3 else pltpu.roll(ext, 3 - k, 0))[8:, :]
            taps.append(shifted)
            pre = pre + w_ref[k:k + 1, :] * shifted
        sig = jax.nn.sigmoid(pre)
        dpre = dext * (sig * (1.0 + pre * (1.0 - sig)))
        dx = jnp.zeros((tm, CONV_TC), F32)
        for k in range(4):
            shifted = dpre if k == 3 else pltpu.roll(dpre, tm + 8 - (3 - k), 0)
            dx = dx + w_ref[k:k + 1, :] * shifted[:tm, :]
        dx_ref[...] = dx.astype(BF16)

        @pl.when(i == 0)
        def _():
            dw_ref[...] = jnp.zeros_like(dw_ref)
            db_ref[...] = jnp.zeros_like(db_ref)

        dtile = dpre[:tm, :]
        for k in range(4):
            dw_ref[k:k + 1, :] += jnp.sum(dtile * taps[k][:tm, :], axis=0, keepdims=True)
        db_ref[...] += jnp.sum(dtile, axis=0, keepdims=True)

    r8 = tm // 8
    return pl.pallas_call(
        body, name="conv_bwd", grid=(CONV // CONV_TC, nt),
        in_specs=[pl.BlockSpec((tm, CONV_TC), lambda j, i: (i, cb0 + j)),
                  pl.BlockSpec((8, CONV_TC), lambda j, i: (jnp.maximum(i * r8 - 1, 0), cb0 + j)),
                  pl.BlockSpec((8, CONV_TC), lambda j, i: (jnp.minimum((i + 1) * r8, nt * r8 - 1), cb0 + j)),
                  pl.BlockSpec((tm, CONV_TC), lambda j, i: (i, j)),
                  pl.BlockSpec((8, CONV_TC), lambda j, i: (jnp.minimum((i + 1) * r8, nt * r8 - 1), j)),
                  pl.BlockSpec((4, CONV_TC), lambda j, i: (0, j)), pl.BlockSpec((1, CONV_TC), lambda j, i: (0, j)), ANY],
        out_specs=[pl.BlockSpec((tm, CONV_TC), lambda j, i: (i, cb0 + j)),
                   pl.BlockSpec((4, CONV_TC), lambda j, i: (0, j)), pl.BlockSpec((1, CONV_TC), lambda j, i: (0, j))],
        out_shape=[jax.ShapeDtypeStruct(dproj.shape, BF16), jax.ShapeDtypeStruct((4, CONV), F32), jax.ShapeDtypeStruct((1, CONV), F32)],
        input_output_aliases={7: 0},
    )(proj, proj, proj, dact, dact, conv_w, conv_b, dproj)


def _dt_fwd(proj, dtb, alog):
    s = proj.shape[0]
    nc = s // Q

    def body(r_ref, b_ref, a_ref, dt_ref, cum_ref, cumt_ref):
        dt, cum, cum_t = _dt_prep(r_ref[...], b_ref[...], a_ref[...])
        dt_ref[...] = dt
        cum_ref[...] = cum
        cumt_ref[...] = cum_t

    blk = pl.BlockSpec((Q, 128), lambda n: (n, 0))
    return pl.pallas_call(
        body, name="dt_fwd", grid=(nc,),
        in_specs=[pl.BlockSpec((Q, 128), lambda n: (n, OFF_DT // 128)), _full((1, 128)), _full((1, 128))],
        out_specs=[blk, blk, pl.BlockSpec((None, 128, Q), lambda n: (n, 0, 0))],
        out_shape=[jax.ShapeDtypeStruct((s, 128), F32), jax.ShapeDtypeStruct((s, 128), F32), jax.ShapeDtypeStruct((nc, 128, Q), F32)],
    )(proj, dtb, alog)


def _dt_bwd(proj, dtb, alog, ddt, dcum, dcumt, dproj):
    s = proj.shape[0]
    nc = s // Q

    def body(r_ref, b_ref, a_ref, ddt_ref, dcum_ref, dcumt_ref, _, dr_ref, db_ref, da_ref):
        _, vjp = jax.vjp(_dt_prep, r_ref[...], b_ref[...], a_ref[...])
        dr, db, da = vjp((ddt_ref[...], dcum_ref[...], dcumt_ref[...]))
        dr_ref[...] = dr.astype(BF16)

        @pl.when(pl.program_id(0) == 0)
        def _():
            db_ref[...] = jnp.zeros_like(db_ref)
            da_ref[...] = jnp.zeros_like(da_ref)

        db_ref[...] += db
        da_ref[...] += da

    blk = pl.BlockSpec((Q, 128), lambda n: (n, 0))
    pblk = pl.BlockSpec((Q, 128), lambda n: (n, OFF_DT // 128))
    return pl.pallas_call(
        body, name="dt_bwd", grid=(nc,),
        in_specs=[pblk, _full((1, 128)), _full((1, 128)), blk, blk, pl.BlockSpec((None, 128, Q), lambda n: (n, 0, 0)), ANY],
        out_specs=[pblk, _full((1, 128)), _full((1, 128))],
        out_shape=[jax.ShapeDtypeStruct(dproj.shape, BF16), jax.ShapeDtypeStruct((1, 128), F32), jax.ShapeDtypeStruct((1, 128), F32)],
        input_output_aliases={6: 0},
    )(proj, dtb, alog, ddt, dcum, dcumt, dproj)


def _ssd_specs(chunk_of):
    xs = pl.BlockSpec((Q, 256), lambda n, g: (chunk_of(n), g))
    bm = pl.BlockSpec((Q, 128), lambda n, g: (chunk_of(n), DI // 128 + g))
    cm = pl.BlockSpec((Q, 128), lambda n, g: (chunk_of(n), DI // 128 + NGRP + g))
    per_chunk = pl.BlockSpec((Q, 128), lambda n, g: (chunk_of(n), 0))
    cum_t = pl.BlockSpec((None, 128, Q), lambda n, g: (chunk_of(n), 0, 0))
    state = pl.BlockSpec((None, 256, 128), lambda n, g: (chunk_of(n), g, 0))
    vec = pl.BlockSpec((1, 128), lambda n, g: (0, 0))
    return xs, bm, cm, per_chunk, cum_t, state, vec


def _ssd_fwd(xbc, dt, cum, cum_t, dsk):
    s = xbc.shape[0]
    nc = s // Q
    xs, bm, cm, per_chunk, cumt_spec, state_spec, vec = _ssd_specs(lambda n: n)

    def body(x_ref, b_ref, c_ref, dt_ref, cum_ref, cumt_ref, dsk_ref, y_ref, st_ref, carry):
        n, g = pl.program_id(0), pl.program_id(1)
        rows = pl.ds(pl.multiple_of(g * 256, 256), 256)

        @pl.when(n == 0)
        def _():
            carry[rows, :] = jnp.zeros((256, 128), F32)

        s_in = carry[rows, :]
        st_ref[...] = s_in
        y0, y1, n0, n1 = _ssd_group(x_ref[:, :128], x_ref[:, 128:], s_in[:128], s_in[128:], b_ref[...], c_ref[...],
                                    dt_ref[...], cum_ref[...], cumt_ref[...], dsk_ref[...], g)
        y_ref[:, :128] = y0
        y_ref[:, 128:] = y1
        carry[pl.ds(pl.multiple_of(g * 256, 256), 128), :] = n0
        carry[pl.ds(pl.multiple_of(g * 256 + 128, 128), 128), :] = n1

    return pl.pallas_call(
        body, name="ssd_fwd", grid=(nc, NGRP),
        in_specs=[xs, bm, cm, per_chunk, per_chunk, cumt_spec, vec],
        out_specs=[xs, state_spec],
        out_shape=[jax.ShapeDtypeStruct((s, DI), F32), jax.ShapeDtypeStruct((nc, DI, 128), F32)],
        scratch_shapes=[pltpu.VMEM((DI, 128), F32)],
    )(xbc, xbc, xbc, dt, cum, cum_t, dsk)


def _ssd_bwd(xbc, dt, cum, cum_t, dsk, states, dy):
    s = xbc.shape[0]
    nc = s // Q
    xs, bm, cm, per_chunk, cumt_spec, state_spec, vec = _ssd_specs(lambda n: nc - 1 - n)

    def body(x_ref, b_ref, c_ref, dt_ref, cum_ref, cumt_ref, dsk_ref, st_ref, dy_ref,
             dx_ref, db_ref, dc_ref, ddt_ref, dcum_ref, dcumt_ref, ddsk_ref, carry):
        n, g = pl.program_id(0), pl.program_id(1)
        rows = pl.ds(pl.multiple_of(g * 256, 256), 256)

        @pl.when(n == 0)
        def _():
            carry[rows, :] = jnp.zeros((256, 128), F32)

        @pl.when((n == 0) & (g == 0))
        def _():
            ddsk_ref[...] = jnp.zeros_like(ddsk_ref)

        @pl.when(g == 0)
        def _():
            ddt_ref[...] = jnp.zeros_like(ddt_ref)
            dcum_ref[...] = jnp.zeros_like(dcum_ref)
            dcumt_ref[...] = jnp.zeros_like(dcumt_ref)

        s_in = st_ref[...]
        d_out = carry[rows, :]
        fn = functools.partial(_ssd_group, grp=g)
        _, vjp = jax.vjp(fn, x_ref[:, :128], x_ref[:, 128:], s_in[:128], s_in[128:], b_ref[...], c_ref[...],
                         dt_ref[...], cum_ref[...], cumt_ref[...], dsk_ref[...])
        dx0, dx1, ds0, ds1, dbm, dcm, ddt, dcum, dcumt, ddsk = vjp((dy_ref[:, :128], dy_ref[:, 128:], d_out[:128], d_out[128:]))
        dx_ref[:, :128] = dx0
        dx_ref[:, 128:] = dx1
        db_ref[...] = dbm
        dc_ref[...] = dcm
        ddt_ref[...] += ddt
        dcum_ref[...] += dcum
        dcumt_ref[...] += dcumt
        ddsk_ref[...] += ddsk
        carry[pl.ds(pl.multiple_of(g * 256, 256), 128), :] = ds0
        carry[pl.ds(pl.multiple_of(g * 256 + 128, 128), 128), :] = ds1

    dxbc_specs = [xs, bm, cm]
    out = pl.pallas_call(
        body, name="ssd_bwd", grid=(nc, NGRP),
        in_specs=[xs, bm, cm, per_chunk, per_chunk, cumt_spec, vec, state_spec, xs],
        out_specs=[xs, pl.BlockSpec((Q, 128), lambda n, g: (nc - 1 - n, g)), pl.BlockSpec((Q, 128), lambda n, g: (nc - 1 - n, g)),
                   per_chunk, per_chunk, cumt_spec, vec],
        out_shape=[jax.ShapeDtypeStruct((s, DI), F32), jax.ShapeDtypeStruct((s, D), F32), jax.ShapeDtypeStruct((s, D), F32),
                   jax.ShapeDtypeStruct((s, 128), F32), jax.ShapeDtypeStruct((s, 128), F32),
                   jax.ShapeDtypeStruct((nc, 128, Q), F32), jax.ShapeDtypeStruct((1, 128), F32)],
        scratch_shapes=[pltpu.VMEM((DI, 128), F32)],
    )(xbc, xbc, xbc, dt, cum, cum_t, dsk, states, dy)
    del dxbc_specs
    return out


def _adam_math(w, g, m, v):
    m2 = ADAM_B1 * m + (1.0 - ADAM_B1) * g
    v2 = ADAM_B2 * v + (1.0 - ADAM_B2) * jnp.square(g)
    m_hat = m2 / (1.0 - ADAM_B1 ** ADAM_STEP)
    v_hat = v2 / (1.0 - ADAM_B2 ** ADAM_STEP)
    delta = -ADAM_LR * (m_hat / (jnp.sqrt(v_hat) + ADAM_EPS) + ADAM_WD * w)
    return delta, m2, v2


def _adam(w, g, m, v, name):
    r, c = w.shape
    tr = r if r * c * 4 <= (1 << 20) else _pick(r, (128, 64, 32, 16, 8))

    def body(w_ref, g_ref, m_ref, v_ref, d_ref, m2_ref, v2_ref):
        d, m2, v2 = _adam_math(w_ref[...], g_ref[...], m_ref[...], v_ref[...])
        d_ref[...] = d
        m2_ref[...] = m2
        v2_ref[...] = v2

    blk = pl.BlockSpec((tr, c), lambda i: (i, 0))
    shp = jax.ShapeDtypeStruct((r, c), F32)
    return pl.pallas_call(body, name=name, grid=(r // tr,), in_specs=[blk] * 4, out_specs=[blk] * 3,
                          out_shape=[shp] * 3)(w, g, m, v)


def _sum_leading(xs, name, out_dtype=F32, tr=256):
    if isinstance(xs, tuple):
        a, b = xs
        n, r, c = a.shape
        tr = _pick(r, (tr, 128, 64, 32, 16, 8))

        def body2(a_ref, b_ref, o_ref):
            o_ref[...] = (a_ref[...].astype(F32) + b_ref[...].astype(F32)).astype(out_dtype)

        blk = pl.BlockSpec((None, tr, c), lambda s, i: (s, i, 0))
        return pl.pallas_call(body2, name=name, grid=(n, r // tr), in_specs=[blk, blk], out_specs=blk,
                              out_shape=jax.ShapeDtypeStruct((n, r, c), out_dtype))(a, b)
    n, r, c = xs.shape
    tr = r if n * r * c * 4 <= (8 << 20) else _pick(r, (tr, 128, 64, 32, 16, 8))

    def body(x_ref, o_ref):
        acc = x_ref[0].astype(F32)
        for s in range(1, n):
            acc = acc + x_ref[s].astype(F32)
        o_ref[...] = acc.astype(out_dtype)

    return pl.pallas_call(body, name=name, grid=(r // tr,), in_specs=[pl.BlockSpec((n, tr, c), lambda i: (0, i, 0))],
                          out_specs=pl.BlockSpec((tr, c), lambda i: (i, 0)),
                          out_shape=jax.ShapeDtypeStruct((r, c), out_dtype))(xs)


def _mod_fwd(c8, w_mod, b_sl):
    def body(c_ref, w_ref, b_ref, o_ref):
        o_ref[...] = jnp.dot(jax.nn.silu(c_ref[...]), w_ref[...], precision=HIGHEST, preferred_element_type=F32) + b_ref[...]

    return pl.pallas_call(body, name="mod_fwd", in_specs=[VMEM, VMEM, VMEM], out_specs=VMEM,
                          out_shape=jax.ShapeDtypeStruct((N_DEV, w_mod.shape[1]), F32))(c8, w_mod, b_sl)


def _mod_wgrad(c8, dmod):
    def body(c_ref, d_ref, o_ref):
        o_ref[...] = lax.dot_general(jax.nn.silu(c_ref[...]), d_ref[...], (((0,), (0,)), ((), ())),
                                     precision=HIGHEST, preferred_element_type=F32)

    return pl.pallas_call(body, name="mod_wgrad", in_specs=[VMEM, VMEM], out_specs=VMEM,
                          out_shape=jax.ShapeDtypeStruct((D, dmod.shape[1]), F32))(c8, dmod)


def _place():
    x, y, c = lax.axis_index("x"), lax.axis_index("y"), lax.axis_index("c")
    chips = [(1 - x, y), (x, 1 - y), (1 - x, 1 - y)]
    return x, y, c, chips


def _all_gather_small(v, name):
    r, w = v.shape

    def body(v_ref, o_ref, send_sems, recv_sems, local_sem):
        x, y, c, _ = _place()
        me = 4 * x + 2 * y + c
        own = pltpu.make_async_copy(v_ref, o_ref.at[me], local_sem)
        own.start()
        sends = []
        for k in range(1, N_DEV):
            tx = 1 - x if k & 4 else x
            ty = 1 - y if k & 2 else y
            tc = 1 - c if k & 1 else c
            peer = 4 * tx + 2 * ty + tc
            cp = pltpu.make_async_remote_copy(src_ref=v_ref, dst_ref=o_ref.at[me], send_sem=send_sems.at[k - 1],
                                              recv_sem=recv_sems.at[k - 1], device_id=(tx, ty, tc), device_id_type=MESH)
            cp.start()
            sends.append((cp, peer, (tx, ty, tc)))
        for k, (cp, peer, dev) in enumerate(sends):
            pltpu.make_async_remote_copy(src_ref=v_ref, dst_ref=o_ref.at[peer], send_sem=send_sems.at[k],
                                         recv_sem=recv_sems.at[k], device_id=dev, device_id_type=MESH).wait_recv()
        for cp, _, _ in sends:
            cp.wait_send()
        own.wait()

    return pl.pallas_call(
        body, name=name, in_specs=[VMEM], out_specs=VMEM, out_shape=jax.ShapeDtypeStruct((N_DEV, r, w), v.dtype),
        scratch_shapes=[pltpu.SemaphoreType.DMA((N_DEV - 1,)), pltpu.SemaphoreType.DMA((N_DEV - 1,)), pltpu.SemaphoreType.DMA],
    )(v)


def _all_gather_weights(shards):
    n = len(shards)

    def body(*refs):
        srcs, dsts = refs[:n], refs[n:2 * n]
        send_sems, recv_sems, local_sems = refs[2 * n:]
        x, y, c, chips = _place()
        q = 2 * x + y
        sends, owns = [], []
        for t in range(n):
            h = srcs[t].shape[0] // 2
            own = pltpu.make_async_copy(srcs[t], dsts[t].at[q], local_sems.at[t])
            own.start()
            owns.append(own)
            mine = pl.ds(c * h, h)
            for j, (cx, cy) in enumerate(chips):
                cp = pltpu.make_async_remote_copy(src_ref=srcs[t].at[mine], dst_ref=dsts[t].at[q, mine],
                                                  send_sem=send_sems.at[6 * t + j], recv_sem=recv_sems.at[6 * t + j],
                                                  device_id=(cx, cy, c), device_id_type=MESH)
                cp.start()
                sends.append(cp)
        for t in range(n):
            h = srcs[t].shape[0] // 2
            mine = pl.ds(c * h, h)
            for j, (cx, cy) in enumerate(chips):
                part = dsts[t].at[2 * cx + cy, mine]
                pltpu.make_async_remote_copy(src_ref=part, dst_ref=part, send_sem=send_sems.at[6 * t + j],
                                             recv_sem=recv_sems.at[6 * t + j], device_id=(cx, cy, c), device_id_type=MESH).wait_recv()
                fwd = pltpu.make_async_remote_copy(src_ref=part, dst_ref=part, send_sem=send_sems.at[6 * t + 3 + j],
                                                   recv_sem=recv_sems.at[6 * t + 3 + j], device_id=(x, y, 1 - c), device_id_type=MESH)
                fwd.start()
                sends.append(fwd)
        for t in range(n):
            h = srcs[t].shape[0] // 2
            theirs = pl.ds((1 - c) * h, h)
            for j, (cx, cy) in enumerate(chips):
                part = dsts[t].at[2 * cx + cy, theirs]
                pltpu.make_async_remote_copy(src_ref=part, dst_ref=part, send_sem=send_sems.at[6 * t + 3 + j],
                                             recv_sem=recv_sems.at[6 * t + 3 + j], device_id=(x, y, 1 - c), device_id_type=MESH).wait_recv()
        for cp in sends:
            cp.wait_send()
        for own in owns:
            own.wait()

    return pl.pallas_call(
        body, name="all_gather_weights", in_specs=[ANY] * n, out_specs=[ANY] * n,
        out_shape=[jax.ShapeDtypeStruct((N_CHIP,) + s.shape, s.dtype) for s in shards],
        scratch_shapes=[pltpu.SemaphoreType.DMA((6 * n,)), pltpu.SemaphoreType.DMA((6 * n,)), pltpu.SemaphoreType.DMA((n,))],
    )(*shards)


def _exchange_halves(grads):
    n = len(grads)

    def body(*refs):
        srcs, mines, theirs = refs[:n], refs[n:2 * n], refs[2 * n:3 * n]
        send_sems, recv_sems, local_sems = refs[3 * n:]
        x, y, c, _ = _place()
        copies = []
        for t in range(n):
            h = srcs[t].shape[1] // 2
            own = pltpu.make_async_copy(srcs[t].at[:, pl.ds(c * h, h)], mines[t], local_sems.at[t])
            own.start()
            cp = pltpu.make_async_remote_copy(src_ref=srcs[t].at[:, pl.ds((1 - c) * h, h)], dst_ref=theirs[t],
                                              send_sem=send_sems.at[t], recv_sem=recv_sems.at[t],
                                              device_id=(x, y, 1 - c), device_id_type=MESH)
            cp.start()
            copies.append((own, cp))
        for own, cp in copies:
            cp.wait()
            own.wait()

    half = [jax.ShapeDtypeStruct((N_CHIP, g.shape[1] // 2, g.shape[2]), g.dtype) for g in grads]
    out = pl.pallas_call(
        body, name="exchange_halves", in_specs=[ANY] * n, out_specs=[ANY] * (2 * n), out_shape=half + half,
        scratch_shapes=[pltpu.SemaphoreType.DMA((n,)), pltpu.SemaphoreType.DMA((n,)), pltpu.SemaphoreType.DMA((n,))],
    )(*grads)
    return out[:n], out[n:]


def _exchange_chips(parts):
    n = len(parts)

    def body(*refs):
        srcs, dsts = refs[:n], refs[n:2 * n]
        send_sems, recv_sems, local_sems = refs[2 * n:]
        x, y, c, chips = _place()
        q = 2 * x + y
        sends, owns = [], []
        for t in range(n):
            own = pltpu.make_async_copy(srcs[t].at[q], dsts[t].at[q], local_sems.at[t])
            own.start()
            owns.append(own)
            for j, (cx, cy) in enumerate(chips):
                cp = pltpu.make_async_remote_copy(src_ref=srcs[t].at[2 * cx + cy], dst_ref=dsts[t].at[q],
                                                  send_sem=send_sems.at[3 * t + j], recv_sem=recv_sems.at[3 * t + j],
                                                  device_id=(cx, cy, c), device_id_type=MESH)
                cp.start()
                sends.append(cp)
        for t in range(n):
            for j, (cx, cy) in enumerate(chips):
                part = dsts[t].at[2 * cx + cy]
                pltpu.make_async_remote_copy(src_ref=part, dst_ref=part, send_sem=send_sems.at[3 * t + j],
                                             recv_sem=recv_sems.at[3 * t + j], device_id=(cx, cy, c), device_id_type=MESH).wait_recv()
        for cp in sends:
            cp.wait_send()
        for own in owns:
            own.wait()

    return pl.pallas_call(
        body, name="exchange_chips", in_specs=[ANY] * n, out_specs=[ANY] * n,
        out_shape=[jax.ShapeDtypeStruct(p.shape, p.dtype) for p in parts],
        scratch_shapes=[pltpu.SemaphoreType.DMA((3 * n,)), pltpu.SemaphoreType.DMA((3 * n,)), pltpu.SemaphoreType.DMA((n,))],
    )(*parts)


def _share_halves(halves):
    n = len(halves)

    def body(*refs):
        srcs, dsts = refs[:n], refs[n:2 * n]
        send_sems, recv_sems, local_sems = refs[2 * n:]
        x, y, c, _ = _place()
        copies = []
        for t in range(n):
            h = srcs[t].shape[0]
            mine = pl.ds(c * h, h)
            own = pltpu.make_async_copy(srcs[t], dsts[t].at[mine], local_sems.at[t])
            own.start()
            cp = pltpu.make_async_remote_copy(src_ref=srcs[t], dst_ref=dsts[t].at[mine], send_sem=send_sems.at[t],
                                              recv_sem=recv_sems.at[t], device_id=(x, y, 1 - c), device_id_type=MESH)
            cp.start()
            copies.append((own, cp, h))
        for t, (own, cp, h) in enumerate(copies):
            theirs = dsts[t].at[pl.ds((1 - c) * h, h)]
            cp.wait_send()
            pltpu.make_async_remote_copy(src_ref=srcs[t], dst_ref=theirs, send_sem=send_sems.at[t], recv_sem=recv_sems.at[t],
                                         device_id=(x, y, 1 - c), device_id_type=MESH).wait_recv()
            own.wait()

    return pl.pallas_call(
        body, name="share_halves", in_specs=[ANY] * n, out_specs=[ANY] * n,
        out_shape=[jax.ShapeDtypeStruct((2 * h.shape[0], h.shape[1]), h.dtype) for h in halves],
        scratch_shapes=[pltpu.SemaphoreType.DMA((n,)), pltpu.SemaphoreType.DMA((n,)), pltpu.SemaphoreType.DMA((n,))],
    )(*halves)


def _pack(arrs):
    rows = []
    for a in arrs:
        flat = a.astype(F32).reshape(-1)
        pad = (-flat.shape[0]) % 1024
        rows.append(jnp.pad(flat, (0, pad)).reshape(-1, 128))
    return jnp.concatenate(rows, axis=0)


def _unpack(buf, shapes):
    out, r = [], 0
    for shp in shapes:
        size = 1
        for d in shp:
            size *= d
        nr = (size + 1023) // 1024 * 8
        out.append(buf[r:r + nr].reshape(-1)[:size].reshape(shp))
        r += nr
    return out


def _local_step(x, t, mod, w_in_r, w_bg, w_bs, w_out, w_ff1, w_ff2, gm_norm_w, gm_ws, gm_bs, conv_w, conv_b,
                dt_bias, a_log, d_skip, ssm_norm_w, final_norm_w):
    sh1, sc1, g1, sh2, sc2, g2 = [mod[:, i * D:(i + 1) * D] for i in range(6)]
    ws_b = jnp.where(jnp.tril(jnp.ones((Q, Q), bool))[None], gm_ws, 0.0).astype(BF16)
    bs_t = gm_bs.T
    pad32 = lambda v: jnp.pad(v, ((0, 0), (0, 128 - NHEAD)))
    dtb, alog, dsk = pad32(dt_bias), pad32(a_log), pad32(d_skip)

    h1 = _prenorm(x, sc1, sh1)
    (proj,) = _matmul(h1, w_in_r, "nn", name="mm_proj", tn=1152)
    y_a = _sgu_fwd(proj, gm_norm_w, ws_b, bs_t)
    xbc = _conv_fwd(proj, conv_w, conv_b)
    dt, cum, cum_t = _dt_fwd(proj, dtb, alog)
    y_ssd, states = _ssd_fwd(xbc, dt, cum, cum_t, dsk)
    y_b = _gatenorm_fwd(y_ssd, proj, ssm_norm_w)
    (ba,) = _matmul(y_a, w_bg, "nn", name="mm_branch_gm")
    (bb,) = _matmul(y_b, w_bs, "nn", name="mm_branch_ssm")
    mixed = _mix_fwd(proj, ba, bb)
    (mo,) = _matmul(mixed, w_out, "nn", name="mm_out")
    x1, h2 = _resid_norm(x, mo, g1, sc2, sh2)
    f, act = _matmul(h2, w_ff1, "nn", name="mm_ff1", out_dtypes=(BF16, BF16),
                     epi=lambda acc: (acc, jnp.square(jnp.maximum(acc, 0.0))))
    (fo,) = _matmul(act, w_ff2, "nn", name="mm_ff2")
    loss8, dx2, dfo, dg2, dfw = _loss_head(x1, fo, g2, final_norm_w, t)

    (df,) = _matmul(dfo, w_ff2, "nt", name="mm_ff2_dx", out_dtypes=(BF16,), epi_ins=(f,),
                    epi=lambda acc, fv: (acc * (2.0 * jnp.maximum(fv.astype(F32), 0.0)),))
    (g_ff2,) = _matmul(act, dfo, "tn", name="mm_ff2_dw", tk=512)
    (dh2,) = _matmul(df, w_ff1, "nt", name="mm_ff1_dx")
    (g_ff1,) = _matmul(h2, df, "tn", name="mm_ff1_dw", tk=512)
    dx1, dmo, dg1, dsc2, dsh2 = _resid_norm_bwd(x1, mo, g1, sc2, sh2, dh2, dx2)
    (dmixed,) = _matmul(dmo, w_out, "nt", name="mm_out_dx")
    (g_out,) = _matmul(mixed, dmo, "tn", name="mm_out_dw", tk=512)
    dproj, dba, dbb = _mix_bwd(proj, ba, bb, dmixed)
    (dy_a,) = _matmul(dba, w_bg, "nt", name="mm_branch_gm_dx")
    (g_bg,) = _matmul(y_a, dba, "tn", name="mm_branch_gm_dw", tk=512)
    (dy_b,) = _matmul(dbb, w_bs, "nt", name="mm_branch_ssm_dx")
    (g_bs,) = _matmul(y_b, dbb, "tn", name="mm_branch_ssm_dw", tk=512)
    dproj, d_gm_norm, d_ws, d_bs_t = _sgu_bwd(proj, gm_norm_w, ws_b, bs_t, dy_a, dproj)
    dy_ssd, dproj, d_ssm_norm = _gatenorm_bwd(y_ssd, proj, ssm_norm_w, dy_b, dproj)
    dxs, dbm, dcm, ddt, dcum, dcum_t, d_dsk = _ssd_bwd(xbc, dt, cum, cum_t, dsk, states, dy_ssd)
    dproj, d_dtb, d_alog = _dt_bwd(proj, dtb, alog, ddt, dcum, dcum_t, dproj)
    dact = jnp.concatenate([dxs, dbm, dcm], axis=1)
    dproj, d_conv_w, d_conv_b = _conv_bwd(proj, conv_w, conv_b, dact, dproj)
    (dh1,) = _matmul(dproj, w_in_r, "nt", name="mm_proj_dx", tk=1152)
    (g_in_r,) = _matmul(h1, dproj, "tn", name="mm_proj_dw", tn=1152, tk=512)
    grad_x, dsc1, dsh1 = _prenorm_bwd(x, sc1, sh1, dh1, dx1)

    dmod = jnp.concatenate([dsh1, dsc1, dg1, dsh2, dsc2, dg2], axis=1)
    small = dict(gm_ws=d_ws, gm_norm_w=d_gm_norm, gm_bs=d_bs_t.T, conv_w=d_conv_w, conv_b=d_conv_b,
                 dt_bias=d_dtb[:, :NHEAD], a_log=d_alog[:, :NHEAD], d_skip=d_dsk[:, :NHEAD],
                 ssm_norm_w=d_ssm_norm, final_norm_w=dfw, dmod=dmod, loss=loss8[:1, :1])
    big = dict(w_in_r=g_in_r, w_bg=g_bg, w_bs=g_bs, w_out=g_out, w_ff1=g_ff1, w_ff2=g_ff2)
    return grad_x, small, big


SMALL_KEYS = ("gm_ws", "gm_norm_w", "gm_bs", "conv_w", "conv_b", "dt_bias", "a_log", "d_skip", "ssm_norm_w",
              "final_norm_w", "dmod", "loss")
SMALL_SHAPES = ((GM_G, Q, Q), (1, D), (GM_G, Q), (4, CONV), (1, CONV), (1, NHEAD), (1, NHEAD), (1, NHEAD), (1, DI),
                (1, D), (1, 6 * D), (1, 1))


def _reorder_w_in(w_full):
    k = w_full.shape[0]
    return jnp.concatenate([w_full[:, :8192], w_full[:, 8224:], w_full[:, 8192:8224],
                            jnp.zeros((k, W_IN_R - W_IN), w_full.dtype)], axis=1)


def _restore_w_in(g_r):
    return jnp.concatenate([g_r[:, :8192], g_r[:, OFF_DT:OFF_DT + NHEAD], g_r[:, 8192:OFF_DT]], axis=1)


def kernel(x, c, w_mod, b_mod, w_in, gm_norm_w, gm_ws, gm_bs, conv_w, conv_b, dt_bias, a_log, d_skip, ssm_norm_w, w_branch_gm, w_branch_ssm, w_out, w_ff1, w_ff2, final_norm_w, loss_target, m_w_mod, m_b_mod, m_w_in, m_gm_norm_w, m_gm_ws, m_gm_bs, m_conv_w, m_conv_b, m_dt_bias, m_a_log, m_d_skip, m_ssm_norm_w, m_w_branch_gm, m_w_branch_ssm, m_w_out, m_w_ff1, m_w_ff2, m_final_norm_w, v_w_mod, v_b_mod, v_w_in, v_gm_norm_w, v_gm_ws, v_gm_bs, v_conv_w, v_conv_b, v_dt_bias, v_a_log, v_d_skip, v_ssm_norm_w, v_w_branch_gm, v_w_branch_ssm, v_w_out, v_w_ff1, v_w_ff2, v_final_norm_w):
    ax, ay, ac = lax.axis_index("x"), lax.axis_index("y"), lax.axis_index("c")
    chip = 2 * ax + ay
    dev = 2 * chip + ac
    seq = x.shape[1]
    nmod = w_mod.shape[2]

    first = jnp.concatenate([c, conv_w[0], jnp.zeros((3, D), F32)], axis=0)
    first_all = _all_gather_small(first, "all_gather_cond")
    c8 = first_all[:, 0, :]
    conv_w_full = jnp.concatenate([first_all[2 * k, 1:5, :] for k in range(N_CHIP)], axis=1)
    b_sl = lax.dynamic_slice(b_mod, (0, chip * nmod), (1, nmod))
    mod_part = _mod_fwd(c8, w_mod[0], b_sl)
    mod_all = _all_gather_small(mod_part, "all_gather_mod")
    mod = jnp.concatenate([lax.dynamic_slice(mod_all, (2 * k, dev, 0), (1, 1, nmod))[0] for k in range(N_CHIP)], axis=1)

    packed = jnp.concatenate([w_branch_gm[0], w_branch_ssm[0], w_out[0], w_ff2[0], w_ff1[0]], axis=0).astype(BF16)
    g_in, g_rest = _all_gather_weights([w_in[0].astype(BF16), packed])
    w_in_r = _reorder_w_in(jnp.transpose(g_in, (1, 0, 2)).reshape(D, W_IN))
    r_bg, r_bs, r_out, r_ff2 = D // N_CHIP, DI // N_CHIP, D // N_CHIP, DFF // N_CHIP
    o1, o2, o3, o4 = r_bg, r_bg + r_bs, r_bg + r_bs + r_out, r_bg + r_bs + r_out + r_ff2
    w_bg_f = g_rest[:, :o1].reshape(D, D)
    w_bs_f = g_rest[:, o1:o2].reshape(DI, D)
    w_out_f = g_rest[:, o2:o3].reshape(D, D)
    w_ff2_f = g_rest[:, o3:o4].reshape(DFF, D)
    w_ff1_f = jnp.transpose(g_rest[:, o4:], (1, 0, 2)).reshape(D, DFF)

    grad_x, small, big = _local_step(
        x[0], loss_target[0], mod, w_in_r, w_bg_f, w_bs_f, w_out_f, w_ff1_f, w_ff2_f, gm_norm_w, gm_ws[0], gm_bs[0],
        conv_w_full, conv_b, dt_bias, a_log, d_skip, ssm_norm_w, final_norm_w.reshape(1, D))

    small_all = _all_gather_small(_pack([small[k] for k in SMALL_KEYS]), "all_gather_small_grads")
    small_sum = _sum_leading(small_all, "sum_small_grads")
    s_ws, s_gnw, s_bs, s_cw, s_cb, s_dtb, s_alog, s_dsk, s_snw, s_fnw, s_bmod, s_loss = _unpack(small_sum, SMALL_SHAPES)
    dmod_all = jnp.stack([_unpack(small_all[k], SMALL_SHAPES)[10][0] for k in range(N_DEV)], axis=0)
    g_w_mod = _mod_wgrad(c8, lax.dynamic_slice(dmod_all, (0, chip * nmod), (N_DEV, nmod)))
    g_conv_w = lax.dynamic_slice(s_cw, (0, chip * (CONV // N_CHIP)), (4, CONV // N_CHIP))

    g_in_full = _restore_w_in(big["w_in_r"])
    send_in = jnp.transpose(g_in_full.reshape(D, N_CHIP, W_IN // N_CHIP), (1, 0, 2)).astype(BF16)
    send_rest = jnp.concatenate([
        big["w_bg"].reshape(N_CHIP, r_bg, D), big["w_bs"].reshape(N_CHIP, r_bs, D), big["w_out"].reshape(N_CHIP, r_out, D),
        big["w_ff2"].reshape(N_CHIP, r_ff2, D), jnp.transpose(big["w_ff1"].reshape(D, N_CHIP, D), (1, 0, 2))], axis=1).astype(BF16)
    mines, theirs = _exchange_halves([send_in, send_rest])
    pair = [_sum_leading((mines[t], theirs[t]), f"pair_sum_{t}", out_dtype=BF16) for t in range(2)]
    contrib = _exchange_chips(pair)
    halves = [_sum_leading(contrib[t], f"chip_sum_{t}") for t in range(2)]
    g_w_in, g_rest_sh = _share_halves(halves)
    g_w_bg, g_w_bs, g_w_out, g_w_ff2, g_w_ff1 = g_rest_sh[:o1], g_rest_sh[o1:o2], g_rest_sh[o2:o3], g_rest_sh[o3:o4], g_rest_sh[o4:]

    def adam_big(w, g, m, v, name):
        d, m2, v2 = _adam(w.reshape(g.shape), g, m.reshape(g.shape), v.reshape(g.shape), name)
        return g.reshape(w.shape), d.reshape(w.shape), m2.reshape(w.shape), v2.reshape(w.shape)

    res = {}
    res["w_mod"] = adam_big(w_mod, g_w_mod, m_w_mod, v_w_mod, "adam_w_mod")
    res["w_in"] = adam_big(w_in, g_w_in, m_w_in, v_w_in, "adam_w_in")
    res["w_branch_gm"] = adam_big(w_branch_gm, g_w_bg, m_w_branch_gm, v_w_branch_gm, "adam_w_branch_gm")
    res["w_branch_ssm"] = adam_big(w_branch_ssm, g_w_bs, m_w_branch_ssm, v_w_branch_ssm, "adam_w_branch_ssm")
    res["w_out"] = adam_big(w_out, g_w_out, m_w_out, v_w_out, "adam_w_out")
    res["w_ff1"] = adam_big(w_ff1, g_w_ff1, m_w_ff1, v_w_ff1, "adam_w_ff1")
    res["w_ff2"] = adam_big(w_ff2, g_w_ff2, m_w_ff2, v_w_ff2, "adam_w_ff2")

    names = ("b_mod", "gm_norm_w", "gm_ws", "gm_bs", "conv_w", "conv_b", "dt_bias", "a_log", "d_skip", "ssm_norm_w", "final_norm_w")
    ws = (b_mod, gm_norm_w, gm_ws, gm_bs, conv_w, conv_b, dt_bias, a_log, d_skip, ssm_norm_w, final_norm_w)
    ms = (m_b_mod, m_gm_norm_w, m_gm_ws, m_gm_bs, m_conv_w, m_conv_b, m_dt_bias, m_a_log, m_d_skip, m_ssm_norm_w, m_final_norm_w)
    vs = (v_b_mod, v_gm_norm_w, v_gm_ws, v_gm_bs, v_conv_w, v_conv_b, v_dt_bias, v_a_log, v_d_skip, v_ssm_norm_w, v_final_norm_w)
    gs = (s_bmod, s_gnw, s_ws, s_bs, g_conv_w, s_cb, s_dtb, s_alog, s_dsk, s_snw, s_fnw)
    gs = [g.reshape(w.shape) for g, w in zip(gs, ws)]
    shapes = [w.shape for w in ws]
    d_p, m_p, v_p = _adam(_pack(ws), _pack(gs), _pack(ms), _pack(vs), "adam_small")
    for name, g, d, m2, v2 in zip(names, gs, _unpack(d_p, shapes), _unpack(m_p, shapes), _unpack(v_p, shapes)):
        res[name] = (g, d, m2, v2)

    order = ("w_mod", "b_mod", "w_in", "gm_norm_w", "gm_ws", "gm_bs", "conv_w", "conv_b", "dt_bias", "a_log", "d_skip",
             "ssm_norm_w", "w_branch_gm", "w_branch_ssm", "w_out", "w_ff1", "w_ff2", "final_norm_w")
    loss = s_loss.reshape(())
    return (loss, grad_x.reshape(x.shape), *[res[k][0] for k in order], *[res[k][1] for k in order],
            *[res[k][2] for k in order], *[res[k][3] for k in order])
```

```python
import functools

import jax
import jax.numpy as jnp
from jax import lax
from jax.experimental import pallas as pl
from jax.experimental.pallas import tpu as pltpu

F32 = jnp.float32
BF16 = jnp.bfloat16
MESH = pl.DeviceIdType.MESH
HIGHEST = lax.Precision.HIGHEST

D = 1024
EPS = 1e-6
Q = 128
GM_G = 8
NHEAD = 32
NGRP = 8
DI = 2048
CONV = 4096
DFF = 4096
W_IN = 10272
W_IN_R = 10368
OFF_Z, OFF_XBC, OFF_GA, OFF_DT = 2048, 4096, 8192, 10240
N_CHIP = 4
N_DEV = 8
AG_CHUNKS = 4
D2D_SPLIT = 4

ADAM_LR, ADAM_B1, ADAM_B2, ADAM_EPS, ADAM_WD, ADAM_STEP = 0.001, 0.9, 0.999, 1e-08, 0.01, 10

ANY = pl.BlockSpec(memory_space=pl.ANY)
VMEM = pl.BlockSpec(memory_space=pltpu.VMEM)


def _full(shape):
    return pl.BlockSpec(shape, lambda *_: (0,) * len(shape))


def _pick(n, prefs):
    for p in prefs:
        if n % p == 0:
            return p
    return n


def _rms(x):
    return x * lax.rsqrt(jnp.mean(x * x, axis=-1, keepdims=True) + EPS)


def _modnorm(x, sc, sh):
    return _rms(x) * (1.0 + sc) + sh


def _sgu_pre(u, v, w):
    return jax.nn.gelu(u), _rms(jax.nn.gelu(v)) * w


def _gatenorm(y, z, w):
    g = y * jax.nn.silu(z)
    return _rms(g) * w


def _mix(ga, gb, ba, bb):
    return jax.nn.sigmoid(ga) * ba + jax.nn.sigmoid(gb) * bb


def _loss_tile(x1, fo, g2, fw, t):
    x2 = x1 + g2 * fo
    y = _rms(x2) * fw
    err = jnp.square(y - t)
    return 0.5 * jnp.sum(jnp.mean(err, axis=-1))


def _tril(n):
    r = lax.broadcasted_iota(jnp.int32, (n, n), 0)
    c = lax.broadcasted_iota(jnp.int32, (n, n), 1)
    return r >= c


def _dt_prep(dtr, dtb, alog):
    dt = jax.nn.softplus(dtr + dtb)
    a = dt * (-jnp.exp(alog))
    ones = _tril(Q).astype(F32)
    cum = jnp.dot(ones, a, precision=HIGHEST, preferred_element_type=F32)
    cum_t = lax.dot_general(a, ones, (((0,), (1,)), ((), ())), precision=HIGHEST, preferred_element_type=F32)
    return dt, cum, cum_t


def _ssd_group(x0, x1, s0, s1, bm, cm, dt, cum, cum_t, dsk, grp):
    lane = lax.broadcasted_iota(jnp.int32, (1, 128), 1)
    sub = lax.broadcasted_iota(jnp.int32, (128, 1), 0)
    causal = _tril(Q)
    half = lane < 64
    half_rows = sub < 64
    bmb = bm.astype(BF16)
    cmb = cm.astype(BF16)
    cb = lax.dot_general(cmb, bmb, (((1,), (1,)), ((), ())), preferred_element_type=F32)

    def col(v, h):
        return jnp.sum(jnp.where(lane == h, v, 0.0), axis=1, keepdims=True)

    def row(v, h):
        return jnp.sum(jnp.where(sub == h, v, 0.0), axis=0, keepdims=True)

    def last(c):
        return jnp.sum(jnp.where(sub == Q - 1, c, 0.0), axis=0, keepdims=True)

    outs, states = [], []
    for p, (xp, sp) in enumerate(((x0, s0), (x1, s1))):
        h_a = 4 * grp + 2 * p
        h_b = h_a + 1
        dt_a, dt_b = col(dt, h_a), col(dt, h_b)
        cum_a, cum_b = col(cum, h_a), col(cum, h_b)
        row_a, row_b = row(cum_t, h_a), row(cum_t, h_b)
        last_a, last_b = last(cum_a), last(cum_b)
        xdt = xp * jnp.where(half, dt_a, dt_b)
        xdtb = xdt.astype(BF16)
        m_a = (cb * jnp.exp(jnp.where(causal, cum_a - row_a, -jnp.inf))).astype(BF16)
        m_b = (cb * jnp.exp(jnp.where(causal, cum_b - row_b, -jnp.inf))).astype(BF16)
        y_intra = jnp.where(half, jnp.dot(m_a, xdtb, preferred_element_type=F32),
                            jnp.dot(m_b, xdtb, preferred_element_type=F32))
        y_inter = lax.dot_general(cmb, sp.astype(BF16), (((1,), (1,)), ((), ())), preferred_element_type=F32)
        y_inter = y_inter * jnp.where(half, jnp.exp(cum_a), jnp.exp(cum_b))
        w_end = jnp.where(half, jnp.exp(last_a - cum_a), jnp.exp(last_b - cum_b))
        upd = lax.dot_general((xdt * w_end).astype(BF16), bmb, (((0,), (0,)), ((), ())), preferred_element_type=F32)
        states.append(sp * jnp.where(half_rows, jnp.exp(last_a), jnp.exp(last_b)) + upd)
        outs.append(y_intra + y_inter + xp * jnp.where(half, col(dsk, h_a), col(dsk, h_b)))
    return outs[0], outs[1], states[0], states[1]


def _matmul(a, b, mode, *, name, out_dtypes=(F32,), epi=None, epi_ins=(), tm=1024, tn=1024, tk=1024):
    if mode == "nn":
        (m, k), n = a.shape, b.shape[1]
    elif mode == "nt":
        (m, k), n = a.shape, b.shape[0]
    else:
        (k, m), n = a.shape, b.shape[1]
    tm, tn, tk = _pick(m, (tm, 512, 256, 128)), _pick(n, (tn, 1152, 1024, 512, 384, 256, 128)), _pick(k, (tk, 1152, 1024, 512, 256, 128))
    nk = k // tk
    if mode == "nn":
        a_spec = pl.BlockSpec((tm, tk), lambda i, j, kk: (i, kk))
        b_spec = pl.BlockSpec((tk, tn), lambda i, j, kk: (kk, j))
        dims = (((1,), (0,)), ((), ()))
    elif mode == "nt":
        a_spec = pl.BlockSpec((tm, tk), lambda i, j, kk: (i, kk))
        b_spec = pl.BlockSpec((tn, tk), lambda i, j, kk: (j, kk))
        dims = (((1,), (1,)), ((), ()))
    else:
        a_spec = pl.BlockSpec((tk, tm), lambda i, j, kk: (kk, i))
        b_spec = pl.BlockSpec((tk, tn), lambda i, j, kk: (kk, j))
        dims = (((0,), (0,)), ((), ()))
    o_spec = pl.BlockSpec((tm, tn), lambda i, j, kk: (i, j))
    n_epi, n_out = len(epi_ins), len(out_dtypes)

    def body(*refs):
        a_ref, b_ref = refs[0], refs[1]
        e_refs = refs[2:2 + n_epi]
        o_refs = refs[2 + n_epi:2 + n_epi + n_out]
        acc_ref = refs[-1]

        def finish(acc):
            outs = epi(acc, *[e[...] for e in e_refs]) if epi is not None else (acc,)
            for o_ref, val in zip(o_refs, outs):
                o_ref[...] = val.astype(o_ref.dtype)

        part = lax.dot_general(a_ref[...], b_ref[...], dims, preferred_element_type=F32)
        if nk == 1:
            finish(part)
        else:
            kk = pl.program_id(2)

            @pl.when(kk == 0)
            def _():
                acc_ref[...] = part

            @pl.when(kk > 0)
            def _():
                acc_ref[...] += part

            @pl.when(kk == nk - 1)
            def _():
                finish(acc_ref[...])

    return pl.pallas_call(
        body, name=name, grid=(m // tm, n // tn, nk),
        in_specs=[a_spec, b_spec] + [o_spec] * n_epi,
        out_specs=[o_spec] * n_out,
        out_shape=[jax.ShapeDtypeStruct((m, n), dt) for dt in out_dtypes],
        scratch_shapes=[pltpu.VMEM((tm, tn) if nk > 1 else (8, 128), F32)],
        compiler_params=pltpu.CompilerParams(dimension_semantics=("parallel", "parallel", "arbitrary")),
    )(a, b, *epi_ins)


def _row_tile(s):
    return _pick(s, (512, 256, 128))


def _prenorm(x, sc, sh):
    s = x.shape[0]
    tm = _row_tile(s)

    def body(x_ref, sc_ref, sh_ref, h_ref):
        h_ref[...] = _modnorm(x_ref[...], sc_ref[...], sh_ref[...]).astype(BF16)

    row = pl.BlockSpec((tm, D), lambda i: (i, 0))
    return pl.pallas_call(body, name="prenorm", grid=(s // tm,), in_specs=[row, _full((1, D)), _full((1, D))],
                          out_specs=row, out_shape=jax.ShapeDtypeStruct((s, D), BF16))(x, sc, sh)


def _prenorm_bwd(x, sc, sh, dh, dx_res):
    s = x.shape[0]
    tm = _row_tile(s)

    def body(x_ref, sc_ref, sh_ref, dh_ref, dr_ref, dx_ref, dsc_ref, dsh_ref):
        _, vjp = jax.vjp(_modnorm, x_ref[...], sc_ref[...], sh_ref[...])
        dx, dsc, dsh = vjp(dh_ref[...])
        dx_ref[...] = dr_ref[...] + dx

        @pl.when(pl.program_id(0) == 0)
        def _():
            dsc_ref[...] = jnp.zeros_like(dsc_ref)
            dsh_ref[...] = jnp.zeros_like(dsh_ref)

        dsc_ref[...] += dsc
        dsh_ref[...] += dsh

    row = pl.BlockSpec((tm, D), lambda i: (i, 0))
    vec = _full((1, D))
    return pl.pallas_call(
        body, name="prenorm_bwd", grid=(s // tm,), in_specs=[row, vec, vec, row, row], out_specs=[row, vec, vec],
        out_shape=[jax.ShapeDtypeStruct((s, D), F32), jax.ShapeDtypeStruct((1, D), F32), jax.ShapeDtypeStruct((1, D), F32)],
    )(x, sc, sh, dh, dx_res)


def _resid_norm(x, mo, g1, sc, sh):
    s = x.shape[0]
    tm = _row_tile(s)

    def body(x_ref, mo_ref, g_ref, sc_ref, sh_ref, x1_ref, h_ref):
        x1 = x_ref[...] + g_ref[...] * mo_ref[...]
        x1_ref[...] = x1
        h_ref[...] = _modnorm(x1, sc_ref[...], sh_ref[...]).astype(BF16)

    row = pl.BlockSpec((tm, D), lambda i: (i, 0))
    vec = _full((1, D))
    return pl.pallas_call(
        body, name="resid_norm", grid=(s // tm,), in_specs=[row, row, vec, vec, vec], out_specs=[row, row],
        out_shape=[jax.ShapeDtypeStruct((s, D), F32), jax.ShapeDtypeStruct((s, D), BF16)],
    )(x, mo, g1, sc, sh)


def _resid_norm_bwd(x1, mo, g1, sc, sh, dh, dx2):
    s = x1.shape[0]
    tm = _row_tile(s)

    def body(x1_ref, mo_ref, g_ref, sc_ref, sh_ref, dh_ref, dx2_ref, dx1_ref, dmo_ref, dg_ref, dsc_ref, dsh_ref):
        _, vjp = jax.vjp(_modnorm, x1_ref[...], sc_ref[...], sh_ref[...])
        dx, dsc, dsh = vjp(dh_ref[...])
        dx1 = dx2_ref[...] + dx
        dx1_ref[...] = dx1
        dmo_ref[...] = (dx1 * g_ref[...]).astype(BF16)

        @pl.when(pl.program_id(0) == 0)
        def _():
            dg_ref[...] = jnp.zeros_like(dg_ref)
            dsc_ref[...] = jnp.zeros_like(dsc_ref)
            dsh_ref[...] = jnp.zeros_like(dsh_ref)

        dg_ref[...] += jnp.sum(dx1 * mo_ref[...], axis=0, keepdims=True)
        dsc_ref[...] += dsc
        dsh_ref[...] += dsh

    row = pl.BlockSpec((tm, D), lambda i: (i, 0))
    vec = _full((1, D))
    vshape = jax.ShapeDtypeStruct((1, D), F32)
    return pl.pallas_call(
        body, name="resid_norm_bwd", grid=(s // tm,), in_specs=[row, row, vec, vec, vec, row, row],
        out_specs=[row, row, vec, vec, vec],
        out_shape=[jax.ShapeDtypeStruct((s, D), F32), jax.ShapeDtypeStruct((s, D), BF16), vshape, vshape, vshape],
    )(x1, mo, g1, sc, sh, dh, dx2)


def _loss_head(x1, fo, g2, fw, t):
    s = x1.shape[0]
    tm = _row_tile(s)

    def body(x1_ref, fo_ref, g_ref, fw_ref, t_ref, loss_ref, dx_ref, dfo_ref, dg_ref, dfw_ref):
        loss, (dx1, dfo, dg, dfw) = jax.value_and_grad(_loss_tile, argnums=(0, 1, 2, 3))(
            x1_ref[...], fo_ref[...], g_ref[...], fw_ref[...], t_ref[...])
        dx_ref[...] = dx1
        dfo_ref[...] = dfo.astype(BF16)

        @pl.when(pl.program_id(0) == 0)
        def _():
            loss_ref[...] = jnp.zeros_like(loss_ref)
            dg_ref[...] = jnp.zeros_like(dg_ref)
            dfw_ref[...] = jnp.zeros_like(dfw_ref)

        loss_ref[...] += jnp.full(loss_ref.shape, loss, F32)
        dg_ref[...] += dg
        dfw_ref[...] += dfw

    row = pl.BlockSpec((tm, D), lambda i: (i, 0))
    vec = _full((1, D))
    vshape = jax.ShapeDtypeStruct((1, D), F32)
    return pl.pallas_call(
        body, name="loss_head", grid=(s // tm,), in_specs=[row, row, vec, vec, row],
        out_specs=[_full((8, 128)), row, row, vec, vec],
        out_shape=[jax.ShapeDtypeStruct((8, 128), F32), jax.ShapeDtypeStruct((s, D), F32),
                   jax.ShapeDtypeStruct((s, D), BF16), vshape, vshape],
    )(x1, fo, g2, fw, t)


def _sgu_fwd(proj, norm_w, ws_b, bs_t):
    s = proj.shape[0]
    tm = _pick(s, (256, 128))

    def body(u_ref, v_ref, w_ref, ws_ref, bs_ref, y_ref):
        ug, vn = _sgu_pre(u_ref[...], v_ref[...], w_ref[...])
        vnb = vn.astype(BF16)
        for c in range(tm // Q):
            r = slice(c * Q, (c + 1) * Q)
            for g in range(GM_G):
                cs = slice(g * 128, (g + 1) * 128)
                sv = jnp.dot(ws_ref[g], vnb[r, cs], preferred_element_type=F32) + bs_ref[:, g:g + 1]
                y_ref[r, cs] = (ug[r, cs] * sv).astype(BF16)

    return pl.pallas_call(
        body, name="sgu_fwd", grid=(s // tm,),
        in_specs=[pl.BlockSpec((tm, D), lambda i: (i, 0)), pl.BlockSpec((tm, D), lambda i: (i, 1)),
                  _full((1, D)), _full((GM_G, Q, Q)), _full((Q, GM_G))],
        out_specs=pl.BlockSpec((tm, D), lambda i: (i, 0)),
        out_shape=jax.ShapeDtypeStruct((s, D), BF16),
    )(proj, proj, norm_w, ws_b, bs_t)


def _sgu_bwd(proj, norm_w, ws_b, bs_t, dy, dproj):
    s = proj.shape[0]
    tm = _pick(s, (256, 128))

    def body(u_ref, v_ref, w_ref, ws_ref, bs_ref, dy_ref, _, duv_ref, dw_ref, dws_ref, dbs_ref, dug_scr, dvn_scr):
        @pl.when(pl.program_id(0) == 0)
        def _():
            dw_ref[...] = jnp.zeros_like(dw_ref)
            dws_ref[...] = jnp.zeros_like(dws_ref)
            dbs_ref[...] = jnp.zeros_like(dbs_ref)

        (ug, vn), vjp = jax.vjp(_sgu_pre, u_ref[...], v_ref[...], w_ref[...])
        vnb = vn.astype(BF16)
        dy = dy_ref[...]
        causal = _tril(Q).astype(F32)
        for c in range(tm // Q):
            r = slice(c * Q, (c + 1) * Q)
            for g in range(GM_G):
                cs = slice(g * 128, (g + 1) * 128)
                blk = vnb[r, cs]
                sv = jnp.dot(ws_ref[g], blk, preferred_element_type=F32) + bs_ref[:, g:g + 1]
                dug_scr[r, cs] = dy[r, cs] * sv
                dsv = dy[r, cs] * ug[r, cs]
                dsvb = dsv.astype(BF16)
                dws_ref[g] += causal * lax.dot_general(dsvb, blk, (((1,), (1,)), ((), ())), preferred_element_type=F32)
                dbs_ref[:, g:g + 1] += jnp.sum(dsv, axis=1, keepdims=True)
                dvn_scr[r, cs] = lax.dot_general(ws_ref[g], dsvb, (((0,), (0,)), ((), ())), preferred_element_type=F32)
        du, dv, dw = vjp((dug_scr[...], dvn_scr[...]))
        duv_ref[:, :D] = du.astype(BF16)
        duv_ref[:, D:] = dv.astype(BF16)
        dw_ref[...] += dw

    return pl.pallas_call(
        body, name="sgu_bwd", grid=(s // tm,),
        in_specs=[pl.BlockSpec((tm, D), lambda i: (i, 0)), pl.BlockSpec((tm, D), lambda i: (i, 1)),
                  _full((1, D)), _full((GM_G, Q, Q)), _full((Q, GM_G)), pl.BlockSpec((tm, D), lambda i: (i, 0)), ANY],
        out_specs=[pl.BlockSpec((tm, 2 * D), lambda i: (i, 0)), _full((1, D)), _full((GM_G, Q, Q)), _full((Q, GM_G))],
        out_shape=[jax.ShapeDtypeStruct(dproj.shape, BF16), jax.ShapeDtypeStruct((1, D), F32),
                   jax.ShapeDtypeStruct((GM_G, Q, Q), F32), jax.ShapeDtypeStruct((Q, GM_G), F32)],
        scratch_shapes=[pltpu.VMEM((tm, D), F32), pltpu.VMEM((tm, D), F32)],
        input_output_aliases={6: 0},
    )(proj, proj, norm_w, ws_b, bs_t, dy, dproj)


def _gatenorm_fwd(y_ssd, proj, norm_w):
    s = y_ssd.shape[0]
    tm = _pick(s, (256, 128))
    gw = DI // NGRP

    def body(y_ref, z_ref, w_ref, o_ref):
        for g in range(NGRP):
            cs = slice(g * gw, (g + 1) * gw)
            o_ref[:, cs] = _gatenorm(y_ref[:, cs], z_ref[:, cs], w_ref[:, cs]).astype(BF16)

    return pl.pallas_call(
        body, name="gatenorm_fwd", grid=(s // tm,),
        in_specs=[pl.BlockSpec((tm, DI), lambda i: (i, 0)), pl.BlockSpec((tm, DI), lambda i: (i, OFF_Z // DI)), _full((1, DI))],
        out_specs=pl.BlockSpec((tm, DI), lambda i: (i, 0)),
        out_shape=jax.ShapeDtypeStruct((s, DI), BF16),
    )(y_ssd, proj, norm_w)


def _gatenorm_bwd(y_ssd, proj, norm_w, dyb, dproj):
    s = y_ssd.shape[0]
    tm = _pick(s, (256, 128))
    gw = DI // NGRP

    def body(y_ref, z_ref, w_ref, d_ref, _, dy_ref, dz_ref, dw_ref):
        @pl.when(pl.program_id(0) == 0)
        def _():
            dw_ref[...] = jnp.zeros_like(dw_ref)

        for g in range(NGRP):
            cs = slice(g * gw, (g + 1) * gw)
            _, vjp = jax.vjp(_gatenorm, y_ref[:, cs], z_ref[:, cs], w_ref[:, cs])
            dy, dz, dw = vjp(d_ref[:, cs])
            dy_ref[:, cs] = dy
            dz_ref[:, cs] = dz.astype(BF16)
            dw_ref[:, cs] += dw

    blk = pl.BlockSpec((tm, DI), lambda i: (i, 0))
    zblk = pl.BlockSpec((tm, DI), lambda i: (i, OFF_Z // DI))
    return pl.pallas_call(
        body, name="gatenorm_bwd", grid=(s // tm,),
        in_specs=[blk, zblk, _full((1, DI)), blk, ANY],
        out_specs=[blk, zblk, _full((1, DI))],
        out_shape=[jax.ShapeDtypeStruct((s, DI), F32), jax.ShapeDtypeStruct(dproj.shape, BF16), jax.ShapeDtypeStruct((1, DI), F32)],
        input_output_aliases={4: 1},
    )(y_ssd, proj, norm_w, dyb, dproj)


def _mix_fwd(proj, ba, bb):
    s = proj.shape[0]
    tm = _row_tile(s)
    gb0 = OFF_GA // D

    def body(ga_ref, gb_ref, ba_ref, bb_ref, o_ref):
        o_ref[...] = _mix(ga_ref[...], gb_ref[...], ba_ref[...], bb_ref[...]).astype(BF16)

    row = pl.BlockSpec((tm, D), lambda i: (i, 0))
    return pl.pallas_call(
        body, name="mix_fwd", grid=(s // tm,),
        in_specs=[pl.BlockSpec((tm, D), lambda i: (i, gb0)), pl.BlockSpec((tm, D), lambda i: (i, gb0 + 1)), row, row],
        out_specs=row, out_shape=jax.ShapeDtypeStruct((s, D), BF16),
    )(proj, proj, ba, bb)


def _mix_bwd(proj, ba, bb, dmixed):
    s = proj.shape[0]
    tm = _row_tile(s)
    gb0 = OFF_GA // D

    def body(ga_ref, gb_ref, ba_ref, bb_ref, d_ref, dg_ref, dba_ref, dbb_ref):
        _, vjp = jax.vjp(_mix, ga_ref[...], gb_ref[...], ba_ref[...], bb_ref[...])
        dga, dgb, dba, dbb = vjp(d_ref[...])
        dg_ref[:, :D] = dga.astype(BF16)
        dg_ref[:, D:] = dgb.astype(BF16)
        dba_ref[...] = dba.astype(BF16)
        dbb_ref[...] = dbb.astype(BF16)

    row = pl.BlockSpec((tm, D), lambda i: (i, 0))
    return pl.pallas_call(
        body, name="mix_bwd", grid=(s // tm,),
        in_specs=[pl.BlockSpec((tm, D), lambda i: (i, gb0)), pl.BlockSpec((tm, D), lambda i: (i, gb0 + 1)), row, row, row],
        out_specs=[pl.BlockSpec((tm, 2 * D), lambda i: (i, OFF_GA // (2 * D))), row, row],
        out_shape=[jax.ShapeDtypeStruct((s, W_IN_R), BF16), jax.ShapeDtypeStruct((s, D), BF16), jax.ShapeDtypeStruct((s, D), BF16)],
    )(proj, proj, ba, bb, dmixed)


CONV_TC = 1024


def _conv_fwd(proj, conv_w, conv_b):
    s = proj.shape[0]
    tm = _row_tile(s)
    cb0 = OFF_XBC // CONV_TC

    def body(x_ref, halo_ref, w_ref, b_ref, o_ref):
        halo = jnp.where(pl.program_id(0) > 0, halo_ref[...], 0.0)
        ext = jnp.concatenate([halo, x_ref[...]], axis=0)
        acc = jnp.broadcast_to(b_ref[...], (tm, CONV_TC))
        for k in range(4):
            shifted = ext if k == 3 else pltpu.roll(ext, 3 - k, 0)
            acc = acc + w_ref[k:k + 1, :] * shifted[8:, :]
        o_ref[...] = jax.nn.silu(acc)

    return pl.pallas_call(
        body, name="conv_fwd", grid=(s // tm, CONV // CONV_TC),
        in_specs=[pl.BlockSpec((tm, CONV_TC), lambda i, j: (i, cb0 + j)),
                  pl.BlockSpec((8, CONV_TC), lambda i, j: (jnp.maximum(i * (tm // 8) - 1, 0), cb0 + j)),
                  pl.BlockSpec((4, CONV_TC), lambda i, j: (0, j)), pl.BlockSpec((1, CONV_TC), lambda i, j: (0, j))],
        out_specs=pl.BlockSpec((tm, CONV_TC), lambda i, j: (i, j)),
        out_shape=jax.ShapeDtypeStruct((s, CONV), F32),
    )(proj, proj, conv_w, conv_b)


def _conv_bwd(proj, conv_w, conv_b, dact, dproj, col0, name):
    s, width = dact.shape
    tm = _row_tile(s)
    nt = s // tm
    c0 = col0 // CONV_TC
    cb0 = OFF_XBC // CONV_TC + c0

    def body(x_ref, prev_ref, next_ref, d_ref, dnext_ref, w_ref, b_ref, _, dx_ref, dw_ref, db_ref):
        i = pl.program_id(1)
        prev = jnp.where(i > 0, prev_ref[...], 0.0)
        ext = jnp.concatenate([prev, x_ref[...], next_ref[...]], axis=0)
        dext = jnp.concatenate([d_ref[...], jnp.where(i < nt - 1, dnext_ref[...], 0.0)], axis=0)
        pre = jnp.broadcast_to(b_ref[...], (tm + 8, CONV_TC))
        taps = []
        for k in range(4):
            shifted = (ext if k == 3 else pltpu.roll(ext, 3 - k, 0))[8:, :]
            taps.append(shifted)
            pre = pre + w_ref[k:k + 1, :] * shifted
        sig = jax.nn.sigmoid(pre)
        dpre = dext * (sig * (1.0 + pre * (1.0 - sig)))
        dx = jnp.zeros((tm, CONV_TC), F32)
        for k in range(4):
            shifted = dpre if k == 3 else pltpu.roll(dpre, tm + 8 - (3 - k), 0)
            dx = dx + w_ref[k:k + 1, :] * shifted[:tm, :]
        dx_ref[...] = dx.astype(BF16)

        @pl.when(i == 0)
        def _():
            dw_ref[...] = jnp.zeros_like(dw_ref)
            db_ref[...] = jnp.zeros_like(db_ref)

        dtile = dpre[:tm, :]
        for k in range(4):
            dw_ref[k:k + 1, :] += jnp.sum(dtile * taps[k][:tm, :], axis=0, keepdims=True)
        db_ref[...] += jnp.sum(dtile, axis=0, keepdims=True)

    r8 = tm // 8
    return pl.pallas_call(
        body, name=name, grid=(width // CONV_TC, nt),
        in_specs=[pl.BlockSpec((tm, CONV_TC), lambda j, i: (i, cb0 + j)),
                  pl.BlockSpec((8, CONV_TC), lambda j, i: (jnp.maximum(i * r8 - 1, 0), cb0 + j)),
                  pl.BlockSpec((8, CONV_TC), lambda j, i: (jnp.minimum((i + 1) * r8, nt * r8 - 1), cb0 + j)),
                  pl.BlockSpec((tm, CONV_TC), lambda j, i: (i, j)),
                  pl.BlockSpec((8, CONV_TC), lambda j, i: (jnp.minimum((i + 1) * r8, nt * r8 - 1), j)),
                  pl.BlockSpec((4, CONV_TC), lambda j, i: (0, c0 + j)), pl.BlockSpec((1, CONV_TC), lambda j, i: (0, c0 + j)), ANY],
        out_specs=[pl.BlockSpec((tm, CONV_TC), lambda j, i: (i, cb0 + j)),
                   pl.BlockSpec((4, CONV_TC), lambda j, i: (0, j)), pl.BlockSpec((1, CONV_TC), lambda j, i: (0, j))],
        out_shape=[jax.ShapeDtypeStruct(dproj.shape, BF16), jax.ShapeDtypeStruct((4, width), F32), jax.ShapeDtypeStruct((1, width), F32)],
        input_output_aliases={7: 0},
    )(proj, proj, proj, dact, dact, conv_w, conv_b, dproj)


def _dt_fwd(proj, dtb, alog):
    s = proj.shape[0]
    nc = s // Q

    def body(r_ref, b_ref, a_ref, dt_ref, cum_ref, cumt_ref):
        dt, cum, cum_t = _dt_prep(r_ref[...], b_ref[...], a_ref[...])
        dt_ref[...] = dt
        cum_ref[...] = cum
        cumt_ref[...] = cum_t

    blk = pl.BlockSpec((Q, 128), lambda n: (n, 0))
    return pl.pallas_call(
        body, name="dt_fwd", grid=(nc,),
        in_specs=[pl.BlockSpec((Q, 128), lambda n: (n, OFF_DT // 128)), _full((1, 128)), _full((1, 128))],
        out_specs=[blk, blk, pl.BlockSpec((None, 128, Q), lambda n: (n, 0, 0))],
        out_shape=[jax.ShapeDtypeStruct((s, 128), F32), jax.ShapeDtypeStruct((s, 128), F32), jax.ShapeDtypeStruct((nc, 128, Q), F32)],
    )(proj, dtb, alog)


def _dt_bwd(proj, dtb, alog, ddt, dcum, dcumt, dproj):
    s = proj.shape[0]
    nc = s // Q

    def body(r_ref, b_ref, a_ref, ddt_ref, dcum_ref, dcumt_ref, _, dr_ref, db_ref, da_ref):
        _, vjp = jax.vjp(_dt_prep, r_ref[...], b_ref[...], a_ref[...])
        dr, db, da = vjp((ddt_ref[...], dcum_ref[...], dcumt_ref[...]))
        dr_ref[...] = dr.astype(BF16)

        @pl.when(pl.program_id(0) == 0)
        def _():
            db_ref[...] = jnp.zeros_like(db_ref)
            da_ref[...] = jnp.zeros_like(da_ref)

        db_ref[...] += db
        da_ref[...] += da

    blk = pl.BlockSpec((Q, 128), lambda n: (n, 0))
    pblk = pl.BlockSpec((Q, 128), lambda n: (n, OFF_DT // 128))
    return pl.pallas_call(
        body, name="dt_bwd", grid=(nc,),
        in_specs=[pblk, _full((1, 128)), _full((1, 128)), blk, blk, pl.BlockSpec((None, 128, Q), lambda n: (n, 0, 0)), ANY],
        out_specs=[pblk, _full((1, 128)), _full((1, 128))],
        out_shape=[jax.ShapeDtypeStruct(dproj.shape, BF16), jax.ShapeDtypeStruct((1, 128), F32), jax.ShapeDtypeStruct((1, 128), F32)],
        input_output_aliases={6: 0},
    )(proj, dtb, alog, ddt, dcum, dcumt, dproj)


def _ssd_specs(chunk_of):
    xs = pl.BlockSpec((Q, 256), lambda n, g: (chunk_of(n), g))
    bm = pl.BlockSpec((Q, 128), lambda n, g: (chunk_of(n), DI // 128 + g))
    cm = pl.BlockSpec((Q, 128), lambda n, g: (chunk_of(n), DI // 128 + NGRP + g))
    per_chunk = pl.BlockSpec((Q, 128), lambda n, g: (chunk_of(n), 0))
    cum_t = pl.BlockSpec((None, 128, Q), lambda n, g: (chunk_of(n), 0, 0))
    state = pl.BlockSpec((None, 256, 128), lambda n, g: (chunk_of(n), g, 0))
    vec = pl.BlockSpec((1, 128), lambda n, g: (0, 0))
    return xs, bm, cm, per_chunk, cum_t, state, vec


def _ssd_fwd(xbc, dt, cum, cum_t, dsk):
    s = xbc.shape[0]
    nc = s // Q
    xs, bm, cm, per_chunk, cumt_spec, state_spec, vec = _ssd_specs(lambda n: n)

    def body(x_ref, b_ref, c_ref, dt_ref, cum_ref, cumt_ref, dsk_ref, y_ref, st_ref, carry):
        n, g = pl.program_id(0), pl.program_id(1)
        rows = pl.ds(pl.multiple_of(g * 256, 256), 256)

        @pl.when(n == 0)
        def _():
            carry[rows, :] = jnp.zeros((256, 128), F32)

        s_in = carry[rows, :]
        st_ref[...] = s_in
        y0, y1, n0, n1 = _ssd_group(x_ref[:, :128], x_ref[:, 128:], s_in[:128], s_in[128:], b_ref[...], c_ref[...],
                                    dt_ref[...], cum_ref[...], cumt_ref[...], dsk_ref[...], g)
        y_ref[:, :128] = y0
        y_ref[:, 128:] = y1
        carry[pl.ds(pl.multiple_of(g * 256, 256), 128), :] = n0
        carry[pl.ds(pl.multiple_of(g * 256 + 128, 128), 128), :] = n1

    return pl.pallas_call(
        body, name="ssd_fwd", grid=(nc, NGRP),
        in_specs=[xs, bm, cm, per_chunk, per_chunk, cumt_spec, vec],
        out_specs=[xs, state_spec],
        out_shape=[jax.ShapeDtypeStruct((s, DI), F32), jax.ShapeDtypeStruct((nc, DI, 128), F32)],
        scratch_shapes=[pltpu.VMEM((DI, 128), F32)],
    )(xbc, xbc, xbc, dt, cum, cum_t, dsk)


def _ssd_bwd(xbc, dt, cum, cum_t, dsk, states, dy):
    s = xbc.shape[0]
    nc = s // Q
    xs, bm, cm, per_chunk, cumt_spec, state_spec, vec = _ssd_specs(lambda n: nc - 1 - n)

    def body(x_ref, b_ref, c_ref, dt_ref, cum_ref, cumt_ref, dsk_ref, st_ref, dy_ref,
             dx_ref, db_ref, dc_ref, ddt_ref, dcum_ref, dcumt_ref, ddsk_ref, carry):
        n, g = pl.program_id(0), pl.program_id(1)
        rows = pl.ds(pl.multiple_of(g * 256, 256), 256)

        @pl.when(n == 0)
        def _():
            carry[rows, :] = jnp.zeros((256, 128), F32)

        @pl.when((n == 0) & (g == 0))
        def _():
            ddsk_ref[...] = jnp.zeros_like(ddsk_ref)

        @pl.when(g == 0)
        def _():
            ddt_ref[...] = jnp.zeros_like(ddt_ref)
            dcum_ref[...] = jnp.zeros_like(dcum_ref)
            dcumt_ref[...] = jnp.zeros_like(dcumt_ref)

        s_in = st_ref[...]
        d_out = carry[rows, :]
        fn = functools.partial(_ssd_group, grp=g)
        _, vjp = jax.vjp(fn, x_ref[:, :128], x_ref[:, 128:], s_in[:128], s_in[128:], b_ref[...], c_ref[...],
                         dt_ref[...], cum_ref[...], cumt_ref[...], dsk_ref[...])
        dx0, dx1, ds0, ds1, dbm, dcm, ddt, dcum, dcumt, ddsk = vjp((dy_ref[:, :128], dy_ref[:, 128:], d_out[:128], d_out[128:]))
        dx_ref[:, :128] = dx0
        dx_ref[:, 128:] = dx1
        db_ref[...] = dbm
        dc_ref[...] = dcm
        ddt_ref[...] += ddt
        dcum_ref[...] += dcum
        dcumt_ref[...] += dcumt
        ddsk_ref[...] += ddsk
        carry[pl.ds(pl.multiple_of(g * 256, 256), 128), :] = ds0
        carry[pl.ds(pl.multiple_of(g * 256 + 128, 128), 128), :] = ds1

    grp_blk = pl.BlockSpec((Q, 128), lambda n, g: (nc - 1 - n, g))
    return pl.pallas_call(
        body, name="ssd_bwd", grid=(nc, NGRP),
        in_specs=[xs, bm, cm, per_chunk, per_chunk, cumt_spec, vec, state_spec, xs],
        out_specs=[xs, grp_blk, grp_blk, per_chunk, per_chunk, cumt_spec, vec],
        out_shape=[jax.ShapeDtypeStruct((s, DI), F32), jax.ShapeDtypeStruct((s, D), F32), jax.ShapeDtypeStruct((s, D), F32),
                   jax.ShapeDtypeStruct((s, 128), F32), jax.ShapeDtypeStruct((s, 128), F32),
                   jax.ShapeDtypeStruct((nc, 128, Q), F32), jax.ShapeDtypeStruct((1, 128), F32)],
        scratch_shapes=[pltpu.VMEM((DI, 128), F32)],
    )(xbc, xbc, xbc, dt, cum, cum_t, dsk, states, dy)


def _adam_math(w, g, m, v):
    m2 = ADAM_B1 * m + (1.0 - ADAM_B1) * g
    v2 = ADAM_B2 * v + (1.0 - ADAM_B2) * jnp.square(g)
    m_hat = m2 / (1.0 - ADAM_B1 ** ADAM_STEP)
    v_hat = v2 / (1.0 - ADAM_B2 ** ADAM_STEP)
    delta = -ADAM_LR * (m_hat / (jnp.sqrt(v_hat) + ADAM_EPS) + ADAM_WD * w)
    return delta, m2, v2


def _adam(w, g, m, v, name):
    r, c = w.shape
    tr = r if r * c * 4 <= (1 << 20) else _pick(r, (128, 64, 32, 16, 8))

    def body(w_ref, g_ref, m_ref, v_ref, d_ref, m2_ref, v2_ref):
        d, m2, v2 = _adam_math(w_ref[...], g_ref[...], m_ref[...], v_ref[...])
        d_ref[...] = d
        m2_ref[...] = m2
        v2_ref[...] = v2

    blk = pl.BlockSpec((tr, c), lambda i: (i, 0))
    shp = jax.ShapeDtypeStruct((r, c), F32)
    return pl.pallas_call(body, name=name, grid=(r // tr,), in_specs=[blk] * 4, out_specs=[blk] * 3,
                          out_shape=[shp] * 3)(w, g, m, v)


def _sum_leading(xs, name, out_dtype=F32, tr=256):
    if isinstance(xs, tuple):
        a, b = xs
        n, r, c = a.shape
        tr = _pick(r, (tr, 128, 64, 32, 16, 8))

        def body2(a_ref, b_ref, o_ref):
            o_ref[...] = (a_ref[...].astype(F32) + b_ref[...].astype(F32)).astype(out_dtype)

        blk = pl.BlockSpec((None, tr, c), lambda s, i: (s, i, 0))
        return pl.pallas_call(body2, name=name, grid=(n, r // tr), in_specs=[blk, blk], out_specs=blk,
                              out_shape=jax.ShapeDtypeStruct((n, r, c), out_dtype))(a, b)
    n, r, c = xs.shape
    tr = r if n * r * c * 4 <= (8 << 20) else _pick(r, (tr, 128, 64, 32, 16, 8))

    def body(x_ref, o_ref):
        acc = x_ref[0].astype(F32)
        for s in range(1, n):
            acc = acc + x_ref[s].astype(F32)
        o_ref[...] = acc.astype(out_dtype)

    return pl.pallas_call(body, name=name, grid=(r // tr,), in_specs=[pl.BlockSpec((n, tr, c), lambda i: (0, i, 0))],
                          out_specs=pl.BlockSpec((tr, c), lambda i: (i, 0)),
                          out_shape=jax.ShapeDtypeStruct((r, c), out_dtype))(xs)


def _mod_fwd(c8, w_mod, b_sl):
    def body(c_ref, w_ref, b_ref, o_ref):
        o_ref[...] = jnp.dot(jax.nn.silu(c_ref[...]), w_ref[...], precision=HIGHEST, preferred_element_type=F32) + b_ref[...]

    return pl.pallas_call(body, name="mod_fwd", in_specs=[VMEM, VMEM, VMEM], out_specs=VMEM,
                          out_shape=jax.ShapeDtypeStruct((N_DEV, w_mod.shape[1]), F32))(c8, w_mod, b_sl)


def _mod_wgrad(c8, dmod):
    def body(c_ref, d_ref, o_ref):
        o_ref[...] = lax.dot_general(jax.nn.silu(c_ref[...]), d_ref[...], (((0,), (0,)), ((), ())),
                                     precision=HIGHEST, preferred_element_type=F32)

    return pl.pallas_call(body, name="mod_wgrad", in_specs=[VMEM, VMEM], out_specs=VMEM,
                          out_shape=jax.ShapeDtypeStruct((D, dmod.shape[1]), F32))(c8, dmod)


def _place():
    x, y, c = lax.axis_index("x"), lax.axis_index("y"), lax.axis_index("c")
    chips = [(1 - x, y), (x, 1 - y), (1 - x, 1 - y)]
    return x, y, c, chips


def _all_gather_small(v, name):
    r, w = v.shape

    def body(v_ref, o_ref, send_sems, recv_sems, local_sem):
        x, y, c, _ = _place()
        me = 4 * x + 2 * y + c
        own = pltpu.make_async_copy(v_ref, o_ref.at[me], local_sem)
        own.start()
        sends = []
        for k in range(1, N_DEV):
            tx = 1 - x if k & 4 else x
            ty = 1 - y if k & 2 else y
            tc = 1 - c if k & 1 else c
            peer = 4 * tx + 2 * ty + tc
            cp = pltpu.make_async_remote_copy(src_ref=v_ref, dst_ref=o_ref.at[me], send_sem=send_sems.at[k - 1],
                                              recv_sem=recv_sems.at[k - 1], device_id=(tx, ty, tc), device_id_type=MESH)
            cp.start()
            sends.append((cp, peer, (tx, ty, tc)))
        for k, (cp, peer, dev) in enumerate(sends):
            pltpu.make_async_remote_copy(src_ref=v_ref, dst_ref=o_ref.at[peer], send_sem=send_sems.at[k],
                                         recv_sem=recv_sems.at[k], device_id=dev, device_id_type=MESH).wait_recv()
        for cp, _, _ in sends:
            cp.wait_send()
        own.wait()

    return pl.pallas_call(
        body, name=name, in_specs=[VMEM], out_specs=VMEM, out_shape=jax.ShapeDtypeStruct((N_DEV, r, w), v.dtype),
        scratch_shapes=[pltpu.SemaphoreType.DMA((N_DEV - 1,)), pltpu.SemaphoreType.DMA((N_DEV - 1,)), pltpu.SemaphoreType.DMA],
    )(v)


def _all_gather_weights(shards):
    n = len(shards)
    nsem = n * 3 * AG_CHUNKS

    def body(*refs):
        srcs, dsts = refs[:n], refs[n:2 * n]
        send_sems, recv_sems, fwd_send_sems, fwd_recv_sems, local_sems = refs[2 * n:]
        x, y, c, chips = _place()
        q = 2 * x + y
        sibling = (x, y, 1 - c)
        sends, owns = [], []
        for t in range(n):
            own = pltpu.make_async_copy(srcs[t], dsts[t].at[q], local_sems.at[t])
            own.start()
            owns.append(own)
        for r in range(AG_CHUNKS):
            for t in range(n):
                hc = srcs[t].shape[0] // 2 // AG_CHUNKS
                rows = pl.ds(c * (hc * AG_CHUNKS) + r * hc, hc)
                for j, (cx, cy) in enumerate(chips):
                    k = (t * 3 + j) * AG_CHUNKS + r
                    cp = pltpu.make_async_remote_copy(src_ref=srcs[t].at[rows], dst_ref=dsts[t].at[q, rows],
                                                      send_sem=send_sems.at[k], recv_sem=recv_sems.at[k],
                                                      device_id=(cx, cy, c), device_id_type=MESH)
                    cp.start()
                    sends.append(cp)
        fwds = []
        for r in range(AG_CHUNKS):
            for t in range(n):
                hc = srcs[t].shape[0] // 2 // AG_CHUNKS
                sub = hc // D2D_SPLIT
                for j, (cx, cy) in enumerate(chips):
                    k = (t * 3 + j) * AG_CHUNKS + r
                    base = c * (hc * AG_CHUNKS) + r * hc
                    part = dsts[t].at[2 * cx + cy, pl.ds(base, hc)]
                    pltpu.make_async_remote_copy(src_ref=part, dst_ref=part, send_sem=send_sems.at[k], recv_sem=recv_sems.at[k],
                                                 device_id=(cx, cy, c), device_id_type=MESH).wait_recv()
                    for u in range(D2D_SPLIT):
                        piece = dsts[t].at[2 * cx + cy, pl.ds(base + u * sub, sub)]
                        pltpu.make_async_remote_copy(src_ref=piece, dst_ref=piece, send_sem=fwd_send_sems.at[k],
                                                     recv_sem=fwd_recv_sems.at[k], device_id=sibling, device_id_type=MESH).start()
                    fwds.append((part, k))
        for r in range(AG_CHUNKS):
            for t in range(n):
                hc = srcs[t].shape[0] // 2 // AG_CHUNKS
                for j, (cx, cy) in enumerate(chips):
                    k = (t * 3 + j) * AG_CHUNKS + r
                    part = dsts[t].at[2 * cx + cy, pl.ds((1 - c) * (hc * AG_CHUNKS) + r * hc, hc)]
                    pltpu.make_async_remote_copy(src_ref=part, dst_ref=part, send_sem=fwd_send_sems.at[k],
                                                 recv_sem=fwd_recv_sems.at[k], device_id=sibling, device_id_type=MESH).wait_recv()
        for cp in sends:
            cp.wait_send()
        for part, k in fwds:
            pltpu.make_async_remote_copy(src_ref=part, dst_ref=part, send_sem=fwd_send_sems.at[k], recv_sem=fwd_recv_sems.at[k],
                                         device_id=sibling, device_id_type=MESH).wait_send()
        for own in owns:
            own.wait()

    return pl.pallas_call(
        body, name="all_gather_weights", in_specs=[ANY] * n, out_specs=[ANY] * n,
        out_shape=[jax.ShapeDtypeStruct((N_CHIP,) + s.shape, s.dtype) for s in shards],
        scratch_shapes=[pltpu.SemaphoreType.DMA((nsem,)), pltpu.SemaphoreType.DMA((nsem,)), pltpu.SemaphoreType.DMA((nsem,)),
                        pltpu.SemaphoreType.DMA((nsem,)), pltpu.SemaphoreType.DMA((n,))],
    )(*shards)


def _exchange_halves(grads):
    n = len(grads)

    def body(*refs):
        srcs, mines, theirs = refs[:n], refs[n:2 * n], refs[2 * n:3 * n]
        send_sems, recv_sems, local_sems = refs[3 * n:]
        x, y, c, _ = _place()
        sibling = (x, y, 1 - c)
        waits = []
        for t in range(n):
            h = srcs[t].shape[1] // 2
            sub = h // D2D_SPLIT
            for s in range(N_CHIP):
                pltpu.make_async_copy(srcs[t].at[s, pl.ds(c * h, h)], mines[t].at[s], local_sems.at[t]).start()
                for u in range(D2D_SPLIT):
                    pltpu.make_async_remote_copy(src_ref=srcs[t].at[s, pl.ds((1 - c) * h + u * sub, sub)],
                                                 dst_ref=theirs[t].at[s, pl.ds(u * sub, sub)],
                                                 send_sem=send_sems.at[t], recv_sem=recv_sems.at[t],
                                                 device_id=sibling, device_id_type=MESH).start()
            whole = pltpu.make_async_remote_copy(src_ref=srcs[t].at[:, pl.ds((1 - c) * h, h)], dst_ref=theirs[t],
                                                 send_sem=send_sems.at[t], recv_sem=recv_sems.at[t],
                                                 device_id=sibling, device_id_type=MESH)
            own = pltpu.make_async_copy(srcs[t].at[:, pl.ds(c * h, h)], mines[t], local_sems.at[t])
            waits.append((own, whole))
        for own, whole in waits:
            whole.wait()
            own.wait()

    half = [jax.ShapeDtypeStruct((N_CHIP, g.shape[1] // 2, g.shape[2]), g.dtype) for g in grads]
    out = pl.pallas_call(
        body, name="exchange_halves", in_specs=[ANY] * n, out_specs=[ANY] * (2 * n), out_shape=half + half,
        scratch_shapes=[pltpu.SemaphoreType.DMA((n,)), pltpu.SemaphoreType.DMA((n,)), pltpu.SemaphoreType.DMA((n,))],
    )(*grads)
    return out[:n], out[n:]


def _exchange_chips(parts):
    n = len(parts)

    def body(*refs):
        srcs, dsts = refs[:n], refs[n:2 * n]
        send_sems, recv_sems, local_sems = refs[2 * n:]
        x, y, c, chips = _place()
        q = 2 * x + y
        sends, owns = [], []
        for t in range(n):
            own = pltpu.make_async_copy(srcs[t].at[q], dsts[t].at[q], local_sems.at[t])
            own.start()
            owns.append(own)
            for j, (cx, cy) in enumerate(chips):
                cp = pltpu.make_async_remote_copy(src_ref=srcs[t].at[2 * cx + cy], dst_ref=dsts[t].at[q],
                                                  send_sem=send_sems.at[3 * t + j], recv_sem=recv_sems.at[3 * t + j],
                                                  device_id=(cx, cy, c), device_id_type=MESH)
                cp.start()
                sends.append(cp)
        for t in range(n):
            for j, (cx, cy) in enumerate(chips):
                part = dsts[t].at[2 * cx + cy]
                pltpu.make_async_remote_copy(src_ref=part, dst_ref=part, send_sem=send_sems.at[3 * t + j],
                                             recv_sem=recv_sems.at[3 * t + j], device_id=(cx, cy, c), device_id_type=MESH).wait_recv()
        for cp in sends:
            cp.wait_send()
        for own in owns:
            own.wait()

    return pl.pallas_call(
        body, name="exchange_chips", in_specs=[ANY] * n, out_specs=[ANY] * n,
        out_shape=[jax.ShapeDtypeStruct(p.shape, p.dtype) for p in parts],
        scratch_shapes=[pltpu.SemaphoreType.DMA((3 * n,)), pltpu.SemaphoreType.DMA((3 * n,)), pltpu.SemaphoreType.DMA((n,))],
    )(*parts)


def _share_halves(halves):
    n = len(halves)

    def body(*refs):
        srcs, dsts = refs[:n], refs[n:2 * n]
        send_sems, recv_sems, local_sems = refs[2 * n:]
        x, y, c, _ = _place()
        sibling = (x, y, 1 - c)
        copies = []
        for t in range(n):
            h = srcs[t].shape[0]
            sub = h // (2 * D2D_SPLIT)
            own = pltpu.make_async_copy(srcs[t], dsts[t].at[pl.ds(c * h, h)], local_sems.at[t])
            own.start()
            for u in range(2 * D2D_SPLIT):
                pltpu.make_async_remote_copy(src_ref=srcs[t].at[pl.ds(u * sub, sub)], dst_ref=dsts[t].at[pl.ds(c * h + u * sub, sub)],
                                             send_sem=send_sems.at[t], recv_sem=recv_sems.at[t],
                                             device_id=sibling, device_id_type=MESH).start()
            copies.append((own, h))
        for t, (own, h) in enumerate(copies):
            theirs = dsts[t].at[pl.ds((1 - c) * h, h)]
            pltpu.make_async_remote_copy(src_ref=srcs[t], dst_ref=theirs, send_sem=send_sems.at[t], recv_sem=recv_sems.at[t],
                                         device_id=sibling, device_id_type=MESH).wait()
            own.wait()

    return pl.pallas_call(
        body, name="share_halves", in_specs=[ANY] * n, out_specs=[ANY] * n,
        out_shape=[jax.ShapeDtypeStruct((2 * h.shape[0], h.shape[1]), h.dtype) for h in halves],
        scratch_shapes=[pltpu.SemaphoreType.DMA((n,)), pltpu.SemaphoreType.DMA((n,)), pltpu.SemaphoreType.DMA((n,))],
    )(*halves)


def _pack(arrs):
    rows = []
    for a in arrs:
        flat = a.astype(F32).reshape(-1)
        pad = (-flat.shape[0]) % 1024
        rows.append(jnp.pad(flat, (0, pad)).reshape(-1, 128))
    return jnp.concatenate(rows, axis=0)


def _unpack(buf, shapes):
    out, r = [], 0
    for shp in shapes:
        size = 1
        for d in shp:
            size *= d
        nr = (size + 1023) // 1024 * 8
        out.append(buf[r:r + nr].reshape(-1)[:size].reshape(shp))
        r += nr
    return out


def _local_step(x, t, mod, w_in_r, w_bg, w_bs, w_out, w_ff1, w_ff2, gm_norm_w, gm_ws, gm_bs, conv_w, conv_b,
                dt_bias, a_log, d_skip, ssm_norm_w, final_norm_w):
    sh1, sc1, g1, sh2, sc2, g2 = [mod[:, i * D:(i + 1) * D] for i in range(6)]
    ws_b = jnp.where(jnp.tril(jnp.ones((Q, Q), bool))[None], gm_ws, 0.0).astype(BF16)
    bs_t = gm_bs.T
    pad32 = lambda v: jnp.pad(v, ((0, 0), (0, 128 - NHEAD)))
    dtb, alog, dsk = pad32(dt_bias), pad32(a_log), pad32(d_skip)

    h1 = _prenorm(x, sc1, sh1)
    (proj,) = _matmul(h1, w_in_r, "nn", name="mm_proj", tn=1152)
    y_a = _sgu_fwd(proj, gm_norm_w, ws_b, bs_t)
    xbc = _conv_fwd(proj, conv_w, conv_b)
    dt, cum, cum_t = _dt_fwd(proj, dtb, alog)
    y_ssd, states = _ssd_fwd(xbc, dt, cum, cum_t, dsk)
    y_b = _gatenorm_fwd(y_ssd, proj, ssm_norm_w)
    (ba,) = _matmul(y_a, w_bg, "nn", name="mm_branch_gm")
    (bb,) = _matmul(y_b, w_bs, "nn", name="mm_branch_ssm")
    mixed = _mix_fwd(proj, ba, bb)
    (mo,) = _matmul(mixed, w_out, "nn", name="mm_out")
    x1, h2 = _resid_norm(x, mo, g1, sc2, sh2)
    f, act = _matmul(h2, w_ff1, "nn", name="mm_ff1", out_dtypes=(BF16, BF16),
                     epi=lambda acc: (acc, jnp.square(jnp.maximum(acc, 0.0))))
    (fo,) = _matmul(act, w_ff2, "nn", name="mm_ff2")
    loss8, dx2, dfo, dg2, dfw = _loss_head(x1, fo, g2, final_norm_w, t)

    (df,) = _matmul(dfo, w_ff2, "nt", name="mm_ff2_dx", out_dtypes=(BF16,), epi_ins=(f,),
                    epi=lambda acc, fv: (acc * (2.0 * jnp.maximum(fv.astype(F32), 0.0)),))
    (g_ff2,) = _matmul(act, dfo, "tn", name="mm_ff2_dw", tk=512)
    (dh2,) = _matmul(df, w_ff1, "nt", name="mm_ff1_dx")
    (g_ff1,) = _matmul(h2, df, "tn", name="mm_ff1_dw", tk=512)
    dx1, dmo, dg1, dsc2, dsh2 = _resid_norm_bwd(x1, mo, g1, sc2, sh2, dh2, dx2)
    (dmixed,) = _matmul(dmo, w_out, "nt", name="mm_out_dx")
    (g_out,) = _matmul(mixed, dmo, "tn", name="mm_out_dw", tk=512)
    dproj, dba, dbb = _mix_bwd(proj, ba, bb, dmixed)
    (dy_a,) = _matmul(dba, w_bg, "nt", name="mm_branch_gm_dx")
    (g_bg,) = _matmul(y_a, dba, "tn", name="mm_branch_gm_dw", tk=512)
    (dy_b,) = _matmul(dbb, w_bs, "nt", name="mm_branch_ssm_dx")
    (g_bs,) = _matmul(y_b, dbb, "tn", name="mm_branch_ssm_dw", tk=512)
    dproj, d_gm_norm, d_ws, d_bs_t = _sgu_bwd(proj, gm_norm_w, ws_b, bs_t, dy_a, dproj)
    dy_ssd, dproj, d_ssm_norm = _gatenorm_bwd(y_ssd, proj, ssm_norm_w, dy_b, dproj)
    dxs, dbm, dcm, ddt, dcum, dcum_t, d_dsk = _ssd_bwd(xbc, dt, cum, cum_t, dsk, states, dy_ssd)
    dproj, d_dtb, d_alog = _dt_bwd(proj, dtb, alog, ddt, dcum, dcum_t, dproj)
    dproj, dw_x, db_x = _conv_bwd(proj, conv_w, conv_b, dxs, dproj, 0, "conv_bwd_x")
    dproj, dw_b, db_b = _conv_bwd(proj, conv_w, conv_b, dbm, dproj, DI, "conv_bwd_b")
    dproj, dw_c, db_c = _conv_bwd(proj, conv_w, conv_b, dcm, dproj, DI + D, "conv_bwd_c")
    d_conv_w = jnp.concatenate([dw_x, dw_b, dw_c], axis=1)
    d_conv_b = jnp.concatenate([db_x, db_b, db_c], axis=1)
    (dh1,) = _matmul(dproj, w_in_r, "nt", name="mm_proj_dx", tk=1152)
    (g_in_r,) = _matmul(h1, dproj, "tn", name="mm_proj_dw", tn=1152, tk=512)
    grad_x, dsc1, dsh1 = _prenorm_bwd(x, sc1, sh1, dh1, dx1)

    dmod = jnp.concatenate([dsh1, dsc1, dg1, dsh2, dsc2, dg2], axis=1)
    small = dict(gm_ws=d_ws, gm_norm_w=d_gm_norm, gm_bs=d_bs_t.T, conv_w=d_conv_w, conv_b=d_conv_b,
                 dt_bias=d_dtb[:, :NHEAD], a_log=d_alog[:, :NHEAD], d_skip=d_dsk[:, :NHEAD],
                 ssm_norm_w=d_ssm_norm, final_norm_w=dfw, dmod=dmod, loss=loss8[:1, :1])
    big = dict(w_in_r=g_in_r, w_bg=g_bg, w_bs=g_bs, w_out=g_out, w_ff1=g_ff1, w_ff2=g_ff2)
    return grad_x, small, big


SMALL_KEYS = ("gm_ws", "gm_norm_w", "gm_bs", "conv_w", "conv_b", "dt_bias", "a_log", "d_skip", "ssm_norm_w",
              "final_norm_w", "dmod", "loss")
SMALL_SHAPES = ((GM_G, Q, Q), (1, D), (GM_G, Q), (4, CONV), (1, CONV), (1, NHEAD), (1, NHEAD), (1, NHEAD), (1, DI),
                (1, D), (1, 6 * D), (1, 1))


def _reorder_w_in(w_full):
    k = w_full.shape[0]
    return jnp.concatenate([w_full[:, :8192], w_full[:, 8224:], w_full[:, 8192:8224],
                            jnp.zeros((k, W_IN_R - W_IN), w_full.dtype)], axis=1)


def _restore_w_in(g_r):
    return jnp.concatenate([g_r[:, :8192], g_r[:, OFF_DT:OFF_DT + NHEAD], g_r[:, 8192:OFF_DT]], axis=1)


def kernel(x, c, w_mod, b_mod, w_in, gm_norm_w, gm_ws, gm_bs, conv_w, conv_b, dt_bias, a_log, d_skip, ssm_norm_w, w_branch_gm, w_branch_ssm, w_out, w_ff1, w_ff2, final_norm_w, loss_target, m_w_mod, m_b_mod, m_w_in, m_gm_norm_w, m_gm_ws, m_gm_bs, m_conv_w, m_conv_b, m_dt_bias, m_a_log, m_d_skip, m_ssm_norm_w, m_w_branch_gm, m_w_branch_ssm, m_w_out, m_w_ff1, m_w_ff2, m_final_norm_w, v_w_mod, v_b_mod, v_w_in, v_gm_norm_w, v_gm_ws, v_gm_bs, v_conv_w, v_conv_b, v_dt_bias, v_a_log, v_d_skip, v_ssm_norm_w, v_w_branch_gm, v_w_branch_ssm, v_w_out, v_w_ff1, v_w_ff2, v_final_norm_w):
    ax, ay, ac = lax.axis_index("x"), lax.axis_index("y"), lax.axis_index("c")
    chip = 2 * ax + ay
    dev = 2 * chip + ac
    seq = x.shape[1]
    nmod = w_mod.shape[2]

    first = jnp.concatenate([c, conv_w[0], jnp.zeros((3, D), F32)], axis=0)
    first_all = _all_gather_small(first, "all_gather_cond")
    c8 = first_all[:, 0, :]
    conv_w_full = jnp.concatenate([first_all[2 * k, 1:5, :] for k in range(N_CHIP)], axis=1)
    b_sl = lax.dynamic_slice(b_mod, (0, chip * nmod), (1, nmod))
    mod_part = _mod_fwd(c8, w_mod[0], b_sl)
    mod_all = _all_gather_small(mod_part, "all_gather_mod")
    mod = jnp.concatenate([lax.dynamic_slice(mod_all, (2 * k, dev, 0), (1, 1, nmod))[0] for k in range(N_CHIP)], axis=1)

    packed = jnp.concatenate([w_branch_gm[0], w_branch_ssm[0], w_out[0], w_ff2[0], w_ff1[0]], axis=0).astype(BF16)
    g_in, g_rest = _all_gather_weights([w_in[0].astype(BF16), packed])
    w_in_r = _reorder_w_in(jnp.transpose(g_in, (1, 0, 2)).reshape(D, W_IN))
    r_bg, r_bs, r_out, r_ff2 = D // N_CHIP, DI // N_CHIP, D // N_CHIP, DFF // N_CHIP
    o1, o2, o3, o4 = r_bg, r_bg + r_bs, r_bg + r_bs + r_out, r_bg + r_bs + r_out + r_ff2
    w_bg_f = g_rest[:, :o1].reshape(D, D)
    w_bs_f = g_rest[:, o1:o2].reshape(DI, D)
    w_out_f = g_rest[:, o2:o3].reshape(D, D)
    w_ff2_f = g_rest[:, o3:o4].reshape(DFF, D)
    w_ff1_f = jnp.transpose(g_rest[:, o4:], (1, 0, 2)).reshape(D, DFF)

    grad_x, small, big = _local_step(
        x[0], loss_target[0], mod, w_in_r, w_bg_f, w_bs_f, w_out_f, w_ff1_f, w_ff2_f, gm_norm_w, gm_ws[0], gm_bs[0],
        conv_w_full, conv_b, dt_bias, a_log, d_skip, ssm_norm_w, final_norm_w.reshape(1, D))

    small_all = _all_gather_small(_pack([small[k] for k in SMALL_KEYS]), "all_gather_small_grads")
    small_sum = _sum_leading(small_all, "sum_small_grads")
    s_ws, s_gnw, s_bs, s_cw, s_cb, s_dtb, s_alog, s_dsk, s_snw, s_fnw, s_bmod, s_loss = _unpack(small_sum, SMALL_SHAPES)
    dmod_all = jnp.stack([_unpack(small_all[k], SMALL_SHAPES)[10][0] for k in range(N_DEV)], axis=0)
    g_w_mod = _mod_wgrad(c8, lax.dynamic_slice(dmod_all, (0, chip * nmod), (N_DEV, nmod)))
    g_conv_w = lax.dynamic_slice(s_cw, (0, chip * (CONV // N_CHIP)), (4, CONV // N_CHIP))

    g_in_full = _restore_w_in(big["w_in_r"])
    send_in = jnp.transpose(g_in_full.reshape(D, N_CHIP, W_IN // N_CHIP), (1, 0, 2)).astype(BF16)
    send_rest = jnp.concatenate([
        big["w_bg"].reshape(N_CHIP, r_bg, D), big["w_bs"].reshape(N_CHIP, r_bs, D), big["w_out"].reshape(N_CHIP, r_out, D),
        big["w_ff2"].reshape(N_CHIP, r_ff2, D), jnp.transpose(big["w_ff1"].reshape(D, N_CHIP, D), (1, 0, 2))], axis=1).astype(BF16)
    mines, theirs = _exchange_halves([send_in, send_rest])
    pair = [_sum_leading((mines[t], theirs[t]), f"pair_sum_{t}", out_dtype=BF16) for t in range(2)]
    contrib = _exchange_chips(pair)
    halves = [_sum_leading(contrib[t], f"chip_sum_{t}") for t in range(2)]
    g_w_in, g_rest_sh = _share_halves(halves)
    g_w_bg, g_w_bs, g_w_out, g_w_ff2, g_w_ff1 = g_rest_sh[:o1], g_rest_sh[o1:o2], g_rest_sh[o2:o3], g_rest_sh[o3:o4], g_rest_sh[o4:]

    def adam_big(w, g, m, v, name):
        d, m2, v2 = _adam(w.reshape(g.shape), g, m.reshape(g.shape), v.reshape(g.shape), name)
        return g.reshape(w.shape), d.reshape(w.shape), m2.reshape(w.shape), v2.reshape(w.shape)

    res = {}
    res["w_mod"] = adam_big(w_mod, g_w_mod, m_w_mod, v_w_mod, "adam_w_mod")
    res["w_in"] = adam_big(w_in, g_w_in, m_w_in, v_w_in, "adam_w_in")
    res["w_branch_gm"] = adam_big(w_branch_gm, g_w_bg, m_w_branch_gm, v_w_branch_gm, "adam_w_branch_gm")
    res["w_branch_ssm"] = adam_big(w_branch_ssm, g_w_bs, m_w_branch_ssm, v_w_branch_ssm, "adam_w_branch_ssm")
    res["w_out"] = adam_big(w_out, g_w_out, m_w_out, v_w_out, "adam_w_out")
    res["w_ff1"] = adam_big(w_ff1, g_w_ff1, m_w_ff1, v_w_ff1, "adam_w_ff1")
    res["w_ff2"] = adam_big(w_ff2, g_w_ff2, m_w_ff2, v_w_ff2, "adam_w_ff2")

    names = ("b_mod", "gm_norm_w", "gm_ws", "gm_bs", "conv_w", "conv_b", "dt_bias", "a_log", "d_skip", "ssm_norm_w", "final_norm_w")
    ws = (b_mod, gm_norm_w, gm_ws, gm_bs, conv_w, conv_b, dt_bias, a_log, d_skip, ssm_norm_w, final_norm_w)
    ms = (m_b_mod, m_gm_norm_w, m_gm_ws, m_gm_bs, m_conv_w, m_conv_b, m_dt_bias, m_a_log, m_d_skip, m_ssm_norm_w, m_final_norm_w)
    vs = (v_b_mod, v_gm_norm_w, v_gm_ws, v_gm_bs, v_conv_w, v_conv_b, v_dt_bias, v_a_log, v_d_skip, v_ssm_norm_w, v_final_norm_w)
    gs = (s_bmod, s_gnw, s_ws, s_bs, g_conv_w, s_cb, s_dtb, s_alog, s_dsk, s_snw, s_fnw)
    gs = [g.reshape(w.shape) for g, w in zip(gs, ws)]
    shapes = [w.shape for w in ws]
    d_p, m_p, v_p = _adam(_pack(ws), _pack(gs), _pack(ms), _pack(vs), "adam_small")
    for name, g, d, m2, v2 in zip(names, gs, _unpack(d_p, shapes), _unpack(m_p, shapes), _unpack(v_p, shapes)):
        res[name] = (g, d, m2, v2)

    order = ("w_mod", "b_mod", "w_in", "gm_norm_w", "gm_ws", "gm_bs", "conv_w", "conv_b", "dt_bias", "a_log", "d_skip",
             "ssm_norm_w", "w_branch_gm", "w_branch_ssm", "w_out", "w_ff1", "w_ff2", "final_norm_w")
    loss = s_loss.reshape(())
    return (loss, grad_x.reshape(x.shape), *[res[k][0] for k in order], *[res[k][1] for k in order],
            *[res[k][2] for k in order], *[res[k][3] for k in order])
```

```python
import functools

import jax
import jax.numpy as jnp
from jax import lax
from jax.experimental import pallas as pl
from jax.experimental.pallas import tpu as pltpu

F32 = jnp.float32
BF16 = jnp.bfloat16
MESH = pl.DeviceIdType.MESH
HIGHEST = lax.Precision.HIGHEST

D = 1024
EPS = 1e-6
Q = 128
GM_G = 8
NHEAD = 32
NGRP = 8
DI = 2048
CONV = 4096
DFF = 4096
W_IN = 10272
W_IN_R = 10368
OFF_Z, OFF_XBC, OFF_GA, OFF_DT = 2048, 4096, 8192, 10240
N_CHIP = 4
N_DEV = 8
AG_CHUNKS = 4
D2D_SPLIT = 4

ADAM_LR, ADAM_B1, ADAM_B2, ADAM_EPS, ADAM_WD, ADAM_STEP = 0.001, 0.9, 0.999, 1e-08, 0.01, 10

ANY = pl.BlockSpec(memory_space=pl.ANY)
VMEM = pl.BlockSpec(memory_space=pltpu.VMEM)


def _full(shape):
    return pl.BlockSpec(shape, lambda *_: (0,) * len(shape))


def _pick(n, prefs):
    for p in prefs:
        if n % p == 0:
            return p
    return n


def _rms(x):
    return x * lax.rsqrt(jnp.mean(x * x, axis=-1, keepdims=True) + EPS)


def _modnorm(x, sc, sh):
    return _rms(x) * (1.0 + sc) + sh


def _sgu_pre(u, v, w):
    return jax.nn.gelu(u), _rms(jax.nn.gelu(v)) * w


def _gatenorm(y, z, w):
    g = y * jax.nn.silu(z)
    return _rms(g) * w


def _mix(ga, gb, ba, bb):
    return jax.nn.sigmoid(ga) * ba + jax.nn.sigmoid(gb) * bb


def _loss_tile(x1, fo, g2, fw, t):
    x2 = x1 + g2 * fo
    y = _rms(x2) * fw
    err = jnp.square(y - t)
    return 0.5 * jnp.sum(jnp.mean(err, axis=-1))


def _tril(n):
    r = lax.broadcasted_iota(jnp.int32, (n, n), 0)
    c = lax.broadcasted_iota(jnp.int32, (n, n), 1)
    return r >= c


def _dt_prep(dtr, dtb, alog):
    dt = jax.nn.softplus(dtr + dtb)
    a = dt * (-jnp.exp(alog))
    ones = _tril(Q).astype(F32)
    cum = jnp.dot(ones, a, precision=HIGHEST, preferred_element_type=F32)
    cum_t = lax.dot_general(a, ones, (((0,), (1,)), ((), ())), precision=HIGHEST, preferred_element_type=F32)
    return dt, cum, cum_t


def _ssd_group(x0, x1, s0, s1, bm, cm, dt, cum, cum_t, dsk, grp):
    lane = lax.broadcasted_iota(jnp.int32, (1, 128), 1)
    sub = lax.broadcasted_iota(jnp.int32, (128, 1), 0)
    causal = _tril(Q)
    half = lane < 64
    half_rows = sub < 64
    bmb = bm.astype(BF16)
    cmb = cm.astype(BF16)
    cb = lax.dot_general(cmb, bmb, (((1,), (1,)), ((), ())), preferred_element_type=F32)

    def col(v, h):
        return jnp.sum(jnp.where(lane == h, v, 0.0), axis=1, keepdims=True)

    def row(v, h):
        return jnp.sum(jnp.where(sub == h, v, 0.0), axis=0, keepdims=True)

    def last(c):
        return jnp.sum(jnp.where(sub == Q - 1, c, 0.0), axis=0, keepdims=True)

    outs, states = [], []
    for p, (xp, sp) in enumerate(((x0, s0), (x1, s1))):
        h_a = 4 * grp + 2 * p
        h_b = h_a + 1
        dt_a, dt_b = col(dt, h_a), col(dt, h_b)
        cum_a, cum_b = col(cum, h_a), col(cum, h_b)
        row_a, row_b = row(cum_t, h_a), row(cum_t, h_b)
        last_a, last_b = last(cum_a), last(cum_b)
        xdt = xp * jnp.where(half, dt_a, dt_b)
        xdtb = xdt.astype(BF16)
        m_a = (cb * jnp.exp(jnp.where(causal, cum_a - row_a, -jnp.inf))).astype(BF16)
        m_b = (cb * jnp.exp(jnp.where(causal, cum_b - row_b, -jnp.inf))).astype(BF16)
        y_intra = jnp.where(half, jnp.dot(m_a, xdtb, preferred_element_type=F32),
                            jnp.dot(m_b, xdtb, preferred_element_type=F32))
        y_inter = lax.dot_general(cmb, sp.astype(BF16), (((1,), (1,)), ((), ())), preferred_element_type=F32)
        y_inter = y_inter * jnp.where(half, jnp.exp(cum_a), jnp.exp(cum_b))
        w_end = jnp.where(half, jnp.exp(last_a - cum_a), jnp.exp(last_b - cum_b))
        upd = lax.dot_general((xdt * w_end).astype(BF16), bmb, (((0,), (0,)), ((), ())), preferred_element_type=F32)
        states.append(sp * jnp.where(half_rows, jnp.exp(last_a), jnp.exp(last_b)) + upd)
        outs.append(y_intra + y_inter + xp * jnp.where(half, col(dsk, h_a), col(dsk, h_b)))
    return outs[0], outs[1], states[0], states[1]


def _matmul(a, b, mode, *, name, out_dtypes=(F32,), epi=None, epi_ins=(), tm=1024, tn=1024, tk=1024):
    if mode == "nn":
        (m, k), n = a.shape, b.shape[1]
    elif mode == "nt":
        (m, k), n = a.shape, b.shape[0]
    else:
        (k, m), n = a.shape, b.shape[1]
    tm, tn, tk = _pick(m, (tm, 512, 256, 128)), _pick(n, (tn, 1152, 1024, 512, 384, 256, 128)), _pick(k, (tk, 1152, 1024, 512, 256, 128))
    nk = k // tk
    if mode == "nn":
        a_spec = pl.BlockSpec((tm, tk), lambda i, j, kk: (i, kk))
        b_spec = pl.BlockSpec((tk, tn), lambda i, j, kk: (kk, j))
        dims = (((1,), (0,)), ((), ()))
    elif mode == "nt":
        a_spec = pl.BlockSpec((tm, tk), lambda i, j, kk: (i, kk))
        b_spec = pl.BlockSpec((tn, tk), lambda i, j, kk: (j, kk))
        dims = (((1,), (1,)), ((), ()))
    else:
        a_spec = pl.BlockSpec((tk, tm), lambda i, j, kk: (kk, i))
        b_spec = pl.BlockSpec((tk, tn), lambda i, j, kk: (kk, j))
        dims = (((0,), (0,)), ((), ()))
    o_spec = pl.BlockSpec((tm, tn), lambda i, j, kk: (i, j))
    n_epi, n_out = len(epi_ins), len(out_dtypes)

    def body(*refs):
        a_ref, b_ref = refs[0], refs[1]
        e_refs = refs[2:2 + n_epi]
        o_refs = refs[2 + n_epi:2 + n_epi + n_out]
        acc_ref = refs[-1]

        def finish(acc):
            outs = epi(acc, *[e[...] for e in e_refs]) if epi is not None else (acc,)
            for o_ref, val in zip(o_refs, outs):
                o_ref[...] = val.astype(o_ref.dtype)

        part = lax.dot_general(a_ref[...], b_ref[...], dims, preferred_element_type=F32)
        if nk == 1:
            finish(part)
        else:
            kk = pl.program_id(2)

            @pl.when(kk == 0)
            def _():
                acc_ref[...] = part

            @pl.when(kk > 0)
            def _():
                acc_ref[...] += part

            @pl.when(kk == nk - 1)
            def _():
                finish(acc_ref[...])

    return pl.pallas_call(
        body, name=name, grid=(m // tm, n // tn, nk),
        in_specs=[a_spec, b_spec] + [o_spec] * n_epi,
        out_specs=[o_spec] * n_out,
        out_shape=[jax.ShapeDtypeStruct((m, n), dt) for dt in out_dtypes],
        scratch_shapes=[pltpu.VMEM((tm, tn) if nk > 1 else (8, 128), F32)],
        compiler_params=pltpu.CompilerParams(dimension_semantics=("parallel", "parallel", "arbitrary")),
    )(a, b, *epi_ins)


def _row_tile(s):
    return _pick(s, (512, 256, 128))


def _prenorm(x, sc, sh):
    s = x.shape[0]
    tm = _row_tile(s)

    def body(x_ref, sc_ref, sh_ref, h_ref):
        h_ref[...] = _modnorm(x_ref[...], sc_ref[...], sh_ref[...]).astype(BF16)

    row = pl.BlockSpec((tm, D), lambda i: (i, 0))
    return pl.pallas_call(body, name="prenorm", grid=(s // tm,), in_specs=[row, _full((1, D)), _full((1, D))],
                          out_specs=row, out_shape=jax.ShapeDtypeStruct((s, D), BF16))(x, sc, sh)


def _prenorm_bwd(x, sc, sh, dh, dx_res):
    s = x.shape[0]
    tm = _row_tile(s)

    def body(x_ref, sc_ref, sh_ref, dh_ref, dr_ref, dx_ref, dsc_ref, dsh_ref):
        _, vjp = jax.vjp(_modnorm, x_ref[...], sc_ref[...], sh_ref[...])
        dx, dsc, dsh = vjp(dh_ref[...])
        dx_ref[...] = dr_ref[...] + dx

        @pl.when(pl.program_id(0) == 0)
        def _():
            dsc_ref[...] = jnp.zeros_like(dsc_ref)
            dsh_ref[...] = jnp.zeros_like(dsh_ref)

        dsc_ref[...] += dsc
        dsh_ref[...] += dsh

    row = pl.BlockSpec((tm, D), lambda i: (i, 0))
    vec = _full((1, D))
    return pl.pallas_call(
        body, name="prenorm_bwd", grid=(s // tm,), in_specs=[row, vec, vec, row, row], out_specs=[row, vec, vec],
        out_shape=[jax.ShapeDtypeStruct((s, D), F32), jax.ShapeDtypeStruct((1, D), F32), jax.ShapeDtypeStruct((1, D), F32)],
    )(x, sc, sh, dh, dx_res)


def _resid_norm(x, mo, g1, sc, sh):
    s = x.shape[0]
    tm = _row_tile(s)

    def body(x_ref, mo_ref, g_ref, sc_ref, sh_ref, x1_ref, h_ref):
        x1 = x_ref[...] + g_ref[...] * mo_ref[...]
        x1_ref[...] = x1
        h_ref[...] = _modnorm(x1, sc_ref[...], sh_ref[...]).astype(BF16)

    row = pl.BlockSpec((tm, D), lambda i: (i, 0))
    vec = _full((1, D))
    return pl.pallas_call(
        body, name="resid_norm", grid=(s // tm,), in_specs=[row, row, vec, vec, vec], out_specs=[row, row],
        out_shape=[jax.ShapeDtypeStruct((s, D), F32), jax.ShapeDtypeStruct((s, D), BF16)],
    )(x, mo, g1, sc, sh)


def _resid_norm_bwd(x1, mo, g1, sc, sh, dh, dx2):
    s = x1.shape[0]
    tm = _row_tile(s)

    def body(x1_ref, mo_ref, g_ref, sc_ref, sh_ref, dh_ref, dx2_ref, dx1_ref, dmo_ref, dg_ref, dsc_ref, dsh_ref):
        _, vjp = jax.vjp(_modnorm, x1_ref[...], sc_ref[...], sh_ref[...])
        dx, dsc, dsh = vjp(dh_ref[...])
        dx1 = dx2_ref[...] + dx
        dx1_ref[...] = dx1
        dmo_ref[...] = (dx1 * g_ref[...]).astype(BF16)

        @pl.when(pl.program_id(0) == 0)
        def _():
            dg_ref[...] = jnp.zeros_like(dg_ref)
            dsc_ref[...] = jnp.zeros_like(dsc_ref)
            dsh_ref[...] = jnp.zeros_like(dsh_ref)

        dg_ref[...] += jnp.sum(dx1 * mo_ref[...], axis=0, keepdims=True)
        dsc_ref[...] += dsc
        dsh_ref[...] += dsh

    row = pl.BlockSpec((tm, D), lambda i: (i, 0))
    vec = _full((1, D))
    vshape = jax.ShapeDtypeStruct((1, D), F32)
    return pl.pallas_call(
        body, name="resid_norm_bwd", grid=(s // tm,), in_specs=[row, row, vec, vec, vec, row, row],
        out_specs=[row, row, vec, vec, vec],
        out_shape=[jax.ShapeDtypeStruct((s, D), F32), jax.ShapeDtypeStruct((s, D), BF16), vshape, vshape, vshape],
    )(x1, mo, g1, sc, sh, dh, dx2)


def _loss_head(x1, fo, g2, fw, t):
    s = x1.shape[0]
    tm = _row_tile(s)

    def body(x1_ref, fo_ref, g_ref, fw_ref, t_ref, loss_ref, dx_ref, dfo_ref, dg_ref, dfw_ref):
        loss, (dx1, dfo, dg, dfw) = jax.value_and_grad(_loss_tile, argnums=(0, 1, 2, 3))(
            x1_ref[...], fo_ref[...], g_ref[...], fw_ref[...], t_ref[...])
        dx_ref[...] = dx1
        dfo_ref[...] = dfo.astype(BF16)

        @pl.when(pl.program_id(0) == 0)
        def _():
            loss_ref[...] = jnp.zeros_like(loss_ref)
            dg_ref[...] = jnp.zeros_like(dg_ref)
            dfw_ref[...] = jnp.zeros_like(dfw_ref)

        loss_ref[...] += jnp.full(loss_ref.shape, loss, F32)
        dg_ref[...] += dg
        dfw_ref[...] += dfw

    row = pl.BlockSpec((tm, D), lambda i: (i, 0))
    vec = _full((1, D))
    vshape = jax.ShapeDtypeStruct((1, D), F32)
    return pl.pallas_call(
        body, name="loss_head", grid=(s // tm,), in_specs=[row, row, vec, vec, row],
        out_specs=[_full((8, 128)), row, row, vec, vec],
        out_shape=[jax.ShapeDtypeStruct((8, 128), F32), jax.ShapeDtypeStruct((s, D), F32),
                   jax.ShapeDtypeStruct((s, D), BF16), vshape, vshape],
    )(x1, fo, g2, fw, t)


def _sgu_fwd(proj, norm_w, ws_b, bs_t):
    s = proj.shape[0]
    tm = _pick(s, (256, 128))

    def body(u_ref, v_ref, w_ref, ws_ref, bs_ref, y_ref):
        ug, vn = _sgu_pre(u_ref[...], v_ref[...], w_ref[...])
        vnb = vn.astype(BF16)
        for c in range(tm // Q):
            r = slice(c * Q, (c + 1) * Q)
            for g in range(GM_G):
                cs = slice(g * 128, (g + 1) * 128)
                sv = jnp.dot(ws_ref[g], vnb[r, cs], preferred_element_type=F32) + bs_ref[:, g:g + 1]
                y_ref[r, cs] = (ug[r, cs] * sv).astype(BF16)

    return pl.pallas_call(
        body, name="sgu_fwd", grid=(s // tm,),
        in_specs=[pl.BlockSpec((tm, D), lambda i: (i, 0)), pl.BlockSpec((tm, D), lambda i: (i, 1)),
                  _full((1, D)), _full((GM_G, Q, Q)), _full((Q, GM_G))],
        out_specs=pl.BlockSpec((tm, D), lambda i: (i, 0)),
        out_shape=jax.ShapeDtypeStruct((s, D), BF16),
    )(proj, proj, norm_w, ws_b, bs_t)


def _sgu_bwd(proj, norm_w, ws_b, bs_t, dy, dproj):
    s = proj.shape[0]
    tm = _pick(s, (256, 128))

    def body(u_ref, v_ref, w_ref, ws_ref, bs_ref, dy_ref, _, duv_ref, dw_ref, dws_ref, dbs_ref, dug_scr, dvn_scr):
        @pl.when(pl.program_id(0) == 0)
        def _():
            dw_ref[...] = jnp.zeros_like(dw_ref)
            dws_ref[...] = jnp.zeros_like(dws_ref)
            dbs_ref[...] = jnp.zeros_like(dbs_ref)

        (ug, vn), vjp = jax.vjp(_sgu_pre, u_ref[...], v_ref[...], w_ref[...])
        vnb = vn.astype(BF16)
        dy = dy_ref[...]
        causal = _tril(Q).astype(F32)
        for c in range(tm // Q):
            r = slice(c * Q, (c + 1) * Q)
            for g in range(GM_G):
                cs = slice(g * 128, (g + 1) * 128)
                blk = vnb[r, cs]
                sv = jnp.dot(ws_ref[g], blk, preferred_element_type=F32) + bs_ref[:, g:g + 1]
                dug_scr[r, cs] = dy[r, cs] * sv
                dsv = dy[r, cs] * ug[r, cs]
                dsvb = dsv.astype(BF16)
                dws_ref[g] += causal * lax.dot_general(dsvb, blk, (((1,), (1,)), ((), ())), preferred_element_type=F32)
                dbs_ref[:, g:g + 1] += jnp.sum(dsv, axis=1, keepdims=True)
                dvn_scr[r, cs] = lax.dot_general(ws_ref[g], dsvb, (((0,), (0,)), ((), ())), preferred_element_type=F32)
        du, dv, dw = vjp((dug_scr[...], dvn_scr[...]))
        duv_ref[:, :D] = du.astype(BF16)
        duv_ref[:, D:] = dv.astype(BF16)
        dw_ref[...] += dw

    return pl.pallas_call(
        body, name="sgu_bwd", grid=(s // tm,),
        in_specs=[pl.BlockSpec((tm, D), lambda i: (i, 0)), pl.BlockSpec((tm, D), lambda i: (i, 1)),
                  _full((1, D)), _full((GM_G, Q, Q)), _full((Q, GM_G)), pl.BlockSpec((tm, D), lambda i: (i, 0)), ANY],
        out_specs=[pl.BlockSpec((tm, 2 * D), lambda i: (i, 0)), _full((1, D)), _full((GM_G, Q, Q)), _full((Q, GM_G))],
        out_shape=[jax.ShapeDtypeStruct(dproj.shape, BF16), jax.ShapeDtypeStruct((1, D), F32),
                   jax.ShapeDtypeStruct((GM_G, Q, Q), F32), jax.ShapeDtypeStruct((Q, GM_G), F32)],
        scratch_shapes=[pltpu.VMEM((tm, D), F32), pltpu.VMEM((tm, D), F32)],
        input_output_aliases={6: 0},
    )(proj, proj, norm_w, ws_b, bs_t, dy, dproj)


def _gatenorm_fwd(y_ssd, proj, norm_w):
    s = y_ssd.shape[0]
    tm = _pick(s, (256, 128))
    gw = DI // NGRP

    def body(y_ref, z_ref, w_ref, o_ref):
        for g in range(NGRP):
            cs = slice(g * gw, (g + 1) * gw)
            o_ref[:, cs] = _gatenorm(y_ref[:, cs], z_ref[:, cs], w_ref[:, cs]).astype(BF16)

    return pl.pallas_call(
        body, name="gatenorm_fwd", grid=(s // tm,),
        in_specs=[pl.BlockSpec((tm, DI), lambda i: (i, 0)), pl.BlockSpec((tm, DI), lambda i: (i, OFF_Z // DI)), _full((1, DI))],
        out_specs=pl.BlockSpec((tm, DI), lambda i: (i, 0)),
        out_shape=jax.ShapeDtypeStruct((s, DI), BF16),
    )(y_ssd, proj, norm_w)


def _gatenorm_bwd(y_ssd, proj, norm_w, dyb, dproj):
    s = y_ssd.shape[0]
    tm = _pick(s, (256, 128))
    gw = DI // NGRP

    def body(y_ref, z_ref, w_ref, d_ref, _, dy_ref, dz_ref, dw_ref):
        @pl.when(pl.program_id(0) == 0)
        def _():
            dw_ref[...] = jnp.zeros_like(dw_ref)

        for g in range(NGRP):
            cs = slice(g * gw, (g + 1) * gw)
            _, vjp = jax.vjp(_gatenorm, y_ref[:, cs], z_ref[:, cs], w_ref[:, cs])
            dy, dz, dw = vjp(d_ref[:, cs])
            dy_ref[:, cs] = dy
            dz_ref[:, cs] = dz.astype(BF16)
            dw_ref[:, cs] += dw

    blk = pl.BlockSpec((tm, DI), lambda i: (i, 0))
    zblk = pl.BlockSpec((tm, DI), lambda i: (i, OFF_Z // DI))
    return pl.pallas_call(
        body, name="gatenorm_bwd", grid=(s // tm,),
        in_specs=[blk, zblk, _full((1, DI)), blk, ANY],
        out_specs=[blk, zblk, _full((1, DI))],
        out_shape=[jax.ShapeDtypeStruct((s, DI), F32), jax.ShapeDtypeStruct(dproj.shape, BF16), jax.ShapeDtypeStruct((1, DI), F32)],
        input_output_aliases={4: 1},
    )(y_ssd, proj, norm_w, dyb, dproj)


def _mix_fwd(proj, ba, bb):
    s = proj.shape[0]
    tm = _row_tile(s)
    gb0 = OFF_GA // D

    def body(ga_ref, gb_ref, ba_ref, bb_ref, o_ref):
        o_ref[...] = _mix(ga_ref[...], gb_ref[...], ba_ref[...], bb_ref[...]).astype(BF16)

    row = pl.BlockSpec((tm, D), lambda i: (i, 0))
    return pl.pallas_call(
        body, name="mix_fwd", grid=(s // tm,),
        in_specs=[pl.BlockSpec((tm, D), lambda i: (i, gb0)), pl.BlockSpec((tm, D), lambda i: (i, gb0 + 1)), row, row],
        out_specs=row, out_shape=jax.ShapeDtypeStruct((s, D), BF16),
    )(proj, proj, ba, bb)


def _mix_bwd(proj, ba, bb, dmixed):
    s = proj.shape[0]
    tm = _row_tile(s)
    gb0 = OFF_GA // D

    def body(ga_ref, gb_ref, ba_ref, bb_ref, d_ref, dg_ref, dba_ref, dbb_ref):
        _, vjp = jax.vjp(_mix, ga_ref[...], gb_ref[...], ba_ref[...], bb_ref[...])
        dga, dgb, dba, dbb = vjp(d_ref[...])
        dg_ref[:, :D] = dga.astype(BF16)
        dg_ref[:, D:] = dgb.astype(BF16)
        dba_ref[...] = dba.astype(BF16)
        dbb_ref[...] = dbb.astype(BF16)

    row = pl.BlockSpec((tm, D), lambda i: (i, 0))
    return pl.pallas_call(
        body, name="mix_bwd", grid=(s // tm,),
        in_specs=[pl.BlockSpec((tm, D), lambda i: (i, gb0)), pl.BlockSpec((tm, D), lambda i: (i, gb0 + 1)), row, row, row],
        out_specs=[pl.BlockSpec((tm, 2 * D), lambda i: (i, OFF_GA // (2 * D))), row, row],
        out_shape=[jax.ShapeDtypeStruct((s, W_IN_R), BF16), jax.ShapeDtypeStruct((s, D), BF16), jax.ShapeDtypeStruct((s, D), BF16)],
    )(proj, proj, ba, bb, dmixed)


CONV_TC = 1024


def _conv_fwd(proj, conv_w, conv_b):
    s = proj.shape[0]
    tm = _row_tile(s)
    cb0 = OFF_XBC // CONV_TC

    def body(x_ref, halo_ref, w_ref, b_ref, o_ref):
        halo = jnp.where(pl.program_id(0) > 0, halo_ref[...], 0.0)
        ext = jnp.concatenate([halo, x_ref[...]], axis=0)
        acc = jnp.broadcast_to(b_ref[...], (tm, CONV_TC))
        for k in range(4):
            shifted = ext if k == 3 else pltpu.roll(ext, 3 - k, 0)
            acc = acc + w_ref[k:k + 1, :] * shifted[8:, :]
        o_ref[...] = jax.nn.silu(acc)

    return pl.pallas_call(
        body, name="conv_fwd", grid=(s // tm, CONV // CONV_TC),
        in_specs=[pl.BlockSpec((tm, CONV_TC), lambda i, j: (i, cb0 + j)),
                  pl.BlockSpec((8, CONV_TC), lambda i, j: (jnp.maximum(i * (tm // 8) - 1, 0), cb0 + j)),
                  pl.BlockSpec((4, CONV_TC), lambda i, j: (0, j)), pl.BlockSpec((1, CONV_TC), lambda i, j: (0, j))],
        out_specs=pl.BlockSpec((tm, CONV_TC), lambda i, j: (i, j)),
        out_shape=jax.ShapeDtypeStruct((s, CONV), F32),
    )(proj, proj, conv_w, conv_b)


def _conv_bwd(proj, conv_w, conv_b, dact, dproj, col0, name):
    s, width = dact.shape
    tm = _row_tile(s)
    nt = s // tm
    c0 = col0 // CONV_TC
    cb0 = OFF_XBC // CONV_TC + c0

    def body(x_ref, prev_ref, next_ref, d_ref, dnext_ref, w_ref, b_ref, _, dx_ref, dw_ref, db_ref):
        i = pl.program_id(1)
        prev = jnp.where(i > 0, prev_ref[...], 0.0)
        ext = jnp.concatenate([prev, x_ref[...], next_ref[...]], axis=0)
        dext = jnp.concatenate([d_ref[...], jnp.where(i < nt - 1, dnext_ref[...], 0.0)], axis=0)
        pre = jnp.broadcast_to(b_ref[...], (tm + 8, CONV_TC))
        taps = []
        for k in range(4):
            shifted = (ext if k == 3 else pltpu.roll(ext, 3 - k, 0))[8:, :]
            taps.append(shifted)
            pre = pre + w_ref[k:k + 1, :] * shifted
        sig = jax.nn.sigmoid(pre)
        dpre = dext * (sig * (1.0 + pre * (1.0 - sig)))
        dx = jnp.zeros((tm, CONV_TC), F32)
        for k in range(4):
            shifted = dpre if k == 3 else pltpu.roll(dpre, tm + 8 - (3 - k), 0)
            dx = dx + w_ref[k:k + 1, :] * shifted[:tm, :]
        dx_ref[...] = dx.astype(BF16)

        @pl.when(i == 0)
        def _():
            dw_ref[...] = jnp.zeros_like(dw_ref)
            db_ref[...] = jnp.zeros_like(db_ref)

        dtile = dpre[:tm, :]
        for k in range(4):
            dw_ref[k:k + 1, :] += jnp.sum(dtile * taps[k][:tm, :], axis=0, keepdims=True)
        db_ref[...] += jnp.sum(dtile, axis=0, keepdims=True)

    r8 = tm // 8
    return pl.pallas_call(
        body, name=name, grid=(width // CONV_TC, nt),
        in_specs=[pl.BlockSpec((tm, CONV_TC), lambda j, i: (i, cb0 + j)),
                  pl.BlockSpec((8, CONV_TC), lambda j, i: (jnp.maximum(i * r8 - 1, 0), cb0 + j)),
                  pl.BlockSpec((8, CONV_TC), lambda j, i: (jnp.minimum((i + 1) * r8, nt * r8 - 1), cb0 + j)),
                  pl.BlockSpec((tm, CONV_TC), lambda j, i: (i, j)),
                  pl.BlockSpec((8, CONV_TC), lambda j, i: (jnp.minimum((i + 1) * r8, nt * r8 - 1), j)),
                  pl.BlockSpec((4, CONV_TC), lambda j, i: (0, c0 + j)), pl.BlockSpec((1, CONV_TC), lambda j, i: (0, c0 + j)), ANY],
        out_specs=[pl.BlockSpec((tm, CONV_TC), lambda j, i: (i, cb0 + j)),
                   pl.BlockSpec((4, CONV_TC), lambda j, i: (0, j)), pl.BlockSpec((1, CONV_TC), lambda j, i: (0, j))],
        out_shape=[jax.ShapeDtypeStruct(dproj.shape, BF16), jax.ShapeDtypeStruct((4, width), F32), jax.ShapeDtypeStruct((1, width), F32)],
        input_output_aliases={7: 0},
    )(proj, proj, proj, dact, dact, conv_w, conv_b, dproj)


def _dt_fwd(proj, dtb, alog):
    s = proj.shape[0]
    nc = s // Q

    def body(r_ref, b_ref, a_ref, dt_ref, cum_ref, cumt_ref):
        dt, cum, cum_t = _dt_prep(r_ref[...], b_ref[...], a_ref[...])
        dt_ref[...] = dt
        cum_ref[...] = cum
        cumt_ref[...] = cum_t

    blk = pl.BlockSpec((Q, 128), lambda n: (n, 0))
    return pl.pallas_call(
        body, name="dt_fwd", grid=(nc,),
        in_specs=[pl.BlockSpec((Q, 128), lambda n: (n, OFF_DT // 128)), _full((1, 128)), _full((1, 128))],
        out_specs=[blk, blk, pl.BlockSpec((None, 128, Q), lambda n: (n, 0, 0))],
        out_shape=[jax.ShapeDtypeStruct((s, 128), F32), jax.ShapeDtypeStruct((s, 128), F32), jax.ShapeDtypeStruct((nc, 128, Q), F32)],
    )(proj, dtb, alog)


def _dt_bwd(proj, dtb, alog, ddt, dcum, dcumt, dproj):
    s = proj.shape[0]
    nc = s // Q

    def body(r_ref, b_ref, a_ref, ddt_ref, dcum_ref, dcumt_ref, _, dr_ref, db_ref, da_ref):
        _, vjp = jax.vjp(_dt_prep, r_ref[...], b_ref[...], a_ref[...])
        dr, db, da = vjp((ddt_ref[...], dcum_ref[...], dcumt_ref[...]))
        dr_ref[...] = dr.astype(BF16)

        @pl.when(pl.program_id(0) == 0)
        def _():
            db_ref[...] = jnp.zeros_like(db_ref)
            da_ref[...] = jnp.zeros_like(da_ref)

        db_ref[...] += db
        da_ref[...] += da

    blk = pl.BlockSpec((Q, 128), lambda n: (n, 0))
    pblk = pl.BlockSpec((Q, 128), lambda n: (n, OFF_DT // 128))
    return pl.pallas_call(
        body, name="dt_bwd", grid=(nc,),
        in_specs=[pblk, _full((1, 128)), _full((1, 128)), blk, blk, pl.BlockSpec((None, 128, Q), lambda n: (n, 0, 0)), ANY],
        out_specs=[pblk, _full((1, 128)), _full((1, 128))],
        out_shape=[jax.ShapeDtypeStruct(dproj.shape, BF16), jax.ShapeDtypeStruct((1, 128), F32), jax.ShapeDtypeStruct((1, 128), F32)],
        input_output_aliases={6: 0},
    )(proj, dtb, alog, ddt, dcum, dcumt, dproj)


def _ssd_specs(chunk_of):
    xs = pl.BlockSpec((Q, 256), lambda n, g: (chunk_of(n), g))
    bm = pl.BlockSpec((Q, 128), lambda n, g: (chunk_of(n), DI // 128 + g))
    cm = pl.BlockSpec((Q, 128), lambda n, g: (chunk_of(n), DI // 128 + NGRP + g))
    per_chunk = pl.BlockSpec((Q, 128), lambda n, g: (chunk_of(n), 0))
    cum_t = pl.BlockSpec((None, 128, Q), lambda n, g: (chunk_of(n), 0, 0))
    state = pl.BlockSpec((None, 256, 128), lambda n, g: (chunk_of(n), g, 0))
    vec = pl.BlockSpec((1, 128), lambda n, g: (0, 0))
    return xs, bm, cm, per_chunk, cum_t, state, vec


def _ssd_fwd(xbc, dt, cum, cum_t, dsk):
    s = xbc.shape[0]
    nc = s // Q
    xs, bm, cm, per_chunk, cumt_spec, state_spec, vec = _ssd_specs(lambda n: n)

    def body(x_ref, b_ref, c_ref, dt_ref, cum_ref, cumt_ref, dsk_ref, y_ref, st_ref, carry):
        n, g = pl.program_id(0), pl.program_id(1)
        rows = pl.ds(pl.multiple_of(g * 256, 256), 256)

        @pl.when(n == 0)
        def _():
            carry[rows, :] = jnp.zeros((256, 128), F32)

        s_in = carry[rows, :]
        st_ref[...] = s_in
        y0, y1, n0, n1 = _ssd_group(x_ref[:, :128], x_ref[:, 128:], s_in[:128], s_in[128:], b_ref[...], c_ref[...],
                                    dt_ref[...], cum_ref[...], cumt_ref[...], dsk_ref[...], g)
        y_ref[:, :128] = y0
        y_ref[:, 128:] = y1
        carry[pl.ds(pl.multiple_of(g * 256, 256), 128), :] = n0
        carry[pl.ds(pl.multiple_of(g * 256 + 128, 128), 128), :] = n1

    return pl.pallas_call(
        body, name="ssd_fwd", grid=(nc, NGRP),
        in_specs=[xs, bm, cm, per_chunk, per_chunk, cumt_spec, vec],
        out_specs=[xs, state_spec],
        out_shape=[jax.ShapeDtypeStruct((s, DI), F32), jax.ShapeDtypeStruct((nc, DI, 128), F32)],
        scratch_shapes=[pltpu.VMEM((DI, 128), F32)],
    )(xbc, xbc, xbc, dt, cum, cum_t, dsk)


def _ssd_bwd(xbc, dt, cum, cum_t, dsk, states, dy):
    s = xbc.shape[0]
    nc = s // Q
    xs, bm, cm, per_chunk, cumt_spec, state_spec, vec = _ssd_specs(lambda n: nc - 1 - n)

    def body(x_ref, b_ref, c_ref, dt_ref, cum_ref, cumt_ref, dsk_ref, st_ref, dy_ref,
             dx_ref, db_ref, dc_ref, ddt_ref, dcum_ref, dcumt_ref, ddsk_ref, carry):
        n, g = pl.program_id(0), pl.program_id(1)
        rows = pl.ds(pl.multiple_of(g * 256, 256), 256)

        @pl.when(n == 0)
        def _():
            carry[rows, :] = jnp.zeros((256, 128), F32)

        @pl.when((n == 0) & (g == 0))
        def _():
            ddsk_ref[...] = jnp.zeros_like(ddsk_ref)

        @pl.when(g == 0)
        def _():
            ddt_ref[...] = jnp.zeros_like(ddt_ref)
            dcum_ref[...] = jnp.zeros_like(dcum_ref)
            dcumt_ref[...] = jnp.zeros_like(dcumt_ref)

        s_in = st_ref[...]
        d_out = carry[rows, :]
        fn = functools.partial(_ssd_group, grp=g)
        _, vjp = jax.vjp(fn, x_ref[:, :128], x_ref[:, 128:], s_in[:128], s_in[128:], b_ref[...], c_ref[...],
                         dt_ref[...], cum_ref[...], cumt_ref[...], dsk_ref[...])
        dx0, dx1, ds0, ds1, dbm, dcm, ddt, dcum, dcumt, ddsk = vjp((dy_ref[:, :128], dy_ref[:, 128:], d_out[:128], d_out[128:]))
        dx_ref[:, :128] = dx0
        dx_ref[:, 128:] = dx1
        db_ref[...] = dbm
        dc_ref[...] = dcm
        ddt_ref[...] += ddt
        dcum_ref[...] += dcum
        dcumt_ref[...] += dcumt
        ddsk_ref[...] += ddsk
        carry[pl.ds(pl.multiple_of(g * 256, 256), 128), :] = ds0
        carry[pl.ds(pl.multiple_of(g * 256 + 128, 128), 128), :] = ds1

    grp_blk = pl.BlockSpec((Q, 128), lambda n, g: (nc - 1 - n, g))
    return pl.pallas_call(
        body, name="ssd_bwd", grid=(nc, NGRP),
        in_specs=[xs, bm, cm, per_chunk, per_chunk, cumt_spec, vec, state_spec, xs],
        out_specs=[xs, grp_blk, grp_blk, per_chunk, per_chunk, cumt_spec, vec],
        out_shape=[jax.ShapeDtypeStruct((s, DI), F32), jax.ShapeDtypeStruct((s, D), F32), jax.ShapeDtypeStruct((s, D), F32),
                   jax.ShapeDtypeStruct((s, 128), F32), jax.ShapeDtypeStruct((s, 128), F32),
                   jax.ShapeDtypeStruct((nc, 128, Q), F32), jax.ShapeDtypeStruct((1, 128), F32)],
        scratch_shapes=[pltpu.VMEM((DI, 128), F32)],
    )(xbc, xbc, xbc, dt, cum, cum_t, dsk, states, dy)


def _adam_math(w, g, m, v):
    m2 = ADAM_B1 * m + (1.0 - ADAM_B1) * g
    v2 = ADAM_B2 * v + (1.0 - ADAM_B2) * jnp.square(g)
    m_hat = m2 / (1.0 - ADAM_B1 ** ADAM_STEP)
    v_hat = v2 / (1.0 - ADAM_B2 ** ADAM_STEP)
    delta = -ADAM_LR * (m_hat / (jnp.sqrt(v_hat) + ADAM_EPS) + ADAM_WD * w)
    return delta, m2, v2


def _adam(w, g, m, v, name):
    r, c = w.shape
    tr = r if r * c * 4 <= (1 << 20) else _pick(r, (128, 64, 32, 16, 8))

    def body(w_ref, g_ref, m_ref, v_ref, d_ref, m2_ref, v2_ref):
        d, m2, v2 = _adam_math(w_ref[...], g_ref[...], m_ref[...], v_ref[...])
        d_ref[...] = d
        m2_ref[...] = m2
        v2_ref[...] = v2

    blk = pl.BlockSpec((tr, c), lambda i: (i, 0))
    shp = jax.ShapeDtypeStruct((r, c), F32)
    return pl.pallas_call(body, name=name, grid=(r // tr,), in_specs=[blk] * 4, out_specs=[blk] * 3,
                          out_shape=[shp] * 3)(w, g, m, v)


def _sum_leading(xs, name, out_dtype=F32, tr=256):
    if isinstance(xs, tuple):
        a, b = xs
        n, r, c = a.shape
        tr = _pick(r, (tr, 128, 64, 32, 16, 8))

        def body2(a_ref, b_ref, o_ref):
            o_ref[...] = (a_ref[...].astype(F32) + b_ref[...].astype(F32)).astype(out_dtype)

        blk = pl.BlockSpec((None, tr, c), lambda s, i: (s, i, 0))
        return pl.pallas_call(body2, name=name, grid=(n, r // tr), in_specs=[blk, blk], out_specs=blk,
                              out_shape=jax.ShapeDtypeStruct((n, r, c), out_dtype))(a, b)
    n, r, c = xs.shape
    tr = r if n * r * c * 4 <= (8 << 20) else _pick(r, (tr, 128, 64, 32, 16, 8))

    def body(x_ref, o_ref):
        acc = x_ref[0].astype(F32)
        for s in range(1, n):
            acc = acc + x_ref[s].astype(F32)
        o_ref[...] = acc.astype(out_dtype)

    return pl.pallas_call(body, name=name, grid=(r // tr,), in_specs=[pl.BlockSpec((n, tr, c), lambda i: (0, i, 0))],
                          out_specs=pl.BlockSpec((tr, c), lambda i: (i, 0)),
                          out_shape=jax.ShapeDtypeStruct((r, c), out_dtype))(xs)


def _mod_fwd(c8, w_mod, b_sl):
    def body(c_ref, w_ref, b_ref, o_ref):
        o_ref[...] = jnp.dot(jax.nn.silu(c_ref[...]), w_ref[...], precision=HIGHEST, preferred_element_type=F32) + b_ref[...]

    return pl.pallas_call(body, name="mod_fwd", in_specs=[VMEM, VMEM, VMEM], out_specs=VMEM,
                          out_shape=jax.ShapeDtypeStruct((N_DEV, w_mod.shape[1]), F32))(c8, w_mod, b_sl)


def _mod_wgrad(c8, dmod):
    def body(c_ref, d_ref, o_ref):
        o_ref[...] = lax.dot_general(jax.nn.silu(c_ref[...]), d_ref[...], (((0,), (0,)), ((), ())),
                                     precision=HIGHEST, preferred_element_type=F32)

    return pl.pallas_call(body, name="mod_wgrad", in_specs=[VMEM, VMEM], out_specs=VMEM,
                          out_shape=jax.ShapeDtypeStruct((D, dmod.shape[1]), F32))(c8, dmod)


def _place():
    x, y, c = lax.axis_index("x"), lax.axis_index("y"), lax.axis_index("c")
    chips = [(1 - x, y), (x, 1 - y), (1 - x, 1 - y)]
    return x, y, c, chips


def _all_gather_small(v, name):
    r, w = v.shape

    def body(v_ref, o_ref, send_sems, recv_sems, local_sem):
        x, y, c, _ = _place()
        me = 4 * x + 2 * y + c
        own = pltpu.make_async_copy(v_ref, o_ref.at[me], local_sem)
        own.start()
        sends = []
        for k in range(1, N_DEV):
            tx = 1 - x if k & 4 else x
            ty = 1 - y if k & 2 else y
            tc = 1 - c if k & 1 else c
            peer = 4 * tx + 2 * ty + tc
            cp = pltpu.make_async_remote_copy(src_ref=v_ref, dst_ref=o_ref.at[me], send_sem=send_sems.at[k - 1],
                                              recv_sem=recv_sems.at[k - 1], device_id=(tx, ty, tc), device_id_type=MESH)
            cp.start()
            sends.append((cp, peer, (tx, ty, tc)))
        for k, (cp, peer, dev) in enumerate(sends):
            pltpu.make_async_remote_copy(src_ref=v_ref, dst_ref=o_ref.at[peer], send_sem=send_sems.at[k],
                                         recv_sem=recv_sems.at[k], device_id=dev, device_id_type=MESH).wait_recv()
        for cp, _, _ in sends:
            cp.wait_send()
        own.wait()

    return pl.pallas_call(
        body, name=name, in_specs=[VMEM], out_specs=VMEM, out_shape=jax.ShapeDtypeStruct((N_DEV, r, w), v.dtype),
        scratch_shapes=[pltpu.SemaphoreType.DMA((N_DEV - 1,)), pltpu.SemaphoreType.DMA((N_DEV - 1,)), pltpu.SemaphoreType.DMA],
    )(v)


def _all_gather_weights(shards):
    n = len(shards)
    nsem = n * 3 * AG_CHUNKS

    def body(*refs):
        srcs, dsts = refs[:n], refs[n:2 * n]
        send_sems, recv_sems, fwd_send_sems, fwd_recv_sems = refs[2 * n:]
        x, y, c, chips = _place()
        q = 2 * x + y
        sibling = (x, y, 1 - c)
        sends = []
        for r in range(AG_CHUNKS):
            for t in range(n):
                hc = srcs[t].shape[0] // 2 // AG_CHUNKS
                rows = pl.ds(c * (hc * AG_CHUNKS) + r * hc, hc)
                for j, (cx, cy) in enumerate(chips):
                    k = (t * 3 + j) * AG_CHUNKS + r
                    cp = pltpu.make_async_remote_copy(src_ref=srcs[t].at[rows], dst_ref=dsts[t].at[q, rows],
                                                      send_sem=send_sems.at[k], recv_sem=recv_sems.at[k],
                                                      device_id=(cx, cy, c), device_id_type=MESH)
                    cp.start()
                    sends.append(cp)
        fwds = []
        for r in range(AG_CHUNKS):
            for t in range(n):
                hc = srcs[t].shape[0] // 2 // AG_CHUNKS
                sub = hc // D2D_SPLIT
                for j, (cx, cy) in enumerate(chips):
                    k = (t * 3 + j) * AG_CHUNKS + r
                    base = c * (hc * AG_CHUNKS) + r * hc
                    part = dsts[t].at[2 * cx + cy, pl.ds(base, hc)]
                    pltpu.make_async_remote_copy(src_ref=part, dst_ref=part, send_sem=send_sems.at[k], recv_sem=recv_sems.at[k],
                                                 device_id=(cx, cy, c), device_id_type=MESH).wait_recv()
                    for u in range(D2D_SPLIT):
                        piece = dsts[t].at[2 * cx + cy, pl.ds(base + u * sub, sub)]
                        pltpu.make_async_remote_copy(src_ref=piece, dst_ref=piece, send_sem=fwd_send_sems.at[k],
                                                     recv_sem=fwd_recv_sems.at[k], device_id=sibling, device_id_type=MESH).start()
                    fwds.append((part, k))
        for r in range(AG_CHUNKS):
            for t in range(n):
                hc = srcs[t].shape[0] // 2 // AG_CHUNKS
                for j, (cx, cy) in enumerate(chips):
                    k = (t * 3 + j) * AG_CHUNKS + r
                    part = dsts[t].at[2 * cx + cy, pl.ds((1 - c) * (hc * AG_CHUNKS) + r * hc, hc)]
                    pltpu.make_async_remote_copy(src_ref=part, dst_ref=part, send_sem=fwd_send_sems.at[k],
                                                 recv_sem=fwd_recv_sems.at[k], device_id=sibling, device_id_type=MESH).wait_recv()
        for cp in sends:
            cp.wait_send()
        for part, k in fwds:
            pltpu.make_async_remote_copy(src_ref=part, dst_ref=part, send_sem=fwd_send_sems.at[k], recv_sem=fwd_recv_sems.at[k],
                                         device_id=sibling, device_id_type=MESH).wait_send()

    return pl.pallas_call(
        body, name="all_gather_weights", in_specs=[ANY] * n, out_specs=[ANY] * n,
        out_shape=[jax.ShapeDtypeStruct((N_CHIP,) + s.shape, s.dtype) for s in shards],
        scratch_shapes=[pltpu.SemaphoreType.DMA((nsem,)), pltpu.SemaphoreType.DMA((nsem,)), pltpu.SemaphoreType.DMA((nsem,)),
                        pltpu.SemaphoreType.DMA((nsem,))],
    )(*shards)


def _exchange_halves(grads):
    n = len(grads)

    def body(*refs):
        srcs, theirs = refs[:n], refs[n:2 * n]
        send_sems, recv_sems = refs[2 * n:]
        x, y, c, _ = _place()
        sibling = (x, y, 1 - c)
        waits = []
        for t in range(n):
            h = srcs[t].shape[1] // 2
            sub = h // D2D_SPLIT
            for s in range(N_CHIP):
                for u in range(D2D_SPLIT):
                    pltpu.make_async_remote_copy(src_ref=srcs[t].at[s, pl.ds((1 - c) * h + u * sub, sub)],
                                                 dst_ref=theirs[t].at[s, pl.ds(u * sub, sub)],
                                                 send_sem=send_sems.at[t], recv_sem=recv_sems.at[t],
                                                 device_id=sibling, device_id_type=MESH).start()
            waits.append(pltpu.make_async_remote_copy(src_ref=srcs[t].at[:, pl.ds((1 - c) * h, h)], dst_ref=theirs[t],
                                                      send_sem=send_sems.at[t], recv_sem=recv_sems.at[t],
                                                      device_id=sibling, device_id_type=MESH))
        for whole in waits:
            whole.wait()

    half = [jax.ShapeDtypeStruct((N_CHIP, g.shape[1] // 2, g.shape[2]), g.dtype) for g in grads]
    return pl.pallas_call(
        body, name="exchange_halves", in_specs=[ANY] * n, out_specs=[ANY] * n, out_shape=half,
        scratch_shapes=[pltpu.SemaphoreType.DMA((n,)), pltpu.SemaphoreType.DMA((n,))],
    )(*grads)


def _exchange_chips(parts):
    n = len(parts)

    def body(*refs):
        srcs, dsts = refs[:n], refs[n:2 * n]
        send_sems, recv_sems = refs[2 * n:]
        x, y, c, chips = _place()
        q = 2 * x + y
        sends = []
        for t in range(n):
            for j, (cx, cy) in enumerate(chips):
                cp = pltpu.make_async_remote_copy(src_ref=srcs[t].at[2 * cx + cy], dst_ref=dsts[t].at[q],
                                                  send_sem=send_sems.at[3 * t + j], recv_sem=recv_sems.at[3 * t + j],
                                                  device_id=(cx, cy, c), device_id_type=MESH)
                cp.start()
                sends.append(cp)
        for t in range(n):
            for j, (cx, cy) in enumerate(chips):
                part = dsts[t].at[2 * cx + cy]
                pltpu.make_async_remote_copy(src_ref=part, dst_ref=part, send_sem=send_sems.at[3 * t + j],
                                             recv_sem=recv_sems.at[3 * t + j], device_id=(cx, cy, c), device_id_type=MESH).wait_recv()
        for cp in sends:
            cp.wait_send()

    return pl.pallas_call(
        body, name="exchange_chips", in_specs=[ANY] * n, out_specs=[ANY] * n,
        out_shape=[jax.ShapeDtypeStruct(p.shape, p.dtype) for p in parts],
        scratch_shapes=[pltpu.SemaphoreType.DMA((3 * n,)), pltpu.SemaphoreType.DMA((3 * n,))],
    )(*parts)


def _share_halves(halves):
    n = len(halves)

    def body(*refs):
        srcs, dsts = refs[:n], refs[n:2 * n]
        send_sems, recv_sems = refs[2 * n:]
        x, y, c, _ = _place()
        sibling = (x, y, 1 - c)
        for t in range(n):
            sub = srcs[t].shape[0] // (2 * D2D_SPLIT)
            for u in range(2 * D2D_SPLIT):
                rows = pl.ds(u * sub, sub)
                pltpu.make_async_remote_copy(src_ref=srcs[t].at[rows], dst_ref=dsts[t].at[rows], send_sem=send_sems.at[t],
                                             recv_sem=recv_sems.at[t], device_id=sibling, device_id_type=MESH).start()
        for t in range(n):
            pltpu.make_async_remote_copy(src_ref=srcs[t], dst_ref=dsts[t], send_sem=send_sems.at[t], recv_sem=recv_sems.at[t],
                                         device_id=sibling, device_id_type=MESH).wait()

    return pl.pallas_call(
        body, name="share_halves", in_specs=[ANY] * n, out_specs=[ANY] * n,
        out_shape=[jax.ShapeDtypeStruct(h.shape, h.dtype) for h in halves],
        scratch_shapes=[pltpu.SemaphoreType.DMA((n,)), pltpu.SemaphoreType.DMA((n,))],
    )(*halves)


def _gather_weights(shards, chip):
    gathered = _all_gather_weights(shards)
    return [lax.dynamic_update_slice(g, s[None], (chip, 0, 0)) for g, s in zip(gathered, shards)]


def _reduce_scatter(sends, chip, core):
    n = len(sends)
    theirs = _exchange_halves(sends)
    mines = [lax.dynamic_slice(g, (0, core * (g.shape[1] // 2), 0), (N_CHIP, g.shape[1] // 2, g.shape[2])) for g in sends]
    pair = [_sum_leading((mines[t], theirs[t]), f"pair_sum_{t}", out_dtype=BF16) for t in range(n)]
    contrib = _exchange_chips(pair)
    contrib = [lax.dynamic_update_slice(contrib[t], lax.dynamic_slice(pair[t], (chip, 0, 0), (1,) + pair[t].shape[1:]), (chip, 0, 0))
               for t in range(n)]
    halves = [_sum_leading(contrib[t], f"chip_sum_{t}") for t in range(n)]
    others = _share_halves(halves)
    out = []
    for mine, other in zip(halves, others):
        h = mine.shape[0]
        full = jnp.zeros((2 * h, mine.shape[1]), mine.dtype)
        full = lax.dynamic_update_slice(full, mine, (core * h, 0))
        out.append(lax.dynamic_update_slice(full, other, ((1 - core) * h, 0)))
    return out


def _pack(arrs):
    rows = []
    for a in arrs:
        flat = a.astype(F32).reshape(-1)
        pad = (-flat.shape[0]) % 1024
        rows.append(jnp.pad(flat, (0, pad)).reshape(-1, 128))
    return jnp.concatenate(rows, axis=0)


def _unpack(buf, shapes):
    out, r = [], 0
    for shp in shapes:
        size = 1
        for d in shp:
            size *= d
        nr = (size + 1023) // 1024 * 8
        out.append(buf[r:r + nr].reshape(-1)[:size].reshape(shp))
        r += nr
    return out


def _local_step(x, t, mod, w_in_r, w_bg, w_bs, w_out, w_ff1, w_ff2, gm_norm_w, gm_ws, gm_bs, conv_w, conv_b,
                dt_bias, a_log, d_skip, ssm_norm_w, final_norm_w):
    sh1, sc1, g1, sh2, sc2, g2 = [mod[:, i * D:(i + 1) * D] for i in range(6)]
    ws_b = jnp.where(jnp.tril(jnp.ones((Q, Q), bool))[None], gm_ws, 0.0).astype(BF16)
    bs_t = gm_bs.T
    pad32 = lambda v: jnp.pad(v, ((0, 0), (0, 128 - NHEAD)))
    dtb, alog, dsk = pad32(dt_bias), pad32(a_log), pad32(d_skip)

    h1 = _prenorm(x, sc1, sh1)
    (proj,) = _matmul(h1, w_in_r, "nn", name="mm_proj", tn=1152)
    y_a = _sgu_fwd(proj, gm_norm_w, ws_b, bs_t)
    xbc = _conv_fwd(proj, conv_w, conv_b)
    dt, cum, cum_t = _dt_fwd(proj, dtb, alog)
    y_ssd, states = _ssd_fwd(xbc, dt, cum, cum_t, dsk)
    y_b = _gatenorm_fwd(y_ssd, proj, ssm_norm_w)
    (ba,) = _matmul(y_a, w_bg, "nn", name="mm_branch_gm")
    (bb,) = _matmul(y_b, w_bs, "nn", name="mm_branch_ssm")
    mixed = _mix_fwd(proj, ba, bb)
    (mo,) = _matmul(mixed, w_out, "nn", name="mm_out")
    x1, h2 = _resid_norm(x, mo, g1, sc2, sh2)
    f, act = _matmul(h2, w_ff1, "nn", name="mm_ff1", out_dtypes=(BF16, BF16),
                     epi=lambda acc: (acc, jnp.square(jnp.maximum(acc, 0.0))))
    (fo,) = _matmul(act, w_ff2, "nn", name="mm_ff2")
    loss8, dx2, dfo, dg2, dfw = _loss_head(x1, fo, g2, final_norm_w, t)

    (df,) = _matmul(dfo, w_ff2, "nt", name="mm_ff2_dx", out_dtypes=(BF16,), epi_ins=(f,),
                    epi=lambda acc, fv: (acc * (2.0 * jnp.maximum(fv.astype(F32), 0.0)),))
    (g_ff2,) = _matmul(act, dfo, "tn", name="mm_ff2_dw", tk=512)
    (dh2,) = _matmul(df, w_ff1, "nt", name="mm_ff1_dx")
    (g_ff1,) = _matmul(h2, df, "tn", name="mm_ff1_dw", tk=512)
    dx1, dmo, dg1, dsc2, dsh2 = _resid_norm_bwd(x1, mo, g1, sc2, sh2, dh2, dx2)
    (dmixed,) = _matmul(dmo, w_out, "nt", name="mm_out_dx")
    (g_out,) = _matmul(mixed, dmo, "tn", name="mm_out_dw", tk=512)
    dproj, dba, dbb = _mix_bwd(proj, ba, bb, dmixed)
    (dy_a,) = _matmul(dba, w_bg, "nt", name="mm_branch_gm_dx")
    (g_bg,) = _matmul(y_a, dba, "tn", name="mm_branch_gm_dw", tk=512)
    (dy_b,) = _matmul(dbb, w_bs, "nt", name="mm_branch_ssm_dx")
    (g_bs,) = _matmul(y_b, dbb, "tn", name="mm_branch_ssm_dw", tk=512)
    dproj, d_gm_norm, d_ws, d_bs_t = _sgu_bwd(proj, gm_norm_w, ws_b, bs_t, dy_a, dproj)
    dy_ssd, dproj, d_ssm_norm = _gatenorm_bwd(y_ssd, proj, ssm_norm_w, dy_b, dproj)
    dxs, dbm, dcm, ddt, dcum, dcum_t, d_dsk = _ssd_bwd(xbc, dt, cum, cum_t, dsk, states, dy_ssd)
    dproj, d_dtb, d_alog = _dt_bwd(proj, dtb, alog, ddt, dcum, dcum_t, dproj)
    dproj, dw_x, db_x = _conv_bwd(proj, conv_w, conv_b, dxs, dproj, 0, "conv_bwd_x")
    dproj, dw_b, db_b = _conv_bwd(proj, conv_w, conv_b, dbm, dproj, DI, "conv_bwd_b")
    dproj, dw_c, db_c = _conv_bwd(proj, conv_w, conv_b, dcm, dproj, DI + D, "conv_bwd_c")
    d_conv_w = jnp.concatenate([dw_x, dw_b, dw_c], axis=1)
    d_conv_b = jnp.concatenate([db_x, db_b, db_c], axis=1)
    (dh1,) = _matmul(dproj, w_in_r, "nt", name="mm_proj_dx", tk=1152)
    (g_in_r,) = _matmul(h1, dproj, "tn", name="mm_proj_dw", tn=1152, tk=512)
    grad_x, dsc1, dsh1 = _prenorm_bwd(x, sc1, sh1, dh1, dx1)

    dmod = jnp.concatenate([dsh1, dsc1, dg1, dsh2, dsc2, dg2], axis=1)
    small = dict(gm_ws=d_ws, gm_norm_w=d_gm_norm, gm_bs=d_bs_t.T, conv_w=d_conv_w, conv_b=d_conv_b,
                 dt_bias=d_dtb[:, :NHEAD], a_log=d_alog[:, :NHEAD], d_skip=d_dsk[:, :NHEAD],
                 ssm_norm_w=d_ssm_norm, final_norm_w=dfw, dmod=dmod, loss=loss8[:1, :1])
    big = dict(w_in_r=g_in_r, w_bg=g_bg, w_bs=g_bs, w_out=g_out, w_ff1=g_ff1, w_ff2=g_ff2)
    return grad_x, small, big


SMALL_KEYS = ("gm_ws", "gm_norm_w", "gm_bs", "conv_w", "conv_b", "dt_bias", "a_log", "d_skip", "ssm_norm_w",
              "final_norm_w", "dmod", "loss")
SMALL_SHAPES = ((GM_G, Q, Q), (1, D), (GM_G, Q), (4, CONV), (1, CONV), (1, NHEAD), (1, NHEAD), (1, NHEAD), (1, DI),
                (1, D), (1, 6 * D), (1, 1))


def _reorder_w_in(w_full):
    k = w_full.shape[0]
    return jnp.concatenate([w_full[:, :8192], w_full[:, 8224:], w_full[:, 8192:8224],
                            jnp.zeros((k, W_IN_R - W_IN), w_full.dtype)], axis=1)


def _restore_w_in(g_r):
    return jnp.concatenate([g_r[:, :8192], g_r[:, OFF_DT:OFF_DT + NHEAD], g_r[:, 8192:OFF_DT]], axis=1)


def kernel(x, c, w_mod, b_mod, w_in, gm_norm_w, gm_ws, gm_bs, conv_w, conv_b, dt_bias, a_log, d_skip, ssm_norm_w, w_branch_gm, w_branch_ssm, w_out, w_ff1, w_ff2, final_norm_w, loss_target, m_w_mod, m_b_mod, m_w_in, m_gm_norm_w, m_gm_ws, m_gm_bs, m_conv_w, m_conv_b, m_dt_bias, m_a_log, m_d_skip, m_ssm_norm_w, m_w_branch_gm, m_w_branch_ssm, m_w_out, m_w_ff1, m_w_ff2, m_final_norm_w, v_w_mod, v_b_mod, v_w_in, v_gm_norm_w, v_gm_ws, v_gm_bs, v_conv_w, v_conv_b, v_dt_bias, v_a_log, v_d_skip, v_ssm_norm_w, v_w_branch_gm, v_w_branch_ssm, v_w_out, v_w_ff1, v_w_ff2, v_final_norm_w):
    ax, ay, ac = lax.axis_index("x"), lax.axis_index("y"), lax.axis_index("c")
    chip = 2 * ax + ay
    dev = 2 * chip + ac
    seq = x.shape[1]
    nmod = w_mod.shape[2]

    first = jnp.concatenate([c, conv_w[0], jnp.zeros((3, D), F32)], axis=0)
    first_all = _all_gather_small(first, "all_gather_cond")
    c8 = first_all[:, 0, :]
    conv_w_full = jnp.concatenate([first_all[2 * k, 1:5, :] for k in range(N_CHIP)], axis=1)
    b_sl = lax.dynamic_slice(b_mod, (0, chip * nmod), (1, nmod))
    mod_part = _mod_fwd(c8, w_mod[0], b_sl)
    mod_all = _all_gather_small(mod_part, "all_gather_mod")
    mod = jnp.concatenate([lax.dynamic_slice(mod_all, (2 * k, dev, 0), (1, 1, nmod))[0] for k in range(N_CHIP)], axis=1)

    packed = jnp.concatenate([w_branch_gm[0], w_branch_ssm[0], w_out[0], w_ff2[0], w_ff1[0]], axis=0).astype(BF16)
    g_in, g_rest = _gather_weights([w_in[0].astype(BF16), packed], chip)
    w_in_r = _reorder_w_in(jnp.transpose(g_in, (1, 0, 2)).reshape(D, W_IN))
    r_bg, r_bs, r_out, r_ff2 = D // N_CHIP, DI // N_CHIP, D // N_CHIP, DFF // N_CHIP
    o1, o2, o3, o4 = r_bg, r_bg + r_bs, r_bg + r_bs + r_out, r_bg + r_bs + r_out + r_ff2
    w_bg_f = g_rest[:, :o1].reshape(D, D)
    w_bs_f = g_rest[:, o1:o2].reshape(DI, D)
    w_out_f = g_rest[:, o2:o3].reshape(D, D)
    w_ff2_f = g_rest[:, o3:o4].reshape(DFF, D)
    w_ff1_f = jnp.transpose(g_rest[:, o4:], (1, 0, 2)).reshape(D, DFF)

    grad_x, small, big = _local_step(
        x[0], loss_target[0], mod, w_in_r, w_bg_f, w_bs_f, w_out_f, w_ff1_f, w_ff2_f, gm_norm_w, gm_ws[0], gm_bs[0],
        conv_w_full, conv_b, dt_bias, a_log, d_skip, ssm_norm_w, final_norm_w.reshape(1, D))

    small_all = _all_gather_small(_pack([small[k] for k in SMALL_KEYS]), "all_gather_small_grads")
    small_sum = _sum_leading(small_all, "sum_small_grads")
    s_ws, s_gnw, s_bs, s_cw, s_cb, s_dtb, s_alog, s_dsk, s_snw, s_fnw, s_bmod, s_loss = _unpack(small_sum, SMALL_SHAPES)
    dmod_all = jnp.stack([_unpack(small_all[k], SMALL_SHAPES)[10][0] for k in range(N_DEV)], axis=0)
    g_w_mod = _mod_wgrad(c8, lax.dynamic_slice(dmod_all, (0, chip * nmod), (N_DEV, nmod)))
    g_conv_w = lax.dynamic_slice(s_cw, (0, chip * (CONV // N_CHIP)), (4, CONV // N_CHIP))

    g_in_full = _restore_w_in(big["w_in_r"])
    send_in = jnp.transpose(g_in_full.reshape(D, N_CHIP, W_IN // N_CHIP), (1, 0, 2)).astype(BF16)
    send_rest = jnp.concatenate([
        big["w_bg"].reshape(N_CHIP, r_bg, D), big["w_bs"].reshape(N_CHIP, r_bs, D), big["w_out"].reshape(N_CHIP, r_out, D),
        big["w_ff2"].reshape(N_CHIP, r_ff2, D), jnp.transpose(big["w_ff1"].reshape(D, N_CHIP, D), (1, 0, 2))], axis=1).astype(BF16)
    g_w_in, g_rest_sh = _reduce_scatter([send_in, send_rest], chip, ac)
    g_w_bg, g_w_bs, g_w_out, g_w_ff2, g_w_ff1 = g_rest_sh[:o1], g_rest_sh[o1:o2], g_rest_sh[o2:o3], g_rest_sh[o3:o4], g_rest_sh[o4:]

    def adam_big(w, g, m, v, name):
        d, m2, v2 = _adam(w.reshape(g.shape), g, m.reshape(g.shape), v.reshape(g.shape), name)
        return g.reshape(w.shape), d.reshape(w.shape), m2.reshape(w.shape), v2.reshape(w.shape)

    res = {}
    res["w_mod"] = adam_big(w_mod, g_w_mod, m_w_mod, v_w_mod, "adam_w_mod")
    res["w_in"] = adam_big(w_in, g_w_in, m_w_in, v_w_in, "adam_w_in")
    res["w_branch_gm"] = adam_big(w_branch_gm, g_w_bg, m_w_branch_gm, v_w_branch_gm, "adam_w_branch_gm")
    res["w_branch_ssm"] = adam_big(w_branch_ssm, g_w_bs, m_w_branch_ssm, v_w_branch_ssm, "adam_w_branch_ssm")
    res["w_out"] = adam_big(w_out, g_w_out, m_w_out, v_w_out, "adam_w_out")
    res["w_ff1"] = adam_big(w_ff1, g_w_ff1, m_w_ff1, v_w_ff1, "adam_w_ff1")
    res["w_ff2"] = adam_big(w_ff2, g_w_ff2, m_w_ff2, v_w_ff2, "adam_w_ff2")

    names = ("b_mod", "gm_norm_w", "gm_ws", "gm_bs", "conv_w", "conv_b", "dt_bias", "a_log", "d_skip", "ssm_norm_w", "final_norm_w")
    ws = (b_mod, gm_norm_w, gm_ws, gm_bs, conv_w, conv_b, dt_bias, a_log, d_skip, ssm_norm_w, final_norm_w)
    ms = (m_b_mod, m_gm_norm_w, m_gm_ws, m_gm_bs, m_conv_w, m_conv_b, m_dt_bias, m_a_log, m_d_skip, m_ssm_norm_w, m_final_norm_w)
    vs = (v_b_mod, v_gm_norm_w, v_gm_ws, v_gm_bs, v_conv_w, v_conv_b, v_dt_bias, v_a_log, v_d_skip, v_ssm_norm_w, v_final_norm_w)
    gs = (s_bmod, s_gnw, s_ws, s_bs, g_conv_w, s_cb, s_dtb, s_alog, s_dsk, s_snw, s_fnw)
    gs = [g.reshape(w.shape) for g, w in zip(gs, ws)]
    shapes = [w.shape for w in ws]
    d_p, m_p, v_p = _adam(_pack(ws), _pack(gs), _pack(ms), _pack(vs), "adam_small")
    for name, g, d, m2, v2 in zip(names, gs, _unpack(d_p, shapes), _unpack(m_p, shapes), _unpack(v_p, shapes)):
        res[name] = (g, d, m2, v2)

    order = ("w_mod", "b_mod", "w_in", "gm_norm_w", "gm_ws", "gm_bs", "conv_w", "conv_b", "dt_bias", "a_log", "d_skip",
             "ssm_norm_w", "w_branch_gm", "w_branch_ssm", "w_out", "w_ff1", "w_ff2", "final_norm_w")
    loss = s_loss.reshape(())
    return (loss, grad_x.reshape(x.shape), *[res[k][0] for k in order], *[res[k][1] for k in order],
            *[res[k][2] for k in order], *[res[k][3] for k in order])
```

```python
import functools

import jax
import jax.numpy as jnp
from jax import lax
from jax.experimental import pallas as pl
from jax.experimental.pallas import tpu as pltpu

F32 = jnp.float32
BF16 = jnp.bfloat16
MESH = pl.DeviceIdType.MESH
HIGHEST = lax.Precision.HIGHEST

D = 1024
EPS = 1e-6
Q = 128
GM_G = 8
NHEAD = 32
NGRP = 8
DI = 2048
CONV = 4096
DFF = 4096
W_IN = 10272
W_IN_R = 10368
OFF_Z, OFF_XBC, OFF_GA, OFF_DT = 2048, 4096, 8192, 10240
N_CHIP = 4
N_DEV = 8
AG_CHUNKS = 4
D2D_SPLIT = 4

ADAM_LR, ADAM_B1, ADAM_B2, ADAM_EPS, ADAM_WD, ADAM_STEP = 0.001, 0.9, 0.999, 1e-08, 0.01, 10

ANY = pl.BlockSpec(memory_space=pl.ANY)
VMEM = pl.BlockSpec(memory_space=pltpu.VMEM)


def _full(shape):
    return pl.BlockSpec(shape, lambda *_: (0,) * len(shape))


def _pick(n, prefs):
    for p in prefs:
        if n % p == 0:
            return p
    return n


def _rms(x):
    return x * lax.rsqrt(jnp.mean(x * x, axis=-1, keepdims=True) + EPS)


def _modnorm(x, sc, sh):
    return _rms(x) * (1.0 + sc) + sh


def _sgu_pre(u, v, w):
    return jax.nn.gelu(u), _rms(jax.nn.gelu(v)) * w


def _gatenorm(y, z, w):
    g = y * jax.nn.silu(z)
    return _rms(g) * w


def _mix(ga, gb, ba, bb):
    return jax.nn.sigmoid(ga) * ba + jax.nn.sigmoid(gb) * bb


def _loss_tile(x1, fo, g2, fw, t):
    x2 = x1 + g2 * fo
    y = _rms(x2) * fw
    err = jnp.square(y - t)
    return 0.5 * jnp.sum(jnp.mean(err, axis=-1))


def _tril(n):
    r = lax.broadcasted_iota(jnp.int32, (n, n), 0)
    c = lax.broadcasted_iota(jnp.int32, (n, n), 1)
    return r >= c


def _dt_prep(dtr, dtb, alog):
    dt = jax.nn.softplus(dtr + dtb)
    a = dt * (-jnp.exp(alog))
    ones = _tril(Q).astype(F32)
    cum = jnp.dot(ones, a, precision=HIGHEST, preferred_element_type=F32)
    cum_t = lax.dot_general(a, ones, (((0,), (1,)), ((), ())), precision=HIGHEST, preferred_element_type=F32)
    return dt, cum, cum_t


def _ssd_group(x0, x1, s0, s1, bm, cm, dt, cum, cum_t, dsk, grp):
    lane = lax.broadcasted_iota(jnp.int32, (1, 128), 1)
    sub = lax.broadcasted_iota(jnp.int32, (128, 1), 0)
    causal = _tril(Q)
    half = lane < 64
    half_rows = sub < 64
    bmb = bm.astype(BF16)
    cmb = cm.astype(BF16)
    cb = lax.dot_general(cmb, bmb, (((1,), (1,)), ((), ())), preferred_element_type=F32)

    def col(v, h):
        return jnp.sum(jnp.where(lane == h, v, 0.0), axis=1, keepdims=True)

    def row(v, h):
        return jnp.sum(jnp.where(sub == h, v, 0.0), axis=0, keepdims=True)

    def last(c):
        return jnp.sum(jnp.where(sub == Q - 1, c, 0.0), axis=0, keepdims=True)

    outs, states = [], []
    for p, (xp, sp) in enumerate(((x0, s0), (x1, s1))):
        h_a = 4 * grp + 2 * p
        h_b = h_a + 1
        dt_a, dt_b = col(dt, h_a), col(dt, h_b)
        cum_a, cum_b = col(cum, h_a), col(cum, h_b)
        row_a, row_b = row(cum_t, h_a), row(cum_t, h_b)
        last_a, last_b = last(cum_a), last(cum_b)
        xdt = xp * jnp.where(half, dt_a, dt_b)
        xdtb = xdt.astype(BF16)
        m_a = (cb * jnp.exp(jnp.where(causal, cum_a - row_a, -jnp.inf))).astype(BF16)
        m_b = (cb * jnp.exp(jnp.where(causal, cum_b - row_b, -jnp.inf))).astype(BF16)
        y_intra = jnp.where(half, jnp.dot(m_a, xdtb, preferred_element_type=F32),
                            jnp.dot(m_b, xdtb, preferred_element_type=F32))
        y_inter = lax.dot_general(cmb, sp.astype(BF16), (((1,), (1,)), ((), ())), preferred_element_type=F32)
        y_inter = y_inter * jnp.where(half, jnp.exp(cum_a), jnp.exp(cum_b))
        w_end = jnp.where(half, jnp.exp(last_a - cum_a), jnp.exp(last_b - cum_b))
        upd = lax.dot_general((xdt * w_end).astype(BF16), bmb, (((0,), (0,)), ((), ())), preferred_element_type=F32)
        states.append(sp * jnp.where(half_rows, jnp.exp(last_a), jnp.exp(last_b)) + upd)
        outs.append(y_intra + y_inter + xp * jnp.where(half, col(dsk, h_a), col(dsk, h_b)))
    return outs[0], outs[1], states[0], states[1]


def _matmul(a, b, mode, *, name, out_dtypes=(F32,), epi=None, epi_ins=(), tm=1024, tn=1024, tk=1024):
    if mode == "nn":
        (m, k), n = a.shape, b.shape[1]
    elif mode == "nt":
        (m, k), n = a.shape, b.shape[0]
    else:
        (k, m), n = a.shape, b.shape[1]
    tm, tn, tk = _pick(m, (tm, 512, 256, 128)), _pick(n, (tn, 1152, 1024, 512, 384, 256, 128)), _pick(k, (tk, 1152, 1024, 512, 256, 128))
    nk = k // tk
    if mode == "nn":
        a_spec = pl.BlockSpec((tm, tk), lambda i, j, kk: (i, kk))
        b_spec = pl.BlockSpec((tk, tn), lambda i, j, kk: (kk, j))
        dims = (((1,), (0,)), ((), ()))
    elif mode == "nt":
        a_spec = pl.BlockSpec((tm, tk), lambda i, j, kk: (i, kk))
        b_spec = pl.BlockSpec((tn, tk), lambda i, j, kk: (j, kk))
        dims = (((1,), (1,)), ((), ()))
    else:
        a_spec = pl.BlockSpec((tk, tm), lambda i, j, kk: (kk, i))
        b_spec = pl.BlockSpec((tk, tn), lambda i, j, kk: (kk, j))
        dims = (((0,), (0,)), ((), ()))
    o_spec = pl.BlockSpec((tm, tn), lambda i, j, kk: (i, j))
    n_epi, n_out = len(epi_ins), len(out_dtypes)

    def body(*refs):
        a_ref, b_ref = refs[0], refs[1]
        e_refs = refs[2:2 + n_epi]
        o_refs = refs[2 + n_epi:2 + n_epi + n_out]
        acc_ref = refs[-1]

        def finish(acc):
            outs = epi(acc, *[e[...] for e in e_refs]) if epi is not None else (acc,)
            for o_ref, val in zip(o_refs, outs):
                o_ref[...] = val.astype(o_ref.dtype)

        part = lax.dot_general(a_ref[...], b_ref[...], dims, preferred_element_type=F32)
        if nk == 1:
            finish(part)
        else:
            kk = pl.program_id(2)

            @pl.when(kk == 0)
            def _():
                acc_ref[...] = part

            @pl.when(kk > 0)
            def _():
                acc_ref[...] += part

            @pl.when(kk == nk - 1)
            def _():
                finish(acc_ref[...])

    return pl.pallas_call(
        body, name=name, grid=(m // tm, n // tn, nk),
        in_specs=[a_spec, b_spec] + [o_spec] * n_epi,
        out_specs=[o_spec] * n_out,
        out_shape=[jax.ShapeDtypeStruct((m, n), dt) for dt in out_dtypes],
        scratch_shapes=[pltpu.VMEM((tm, tn) if nk > 1 else (8, 128), F32)],
        compiler_params=pltpu.CompilerParams(dimension_semantics=("parallel", "parallel", "arbitrary")),
    )(a, b, *epi_ins)


def _row_tile(s):
    return _pick(s, (512, 256, 128))


def _prenorm(x, sc, sh):
    s = x.shape[0]
    tm = _row_tile(s)

    def body(x_ref, sc_ref, sh_ref, h_ref):
        h_ref[...] = _modnorm(x_ref[...], sc_ref[...], sh_ref[...]).astype(BF16)

    row = pl.BlockSpec((tm, D), lambda i: (i, 0))
    return pl.pallas_call(body, name="prenorm", grid=(s // tm,), in_specs=[row, _full((1, D)), _full((1, D))],
                          out_specs=row, out_shape=jax.ShapeDtypeStruct((s, D), BF16))(x, sc, sh)


def _prenorm_bwd(x, sc, sh, dh, dx_res):
    s = x.shape[0]
    tm = _row_tile(s)

    def body(x_ref, sc_ref, sh_ref, dh_ref, dr_ref, dx_ref, dsc_ref, dsh_ref):
        _, vjp = jax.vjp(_modnorm, x_ref[...], sc_ref[...], sh_ref[...])
        dx, dsc, dsh = vjp(dh_ref[...])
        dx_ref[...] = dr_ref[...] + dx

        @pl.when(pl.program_id(0) == 0)
        def _():
            dsc_ref[...] = jnp.zeros_like(dsc_ref)
            dsh_ref[...] = jnp.zeros_like(dsh_ref)

        dsc_ref[...] += dsc
        dsh_ref[...] += dsh

    row = pl.BlockSpec((tm, D), lambda i: (i, 0))
    vec = _full((1, D))
    return pl.pallas_call(
        body, name="prenorm_bwd", grid=(s // tm,), in_specs=[row, vec, vec, row, row], out_specs=[row, vec, vec],
        out_shape=[jax.ShapeDtypeStruct((s, D), F32), jax.ShapeDtypeStruct((1, D), F32), jax.ShapeDtypeStruct((1, D), F32)],
    )(x, sc, sh, dh, dx_res)


def _resid_norm(x, mo, g1, sc, sh):
    s = x.shape[0]
    tm = _row_tile(s)

    def body(x_ref, mo_ref, g_ref, sc_ref, sh_ref, x1_ref, h_ref):
        x1 = x_ref[...] + g_ref[...] * mo_ref[...]
        x1_ref[...] = x1
        h_ref[...] = _modnorm(x1, sc_ref[...], sh_ref[...]).astype(BF16)

    row = pl.BlockSpec((tm, D), lambda i: (i, 0))
    vec = _full((1, D))
    return pl.pallas_call(
        body, name="resid_norm", grid=(s // tm,), in_specs=[row, row, vec, vec, vec], out_specs=[row, row],
        out_shape=[jax.ShapeDtypeStruct((s, D), F32), jax.ShapeDtypeStruct((s, D), BF16)],
    )(x, mo, g1, sc, sh)


def _resid_norm_bwd(x1, mo, g1, sc, sh, dh, dx2):
    s = x1.shape[0]
    tm = _row_tile(s)

    def body(x1_ref, mo_ref, g_ref, sc_ref, sh_ref, dh_ref, dx2_ref, dx1_ref, dmo_ref, dg_ref, dsc_ref, dsh_ref):
        _, vjp = jax.vjp(_modnorm, x1_ref[...], sc_ref[...], sh_ref[...])
        dx, dsc, dsh = vjp(dh_ref[...])
        dx1 = dx2_ref[...] + dx
        dx1_ref[...] = dx1
        dmo_ref[...] = (dx1 * g_ref[...]).astype(BF16)

        @pl.when(pl.program_id(0) == 0)
        def _():
            dg_ref[...] = jnp.zeros_like(dg_ref)
            dsc_ref[...] = jnp.zeros_like(dsc_ref)
            dsh_ref[...] = jnp.zeros_like(dsh_ref)

        dg_ref[...] += jnp.sum(dx1 * mo_ref[...], axis=0, keepdims=True)
        dsc_ref[...] += dsc
        dsh_ref[...] += dsh

    row = pl.BlockSpec((tm, D), lambda i: (i, 0))
    vec = _full((1, D))
    vshape = jax.ShapeDtypeStruct((1, D), F32)
    return pl.pallas_call(
        body, name="resid_norm_bwd", grid=(s // tm,), in_specs=[row, row, vec, vec, vec, row, row],
        out_specs=[row, row, vec, vec, vec],
        out_shape=[jax.ShapeDtypeStruct((s, D), F32), jax.ShapeDtypeStruct((s, D), BF16), vshape, vshape, vshape],
    )(x1, mo, g1, sc, sh, dh, dx2)


def _loss_head(x1, fo, g2, fw, t):
    s = x1.shape[0]
    tm = _row_tile(s)

    def body(x1_ref, fo_ref, g_ref, fw_ref, t_ref, loss_ref, dx_ref, dfo_ref, dg_ref, dfw_ref):
        loss, (dx1, dfo, dg, dfw) = jax.value_and_grad(_loss_tile, argnums=(0, 1, 2, 3))(
            x1_ref[...], fo_ref[...], g_ref[...], fw_ref[...], t_ref[...])
        dx_ref[...] = dx1
        dfo_ref[...] = dfo.astype(BF16)

        @pl.when(pl.program_id(0) == 0)
        def _():
            loss_ref[...] = jnp.zeros_like(loss_ref)
            dg_ref[...] = jnp.zeros_like(dg_ref)
            dfw_ref[...] = jnp.zeros_like(dfw_ref)

        loss_ref[...] += jnp.full(loss_ref.shape, loss, F32)
        dg_ref[...] += dg
        dfw_ref[...] += dfw

    row = pl.BlockSpec((tm, D), lambda i: (i, 0))
    vec = _full((1, D))
    vshape = jax.ShapeDtypeStruct((1, D), F32)
    return pl.pallas_call(
        body, name="loss_head", grid=(s // tm,), in_specs=[row, row, vec, vec, row],
        out_specs=[_full((8, 128)), row, row, vec, vec],
        out_shape=[jax.ShapeDtypeStruct((8, 128), F32), jax.ShapeDtypeStruct((s, D), F32),
                   jax.ShapeDtypeStruct((s, D), BF16), vshape, vshape],
    )(x1, fo, g2, fw, t)


def _sgu_fwd(proj, norm_w, ws_b, bs_t):
    s = proj.shape[0]
    tm = _pick(s, (256, 128))

    def body(u_ref, v_ref, w_ref, ws_ref, bs_ref, y_ref):
        ug, vn = _sgu_pre(u_ref[...], v_ref[...], w_ref[...])
        vnb = vn.astype(BF16)
        for c in range(tm // Q):
            r = slice(c * Q, (c + 1) * Q)
            for g in range(GM_G):
                cs = slice(g * 128, (g + 1) * 128)
                sv = jnp.dot(ws_ref[g], vnb[r, cs], preferred_element_type=F32) + bs_ref[:, g:g + 1]
                y_ref[r, cs] = (ug[r, cs] * sv).astype(BF16)

    return pl.pallas_call(
        body, name="sgu_fwd", grid=(s // tm,),
        in_specs=[pl.BlockSpec((tm, D), lambda i: (i, 0)), pl.BlockSpec((tm, D), lambda i: (i, 1)),
                  _full((1, D)), _full((GM_G, Q, Q)), _full((Q, GM_G))],
        out_specs=pl.BlockSpec((tm, D), lambda i: (i, 0)),
        out_shape=jax.ShapeDtypeStruct((s, D), BF16),
    )(proj, proj, norm_w, ws_b, bs_t)


def _sgu_bwd(proj, norm_w, ws_b, bs_t, dy, dproj):
    s = proj.shape[0]
    tm = _pick(s, (256, 128))

    def body(u_ref, v_ref, w_ref, ws_ref, bs_ref, dy_ref, _, duv_ref, dw_ref, dws_ref, dbs_ref, dug_scr, dvn_scr):
        @pl.when(pl.program_id(0) == 0)
        def _():
            dw_ref[...] = jnp.zeros_like(dw_ref)
            dws_ref[...] = jnp.zeros_like(dws_ref)
            dbs_ref[...] = jnp.zeros_like(dbs_ref)

        (ug, vn), vjp = jax.vjp(_sgu_pre, u_ref[...], v_ref[...], w_ref[...])
        vnb = vn.astype(BF16)
        dy = dy_ref[...]
        causal = _tril(Q).astype(F32)
        for c in range(tm // Q):
            r = slice(c * Q, (c + 1) * Q)
            for g in range(GM_G):
                cs = slice(g * 128, (g + 1) * 128)
                blk = vnb[r, cs]
                sv = jnp.dot(ws_ref[g], blk, preferred_element_type=F32) + bs_ref[:, g:g + 1]
                dug_scr[r, cs] = dy[r, cs] * sv
                dsv = dy[r, cs] * ug[r, cs]
                dsvb = dsv.astype(BF16)
                dws_ref[g] += causal * lax.dot_general(dsvb, blk, (((1,), (1,)), ((), ())), preferred_element_type=F32)
                dbs_ref[:, g:g + 1] += jnp.sum(dsv, axis=1, keepdims=True)
                dvn_scr[r, cs] = lax.dot_general(ws_ref[g], dsvb, (((0,), (0,)), ((), ())), preferred_element_type=F32)
        du, dv, dw = vjp((dug_scr[...], dvn_scr[...]))
        duv_ref[:, :D] = du.astype(BF16)
        duv_ref[:, D:] = dv.astype(BF16)
        dw_ref[...] += dw

    return pl.pallas_call(
        body, name="sgu_bwd", grid=(s // tm,),
        in_specs=[pl.BlockSpec((tm, D), lambda i: (i, 0)), pl.BlockSpec((tm, D), lambda i: (i, 1)),
                  _full((1, D)), _full((GM_G, Q, Q)), _full((Q, GM_G)), pl.BlockSpec((tm, D), lambda i: (i, 0)), ANY],
        out_specs=[pl.BlockSpec((tm, 2 * D), lambda i: (i, 0)), _full((1, D)), _full((GM_G, Q, Q)), _full((Q, GM_G))],
        out_shape=[jax.ShapeDtypeStruct(dproj.shape, BF16), jax.ShapeDtypeStruct((1, D), F32),
                   jax.ShapeDtypeStruct((GM_G, Q, Q), F32), jax.ShapeDtypeStruct((Q, GM_G), F32)],
        scratch_shapes=[pltpu.VMEM((tm, D), F32), pltpu.VMEM((tm, D), F32)],
        input_output_aliases={6: 0},
    )(proj, proj, norm_w, ws_b, bs_t, dy, dproj)


def _gatenorm_fwd(y_ssd, proj, norm_w):
    s = y_ssd.shape[0]
    tm = _pick(s, (256, 128))
    gw = DI // NGRP

    def body(y_ref, z_ref, w_ref, o_ref):
        for g in range(NGRP):
            cs = slice(g * gw, (g + 1) * gw)
            o_ref[:, cs] = _gatenorm(y_ref[:, cs], z_ref[:, cs], w_ref[:, cs]).astype(BF16)

    return pl.pallas_call(
        body, name="gatenorm_fwd", grid=(s // tm,),
        in_specs=[pl.BlockSpec((tm, DI), lambda i: (i, 0)), pl.BlockSpec((tm, DI), lambda i: (i, OFF_Z // DI)), _full((1, DI))],
        out_specs=pl.BlockSpec((tm, DI), lambda i: (i, 0)),
        out_shape=jax.ShapeDtypeStruct((s, DI), BF16),
    )(y_ssd, proj, norm_w)


def _gatenorm_bwd(y_ssd, proj, norm_w, dyb, dproj):
    s = y_ssd.shape[0]
    tm = _pick(s, (256, 128))
    gw = DI // NGRP

    def body(y_ref, z_ref, w_ref, d_ref, _, dy_ref, dz_ref, dw_ref):
        @pl.when(pl.program_id(0) == 0)
        def _():
            dw_ref[...] = jnp.zeros_like(dw_ref)

        for g in range(NGRP):
            cs = slice(g * gw, (g + 1) * gw)
            _, vjp = jax.vjp(_gatenorm, y_ref[:, cs], z_ref[:, cs], w_ref[:, cs])
            dy, dz, dw = vjp(d_ref[:, cs])
            dy_ref[:, cs] = dy
            dz_ref[:, cs] = dz.astype(BF16)
            dw_ref[:, cs] += dw

    blk = pl.BlockSpec((tm, DI), lambda i: (i, 0))
    zblk = pl.BlockSpec((tm, DI), lambda i: (i, OFF_Z // DI))
    return pl.pallas_call(
        body, name="gatenorm_bwd", grid=(s // tm,),
        in_specs=[blk, zblk, _full((1, DI)), blk, ANY],
        out_specs=[blk, zblk, _full((1, DI))],
        out_shape=[jax.ShapeDtypeStruct((s, DI), F32), jax.ShapeDtypeStruct(dproj.shape, BF16), jax.ShapeDtypeStruct((1, DI), F32)],
        input_output_aliases={4: 1},
    )(y_ssd, proj, norm_w, dyb, dproj)


def _mix_fwd(proj, ba, bb):
    s = proj.shape[0]
    tm = _row_tile(s)
    gb0 = OFF_GA // D

    def body(ga_ref, gb_ref, ba_ref, bb_ref, o_ref):
        o_ref[...] = _mix(ga_ref[...], gb_ref[...], ba_ref[...], bb_ref[...]).astype(BF16)

    row = pl.BlockSpec((tm, D), lambda i: (i, 0))
    return pl.pallas_call(
        body, name="mix_fwd", grid=(s // tm,),
        in_specs=[pl.BlockSpec((tm, D), lambda i: (i, gb0)), pl.BlockSpec((tm, D), lambda i: (i, gb0 + 1)), row, row],
        out_specs=row, out_shape=jax.ShapeDtypeStruct((s, D), BF16),
    )(proj, proj, ba, bb)


def _mix_bwd(proj, ba, bb, dmixed):
    s = proj.shape[0]
    tm = _row_tile(s)
    gb0 = OFF_GA // D

    def body(ga_ref, gb_ref, ba_ref, bb_ref, d_ref, dg_ref, dba_ref, dbb_ref):
        _, vjp = jax.vjp(_mix, ga_ref[...], gb_ref[...], ba_ref[...], bb_ref[...])
        dga, dgb, dba, dbb = vjp(d_ref[...])
        dg_ref[:, :D] = dga.astype(BF16)
        dg_ref[:, D:] = dgb.astype(BF16)
        dba_ref[...] = dba.astype(BF16)
        dbb_ref[...] = dbb.astype(BF16)

    row = pl.BlockSpec((tm, D), lambda i: (i, 0))
    return pl.pallas_call(
        body, name="mix_bwd", grid=(s // tm,),
        in_specs=[pl.BlockSpec((tm, D), lambda i: (i, gb0)), pl.BlockSpec((tm, D), lambda i: (i, gb0 + 1)), row, row, row],
        out_specs=[pl.BlockSpec((tm, 2 * D), lambda i: (i, OFF_GA // (2 * D))), row, row],
        out_shape=[jax.ShapeDtypeStruct((s, W_IN_R), BF16), jax.ShapeDtypeStruct((s, D), BF16), jax.ShapeDtypeStruct((s, D), BF16)],
    )(proj, proj, ba, bb, dmixed)


CONV_TC = 1024


def _conv_fwd(proj, conv_w, conv_b):
    s = proj.shape[0]
    tm = _row_tile(s)
    cb0 = OFF_XBC // CONV_TC

    def body(x_ref, halo_ref, w_ref, b_ref, o_ref):
        halo = jnp.where(pl.program_id(0) > 0, halo_ref[...], 0.0)
        ext = jnp.concatenate([halo, x_ref[...]], axis=0)
        acc = jnp.broadcast_to(b_ref[...], (tm, CONV_TC))
        for k in range(4):
            shifted = ext if k == 3 else pltpu.roll(ext, 3 - k, 0)
            acc = acc + w_ref[k:k + 1, :] * shifted[8:, :]
        o_ref[...] = jax.nn.silu(acc)

    return pl.pallas_call(
        body, name="conv_fwd", grid=(s // tm, CONV // CONV_TC),
        in_specs=[pl.BlockSpec((tm, CONV_TC), lambda i, j: (i, cb0 + j)),
                  pl.BlockSpec((8, CONV_TC), lambda i, j: (jnp.maximum(i * (tm // 8) - 1, 0), cb0 + j)),
                  pl.BlockSpec((4, CONV_TC), lambda i, j: (0, j)), pl.BlockSpec((1, CONV_TC), lambda i, j: (0, j))],
        out_specs=pl.BlockSpec((tm, CONV_TC), lambda i, j: (i, j)),
        out_shape=jax.ShapeDtypeStruct((s, CONV), F32),
    )(proj, proj, conv_w, conv_b)


def _conv_bwd(proj, conv_w, conv_b, dact, dproj, col0, name):
    s, width = dact.shape
    tm = _row_tile(s)
    nt = s // tm
    c0 = col0 // CONV_TC
    cb0 = OFF_XBC // CONV_TC + c0

    def body(x_ref, prev_ref, next_ref, d_ref, dnext_ref, w_ref, b_ref, _, dx_ref, dw_ref, db_ref):
        i = pl.program_id(1)
        prev = jnp.where(i > 0, prev_ref[...], 0.0)
        ext = jnp.concatenate([prev, x_ref[...], next_ref[...]], axis=0)
        dext = jnp.concatenate([d_ref[...], jnp.where(i < nt - 1, dnext_ref[...], 0.0)], axis=0)
        pre = jnp.broadcast_to(b_ref[...], (tm + 8, CONV_TC))
        taps = []
        for k in range(4):
            shifted = (ext if k == 3 else pltpu.roll(ext, 3 - k, 0))[8:, :]
            taps.append(shifted)
            pre = pre + w_ref[k:k + 1, :] * shifted
        sig = jax.nn.sigmoid(pre)
        dpre = dext * (sig * (1.0 + pre * (1.0 - sig)))
        dx = jnp.zeros((tm, CONV_TC), F32)
        for k in range(4):
            shifted = dpre if k == 3 else pltpu.roll(dpre, tm + 8 - (3 - k), 0)
            dx = dx + w_ref[k:k + 1, :] * shifted[:tm, :]
        dx_ref[...] = dx.astype(BF16)

        @pl.when(i == 0)
        def _():
            dw_ref[...] = jnp.zeros_like(dw_ref)
            db_ref[...] = jnp.zeros_like(db_ref)

        dtile = dpre[:tm, :]
        for k in range(4):
            dw_ref[k:k + 1, :] += jnp.sum(dtile * taps[k][:tm, :], axis=0, keepdims=True)
        db_ref[...] += jnp.sum(dtile, axis=0, keepdims=True)

    r8 = tm // 8
    return pl.pallas_call(
        body, name=name, grid=(width // CONV_TC, nt),
        in_specs=[pl.BlockSpec((tm, CONV_TC), lambda j, i: (i, cb0 + j)),
                  pl.BlockSpec((8, CONV_TC), lambda j, i: (jnp.maximum(i * r8 - 1, 0), cb0 + j)),
                  pl.BlockSpec((8, CONV_TC), lambda j, i: (jnp.minimum((i + 1) * r8, nt * r8 - 1), cb0 + j)),
                  pl.BlockSpec((tm, CONV_TC), lambda j, i: (i, j)),
                  pl.BlockSpec((8, CONV_TC), lambda j, i: (jnp.minimum((i + 1) * r8, nt * r8 - 1), j)),
                  pl.BlockSpec((4, CONV_TC), lambda j, i: (0, c0 + j)), pl.BlockSpec((1, CONV_TC), lambda j, i: (0, c0 + j)), ANY],
        out_specs=[pl.BlockSpec((tm, CONV_TC), lambda j, i: (i, cb0 + j)),
                   pl.BlockSpec((4, CONV_TC), lambda j, i: (0, j)), pl.BlockSpec((1, CONV_TC), lambda j, i: (0, j))],
        out_shape=[jax.ShapeDtypeStruct(dproj.shape, BF16), jax.ShapeDtypeStruct((4, width), F32), jax.ShapeDtypeStruct((1, width), F32)],
        input_output_aliases={7: 0},
    )(proj, proj, proj, dact, dact, conv_w, conv_b, dproj)


def _dt_fwd(proj, dtb, alog):
    s = proj.shape[0]
    nc = s // Q

    def body(r_ref, b_ref, a_ref, dt_ref, cum_ref, cumt_ref):
        dt, cum, cum_t = _dt_prep(r_ref[...], b_ref[...], a_ref[...])
        dt_ref[...] = dt
        cum_ref[...] = cum
        cumt_ref[...] = cum_t

    blk = pl.BlockSpec((Q, 128), lambda n: (n, 0))
    return pl.pallas_call(
        body, name="dt_fwd", grid=(nc,),
        in_specs=[pl.BlockSpec((Q, 128), lambda n: (n, OFF_DT // 128)), _full((1, 128)), _full((1, 128))],
        out_specs=[blk, blk, pl.BlockSpec((None, 128, Q), lambda n: (n, 0, 0))],
        out_shape=[jax.ShapeDtypeStruct((s, 128), F32), jax.ShapeDtypeStruct((s, 128), F32), jax.ShapeDtypeStruct((nc, 128, Q), F32)],
    )(proj, dtb, alog)


def _dt_bwd(proj, dtb, alog, ddt, dcum, dcumt, dproj):
    s = proj.shape[0]
    nc = s // Q

    def body(r_ref, b_ref, a_ref, ddt_ref, dcum_ref, dcumt_ref, _, dr_ref, db_ref, da_ref):
        _, vjp = jax.vjp(_dt_prep, r_ref[...], b_ref[...], a_ref[...])
        dr, db, da = vjp((ddt_ref[...], dcum_ref[...], dcumt_ref[...]))
        dr_ref[...] = dr.astype(BF16)

        @pl.when(pl.program_id(0) == 0)
        def _():
            db_ref[...] = jnp.zeros_like(db_ref)
            da_ref[...] = jnp.zeros_like(da_ref)

        db_ref[...] += db
        da_ref[...] += da

    blk = pl.BlockSpec((Q, 128), lambda n: (n, 0))
    pblk = pl.BlockSpec((Q, 128), lambda n: (n, OFF_DT // 128))
    return pl.pallas_call(
        body, name="dt_bwd", grid=(nc,),
        in_specs=[pblk, _full((1, 128)), _full((1, 128)), blk, blk, pl.BlockSpec((None, 128, Q), lambda n: (n, 0, 0)), ANY],
        out_specs=[pblk, _full((1, 128)), _full((1, 128))],
        out_shape=[jax.ShapeDtypeStruct(dproj.shape, BF16), jax.ShapeDtypeStruct((1, 128), F32), jax.ShapeDtypeStruct((1, 128), F32)],
        input_output_aliases={6: 0},
    )(proj, dtb, alog, ddt, dcum, dcumt, dproj)


def _ssd_specs(chunk_of):
    xs = pl.BlockSpec((Q, 256), lambda n, g: (chunk_of(n), g))
    bm = pl.BlockSpec((Q, 128), lambda n, g: (chunk_of(n), DI // 128 + g))
    cm = pl.BlockSpec((Q, 128), lambda n, g: (chunk_of(n), DI // 128 + NGRP + g))
    per_chunk = pl.BlockSpec((Q, 128), lambda n, g: (chunk_of(n), 0))
    cum_t = pl.BlockSpec((None, 128, Q), lambda n, g: (chunk_of(n), 0, 0))
    state = pl.BlockSpec((None, 256, 128), lambda n, g: (chunk_of(n), g, 0))
    vec = pl.BlockSpec((1, 128), lambda n, g: (0, 0))
    return xs, bm, cm, per_chunk, cum_t, state, vec


def _ssd_fwd(xbc, dt, cum, cum_t, dsk):
    s = xbc.shape[0]
    nc = s // Q
    xs, bm, cm, per_chunk, cumt_spec, state_spec, vec = _ssd_specs(lambda n: n)

    def body(x_ref, b_ref, c_ref, dt_ref, cum_ref, cumt_ref, dsk_ref, y_ref, st_ref, carry):
        n, g = pl.program_id(0), pl.program_id(1)
        rows = pl.ds(pl.multiple_of(g * 256, 256), 256)

        @pl.when(n == 0)
        def _():
            carry[rows, :] = jnp.zeros((256, 128), F32)

        s_in = carry[rows, :]
        st_ref[...] = s_in
        y0, y1, n0, n1 = _ssd_group(x_ref[:, :128], x_ref[:, 128:], s_in[:128], s_in[128:], b_ref[...], c_ref[...],
                                    dt_ref[...], cum_ref[...], cumt_ref[...], dsk_ref[...], g)
        y_ref[:, :128] = y0
        y_ref[:, 128:] = y1
        carry[pl.ds(pl.multiple_of(g * 256, 256), 128), :] = n0
        carry[pl.ds(pl.multiple_of(g * 256 + 128, 128), 128), :] = n1

    return pl.pallas_call(
        body, name="ssd_fwd", grid=(nc, NGRP),
        in_specs=[xs, bm, cm, per_chunk, per_chunk, cumt_spec, vec],
        out_specs=[xs, state_spec],
        out_shape=[jax.ShapeDtypeStruct((s, DI), F32), jax.ShapeDtypeStruct((nc, DI, 128), F32)],
        scratch_shapes=[pltpu.VMEM((DI, 128), F32)],
    )(xbc, xbc, xbc, dt, cum, cum_t, dsk)


def _ssd_bwd(xbc, dt, cum, cum_t, dsk, states, dy):
    s = xbc.shape[0]
    nc = s // Q
    xs, bm, cm, per_chunk, cumt_spec, state_spec, vec = _ssd_specs(lambda n: nc - 1 - n)

    def body(x_ref, b_ref, c_ref, dt_ref, cum_ref, cumt_ref, dsk_ref, st_ref, dy_ref,
             dx_ref, db_ref, dc_ref, ddt_ref, dcum_ref, dcumt_ref, ddsk_ref, carry):
        n, g = pl.program_id(0), pl.program_id(1)
        rows = pl.ds(pl.multiple_of(g * 256, 256), 256)

        @pl.when(n == 0)
        def _():
            carry[rows, :] = jnp.zeros((256, 128), F32)

        @pl.when((n == 0) & (g == 0))
        def _():
            ddsk_ref[...] = jnp.zeros_like(ddsk_ref)

        @pl.when(g == 0)
        def _():
            ddt_ref[...] = jnp.zeros_like(ddt_ref)
            dcum_ref[...] = jnp.zeros_like(dcum_ref)
            dcumt_ref[...] = jnp.zeros_like(dcumt_ref)

        s_in = st_ref[...]
        d_out = carry[rows, :]
        fn = functools.partial(_ssd_group, grp=g)
        _, vjp = jax.vjp(fn, x_ref[:, :128], x_ref[:, 128:], s_in[:128], s_in[128:], b_ref[...], c_ref[...],
                         dt_ref[...], cum_ref[...], cumt_ref[...], dsk_ref[...])
        dx0, dx1, ds0, ds1, dbm, dcm, ddt, dcum, dcumt, ddsk = vjp((dy_ref[:, :128], dy_ref[:, 128:], d_out[:128], d_out[128:]))
        dx_ref[:, :128] = dx0
        dx_ref[:, 128:] = dx1
        db_ref[...] = dbm
        dc_ref[...] = dcm
        ddt_ref[...] += ddt
        dcum_ref[...] += dcum
        dcumt_ref[...] += dcumt
        ddsk_ref[...] += ddsk
        carry[pl.ds(pl.multiple_of(g * 256, 256), 128), :] = ds0
        carry[pl.ds(pl.multiple_of(g * 256 + 128, 128), 128), :] = ds1

    grp_blk = pl.BlockSpec((Q, 128), lambda n, g: (nc - 1 - n, g))
    return pl.pallas_call(
        body, name="ssd_bwd", grid=(nc, NGRP),
        in_specs=[xs, bm, cm, per_chunk, per_chunk, cumt_spec, vec, state_spec, xs],
        out_specs=[xs, grp_blk, grp_blk, per_chunk, per_chunk, cumt_spec, vec],
        out_shape=[jax.ShapeDtypeStruct((s, DI), F32), jax.ShapeDtypeStruct((s, D), F32), jax.ShapeDtypeStruct((s, D), F32),
                   jax.ShapeDtypeStruct((s, 128), F32), jax.ShapeDtypeStruct((s, 128), F32),
                   jax.ShapeDtypeStruct((nc, 128, Q), F32), jax.ShapeDtypeStruct((1, 128), F32)],
        scratch_shapes=[pltpu.VMEM((DI, 128), F32)],
    )(xbc, xbc, xbc, dt, cum, cum_t, dsk, states, dy)


def _adam_math(w, g, m, v):
    m2 = ADAM_B1 * m + (1.0 - ADAM_B1) * g
    v2 = ADAM_B2 * v + (1.0 - ADAM_B2) * jnp.square(g)
    m_hat = m2 / (1.0 - ADAM_B1 ** ADAM_STEP)
    v_hat = v2 / (1.0 - ADAM_B2 ** ADAM_STEP)
    delta = -ADAM_LR * (m_hat / (jnp.sqrt(v_hat) + ADAM_EPS) + ADAM_WD * w)
    return delta, m2, v2


def _adam(w, g, m, v, name):
    r, c = w.shape
    tr = r if r * c * 4 <= (1 << 20) else _pick(r, (128, 64, 32, 16, 8))

    def body(w_ref, g_ref, m_ref, v_ref, d_ref, m2_ref, v2_ref):
        d, m2, v2 = _adam_math(w_ref[...], g_ref[...], m_ref[...], v_ref[...])
        d_ref[...] = d
        m2_ref[...] = m2
        v2_ref[...] = v2

    blk = pl.BlockSpec((tr, c), lambda i: (i, 0))
    shp = jax.ShapeDtypeStruct((r, c), F32)
    return pl.pallas_call(body, name=name, grid=(r // tr,), in_specs=[blk] * 4, out_specs=[blk] * 3,
                          out_shape=[shp] * 3)(w, g, m, v)


def _sum_leading(xs, name, out_dtype=F32, tr=256):
    if isinstance(xs, tuple):
        a, b = xs
        n, r, c = a.shape
        tr = _pick(r, (tr, 128, 64, 32, 16, 8))

        def body2(a_ref, b_ref, o_ref):
            o_ref[...] = (a_ref[...].astype(F32) + b_ref[...].astype(F32)).astype(out_dtype)

        blk = pl.BlockSpec((None, tr, c), lambda s, i: (s, i, 0))
        return pl.pallas_call(body2, name=name, grid=(n, r // tr), in_specs=[blk, blk], out_specs=blk,
                              out_shape=jax.ShapeDtypeStruct((n, r, c), out_dtype))(a, b)
    n, r, c = xs.shape
    tr = r if n * r * c * 4 <= (8 << 20) else _pick(r, (tr, 128, 64, 32, 16, 8))

    def body(x_ref, o_ref):
        acc = x_ref[0].astype(F32)
        for s in range(1, n):
            acc = acc + x_ref[s].astype(F32)
        o_ref[...] = acc.astype(out_dtype)

    return pl.pallas_call(body, name=name, grid=(r // tr,), in_specs=[pl.BlockSpec((n, tr, c), lambda i: (0, i, 0))],
                          out_specs=pl.BlockSpec((tr, c), lambda i: (i, 0)),
                          out_shape=jax.ShapeDtypeStruct((r, c), out_dtype))(xs)


def _mod_fwd(c8, w_mod, b_sl):
    def body(c_ref, w_ref, b_ref, o_ref):
        o_ref[...] = jnp.dot(jax.nn.silu(c_ref[...]), w_ref[...], precision=HIGHEST, preferred_element_type=F32) + b_ref[...]

    return pl.pallas_call(body, name="mod_fwd", in_specs=[VMEM, VMEM, VMEM], out_specs=VMEM,
                          out_shape=jax.ShapeDtypeStruct((N_DEV, w_mod.shape[1]), F32))(c8, w_mod, b_sl)


def _mod_wgrad(c8, dmod):
    def body(c_ref, d_ref, o_ref):
        o_ref[...] = lax.dot_general(jax.nn.silu(c_ref[...]), d_ref[...], (((0,), (0,)), ((), ())),
                                     precision=HIGHEST, preferred_element_type=F32)

    return pl.pallas_call(body, name="mod_wgrad", in_specs=[VMEM, VMEM], out_specs=VMEM,
                          out_shape=jax.ShapeDtypeStruct((D, dmod.shape[1]), F32))(c8, dmod)


def _place():
    x, y, c = lax.axis_index("x"), lax.axis_index("y"), lax.axis_index("c")
    chips = [(1 - x, y), (x, 1 - y), (1 - x, 1 - y)]
    return x, y, c, chips


def _all_gather_small(v, name):
    r, w = v.shape

    def body(v_ref, o_ref, send_sems, recv_sems, local_sem):
        x, y, c, _ = _place()
        me = 4 * x + 2 * y + c
        own = pltpu.make_async_copy(v_ref, o_ref.at[me], local_sem)
        own.start()
        sends = []
        for k in range(1, N_DEV):
            tx = 1 - x if k & 4 else x
            ty = 1 - y if k & 2 else y
            tc = 1 - c if k & 1 else c
            peer = 4 * tx + 2 * ty + tc
            cp = pltpu.make_async_remote_copy(src_ref=v_ref, dst_ref=o_ref.at[me], send_sem=send_sems.at[k - 1],
                                              recv_sem=recv_sems.at[k - 1], device_id=(tx, ty, tc), device_id_type=MESH)
            cp.start()
            sends.append((cp, peer, (tx, ty, tc)))
        for k, (cp, peer, dev) in enumerate(sends):
            pltpu.make_async_remote_copy(src_ref=v_ref, dst_ref=o_ref.at[peer], send_sem=send_sems.at[k],
                                         recv_sem=recv_sems.at[k], device_id=dev, device_id_type=MESH).wait_recv()
        for cp, _, _ in sends:
            cp.wait_send()
        own.wait()

    return pl.pallas_call(
        body, name=name, in_specs=[VMEM], out_specs=VMEM, out_shape=jax.ShapeDtypeStruct((N_DEV, r, w), v.dtype),
        scratch_shapes=[pltpu.SemaphoreType.DMA((N_DEV - 1,)), pltpu.SemaphoreType.DMA((N_DEV - 1,)), pltpu.SemaphoreType.DMA],
    )(v)


def _all_gather_weights(shards):
    n = len(shards)
    nsem = n * 3 * AG_CHUNKS

    def body(*refs):
        srcs, dsts = refs[:n], refs[n:2 * n]
        send_sems, recv_sems, fwd_send_sems, fwd_recv_sems = refs[2 * n:]
        x, y, c, chips = _place()
        q = 2 * x + y
        sibling = (x, y, 1 - c)
        sends = []
        for r in range(AG_CHUNKS):
            for t in range(n):
                hc = srcs[t].shape[0] // 2 // AG_CHUNKS
                rows = pl.ds(c * (hc * AG_CHUNKS) + r * hc, hc)
                for j, (cx, cy) in enumerate(chips):
                    k = (t * 3 + j) * AG_CHUNKS + r
                    cp = pltpu.make_async_remote_copy(src_ref=srcs[t].at[rows], dst_ref=dsts[t].at[q, rows],
                                                      send_sem=send_sems.at[k], recv_sem=recv_sems.at[k],
                                                      device_id=(cx, cy, c), device_id_type=MESH)
                    cp.start()
                    sends.append(cp)
        fwds = []
        for r in range(AG_CHUNKS):
            for t in range(n):
                hc = srcs[t].shape[0] // 2 // AG_CHUNKS
                sub = hc // D2D_SPLIT
                for j, (cx, cy) in enumerate(chips):
                    k = (t * 3 + j) * AG_CHUNKS + r
                    base = c * (hc * AG_CHUNKS) + r * hc
                    part = dsts[t].at[2 * cx + cy, pl.ds(base, hc)]
                    pltpu.make_async_remote_copy(src_ref=part, dst_ref=part, send_sem=send_sems.at[k], recv_sem=recv_sems.at[k],
                                                 device_id=(cx, cy, c), device_id_type=MESH).wait_recv()
                    for u in range(D2D_SPLIT):
                        piece = dsts[t].at[2 * cx + cy, pl.ds(base + u * sub, sub)]
                        pltpu.make_async_remote_copy(src_ref=piece, dst_ref=piece, send_sem=fwd_send_sems.at[k],
                                                     recv_sem=fwd_recv_sems.at[k], device_id=sibling, device_id_type=MESH).start()
                    fwds.append((part, k))
        for r in range(AG_CHUNKS):
            for t in range(n):
                hc = srcs[t].shape[0] // 2 // AG_CHUNKS
                for j, (cx, cy) in enumerate(chips):
                    k = (t * 3 + j) * AG_CHUNKS + r
                    part = dsts[t].at[2 * cx + cy, pl.ds((1 - c) * (hc * AG_CHUNKS) + r * hc, hc)]
                    pltpu.make_async_remote_copy(src_ref=part, dst_ref=part, send_sem=fwd_send_sems.at[k],
                                                 recv_sem=fwd_recv_sems.at[k], device_id=sibling, device_id_type=MESH).wait_recv()
        for cp in sends:
            cp.wait_send()
        for part, k in fwds:
            pltpu.make_async_remote_copy(src_ref=part, dst_ref=part, send_sem=fwd_send_sems.at[k], recv_sem=fwd_recv_sems.at[k],
                                         device_id=sibling, device_id_type=MESH).wait_send()

    return pl.pallas_call(
        body, name="all_gather_weights", in_specs=[ANY] * n, out_specs=[ANY] * n,
        out_shape=[jax.ShapeDtypeStruct((N_CHIP,) + s.shape, s.dtype) for s in shards],
        scratch_shapes=[pltpu.SemaphoreType.DMA((nsem,)), pltpu.SemaphoreType.DMA((nsem,)), pltpu.SemaphoreType.DMA((nsem,)),
                        pltpu.SemaphoreType.DMA((nsem,))],
    )(*shards)


def _exchange_halves(grads):
    n = len(grads)

    def body(*refs):
        srcs, theirs = refs[:n], refs[n:2 * n]
        send_sems, recv_sems = refs[2 * n:]
        x, y, c, _ = _place()
        sibling = (x, y, 1 - c)
        waits = []
        for t in range(n):
            h = srcs[t].shape[1] // 2
            sub = h // D2D_SPLIT
            for s in range(N_CHIP):
                for u in range(D2D_SPLIT):
                    pltpu.make_async_remote_copy(src_ref=srcs[t].at[s, pl.ds((1 - c) * h + u * sub, sub)],
                                                 dst_ref=theirs[t].at[s, pl.ds(u * sub, sub)],
                                                 send_sem=send_sems.at[t], recv_sem=recv_sems.at[t],
                                                 device_id=sibling, device_id_type=MESH).start()
            waits.append(pltpu.make_async_remote_copy(src_ref=srcs[t].at[:, pl.ds((1 - c) * h, h)], dst_ref=theirs[t],
                                                      send_sem=send_sems.at[t], recv_sem=recv_sems.at[t],
                                                      device_id=sibling, device_id_type=MESH))
        for whole in waits:
            whole.wait()

    half = [jax.ShapeDtypeStruct((N_CHIP, g.shape[1] // 2, g.shape[2]), g.dtype) for g in grads]
    return pl.pallas_call(
        body, name="exchange_halves", in_specs=[ANY] * n, out_specs=[ANY] * n, out_shape=half,
        scratch_shapes=[pltpu.SemaphoreType.DMA((n,)), pltpu.SemaphoreType.DMA((n,))],
    )(*grads)


def _exchange_chips(parts):
    n = len(parts)

    def body(*refs):
        srcs, dsts = refs[:n], refs[n:2 * n]
        send_sems, recv_sems, local_sems = refs[2 * n:]
        x, y, c, chips = _place()
        q = 2 * x + y
        sends, owns = [], []
        for t in range(n):
            own = pltpu.make_async_copy(srcs[t].at[q], dsts[t].at[q], local_sems.at[t])
            own.start()
            owns.append(own)
            for j, (cx, cy) in enumerate(chips):
                cp = pltpu.make_async_remote_copy(src_ref=srcs[t].at[2 * cx + cy], dst_ref=dsts[t].at[q],
                                                  send_sem=send_sems.at[3 * t + j], recv_sem=recv_sems.at[3 * t + j],
                                                  device_id=(cx, cy, c), device_id_type=MESH)
                cp.start()
                sends.append(cp)
        for t in range(n):
            for j, (cx, cy) in enumerate(chips):
                part = dsts[t].at[2 * cx + cy]
                pltpu.make_async_remote_copy(src_ref=part, dst_ref=part, send_sem=send_sems.at[3 * t + j],
                                             recv_sem=recv_sems.at[3 * t + j], device_id=(cx, cy, c), device_id_type=MESH).wait_recv()
        for cp in sends:
            cp.wait_send()
        for own in owns:
            own.wait()

    return pl.pallas_call(
        body, name="exchange_chips", in_specs=[ANY] * n, out_specs=[ANY] * n,
        out_shape=[jax.ShapeDtypeStruct(p.shape, p.dtype) for p in parts],
        scratch_shapes=[pltpu.SemaphoreType.DMA((3 * n,)), pltpu.SemaphoreType.DMA((3 * n,)), pltpu.SemaphoreType.DMA((n,))],
    )(*parts)


def _share_halves(halves):
    n = len(halves)

    def body(*refs):
        srcs, dsts = refs[:n], refs[n:2 * n]
        send_sems, recv_sems = refs[2 * n:]
        x, y, c, _ = _place()
        sibling = (x, y, 1 - c)
        for t in range(n):
            h = srcs[t].shape[0]
            sub = h // (2 * D2D_SPLIT)
            for u in range(2 * D2D_SPLIT):
                pltpu.make_async_remote_copy(src_ref=srcs[t].at[pl.ds(u * sub, sub)], dst_ref=dsts[t].at[pl.ds(c * h + u * sub, sub)],
                                             send_sem=send_sems.at[t], recv_sem=recv_sems.at[t],
                                             device_id=sibling, device_id_type=MESH).start()
        for t in range(n):
            h = srcs[t].shape[0]
            pltpu.make_async_remote_copy(src_ref=srcs[t], dst_ref=dsts[t].at[pl.ds((1 - c) * h, h)], send_sem=send_sems.at[t],
                                         recv_sem=recv_sems.at[t], device_id=sibling, device_id_type=MESH).wait()

    return pl.pallas_call(
        body, name="share_halves", in_specs=[ANY] * n, out_specs=[ANY] * n,
        out_shape=[jax.ShapeDtypeStruct((2 * h.shape[0], h.shape[1]), h.dtype) for h in halves],
        scratch_shapes=[pltpu.SemaphoreType.DMA((n,)), pltpu.SemaphoreType.DMA((n,))],
    )(*halves)


def _gather_weights(shards, chip):
    gathered = _all_gather_weights(shards)
    return [lax.dynamic_update_slice(g, s[None], (chip, 0, 0)) for g, s in zip(gathered, shards)]


def _reduce_scatter(sends, core):
    n = len(sends)
    theirs = _exchange_halves(sends)
    mines = [lax.dynamic_slice(g, (0, core * (g.shape[1] // 2), 0), (N_CHIP, g.shape[1] // 2, g.shape[2])) for g in sends]
    pair = [_sum_leading((mines[t], theirs[t]), f"pair_sum_{t}", out_dtype=BF16) for t in range(n)]
    contrib = _exchange_chips(pair)
    halves = [_sum_leading(contrib[t], f"chip_sum_{t}") for t in range(n)]
    shared = _share_halves(halves)
    return [lax.dynamic_update_slice(full, mine, (core * mine.shape[0], 0)) for full, mine in zip(shared, halves)]


def _pack(arrs):
    rows = []
    for a in arrs:
        flat = a.astype(F32).reshape(-1)
        pad = (-flat.shape[0]) % 1024
        rows.append(jnp.pad(flat, (0, pad)).reshape(-1, 128))
    return jnp.concatenate(rows, axis=0)


def _unpack(buf, shapes):
    out, r = [], 0
    for shp in shapes:
        size = 1
        for d in shp:
            size *= d
        nr = (size + 1023) // 1024 * 8
        out.append(buf[r:r + nr].reshape(-1)[:size].reshape(shp))
        r += nr
    return out


def _local_step(x, t, mod, w_in_r, w_bg, w_bs, w_out, w_ff1, w_ff2, gm_norm_w, gm_ws, gm_bs, conv_w, conv_b,
                dt_bias, a_log, d_skip, ssm_norm_w, final_norm_w):
    sh1, sc1, g1, sh2, sc2, g2 = [mod[:, i * D:(i + 1) * D] for i in range(6)]
    ws_b = jnp.where(jnp.tril(jnp.ones((Q, Q), bool))[None], gm_ws, 0.0).astype(BF16)
    bs_t = gm_bs.T
    pad32 = lambda v: jnp.pad(v, ((0, 0), (0, 128 - NHEAD)))
    dtb, alog, dsk = pad32(dt_bias), pad32(a_log), pad32(d_skip)

    h1 = _prenorm(x, sc1, sh1)
    (proj,) = _matmul(h1, w_in_r, "nn", name="mm_proj", tn=1152)
    y_a = _sgu_fwd(proj, gm_norm_w, ws_b, bs_t)
    xbc = _conv_fwd(proj, conv_w, conv_b)
    dt, cum, cum_t = _dt_fwd(proj, dtb, alog)
    y_ssd, states = _ssd_fwd(xbc, dt, cum, cum_t, dsk)
    y_b = _gatenorm_fwd(y_ssd, proj, ssm_norm_w)
    (ba,) = _matmul(y_a, w_bg, "nn", name="mm_branch_gm")
    (bb,) = _matmul(y_b, w_bs, "nn", name="mm_branch_ssm", tm=512, tk=2048)
    mixed = _mix_fwd(proj, ba, bb)
    (mo,) = _matmul(mixed, w_out, "nn", name="mm_out")
    x1, h2 = _resid_norm(x, mo, g1, sc2, sh2)
    f, act = _matmul(h2, w_ff1, "nn", name="mm_ff1", out_dtypes=(BF16, BF16),
                     epi=lambda acc: (acc, jnp.square(jnp.maximum(acc, 0.0))))
    (fo,) = _matmul(act, w_ff2, "nn", name="mm_ff2", tm=512, tk=4096)
    loss8, dx2, dfo, dg2, dfw = _loss_head(x1, fo, g2, final_norm_w, t)

    (df,) = _matmul(dfo, w_ff2, "nt", name="mm_ff2_dx", out_dtypes=(BF16,), epi_ins=(f,),
                    epi=lambda acc, fv: (acc * (2.0 * jnp.maximum(fv.astype(F32), 0.0)),))
    (g_ff2,) = _matmul(act, dfo, "tn", name="mm_ff2_dw", tm=512, tk=4096)
    (dh2,) = _matmul(df, w_ff1, "nt", name="mm_ff1_dx", tm=512, tk=4096)
    (g_ff1,) = _matmul(h2, df, "tn", name="mm_ff1_dw", tm=512, tk=4096)
    dx1, dmo, dg1, dsc2, dsh2 = _resid_norm_bwd(x1, mo, g1, sc2, sh2, dh2, dx2)
    (dmixed,) = _matmul(dmo, w_out, "nt", name="mm_out_dx")
    (g_out,) = _matmul(mixed, dmo, "tn", name="mm_out_dw", tm=512, tk=4096)
    dproj, dba, dbb = _mix_bwd(proj, ba, bb, dmixed)
    (dy_a,) = _matmul(dba, w_bg, "nt", name="mm_branch_gm_dx")
    (g_bg,) = _matmul(y_a, dba, "tn", name="mm_branch_gm_dw", tm=512, tk=4096)
    (dy_b,) = _matmul(dbb, w_bs, "nt", name="mm_branch_ssm_dx")
    (g_bs,) = _matmul(y_b, dbb, "tn", name="mm_branch_ssm_dw", tm=512, tk=4096)
    dproj, d_gm_norm, d_ws, d_bs_t = _sgu_bwd(proj, gm_norm_w, ws_b, bs_t, dy_a, dproj)
    dy_ssd, dproj, d_ssm_norm = _gatenorm_bwd(y_ssd, proj, ssm_norm_w, dy_b, dproj)
    dxs, dbm, dcm, ddt, dcum, dcum_t, d_dsk = _ssd_bwd(xbc, dt, cum, cum_t, dsk, states, dy_ssd)
    dproj, d_dtb, d_alog = _dt_bwd(proj, dtb, alog, ddt, dcum, dcum_t, dproj)
    dproj, dw_x, db_x = _conv_bwd(proj, conv_w, conv_b, dxs, dproj, 0, "conv_bwd_x")
    dproj, dw_b, db_b = _conv_bwd(proj, conv_w, conv_b, dbm, dproj, DI, "conv_bwd_b")
    dproj, dw_c, db_c = _conv_bwd(proj, conv_w, conv_b, dcm, dproj, DI + D, "conv_bwd_c")
    d_conv_w = jnp.concatenate([dw_x, dw_b, dw_c], axis=1)
    d_conv_b = jnp.concatenate([db_x, db_b, db_c], axis=1)
    (dh1,) = _matmul(dproj, w_in_r, "nt", name="mm_proj_dx", tk=3456)
    (g_in_r,) = _matmul(h1, dproj, "tn", name="mm_proj_dw", tm=512, tn=1152, tk=4096)
    grad_x, dsc1, dsh1 = _prenorm_bwd(x, sc1, sh1, dh1, dx1)

    dmod = jnp.concatenate([dsh1, dsc1, dg1, dsh2, dsc2, dg2], axis=1)
    small = dict(gm_ws=d_ws, gm_norm_w=d_gm_norm, gm_bs=d_bs_t.T, conv_w=d_conv_w, conv_b=d_conv_b,
                 dt_bias=d_dtb[:, :NHEAD], a_log=d_alog[:, :NHEAD], d_skip=d_dsk[:, :NHEAD],
                 ssm_norm_w=d_ssm_norm, final_norm_w=dfw, dmod=dmod, loss=loss8[:1, :1])
    big = dict(w_in_r=g_in_r, w_bg=g_bg, w_bs=g_bs, w_out=g_out, w_ff1=g_ff1, w_ff2=g_ff2)
    return grad_x, small, big


SMALL_KEYS = ("gm_ws", "gm_norm_w", "gm_bs", "conv_w", "conv_b", "dt_bias", "a_log", "d_skip", "ssm_norm_w",
              "final_norm_w", "dmod", "loss")
SMALL_SHAPES = ((GM_G, Q, Q), (1, D), (GM_G, Q), (4, CONV), (1, CONV), (1, NHEAD), (1, NHEAD), (1, NHEAD), (1, DI),
                (1, D), (1, 6 * D), (1, 1))


def _reorder_w_in(w_full):
    k = w_full.shape[0]
    return jnp.concatenate([w_full[:, :8192], w_full[:, 8224:], w_full[:, 8192:8224],
                            jnp.zeros((k, W_IN_R - W_IN), w_full.dtype)], axis=1)


def _restore_w_in(g_r):
    return jnp.concatenate([g_r[:, :8192], g_r[:, OFF_DT:OFF_DT + NHEAD], g_r[:, 8192:OFF_DT]], axis=1)


def kernel(x, c, w_mod, b_mod, w_in, gm_norm_w, gm_ws, gm_bs, conv_w, conv_b, dt_bias, a_log, d_skip, ssm_norm_w, w_branch_gm, w_branch_ssm, w_out, w_ff1, w_ff2, final_norm_w, loss_target, m_w_mod, m_b_mod, m_w_in, m_gm_norm_w, m_gm_ws, m_gm_bs, m_conv_w, m_conv_b, m_dt_bias, m_a_log, m_d_skip, m_ssm_norm_w, m_w_branch_gm, m_w_branch_ssm, m_w_out, m_w_ff1, m_w_ff2, m_final_norm_w, v_w_mod, v_b_mod, v_w_in, v_gm_norm_w, v_gm_ws, v_gm_bs, v_conv_w, v_conv_b, v_dt_bias, v_a_log, v_d_skip, v_ssm_norm_w, v_w_branch_gm, v_w_branch_ssm, v_w_out, v_w_ff1, v_w_ff2, v_final_norm_w):
    ax, ay, ac = lax.axis_index("x"), lax.axis_index("y"), lax.axis_index("c")
    chip = 2 * ax + ay
    dev = 2 * chip + ac
    seq = x.shape[1]
    nmod = w_mod.shape[2]

    first = jnp.concatenate([c, conv_w[0], jnp.zeros((3, D), F32)], axis=0)
    first_all = _all_gather_small(first, "all_gather_cond")
    c8 = first_all[:, 0, :]
    conv_w_full = jnp.concatenate([first_all[2 * k, 1:5, :] for k in range(N_CHIP)], axis=1)
    b_sl = lax.dynamic_slice(b_mod, (0, chip * nmod), (1, nmod))
    mod_part = _mod_fwd(c8, w_mod[0], b_sl)
    mod_all = _all_gather_small(mod_part, "all_gather_mod")
    mod = jnp.concatenate([lax.dynamic_slice(mod_all, (2 * k, dev, 0), (1, 1, nmod))[0] for k in range(N_CHIP)], axis=1)

    packed = jnp.concatenate([w_branch_gm[0], w_branch_ssm[0], w_out[0], w_ff2[0], w_ff1[0]], axis=0).astype(BF16)
    g_in, g_rest = _gather_weights([w_in[0].astype(BF16), packed], chip)
    w_in_r = _reorder_w_in(jnp.transpose(g_in, (1, 0, 2)).reshape(D, W_IN))
    r_bg, r_bs, r_out, r_ff2 = D // N_CHIP, DI // N_CHIP, D // N_CHIP, DFF // N_CHIP
    o1, o2, o3, o4 = r_bg, r_bg + r_bs, r_bg + r_bs + r_out, r_bg + r_bs + r_out + r_ff2
    w_bg_f = g_rest[:, :o1].reshape(D, D)
    w_bs_f = g_rest[:, o1:o2].reshape(DI, D)
    w_out_f = g_rest[:, o2:o3].reshape(D, D)
    w_ff2_f = g_rest[:, o3:o4].reshape(DFF, D)
    w_ff1_f = jnp.transpose(g_rest[:, o4:], (1, 0, 2)).reshape(D, DFF)

    grad_x, small, big = _local_step(
        x[0], loss_target[0], mod, w_in_r, w_bg_f, w_bs_f, w_out_f, w_ff1_f, w_ff2_f, gm_norm_w, gm_ws[0], gm_bs[0],
        conv_w_full, conv_b, dt_bias, a_log, d_skip, ssm_norm_w, final_norm_w.reshape(1, D))

    small_all = _all_gather_small(_pack([small[k] for k in SMALL_KEYS]), "all_gather_small_grads")
    small_sum = _sum_leading(small_all, "sum_small_grads")
    s_ws, s_gnw, s_bs, s_cw, s_cb, s_dtb, s_alog, s_dsk, s_snw, s_fnw, s_bmod, s_loss = _unpack(small_sum, SMALL_SHAPES)
    dmod_all = jnp.stack([_unpack(small_all[k], SMALL_SHAPES)[10][0] for k in range(N_DEV)], axis=0)
    g_w_mod = _mod_wgrad(c8, lax.dynamic_slice(dmod_all, (0, chip * nmod), (N_DEV, nmod)))
    g_conv_w = lax.dynamic_slice(s_cw, (0, chip * (CONV // N_CHIP)), (4, CONV // N_CHIP))

    g_in_full = _restore_w_in(big["w_in_r"])
    send_in = jnp.transpose(g_in_full.reshape(D, N_CHIP, W_IN // N_CHIP), (1, 0, 2)).astype(BF16)
    send_rest = jnp.concatenate([
        big["w_bg"].reshape(N_CHIP, r_bg, D), big["w_bs"].reshape(N_CHIP, r_bs, D), big["w_out"].reshape(N_CHIP, r_out, D),
        big["w_ff2"].reshape(N_CHIP, r_ff2, D), jnp.transpose(big["w_ff1"].reshape(D, N_CHIP, D), (1, 0, 2))], axis=1).astype(BF16)
    g_w_in, g_rest_sh = _reduce_scatter([send_in, send_rest], ac)
    g_w_bg, g_w_bs, g_w_out, g_w_ff2, g_w_ff1 = g_rest_sh[:o1], g_rest_sh[o1:o2], g_rest_sh[o2:o3], g_rest_sh[o3:o4], g_rest_sh[o4:]

    def adam_big(w, g, m, v, name):
        d, m2, v2 = _adam(w.reshape(g.shape), g, m.reshape(g.shape), v.reshape(g.shape), name)
        return g.reshape(w.shape), d.reshape(w.shape), m2.reshape(w.shape), v2.reshape(w.shape)

    res = {}
    res["w_mod"] = adam_big(w_mod, g_w_mod, m_w_mod, v_w_mod, "adam_w_mod")
    res["w_in"] = adam_big(w_in, g_w_in, m_w_in, v_w_in, "adam_w_in")
    res["w_branch_gm"] = adam_big(w_branch_gm, g_w_bg, m_w_branch_gm, v_w_branch_gm, "adam_w_branch_gm")
    res["w_branch_ssm"] = adam_big(w_branch_ssm, g_w_bs, m_w_branch_ssm, v_w_branch_ssm, "adam_w_branch_ssm")
    res["w_out"] = adam_big(w_out, g_w_out, m_w_out, v_w_out, "adam_w_out")
    res["w_ff1"] = adam_big(w_ff1, g_w_ff1, m_w_ff1, v_w_ff1, "adam_w_ff1")
    res["w_ff2"] = adam_big(w_ff2, g_w_ff2, m_w_ff2, v_w_ff2, "adam_w_ff2")

    names = ("b_mod", "gm_norm_w", "gm_ws", "gm_bs", "conv_w", "conv_b", "dt_bias", "a_log", "d_skip", "ssm_norm_w", "final_norm_w")
    ws = (b_mod, gm_norm_w, gm_ws, gm_bs, conv_w, conv_b, dt_bias, a_log, d_skip, ssm_norm_w, final_norm_w)
    ms = (m_b_mod, m_gm_norm_w, m_gm_ws, m_gm_bs, m_conv_w, m_conv_b, m_dt_bias, m_a_log, m_d_skip, m_ssm_norm_w, m_final_norm_w)
    vs = (v_b_mod, v_gm_norm_w, v_gm_ws, v_gm_bs, v_conv_w, v_conv_b, v_dt_bias, v_a_log, v_d_skip, v_ssm_norm_w, v_final_norm_w)
    gs = (s_bmod, s_gnw, s_ws, s_bs, g_conv_w, s_cb, s_dtb, s_alog, s_dsk, s_snw, s_fnw)
    gs = [g.reshape(w.shape) for g, w in zip(gs, ws)]
    shapes = [w.shape for w in ws]
    d_p, m_p, v_p = _adam(_pack(ws), _pack(gs), _pack(ms), _pack(vs), "adam_small")
    for name, g, d, m2, v2 in zip(names, gs, _unpack(d_p, shapes), _unpack(m_p, shapes), _unpack(v_p, shapes)):
        res[name] = (g, d, m2, v2)

    order = ("w_mod", "b_mod", "w_in", "gm_norm_w", "gm_ws", "gm_bs", "conv_w", "conv_b", "dt_bias", "a_log", "d_skip",
             "ssm_norm_w", "w_branch_gm", "w_branch_ssm", "w_out", "w_ff1", "w_ff2", "final_norm_w")
    loss = s_loss.reshape(())
    return (loss, grad_x.reshape(x.shape), *[res[k][0] for k in order], *[res[k][1] for k in order],
            *[res[k][2] for k in order], *[res[k][3] for k in order])
```

```python
import functools

import jax
import jax.numpy as jnp
from jax import lax
from jax.experimental import pallas as pl
from jax.experimental.pallas import tpu as pltpu

F32 = jnp.float32
BF16 = jnp.bfloat16
MESH = pl.DeviceIdType.MESH
HIGHEST = lax.Precision.HIGHEST

D = 1024
EPS = 1e-6
Q = 128
GM_G = 8
NHEAD = 32
NGRP = 8
DI = 2048
CONV = 4096
DFF = 4096
W_IN = 10272
W_IN_R = 10368
OFF_Z, OFF_XBC, OFF_GA, OFF_DT = 2048, 4096, 8192, 10240
N_CHIP = 4
N_DEV = 8
AG_CHUNKS = 4
D2D_SPLIT = 4

ADAM_LR, ADAM_B1, ADAM_B2, ADAM_EPS, ADAM_WD, ADAM_STEP = 0.001, 0.9, 0.999, 1e-08, 0.01, 10

ANY = pl.BlockSpec(memory_space=pl.ANY)
VMEM = pl.BlockSpec(memory_space=pltpu.VMEM)


def _full(shape):
    return pl.BlockSpec(shape, lambda *_: (0,) * len(shape))


def _pick(n, prefs):
    for p in prefs:
        if n % p == 0:
            return p
    return n


def _rms(x):
    return x * lax.rsqrt(jnp.mean(x * x, axis=-1, keepdims=True) + EPS)


def _modnorm(x, sc, sh):
    return _rms(x) * (1.0 + sc) + sh


def _sgu_pre(u, v, w):
    return jax.nn.gelu(u), _rms(jax.nn.gelu(v)) * w


def _gatenorm(y, z, w):
    g = y * jax.nn.silu(z)
    return _rms(g) * w


def _mix(ga, gb, ba, bb):
    return jax.nn.sigmoid(ga) * ba + jax.nn.sigmoid(gb) * bb


def _loss_tile(x1, fo, g2, fw, t):
    x2 = x1 + g2 * fo
    y = _rms(x2) * fw
    err = jnp.square(y - t)
    return 0.5 * jnp.sum(jnp.mean(err, axis=-1))


def _tril(n):
    r = lax.broadcasted_iota(jnp.int32, (n, n), 0)
    c = lax.broadcasted_iota(jnp.int32, (n, n), 1)
    return r >= c


def _dt_prep(dtr, dtb, alog):
    dt = jax.nn.softplus(dtr + dtb)
    a = dt * (-jnp.exp(alog))
    ones = _tril(Q).astype(F32)
    cum = jnp.dot(ones, a, precision=HIGHEST, preferred_element_type=F32)
    cum_t = lax.dot_general(a, ones, (((0,), (1,)), ((), ())), precision=HIGHEST, preferred_element_type=F32)
    return dt, cum, cum_t


def _ssd_group(x0, x1, s0, s1, bm, cm, dt, cum, cum_t, dsk, grp):
    lane = lax.broadcasted_iota(jnp.int32, (1, 128), 1)
    sub = lax.broadcasted_iota(jnp.int32, (128, 1), 0)
    future = jnp.where(_tril(Q), 0.0, -jnp.inf)
    half = lane < 64
    half_rows = sub < 64
    bmb = bm.astype(BF16)
    cmb = cm.astype(BF16)
    cb = lax.dot_general(cmb, bmb, (((1,), (1,)), ((), ())), preferred_element_type=F32)

    def col(v, h):
        return jnp.sum(jnp.where(lane == h, v, 0.0), axis=1, keepdims=True)

    def row(v, h):
        return jnp.sum(jnp.where(sub == h, v, 0.0), axis=0, keepdims=True)

    def last(c):
        return jnp.sum(jnp.where(sub == Q - 1, c, 0.0), axis=0, keepdims=True)

    outs, states = [], []
    for p, (xp, sp) in enumerate(((x0, s0), (x1, s1))):
        h_a = 4 * grp + 2 * p
        h_b = h_a + 1
        dt_a, dt_b = col(dt, h_a), col(dt, h_b)
        cum_a, cum_b = col(cum, h_a), col(cum, h_b)
        row_a, row_b = row(cum_t, h_a), row(cum_t, h_b)
        last_a, last_b = last(cum_a), last(cum_b)
        xdt = xp * jnp.where(half, dt_a, dt_b)
        xdtb = xdt.astype(BF16)
        m_a = (cb * jnp.exp(cum_a - row_a + future)).astype(BF16)
        m_b = (cb * jnp.exp(cum_b - row_b + future)).astype(BF16)
        y_intra = jnp.where(half, jnp.dot(m_a, xdtb, preferred_element_type=F32),
                            jnp.dot(m_b, xdtb, preferred_element_type=F32))
        y_inter = lax.dot_general(cmb, sp.astype(BF16), (((1,), (1,)), ((), ())), preferred_element_type=F32)
        y_inter = y_inter * jnp.where(half, jnp.exp(cum_a), jnp.exp(cum_b))
        w_end = jnp.where(half, jnp.exp(last_a - cum_a), jnp.exp(last_b - cum_b))
        upd = lax.dot_general((xdt * w_end).astype(BF16), bmb, (((0,), (0,)), ((), ())), preferred_element_type=F32)
        states.append(sp * jnp.where(half_rows, jnp.exp(last_a), jnp.exp(last_b)) + upd)
        outs.append(y_intra + y_inter + xp * jnp.where(half, col(dsk, h_a), col(dsk, h_b)))
    return outs[0], outs[1], states[0], states[1]


def _matmul(a, b, mode, *, name, out_dtypes=(F32,), epi=None, epi_ins=(), tm=1024, tn=1024, tk=1024):
    if mode == "nn":
        (m, k), n = a.shape, b.shape[1]
    elif mode == "nt":
        (m, k), n = a.shape, b.shape[0]
    else:
        (k, m), n = a.shape, b.shape[1]
    tm, tn, tk = _pick(m, (tm, 512, 256, 128)), _pick(n, (tn, 1152, 1024, 512, 384, 256, 128)), _pick(k, (tk, 1152, 1024, 512, 256, 128))
    nk = k // tk
    if mode == "nn":
        a_spec = pl.BlockSpec((tm, tk), lambda i, j, kk: (i, kk))
        b_spec = pl.BlockSpec((tk, tn), lambda i, j, kk: (kk, j))
        dims = (((1,), (0,)), ((), ()))
    elif mode == "nt":
        a_spec = pl.BlockSpec((tm, tk), lambda i, j, kk: (i, kk))
        b_spec = pl.BlockSpec((tn, tk), lambda i, j, kk: (j, kk))
        dims = (((1,), (1,)), ((), ()))
    else:
        a_spec = pl.BlockSpec((tk, tm), lambda i, j, kk: (kk, i))
        b_spec = pl.BlockSpec((tk, tn), lambda i, j, kk: (kk, j))
        dims = (((0,), (0,)), ((), ()))
    o_spec = pl.BlockSpec((tm, tn), lambda i, j, kk: (i, j))
    n_epi, n_out = len(epi_ins), len(out_dtypes)

    def body(*refs):
        a_ref, b_ref = refs[0], refs[1]
        e_refs = refs[2:2 + n_epi]
        o_refs = refs[2 + n_epi:2 + n_epi + n_out]
        acc_ref = refs[-1]

        def finish(acc):
            outs = epi(acc, *[e[...] for e in e_refs]) if epi is not None else (acc,)
            for o_ref, val in zip(o_refs, outs):
                o_ref[...] = val.astype(o_ref.dtype)

        part = lax.dot_general(a_ref[...], b_ref[...], dims, preferred_element_type=F32)
        if nk == 1:
            finish(part)
        else:
            kk = pl.program_id(2)

            @pl.when(kk == 0)
            def _():
                acc_ref[...] = part

            @pl.when(kk > 0)
            def _():
                acc_ref[...] += part

            @pl.when(kk == nk - 1)
            def _():
                finish(acc_ref[...])

    return pl.pallas_call(
        body, name=name, grid=(m // tm, n // tn, nk),
        in_specs=[a_spec, b_spec] + [o_spec] * n_epi,
        out_specs=[o_spec] * n_out,
        out_shape=[jax.ShapeDtypeStruct((m, n), dt) for dt in out_dtypes],
        scratch_shapes=[pltpu.VMEM((tm, tn) if nk > 1 else (8, 128), F32)],
        compiler_params=pltpu.CompilerParams(dimension_semantics=("parallel", "parallel", "arbitrary")),
    )(a, b, *epi_ins)


def _row_tile(s):
    return _pick(s, (512, 256, 128))


def _prenorm(x, sc, sh):
    s = x.shape[0]
    tm = _row_tile(s)

    def body(x_ref, sc_ref, sh_ref, h_ref):
        h_ref[...] = _modnorm(x_ref[...], sc_ref[...], sh_ref[...]).astype(BF16)

    row = pl.BlockSpec((tm, D), lambda i: (i, 0))
    return pl.pallas_call(body, name="prenorm", grid=(s // tm,), in_specs=[row, _full((1, D)), _full((1, D))],
                          out_specs=row, out_shape=jax.ShapeDtypeStruct((s, D), BF16))(x, sc, sh)


def _prenorm_bwd(x, sc, sh, dh, dx_res):
    s = x.shape[0]
    tm = _row_tile(s)

    def body(x_ref, sc_ref, sh_ref, dh_ref, dr_ref, dx_ref, dsc_ref, dsh_ref):
        _, vjp = jax.vjp(_modnorm, x_ref[...], sc_ref[...], sh_ref[...])
        dx, dsc, dsh = vjp(dh_ref[...])
        dx_ref[...] = dr_ref[...] + dx

        @pl.when(pl.program_id(0) == 0)
        def _():
            dsc_ref[...] = jnp.zeros_like(dsc_ref)
            dsh_ref[...] = jnp.zeros_like(dsh_ref)

        dsc_ref[...] += dsc
        dsh_ref[...] += dsh

    row = pl.BlockSpec((tm, D), lambda i: (i, 0))
    vec = _full((1, D))
    return pl.pallas_call(
        body, name="prenorm_bwd", grid=(s // tm,), in_specs=[row, vec, vec, row, row], out_specs=[row, vec, vec],
        out_shape=[jax.ShapeDtypeStruct((s, D), F32), jax.ShapeDtypeStruct((1, D), F32), jax.ShapeDtypeStruct((1, D), F32)],
    )(x, sc, sh, dh, dx_res)


def _resid_norm(x, mo, g1, sc, sh):
    s = x.shape[0]
    tm = _row_tile(s)

    def body(x_ref, mo_ref, g_ref, sc_ref, sh_ref, x1_ref, h_ref):
        x1 = x_ref[...] + g_ref[...] * mo_ref[...]
        x1_ref[...] = x1
        h_ref[...] = _modnorm(x1, sc_ref[...], sh_ref[...]).astype(BF16)

    row = pl.BlockSpec((tm, D), lambda i: (i, 0))
    vec = _full((1, D))
    return pl.pallas_call(
        body, name="resid_norm", grid=(s // tm,), in_specs=[row, row, vec, vec, vec], out_specs=[row, row],
        out_shape=[jax.ShapeDtypeStruct((s, D), F32), jax.ShapeDtypeStruct((s, D), BF16)],
    )(x, mo, g1, sc, sh)


def _resid_norm_bwd(x1, mo, g1, sc, sh, dh, dx2):
    s = x1.shape[0]
    tm = _row_tile(s)

    def body(x1_ref, mo_ref, g_ref, sc_ref, sh_ref, dh_ref, dx2_ref, dx1_ref, dmo_ref, dg_ref, dsc_ref, dsh_ref):
        _, vjp = jax.vjp(_modnorm, x1_ref[...], sc_ref[...], sh_ref[...])
        dx, dsc, dsh = vjp(dh_ref[...])
        dx1 = dx2_ref[...] + dx
        dx1_ref[...] = dx1
        dmo_ref[...] = (dx1 * g_ref[...]).astype(BF16)

        @pl.when(pl.program_id(0) == 0)
        def _():
            dg_ref[...] = jnp.zeros_like(dg_ref)
            dsc_ref[...] = jnp.zeros_like(dsc_ref)
            dsh_ref[...] = jnp.zeros_like(dsh_ref)

        dg_ref[...] += jnp.sum(dx1 * mo_ref[...], axis=0, keepdims=True)
        dsc_ref[...] += dsc
        dsh_ref[...] += dsh

    row = pl.BlockSpec((tm, D), lambda i: (i, 0))
    vec = _full((1, D))
    vshape = jax.ShapeDtypeStruct((1, D), F32)
    return pl.pallas_call(
        body, name="resid_norm_bwd", grid=(s // tm,), in_specs=[row, row, vec, vec, vec, row, row],
        out_specs=[row, row, vec, vec, vec],
        out_shape=[jax.ShapeDtypeStruct((s, D), F32), jax.ShapeDtypeStruct((s, D), BF16), vshape, vshape, vshape],
    )(x1, mo, g1, sc, sh, dh, dx2)


def _loss_head(x1, fo, g2, fw, t):
    s = x1.shape[0]
    tm = _row_tile(s)

    def body(x1_ref, fo_ref, g_ref, fw_ref, t_ref, loss_ref, dx_ref, dfo_ref, dg_ref, dfw_ref):
        loss, (dx1, dfo, dg, dfw) = jax.value_and_grad(_loss_tile, argnums=(0, 1, 2, 3))(
            x1_ref[...], fo_ref[...], g_ref[...], fw_ref[...], t_ref[...])
        dx_ref[...] = dx1
        dfo_ref[...] = dfo.astype(BF16)

        @pl.when(pl.program_id(0) == 0)
        def _():
            loss_ref[...] = jnp.zeros_like(loss_ref)
            dg_ref[...] = jnp.zeros_like(dg_ref)
            dfw_ref[...] = jnp.zeros_like(dfw_ref)

        loss_ref[...] += jnp.full(loss_ref.shape, loss, F32)
        dg_ref[...] += dg
        dfw_ref[...] += dfw

    row = pl.BlockSpec((tm, D), lambda i: (i, 0))
    vec = _full((1, D))
    vshape = jax.ShapeDtypeStruct((1, D), F32)
    return pl.pallas_call(
        body, name="loss_head", grid=(s // tm,), in_specs=[row, row, vec, vec, row],
        out_specs=[_full((8, 128)), row, row, vec, vec],
        out_shape=[jax.ShapeDtypeStruct((8, 128), F32), jax.ShapeDtypeStruct((s, D), F32),
                   jax.ShapeDtypeStruct((s, D), BF16), vshape, vshape],
    )(x1, fo, g2, fw, t)


def _sgu_fwd(proj, norm_w, ws_b, bs_t):
    s = proj.shape[0]
    tm = _pick(s, (256, 128))

    def body(u_ref, v_ref, w_ref, ws_ref, bs_ref, y_ref):
        ug, vn = _sgu_pre(u_ref[...], v_ref[...], w_ref[...])
        vnb = vn.astype(BF16)
        for c in range(tm // Q):
            r = slice(c * Q, (c + 1) * Q)
            for g in range(GM_G):
                cs = slice(g * 128, (g + 1) * 128)
                sv = jnp.dot(ws_ref[g], vnb[r, cs], preferred_element_type=F32) + bs_ref[:, g:g + 1]
                y_ref[r, cs] = (ug[r, cs] * sv).astype(BF16)

    return pl.pallas_call(
        body, name="sgu_fwd", grid=(s // tm,),
        in_specs=[pl.BlockSpec((tm, D), lambda i: (i, 0)), pl.BlockSpec((tm, D), lambda i: (i, 1)),
                  _full((1, D)), _full((GM_G, Q, Q)), _full((Q, GM_G))],
        out_specs=pl.BlockSpec((tm, D), lambda i: (i, 0)),
        out_shape=jax.ShapeDtypeStruct((s, D), BF16),
    )(proj, proj, norm_w, ws_b, bs_t)


def _sgu_bwd(proj, norm_w, ws_b, bs_t, dy, dproj):
    s = proj.shape[0]
    tm = _pick(s, (256, 128))

    def body(u_ref, v_ref, w_ref, ws_ref, bs_ref, dy_ref, _, duv_ref, dw_ref, dws_ref, dbs_ref, dug_scr, dvn_scr):
        @pl.when(pl.program_id(0) == 0)
        def _():
            dw_ref[...] = jnp.zeros_like(dw_ref)
            dws_ref[...] = jnp.zeros_like(dws_ref)
            dbs_ref[...] = jnp.zeros_like(dbs_ref)

        (ug, vn), vjp = jax.vjp(_sgu_pre, u_ref[...], v_ref[...], w_ref[...])
        vnb = vn.astype(BF16)
        dy = dy_ref[...]
        causal = _tril(Q).astype(F32)
        for c in range(tm // Q):
            r = slice(c * Q, (c + 1) * Q)
            for g in range(GM_G):
                cs = slice(g * 128, (g + 1) * 128)
                blk = vnb[r, cs]
                sv = jnp.dot(ws_ref[g], blk, preferred_element_type=F32) + bs_ref[:, g:g + 1]
                dug_scr[r, cs] = dy[r, cs] * sv
                dsv = dy[r, cs] * ug[r, cs]
                dsvb = dsv.astype(BF16)
                dws_ref[g] += causal * lax.dot_general(dsvb, blk, (((1,), (1,)), ((), ())), preferred_element_type=F32)
                dbs_ref[:, g:g + 1] += jnp.sum(dsv, axis=1, keepdims=True)
                dvn_scr[r, cs] = lax.dot_general(ws_ref[g], dsvb, (((0,), (0,)), ((), ())), preferred_element_type=F32)
        du, dv, dw = vjp((dug_scr[...], dvn_scr[...]))
        duv_ref[:, :D] = du.astype(BF16)
        duv_ref[:, D:] = dv.astype(BF16)
        dw_ref[...] += dw

    return pl.pallas_call(
        body, name="sgu_bwd", grid=(s // tm,),
        in_specs=[pl.BlockSpec((tm, D), lambda i: (i, 0)), pl.BlockSpec((tm, D), lambda i: (i, 1)),
                  _full((1, D)), _full((GM_G, Q, Q)), _full((Q, GM_G)), pl.BlockSpec((tm, D), lambda i: (i, 0)), ANY],
        out_specs=[pl.BlockSpec((tm, 2 * D), lambda i: (i, 0)), _full((1, D)), _full((GM_G, Q, Q)), _full((Q, GM_G))],
        out_shape=[jax.ShapeDtypeStruct(dproj.shape, BF16), jax.ShapeDtypeStruct((1, D), F32),
                   jax.ShapeDtypeStruct((GM_G, Q, Q), F32), jax.ShapeDtypeStruct((Q, GM_G), F32)],
        scratch_shapes=[pltpu.VMEM((tm, D), F32), pltpu.VMEM((tm, D), F32)],
        input_output_aliases={6: 0},
    )(proj, proj, norm_w, ws_b, bs_t, dy, dproj)


def _gatenorm_fwd(y_ssd, proj, norm_w):
    s = y_ssd.shape[0]
    tm = _pick(s, (256, 128))
    gw = DI // NGRP

    def body(y_ref, z_ref, w_ref, o_ref):
        for g in range(NGRP):
            cs = slice(g * gw, (g + 1) * gw)
            o_ref[:, cs] = _gatenorm(y_ref[:, cs], z_ref[:, cs], w_ref[:, cs]).astype(BF16)

    return pl.pallas_call(
        body, name="gatenorm_fwd", grid=(s // tm,),
        in_specs=[pl.BlockSpec((tm, DI), lambda i: (i, 0)), pl.BlockSpec((tm, DI), lambda i: (i, OFF_Z // DI)), _full((1, DI))],
        out_specs=pl.BlockSpec((tm, DI), lambda i: (i, 0)),
        out_shape=jax.ShapeDtypeStruct((s, DI), BF16),
    )(y_ssd, proj, norm_w)


def _gatenorm_bwd(y_ssd, proj, norm_w, dyb, dproj):
    s = y_ssd.shape[0]
    tm = _pick(s, (256, 128))
    gw = DI // NGRP

    def body(y_ref, z_ref, w_ref, d_ref, _, dy_ref, dz_ref, dw_ref):
        @pl.when(pl.program_id(0) == 0)
        def _():
            dw_ref[...] = jnp.zeros_like(dw_ref)

        for g in range(NGRP):
            cs = slice(g * gw, (g + 1) * gw)
            _, vjp = jax.vjp(_gatenorm, y_ref[:, cs], z_ref[:, cs], w_ref[:, cs])
            dy, dz, dw = vjp(d_ref[:, cs])
            dy_ref[:, cs] = dy
            dz_ref[:, cs] = dz.astype(BF16)
            dw_ref[:, cs] += dw

    blk = pl.BlockSpec((tm, DI), lambda i: (i, 0))
    zblk = pl.BlockSpec((tm, DI), lambda i: (i, OFF_Z // DI))
    return pl.pallas_call(
        body, name="gatenorm_bwd", grid=(s // tm,),
        in_specs=[blk, zblk, _full((1, DI)), blk, ANY],
        out_specs=[blk, zblk, _full((1, DI))],
        out_shape=[jax.ShapeDtypeStruct((s, DI), F32), jax.ShapeDtypeStruct(dproj.shape, BF16), jax.ShapeDtypeStruct((1, DI), F32)],
        input_output_aliases={4: 1},
    )(y_ssd, proj, norm_w, dyb, dproj)


def _mix_fwd(proj, ba, bb):
    s = proj.shape[0]
    tm = _row_tile(s)
    gb0 = OFF_GA // D

    def body(ga_ref, gb_ref, ba_ref, bb_ref, o_ref):
        o_ref[...] = _mix(ga_ref[...], gb_ref[...], ba_ref[...], bb_ref[...]).astype(BF16)

    row = pl.BlockSpec((tm, D), lambda i: (i, 0))
    return pl.pallas_call(
        body, name="mix_fwd", grid=(s // tm,),
        in_specs=[pl.BlockSpec((tm, D), lambda i: (i, gb0)), pl.BlockSpec((tm, D), lambda i: (i, gb0 + 1)), row, row],
        out_specs=row, out_shape=jax.ShapeDtypeStruct((s, D), BF16),
    )(proj, proj, ba, bb)


def _mix_bwd(proj, ba, bb, dmixed):
    s = proj.shape[0]
    tm = _row_tile(s)
    gb0 = OFF_GA // D

    def body(ga_ref, gb_ref, ba_ref, bb_ref, d_ref, dg_ref, dba_ref, dbb_ref):
        _, vjp = jax.vjp(_mix, ga_ref[...], gb_ref[...], ba_ref[...], bb_ref[...])
        dga, dgb, dba, dbb = vjp(d_ref[...])
        dg_ref[:, :D] = dga.astype(BF16)
        dg_ref[:, D:] = dgb.astype(BF16)
        dba_ref[...] = dba.astype(BF16)
        dbb_ref[...] = dbb.astype(BF16)

    row = pl.BlockSpec((tm, D), lambda i: (i, 0))
    return pl.pallas_call(
        body, name="mix_bwd", grid=(s // tm,),
        in_specs=[pl.BlockSpec((tm, D), lambda i: (i, gb0)), pl.BlockSpec((tm, D), lambda i: (i, gb0 + 1)), row, row, row],
        out_specs=[pl.BlockSpec((tm, 2 * D), lambda i: (i, OFF_GA // (2 * D))), row, row],
        out_shape=[jax.ShapeDtypeStruct((s, W_IN_R), BF16), jax.ShapeDtypeStruct((s, D), BF16), jax.ShapeDtypeStruct((s, D), BF16)],
    )(proj, proj, ba, bb, dmixed)


CONV_TC = 1024


def _conv_fwd(proj, conv_w, conv_b):
    s = proj.shape[0]
    tm = _row_tile(s)
    cb0 = OFF_XBC // CONV_TC

    def body(x_ref, halo_ref, w_ref, b_ref, o_ref):
        halo = jnp.where(pl.program_id(0) > 0, halo_ref[...], 0.0)
        ext = jnp.concatenate([halo, x_ref[...]], axis=0)
        acc = jnp.broadcast_to(b_ref[...], (tm, CONV_TC))
        for k in range(4):
            shifted = ext if k == 3 else pltpu.roll(ext, 3 - k, 0)
            acc = acc + w_ref[k:k + 1, :] * shifted[8:, :]
        o_ref[...] = jax.nn.silu(acc)

    return pl.pallas_call(
        body, name="conv_fwd", grid=(s // tm, CONV // CONV_TC),
        in_specs=[pl.BlockSpec((tm, CONV_TC), lambda i, j: (i, cb0 + j)),
                  pl.BlockSpec((8, CONV_TC), lambda i, j: (jnp.maximum(i * (tm // 8) - 1, 0), cb0 + j)),
                  pl.BlockSpec((4, CONV_TC), lambda i, j: (0, j)), pl.BlockSpec((1, CONV_TC), lambda i, j: (0, j))],
        out_specs=pl.BlockSpec((tm, CONV_TC), lambda i, j: (i, j)),
        out_shape=jax.ShapeDtypeStruct((s, CONV), F32),
    )(proj, proj, conv_w, conv_b)


def _conv_bwd(proj, conv_w, conv_b, dact, dproj, col0, name):
    s, width = dact.shape
    tm = _row_tile(s)
    nt = s // tm
    c0 = col0 // CONV_TC
    cb0 = OFF_XBC // CONV_TC + c0

    def body(x_ref, prev_ref, next_ref, d_ref, dnext_ref, w_ref, b_ref, _, dx_ref, dw_ref, db_ref):
        i = pl.program_id(1)
        prev = jnp.where(i > 0, prev_ref[...], 0.0)
        ext = jnp.concatenate([prev, x_ref[...], next_ref[...]], axis=0)
        dext = jnp.concatenate([d_ref[...], jnp.where(i < nt - 1, dnext_ref[...], 0.0)], axis=0)
        pre = jnp.broadcast_to(b_ref[...], (tm + 8, CONV_TC))
        taps = []
        for k in range(4):
            shifted = (ext if k == 3 else pltpu.roll(ext, 3 - k, 0))[8:, :]
            taps.append(shifted)
            pre = pre + w_ref[k:k + 1, :] * shifted
        sig = jax.nn.sigmoid(pre)
        dpre = dext * (sig * (1.0 + pre * (1.0 - sig)))
        dx = jnp.zeros((tm, CONV_TC), F32)
        for k in range(4):
            shifted = dpre if k == 3 else pltpu.roll(dpre, tm + 8 - (3 - k), 0)
            dx = dx + w_ref[k:k + 1, :] * shifted[:tm, :]
        dx_ref[...] = dx.astype(BF16)

        @pl.when(i == 0)
        def _():
            dw_ref[...] = jnp.zeros_like(dw_ref)
            db_ref[...] = jnp.zeros_like(db_ref)

        dtile = dpre[:tm, :]
        for k in range(4):
            dw_ref[k:k + 1, :] += jnp.sum(dtile * taps[k][:tm, :], axis=0, keepdims=True)
        db_ref[...] += jnp.sum(dtile, axis=0, keepdims=True)

    r8 = tm // 8
    return pl.pallas_call(
        body, name=name, grid=(width // CONV_TC, nt),
        in_specs=[pl.BlockSpec((tm, CONV_TC), lambda j, i: (i, cb0 + j)),
                  pl.BlockSpec((8, CONV_TC), lambda j, i: (jnp.maximum(i * r8 - 1, 0), cb0 + j)),
                  pl.BlockSpec((8, CONV_TC), lambda j, i: (jnp.minimum((i + 1) * r8, nt * r8 - 1), cb0 + j)),
                  pl.BlockSpec((tm, CONV_TC), lambda j, i: (i, j)),
                  pl.BlockSpec((8, CONV_TC), lambda j, i: (jnp.minimum((i + 1) * r8, nt * r8 - 1), j)),
                  pl.BlockSpec((4, CONV_TC), lambda j, i: (0, c0 + j)), pl.BlockSpec((1, CONV_TC), lambda j, i: (0, c0 + j)), ANY],
        out_specs=[pl.BlockSpec((tm, CONV_TC), lambda j, i: (i, cb0 + j)),
                   pl.BlockSpec((4, CONV_TC), lambda j, i: (0, j)), pl.BlockSpec((1, CONV_TC), lambda j, i: (0, j))],
        out_shape=[jax.ShapeDtypeStruct(dproj.shape, BF16), jax.ShapeDtypeStruct((4, width), F32), jax.ShapeDtypeStruct((1, width), F32)],
        input_output_aliases={7: 0},
    )(proj, proj, proj, dact, dact, conv_w, conv_b, dproj)


def _dt_fwd(proj, dtb, alog):
    s = proj.shape[0]
    nc = s // Q

    def body(r_ref, b_ref, a_ref, dt_ref, cum_ref, cumt_ref):
        dt, cum, cum_t = _dt_prep(r_ref[...], b_ref[...], a_ref[...])
        dt_ref[...] = dt
        cum_ref[...] = cum
        cumt_ref[...] = cum_t

    blk = pl.BlockSpec((Q, 128), lambda n: (n, 0))
    return pl.pallas_call(
        body, name="dt_fwd", grid=(nc,),
        in_specs=[pl.BlockSpec((Q, 128), lambda n: (n, OFF_DT // 128)), _full((1, 128)), _full((1, 128))],
        out_specs=[blk, blk, pl.BlockSpec((None, 128, Q), lambda n: (n, 0, 0))],
        out_shape=[jax.ShapeDtypeStruct((s, 128), F32), jax.ShapeDtypeStruct((s, 128), F32), jax.ShapeDtypeStruct((nc, 128, Q), F32)],
    )(proj, dtb, alog)


def _dt_bwd(proj, dtb, alog, ddt, dcum, dcumt, dproj):
    s = proj.shape[0]
    nc = s // Q

    def body(r_ref, b_ref, a_ref, ddt_ref, dcum_ref, dcumt_ref, _, dr_ref, db_ref, da_ref):
        _, vjp = jax.vjp(_dt_prep, r_ref[...], b_ref[...], a_ref[...])
        dr, db, da = vjp((ddt_ref[...], dcum_ref[...], dcumt_ref[...]))
        dr_ref[...] = dr.astype(BF16)

        @pl.when(pl.program_id(0) == 0)
        def _():
            db_ref[...] = jnp.zeros_like(db_ref)
            da_ref[...] = jnp.zeros_like(da_ref)

        db_ref[...] += db
        da_ref[...] += da

    blk = pl.BlockSpec((Q, 128), lambda n: (n, 0))
    pblk = pl.BlockSpec((Q, 128), lambda n: (n, OFF_DT // 128))
    return pl.pallas_call(
        body, name="dt_bwd", grid=(nc,),
        in_specs=[pblk, _full((1, 128)), _full((1, 128)), blk, blk, pl.BlockSpec((None, 128, Q), lambda n: (n, 0, 0)), ANY],
        out_specs=[pblk, _full((1, 128)), _full((1, 128))],
        out_shape=[jax.ShapeDtypeStruct(dproj.shape, BF16), jax.ShapeDtypeStruct((1, 128), F32), jax.ShapeDtypeStruct((1, 128), F32)],
        input_output_aliases={6: 0},
    )(proj, dtb, alog, ddt, dcum, dcumt, dproj)


def _ssd_specs(chunk_of):
    gs = SSD_GPS
    xs = pl.BlockSpec((Q, 256 * gs), lambda n, g: (chunk_of(n), g))
    bm = pl.BlockSpec((Q, 128 * gs), lambda n, g: (chunk_of(n), DI // (128 * gs) + g))
    cm = pl.BlockSpec((Q, 128 * gs), lambda n, g: (chunk_of(n), (DI + D) // (128 * gs) + g))
    per_chunk = pl.BlockSpec((Q, 128), lambda n, g: (chunk_of(n), 0))
    cum_t = pl.BlockSpec((None, 128, Q), lambda n, g: (chunk_of(n), 0, 0))
    state = pl.BlockSpec((None, 256 * gs, 128), lambda n, g: (chunk_of(n), g, 0))
    vec = pl.BlockSpec((1, 128), lambda n, g: (0, 0))
    return xs, bm, cm, per_chunk, cum_t, state, vec


SSD_GPS = 8


def _aligned(v, m):
    return v if isinstance(v, int) else pl.multiple_of(v, m)


def _ssd_fwd(xbc, dt, cum, cum_t, dsk):
    s = xbc.shape[0]
    nc = s // Q
    xs, bm, cm, per_chunk, cumt_spec, state_spec, vec = _ssd_specs(lambda n: n)
    gs = SSD_GPS

    def body(x_ref, b_ref, c_ref, dt_ref, cum_ref, cumt_ref, dsk_ref, y_ref, st_ref, carry):
        n, gstep = pl.program_id(0), (0 if gs == NGRP else pl.program_id(1))
        rows = pl.ds(_aligned(gstep * (256 * gs), 256 * gs), 256 * gs)

        @pl.when(n == 0)
        def _():
            carry[rows, :] = jnp.zeros((256 * gs, 128), F32)

        st_ref[...] = carry[rows, :]
        for k in range(gs):
            xo, so, bo = 256 * k, 256 * k, 128 * k
            y0, y1, n0, n1 = _ssd_group(
                x_ref[:, xo:xo + 128], x_ref[:, xo + 128:xo + 256], st_ref[so:so + 128, :], st_ref[so + 128:so + 256, :],
                b_ref[:, bo:bo + 128], c_ref[:, bo:bo + 128], dt_ref[...], cum_ref[...], cumt_ref[...], dsk_ref[...],
                gstep * gs + k)
            y_ref[:, xo:xo + 128] = y0
            y_ref[:, xo + 128:xo + 256] = y1
            base = gstep * (256 * gs) + so
            carry[pl.ds(_aligned(base, 128), 128), :] = n0
            carry[pl.ds(_aligned(base + 128, 128), 128), :] = n1

    return pl.pallas_call(
        body, name="ssd_fwd", grid=(nc, NGRP // gs),
        in_specs=[xs, bm, cm, per_chunk, per_chunk, cumt_spec, vec],
        out_specs=[xs, state_spec],
        out_shape=[jax.ShapeDtypeStruct((s, DI), F32), jax.ShapeDtypeStruct((nc, DI, 128), F32)],
        scratch_shapes=[pltpu.VMEM((DI, 128), F32)],
    )(xbc, xbc, xbc, dt, cum, cum_t, dsk)


def _ssd_bwd(xbc, dt, cum, cum_t, dsk, states, dy):
    s = xbc.shape[0]
    nc = s // Q
    xs, bm, cm, per_chunk, cumt_spec, state_spec, vec = _ssd_specs(lambda n: nc - 1 - n)
    gs = SSD_GPS

    def body(x_ref, b_ref, c_ref, dt_ref, cum_ref, cumt_ref, dsk_ref, st_ref, dy_ref,
             dx_ref, db_ref, dc_ref, ddt_ref, dcum_ref, dcumt_ref, ddsk_ref, carry):
        n, gstep = pl.program_id(0), (0 if gs == NGRP else pl.program_id(1))
        rows = pl.ds(_aligned(gstep * (256 * gs), 256 * gs), 256 * gs)

        @pl.when(n == 0)
        def _():
            carry[rows, :] = jnp.zeros((256 * gs, 128), F32)

        def zero_skip_sum():
            ddsk_ref[...] = jnp.zeros_like(ddsk_ref)

        def zero_chunk_sums():
            ddt_ref[...] = jnp.zeros_like(ddt_ref)
            dcum_ref[...] = jnp.zeros_like(dcum_ref)
            dcumt_ref[...] = jnp.zeros_like(dcumt_ref)

        if isinstance(gstep, int):
            pl.when(n == 0)(zero_skip_sum)
            zero_chunk_sums()
        else:
            pl.when((n == 0) & (gstep == 0))(zero_skip_sum)
            pl.when(gstep == 0)(zero_chunk_sums)

        for k in range(gs):
            xo, so, bo = 256 * k, 256 * k, 128 * k
            base = gstep * (256 * gs) + so
            lo = pl.ds(_aligned(base, 128), 128)
            hi = pl.ds(_aligned(base + 128, 128), 128)
            fn = functools.partial(_ssd_group, grp=gstep * gs + k)
            _, vjp = jax.vjp(fn, x_ref[:, xo:xo + 128], x_ref[:, xo + 128:xo + 256], st_ref[so:so + 128, :],
                             st_ref[so + 128:so + 256, :], b_ref[:, bo:bo + 128], c_ref[:, bo:bo + 128],
                             dt_ref[...], cum_ref[...], cumt_ref[...], dsk_ref[...])
            dx0, dx1, ds0, ds1, dbm, dcm, ddt, dcum, dcumt, ddsk = vjp(
                (dy_ref[:, xo:xo + 128], dy_ref[:, xo + 128:xo + 256], carry[lo, :], carry[hi, :]))
            dx_ref[:, xo:xo + 128] = dx0
            dx_ref[:, xo + 128:xo + 256] = dx1
            db_ref[:, bo:bo + 128] = dbm
            dc_ref[:, bo:bo + 128] = dcm
            ddt_ref[...] += ddt
            dcum_ref[...] += dcum
            dcumt_ref[...] += dcumt
            ddsk_ref[...] += ddsk
            carry[lo, :] = ds0
            carry[hi, :] = ds1

    grp_blk = pl.BlockSpec((Q, 128 * gs), lambda n, g: (nc - 1 - n, g))
    return pl.pallas_call(
        body, name="ssd_bwd", grid=(nc, NGRP // gs),
        in_specs=[xs, bm, cm, per_chunk, per_chunk, cumt_spec, vec, state_spec, xs],
        out_specs=[xs, grp_blk, grp_blk, per_chunk, per_chunk, cumt_spec, vec],
        out_shape=[jax.ShapeDtypeStruct((s, DI), F32), jax.ShapeDtypeStruct((s, D), F32), jax.ShapeDtypeStruct((s, D), F32),
                   jax.ShapeDtypeStruct((s, 128), F32), jax.ShapeDtypeStruct((s, 128), F32),
                   jax.ShapeDtypeStruct((nc, 128, Q), F32), jax.ShapeDtypeStruct((1, 128), F32)],
        scratch_shapes=[pltpu.VMEM((DI, 128), F32)],
    )(xbc, xbc, xbc, dt, cum, cum_t, dsk, states, dy)


def _adam_math(w, g, m, v):
    m2 = ADAM_B1 * m + (1.0 - ADAM_B1) * g
    v2 = ADAM_B2 * v + (1.0 - ADAM_B2) * jnp.square(g)
    m_hat = m2 / (1.0 - ADAM_B1 ** ADAM_STEP)
    v_hat = v2 / (1.0 - ADAM_B2 ** ADAM_STEP)
    delta = -ADAM_LR * (m_hat / (jnp.sqrt(v_hat) + ADAM_EPS) + ADAM_WD * w)
    return delta, m2, v2


def _adam(w, g, m, v, name):
    r, c = w.shape
    tr = r if r * c * 4 <= (1 << 20) else _pick(r, (128, 64, 32, 16, 8))

    def body(w_ref, g_ref, m_ref, v_ref, d_ref, m2_ref, v2_ref):
        d, m2, v2 = _adam_math(w_ref[...], g_ref[...], m_ref[...], v_ref[...])
        d_ref[...] = d
        m2_ref[...] = m2
        v2_ref[...] = v2

    blk = pl.BlockSpec((tr, c), lambda i: (i, 0))
    shp = jax.ShapeDtypeStruct((r, c), F32)
    return pl.pallas_call(body, name=name, grid=(r // tr,), in_specs=[blk] * 4, out_specs=[blk] * 3,
                          out_shape=[shp] * 3)(w, g, m, v)


def _sum_leading(xs, name, out_dtype=F32, tr=256):
    if isinstance(xs, tuple):
        a, b = xs
        n, r, c = a.shape
        tr = _pick(r, (tr, 128, 64, 32, 16, 8))

        def body2(a_ref, b_ref, o_ref):
            o_ref[...] = (a_ref[...].astype(F32) + b_ref[...].astype(F32)).astype(out_dtype)

        blk = pl.BlockSpec((None, tr, c), lambda s, i: (s, i, 0))
        return pl.pallas_call(body2, name=name, grid=(n, r // tr), in_specs=[blk, blk], out_specs=blk,
                              out_shape=jax.ShapeDtypeStruct((n, r, c), out_dtype))(a, b)
    n, r, c = xs.shape
    tr = r if n * r * c * 4 <= (8 << 20) else _pick(r, (tr, 128, 64, 32, 16, 8))

    def body(x_ref, o_ref):
        acc = x_ref[0].astype(F32)
        for s in range(1, n):
            acc = acc + x_ref[s].astype(F32)
        o_ref[...] = acc.astype(out_dtype)

    return pl.pallas_call(body, name=name, grid=(r // tr,), in_specs=[pl.BlockSpec((n, tr, c), lambda i: (0, i, 0))],
                          out_specs=pl.BlockSpec((tr, c), lambda i: (i, 0)),
                          out_shape=jax.ShapeDtypeStruct((r, c), out_dtype))(xs)


def _mod_fwd(c8, w_mod, b_sl):
    def body(c_ref, w_ref, b_ref, o_ref):
        o_ref[...] = jnp.dot(jax.nn.silu(c_ref[...]), w_ref[...], precision=HIGHEST, preferred_element_type=F32) + b_ref[...]

    return pl.pallas_call(body, name="mod_fwd", in_specs=[VMEM, VMEM, VMEM], out_specs=VMEM,
                          out_shape=jax.ShapeDtypeStruct((N_DEV, w_mod.shape[1]), F32))(c8, w_mod, b_sl)


def _mod_wgrad(c8, dmod):
    def body(c_ref, d_ref, o_ref):
        o_ref[...] = lax.dot_general(jax.nn.silu(c_ref[...]), d_ref[...], (((0,), (0,)), ((), ())),
                                     precision=HIGHEST, preferred_element_type=F32)

    return pl.pallas_call(body, name="mod_wgrad", in_specs=[VMEM, VMEM], out_specs=VMEM,
                          out_shape=jax.ShapeDtypeStruct((D, dmod.shape[1]), F32))(c8, dmod)


def _place():
    x, y, c = lax.axis_index("x"), lax.axis_index("y"), lax.axis_index("c")
    chips = [(1 - x, y), (x, 1 - y), (1 - x, 1 - y)]
    return x, y, c, chips


def _all_gather_small(v, name):
    r, w = v.shape

    def body(v_ref, o_ref, send_sems, recv_sems, local_sem):
        x, y, c, _ = _place()
        me = 4 * x + 2 * y + c
        own = pltpu.make_async_copy(v_ref, o_ref.at[me], local_sem)
        own.start()
        sends = []
        for k in range(1, N_DEV):
            tx = 1 - x if k & 4 else x
            ty = 1 - y if k & 2 else y
            tc = 1 - c if k & 1 else c
            peer = 4 * tx + 2 * ty + tc
            cp = pltpu.make_async_remote_copy(src_ref=v_ref, dst_ref=o_ref.at[me], send_sem=send_sems.at[k - 1],
                                              recv_sem=recv_sems.at[k - 1], device_id=(tx, ty, tc), device_id_type=MESH)
            cp.start()
            sends.append((cp, peer, (tx, ty, tc)))
        for k, (cp, peer, dev) in enumerate(sends):
            pltpu.make_async_remote_copy(src_ref=v_ref, dst_ref=o_ref.at[peer], send_sem=send_sems.at[k],
                                         recv_sem=recv_sems.at[k], device_id=dev, device_id_type=MESH).wait_recv()
        for cp, _, _ in sends:
            cp.wait_send()
        own.wait()

    return pl.pallas_call(
        body, name=name, in_specs=[VMEM], out_specs=VMEM, out_shape=jax.ShapeDtypeStruct((N_DEV, r, w), v.dtype),
        scratch_shapes=[pltpu.SemaphoreType.DMA((N_DEV - 1,)), pltpu.SemaphoreType.DMA((N_DEV - 1,)), pltpu.SemaphoreType.DMA],
    )(v)


def _all_gather_weights(shards):
    n = len(shards)
    nsem = n * 3 * AG_CHUNKS

    def body(*refs):
        srcs, dsts = refs[:n], refs[n:2 * n]
        send_sems, recv_sems, fwd_send_sems, fwd_recv_sems = refs[2 * n:]
        x, y, c, chips = _place()
        q = 2 * x + y
        sibling = (x, y, 1 - c)
        sends = []
        for r in range(AG_CHUNKS):
            for t in range(n):
                hc = srcs[t].shape[0] // 2 // AG_CHUNKS
                rows = pl.ds(c * (hc * AG_CHUNKS) + r * hc, hc)
                for j, (cx, cy) in enumerate(chips):
                    k = (t * 3 + j) * AG_CHUNKS + r
                    cp = pltpu.make_async_remote_copy(src_ref=srcs[t].at[rows], dst_ref=dsts[t].at[q, rows],
                                                      send_sem=send_sems.at[k], recv_sem=recv_sems.at[k],
                                                      device_id=(cx, cy, c), device_id_type=MESH)
                    cp.start()
                    sends.append(cp)
        fwds = []
        for r in range(AG_CHUNKS):
            for t in range(n):
                hc = srcs[t].shape[0] // 2 // AG_CHUNKS
                sub = hc // D2D_SPLIT
                for j, (cx, cy) in enumerate(chips):
                    k = (t * 3 + j) * AG_CHUNKS + r
                    base = c * (hc * AG_CHUNKS) + r * hc
                    part = dsts[t].at[2 * cx + cy, pl.ds(base, hc)]
                    pltpu.make_async_remote_copy(src_ref=part, dst_ref=part, send_sem=send_sems.at[k], recv_sem=recv_sems.at[k],
                                                 device_id=(cx, cy, c), device_id_type=MESH).wait_recv()
                    for u in range(D2D_SPLIT):
                        piece = dsts[t].at[2 * cx + cy, pl.ds(base + u * sub, sub)]
                        pltpu.make_async_remote_copy(src_ref=piece, dst_ref=piece, send_sem=fwd_send_sems.at[k],
                                                     recv_sem=fwd_recv_sems.at[k], device_id=sibling, device_id_type=MESH).start()
                    fwds.append((part, k))
        for r in range(AG_CHUNKS):
            for t in range(n):
                hc = srcs[t].shape[0] // 2 // AG_CHUNKS
                for j, (cx, cy) in enumerate(chips):
                    k = (t * 3 + j) * AG_CHUNKS + r
                    part = dsts[t].at[2 * cx + cy, pl.ds((1 - c) * (hc * AG_CHUNKS) + r * hc, hc)]
                    pltpu.make_async_remote_copy(src_ref=part, dst_ref=part, send_sem=fwd_send_sems.at[k],
                                                 recv_sem=fwd_recv_sems.at[k], device_id=sibling, device_id_type=MESH).wait_recv()
        for cp in sends:
            cp.wait_send()
        for part, k in fwds:
            pltpu.make_async_remote_copy(src_ref=part, dst_ref=part, send_sem=fwd_send_sems.at[k], recv_sem=fwd_recv_sems.at[k],
                                         device_id=sibling, device_id_type=MESH).wait_send()

    return pl.pallas_call(
        body, name="all_gather_weights", in_specs=[ANY] * n, out_specs=[ANY] * n,
        out_shape=[jax.ShapeDtypeStruct((N_CHIP,) + s.shape, s.dtype) for s in shards],
        scratch_shapes=[pltpu.SemaphoreType.DMA((nsem,)), pltpu.SemaphoreType.DMA((nsem,)), pltpu.SemaphoreType.DMA((nsem,)),
                        pltpu.SemaphoreType.DMA((nsem,))],
    )(*shards)


def _exchange_halves(grads):
    n = len(grads)

    def body(*refs):
        srcs, theirs = refs[:n], refs[n:2 * n]
        send_sems, recv_sems = refs[2 * n:]
        x, y, c, _ = _place()
        sibling = (x, y, 1 - c)
        waits = []
        for t in range(n):
            h = srcs[t].shape[1] // 2
            sub = h // D2D_SPLIT
            for s in range(N_CHIP):
                for u in range(D2D_SPLIT):
                    pltpu.make_async_remote_copy(src_ref=srcs[t].at[s, pl.ds((1 - c) * h + u * sub, sub)],
                                                 dst_ref=theirs[t].at[s, pl.ds(u * sub, sub)],
                                                 send_sem=send_sems.at[t], recv_sem=recv_sems.at[t],
                                                 device_id=sibling, device_id_type=MESH).start()
            waits.append(pltpu.make_async_remote_copy(src_ref=srcs[t].at[:, pl.ds((1 - c) * h, h)], dst_ref=theirs[t],
                                                      send_sem=send_sems.at[t], recv_sem=recv_sems.at[t],
                                                      device_id=sibling, device_id_type=MESH))
        for whole in waits:
            whole.wait()

    half = [jax.ShapeDtypeStruct((N_CHIP, g.shape[1] // 2, g.shape[2]), g.dtype) for g in grads]
    return pl.pallas_call(
        body, name="exchange_halves", in_specs=[ANY] * n, out_specs=[ANY] * n, out_shape=half,
        scratch_shapes=[pltpu.SemaphoreType.DMA((n,)), pltpu.SemaphoreType.DMA((n,))],
    )(*grads)


def _exchange_chips(parts):
    n = len(parts)

    def body(*refs):
        srcs, dsts = refs[:n], refs[n:2 * n]
        send_sems, recv_sems, local_sems = refs[2 * n:]
        x, y, c, chips = _place()
        q = 2 * x + y
        sends, owns = [], []
        for t in range(n):
            own = pltpu.make_async_copy(srcs[t].at[q], dsts[t].at[q], local_sems.at[t])
            own.start()
            owns.append(own)
            for j, (cx, cy) in enumerate(chips):
                cp = pltpu.make_async_remote_copy(src_ref=srcs[t].at[2 * cx + cy], dst_ref=dsts[t].at[q],
                                                  send_sem=send_sems.at[3 * t + j], recv_sem=recv_sems.at[3 * t + j],
                                                  device_id=(cx, cy, c), device_id_type=MESH)
                cp.start()
                sends.append(cp)
        for t in range(n):
            for j, (cx, cy) in enumerate(chips):
                part = dsts[t].at[2 * cx + cy]
                pltpu.make_async_remote_copy(src_ref=part, dst_ref=part, send_sem=send_sems.at[3 * t + j],
                                             recv_sem=recv_sems.at[3 * t + j], device_id=(cx, cy, c), device_id_type=MESH).wait_recv()
        for cp in sends:
            cp.wait_send()
        for own in owns:
            own.wait()

    return pl.pallas_call(
        body, name="exchange_chips", in_specs=[ANY] * n, out_specs=[ANY] * n,
        out_shape=[jax.ShapeDtypeStruct(p.shape, p.dtype) for p in parts],
        scratch_shapes=[pltpu.SemaphoreType.DMA((3 * n,)), pltpu.SemaphoreType.DMA((3 * n,)), pltpu.SemaphoreType.DMA((n,))],
    )(*parts)


def _share_halves(halves):
    n = len(halves)

    def body(*refs):
        srcs, dsts = refs[:n], refs[n:2 * n]
        send_sems, recv_sems = refs[2 * n:]
        x, y, c, _ = _place()
        sibling = (x, y, 1 - c)
        for t in range(n):
            h = srcs[t].shape[0]
            sub = h // (2 * D2D_SPLIT)
            for u in range(2 * D2D_SPLIT):
                pltpu.make_async_remote_copy(src_ref=srcs[t].at[pl.ds(u * sub, sub)], dst_ref=dsts[t].at[pl.ds(c * h + u * sub, sub)],
                                             send_sem=send_sems.at[t], recv_sem=recv_sems.at[t],
                                             device_id=sibling, device_id_type=MESH).start()
        for t in range(n):
            h = srcs[t].shape[0]
            pltpu.make_async_remote_copy(src_ref=srcs[t], dst_ref=dsts[t].at[pl.ds((1 - c) * h, h)], send_sem=send_sems.at[t],
                                         recv_sem=recv_sems.at[t], device_id=sibling, device_id_type=MESH).wait()

    return pl.pallas_call(
        body, name="share_halves", in_specs=[ANY] * n, out_specs=[ANY] * n,
        out_shape=[jax.ShapeDtypeStruct((2 * h.shape[0], h.shape[1]), h.dtype) for h in halves],
        scratch_shapes=[pltpu.SemaphoreType.DMA((n,)), pltpu.SemaphoreType.DMA((n,))],
    )(*halves)


def _gather_weights(shards, chip):
    gathered = _all_gather_weights(shards)
    return [lax.dynamic_update_slice(g, s[None], (chip, 0, 0)) for g, s in zip(gathered, shards)]


def _reduce_scatter(sends, core):
    n = len(sends)
    theirs = _exchange_halves(sends)
    mines = [lax.dynamic_slice(g, (0, core * (g.shape[1] // 2), 0), (N_CHIP, g.shape[1] // 2, g.shape[2])) for g in sends]
    pair = [_sum_leading((mines[t], theirs[t]), f"pair_sum_{t}", out_dtype=BF16) for t in range(n)]
    contrib = _exchange_chips(pair)
    halves = [_sum_leading(contrib[t], f"chip_sum_{t}") for t in range(n)]
    shared = _share_halves(halves)
    return [lax.dynamic_update_slice(full, mine, (core * mine.shape[0], 0)) for full, mine in zip(shared, halves)]


def _pack(arrs):
    rows = []
    for a in arrs:
        flat = a.astype(F32).reshape(-1)
        pad = (-flat.shape[0]) % 1024
        rows.append(jnp.pad(flat, (0, pad)).reshape(-1, 128))
    return jnp.concatenate(rows, axis=0)


def _unpack(buf, shapes):
    out, r = [], 0
    for shp in shapes:
        size = 1
        for d in shp:
            size *= d
        nr = (size + 1023) // 1024 * 8
        out.append(buf[r:r + nr].reshape(-1)[:size].reshape(shp))
        r += nr
    return out


def _local_step(x, t, mod, w_in_r, w_bg, w_bs, w_out, w_ff1, w_ff2, gm_norm_w, gm_ws, gm_bs, conv_w, conv_b,
                dt_bias, a_log, d_skip, ssm_norm_w, final_norm_w):
    sh1, sc1, g1, sh2, sc2, g2 = [mod[:, i * D:(i + 1) * D] for i in range(6)]
    ws_b = jnp.where(jnp.tril(jnp.ones((Q, Q), bool))[None], gm_ws, 0.0).astype(BF16)
    bs_t = gm_bs.T
    pad32 = lambda v: jnp.pad(v, ((0, 0), (0, 128 - NHEAD)))
    dtb, alog, dsk = pad32(dt_bias), pad32(a_log), pad32(d_skip)

    h1 = _prenorm(x, sc1, sh1)
    (proj,) = _matmul(h1, w_in_r, "nn", name="mm_proj", tn=1152)
    y_a = _sgu_fwd(proj, gm_norm_w, ws_b, bs_t)
    xbc = _conv_fwd(proj, conv_w, conv_b)
    dt, cum, cum_t = _dt_fwd(proj, dtb, alog)
    y_ssd, states = _ssd_fwd(xbc, dt, cum, cum_t, dsk)
    y_b = _gatenorm_fwd(y_ssd, proj, ssm_norm_w)
    (ba,) = _matmul(y_a, w_bg, "nn", name="mm_branch_gm")
    (bb,) = _matmul(y_b, w_bs, "nn", name="mm_branch_ssm", tm=512, tk=2048)
    mixed = _mix_fwd(proj, ba, bb)
    (mo,) = _matmul(mixed, w_out, "nn", name="mm_out")
    x1, h2 = _resid_norm(x, mo, g1, sc2, sh2)
    f, act = _matmul(h2, w_ff1, "nn", name="mm_ff1", out_dtypes=(BF16, BF16),
                     epi=lambda acc: (acc, jnp.square(jnp.maximum(acc, 0.0))))
    (fo,) = _matmul(act, w_ff2, "nn", name="mm_ff2", tm=512, tk=4096)
    loss8, dx2, dfo, dg2, dfw = _loss_head(x1, fo, g2, final_norm_w, t)

    (df,) = _matmul(dfo, w_ff2, "nt", name="mm_ff2_dx", out_dtypes=(BF16,), epi_ins=(f,),
                    epi=lambda acc, fv: (acc * (2.0 * jnp.maximum(fv.astype(F32), 0.0)),))
    (g_ff2,) = _matmul(act, dfo, "tn", name="mm_ff2_dw", tm=512, tk=4096)
    (dh2,) = _matmul(df, w_ff1, "nt", name="mm_ff1_dx", tm=512, tk=4096)
    (g_ff1,) = _matmul(h2, df, "tn", name="mm_ff1_dw", tm=512, tk=4096)
    dx1, dmo, dg1, dsc2, dsh2 = _resid_norm_bwd(x1, mo, g1, sc2, sh2, dh2, dx2)
    (dmixed,) = _matmul(dmo, w_out, "nt", name="mm_out_dx")
    (g_out,) = _matmul(mixed, dmo, "tn", name="mm_out_dw", tm=512, tk=4096)
    dproj, dba, dbb = _mix_bwd(proj, ba, bb, dmixed)
    (dy_a,) = _matmul(dba, w_bg, "nt", name="mm_branch_gm_dx")
    (g_bg,) = _matmul(y_a, dba, "tn", name="mm_branch_gm_dw", tm=512, tk=4096)
    (dy_b,) = _matmul(dbb, w_bs, "nt", name="mm_branch_ssm_dx")
    (g_bs,) = _matmul(y_b, dbb, "tn", name="mm_branch_ssm_dw", tm=512, tk=4096)
    dproj, d_gm_norm, d_ws, d_bs_t = _sgu_bwd(proj, gm_norm_w, ws_b, bs_t, dy_a, dproj)
    dy_ssd, dproj, d_ssm_norm = _gatenorm_bwd(y_ssd, proj, ssm_norm_w, dy_b, dproj)
    dxs, dbm, dcm, ddt, dcum, dcum_t, d_dsk = _ssd_bwd(xbc, dt, cum, cum_t, dsk, states, dy_ssd)
    dproj, d_dtb, d_alog = _dt_bwd(proj, dtb, alog, ddt, dcum, dcum_t, dproj)
    dproj, dw_x, db_x = _conv_bwd(proj, conv_w, conv_b, dxs, dproj, 0, "conv_bwd_x")
    dproj, dw_b, db_b = _conv_bwd(proj, conv_w, conv_b, dbm, dproj, DI, "conv_bwd_b")
    dproj, dw_c, db_c = _conv_bwd(proj, conv_w, conv_b, dcm, dproj, DI + D, "conv_bwd_c")
    d_conv_w = jnp.concatenate([dw_x, dw_b, dw_c], axis=1)
    d_conv_b = jnp.concatenate([db_x, db_b, db_c], axis=1)
    (dh1,) = _matmul(dproj, w_in_r, "nt", name="mm_proj_dx", tk=3456)
    (g_in_r,) = _matmul(h1, dproj, "tn", name="mm_proj_dw", tm=512, tn=1152, tk=4096)
    grad_x, dsc1, dsh1 = _prenorm_bwd(x, sc1, sh1, dh1, dx1)

    dmod = jnp.concatenate([dsh1, dsc1, dg1, dsh2, dsc2, dg2], axis=1)
    small = dict(gm_ws=d_ws, gm_norm_w=d_gm_norm, gm_bs=d_bs_t.T, conv_w=d_conv_w, conv_b=d_conv_b,
                 dt_bias=d_dtb[:, :NHEAD], a_log=d_alog[:, :NHEAD], d_skip=d_dsk[:, :NHEAD],
                 ssm_norm_w=d_ssm_norm, final_norm_w=dfw, dmod=dmod, loss=loss8[:1, :1])
    big = dict(w_in_r=g_in_r, w_bg=g_bg, w_bs=g_bs, w_out=g_out, w_ff1=g_ff1, w_ff2=g_ff2)
    return grad_x, small, big


SMALL_KEYS = ("gm_ws", "gm_norm_w", "gm_bs", "conv_w", "conv_b", "dt_bias", "a_log", "d_skip", "ssm_norm_w",
              "final_norm_w", "dmod", "loss")
SMALL_SHAPES = ((GM_G, Q, Q), (1, D), (GM_G, Q), (4, CONV), (1, CONV), (1, NHEAD), (1, NHEAD), (1, NHEAD), (1, DI),
                (1, D), (1, 6 * D), (1, 1))


def _reorder_w_in(w_full):
    k = w_full.shape[0]
    return jnp.concatenate([w_full[:, :8192], w_full[:, 8224:], w_full[:, 8192:8224],
                            jnp.zeros((k, W_IN_R - W_IN), w_full.dtype)], axis=1)


def _restore_w_in(g_r):
    return jnp.concatenate([g_r[:, :8192], g_r[:, OFF_DT:OFF_DT + NHEAD], g_r[:, 8192:OFF_DT]], axis=1)


def kernel(x, c, w_mod, b_mod, w_in, gm_norm_w, gm_ws, gm_bs, conv_w, conv_b, dt_bias, a_log, d_skip, ssm_norm_w, w_branch_gm, w_branch_ssm, w_out, w_ff1, w_ff2, final_norm_w, loss_target, m_w_mod, m_b_mod, m_w_in, m_gm_norm_w, m_gm_ws, m_gm_bs, m_conv_w, m_conv_b, m_dt_bias, m_a_log, m_d_skip, m_ssm_norm_w, m_w_branch_gm, m_w_branch_ssm, m_w_out, m_w_ff1, m_w_ff2, m_final_norm_w, v_w_mod, v_b_mod, v_w_in, v_gm_norm_w, v_gm_ws, v_gm_bs, v_conv_w, v_conv_b, v_dt_bias, v_a_log, v_d_skip, v_ssm_norm_w, v_w_branch_gm, v_w_branch_ssm, v_w_out, v_w_ff1, v_w_ff2, v_final_norm_w):
    ax, ay, ac = lax.axis_index("x"), lax.axis_index("y"), lax.axis_index("c")
    chip = 2 * ax + ay
    dev = 2 * chip + ac
    seq = x.shape[1]
    nmod = w_mod.shape[2]

    first = jnp.concatenate([c, conv_w[0], jnp.zeros((3, D), F32)], axis=0)
    first_all = _all_gather_small(first, "all_gather_cond")
    c8 = first_all[:, 0, :]
    conv_w_full = jnp.concatenate([first_all[2 * k, 1:5, :] for k in range(N_CHIP)], axis=1)
    b_sl = lax.dynamic_slice(b_mod, (0, chip * nmod), (1, nmod))
    mod_part = _mod_fwd(c8, w_mod[0], b_sl)
    mod_all = _all_gather_small(mod_part, "all_gather_mod")
    mod = jnp.concatenate([lax.dynamic_slice(mod_all, (2 * k, dev, 0), (1, 1, nmod))[0] for k in range(N_CHIP)], axis=1)

    packed = jnp.concatenate([w_branch_gm[0], w_branch_ssm[0], w_out[0], w_ff2[0], w_ff1[0]], axis=0).astype(BF16)
    g_in, g_rest = _gather_weights([w_in[0].astype(BF16), packed], chip)
    w_in_r = _reorder_w_in(jnp.transpose(g_in, (1, 0, 2)).reshape(D, W_IN))
    r_bg, r_bs, r_out, r_ff2 = D // N_CHIP, DI // N_CHIP, D // N_CHIP, DFF // N_CHIP
    o1, o2, o3, o4 = r_bg, r_bg + r_bs, r_bg + r_bs + r_out, r_bg + r_bs + r_out + r_ff2
    w_bg_f = g_rest[:, :o1].reshape(D, D)
    w_bs_f = g_rest[:, o1:o2].reshape(DI, D)
    w_out_f = g_rest[:, o2:o3].reshape(D, D)
    w_ff2_f = g_rest[:, o3:o4].reshape(DFF, D)
    w_ff1_f = jnp.transpose(g_rest[:, o4:], (1, 0, 2)).reshape(D, DFF)

    grad_x, small, big = _local_step(
        x[0], loss_target[0], mod, w_in_r, w_bg_f, w_bs_f, w_out_f, w_ff1_f, w_ff2_f, gm_norm_w, gm_ws[0], gm_bs[0],
        conv_w_full, conv_b, dt_bias, a_log, d_skip, ssm_norm_w, final_norm_w.reshape(1, D))

    small_all = _all_gather_small(_pack([small[k] for k in SMALL_KEYS]), "all_gather_small_grads")
    small_sum = _sum_leading(small_all, "sum_small_grads")
    s_ws, s_gnw, s_bs, s_cw, s_cb, s_dtb, s_alog, s_dsk, s_snw, s_fnw, s_bmod, s_loss = _unpack(small_sum, SMALL_SHAPES)
    dmod_all = jnp.stack([_unpack(small_all[k], SMALL_SHAPES)[10][0] for k in range(N_DEV)], axis=0)
    g_w_mod = _mod_wgrad(c8, lax.dynamic_slice(dmod_all, (0, chip * nmod), (N_DEV, nmod)))
    g_conv_w = lax.dynamic_slice(s_cw, (0, chip * (CONV // N_CHIP)), (4, CONV // N_CHIP))

    g_in_full = _restore_w_in(big["w_in_r"])
    send_in = jnp.transpose(g_in_full.reshape(D, N_CHIP, W_IN // N_CHIP), (1, 0, 2)).astype(BF16)
    send_rest = jnp.concatenate([
        big["w_bg"].reshape(N_CHIP, r_bg, D), big["w_bs"].reshape(N_CHIP, r_bs, D), big["w_out"].reshape(N_CHIP, r_out, D),
        big["w_ff2"].reshape(N_CHIP, r_ff2, D), jnp.transpose(big["w_ff1"].reshape(D, N_CHIP, D), (1, 0, 2))], axis=1).astype(BF16)
    g_w_in, g_rest_sh = _reduce_scatter([send_in, send_rest], ac)
    g_w_bg, g_w_bs, g_w_out, g_w_ff2, g_w_ff1 = g_rest_sh[:o1], g_rest_sh[o1:o2], g_rest_sh[o2:o3], g_rest_sh[o3:o4], g_rest_sh[o4:]

    def adam_big(w, g, m, v, name):
        d, m2, v2 = _adam(w.reshape(g.shape), g, m.reshape(g.shape), v.reshape(g.shape), name)
        return g.reshape(w.shape), d.reshape(w.shape), m2.reshape(w.shape), v2.reshape(w.shape)

    res = {}
    res["w_mod"] = adam_big(w_mod, g_w_mod, m_w_mod, v_w_mod, "adam_w_mod")
    res["w_in"] = adam_big(w_in, g_w_in, m_w_in, v_w_in, "adam_w_in")
    res["w_branch_gm"] = adam_big(w_branch_gm, g_w_bg, m_w_branch_gm, v_w_branch_gm, "adam_w_branch_gm")
    res["w_branch_ssm"] = adam_big(w_branch_ssm, g_w_bs, m_w_branch_ssm, v_w_branch_ssm, "adam_w_branch_ssm")
    res["w_out"] = adam_big(w_out, g_w_out, m_w_out, v_w_out, "adam_w_out")
    res["w_ff1"] = adam_big(w_ff1, g_w_ff1, m_w_ff1, v_w_ff1, "adam_w_ff1")
    res["w_ff2"] = adam_big(w_ff2, g_w_ff2, m_w_ff2, v_w_ff2, "adam_w_ff2")

    names = ("b_mod", "gm_norm_w", "gm_ws", "gm_bs", "conv_w", "conv_b", "dt_bias", "a_log", "d_skip", "ssm_norm_w", "final_norm_w")
    ws = (b_mod, gm_norm_w, gm_ws, gm_bs, conv_w, conv_b, dt_bias, a_log, d_skip, ssm_norm_w, final_norm_w)
    ms = (m_b_mod, m_gm_norm_w, m_gm_ws, m_gm_bs, m_conv_w, m_conv_b, m_dt_bias, m_a_log, m_d_skip, m_ssm_norm_w, m_final_norm_w)
    vs = (v_b_mod, v_gm_norm_w, v_gm_ws, v_gm_bs, v_conv_w, v_conv_b, v_dt_bias, v_a_log, v_d_skip, v_ssm_norm_w, v_final_norm_w)
    gs = (s_bmod, s_gnw, s_ws, s_bs, g_conv_w, s_cb, s_dtb, s_alog, s_dsk, s_snw, s_fnw)
    gs = [g.reshape(w.shape) for g, w in zip(gs, ws)]
    shapes = [w.shape for w in ws]
    d_p, m_p, v_p = _adam(_pack(ws), _pack(gs), _pack(ms), _pack(vs), "adam_small")
    for name, g, d, m2, v2 in zip(names, gs, _unpack(d_p, shapes), _unpack(m_p, shapes), _unpack(v_p, shapes)):
        res[name] = (g, d, m2, v2)

    order = ("w_mod", "b_mod", "w_in", "gm_norm_w", "gm_ws", "gm_bs", "conv_w", "conv_b", "dt_bias", "a_log", "d_skip",
             "ssm_norm_w", "w_branch_gm", "w_branch_ssm", "w_out", "w_ff1", "w_ff2", "final_norm_w")
    loss = s_loss.reshape(())
    return (loss, grad_x.reshape(x.shape), *[res[k][0] for k in order], *[res[k][1] for k in order],
            *[res[k][2] for k in order], *[res[k][3] for k in order])
```

```python
import functools

import jax
import jax.numpy as jnp
from jax import lax
from jax.experimental import pallas as pl
from jax.experimental.pallas import tpu as pltpu

F32 = jnp.float32
BF16 = jnp.bfloat16
MESH = pl.DeviceIdType.MESH
HIGHEST = lax.Precision.HIGHEST

D = 1024
EPS = 1e-6
Q = 128
GM_G = 8
NHEAD = 32
NGRP = 8
DI = 2048
CONV = 4096
DFF = 4096
W_IN = 10272
W_IN_R = 10368
OFF_Z, OFF_XBC, OFF_GA, OFF_DT = 2048, 4096, 8192, 10240
N_CHIP = 4
N_DEV = 8
AG_CHUNKS = 4
D2D_SPLIT = 4

ADAM_LR, ADAM_B1, ADAM_B2, ADAM_EPS, ADAM_WD, ADAM_STEP = 0.001, 0.9, 0.999, 1e-08, 0.01, 10

ANY = pl.BlockSpec(memory_space=pl.ANY)
VMEM = pl.BlockSpec(memory_space=pltpu.VMEM)


def _full(shape):
    return pl.BlockSpec(shape, lambda *_: (0,) * len(shape))


def _pick(n, prefs):
    for p in prefs:
        if n % p == 0:
            return p
    return n


def _rms(x):
    return x * lax.rsqrt(jnp.mean(x * x, axis=-1, keepdims=True) + EPS)


def _modnorm(x, sc, sh):
    return _rms(x) * (1.0 + sc) + sh


def _sgu_pre(u, v, w):
    return jax.nn.gelu(u), _rms(jax.nn.gelu(v)) * w


def _gatenorm(y, z, w):
    g = y * jax.nn.silu(z)
    return _rms(g) * w


def _mix(ga, gb, ba, bb):
    return jax.nn.sigmoid(ga) * ba + jax.nn.sigmoid(gb) * bb


def _loss_tile(x1, fo, g2, fw, t):
    x2 = x1 + g2 * fo
    y = _rms(x2) * fw
    err = jnp.square(y - t)
    return 0.5 * jnp.sum(jnp.mean(err, axis=-1))


def _tril(n):
    r = lax.broadcasted_iota(jnp.int32, (n, n), 0)
    c = lax.broadcasted_iota(jnp.int32, (n, n), 1)
    return r >= c


def _dt_prep(dtr, dtb, alog):
    dt = jax.nn.softplus(dtr + dtb)
    a = dt * (-jnp.exp(alog))
    ones = _tril(Q).astype(F32)
    cum = jnp.dot(ones, a, precision=HIGHEST, preferred_element_type=F32)
    cum_t = lax.dot_general(a, ones, (((0,), (1,)), ((), ())), precision=HIGHEST, preferred_element_type=F32)
    return dt, cum, cum_t


def _ssd_group(x0, x1, s0, s1, bm, cm, dt, cum, cum_t, dsk, grp):
    lane = lax.broadcasted_iota(jnp.int32, (1, 128), 1)
    sub = lax.broadcasted_iota(jnp.int32, (128, 1), 0)
    future = jnp.where(_tril(Q), 0.0, -jnp.inf)
    half = lane < 64
    half_rows = sub < 64
    bmb = bm.astype(BF16)
    cmb = cm.astype(BF16)
    cb = lax.dot_general(cmb, bmb, (((1,), (1,)), ((), ())), preferred_element_type=F32)

    def col(v, h):
        return jnp.sum(jnp.where(lane == h, v, 0.0), axis=1, keepdims=True)

    def row(v, h):
        return jnp.sum(jnp.where(sub == h, v, 0.0), axis=0, keepdims=True)

    def last(c):
        return jnp.sum(jnp.where(sub == Q - 1, c, 0.0), axis=0, keepdims=True)

    outs, states = [], []
    for p, (xp, sp) in enumerate(((x0, s0), (x1, s1))):
        h_a = 4 * grp + 2 * p
        h_b = h_a + 1
        dt_a, dt_b = col(dt, h_a), col(dt, h_b)
        cum_a, cum_b = col(cum, h_a), col(cum, h_b)
        row_a, row_b = row(cum_t, h_a), row(cum_t, h_b)
        last_a, last_b = last(cum_a), last(cum_b)
        xdt = xp * jnp.where(half, dt_a, dt_b)
        xdtb = xdt.astype(BF16)
        m_a = (cb * jnp.exp(cum_a - row_a + future)).astype(BF16)
        m_b = (cb * jnp.exp(cum_b - row_b + future)).astype(BF16)
        y_intra = jnp.where(half, jnp.dot(m_a, xdtb, preferred_element_type=F32),
                            jnp.dot(m_b, xdtb, preferred_element_type=F32))
        y_inter = lax.dot_general(cmb, sp.astype(BF16), (((1,), (1,)), ((), ())), preferred_element_type=F32)
        y_inter = y_inter * jnp.where(half, jnp.exp(cum_a), jnp.exp(cum_b))
        w_end = jnp.where(half, jnp.exp(last_a - cum_a), jnp.exp(last_b - cum_b))
        upd = lax.dot_general((xdt * w_end).astype(BF16), bmb, (((0,), (0,)), ((), ())), preferred_element_type=F32)
        states.append(sp * jnp.where(half_rows, jnp.exp(last_a), jnp.exp(last_b)) + upd)
        outs.append(y_intra + y_inter + xp * jnp.where(half, col(dsk, h_a), col(dsk, h_b)))
    return outs[0], outs[1], states[0], states[1]


class _Comm:
    def __init__(self, ins, outs, sems, start, finish):
        self.ins, self.outs, self.sems, self.start, self.finish = list(ins), list(outs), list(sems), start, finish


def _matmul(a, b, mode, *, name, out_dtypes=(F32,), epi=None, epi_ins=(), tm=1024, tn=1024, tk=1024, comm=None):
    if mode == "nn":
        (m, k), n = a.shape, b.shape[1]
    elif mode == "nt":
        (m, k), n = a.shape, b.shape[0]
    else:
        (k, m), n = a.shape, b.shape[1]
    tm, tn, tk = _pick(m, (tm, 512, 256, 128)), _pick(n, (tn, 1152, 1024, 512, 384, 256, 128)), _pick(k, (tk, 1152, 1024, 512, 256, 128))
    nk = k // tk
    if mode == "nn":
        a_spec = pl.BlockSpec((tm, tk), lambda i, j, kk: (i, kk))
        b_spec = pl.BlockSpec((tk, tn), lambda i, j, kk: (kk, j))
        dims = (((1,), (0,)), ((), ()))
    elif mode == "nt":
        a_spec = pl.BlockSpec((tm, tk), lambda i, j, kk: (i, kk))
        b_spec = pl.BlockSpec((tn, tk), lambda i, j, kk: (j, kk))
        dims = (((1,), (1,)), ((), ()))
    else:
        a_spec = pl.BlockSpec((tk, tm), lambda i, j, kk: (kk, i))
        b_spec = pl.BlockSpec((tk, tn), lambda i, j, kk: (kk, j))
        dims = (((0,), (0,)), ((), ()))
    o_spec = pl.BlockSpec((tm, tn), lambda i, j, kk: (i, j))
    n_epi, n_out = len(epi_ins), len(out_dtypes)
    n_ci, n_co, n_cs = (len(comm.ins), len(comm.outs), len(comm.sems)) if comm is not None else (0, 0, 0)
    grid = (m // tm, n // tn, nk)

    def body(*refs):
        a_ref, b_ref = refs[0], refs[1]
        e_refs = refs[2:2 + n_epi]
        ci_refs = refs[2 + n_epi:2 + n_epi + n_ci]
        o_refs = refs[2 + n_epi + n_ci:2 + n_epi + n_ci + n_out]
        co_refs = refs[2 + n_epi + n_ci + n_out:2 + n_epi + n_ci + n_out + n_co]
        acc_ref = refs[2 + n_epi + n_ci + n_out + n_co]
        cs_refs = refs[3 + n_epi + n_ci + n_out + n_co:]
        if comm is not None:
            ids = [pl.program_id(d) for d in range(3)]
            pl.when((ids[0] == 0) & (ids[1] == 0) & (ids[2] == 0))(lambda: comm.start(ci_refs, co_refs, cs_refs))

        def finish(acc):
            outs = epi(acc, *[e[...] for e in e_refs]) if epi is not None else (acc,)
            for o_ref, val in zip(o_refs, outs):
                o_ref[...] = val.astype(o_ref.dtype)

        part = lax.dot_general(a_ref[...], b_ref[...], dims, preferred_element_type=F32)
        if nk == 1:
            finish(part)
        else:
            kk = pl.program_id(2)

            @pl.when(kk == 0)
            def _():
                acc_ref[...] = part

            @pl.when(kk > 0)
            def _():
                acc_ref[...] += part

            @pl.when(kk == nk - 1)
            def _():
                finish(acc_ref[...])

        if comm is not None:
            pl.when((ids[0] == grid[0] - 1) & (ids[1] == grid[1] - 1) & (ids[2] == grid[2] - 1))(
                lambda: comm.finish(ci_refs, co_refs, cs_refs))

    extra_in = comm.ins if comm is not None else []
    extra_out = comm.outs if comm is not None else []
    extra_sems = comm.sems if comm is not None else []
    return pl.pallas_call(
        body, name=name, grid=grid,
        in_specs=[a_spec, b_spec] + [o_spec] * n_epi + [ANY] * n_ci,
        out_specs=[o_spec] * n_out + [ANY] * n_co,
        out_shape=[jax.ShapeDtypeStruct((m, n), dt) for dt in out_dtypes] + extra_out,
        scratch_shapes=[pltpu.VMEM((tm, tn) if nk > 1 else (8, 128), F32)] + extra_sems,
        compiler_params=pltpu.CompilerParams(
            dimension_semantics=("arbitrary",) * 3 if comm is not None else ("parallel", "parallel", "arbitrary")),
    )(a, b, *epi_ins, *extra_in)


def _row_tile(s):
    return _pick(s, (512, 256, 128))


def _prenorm(x, sc, sh):
    s = x.shape[0]
    tm = _row_tile(s)

    def body(x_ref, sc_ref, sh_ref, h_ref):
        h_ref[...] = _modnorm(x_ref[...], sc_ref[...], sh_ref[...]).astype(BF16)

    row = pl.BlockSpec((tm, D), lambda i: (i, 0))
    return pl.pallas_call(body, name="prenorm", grid=(s // tm,), in_specs=[row, _full((1, D)), _full((1, D))],
                          out_specs=row, out_shape=jax.ShapeDtypeStruct((s, D), BF16))(x, sc, sh)


def _prenorm_bwd(x, sc, sh, dh, dx_res):
    s = x.shape[0]
    tm = _row_tile(s)

    def body(x_ref, sc_ref, sh_ref, dh_ref, dr_ref, dx_ref, dsc_ref, dsh_ref):
        _, vjp = jax.vjp(_modnorm, x_ref[...], sc_ref[...], sh_ref[...])
        dx, dsc, dsh = vjp(dh_ref[...])
        dx_ref[...] = dr_ref[...] + dx

        @pl.when(pl.program_id(0) == 0)
        def _():
            dsc_ref[...] = jnp.zeros_like(dsc_ref)
            dsh_ref[...] = jnp.zeros_like(dsh_ref)

        dsc_ref[...] += dsc
        dsh_ref[...] += dsh

    row = pl.BlockSpec((tm, D), lambda i: (i, 0))
    vec = _full((1, D))
    return pl.pallas_call(
        body, name="prenorm_bwd", grid=(s // tm,), in_specs=[row, vec, vec, row, row], out_specs=[row, vec, vec],
        out_shape=[jax.ShapeDtypeStruct((s, D), F32), jax.ShapeDtypeStruct((1, D), F32), jax.ShapeDtypeStruct((1, D), F32)],
    )(x, sc, sh, dh, dx_res)


def _resid_norm(x, mo, g1, sc, sh):
    s = x.shape[0]
    tm = _row_tile(s)

    def body(x_ref, mo_ref, g_ref, sc_ref, sh_ref, x1_ref, h_ref):
        x1 = x_ref[...] + g_ref[...] * mo_ref[...]
        x1_ref[...] = x1
        h_ref[...] = _modnorm(x1, sc_ref[...], sh_ref[...]).astype(BF16)

    row = pl.BlockSpec((tm, D), lambda i: (i, 0))
    vec = _full((1, D))
    return pl.pallas_call(
        body, name="resid_norm", grid=(s // tm,), in_specs=[row, row, vec, vec, vec], out_specs=[row, row],
        out_shape=[jax.ShapeDtypeStruct((s, D), F32), jax.ShapeDtypeStruct((s, D), BF16)],
    )(x, mo, g1, sc, sh)


def _resid_norm_bwd(x1, mo, g1, sc, sh, dh, dx2):
    s = x1.shape[0]
    tm = _row_tile(s)

    def body(x1_ref, mo_ref, g_ref, sc_ref, sh_ref, dh_ref, dx2_ref, dx1_ref, dmo_ref, dg_ref, dsc_ref, dsh_ref):
        _, vjp = jax.vjp(_modnorm, x1_ref[...], sc_ref[...], sh_ref[...])
        dx, dsc, dsh = vjp(dh_ref[...])
        dx1 = dx2_ref[...] + dx
        dx1_ref[...] = dx1
        dmo_ref[...] = (dx1 * g_ref[...]).astype(BF16)

        @pl.when(pl.program_id(0) == 0)
        def _():
            dg_ref[...] = jnp.zeros_like(dg_ref)
            dsc_ref[...] = jnp.zeros_like(dsc_ref)
            dsh_ref[...] = jnp.zeros_like(dsh_ref)

        dg_ref[...] += jnp.sum(dx1 * mo_ref[...], axis=0, keepdims=True)
        dsc_ref[...] += dsc
        dsh_ref[...] += dsh

    row = pl.BlockSpec((tm, D), lambda i: (i, 0))
    vec = _full((1, D))
    vshape = jax.ShapeDtypeStruct((1, D), F32)
    return pl.pallas_call(
        body, name="resid_norm_bwd", grid=(s // tm,), in_specs=[row, row, vec, vec, vec, row, row],
        out_specs=[row, row, vec, vec, vec],
        out_shape=[jax.ShapeDtypeStruct((s, D), F32), jax.ShapeDtypeStruct((s, D), BF16), vshape, vshape, vshape],
    )(x1, mo, g1, sc, sh, dh, dx2)


def _loss_head(x1, fo, g2, fw, t):
    s = x1.shape[0]
    tm = _row_tile(s)

    def body(x1_ref, fo_ref, g_ref, fw_ref, t_ref, loss_ref, dx_ref, dfo_ref, dg_ref, dfw_ref):
        loss, (dx1, dfo, dg, dfw) = jax.value_and_grad(_loss_tile, argnums=(0, 1, 2, 3))(
            x1_ref[...], fo_ref[...], g_ref[...], fw_ref[...], t_ref[...])
        dx_ref[...] = dx1
        dfo_ref[...] = dfo.astype(BF16)

        @pl.when(pl.program_id(0) == 0)
        def _():
            loss_ref[...] = jnp.zeros_like(loss_ref)
            dg_ref[...] = jnp.zeros_like(dg_ref)
            dfw_ref[...] = jnp.zeros_like(dfw_ref)

        loss_ref[...] += jnp.full(loss_ref.shape, loss, F32)
        dg_ref[...] += dg
        dfw_ref[...] += dfw

    row = pl.BlockSpec((tm, D), lambda i: (i, 0))
    vec = _full((1, D))
    vshape = jax.ShapeDtypeStruct((1, D), F32)
    return pl.pallas_call(
        body, name="loss_head", grid=(s // tm,), in_specs=[row, row, vec, vec, row],
        out_specs=[_full((8, 128)), row, row, vec, vec],
        out_shape=[jax.ShapeDtypeStruct((8, 128), F32), jax.ShapeDtypeStruct((s, D), F32),
                   jax.ShapeDtypeStruct((s, D), BF16), vshape, vshape],
    )(x1, fo, g2, fw, t)


def _sgu_fwd(proj, norm_w, ws_b, bs_t):
    s = proj.shape[0]
    tm = _pick(s, (256, 128))

    def body(u_ref, v_ref, w_ref, ws_ref, bs_ref, y_ref):
        ug, vn = _sgu_pre(u_ref[...], v_ref[...], w_ref[...])
        vnb = vn.astype(BF16)
        for c in range(tm // Q):
            r = slice(c * Q, (c + 1) * Q)
            for g in range(GM_G):
                cs = slice(g * 128, (g + 1) * 128)
                sv = jnp.dot(ws_ref[g], vnb[r, cs], preferred_element_type=F32) + bs_ref[:, g:g + 1]
                y_ref[r, cs] = (ug[r, cs] * sv).astype(BF16)

    return pl.pallas_call(
        body, name="sgu_fwd", grid=(s // tm,),
        in_specs=[pl.BlockSpec((tm, D), lambda i: (i, 0)), pl.BlockSpec((tm, D), lambda i: (i, 1)),
                  _full((1, D)), _full((GM_G, Q, Q)), _full((Q, GM_G))],
        out_specs=pl.BlockSpec((tm, D), lambda i: (i, 0)),
        out_shape=jax.ShapeDtypeStruct((s, D), BF16),
    )(proj, proj, norm_w, ws_b, bs_t)


def _sgu_bwd(proj, norm_w, ws_b, bs_t, dy, dproj):
    s = proj.shape[0]
    tm = _pick(s, (256, 128))

    def body(u_ref, v_ref, w_ref, ws_ref, bs_ref, dy_ref, _, duv_ref, dw_ref, dws_ref, dbs_ref, dug_scr, dvn_scr):
        @pl.when(pl.program_id(0) == 0)
        def _():
            dw_ref[...] = jnp.zeros_like(dw_ref)
            dws_ref[...] = jnp.zeros_like(dws_ref)
            dbs_ref[...] = jnp.zeros_like(dbs_ref)

        (ug, vn), vjp = jax.vjp(_sgu_pre, u_ref[...], v_ref[...], w_ref[...])
        vnb = vn.astype(BF16)
        dy = dy_ref[...]
        causal = _tril(Q).astype(F32)
        for c in range(tm // Q):
            r = slice(c * Q, (c + 1) * Q)
            for g in range(GM_G):
                cs = slice(g * 128, (g + 1) * 128)
                blk = vnb[r, cs]
                sv = jnp.dot(ws_ref[g], blk, preferred_element_type=F32) + bs_ref[:, g:g + 1]
                dug_scr[r, cs] = dy[r, cs] * sv
                dsv = dy[r, cs] * ug[r, cs]
                dsvb = dsv.astype(BF16)
                dws_ref[g] += causal * lax.dot_general(dsvb, blk, (((1,), (1,)), ((), ())), preferred_element_type=F32)
                dbs_ref[:, g:g + 1] += jnp.sum(dsv, axis=1, keepdims=True)
                dvn_scr[r, cs] = lax.dot_general(ws_ref[g], dsvb, (((0,), (0,)), ((), ())), preferred_element_type=F32)
        du, dv, dw = vjp((dug_scr[...], dvn_scr[...]))
        duv_ref[:, :D] = du.astype(BF16)
        duv_ref[:, D:] = dv.astype(BF16)
        dw_ref[...] += dw

    return pl.pallas_call(
        body, name="sgu_bwd", grid=(s // tm,),
        in_specs=[pl.BlockSpec((tm, D), lambda i: (i, 0)), pl.BlockSpec((tm, D), lambda i: (i, 1)),
                  _full((1, D)), _full((GM_G, Q, Q)), _full((Q, GM_G)), pl.BlockSpec((tm, D), lambda i: (i, 0)), ANY],
        out_specs=[pl.BlockSpec((tm, 2 * D), lambda i: (i, 0)), _full((1, D)), _full((GM_G, Q, Q)), _full((Q, GM_G))],
        out_shape=[jax.ShapeDtypeStruct(dproj.shape, BF16), jax.ShapeDtypeStruct((1, D), F32),
                   jax.ShapeDtypeStruct((GM_G, Q, Q), F32), jax.ShapeDtypeStruct((Q, GM_G), F32)],
        scratch_shapes=[pltpu.VMEM((tm, D), F32), pltpu.VMEM((tm, D), F32)],
        input_output_aliases={6: 0},
    )(proj, proj, norm_w, ws_b, bs_t, dy, dproj)


def _gatenorm_fwd(y_ssd, proj, norm_w):
    s = y_ssd.shape[0]
    tm = _pick(s, (256, 128))
    gw = DI // NGRP

    def body(y_ref, z_ref, w_ref, o_ref):
        for g in range(NGRP):
            cs = slice(g * gw, (g + 1) * gw)
            o_ref[:, cs] = _gatenorm(y_ref[:, cs], z_ref[:, cs], w_ref[:, cs]).astype(BF16)

    return pl.pallas_call(
        body, name="gatenorm_fwd", grid=(s // tm,),
        in_specs=[pl.BlockSpec((tm, DI), lambda i: (i, 0)), pl.BlockSpec((tm, DI), lambda i: (i, OFF_Z // DI)), _full((1, DI))],
        out_specs=pl.BlockSpec((tm, DI), lambda i: (i, 0)),
        out_shape=jax.ShapeDtypeStruct((s, DI), BF16),
    )(y_ssd, proj, norm_w)


def _gatenorm_bwd(y_ssd, proj, norm_w, dyb, dproj):
    s = y_ssd.shape[0]
    tm = _pick(s, (256, 128))
    gw = DI // NGRP

    def body(y_ref, z_ref, w_ref, d_ref, _, dy_ref, dz_ref, dw_ref):
        @pl.when(pl.program_id(0) == 0)
        def _():
            dw_ref[...] = jnp.zeros_like(dw_ref)

        for g in range(NGRP):
            cs = slice(g * gw, (g + 1) * gw)
            _, vjp = jax.vjp(_gatenorm, y_ref[:, cs], z_ref[:, cs], w_ref[:, cs])
            dy, dz, dw = vjp(d_ref[:, cs])
            dy_ref[:, cs] = dy
            dz_ref[:, cs] = dz.astype(BF16)
            dw_ref[:, cs] += dw

    blk = pl.BlockSpec((tm, DI), lambda i: (i, 0))
    zblk = pl.BlockSpec((tm, DI), lambda i: (i, OFF_Z // DI))
    return pl.pallas_call(
        body, name="gatenorm_bwd", grid=(s // tm,),
        in_specs=[blk, zblk, _full((1, DI)), blk, ANY],
        out_specs=[blk, zblk, _full((1, DI))],
        out_shape=[jax.ShapeDtypeStruct((s, DI), F32), jax.ShapeDtypeStruct(dproj.shape, BF16), jax.ShapeDtypeStruct((1, DI), F32)],
        input_output_aliases={4: 1},
    )(y_ssd, proj, norm_w, dyb, dproj)


def _mix_fwd(proj, ba, bb):
    s = proj.shape[0]
    tm = _row_tile(s)
    gb0 = OFF_GA // D

    def body(ga_ref, gb_ref, ba_ref, bb_ref, o_ref):
        o_ref[...] = _mix(ga_ref[...], gb_ref[...], ba_ref[...], bb_ref[...]).astype(BF16)

    row = pl.BlockSpec((tm, D), lambda i: (i, 0))
    return pl.pallas_call(
        body, name="mix_fwd", grid=(s // tm,),
        in_specs=[pl.BlockSpec((tm, D), lambda i: (i, gb0)), pl.BlockSpec((tm, D), lambda i: (i, gb0 + 1)), row, row],
        out_specs=row, out_shape=jax.ShapeDtypeStruct((s, D), BF16),
    )(proj, proj, ba, bb)


def _mix_bwd(proj, ba, bb, dmixed):
    s = proj.shape[0]
    tm = _row_tile(s)
    gb0 = OFF_GA // D

    def body(ga_ref, gb_ref, ba_ref, bb_ref, d_ref, dg_ref, dba_ref, dbb_ref):
        _, vjp = jax.vjp(_mix, ga_ref[...], gb_ref[...], ba_ref[...], bb_ref[...])
        dga, dgb, dba, dbb = vjp(d_ref[...])
        dg_ref[:, :D] = dga.astype(BF16)
        dg_ref[:, D:] = dgb.astype(BF16)
        dba_ref[...] = dba.astype(BF16)
        dbb_ref[...] = dbb.astype(BF16)

    row = pl.BlockSpec((tm, D), lambda i: (i, 0))
    return pl.pallas_call(
        body, name="mix_bwd", grid=(s // tm,),
        in_specs=[pl.BlockSpec((tm, D), lambda i: (i, gb0)), pl.BlockSpec((tm, D), lambda i: (i, gb0 + 1)), row, row, row],
        out_specs=[pl.BlockSpec((tm, 2 * D), lambda i: (i, OFF_GA // (2 * D))), row, row],
        out_shape=[jax.ShapeDtypeStruct((s, W_IN_R), BF16), jax.ShapeDtypeStruct((s, D), BF16), jax.ShapeDtypeStruct((s, D), BF16)],
    )(proj, proj, ba, bb, dmixed)


CONV_TC = 1024


def _conv_fwd(proj, conv_w, conv_b):
    s = proj.shape[0]
    tm = _row_tile(s)
    cb0 = OFF_XBC // CONV_TC

    def body(x_ref, halo_ref, w_ref, b_ref, o_ref):
        halo = jnp.where(pl.program_id(0) > 0, halo_ref[...], 0.0)
        ext = jnp.concatenate([halo, x_ref[...]], axis=0)
        acc = jnp.broadcast_to(b_ref[...], (tm, CONV_TC))
        for k in range(4):
            shifted = ext if k == 3 else pltpu.roll(ext, 3 - k, 0)
            acc = acc + w_ref[k:k + 1, :] * shifted[8:, :]
        o_ref[...] = jax.nn.silu(acc)

    return pl.pallas_call(
        body, name="conv_fwd", grid=(s // tm, CONV // CONV_TC),
        in_specs=[pl.BlockSpec((tm, CONV_TC), lambda i, j: (i, cb0 + j)),
                  pl.BlockSpec((8, CONV_TC), lambda i, j: (jnp.maximum(i * (tm // 8) - 1, 0), cb0 + j)),
                  pl.BlockSpec((4, CONV_TC), lambda i, j: (0, j)), pl.BlockSpec((1, CONV_TC), lambda i, j: (0, j))],
        out_specs=pl.BlockSpec((tm, CONV_TC), lambda i, j: (i, j)),
        out_shape=jax.ShapeDtypeStruct((s, CONV), F32),
    )(proj, proj, conv_w, conv_b)


def _conv_bwd(proj, conv_w, conv_b, dact, dproj, col0, name):
    s, width = dact.shape
    tm = _row_tile(s)
    nt = s // tm
    c0 = col0 // CONV_TC
    cb0 = OFF_XBC // CONV_TC + c0

    def body(x_ref, prev_ref, next_ref, d_ref, dnext_ref, w_ref, b_ref, _, dx_ref, dw_ref, db_ref):
        i = pl.program_id(1)
        prev = jnp.where(i > 0, prev_ref[...], 0.0)
        ext = jnp.concatenate([prev, x_ref[...], next_ref[...]], axis=0)
        dext = jnp.concatenate([d_ref[...], jnp.where(i < nt - 1, dnext_ref[...], 0.0)], axis=0)
        pre = jnp.broadcast_to(b_ref[...], (tm + 8, CONV_TC))
        taps = []
        for k in range(4):
            shifted = (ext if k == 3 else pltpu.roll(ext, 3 - k, 0))[8:, :]
            taps.append(shifted)
            pre = pre + w_ref[k:k + 1, :] * shifted
        sig = jax.nn.sigmoid(pre)
        dpre = dext * (sig * (1.0 + pre * (1.0 - sig)))
        dx = jnp.zeros((tm, CONV_TC), F32)
        for k in range(4):
            shifted = dpre if k == 3 else pltpu.roll(dpre, tm + 8 - (3 - k), 0)
            dx = dx + w_ref[k:k + 1, :] * shifted[:tm, :]
        dx_ref[...] = dx.astype(BF16)

        @pl.when(i == 0)
        def _():
            dw_ref[...] = jnp.zeros_like(dw_ref)
            db_ref[...] = jnp.zeros_like(db_ref)

        dtile = dpre[:tm, :]
        for k in range(4):
            dw_ref[k:k + 1, :] += jnp.sum(dtile * taps[k][:tm, :], axis=0, keepdims=True)
        db_ref[...] += jnp.sum(dtile, axis=0, keepdims=True)

    r8 = tm // 8
    return pl.pallas_call(
        body, name=name, grid=(width // CONV_TC, nt),
        in_specs=[pl.BlockSpec((tm, CONV_TC), lambda j, i: (i, cb0 + j)),
                  pl.BlockSpec((8, CONV_TC), lambda j, i: (jnp.maximum(i * r8 - 1, 0), cb0 + j)),
                  pl.BlockSpec((8, CONV_TC), lambda j, i: (jnp.minimum((i + 1) * r8, nt * r8 - 1), cb0 + j)),
                  pl.BlockSpec((tm, CONV_TC), lambda j, i: (i, j)),
                  pl.BlockSpec((8, CONV_TC), lambda j, i: (jnp.minimum((i + 1) * r8, nt * r8 - 1), j)),
                  pl.BlockSpec((4, CONV_TC), lambda j, i: (0, c0 + j)), pl.BlockSpec((1, CONV_TC), lambda j, i: (0, c0 + j)), ANY],
        out_specs=[pl.BlockSpec((tm, CONV_TC), lambda j, i: (i, cb0 + j)),
                   pl.BlockSpec((4, CONV_TC), lambda j, i: (0, j)), pl.BlockSpec((1, CONV_TC), lambda j, i: (0, j))],
        out_shape=[jax.ShapeDtypeStruct(dproj.shape, BF16), jax.ShapeDtypeStruct((4, width), F32), jax.ShapeDtypeStruct((1, width), F32)],
        input_output_aliases={7: 0},
    )(proj, proj, proj, dact, dact, conv_w, conv_b, dproj)


def _dt_fwd(proj, dtb, alog):
    s = proj.shape[0]
    nc = s // Q

    def body(r_ref, b_ref, a_ref, dt_ref, cum_ref, cumt_ref):
        dt, cum, cum_t = _dt_prep(r_ref[...], b_ref[...], a_ref[...])
        dt_ref[...] = dt
        cum_ref[...] = cum
        cumt_ref[...] = cum_t

    blk = pl.BlockSpec((Q, 128), lambda n: (n, 0))
    return pl.pallas_call(
        body, name="dt_fwd", grid=(nc,),
        in_specs=[pl.BlockSpec((Q, 128), lambda n: (n, OFF_DT // 128)), _full((1, 128)), _full((1, 128))],
        out_specs=[blk, blk, pl.BlockSpec((None, 128, Q), lambda n: (n, 0, 0))],
        out_shape=[jax.ShapeDtypeStruct((s, 128), F32), jax.ShapeDtypeStruct((s, 128), F32), jax.ShapeDtypeStruct((nc, 128, Q), F32)],
    )(proj, dtb, alog)


def _dt_bwd(proj, dtb, alog, ddt, dcum, dcumt, dproj):
    s = proj.shape[0]
    nc = s // Q

    def body(r_ref, b_ref, a_ref, ddt_ref, dcum_ref, dcumt_ref, _, dr_ref, db_ref, da_ref):
        _, vjp = jax.vjp(_dt_prep, r_ref[...], b_ref[...], a_ref[...])
        dr, db, da = vjp((ddt_ref[...], dcum_ref[...], dcumt_ref[...]))
        dr_ref[...] = dr.astype(BF16)

        @pl.when(pl.program_id(0) == 0)
        def _():
            db_ref[...] = jnp.zeros_like(db_ref)
            da_ref[...] = jnp.zeros_like(da_ref)

        db_ref[...] += db
        da_ref[...] += da

    blk = pl.BlockSpec((Q, 128), lambda n: (n, 0))
    pblk = pl.BlockSpec((Q, 128), lambda n: (n, OFF_DT // 128))
    return pl.pallas_call(
        body, name="dt_bwd", grid=(nc,),
        in_specs=[pblk, _full((1, 128)), _full((1, 128)), blk, blk, pl.BlockSpec((None, 128, Q), lambda n: (n, 0, 0)), ANY],
        out_specs=[pblk, _full((1, 128)), _full((1, 128))],
        out_shape=[jax.ShapeDtypeStruct(dproj.shape, BF16), jax.ShapeDtypeStruct((1, 128), F32), jax.ShapeDtypeStruct((1, 128), F32)],
        input_output_aliases={6: 0},
    )(proj, dtb, alog, ddt, dcum, dcumt, dproj)


def _ssd_specs(chunk_of):
    gs = SSD_GPS
    xs = pl.BlockSpec((Q, 256 * gs), lambda n, g: (chunk_of(n), g))
    bm = pl.BlockSpec((Q, 128 * gs), lambda n, g: (chunk_of(n), DI // (128 * gs) + g))
    cm = pl.BlockSpec((Q, 128 * gs), lambda n, g: (chunk_of(n), (DI + D) // (128 * gs) + g))
    per_chunk = pl.BlockSpec((Q, 128), lambda n, g: (chunk_of(n), 0))
    cum_t = pl.BlockSpec((None, 128, Q), lambda n, g: (chunk_of(n), 0, 0))
    state = pl.BlockSpec((None, 256 * gs, 128), lambda n, g: (chunk_of(n), g, 0))
    vec = pl.BlockSpec((1, 128), lambda n, g: (0, 0))
    return xs, bm, cm, per_chunk, cum_t, state, vec


SSD_GPS = 8


def _aligned(v, m):
    return v if isinstance(v, int) else pl.multiple_of(v, m)


def _ssd_fwd(xbc, dt, cum, cum_t, dsk):
    s = xbc.shape[0]
    nc = s // Q
    xs, bm, cm, per_chunk, cumt_spec, state_spec, vec = _ssd_specs(lambda n: n)
    gs = SSD_GPS

    def body(x_ref, b_ref, c_ref, dt_ref, cum_ref, cumt_ref, dsk_ref, y_ref, st_ref, carry):
        n, gstep = pl.program_id(0), (0 if gs == NGRP else pl.program_id(1))
        rows = pl.ds(_aligned(gstep * (256 * gs), 256 * gs), 256 * gs)

        @pl.when(n == 0)
        def _():
            carry[rows, :] = jnp.zeros((256 * gs, 128), F32)

        st_ref[...] = carry[rows, :]
        for k in range(gs):
            xo, so, bo = 256 * k, 256 * k, 128 * k
            y0, y1, n0, n1 = _ssd_group(
                x_ref[:, xo:xo + 128], x_ref[:, xo + 128:xo + 256], st_ref[so:so + 128, :], st_ref[so + 128:so + 256, :],
                b_ref[:, bo:bo + 128], c_ref[:, bo:bo + 128], dt_ref[...], cum_ref[...], cumt_ref[...], dsk_ref[...],
                gstep * gs + k)
            y_ref[:, xo:xo + 128] = y0
            y_ref[:, xo + 128:xo + 256] = y1
            base = gstep * (256 * gs) + so
            carry[pl.ds(_aligned(base, 128), 128), :] = n0
            carry[pl.ds(_aligned(base + 128, 128), 128), :] = n1

    return pl.pallas_call(
        body, name="ssd_fwd", grid=(nc, NGRP // gs),
        in_specs=[xs, bm, cm, per_chunk, per_chunk, cumt_spec, vec],
        out_specs=[xs, state_spec],
        out_shape=[jax.ShapeDtypeStruct((s, DI), F32), jax.ShapeDtypeStruct((nc, DI, 128), F32)],
        scratch_shapes=[pltpu.VMEM((DI, 128), F32)],
    )(xbc, xbc, xbc, dt, cum, cum_t, dsk)


def _ssd_bwd(xbc, dt, cum, cum_t, dsk, states, dy, comm=None):
    s = xbc.shape[0]
    nc = s // Q
    xs, bm, cm, per_chunk, cumt_spec, state_spec, vec = _ssd_specs(lambda n: nc - 1 - n)
    gs = SSD_GPS
    n_ci, n_co = (len(comm.ins), len(comm.outs)) if comm is not None else (0, 0)
    steps = (nc, NGRP // gs)

    def body(*refs):
        x_ref, b_ref, c_ref, dt_ref, cum_ref, cumt_ref, dsk_ref, st_ref, dy_ref = refs[:9]
        ci_refs = refs[9:9 + n_ci]
        dx_ref, db_ref, dc_ref, ddt_ref, dcum_ref, dcumt_ref, ddsk_ref = refs[9 + n_ci:16 + n_ci]
        co_refs = refs[16 + n_ci:16 + n_ci + n_co]
        carry = refs[16 + n_ci + n_co]
        cs_refs = refs[17 + n_ci + n_co:]
        n, gstep = pl.program_id(0), (0 if gs == NGRP else pl.program_id(1))
        rows = pl.ds(_aligned(gstep * (256 * gs), 256 * gs), 256 * gs)
        if comm is not None:
            pl.when((pl.program_id(0) == 0) & (pl.program_id(1) == 0))(lambda: comm.start(ci_refs, co_refs, cs_refs))

        @pl.when(n == 0)
        def _():
            carry[rows, :] = jnp.zeros((256 * gs, 128), F32)

        def zero_skip_sum():
            ddsk_ref[...] = jnp.zeros_like(ddsk_ref)

        def zero_chunk_sums():
            ddt_ref[...] = jnp.zeros_like(ddt_ref)
            dcum_ref[...] = jnp.zeros_like(dcum_ref)
            dcumt_ref[...] = jnp.zeros_like(dcumt_ref)

        if isinstance(gstep, int):
            pl.when(n == 0)(zero_skip_sum)
            zero_chunk_sums()
        else:
            pl.when((n == 0) & (gstep == 0))(zero_skip_sum)
            pl.when(gstep == 0)(zero_chunk_sums)

        for k in range(gs):
            xo, so, bo = 256 * k, 256 * k, 128 * k
            base = gstep * (256 * gs) + so
            lo = pl.ds(_aligned(base, 128), 128)
            hi = pl.ds(_aligned(base + 128, 128), 128)
            fn = functools.partial(_ssd_group, grp=gstep * gs + k)
            _, vjp = jax.vjp(fn, x_ref[:, xo:xo + 128], x_ref[:, xo + 128:xo + 256], st_ref[so:so + 128, :],
                             st_ref[so + 128:so + 256, :], b_ref[:, bo:bo + 128], c_ref[:, bo:bo + 128],
                             dt_ref[...], cum_ref[...], cumt_ref[...], dsk_ref[...])
            dx0, dx1, ds0, ds1, dbm, dcm, ddt, dcum, dcumt, ddsk = vjp(
                (dy_ref[:, xo:xo + 128], dy_ref[:, xo + 128:xo + 256], carry[lo, :], carry[hi, :]))
            dx_ref[:, xo:xo + 128] = dx0
            dx_ref[:, xo + 128:xo + 256] = dx1
            db_ref[:, bo:bo + 128] = dbm
            dc_ref[:, bo:bo + 128] = dcm
            ddt_ref[...] += ddt
            dcum_ref[...] += dcum
            dcumt_ref[...] += dcumt
            ddsk_ref[...] += ddsk
            carry[lo, :] = ds0
            carry[hi, :] = ds1

        if comm is not None:
            pl.when((pl.program_id(0) == steps[0] - 1) & (pl.program_id(1) == steps[1] - 1))(
                lambda: comm.finish(ci_refs, co_refs, cs_refs))

    grp_blk = pl.BlockSpec((Q, 128 * gs), lambda n, g: (nc - 1 - n, g))
    extra_in = comm.ins if comm is not None else []
    extra_out = comm.outs if comm is not None else []
    extra_sems = comm.sems if comm is not None else []
    return pl.pallas_call(
        body, name="ssd_bwd", grid=steps,
        in_specs=[xs, bm, cm, per_chunk, per_chunk, cumt_spec, vec, state_spec, xs] + [ANY] * n_ci,
        out_specs=[xs, grp_blk, grp_blk, per_chunk, per_chunk, cumt_spec, vec] + [ANY] * n_co,
        out_shape=[jax.ShapeDtypeStruct((s, DI), F32), jax.ShapeDtypeStruct((s, D), F32), jax.ShapeDtypeStruct((s, D), F32),
                   jax.ShapeDtypeStruct((s, 128), F32), jax.ShapeDtypeStruct((s, 128), F32),
                   jax.ShapeDtypeStruct((nc, 128, Q), F32), jax.ShapeDtypeStruct((1, 128), F32)] + extra_out,
        scratch_shapes=[pltpu.VMEM((DI, 128), F32)] + extra_sems,
    )(xbc, xbc, xbc, dt, cum, cum_t, dsk, states, dy, *extra_in)


def _adam_math(w, g, m, v):
    m2 = ADAM_B1 * m + (1.0 - ADAM_B1) * g
    v2 = ADAM_B2 * v + (1.0 - ADAM_B2) * jnp.square(g)
    m_hat = m2 / (1.0 - ADAM_B1 ** ADAM_STEP)
    v_hat = v2 / (1.0 - ADAM_B2 ** ADAM_STEP)
    delta = -ADAM_LR * (m_hat / (jnp.sqrt(v_hat) + ADAM_EPS) + ADAM_WD * w)
    return delta, m2, v2


def _adam(w, g, m, v, name):
    r, c = w.shape
    tr = r if r * c * 4 <= (1 << 20) else _pick(r, (128, 64, 32, 16, 8))

    def body(w_ref, g_ref, m_ref, v_ref, d_ref, m2_ref, v2_ref):
        d, m2, v2 = _adam_math(w_ref[...], g_ref[...], m_ref[...], v_ref[...])
        d_ref[...] = d
        m2_ref[...] = m2
        v2_ref[...] = v2

    blk = pl.BlockSpec((tr, c), lambda i: (i, 0))
    shp = jax.ShapeDtypeStruct((r, c), F32)
    return pl.pallas_call(body, name=name, grid=(r // tr,), in_specs=[blk] * 4, out_specs=[blk] * 3,
                          out_shape=[shp] * 3)(w, g, m, v)


def _sum_leading(xs, name, out_dtype=F32, tr=256):
    if isinstance(xs, tuple):
        a, b = xs
        n, r, c = a.shape
        tr = _pick(r, (tr, 128, 64, 32, 16, 8))

        def body2(a_ref, b_ref, o_ref):
            o_ref[...] = (a_ref[...].astype(F32) + b_ref[...].astype(F32)).astype(out_dtype)

        blk = pl.BlockSpec((None, tr, c), lambda s, i: (s, i, 0))
        return pl.pallas_call(body2, name=name, grid=(n, r // tr), in_specs=[blk, blk], out_specs=blk,
                              out_shape=jax.ShapeDtypeStruct((n, r, c), out_dtype))(a, b)
    n, r, c = xs.shape
    tr = r if n * r * c * 4 <= (8 << 20) else _pick(r, (tr, 128, 64, 32, 16, 8))

    def body(x_ref, o_ref):
        acc = x_ref[0].astype(F32)
        for s in range(1, n):
            acc = acc + x_ref[s].astype(F32)
        o_ref[...] = acc.astype(out_dtype)

    return pl.pallas_call(body, name=name, grid=(r // tr,), in_specs=[pl.BlockSpec((n, tr, c), lambda i: (0, i, 0))],
                          out_specs=pl.BlockSpec((tr, c), lambda i: (i, 0)),
                          out_shape=jax.ShapeDtypeStruct((r, c), out_dtype))(xs)


def _mod_fwd(c8, w_mod, b_sl):
    def body(c_ref, w_ref, b_ref, o_ref):
        o_ref[...] = jnp.dot(jax.nn.silu(c_ref[...]), w_ref[...], precision=HIGHEST, preferred_element_type=F32) + b_ref[...]

    return pl.pallas_call(body, name="mod_fwd", in_specs=[VMEM, VMEM, VMEM], out_specs=VMEM,
                          out_shape=jax.ShapeDtypeStruct((N_DEV, w_mod.shape[1]), F32))(c8, w_mod, b_sl)


def _mod_wgrad(c8, dmod):
    def body(c_ref, d_ref, o_ref):
        o_ref[...] = lax.dot_general(jax.nn.silu(c_ref[...]), d_ref[...], (((0,), (0,)), ((), ())),
                                     precision=HIGHEST, preferred_element_type=F32)

    return pl.pallas_call(body, name="mod_wgrad", in_specs=[VMEM, VMEM], out_specs=VMEM,
                          out_shape=jax.ShapeDtypeStruct((D, dmod.shape[1]), F32))(c8, dmod)


def _place():
    x, y, c = lax.axis_index("x"), lax.axis_index("y"), lax.axis_index("c")
    chips = [(1 - x, y), (x, 1 - y), (1 - x, 1 - y)]
    return x, y, c, chips


def _all_gather_small(v, name):
    r, w = v.shape

    def body(v_ref, o_ref, send_sems, recv_sems, local_sem):
        x, y, c, _ = _place()
        me = 4 * x + 2 * y + c
        own = pltpu.make_async_copy(v_ref, o_ref.at[me], local_sem)
        own.start()
        sends = []
        for k in range(1, N_DEV):
            tx = 1 - x if k & 4 else x
            ty = 1 - y if k & 2 else y
            tc = 1 - c if k & 1 else c
            peer = 4 * tx + 2 * ty + tc
            cp = pltpu.make_async_remote_copy(src_ref=v_ref, dst_ref=o_ref.at[me], send_sem=send_sems.at[k - 1],
                                              recv_sem=recv_sems.at[k - 1], device_id=(tx, ty, tc), device_id_type=MESH)
            cp.start()
            sends.append((cp, peer, (tx, ty, tc)))
        for k, (cp, peer, dev) in enumerate(sends):
            pltpu.make_async_remote_copy(src_ref=v_ref, dst_ref=o_ref.at[peer], send_sem=send_sems.at[k],
                                         recv_sem=recv_sems.at[k], device_id=dev, device_id_type=MESH).wait_recv()
        for cp, _, _ in sends:
            cp.wait_send()
        own.wait()

    return pl.pallas_call(
        body, name=name, in_specs=[VMEM], out_specs=VMEM, out_shape=jax.ShapeDtypeStruct((N_DEV, r, w), v.dtype),
        scratch_shapes=[pltpu.SemaphoreType.DMA((N_DEV - 1,)), pltpu.SemaphoreType.DMA((N_DEV - 1,)), pltpu.SemaphoreType.DMA],
    )(v)


def _all_gather_weights(shards):
    n = len(shards)
    nsem = n * 3 * AG_CHUNKS

    def body(*refs):
        srcs, dsts = refs[:n], refs[n:2 * n]
        send_sems, recv_sems, fwd_send_sems, fwd_recv_sems = refs[2 * n:]
        x, y, c, chips = _place()
        q = 2 * x + y
        sibling = (x, y, 1 - c)
        sends = []
        for r in range(AG_CHUNKS):
            for t in range(n):
                hc = srcs[t].shape[0] // 2 // AG_CHUNKS
                rows = pl.ds(c * (hc * AG_CHUNKS) + r * hc, hc)
                for j, (cx, cy) in enumerate(chips):
                    k = (t * 3 + j) * AG_CHUNKS + r
                    cp = pltpu.make_async_remote_copy(src_ref=srcs[t].at[rows], dst_ref=dsts[t].at[q, rows],
                                                      send_sem=send_sems.at[k], recv_sem=recv_sems.at[k],
                                                      device_id=(cx, cy, c), device_id_type=MESH)
                    cp.start()
                    sends.append(cp)
        fwds = []
        for r in range(AG_CHUNKS):
            for t in range(n):
                hc = srcs[t].shape[0] // 2 // AG_CHUNKS
                sub = hc // D2D_SPLIT
                for j, (cx, cy) in enumerate(chips):
                    k = (t * 3 + j) * AG_CHUNKS + r
                    base = c * (hc * AG_CHUNKS) + r * hc
                    part = dsts[t].at[2 * cx + cy, pl.ds(base, hc)]
                    pltpu.make_async_remote_copy(src_ref=part, dst_ref=part, send_sem=send_sems.at[k], recv_sem=recv_sems.at[k],
                                                 device_id=(cx, cy, c), device_id_type=MESH).wait_recv()
                    for u in range(D2D_SPLIT):
                        piece = dsts[t].at[2 * cx + cy, pl.ds(base + u * sub, sub)]
                        pltpu.make_async_remote_copy(src_ref=piece, dst_ref=piece, send_sem=fwd_send_sems.at[k],
                                                     recv_sem=fwd_recv_sems.at[k], device_id=sibling, device_id_type=MESH).start()
                    fwds.append((part, k))
        for r in range(AG_CHUNKS):
            for t in range(n):
                hc = srcs[t].shape[0] // 2 // AG_CHUNKS
                for j, (cx, cy) in enumerate(chips):
                    k = (t * 3 + j) * AG_CHUNKS + r
                    part = dsts[t].at[2 * cx + cy, pl.ds((1 - c) * (hc * AG_CHUNKS) + r * hc, hc)]
                    pltpu.make_async_remote_copy(src_ref=part, dst_ref=part, send_sem=fwd_send_sems.at[k],
                                                 recv_sem=fwd_recv_sems.at[k], device_id=sibling, device_id_type=MESH).wait_recv()
        for cp in sends:
            cp.wait_send()
        for part, k in fwds:
            pltpu.make_async_remote_copy(src_ref=part, dst_ref=part, send_sem=fwd_send_sems.at[k], recv_sem=fwd_recv_sems.at[k],
                                         device_id=sibling, device_id_type=MESH).wait_send()

    return pl.pallas_call(
        body, name="all_gather_weights", in_specs=[ANY] * n, out_specs=[ANY] * n,
        out_shape=[jax.ShapeDtypeStruct((N_CHIP,) + s.shape, s.dtype) for s in shards],
        scratch_shapes=[pltpu.SemaphoreType.DMA((nsem,)), pltpu.SemaphoreType.DMA((nsem,)), pltpu.SemaphoreType.DMA((nsem,)),
                        pltpu.SemaphoreType.DMA((nsem,))],
    )(*shards)


def _exchange_halves(grads, tag=""):
    n = len(grads)

    def body(*refs):
        srcs, theirs = refs[:n], refs[n:2 * n]
        send_sems, recv_sems = refs[2 * n:]
        x, y, c, _ = _place()
        sibling = (x, y, 1 - c)
        waits = []
        for t in range(n):
            h = srcs[t].shape[1] // 2
            sub = h // D2D_SPLIT
            for s in range(N_CHIP):
                for u in range(D2D_SPLIT):
                    pltpu.make_async_remote_copy(src_ref=srcs[t].at[s, pl.ds((1 - c) * h + u * sub, sub)],
                                                 dst_ref=theirs[t].at[s, pl.ds(u * sub, sub)],
                                                 send_sem=send_sems.at[t], recv_sem=recv_sems.at[t],
                                                 device_id=sibling, device_id_type=MESH).start()
            waits.append(pltpu.make_async_remote_copy(src_ref=srcs[t].at[:, pl.ds((1 - c) * h, h)], dst_ref=theirs[t],
                                                      send_sem=send_sems.at[t], recv_sem=recv_sems.at[t],
                                                      device_id=sibling, device_id_type=MESH))
        for whole in waits:
            whole.wait()

    half = [jax.ShapeDtypeStruct((N_CHIP, g.shape[1] // 2, g.shape[2]), g.dtype) for g in grads]
    return pl.pallas_call(
        body, name=f"exchange_halves_{tag}", in_specs=[ANY] * n, out_specs=[ANY] * n, out_shape=half,
        scratch_shapes=[pltpu.SemaphoreType.DMA((n,)), pltpu.SemaphoreType.DMA((n,))],
    )(*grads)


def _chip_exchange_comm(parts):
    n = len(parts)

    def copies(srcs, dsts, sems):
        send_sems, recv_sems, local_sems = sems
        x, y, c, chips = _place()
        q = 2 * x + y
        owns = [pltpu.make_async_copy(srcs[t].at[q], dsts[t].at[q], local_sems.at[t]) for t in range(n)]
        sends, recvs = [], []
        for t in range(n):
            for j, (cx, cy) in enumerate(chips):
                args = dict(send_sem=send_sems.at[3 * t + j], recv_sem=recv_sems.at[3 * t + j],
                            device_id=(cx, cy, c), device_id_type=MESH)
                sends.append(pltpu.make_async_remote_copy(src_ref=srcs[t].at[2 * cx + cy], dst_ref=dsts[t].at[q], **args))
                part = dsts[t].at[2 * cx + cy]
                recvs.append(pltpu.make_async_remote_copy(src_ref=part, dst_ref=part, **args))
        return owns, sends, recvs

    def start(srcs, dsts, sems):
        owns, sends, _ = copies(srcs, dsts, sems)
        for cp in owns + sends:
            cp.start()

    def finish(srcs, dsts, sems):
        owns, sends, recvs = copies(srcs, dsts, sems)
        for cp in recvs:
            cp.wait_recv()
        for cp in sends:
            cp.wait_send()
        for cp in owns:
            cp.wait()

    return _Comm(parts, [jax.ShapeDtypeStruct(p.shape, p.dtype) for p in parts],
                 [pltpu.SemaphoreType.DMA((3 * n,)), pltpu.SemaphoreType.DMA((3 * n,)), pltpu.SemaphoreType.DMA((n,))], start, finish)


def _gather_direct_comm(shard):
    h = shard.shape[0] // 2

    def copies(srcs, dsts, sems):
        send_sems, recv_sems = sems
        x, y, c, chips = _place()
        q = 2 * x + y
        sends, recvs = [], []
        for j, (cx, cy) in enumerate(chips):
            for other in range(2):
                peer_core = c if other == 0 else 1 - c
                args = dict(send_sem=send_sems.at[2 * j + other], recv_sem=recv_sems.at[2 * j + other],
                            device_id=(cx, cy, peer_core), device_id_type=MESH)
                mine = pl.ds(c * h, h)
                sends.append(pltpu.make_async_remote_copy(src_ref=srcs[0].at[mine], dst_ref=dsts[0].at[q, mine], **args))
                part = dsts[0].at[2 * cx + cy, pl.ds(peer_core * h, h)]
                recvs.append(pltpu.make_async_remote_copy(src_ref=part, dst_ref=part, **args))
        return sends, recvs

    def start(srcs, dsts, sems):
        for cp in copies(srcs, dsts, sems)[0]:
            cp.start()

    def finish(srcs, dsts, sems):
        sends, recvs = copies(srcs, dsts, sems)
        for cp in recvs:
            cp.wait_recv()
        for cp in sends:
            cp.wait_send()

    return _Comm([shard], [jax.ShapeDtypeStruct((N_CHIP,) + shard.shape, shard.dtype)],
                 [pltpu.SemaphoreType.DMA((6,)), pltpu.SemaphoreType.DMA((6,))], start, finish)


def _run_comm(comm, name):
    n_ci, n_co = len(comm.ins), len(comm.outs)

    def body(*refs):
        ci, co, cs = refs[:n_ci], refs[n_ci:n_ci + n_co], refs[n_ci + n_co:]
        comm.start(ci, co, cs)
        comm.finish(ci, co, cs)

    return pl.pallas_call(body, name=name, in_specs=[ANY] * n_ci, out_specs=[ANY] * n_co, out_shape=comm.outs,
                          scratch_shapes=comm.sems)(*comm.ins)


def _share_halves(halves, tag=""):
    n = len(halves)

    def body(*refs):
        srcs, dsts = refs[:n], refs[n:2 * n]
        send_sems, recv_sems = refs[2 * n:]
        x, y, c, _ = _place()
        sibling = (x, y, 1 - c)
        for t in range(n):
            h = srcs[t].shape[0]
            sub = h // (2 * D2D_SPLIT)
            for u in range(2 * D2D_SPLIT):
                pltpu.make_async_remote_copy(src_ref=srcs[t].at[pl.ds(u * sub, sub)], dst_ref=dsts[t].at[pl.ds(c * h + u * sub, sub)],
                                             send_sem=send_sems.at[t], recv_sem=recv_sems.at[t],
                                             device_id=sibling, device_id_type=MESH).start()
        for t in range(n):
            h = srcs[t].shape[0]
            pltpu.make_async_remote_copy(src_ref=srcs[t], dst_ref=dsts[t].at[pl.ds((1 - c) * h, h)], send_sem=send_sems.at[t],
                                         recv_sem=recv_sems.at[t], device_id=sibling, device_id_type=MESH).wait()

    return pl.pallas_call(
        body, name=f"share_halves_{tag}", in_specs=[ANY] * n, out_specs=[ANY] * n,
        out_shape=[jax.ShapeDtypeStruct((2 * h.shape[0], h.shape[1]), h.dtype) for h in halves],
        scratch_shapes=[pltpu.SemaphoreType.DMA((n,)), pltpu.SemaphoreType.DMA((n,))],
    )(*halves)


def _gather_weights(shards, chip):
    gathered = _all_gather_weights(shards)
    return [lax.dynamic_update_slice(g, s[None], (chip, 0, 0)) for g, s in zip(gathered, shards)]


def _reduce_scatter_begin(sends, core, tag):
    theirs = _exchange_halves(sends, tag)
    mines = [lax.dynamic_slice(g, (0, core * (g.shape[1] // 2), 0), (N_CHIP, g.shape[1] // 2, g.shape[2])) for g in sends]
    pair = [_sum_leading((m, t), f"pair_sum_{tag}{i}", out_dtype=BF16) for i, (m, t) in enumerate(zip(mines, theirs))]
    return _chip_exchange_comm(pair)


def _reduce_scatter_end(contrib, core, tag):
    halves = [_sum_leading(c, f"chip_sum_{tag}{i}") for i, c in enumerate(contrib)]
    shared = _share_halves(halves, tag)
    return [lax.dynamic_update_slice(full, mine, (core * mine.shape[0], 0)) for full, mine in zip(shared, halves)]


def _reduce_scatter(sends, core, tag=""):
    contrib = _run_comm(_reduce_scatter_begin(sends, core, tag), f"exchange_chips_{tag}")
    return _reduce_scatter_end(contrib, core, tag)


def _pack(arrs):
    rows = []
    for a in arrs:
        flat = a.astype(F32).reshape(-1)
        pad = (-flat.shape[0]) % 1024
        rows.append(jnp.pad(flat, (0, pad)).reshape(-1, 128))
    return jnp.concatenate(rows, axis=0)


def _unpack(buf, shapes):
    out, r = [], 0
    for shp in shapes:
        size = 1
        for d in shp:
            size *= d
        nr = (size + 1023) // 1024 * 8
        out.append(buf[r:r + nr].reshape(-1)[:size].reshape(shp))
        r += nr
    return out


R_BG, R_BS, R_OUT, R_FF = D // N_CHIP, DI // N_CHIP, D // N_CHIP, DFF // N_CHIP


def _pack_rest_shard(w_bg, w_bs, w_out, w_ff2, w_ff1):
    return jnp.concatenate([w_bg, w_bs, w_out, w_ff2, w_ff1], axis=0).astype(BF16)


def _unpack_rest(g):
    o1, o2, o3, o4 = R_BG, R_BG + R_BS, R_BG + R_BS + R_OUT, R_BG + R_BS + R_OUT + R_FF
    return (g[:, :o1].reshape(D, D), g[:, o1:o2].reshape(DI, D), g[:, o2:o3].reshape(D, D),
            jnp.transpose(g[:, o4:], (1, 0, 2)).reshape(D, DFF), g[:, o3:o4].reshape(DFF, D))


def _local_step(x, t, mod, w_in_r, rest, gm_norm_w, gm_ws, gm_bs, conv_w, conv_b, dt_bias, a_log, d_skip, ssm_norm_w,
                final_norm_w, place=None):
    sh1, sc1, g1, sh2, sc2, g2 = [mod[:, i * D:(i + 1) * D] for i in range(6)]
    ws_b = jnp.where(jnp.tril(jnp.ones((Q, Q), bool))[None], gm_ws, 0.0).astype(BF16)
    bs_t = gm_bs.T
    pad32 = lambda v: jnp.pad(v, ((0, 0), (0, 128 - NHEAD)))
    dtb, alog, dsk = pad32(dt_bias), pad32(a_log), pad32(d_skip)

    h1 = _prenorm(x, sc1, sh1)
    if place is None:
        (proj,) = _matmul(h1, w_in_r, "nn", name="mm_proj", tn=1152)
        w_bg, w_bs, w_out, w_ff1, w_ff2 = rest
    else:
        chip, core = place
        proj, g_rest = _matmul(h1, w_in_r, "nn", name="mm_proj", tn=1152, comm=_gather_direct_comm(rest))
        w_bg, w_bs, w_out, w_ff1, w_ff2 = _unpack_rest(lax.dynamic_update_slice(g_rest, rest[None], (chip, 0, 0)))
    y_a = _sgu_fwd(proj, gm_norm_w, ws_b, bs_t)
    xbc = _conv_fwd(proj, conv_w, conv_b)
    dt, cum, cum_t = _dt_fwd(proj, dtb, alog)
    y_ssd, states = _ssd_fwd(xbc, dt, cum, cum_t, dsk)
    y_b = _gatenorm_fwd(y_ssd, proj, ssm_norm_w)
    (ba,) = _matmul(y_a, w_bg, "nn", name="mm_branch_gm")
    (bb,) = _matmul(y_b, w_bs, "nn", name="mm_branch_ssm", tm=512, tk=2048)
    mixed = _mix_fwd(proj, ba, bb)
    (mo,) = _matmul(mixed, w_out, "nn", name="mm_out")
    x1, h2 = _resid_norm(x, mo, g1, sc2, sh2)
    f, act = _matmul(h2, w_ff1, "nn", name="mm_ff1", out_dtypes=(BF16, BF16),
                     epi=lambda acc: (acc, jnp.square(jnp.maximum(acc, 0.0))))
    (fo,) = _matmul(act, w_ff2, "nn", name="mm_ff2", tm=512, tk=4096)
    loss8, dx2, dfo, dg2, dfw = _loss_head(x1, fo, g2, final_norm_w, t)

    (df,) = _matmul(dfo, w_ff2, "nt", name="mm_ff2_dx", out_dtypes=(BF16,), epi_ins=(f,),
                    epi=lambda acc, fv: (acc * (2.0 * jnp.maximum(fv.astype(F32), 0.0)),))
    (g_ff2,) = _matmul(act, dfo, "tn", name="mm_ff2_dw", tm=512, tk=4096)
    (dh2,) = _matmul(df, w_ff1, "nt", name="mm_ff1_dx", tm=512, tk=4096)
    (g_ff1,) = _matmul(h2, df, "tn", name="mm_ff1_dw", tm=512, tk=4096)
    dx1, dmo, dg1, dsc2, dsh2 = _resid_norm_bwd(x1, mo, g1, sc2, sh2, dh2, dx2)
    (dmixed,) = _matmul(dmo, w_out, "nt", name="mm_out_dx")
    (g_out,) = _matmul(mixed, dmo, "tn", name="mm_out_dw", tm=512, tk=4096)
    early = None
    if place is not None:
        stack = jnp.concatenate([g_out.reshape(N_CHIP, R_OUT, D), g_ff2.reshape(N_CHIP, R_FF, D),
                                 jnp.transpose(g_ff1.reshape(D, N_CHIP, D), (1, 0, 2))], axis=1).astype(BF16)
        early = _reduce_scatter_begin([stack], core, "early")
    dproj, dba, dbb = _mix_bwd(proj, ba, bb, dmixed)
    (dy_a,) = _matmul(dba, w_bg, "nt", name="mm_branch_gm_dx")
    (g_bg,) = _matmul(y_a, dba, "tn", name="mm_branch_gm_dw", tm=512, tk=4096)
    (dy_b,) = _matmul(dbb, w_bs, "nt", name="mm_branch_ssm_dx")
    (g_bs,) = _matmul(y_b, dbb, "tn", name="mm_branch_ssm_dw", tm=512, tk=4096)
    dproj, d_gm_norm, d_ws, d_bs_t = _sgu_bwd(proj, gm_norm_w, ws_b, bs_t, dy_a, dproj)
    dy_ssd, dproj, d_ssm_norm = _gatenorm_bwd(y_ssd, proj, ssm_norm_w, dy_b, dproj)
    dxs, dbm, dcm, ddt, dcum, dcum_t, d_dsk, *early_contrib = _ssd_bwd(xbc, dt, cum, cum_t, dsk, states, dy_ssd, comm=early)
    dproj, d_dtb, d_alog = _dt_bwd(proj, dtb, alog, ddt, dcum, dcum_t, dproj)
    dproj, dw_x, db_x = _conv_bwd(proj, conv_w, conv_b, dxs, dproj, 0, "conv_bwd_x")
    dproj, dw_b, db_b = _conv_bwd(proj, conv_w, conv_b, dbm, dproj, DI, "conv_bwd_b")
    dproj, dw_c, db_c = _conv_bwd(proj, conv_w, conv_b, dcm, dproj, DI + D, "conv_bwd_c")
    d_conv_w = jnp.concatenate([dw_x, dw_b, dw_c], axis=1)
    d_conv_b = jnp.concatenate([db_x, db_b, db_c], axis=1)
    (g_in_r,) = _matmul(h1, dproj, "tn", name="mm_proj_dw", tm=512, tn=1152, tk=4096)
    late = None
    if place is not None:
        send_in = jnp.transpose(_restore_w_in(g_in_r).reshape(D, N_CHIP, W_IN // N_CHIP), (1, 0, 2)).astype(BF16)
        stack = jnp.concatenate([g_bg.reshape(N_CHIP, R_BG, D), g_bs.reshape(N_CHIP, R_BS, D)], axis=1).astype(BF16)
        late = _reduce_scatter_begin([send_in, stack], core, "late")
    dh1, *late_contrib = _matmul(dproj, w_in_r, "nt", name="mm_proj_dx", tk=3456, comm=late)
    grad_x, dsc1, dsh1 = _prenorm_bwd(x, sc1, sh1, dh1, dx1)

    dmod = jnp.concatenate([dsh1, dsc1, dg1, dsh2, dsc2, dg2], axis=1)
    small = dict(gm_ws=d_ws, gm_norm_w=d_gm_norm, gm_bs=d_bs_t.T, conv_w=d_conv_w, conv_b=d_conv_b,
                 dt_bias=d_dtb[:, :NHEAD], a_log=d_alog[:, :NHEAD], d_skip=d_dsk[:, :NHEAD],
                 ssm_norm_w=d_ssm_norm, final_norm_w=dfw, dmod=dmod, loss=loss8[:1, :1])
    if place is None:
        return grad_x, small, dict(w_in_r=g_in_r, w_bg=g_bg, w_bs=g_bs, w_out=g_out, w_ff1=g_ff1, w_ff2=g_ff2)
    (s_early,) = _reduce_scatter_end(early_contrib, core, "early")
    s_in, s_late = _reduce_scatter_end(late_contrib, core, "late")
    o1, o2 = R_OUT, R_OUT + R_FF
    big = dict(w_in=s_in, w_bg=s_late[:R_BG], w_bs=s_late[R_BG:], w_out=s_early[:o1], w_ff2=s_early[o1:o2], w_ff1=s_early[o2:])
    return grad_x, small, big


SMALL_KEYS = ("gm_ws", "gm_norm_w", "gm_bs", "conv_w", "conv_b", "dt_bias", "a_log", "d_skip", "ssm_norm_w",
              "final_norm_w", "dmod", "loss")
SMALL_SHAPES = ((GM_G, Q, Q), (1, D), (GM_G, Q), (4, CONV), (1, CONV), (1, NHEAD), (1, NHEAD), (1, NHEAD), (1, DI),
                (1, D), (1, 6 * D), (1, 1))


def _reorder_w_in(w_full):
    k = w_full.shape[0]
    return jnp.concatenate([w_full[:, :8192], w_full[:, 8224:], w_full[:, 8192:8224],
                            jnp.zeros((k, W_IN_R - W_IN), w_full.dtype)], axis=1)


def _restore_w_in(g_r):
    return jnp.concatenate([g_r[:, :8192], g_r[:, OFF_DT:OFF_DT + NHEAD], g_r[:, 8192:OFF_DT]], axis=1)


def kernel(x, c, w_mod, b_mod, w_in, gm_norm_w, gm_ws, gm_bs, conv_w, conv_b, dt_bias, a_log, d_skip, ssm_norm_w, w_branch_gm, w_branch_ssm, w_out, w_ff1, w_ff2, final_norm_w, loss_target, m_w_mod, m_b_mod, m_w_in, m_gm_norm_w, m_gm_ws, m_gm_bs, m_conv_w, m_conv_b, m_dt_bias, m_a_log, m_d_skip, m_ssm_norm_w, m_w_branch_gm, m_w_branch_ssm, m_w_out, m_w_ff1, m_w_ff2, m_final_norm_w, v_w_mod, v_b_mod, v_w_in, v_gm_norm_w, v_gm_ws, v_gm_bs, v_conv_w, v_conv_b, v_dt_bias, v_a_log, v_d_skip, v_ssm_norm_w, v_w_branch_gm, v_w_branch_ssm, v_w_out, v_w_ff1, v_w_ff2, v_final_norm_w):
    ax, ay, ac = lax.axis_index("x"), lax.axis_index("y"), lax.axis_index("c")
    chip = 2 * ax + ay
    dev = 2 * chip + ac
    seq = x.shape[1]
    nmod = w_mod.shape[2]

    first = jnp.concatenate([c, conv_w[0], jnp.zeros((3, D), F32)], axis=0)
    first_all = _all_gather_small(first, "all_gather_cond")
    c8 = first_all[:, 0, :]
    conv_w_full = jnp.concatenate([first_all[2 * k, 1:5, :] for k in range(N_CHIP)], axis=1)
    b_sl = lax.dynamic_slice(b_mod, (0, chip * nmod), (1, nmod))
    mod_part = _mod_fwd(c8, w_mod[0], b_sl)
    mod_all = _all_gather_small(mod_part, "all_gather_mod")
    mod = jnp.concatenate([lax.dynamic_slice(mod_all, (2 * k, dev, 0), (1, 1, nmod))[0] for k in range(N_CHIP)], axis=1)

    (g_in,) = _gather_weights([w_in[0].astype(BF16)], chip)
    w_in_r = _reorder_w_in(jnp.transpose(g_in, (1, 0, 2)).reshape(D, W_IN))
    rest = _pack_rest_shard(w_branch_gm[0], w_branch_ssm[0], w_out[0], w_ff2[0], w_ff1[0])

    grad_x, small, big = _local_step(
        x[0], loss_target[0], mod, w_in_r, rest, gm_norm_w, gm_ws[0], gm_bs[0], conv_w_full, conv_b, dt_bias, a_log,
        d_skip, ssm_norm_w, final_norm_w.reshape(1, D), place=(chip, ac))

    small_all = _all_gather_small(_pack([small[k] for k in SMALL_KEYS]), "all_gather_small_grads")
    small_sum = _sum_leading(small_all, "sum_small_grads")
    s_ws, s_gnw, s_bs, s_cw, s_cb, s_dtb, s_alog, s_dsk, s_snw, s_fnw, s_bmod, s_loss = _unpack(small_sum, SMALL_SHAPES)
    dmod_all = jnp.stack([_unpack(small_all[k], SMALL_SHAPES)[10][0] for k in range(N_DEV)], axis=0)
    g_w_mod = _mod_wgrad(c8, lax.dynamic_slice(dmod_all, (0, chip * nmod), (N_DEV, nmod)))
    g_conv_w = lax.dynamic_slice(s_cw, (0, chip * (CONV // N_CHIP)), (4, CONV // N_CHIP))

    g_w_in, g_w_bg, g_w_bs, g_w_out, g_w_ff2, g_w_ff1 = (big[k] for k in ("w_in", "w_bg", "w_bs", "w_out", "w_ff2", "w_ff1"))

    def adam_big(w, g, m, v, name):
        d, m2, v2 = _adam(w.reshape(g.shape), g, m.reshape(g.shape), v.reshape(g.shape), name)
        return g.reshape(w.shape), d.reshape(w.shape), m2.reshape(w.shape), v2.reshape(w.shape)

    res = {}
    res["w_mod"] = adam_big(w_mod, g_w_mod, m_w_mod, v_w_mod, "adam_w_mod")
    res["w_in"] = adam_big(w_in, g_w_in, m_w_in, v_w_in, "adam_w_in")
    res["w_branch_gm"] = adam_big(w_branch_gm, g_w_bg, m_w_branch_gm, v_w_branch_gm, "adam_w_branch_gm")
    res["w_branch_ssm"] = adam_big(w_branch_ssm, g_w_bs, m_w_branch_ssm, v_w_branch_ssm, "adam_w_branch_ssm")
    res["w_out"] = adam_big(w_out, g_w_out, m_w_out, v_w_out, "adam_w_out")
    res["w_ff1"] = adam_big(w_ff1, g_w_ff1, m_w_ff1, v_w_ff1, "adam_w_ff1")
    res["w_ff2"] = adam_big(w_ff2, g_w_ff2, m_w_ff2, v_w_ff2, "adam_w_ff2")

    names = ("b_mod", "gm_norm_w", "gm_ws", "gm_bs", "conv_w", "conv_b", "dt_bias", "a_log", "d_skip", "ssm_norm_w", "final_norm_w")
    ws = (b_mod, gm_norm_w, gm_ws, gm_bs, conv_w, conv_b, dt_bias, a_log, d_skip, ssm_norm_w, final_norm_w)
    ms = (m_b_mod, m_gm_norm_w, m_gm_ws, m_gm_bs, m_conv_w, m_conv_b, m_dt_bias, m_a_log, m_d_skip, m_ssm_norm_w, m_final_norm_w)
    vs = (v_b_mod, v_gm_norm_w, v_gm_ws, v_gm_bs, v_conv_w, v_conv_b, v_dt_bias, v_a_log, v_d_skip, v_ssm_norm_w, v_final_norm_w)
    gs = (s_bmod, s_gnw, s_ws, s_bs, g_conv_w, s_cb, s_dtb, s_alog, s_dsk, s_snw, s_fnw)
    gs = [g.reshape(w.shape) for g, w in zip(gs, ws)]
    shapes = [w.shape for w in ws]
    d_p, m_p, v_p = _adam(_pack(ws), _pack(gs), _pack(ms), _pack(vs), "adam_small")
    for name, g, d, m2, v2 in zip(names, gs, _unpack(d_p, shapes), _unpack(m_p, shapes), _unpack(v_p, shapes)):
        res[name] = (g, d, m2, v2)

    order = ("w_mod", "b_mod", "w_in", "gm_norm_w", "gm_ws", "gm_bs", "conv_w", "conv_b", "dt_bias", "a_log", "d_skip",
             "ssm_norm_w", "w_branch_gm", "w_branch_ssm", "w_out", "w_ff1", "w_ff2", "final_norm_w")
    loss = s_loss.reshape(())
    return (loss, grad_x.reshape(x.shape), *[res[k][0] for k in order], *[res[k][1] for k in order],
            *[res[k][2] for k in order], *[res[k][3] for k in order])
```

```python
import functools

import jax
import jax.numpy as jnp
from jax import lax
from jax.experimental import pallas as pl
from jax.experimental.pallas import tpu as pltpu

F32 = jnp.float32
BF16 = jnp.bfloat16
MESH = pl.DeviceIdType.MESH
HIGHEST = lax.Precision.HIGHEST

D = 1024
EPS = 1e-6
Q = 128
GM_G = 8
NHEAD = 32
NGRP = 8
DI = 2048
CONV = 4096
DFF = 4096
W_IN = 10272
W_IN_R = 10368
OFF_Z, OFF_XBC, OFF_GA, OFF_DT = 2048, 4096, 8192, 10240
N_CHIP = 4
N_DEV = 8
AG_CHUNKS = 4
D2D_SPLIT = 4

ADAM_LR, ADAM_B1, ADAM_B2, ADAM_EPS, ADAM_WD, ADAM_STEP = 0.001, 0.9, 0.999, 1e-08, 0.01, 10

ANY = pl.BlockSpec(memory_space=pl.ANY)
VMEM = pl.BlockSpec(memory_space=pltpu.VMEM)


def _full(shape):
    return pl.BlockSpec(shape, lambda *_: (0,) * len(shape))


def _pick(n, prefs):
    for p in prefs:
        if n % p == 0:
            return p
    return n


def _rms(x):
    return x * lax.rsqrt(jnp.mean(x * x, axis=-1, keepdims=True) + EPS)


def _modnorm(x, sc, sh):
    return _rms(x) * (1.0 + sc) + sh


def _sgu_pre(u, v, w):
    return jax.nn.gelu(u), _rms(jax.nn.gelu(v)) * w


def _gatenorm(y, z, w):
    g = y * jax.nn.silu(z)
    return _rms(g) * w


def _mix(ga, gb, ba, bb):
    return jax.nn.sigmoid(ga) * ba + jax.nn.sigmoid(gb) * bb


def _loss_tile(x1, fo, g2, fw, t):
    x2 = x1 + g2 * fo
    y = _rms(x2) * fw
    err = jnp.square(y - t)
    return 0.5 * jnp.sum(jnp.mean(err, axis=-1))


def _tril(n):
    r = lax.broadcasted_iota(jnp.int32, (n, n), 0)
    c = lax.broadcasted_iota(jnp.int32, (n, n), 1)
    return r >= c


def _dt_prep(dtr, dtb, alog):
    dt = jax.nn.softplus(dtr + dtb)
    a = dt * (-jnp.exp(alog))
    ones = _tril(Q).astype(F32)
    cum = jnp.dot(ones, a, precision=HIGHEST, preferred_element_type=F32)
    cum_t = lax.dot_general(a, ones, (((0,), (1,)), ((), ())), precision=HIGHEST, preferred_element_type=F32)
    return dt, cum, cum_t


def _ssd_group(x0, x1, s0, s1, bm, cm, dt, cum, cum_t, dsk, grp):
    lane = lax.broadcasted_iota(jnp.int32, (1, 128), 1)
    sub = lax.broadcasted_iota(jnp.int32, (128, 1), 0)
    future = jnp.where(_tril(Q), 0.0, -jnp.inf)
    half = lane < 64
    half_rows = sub < 64
    bmb = bm.astype(BF16)
    cmb = cm.astype(BF16)
    cb = lax.dot_general(cmb, bmb, (((1,), (1,)), ((), ())), preferred_element_type=F32)

    def col(v, h):
        return jnp.sum(jnp.where(lane == h, v, 0.0), axis=1, keepdims=True)

    def row(v, h):
        return jnp.sum(jnp.where(sub == h, v, 0.0), axis=0, keepdims=True)

    def last(c):
        return jnp.sum(jnp.where(sub == Q - 1, c, 0.0), axis=0, keepdims=True)

    outs, states = [], []
    for p, (xp, sp) in enumerate(((x0, s0), (x1, s1))):
        h_a = 4 * grp + 2 * p
        h_b = h_a + 1
        dt_a, dt_b = col(dt, h_a), col(dt, h_b)
        cum_a, cum_b = col(cum, h_a), col(cum, h_b)
        row_a, row_b = row(cum_t, h_a), row(cum_t, h_b)
        last_a, last_b = last(cum_a), last(cum_b)
        xdt = xp * jnp.where(half, dt_a, dt_b)
        xdtb = xdt.astype(BF16)
        m_a = (cb * jnp.exp(cum_a - row_a + future)).astype(BF16)
        m_b = (cb * jnp.exp(cum_b - row_b + future)).astype(BF16)
        y_intra = jnp.where(half, jnp.dot(m_a, xdtb, preferred_element_type=F32),
                            jnp.dot(m_b, xdtb, preferred_element_type=F32))
        y_inter = lax.dot_general(cmb, sp.astype(BF16), (((1,), (1,)), ((), ())), preferred_element_type=F32)
        y_inter = y_inter * jnp.where(half, jnp.exp(cum_a), jnp.exp(cum_b))
        w_end = jnp.where(half, jnp.exp(last_a - cum_a), jnp.exp(last_b - cum_b))
        upd = lax.dot_general((xdt * w_end).astype(BF16), bmb, (((0,), (0,)), ((), ())), preferred_element_type=F32)
        states.append(sp * jnp.where(half_rows, jnp.exp(last_a), jnp.exp(last_b)) + upd)
        outs.append(y_intra + y_inter + xp * jnp.where(half, col(dsk, h_a), col(dsk, h_b)))
    return outs[0], outs[1], states[0], states[1]


class _Comm:
    def __init__(self, ins, outs, sems, start, finish):
        self.ins, self.outs, self.sems, self.start, self.finish = list(ins), list(outs), list(sems), start, finish
        self.middle = None


def _matmul(a, b, mode, *, name, out_dtypes=(F32,), epi=None, epi_ins=(), tm=1024, tn=1024, tk=1024, comm=None):
    if mode == "nn":
        (m, k), n = a.shape, b.shape[1]
    elif mode == "nt":
        (m, k), n = a.shape, b.shape[0]
    else:
        (k, m), n = a.shape, b.shape[1]
    tm, tn, tk = _pick(m, (tm, 512, 256, 128)), _pick(n, (tn, 1152, 1024, 512, 384, 256, 128)), _pick(k, (tk, 1152, 1024, 512, 256, 128))
    nk = k // tk
    if mode == "nn":
        a_spec = pl.BlockSpec((tm, tk), lambda i, j, kk: (i, kk))
        b_spec = pl.BlockSpec((tk, tn), lambda i, j, kk: (kk, j))
        dims = (((1,), (0,)), ((), ()))
    elif mode == "nt":
        a_spec = pl.BlockSpec((tm, tk), lambda i, j, kk: (i, kk))
        b_spec = pl.BlockSpec((tn, tk), lambda i, j, kk: (j, kk))
        dims = (((1,), (1,)), ((), ()))
    else:
        a_spec = pl.BlockSpec((tk, tm), lambda i, j, kk: (kk, i))
        b_spec = pl.BlockSpec((tk, tn), lambda i, j, kk: (kk, j))
        dims = (((0,), (0,)), ((), ()))
    o_spec = pl.BlockSpec((tm, tn), lambda i, j, kk: (i, j))
    n_epi, n_out = len(epi_ins), len(out_dtypes)
    n_ci, n_co, n_cs = (len(comm.ins), len(comm.outs), len(comm.sems)) if comm is not None else (0, 0, 0)
    grid = (m // tm, n // tn, nk)

    def body(*refs):
        a_ref, b_ref = refs[0], refs[1]
        e_refs = refs[2:2 + n_epi]
        ci_refs = refs[2 + n_epi:2 + n_epi + n_ci]
        o_refs = refs[2 + n_epi + n_ci:2 + n_epi + n_ci + n_out]
        co_refs = refs[2 + n_epi + n_ci + n_out:2 + n_epi + n_ci + n_out + n_co]
        acc_ref = refs[2 + n_epi + n_ci + n_out + n_co]
        cs_refs = refs[3 + n_epi + n_ci + n_out + n_co:]
        if comm is not None:
            ids = [pl.program_id(d) for d in range(3)]
            pl.when((ids[0] == 0) & (ids[1] == 0) & (ids[2] == 0))(lambda: comm.start(ci_refs, co_refs, cs_refs))
            if comm.middle is not None:
                pl.when((ids[0] == (5 * grid[0]) // 8) & (ids[1] == 0) & (ids[2] == 0))(
                    lambda: comm.middle(ci_refs, co_refs, cs_refs))

        def finish(acc):
            outs = epi(acc, *[e[...] for e in e_refs]) if epi is not None else (acc,)
            for o_ref, val in zip(o_refs, outs):
                o_ref[...] = val.astype(o_ref.dtype)

        part = lax.dot_general(a_ref[...], b_ref[...], dims, preferred_element_type=F32)
        if nk == 1:
            finish(part)
        else:
            kk = pl.program_id(2)

            @pl.when(kk == 0)
            def _():
                acc_ref[...] = part

            @pl.when(kk > 0)
            def _():
                acc_ref[...] += part

            @pl.when(kk == nk - 1)
            def _():
                finish(acc_ref[...])

        if comm is not None:
            pl.when((ids[0] == grid[0] - 1) & (ids[1] == grid[1] - 1) & (ids[2] == grid[2] - 1))(
                lambda: comm.finish(ci_refs, co_refs, cs_refs))

    extra_in = comm.ins if comm is not None else []
    extra_out = comm.outs if comm is not None else []
    extra_sems = comm.sems if comm is not None else []
    return pl.pallas_call(
        body, name=name, grid=grid,
        in_specs=[a_spec, b_spec] + [o_spec] * n_epi + [ANY] * n_ci,
        out_specs=[o_spec] * n_out + [ANY] * n_co,
        out_shape=[jax.ShapeDtypeStruct((m, n), dt) for dt in out_dtypes] + extra_out,
        scratch_shapes=[pltpu.VMEM((tm, tn) if nk > 1 else (8, 128), F32)] + extra_sems,
        compiler_params=pltpu.CompilerParams(
            dimension_semantics=("arbitrary",) * 3 if comm is not None else ("parallel", "parallel", "arbitrary")),
    )(a, b, *epi_ins, *extra_in)


def _row_tile(s):
    return _pick(s, (512, 256, 128))


def _prenorm(x, sc, sh):
    s = x.shape[0]
    tm = _row_tile(s)

    def body(x_ref, sc_ref, sh_ref, h_ref):
        h_ref[...] = _modnorm(x_ref[...], sc_ref[...], sh_ref[...]).astype(BF16)

    row = pl.BlockSpec((tm, D), lambda i: (i, 0))
    return pl.pallas_call(body, name="prenorm", grid=(s // tm,), in_specs=[row, _full((1, D)), _full((1, D))],
                          out_specs=row, out_shape=jax.ShapeDtypeStruct((s, D), BF16))(x, sc, sh)


def _prenorm_bwd(x, sc, sh, dh, dx_res):
    s = x.shape[0]
    tm = _row_tile(s)

    def body(x_ref, sc_ref, sh_ref, dh_ref, dr_ref, dx_ref, dsc_ref, dsh_ref):
        _, vjp = jax.vjp(_modnorm, x_ref[...], sc_ref[...], sh_ref[...])
        dx, dsc, dsh = vjp(dh_ref[...])
        dx_ref[...] = dr_ref[...] + dx

        @pl.when(pl.program_id(0) == 0)
        def _():
            dsc_ref[...] = jnp.zeros_like(dsc_ref)
            dsh_ref[...] = jnp.zeros_like(dsh_ref)

        dsc_ref[...] += dsc
        dsh_ref[...] += dsh

    row = pl.BlockSpec((tm, D), lambda i: (i, 0))
    vec = _full((1, D))
    return pl.pallas_call(
        body, name="prenorm_bwd", grid=(s // tm,), in_specs=[row, vec, vec, row, row], out_specs=[row, vec, vec],
        out_shape=[jax.ShapeDtypeStruct((s, D), F32), jax.ShapeDtypeStruct((1, D), F32), jax.ShapeDtypeStruct((1, D), F32)],
    )(x, sc, sh, dh, dx_res)


def _resid_norm(x, mo, g1, sc, sh):
    s = x.shape[0]
    tm = _row_tile(s)

    def body(x_ref, mo_ref, g_ref, sc_ref, sh_ref, x1_ref, h_ref):
        x1 = x_ref[...] + g_ref[...] * mo_ref[...]
        x1_ref[...] = x1
        h_ref[...] = _modnorm(x1, sc_ref[...], sh_ref[...]).astype(BF16)

    row = pl.BlockSpec((tm, D), lambda i: (i, 0))
    vec = _full((1, D))
    return pl.pallas_call(
        body, name="resid_norm", grid=(s // tm,), in_specs=[row, row, vec, vec, vec], out_specs=[row, row],
        out_shape=[jax.ShapeDtypeStruct((s, D), F32), jax.ShapeDtypeStruct((s, D), BF16)],
    )(x, mo, g1, sc, sh)


def _resid_norm_bwd(x1, mo, g1, sc, sh, dh, dx2):
    s = x1.shape[0]
    tm = _row_tile(s)

    def body(x1_ref, mo_ref, g_ref, sc_ref, sh_ref, dh_ref, dx2_ref, dx1_ref, dmo_ref, dg_ref, dsc_ref, dsh_ref):
        _, vjp = jax.vjp(_modnorm, x1_ref[...], sc_ref[...], sh_ref[...])
        dx, dsc, dsh = vjp(dh_ref[...])
        dx1 = dx2_ref[...] + dx
        dx1_ref[...] = dx1
        dmo_ref[...] = (dx1 * g_ref[...]).astype(BF16)

        @pl.when(pl.program_id(0) == 0)
        def _():
            dg_ref[...] = jnp.zeros_like(dg_ref)
            dsc_ref[...] = jnp.zeros_like(dsc_ref)
            dsh_ref[...] = jnp.zeros_like(dsh_ref)

        dg_ref[...] += jnp.sum(dx1 * mo_ref[...], axis=0, keepdims=True)
        dsc_ref[...] += dsc
        dsh_ref[...] += dsh

    row = pl.BlockSpec((tm, D), lambda i: (i, 0))
    vec = _full((1, D))
    vshape = jax.ShapeDtypeStruct((1, D), F32)
    return pl.pallas_call(
        body, name="resid_norm_bwd", grid=(s // tm,), in_specs=[row, row, vec, vec, vec, row, row],
        out_specs=[row, row, vec, vec, vec],
        out_shape=[jax.ShapeDtypeStruct((s, D), F32), jax.ShapeDtypeStruct((s, D), BF16), vshape, vshape, vshape],
    )(x1, mo, g1, sc, sh, dh, dx2)


def _loss_head(x1, fo, g2, fw, t):
    s = x1.shape[0]
    tm = _row_tile(s)

    def body(x1_ref, fo_ref, g_ref, fw_ref, t_ref, loss_ref, dx_ref, dfo_ref, dg_ref, dfw_ref):
        loss, (dx1, dfo, dg, dfw) = jax.value_and_grad(_loss_tile, argnums=(0, 1, 2, 3))(
            x1_ref[...], fo_ref[...], g_ref[...], fw_ref[...], t_ref[...])
        dx_ref[...] = dx1
        dfo_ref[...] = dfo.astype(BF16)

        @pl.when(pl.program_id(0) == 0)
        def _():
            loss_ref[...] = jnp.zeros_like(loss_ref)
            dg_ref[...] = jnp.zeros_like(dg_ref)
            dfw_ref[...] = jnp.zeros_like(dfw_ref)

        loss_ref[...] += jnp.full(loss_ref.shape, loss, F32)
        dg_ref[...] += dg
        dfw_ref[...] += dfw

    row = pl.BlockSpec((tm, D), lambda i: (i, 0))
    vec = _full((1, D))
    vshape = jax.ShapeDtypeStruct((1, D), F32)
    return pl.pallas_call(
        body, name="loss_head", grid=(s // tm,), in_specs=[row, row, vec, vec, row],
        out_specs=[_full((8, 128)), row, row, vec, vec],
        out_shape=[jax.ShapeDtypeStruct((8, 128), F32), jax.ShapeDtypeStruct((s, D), F32),
                   jax.ShapeDtypeStruct((s, D), BF16), vshape, vshape],
    )(x1, fo, g2, fw, t)


def _sgu_fwd(proj, norm_w, ws_b, bs_t):
    s = proj.shape[0]
    tm = _pick(s, (256, 128))

    def body(u_ref, v_ref, w_ref, ws_ref, bs_ref, y_ref):
        ug, vn = _sgu_pre(u_ref[...], v_ref[...], w_ref[...])
        vnb = vn.astype(BF16)
        for c in range(tm // Q):
            r = slice(c * Q, (c + 1) * Q)
            for g in range(GM_G):
                cs = slice(g * 128, (g + 1) * 128)
                sv = jnp.dot(ws_ref[g], vnb[r, cs], preferred_element_type=F32) + bs_ref[:, g:g + 1]
                y_ref[r, cs] = (ug[r, cs] * sv).astype(BF16)

    return pl.pallas_call(
        body, name="sgu_fwd", grid=(s // tm,),
        in_specs=[pl.BlockSpec((tm, D), lambda i: (i, 0)), pl.BlockSpec((tm, D), lambda i: (i, 1)),
                  _full((1, D)), _full((GM_G, Q, Q)), _full((Q, GM_G))],
        out_specs=pl.BlockSpec((tm, D), lambda i: (i, 0)),
        out_shape=jax.ShapeDtypeStruct((s, D), BF16),
    )(proj, proj, norm_w, ws_b, bs_t)


def _sgu_bwd(proj, norm_w, ws_b, bs_t, dy, dproj):
    s = proj.shape[0]
    tm = _pick(s, (256, 128))

    def body(u_ref, v_ref, w_ref, ws_ref, bs_ref, dy_ref, _, duv_ref, dw_ref, dws_ref, dbs_ref, dug_scr, dvn_scr):
        @pl.when(pl.program_id(0) == 0)
        def _():
            dw_ref[...] = jnp.zeros_like(dw_ref)
            dws_ref[...] = jnp.zeros_like(dws_ref)
            dbs_ref[...] = jnp.zeros_like(dbs_ref)

        (ug, vn), vjp = jax.vjp(_sgu_pre, u_ref[...], v_ref[...], w_ref[...])
        vnb = vn.astype(BF16)
        dy = dy_ref[...]
        causal = _tril(Q).astype(F32)
        for c in range(tm // Q):
            r = slice(c * Q, (c + 1) * Q)
            for g in range(GM_G):
                cs = slice(g * 128, (g + 1) * 128)
                blk = vnb[r, cs]
                sv = jnp.dot(ws_ref[g], blk, preferred_element_type=F32) + bs_ref[:, g:g + 1]
                dug_scr[r, cs] = dy[r, cs] * sv
                dsv = dy[r, cs] * ug[r, cs]
                dsvb = dsv.astype(BF16)
                dws_ref[g] += causal * lax.dot_general(dsvb, blk, (((1,), (1,)), ((), ())), preferred_element_type=F32)
                dbs_ref[:, g:g + 1] += jnp.sum(dsv, axis=1, keepdims=True)
                dvn_scr[r, cs] = lax.dot_general(ws_ref[g], dsvb, (((0,), (0,)), ((), ())), preferred_element_type=F32)
        du, dv, dw = vjp((dug_scr[...], dvn_scr[...]))
        duv_ref[:, :D] = du.astype(BF16)
        duv_ref[:, D:] = dv.astype(BF16)
        dw_ref[...] += dw

    return pl.pallas_call(
        body, name="sgu_bwd", grid=(s // tm,),
        in_specs=[pl.BlockSpec((tm, D), lambda i: (i, 0)), pl.BlockSpec((tm, D), lambda i: (i, 1)),
                  _full((1, D)), _full((GM_G, Q, Q)), _full((Q, GM_G)), pl.BlockSpec((tm, D), lambda i: (i, 0)), ANY],
        out_specs=[pl.BlockSpec((tm, 2 * D), lambda i: (i, 0)), _full((1, D)), _full((GM_G, Q, Q)), _full((Q, GM_G))],
        out_shape=[jax.ShapeDtypeStruct(dproj.shape, BF16), jax.ShapeDtypeStruct((1, D), F32),
                   jax.ShapeDtypeStruct((GM_G, Q, Q), F32), jax.ShapeDtypeStruct((Q, GM_G), F32)],
        scratch_shapes=[pltpu.VMEM((tm, D), F32), pltpu.VMEM((tm, D), F32)],
        input_output_aliases={6: 0},
    )(proj, proj, norm_w, ws_b, bs_t, dy, dproj)


def _gatenorm_fwd(y_ssd, proj, norm_w):
    s = y_ssd.shape[0]
    tm = _pick(s, (256, 128))
    gw = DI // NGRP

    def body(y_ref, z_ref, w_ref, o_ref):
        for g in range(NGRP):
            cs = slice(g * gw, (g + 1) * gw)
            o_ref[:, cs] = _gatenorm(y_ref[:, cs], z_ref[:, cs], w_ref[:, cs]).astype(BF16)

    return pl.pallas_call(
        body, name="gatenorm_fwd", grid=(s // tm,),
        in_specs=[pl.BlockSpec((tm, DI), lambda i: (i, 0)), pl.BlockSpec((tm, DI), lambda i: (i, OFF_Z // DI)), _full((1, DI))],
        out_specs=pl.BlockSpec((tm, DI), lambda i: (i, 0)),
        out_shape=jax.ShapeDtypeStruct((s, DI), BF16),
    )(y_ssd, proj, norm_w)


def _gatenorm_bwd(y_ssd, proj, norm_w, dyb, dproj):
    s = y_ssd.shape[0]
    tm = _pick(s, (256, 128))
    gw = DI // NGRP

    def body(y_ref, z_ref, w_ref, d_ref, _, dy_ref, dz_ref, dw_ref):
        @pl.when(pl.program_id(0) == 0)
        def _():
            dw_ref[...] = jnp.zeros_like(dw_ref)

        for g in range(NGRP):
            cs = slice(g * gw, (g + 1) * gw)
            _, vjp = jax.vjp(_gatenorm, y_ref[:, cs], z_ref[:, cs], w_ref[:, cs])
            dy, dz, dw = vjp(d_ref[:, cs])
            dy_ref[:, cs] = dy
            dz_ref[:, cs] = dz.astype(BF16)
            dw_ref[:, cs] += dw

    blk = pl.BlockSpec((tm, DI), lambda i: (i, 0))
    zblk = pl.BlockSpec((tm, DI), lambda i: (i, OFF_Z // DI))
    return pl.pallas_call(
        body, name="gatenorm_bwd", grid=(s // tm,),
        in_specs=[blk, zblk, _full((1, DI)), blk, ANY],
        out_specs=[blk, zblk, _full((1, DI))],
        out_shape=[jax.ShapeDtypeStruct((s, DI), F32), jax.ShapeDtypeStruct(dproj.shape, BF16), jax.ShapeDtypeStruct((1, DI), F32)],
        input_output_aliases={4: 1},
    )(y_ssd, proj, norm_w, dyb, dproj)


def _mix_fwd(proj, ba, bb):
    s = proj.shape[0]
    tm = _row_tile(s)
    gb0 = OFF_GA // D

    def body(ga_ref, gb_ref, ba_ref, bb_ref, o_ref):
        o_ref[...] = _mix(ga_ref[...], gb_ref[...], ba_ref[...], bb_ref[...]).astype(BF16)

    row = pl.BlockSpec((tm, D), lambda i: (i, 0))
    return pl.pallas_call(
        body, name="mix_fwd", grid=(s // tm,),
        in_specs=[pl.BlockSpec((tm, D), lambda i: (i, gb0)), pl.BlockSpec((tm, D), lambda i: (i, gb0 + 1)), row, row],
        out_specs=row, out_shape=jax.ShapeDtypeStruct((s, D), BF16),
    )(proj, proj, ba, bb)


def _mix_bwd(proj, ba, bb, dmixed):
    s = proj.shape[0]
    tm = _row_tile(s)
    gb0 = OFF_GA // D

    def body(ga_ref, gb_ref, ba_ref, bb_ref, d_ref, dg_ref, dba_ref, dbb_ref):
        _, vjp = jax.vjp(_mix, ga_ref[...], gb_ref[...], ba_ref[...], bb_ref[...])
        dga, dgb, dba, dbb = vjp(d_ref[...])
        dg_ref[:, :D] = dga.astype(BF16)
        dg_ref[:, D:] = dgb.astype(BF16)
        dba_ref[...] = dba.astype(BF16)
        dbb_ref[...] = dbb.astype(BF16)

    row = pl.BlockSpec((tm, D), lambda i: (i, 0))
    return pl.pallas_call(
        body, name="mix_bwd", grid=(s // tm,),
        in_specs=[pl.BlockSpec((tm, D), lambda i: (i, gb0)), pl.BlockSpec((tm, D), lambda i: (i, gb0 + 1)), row, row, row],
        out_specs=[pl.BlockSpec((tm, 2 * D), lambda i: (i, OFF_GA // (2 * D))), row, row],
        out_shape=[jax.ShapeDtypeStruct((s, W_IN_R), BF16), jax.ShapeDtypeStruct((s, D), BF16), jax.ShapeDtypeStruct((s, D), BF16)],
    )(proj, proj, ba, bb, dmixed)


CONV_TC = 1024


def _conv_fwd(proj, conv_w, conv_b):
    s = proj.shape[0]
    tm = _row_tile(s)
    cb0 = OFF_XBC // CONV_TC

    def body(x_ref, halo_ref, w_ref, b_ref, o_ref):
        halo = jnp.where(pl.program_id(0) > 0, halo_ref[...], 0.0)
        ext = jnp.concatenate([halo, x_ref[...]], axis=0)
        acc = jnp.broadcast_to(b_ref[...], (tm, CONV_TC))
        for k in range(4):
            shifted = ext if k == 3 else pltpu.roll(ext, 3 - k, 0)
            acc = acc + w_ref[k:k + 1, :] * shifted[8:, :]
        o_ref[...] = jax.nn.silu(acc)

    return pl.pallas_call(
        body, name="conv_fwd", grid=(s // tm, CONV // CONV_TC),
        in_specs=[pl.BlockSpec((tm, CONV_TC), lambda i, j: (i, cb0 + j)),
                  pl.BlockSpec((8, CONV_TC), lambda i, j: (jnp.maximum(i * (tm // 8) - 1, 0), cb0 + j)),
                  pl.BlockSpec((4, CONV_TC), lambda i, j: (0, j)), pl.BlockSpec((1, CONV_TC), lambda i, j: (0, j))],
        out_specs=pl.BlockSpec((tm, CONV_TC), lambda i, j: (i, j)),
        out_shape=jax.ShapeDtypeStruct((s, CONV), F32),
    )(proj, proj, conv_w, conv_b)


def _conv_bwd(proj, conv_w, conv_b, dact, dproj, col0, name):
    s, width = dact.shape
    tm = _row_tile(s)
    nt = s // tm
    c0 = col0 // CONV_TC
    cb0 = OFF_XBC // CONV_TC + c0

    def body(x_ref, prev_ref, next_ref, d_ref, dnext_ref, w_ref, b_ref, _, dx_ref, dw_ref, db_ref):
        i = pl.program_id(1)
        prev = jnp.where(i > 0, prev_ref[...], 0.0)
        ext = jnp.concatenate([prev, x_ref[...], next_ref[...]], axis=0)
        dext = jnp.concatenate([d_ref[...], jnp.where(i < nt - 1, dnext_ref[...], 0.0)], axis=0)
        pre = jnp.broadcast_to(b_ref[...], (tm + 8, CONV_TC))
        taps = []
        for k in range(4):
            shifted = (ext if k == 3 else pltpu.roll(ext, 3 - k, 0))[8:, :]
            taps.append(shifted)
            pre = pre + w_ref[k:k + 1, :] * shifted
        sig = jax.nn.sigmoid(pre)
        dpre = dext * (sig * (1.0 + pre * (1.0 - sig)))
        dx = jnp.zeros((tm, CONV_TC), F32)
        for k in range(4):
            shifted = dpre if k == 3 else pltpu.roll(dpre, tm + 8 - (3 - k), 0)
            dx = dx + w_ref[k:k + 1, :] * shifted[:tm, :]
        dx_ref[...] = dx.astype(BF16)

        @pl.when(i == 0)
        def _():
            dw_ref[...] = jnp.zeros_like(dw_ref)
            db_ref[...] = jnp.zeros_like(db_ref)

        dtile = dpre[:tm, :]
        for k in range(4):
            dw_ref[k:k + 1, :] += jnp.sum(dtile * taps[k][:tm, :], axis=0, keepdims=True)
        db_ref[...] += jnp.sum(dtile, axis=0, keepdims=True)

    r8 = tm // 8
    return pl.pallas_call(
        body, name=name, grid=(width // CONV_TC, nt),
        in_specs=[pl.BlockSpec((tm, CONV_TC), lambda j, i: (i, cb0 + j)),
                  pl.BlockSpec((8, CONV_TC), lambda j, i: (jnp.maximum(i * r8 - 1, 0), cb0 + j)),
                  pl.BlockSpec((8, CONV_TC), lambda j, i: (jnp.minimum((i + 1) * r8, nt * r8 - 1), cb0 + j)),
                  pl.BlockSpec((tm, CONV_TC), lambda j, i: (i, j)),
                  pl.BlockSpec((8, CONV_TC), lambda j, i: (jnp.minimum((i + 1) * r8, nt * r8 - 1), j)),
                  pl.BlockSpec((4, CONV_TC), lambda j, i: (0, c0 + j)), pl.BlockSpec((1, CONV_TC), lambda j, i: (0, c0 + j)), ANY],
        out_specs=[pl.BlockSpec((tm, CONV_TC), lambda j, i: (i, cb0 + j)),
                   pl.BlockSpec((4, CONV_TC), lambda j, i: (0, j)), pl.BlockSpec((1, CONV_TC), lambda j, i: (0, j))],
        out_shape=[jax.ShapeDtypeStruct(dproj.shape, BF16), jax.ShapeDtypeStruct((4, width), F32), jax.ShapeDtypeStruct((1, width), F32)],
        input_output_aliases={7: 0},
    )(proj, proj, proj, dact, dact, conv_w, conv_b, dproj)


def _dt_fwd(proj, dtb, alog):
    s = proj.shape[0]
    nc = s // Q

    def body(r_ref, b_ref, a_ref, dt_ref, cum_ref, cumt_ref):
        dt, cum, cum_t = _dt_prep(r_ref[...], b_ref[...], a_ref[...])
        dt_ref[...] = dt
        cum_ref[...] = cum
        cumt_ref[...] = cum_t

    blk = pl.BlockSpec((Q, 128), lambda n: (n, 0))
    return pl.pallas_call(
        body, name="dt_fwd", grid=(nc,),
        in_specs=[pl.BlockSpec((Q, 128), lambda n: (n, OFF_DT // 128)), _full((1, 128)), _full((1, 128))],
        out_specs=[blk, blk, pl.BlockSpec((None, 128, Q), lambda n: (n, 0, 0))],
        out_shape=[jax.ShapeDtypeStruct((s, 128), F32), jax.ShapeDtypeStruct((s, 128), F32), jax.ShapeDtypeStruct((nc, 128, Q), F32)],
    )(proj, dtb, alog)


def _dt_bwd(proj, dtb, alog, ddt, dcum, dcumt, dproj):
    s = proj.shape[0]
    nc = s // Q

    def body(r_ref, b_ref, a_ref, ddt_ref, dcum_ref, dcumt_ref, _, dr_ref, db_ref, da_ref):
        _, vjp = jax.vjp(_dt_prep, r_ref[...], b_ref[...], a_ref[...])
        dr, db, da = vjp((ddt_ref[...], dcum_ref[...], dcumt_ref[...]))
        dr_ref[...] = dr.astype(BF16)

        @pl.when(pl.program_id(0) == 0)
        def _():
            db_ref[...] = jnp.zeros_like(db_ref)
            da_ref[...] = jnp.zeros_like(da_ref)

        db_ref[...] += db
        da_ref[...] += da

    blk = pl.BlockSpec((Q, 128), lambda n: (n, 0))
    pblk = pl.BlockSpec((Q, 128), lambda n: (n, OFF_DT // 128))
    return pl.pallas_call(
        body, name="dt_bwd", grid=(nc,),
        in_specs=[pblk, _full((1, 128)), _full((1, 128)), blk, blk, pl.BlockSpec((None, 128, Q), lambda n: (n, 0, 0)), ANY],
        out_specs=[pblk, _full((1, 128)), _full((1, 128))],
        out_shape=[jax.ShapeDtypeStruct(dproj.shape, BF16), jax.ShapeDtypeStruct((1, 128), F32), jax.ShapeDtypeStruct((1, 128), F32)],
        input_output_aliases={6: 0},
    )(proj, dtb, alog, ddt, dcum, dcumt, dproj)


def _ssd_specs(chunk_of):
    gs = SSD_GPS
    xs = pl.BlockSpec((Q, 256 * gs), lambda n, g: (chunk_of(n), g))
    bm = pl.BlockSpec((Q, 128 * gs), lambda n, g: (chunk_of(n), DI // (128 * gs) + g))
    cm = pl.BlockSpec((Q, 128 * gs), lambda n, g: (chunk_of(n), (DI + D) // (128 * gs) + g))
    per_chunk = pl.BlockSpec((Q, 128), lambda n, g: (chunk_of(n), 0))
    cum_t = pl.BlockSpec((None, 128, Q), lambda n, g: (chunk_of(n), 0, 0))
    state = pl.BlockSpec((None, 256 * gs, 128), lambda n, g: (chunk_of(n), g, 0))
    vec = pl.BlockSpec((1, 128), lambda n, g: (0, 0))
    return xs, bm, cm, per_chunk, cum_t, state, vec


SSD_GPS = 8


def _aligned(v, m):
    return v if isinstance(v, int) else pl.multiple_of(v, m)


def _ssd_fwd(xbc, dt, cum, cum_t, dsk):
    s = xbc.shape[0]
    nc = s // Q
    xs, bm, cm, per_chunk, cumt_spec, state_spec, vec = _ssd_specs(lambda n: n)
    gs = SSD_GPS

    def body(x_ref, b_ref, c_ref, dt_ref, cum_ref, cumt_ref, dsk_ref, y_ref, st_ref, carry):
        n, gstep = pl.program_id(0), (0 if gs == NGRP else pl.program_id(1))
        rows = pl.ds(_aligned(gstep * (256 * gs), 256 * gs), 256 * gs)

        @pl.when(n == 0)
        def _():
            carry[rows, :] = jnp.zeros((256 * gs, 128), F32)

        st_ref[...] = carry[rows, :]
        for k in range(gs):
            xo, so, bo = 256 * k, 256 * k, 128 * k
            y0, y1, n0, n1 = _ssd_group(
                x_ref[:, xo:xo + 128], x_ref[:, xo + 128:xo + 256], st_ref[so:so + 128, :], st_ref[so + 128:so + 256, :],
                b_ref[:, bo:bo + 128], c_ref[:, bo:bo + 128], dt_ref[...], cum_ref[...], cumt_ref[...], dsk_ref[...],
                gstep * gs + k)
            y_ref[:, xo:xo + 128] = y0
            y_ref[:, xo + 128:xo + 256] = y1
            base = gstep * (256 * gs) + so
            carry[pl.ds(_aligned(base, 128), 128), :] = n0
            carry[pl.ds(_aligned(base + 128, 128), 128), :] = n1

    return pl.pallas_call(
        body, name="ssd_fwd", grid=(nc, NGRP // gs),
        in_specs=[xs, bm, cm, per_chunk, per_chunk, cumt_spec, vec],
        out_specs=[xs, state_spec],
        out_shape=[jax.ShapeDtypeStruct((s, DI), F32), jax.ShapeDtypeStruct((nc, DI, 128), F32)],
        scratch_shapes=[pltpu.VMEM((DI, 128), F32)],
    )(xbc, xbc, xbc, dt, cum, cum_t, dsk)


def _ssd_bwd(xbc, dt, cum, cum_t, dsk, states, dy, comm=None):
    s = xbc.shape[0]
    nc = s // Q
    xs, bm, cm, per_chunk, cumt_spec, state_spec, vec = _ssd_specs(lambda n: nc - 1 - n)
    gs = SSD_GPS
    n_ci, n_co = (len(comm.ins), len(comm.outs)) if comm is not None else (0, 0)
    steps = (nc, NGRP // gs)

    def body(*refs):
        x_ref, b_ref, c_ref, dt_ref, cum_ref, cumt_ref, dsk_ref, st_ref, dy_ref = refs[:9]
        ci_refs = refs[9:9 + n_ci]
        dx_ref, db_ref, dc_ref, ddt_ref, dcum_ref, dcumt_ref, ddsk_ref = refs[9 + n_ci:16 + n_ci]
        co_refs = refs[16 + n_ci:16 + n_ci + n_co]
        carry = refs[16 + n_ci + n_co]
        cs_refs = refs[17 + n_ci + n_co:]
        n, gstep = pl.program_id(0), (0 if gs == NGRP else pl.program_id(1))
        rows = pl.ds(_aligned(gstep * (256 * gs), 256 * gs), 256 * gs)
        if comm is not None:
            pl.when((pl.program_id(0) == 0) & (pl.program_id(1) == 0))(lambda: comm.start(ci_refs, co_refs, cs_refs))

        @pl.when(n == 0)
        def _():
            carry[rows, :] = jnp.zeros((256 * gs, 128), F32)

        def zero_skip_sum():
            ddsk_ref[...] = jnp.zeros_like(ddsk_ref)

        def zero_chunk_sums():
            ddt_ref[...] = jnp.zeros_like(ddt_ref)
            dcum_ref[...] = jnp.zeros_like(dcum_ref)
            dcumt_ref[...] = jnp.zeros_like(dcumt_ref)

        if isinstance(gstep, int):
            pl.when(n == 0)(zero_skip_sum)
            zero_chunk_sums()
        else:
            pl.when((n == 0) & (gstep == 0))(zero_skip_sum)
            pl.when(gstep == 0)(zero_chunk_sums)

        for k in range(gs):
            xo, so, bo = 256 * k, 256 * k, 128 * k
            base = gstep * (256 * gs) + so
            lo = pl.ds(_aligned(base, 128), 128)
            hi = pl.ds(_aligned(base + 128, 128), 128)
            fn = functools.partial(_ssd_group, grp=gstep * gs + k)
            _, vjp = jax.vjp(fn, x_ref[:, xo:xo + 128], x_ref[:, xo + 128:xo + 256], st_ref[so:so + 128, :],
                             st_ref[so + 128:so + 256, :], b_ref[:, bo:bo + 128], c_ref[:, bo:bo + 128],
                             dt_ref[...], cum_ref[...], cumt_ref[...], dsk_ref[...])
            dx0, dx1, ds0, ds1, dbm, dcm, ddt, dcum, dcumt, ddsk = vjp(
                (dy_ref[:, xo:xo + 128], dy_ref[:, xo + 128:xo + 256], carry[lo, :], carry[hi, :]))
            dx_ref[:, xo:xo + 128] = dx0
            dx_ref[:, xo + 128:xo + 256] = dx1
            db_ref[:, bo:bo + 128] = dbm
            dc_ref[:, bo:bo + 128] = dcm
            ddt_ref[...] += ddt
            dcum_ref[...] += dcum
            dcumt_ref[...] += dcumt
            ddsk_ref[...] += ddsk
            carry[lo, :] = ds0
            carry[hi, :] = ds1

        if comm is not None:
            pl.when((pl.program_id(0) == steps[0] - 1) & (pl.program_id(1) == steps[1] - 1))(
                lambda: comm.finish(ci_refs, co_refs, cs_refs))

    grp_blk = pl.BlockSpec((Q, 128 * gs), lambda n, g: (nc - 1 - n, g))
    extra_in = comm.ins if comm is not None else []
    extra_out = comm.outs if comm is not None else []
    extra_sems = comm.sems if comm is not None else []
    return pl.pallas_call(
        body, name="ssd_bwd", grid=steps,
        in_specs=[xs, bm, cm, per_chunk, per_chunk, cumt_spec, vec, state_spec, xs] + [ANY] * n_ci,
        out_specs=[xs, grp_blk, grp_blk, per_chunk, per_chunk, cumt_spec, vec] + [ANY] * n_co,
        out_shape=[jax.ShapeDtypeStruct((s, DI), F32), jax.ShapeDtypeStruct((s, D), F32), jax.ShapeDtypeStruct((s, D), F32),
                   jax.ShapeDtypeStruct((s, 128), F32), jax.ShapeDtypeStruct((s, 128), F32),
                   jax.ShapeDtypeStruct((nc, 128, Q), F32), jax.ShapeDtypeStruct((1, 128), F32)] + extra_out,
        scratch_shapes=[pltpu.VMEM((DI, 128), F32)] + extra_sems,
    )(xbc, xbc, xbc, dt, cum, cum_t, dsk, states, dy, *extra_in)


def _adam_math(w, g, m, v):
    m2 = ADAM_B1 * m + (1.0 - ADAM_B1) * g
    v2 = ADAM_B2 * v + (1.0 - ADAM_B2) * jnp.square(g)
    m_hat = m2 / (1.0 - ADAM_B1 ** ADAM_STEP)
    v_hat = v2 / (1.0 - ADAM_B2 ** ADAM_STEP)
    delta = -ADAM_LR * (m_hat / (jnp.sqrt(v_hat) + ADAM_EPS) + ADAM_WD * w)
    return delta, m2, v2


def _adam(w, g, m, v, name):
    r, c = w.shape
    tr = r if r * c * 4 <= (1 << 20) else _pick(r, (128, 64, 32, 16, 8))

    def body(w_ref, g_ref, m_ref, v_ref, d_ref, m2_ref, v2_ref):
        d, m2, v2 = _adam_math(w_ref[...], g_ref[...], m_ref[...], v_ref[...])
        d_ref[...] = d
        m2_ref[...] = m2
        v2_ref[...] = v2

    blk = pl.BlockSpec((tr, c), lambda i: (i, 0))
    shp = jax.ShapeDtypeStruct((r, c), F32)
    return pl.pallas_call(body, name=name, grid=(r // tr,), in_specs=[blk] * 4, out_specs=[blk] * 3,
                          out_shape=[shp] * 3)(w, g, m, v)


def _sum_leading(xs, name, out_dtype=F32, tr=256):
    if isinstance(xs, tuple):
        a, b = xs
        n, r, c = a.shape
        tr = _pick(r, (tr, 128, 64, 32, 16, 8))

        def body2(a_ref, b_ref, o_ref):
            o_ref[...] = (a_ref[...].astype(F32) + b_ref[...].astype(F32)).astype(out_dtype)

        blk = pl.BlockSpec((None, tr, c), lambda s, i: (s, i, 0))
        return pl.pallas_call(body2, name=name, grid=(n, r // tr), in_specs=[blk, blk], out_specs=blk,
                              out_shape=jax.ShapeDtypeStruct((n, r, c), out_dtype))(a, b)
    n, r, c = xs.shape
    tr = r if n * r * c * 4 <= (8 << 20) else _pick(r, (tr, 128, 64, 32, 16, 8))

    def body(x_ref, o_ref):
        acc = x_ref[0].astype(F32)
        for s in range(1, n):
            acc = acc + x_ref[s].astype(F32)
        o_ref[...] = acc.astype(out_dtype)

    return pl.pallas_call(body, name=name, grid=(r // tr,), in_specs=[pl.BlockSpec((n, tr, c), lambda i: (0, i, 0))],
                          out_specs=pl.BlockSpec((tr, c), lambda i: (i, 0)),
                          out_shape=jax.ShapeDtypeStruct((r, c), out_dtype))(xs)


def _mod_fwd(c8, w_mod, b_sl):
    def body(c_ref, w_ref, b_ref, o_ref):
        o_ref[...] = jnp.dot(jax.nn.silu(c_ref[...]), w_ref[...], precision=HIGHEST, preferred_element_type=F32) + b_ref[...]

    return pl.pallas_call(body, name="mod_fwd", in_specs=[VMEM, VMEM, VMEM], out_specs=VMEM,
                          out_shape=jax.ShapeDtypeStruct((N_DEV, w_mod.shape[1]), F32))(c8, w_mod, b_sl)


def _mod_wgrad(c8, dmod):
    def body(c_ref, d_ref, o_ref):
        o_ref[...] = lax.dot_general(jax.nn.silu(c_ref[...]), d_ref[...], (((0,), (0,)), ((), ())),
                                     precision=HIGHEST, preferred_element_type=F32)

    return pl.pallas_call(body, name="mod_wgrad", in_specs=[VMEM, VMEM], out_specs=VMEM,
                          out_shape=jax.ShapeDtypeStruct((D, dmod.shape[1]), F32))(c8, dmod)


def _place():
    x, y, c = lax.axis_index("x"), lax.axis_index("y"), lax.axis_index("c")
    chips = [(1 - x, y), (x, 1 - y), (1 - x, 1 - y)]
    return x, y, c, chips


def _all_gather_small(v, name):
    r, w = v.shape

    def body(v_ref, o_ref, send_sems, recv_sems, local_sem):
        x, y, c, _ = _place()
        me = 4 * x + 2 * y + c
        own = pltpu.make_async_copy(v_ref, o_ref.at[me], local_sem)
        own.start()
        sends = []
        for k in range(1, N_DEV):
            tx = 1 - x if k & 4 else x
            ty = 1 - y if k & 2 else y
            tc = 1 - c if k & 1 else c
            peer = 4 * tx + 2 * ty + tc
            cp = pltpu.make_async_remote_copy(src_ref=v_ref, dst_ref=o_ref.at[me], send_sem=send_sems.at[k - 1],
                                              recv_sem=recv_sems.at[k - 1], device_id=(tx, ty, tc), device_id_type=MESH)
            cp.start()
            sends.append((cp, peer, (tx, ty, tc)))
        for k, (cp, peer, dev) in enumerate(sends):
            pltpu.make_async_remote_copy(src_ref=v_ref, dst_ref=o_ref.at[peer], send_sem=send_sems.at[k],
                                         recv_sem=recv_sems.at[k], device_id=dev, device_id_type=MESH).wait_recv()
        for cp, _, _ in sends:
            cp.wait_send()
        own.wait()

    return pl.pallas_call(
        body, name=name, in_specs=[VMEM], out_specs=VMEM, out_shape=jax.ShapeDtypeStruct((N_DEV, r, w), v.dtype),
        scratch_shapes=[pltpu.SemaphoreType.DMA((N_DEV - 1,)), pltpu.SemaphoreType.DMA((N_DEV - 1,)), pltpu.SemaphoreType.DMA],
    )(v)


def _all_gather_weights(shards):
    n = len(shards)
    nsem = n * 3 * AG_CHUNKS

    def body(*refs):
        srcs, dsts = refs[:n], refs[n:2 * n]
        send_sems, recv_sems, fwd_send_sems, fwd_recv_sems = refs[2 * n:]
        x, y, c, chips = _place()
        q = 2 * x + y
        sibling = (x, y, 1 - c)
        sends = []
        for r in range(AG_CHUNKS):
            for t in range(n):
                hc = srcs[t].shape[0] // 2 // AG_CHUNKS
                rows = pl.ds(c * (hc * AG_CHUNKS) + r * hc, hc)
                for j, (cx, cy) in enumerate(chips):
                    k = (t * 3 + j) * AG_CHUNKS + r
                    cp = pltpu.make_async_remote_copy(src_ref=srcs[t].at[rows], dst_ref=dsts[t].at[q, rows],
                                                      send_sem=send_sems.at[k], recv_sem=recv_sems.at[k],
                                                      device_id=(cx, cy, c), device_id_type=MESH)
                    cp.start()
                    sends.append(cp)
        fwds = []
        for r in range(AG_CHUNKS):
            for t in range(n):
                hc = srcs[t].shape[0] // 2 // AG_CHUNKS
                sub = hc // D2D_SPLIT
                for j, (cx, cy) in enumerate(chips):
                    k = (t * 3 + j) * AG_CHUNKS + r
                    base = c * (hc * AG_CHUNKS) + r * hc
                    part = dsts[t].at[2 * cx + cy, pl.ds(base, hc)]
                    pltpu.make_async_remote_copy(src_ref=part, dst_ref=part, send_sem=send_sems.at[k], recv_sem=recv_sems.at[k],
                                                 device_id=(cx, cy, c), device_id_type=MESH).wait_recv()
                    for u in range(D2D_SPLIT):
                        piece = dsts[t].at[2 * cx + cy, pl.ds(base + u * sub, sub)]
                        pltpu.make_async_remote_copy(src_ref=piece, dst_ref=piece, send_sem=fwd_send_sems.at[k],
                                                     recv_sem=fwd_recv_sems.at[k], device_id=sibling, device_id_type=MESH).start()
                    fwds.append((part, k))
        for r in range(AG_CHUNKS):
            for t in range(n):
                hc = srcs[t].shape[0] // 2 // AG_CHUNKS
                for j, (cx, cy) in enumerate(chips):
                    k = (t * 3 + j) * AG_CHUNKS + r
                    part = dsts[t].at[2 * cx + cy, pl.ds((1 - c) * (hc * AG_CHUNKS) + r * hc, hc)]
                    pltpu.make_async_remote_copy(src_ref=part, dst_ref=part, send_sem=fwd_send_sems.at[k],
                                                 recv_sem=fwd_recv_sems.at[k], device_id=sibling, device_id_type=MESH).wait_recv()
        for cp in sends:
            cp.wait_send()
        for part, k in fwds:
            pltpu.make_async_remote_copy(src_ref=part, dst_ref=part, send_sem=fwd_send_sems.at[k], recv_sem=fwd_recv_sems.at[k],
                                         device_id=sibling, device_id_type=MESH).wait_send()

    return pl.pallas_call(
        body, name="all_gather_weights", in_specs=[ANY] * n, out_specs=[ANY] * n,
        out_shape=[jax.ShapeDtypeStruct((N_CHIP,) + s.shape, s.dtype) for s in shards],
        scratch_shapes=[pltpu.SemaphoreType.DMA((nsem,)), pltpu.SemaphoreType.DMA((nsem,)), pltpu.SemaphoreType.DMA((nsem,)),
                        pltpu.SemaphoreType.DMA((nsem,))],
    )(*shards)


def _exchange_halves(grads, tag=""):
    n = len(grads)

    def body(*refs):
        srcs, theirs = refs[:n], refs[n:2 * n]
        send_sems, recv_sems = refs[2 * n:]
        x, y, c, _ = _place()
        sibling = (x, y, 1 - c)
        waits = []
        for t in range(n):
            h = srcs[t].shape[1] // 2
            sub = h // D2D_SPLIT
            for s in range(N_CHIP):
                for u in range(D2D_SPLIT):
                    pltpu.make_async_remote_copy(src_ref=srcs[t].at[s, pl.ds((1 - c) * h + u * sub, sub)],
                                                 dst_ref=theirs[t].at[s, pl.ds(u * sub, sub)],
                                                 send_sem=send_sems.at[t], recv_sem=recv_sems.at[t],
                                                 device_id=sibling, device_id_type=MESH).start()
            waits.append(pltpu.make_async_remote_copy(src_ref=srcs[t].at[:, pl.ds((1 - c) * h, h)], dst_ref=theirs[t],
                                                      send_sem=send_sems.at[t], recv_sem=recv_sems.at[t],
                                                      device_id=sibling, device_id_type=MESH))
        for whole in waits:
            whole.wait()

    half = [jax.ShapeDtypeStruct((N_CHIP, g.shape[1] // 2, g.shape[2]), g.dtype) for g in grads]
    return pl.pallas_call(
        body, name=f"exchange_halves_{tag}", in_specs=[ANY] * n, out_specs=[ANY] * n, out_shape=half,
        scratch_shapes=[pltpu.SemaphoreType.DMA((n,)), pltpu.SemaphoreType.DMA((n,))],
    )(*grads)


def _chip_exchange_comm(parts):
    n = len(parts)

    def copies(srcs, dsts, sems):
        send_sems, recv_sems, local_sems = sems
        x, y, c, chips = _place()
        q = 2 * x + y
        owns = [pltpu.make_async_copy(srcs[t].at[q], dsts[t].at[q], local_sems.at[t]) for t in range(n)]
        sends, recvs = [], []
        for t in range(n):
            for j, (cx, cy) in enumerate(chips):
                args = dict(send_sem=send_sems.at[3 * t + j], recv_sem=recv_sems.at[3 * t + j],
                            device_id=(cx, cy, c), device_id_type=MESH)
                sends.append(pltpu.make_async_remote_copy(src_ref=srcs[t].at[2 * cx + cy], dst_ref=dsts[t].at[q], **args))
                part = dsts[t].at[2 * cx + cy]
                recvs.append(pltpu.make_async_remote_copy(src_ref=part, dst_ref=part, **args))
        return owns, sends, recvs

    def start(srcs, dsts, sems):
        owns, sends, _ = copies(srcs, dsts, sems)
        for cp in owns + sends:
            cp.start()

    def finish(srcs, dsts, sems):
        owns, sends, recvs = copies(srcs, dsts, sems)
        for cp in recvs:
            cp.wait_recv()
        for cp in sends:
            cp.wait_send()
        for cp in owns:
            cp.wait()

    return _Comm(parts, [jax.ShapeDtypeStruct(p.shape, p.dtype) for p in parts],
                 [pltpu.SemaphoreType.DMA((3 * n,)), pltpu.SemaphoreType.DMA((3 * n,)), pltpu.SemaphoreType.DMA((n,))], start, finish)


def _gather_relay_comm(shard):
    h = shard.shape[0] // 2
    sub = h // D2D_SPLIT

    def place_and_copies(srcs, dsts, sems):
        send_sems, recv_sems, fwd_send_sems, fwd_recv_sems = sems
        x, y, c, chips = _place()
        q = 2 * x + y
        mine, theirs = pl.ds(c * h, h), pl.ds((1 - c) * h, h)
        sends, recvs, relays, relayed = [], [], [], []
        for j, (cx, cy) in enumerate(chips):
            ici = dict(send_sem=send_sems.at[j], recv_sem=recv_sems.at[j], device_id=(cx, cy, c), device_id_type=MESH)
            d2d = dict(send_sem=fwd_send_sems.at[j], recv_sem=fwd_recv_sems.at[j], device_id=(x, y, 1 - c), device_id_type=MESH)
            sends.append(pltpu.make_async_remote_copy(src_ref=srcs[0].at[mine], dst_ref=dsts[0].at[q, mine], **ici))
            landed = dsts[0].at[2 * cx + cy, mine]
            recvs.append(pltpu.make_async_remote_copy(src_ref=landed, dst_ref=landed, **ici))
            pieces = [dsts[0].at[2 * cx + cy, pl.ds(c * h + u * sub, sub)] for u in range(D2D_SPLIT)]
            relays.append(([pltpu.make_async_remote_copy(src_ref=p, dst_ref=p, **d2d) for p in pieces],
                           pltpu.make_async_remote_copy(src_ref=landed, dst_ref=landed, **d2d)))
            other = dsts[0].at[2 * cx + cy, theirs]
            relayed.append(pltpu.make_async_remote_copy(src_ref=other, dst_ref=other, **d2d))
        return sends, recvs, relays, relayed

    def start(srcs, dsts, sems):
        for cp in place_and_copies(srcs, dsts, sems)[0]:
            cp.start()

    def middle(srcs, dsts, sems):
        _, recvs, relays, _ = place_and_copies(srcs, dsts, sems)
        for cp, (pieces, _) in zip(recvs, relays):
            cp.wait_recv()
            for piece in pieces:
                piece.start()

    def finish(srcs, dsts, sems):
        sends, _, relays, relayed = place_and_copies(srcs, dsts, sems)
        for cp in relayed:
            cp.wait_recv()
        for cp in sends:
            cp.wait_send()
        for _, whole in relays:
            whole.wait_send()

    comm = _Comm([shard], [jax.ShapeDtypeStruct((N_CHIP,) + shard.shape, shard.dtype)],
                 [pltpu.SemaphoreType.DMA((3,))] * 4, start, finish)
    comm.middle = middle
    return comm


def _run_comm(comm, name):
    n_ci, n_co = len(comm.ins), len(comm.outs)

    def body(*refs):
        ci, co, cs = refs[:n_ci], refs[n_ci:n_ci + n_co], refs[n_ci + n_co:]
        comm.start(ci, co, cs)
        if comm.middle is not None:
            comm.middle(ci, co, cs)
        comm.finish(ci, co, cs)

    return pl.pallas_call(body, name=name, in_specs=[ANY] * n_ci, out_specs=[ANY] * n_co, out_shape=comm.outs,
                          scratch_shapes=comm.sems)(*comm.ins)


def _share_halves(halves, tag=""):
    n = len(halves)

    def body(*refs):
        srcs, dsts = refs[:n], refs[n:2 * n]
        send_sems, recv_sems = refs[2 * n:]
        x, y, c, _ = _place()
        sibling = (x, y, 1 - c)
        for t in range(n):
            h = srcs[t].shape[0]
            sub = h // (2 * D2D_SPLIT)
            for u in range(2 * D2D_SPLIT):
                pltpu.make_async_remote_copy(src_ref=srcs[t].at[pl.ds(u * sub, sub)], dst_ref=dsts[t].at[pl.ds(c * h + u * sub, sub)],
                                             send_sem=send_sems.at[t], recv_sem=recv_sems.at[t],
                                             device_id=sibling, device_id_type=MESH).start()
        for t in range(n):
            h = srcs[t].shape[0]
            pltpu.make_async_remote_copy(src_ref=srcs[t], dst_ref=dsts[t].at[pl.ds((1 - c) * h, h)], send_sem=send_sems.at[t],
                                         recv_sem=recv_sems.at[t], device_id=sibling, device_id_type=MESH).wait()

    return pl.pallas_call(
        body, name=f"share_halves_{tag}", in_specs=[ANY] * n, out_specs=[ANY] * n,
        out_shape=[jax.ShapeDtypeStruct((2 * h.shape[0], h.shape[1]), h.dtype) for h in halves],
        scratch_shapes=[pltpu.SemaphoreType.DMA((n,)), pltpu.SemaphoreType.DMA((n,))],
    )(*halves)


def _gather_weights(shards, chip):
    gathered = _all_gather_weights(shards)
    return [lax.dynamic_update_slice(g, s[None], (chip, 0, 0)) for g, s in zip(gathered, shards)]


def _reduce_scatter_begin(sends, core, tag):
    theirs = _exchange_halves(sends, tag)
    mines = [lax.dynamic_slice(g, (0, core * (g.shape[1] // 2), 0), (N_CHIP, g.shape[1] // 2, g.shape[2])) for g in sends]
    pair = [_sum_leading((m, t), f"pair_sum_{tag}{i}", out_dtype=BF16) for i, (m, t) in enumerate(zip(mines, theirs))]
    return _chip_exchange_comm(pair)


def _reduce_scatter_end(contrib, core, tag):
    halves = [_sum_leading(c, f"chip_sum_{tag}{i}") for i, c in enumerate(contrib)]
    shared = _share_halves(halves, tag)
    return [lax.dynamic_update_slice(full, mine, (core * mine.shape[0], 0)) for full, mine in zip(shared, halves)]


def _reduce_scatter(sends, core, tag=""):
    contrib = _run_comm(_reduce_scatter_begin(sends, core, tag), f"exchange_chips_{tag}")
    return _reduce_scatter_end(contrib, core, tag)


def _pack(arrs):
    rows = []
    for a in arrs:
        flat = a.astype(F32).reshape(-1)
        pad = (-flat.shape[0]) % 1024
        rows.append(jnp.pad(flat, (0, pad)).reshape(-1, 128))
    return jnp.concatenate(rows, axis=0)


def _unpack(buf, shapes):
    out, r = [], 0
    for shp in shapes:
        size = 1
        for d in shp:
            size *= d
        nr = (size + 1023) // 1024 * 8
        out.append(buf[r:r + nr].reshape(-1)[:size].reshape(shp))
        r += nr
    return out


R_BG, R_BS, R_OUT, R_FF = D // N_CHIP, DI // N_CHIP, D // N_CHIP, DFF // N_CHIP


def _pack_rest_shard(w_bg, w_bs, w_out, w_ff2, w_ff1):
    return jnp.concatenate([w_bg, w_bs, w_out, w_ff2, w_ff1], axis=0).astype(BF16)


def _unpack_rest(g):
    o1, o2, o3, o4 = R_BG, R_BG + R_BS, R_BG + R_BS + R_OUT, R_BG + R_BS + R_OUT + R_FF
    return (g[:, :o1].reshape(D, D), g[:, o1:o2].reshape(DI, D), g[:, o2:o3].reshape(D, D),
            jnp.transpose(g[:, o4:], (1, 0, 2)).reshape(D, DFF), g[:, o3:o4].reshape(DFF, D))


def _local_step(x, t, mod, w_in_r, rest, gm_norm_w, gm_ws, gm_bs, conv_w, conv_b, dt_bias, a_log, d_skip, ssm_norm_w,
                final_norm_w, place=None):
    sh1, sc1, g1, sh2, sc2, g2 = [mod[:, i * D:(i + 1) * D] for i in range(6)]
    ws_b = jnp.where(jnp.tril(jnp.ones((Q, Q), bool))[None], gm_ws, 0.0).astype(BF16)
    bs_t = gm_bs.T
    pad32 = lambda v: jnp.pad(v, ((0, 0), (0, 128 - NHEAD)))
    dtb, alog, dsk = pad32(dt_bias), pad32(a_log), pad32(d_skip)

    h1 = _prenorm(x, sc1, sh1)
    gdt = (F32,) if place is None else (BF16,)
    if place is None:
        (proj,) = _matmul(h1, w_in_r, "nn", name="mm_proj", tn=1152)
        w_bg, w_bs, w_out, w_ff1, w_ff2 = rest
    else:
        chip, core = place
        proj, g_rest = _matmul(h1, w_in_r, "nn", name="mm_proj", tn=1152, comm=_gather_relay_comm(rest))
        w_bg, w_bs, w_out, w_ff1, w_ff2 = _unpack_rest(lax.dynamic_update_slice(g_rest, rest[None], (chip, 0, 0)))
    y_a = _sgu_fwd(proj, gm_norm_w, ws_b, bs_t)
    xbc = _conv_fwd(proj, conv_w, conv_b)
    dt, cum, cum_t = _dt_fwd(proj, dtb, alog)
    y_ssd, states = _ssd_fwd(xbc, dt, cum, cum_t, dsk)
    y_b = _gatenorm_fwd(y_ssd, proj, ssm_norm_w)
    (ba,) = _matmul(y_a, w_bg, "nn", name="mm_branch_gm")
    (bb,) = _matmul(y_b, w_bs, "nn", name="mm_branch_ssm", tm=512, tk=2048)
    mixed = _mix_fwd(proj, ba, bb)
    (mo,) = _matmul(mixed, w_out, "nn", name="mm_out")
    x1, h2 = _resid_norm(x, mo, g1, sc2, sh2)
    f, act = _matmul(h2, w_ff1, "nn", name="mm_ff1", out_dtypes=(BF16, BF16),
                     epi=lambda acc: (acc, jnp.square(jnp.maximum(acc, 0.0))))
    (fo,) = _matmul(act, w_ff2, "nn", name="mm_ff2", tm=512, tk=4096)
    loss8, dx2, dfo, dg2, dfw = _loss_head(x1, fo, g2, final_norm_w, t)

    (df,) = _matmul(dfo, w_ff2, "nt", name="mm_ff2_dx", out_dtypes=(BF16,), epi_ins=(f,),
                    epi=lambda acc, fv: (acc * (2.0 * jnp.maximum(fv.astype(F32), 0.0)),))
    (g_ff2,) = _matmul(act, dfo, "tn", name="mm_ff2_dw", out_dtypes=gdt, tm=512, tk=4096)
    (dh2,) = _matmul(df, w_ff1, "nt", name="mm_ff1_dx", tm=512, tk=4096)
    (g_ff1,) = _matmul(h2, df, "tn", name="mm_ff1_dw", out_dtypes=gdt, tm=512, tk=4096)
    dx1, dmo, dg1, dsc2, dsh2 = _resid_norm_bwd(x1, mo, g1, sc2, sh2, dh2, dx2)
    (dmixed,) = _matmul(dmo, w_out, "nt", name="mm_out_dx")
    (g_out,) = _matmul(mixed, dmo, "tn", name="mm_out_dw", out_dtypes=gdt, tm=512, tk=4096)
    early = None
    if place is not None:
        stack = jnp.concatenate([g_out.reshape(N_CHIP, R_OUT, D), g_ff2.reshape(N_CHIP, R_FF, D),
                                 jnp.transpose(g_ff1.reshape(D, N_CHIP, D), (1, 0, 2))], axis=1).astype(BF16)
        early = _reduce_scatter_begin([stack], core, "early")
    dproj, dba, dbb = _mix_bwd(proj, ba, bb, dmixed)
    (dy_a,) = _matmul(dba, w_bg, "nt", name="mm_branch_gm_dx")
    (g_bg,) = _matmul(y_a, dba, "tn", name="mm_branch_gm_dw", out_dtypes=gdt, tm=512, tk=4096)
    (dy_b,) = _matmul(dbb, w_bs, "nt", name="mm_branch_ssm_dx")
    (g_bs,) = _matmul(y_b, dbb, "tn", name="mm_branch_ssm_dw", out_dtypes=gdt, tm=512, tk=4096)
    dproj, d_gm_norm, d_ws, d_bs_t = _sgu_bwd(proj, gm_norm_w, ws_b, bs_t, dy_a, dproj)
    dy_ssd, dproj, d_ssm_norm = _gatenorm_bwd(y_ssd, proj, ssm_norm_w, dy_b, dproj)
    dxs, dbm, dcm, ddt, dcum, dcum_t, d_dsk, *early_contrib = _ssd_bwd(xbc, dt, cum, cum_t, dsk, states, dy_ssd, comm=early)
    dproj, d_dtb, d_alog = _dt_bwd(proj, dtb, alog, ddt, dcum, dcum_t, dproj)
    dproj, dw_x, db_x = _conv_bwd(proj, conv_w, conv_b, dxs, dproj, 0, "conv_bwd_x")
    dproj, dw_b, db_b = _conv_bwd(proj, conv_w, conv_b, dbm, dproj, DI, "conv_bwd_b")
    dproj, dw_c, db_c = _conv_bwd(proj, conv_w, conv_b, dcm, dproj, DI + D, "conv_bwd_c")
    d_conv_w = jnp.concatenate([dw_x, dw_b, dw_c], axis=1)
    d_conv_b = jnp.concatenate([db_x, db_b, db_c], axis=1)
    (g_in_r,) = _matmul(h1, dproj, "tn", name="mm_proj_dw", out_dtypes=gdt, tm=512, tn=1152, tk=4096)
    late = None
    if place is not None:
        send_in = jnp.transpose(_restore_w_in(g_in_r).reshape(D, N_CHIP, W_IN // N_CHIP), (1, 0, 2)).astype(BF16)
        stack = jnp.concatenate([g_bg.reshape(N_CHIP, R_BG, D), g_bs.reshape(N_CHIP, R_BS, D)], axis=1).astype(BF16)
        late = _reduce_scatter_begin([send_in, stack], core, "late")
    dh1, *late_contrib = _matmul(dproj, w_in_r, "nt", name="mm_proj_dx", tk=3456, comm=late)
    grad_x, dsc1, dsh1 = _prenorm_bwd(x, sc1, sh1, dh1, dx1)

    dmod = jnp.concatenate([dsh1, dsc1, dg1, dsh2, dsc2, dg2], axis=1)
    small = dict(gm_ws=d_ws, gm_norm_w=d_gm_norm, gm_bs=d_bs_t.T, conv_w=d_conv_w, conv_b=d_conv_b,
                 dt_bias=d_dtb[:, :NHEAD], a_log=d_alog[:, :NHEAD], d_skip=d_dsk[:, :NHEAD],
                 ssm_norm_w=d_ssm_norm, final_norm_w=dfw, dmod=dmod, loss=loss8[:1, :1])
    if place is None:
        return grad_x, small, dict(w_in_r=g_in_r, w_bg=g_bg, w_bs=g_bs, w_out=g_out, w_ff1=g_ff1, w_ff2=g_ff2)
    (s_early,) = _reduce_scatter_end(early_contrib, core, "early")
    s_in, s_late = _reduce_scatter_end(late_contrib, core, "late")
    o1, o2 = R_OUT, R_OUT + R_FF
    big = dict(w_in=s_in, w_bg=s_late[:R_BG], w_bs=s_late[R_BG:], w_out=s_early[:o1], w_ff2=s_early[o1:o2], w_ff1=s_early[o2:])
    return grad_x, small, big


SMALL_KEYS = ("gm_ws", "gm_norm_w", "gm_bs", "conv_w", "conv_b", "dt_bias", "a_log", "d_skip", "ssm_norm_w",
              "final_norm_w", "dmod", "loss")
SMALL_SHAPES = ((GM_G, Q, Q), (1, D), (GM_G, Q), (4, CONV), (1, CONV), (1, NHEAD), (1, NHEAD), (1, NHEAD), (1, DI),
                (1, D), (1, 6 * D), (1, 1))


def _reorder_w_in(w_full):
    k = w_full.shape[0]
    return jnp.concatenate([w_full[:, :8192], w_full[:, 8224:], w_full[:, 8192:8224],
                            jnp.zeros((k, W_IN_R - W_IN), w_full.dtype)], axis=1)


def _restore_w_in(g_r):
    return jnp.concatenate([g_r[:, :8192], g_r[:, OFF_DT:OFF_DT + NHEAD], g_r[:, 8192:OFF_DT]], axis=1)


def kernel(x, c, w_mod, b_mod, w_in, gm_norm_w, gm_ws, gm_bs, conv_w, conv_b, dt_bias, a_log, d_skip, ssm_norm_w, w_branch_gm, w_branch_ssm, w_out, w_ff1, w_ff2, final_norm_w, loss_target, m_w_mod, m_b_mod, m_w_in, m_gm_norm_w, m_gm_ws, m_gm_bs, m_conv_w, m_conv_b, m_dt_bias, m_a_log, m_d_skip, m_ssm_norm_w, m_w_branch_gm, m_w_branch_ssm, m_w_out, m_w_ff1, m_w_ff2, m_final_norm_w, v_w_mod, v_b_mod, v_w_in, v_gm_norm_w, v_gm_ws, v_gm_bs, v_conv_w, v_conv_b, v_dt_bias, v_a_log, v_d_skip, v_ssm_norm_w, v_w_branch_gm, v_w_branch_ssm, v_w_out, v_w_ff1, v_w_ff2, v_final_norm_w):
    ax, ay, ac = lax.axis_index("x"), lax.axis_index("y"), lax.axis_index("c")
    chip = 2 * ax + ay
    dev = 2 * chip + ac
    seq = x.shape[1]
    nmod = w_mod.shape[2]

    first = jnp.concatenate([c, conv_w[0], jnp.zeros((3, D), F32)], axis=0)
    first_all = _all_gather_small(first, "all_gather_cond")
    c8 = first_all[:, 0, :]
    conv_w_full = jnp.concatenate([first_all[2 * k, 1:5, :] for k in range(N_CHIP)], axis=1)
    b_sl = lax.dynamic_slice(b_mod, (0, chip * nmod), (1, nmod))
    mod_part = _mod_fwd(c8, w_mod[0], b_sl)
    mod_all = _all_gather_small(mod_part, "all_gather_mod")
    mod = jnp.concatenate([lax.dynamic_slice(mod_all, (2 * k, dev, 0), (1, 1, nmod))[0] for k in range(N_CHIP)], axis=1)

    (g_in,) = _gather_weights([w_in[0].astype(BF16)], chip)
    w_in_r = _reorder_w_in(jnp.transpose(g_in, (1, 0, 2)).reshape(D, W_IN))
    rest = _pack_rest_shard(w_branch_gm[0], w_branch_ssm[0], w_out[0], w_ff2[0], w_ff1[0])

    grad_x, small, big = _local_step(
        x[0], loss_target[0], mod, w_in_r, rest, gm_norm_w, gm_ws[0], gm_bs[0], conv_w_full, conv_b, dt_bias, a_log,
        d_skip, ssm_norm_w, final_norm_w.reshape(1, D), place=(chip, ac))

    small_all = _all_gather_small(_pack([small[k] for k in SMALL_KEYS]), "all_gather_small_grads")
    small_sum = _sum_leading(small_all, "sum_small_grads")
    s_ws, s_gnw, s_bs, s_cw, s_cb, s_dtb, s_alog, s_dsk, s_snw, s_fnw, s_bmod, s_loss = _unpack(small_sum, SMALL_SHAPES)
    dmod_all = jnp.stack([_unpack(small_all[k], SMALL_SHAPES)[10][0] for k in range(N_DEV)], axis=0)
    g_w_mod = _mod_wgrad(c8, lax.dynamic_slice(dmod_all, (0, chip * nmod), (N_DEV, nmod)))
    g_conv_w = lax.dynamic_slice(s_cw, (0, chip * (CONV // N_CHIP)), (4, CONV // N_CHIP))

    g_w_in, g_w_bg, g_w_bs, g_w_out, g_w_ff2, g_w_ff1 = (big[k] for k in ("w_in", "w_bg", "w_bs", "w_out", "w_ff2", "w_ff1"))

    def adam_big(w, g, m, v, name):
        d, m2, v2 = _adam(w.reshape(g.shape), g, m.reshape(g.shape), v.reshape(g.shape), name)
        return g.reshape(w.shape), d.reshape(w.shape), m2.reshape(w.shape), v2.reshape(w.shape)

    res = {}
    res["w_mod"] = adam_big(w_mod, g_w_mod, m_w_mod, v_w_mod, "adam_w_mod")
    res["w_in"] = adam_big(w_in, g_w_in, m_w_in, v_w_in, "adam_w_in")
    res["w_branch_gm"] = adam_big(w_branch_gm, g_w_bg, m_w_branch_gm, v_w_branch_gm, "adam_w_branch_gm")
    res["w_branch_ssm"] = adam_big(w_branch_ssm, g_w_bs, m_w_branch_ssm, v_w_branch_ssm, "adam_w_branch_ssm")
    res["w_out"] = adam_big(w_out, g_w_out, m_w_out, v_w_out, "adam_w_out")
    res["w_ff1"] = adam_big(w_ff1, g_w_ff1, m_w_ff1, v_w_ff1, "adam_w_ff1")
    res["w_ff2"] = adam_big(w_ff2, g_w_ff2, m_w_ff2, v_w_ff2, "adam_w_ff2")

    names = ("b_mod", "gm_norm_w", "gm_ws", "gm_bs", "conv_w", "conv_b", "dt_bias", "a_log", "d_skip", "ssm_norm_w", "final_norm_w")
    ws = (b_mod, gm_norm_w, gm_ws, gm_bs, conv_w, conv_b, dt_bias, a_log, d_skip, ssm_norm_w, final_norm_w)
    ms = (m_b_mod, m_gm_norm_w, m_gm_ws, m_gm_bs, m_conv_w, m_conv_b, m_dt_bias, m_a_log, m_d_skip, m_ssm_norm_w, m_final_norm_w)
    vs = (v_b_mod, v_gm_norm_w, v_gm_ws, v_gm_bs, v_conv_w, v_conv_b, v_dt_bias, v_a_log, v_d_skip, v_ssm_norm_w, v_final_norm_w)
    gs = (s_bmod, s_gnw, s_ws, s_bs, g_conv_w, s_cb, s_dtb, s_alog, s_dsk, s_snw, s_fnw)
    gs = [g.reshape(w.shape) for g, w in zip(gs, ws)]
    shapes = [w.shape for w in ws]
    d_p, m_p, v_p = _adam(_pack(ws), _pack(gs), _pack(ms), _pack(vs), "adam_small")
    for name, g, d, m2, v2 in zip(names, gs, _unpack(d_p, shapes), _unpack(m_p, shapes), _unpack(v_p, shapes)):
        res[name] = (g, d, m2, v2)

    order = ("w_mod", "b_mod", "w_in", "gm_norm_w", "gm_ws", "gm_bs", "conv_w", "conv_b", "dt_bias", "a_log", "d_skip",
             "ssm_norm_w", "w_branch_gm", "w_branch_ssm", "w_out", "w_ff1", "w_ff2", "final_norm_w")
    loss = s_loss.reshape(())
    return (loss, grad_x.reshape(x.shape), *[res[k][0] for k in order], *[res[k][1] for k in order],
            *[res[k][2] for k in order], *[res[k][3] for k in order])
```

```python
import functools

import jax
import jax.numpy as jnp
from jax import lax
from jax.experimental import pallas as pl
from jax.experimental.pallas import tpu as pltpu

F32 = jnp.float32
BF16 = jnp.bfloat16
MESH = pl.DeviceIdType.MESH
HIGHEST = lax.Precision.HIGHEST

D = 1024
EPS = 1e-6
Q = 128
GM_G = 8
NHEAD = 32
NGRP = 8
DI = 2048
CONV = 4096
DFF = 4096
W_IN = 10272
W_IN_R = 10368
OFF_Z, OFF_XBC, OFF_GA, OFF_DT = 2048, 4096, 8192, 10240
N_CHIP = 4
N_DEV = 8
AG_CHUNKS = 4
D2D_SPLIT = 4

ADAM_LR, ADAM_B1, ADAM_B2, ADAM_EPS, ADAM_WD, ADAM_STEP = 0.001, 0.9, 0.999, 1e-08, 0.01, 10

ANY = pl.BlockSpec(memory_space=pl.ANY)
VMEM = pl.BlockSpec(memory_space=pltpu.VMEM)


def _full(shape):
    return pl.BlockSpec(shape, lambda *_: (0,) * len(shape))


def _pick(n, prefs):
    for p in prefs:
        if n % p == 0:
            return p
    return n


def _rms(x):
    return x * lax.rsqrt(jnp.mean(x * x, axis=-1, keepdims=True) + EPS)


def _modnorm(x, sc, sh):
    return _rms(x) * (1.0 + sc) + sh


def _sgu_pre(u, v, w):
    return jax.nn.gelu(u), _rms(jax.nn.gelu(v)) * w


def _gatenorm(y, z, w):
    g = y * jax.nn.silu(z)
    return _rms(g) * w


def _mix(ga, gb, ba, bb):
    return jax.nn.sigmoid(ga) * ba + jax.nn.sigmoid(gb) * bb


def _loss_tile(x1, fo, g2, fw, t):
    x2 = x1 + g2 * fo
    y = _rms(x2) * fw
    err = jnp.square(y - t)
    return 0.5 * jnp.sum(jnp.mean(err, axis=-1))


def _tril(n):
    r = lax.broadcasted_iota(jnp.int32, (n, n), 0)
    c = lax.broadcasted_iota(jnp.int32, (n, n), 1)
    return r >= c


def _dt_prep(dtr, dtb, alog):
    dt = jax.nn.softplus(dtr + dtb)
    a = dt * (-jnp.exp(alog))
    ones = _tril(Q).astype(F32)
    cum = jnp.dot(ones, a, precision=HIGHEST, preferred_element_type=F32)
    cum_t = lax.dot_general(a, ones, (((0,), (1,)), ((), ())), precision=HIGHEST, preferred_element_type=F32)
    return dt, cum, cum_t


def _ssd_group(x0, x1, s0, s1, bm, cm, dt, cum, cum_t, dsk, grp):
    lane = lax.broadcasted_iota(jnp.int32, (1, 128), 1)
    sub = lax.broadcasted_iota(jnp.int32, (128, 1), 0)
    future = jnp.where(_tril(Q), 0.0, -jnp.inf)
    half = lane < 64
    half_rows = sub < 64
    bmb = bm.astype(BF16)
    cmb = cm.astype(BF16)
    cb = lax.dot_general(cmb, bmb, (((1,), (1,)), ((), ())), preferred_element_type=F32)

    def col(v, h):
        return jnp.sum(jnp.where(lane == h, v, 0.0), axis=1, keepdims=True)

    def row(v, h):
        return jnp.sum(jnp.where(sub == h, v, 0.0), axis=0, keepdims=True)

    def last(c):
        return jnp.sum(jnp.where(sub == Q - 1, c, 0.0), axis=0, keepdims=True)

    outs, states = [], []
    for p, (xp, sp) in enumerate(((x0, s0), (x1, s1))):
        h_a = 4 * grp + 2 * p
        h_b = h_a + 1
        dt_a, dt_b = col(dt, h_a), col(dt, h_b)
        cum_a, cum_b = col(cum, h_a), col(cum, h_b)
        row_a, row_b = row(cum_t, h_a), row(cum_t, h_b)
        last_a, last_b = last(cum_a), last(cum_b)
        xdt = xp * jnp.where(half, dt_a, dt_b)
        xdtb = xdt.astype(BF16)
        m_a = (cb * jnp.exp(cum_a - row_a + future)).astype(BF16)
        m_b = (cb * jnp.exp(cum_b - row_b + future)).astype(BF16)
        y_intra = jnp.where(half, jnp.dot(m_a, xdtb, preferred_element_type=F32),
                            jnp.dot(m_b, xdtb, preferred_element_type=F32))
        y_inter = lax.dot_general(cmb, sp.astype(BF16), (((1,), (1,)), ((), ())), preferred_element_type=F32)
        y_inter = y_inter * jnp.where(half, jnp.exp(cum_a), jnp.exp(cum_b))
        w_end = jnp.where(half, jnp.exp(last_a - cum_a), jnp.exp(last_b - cum_b))
        upd = lax.dot_general((xdt * w_end).astype(BF16), bmb, (((0,), (0,)), ((), ())), preferred_element_type=F32)
        states.append(sp * jnp.where(half_rows, jnp.exp(last_a), jnp.exp(last_b)) + upd)
        outs.append(y_intra + y_inter + xp * jnp.where(half, col(dsk, h_a), col(dsk, h_b)))
    return outs[0], outs[1], states[0], states[1]


class _Comm:
    def __init__(self, ins, outs, sems, start, finish):
        self.ins, self.outs, self.sems, self.start, self.finish = list(ins), list(outs), list(sems), start, finish
        self.middle = None


def _matmul(a, b, mode, *, name, out_dtypes=(F32,), epi=None, epi_ins=(), tm=1024, tn=1024, tk=1024, comm=None):
    if mode == "nn":
        (m, k), n = a.shape, b.shape[1]
    elif mode == "nt":
        (m, k), n = a.shape, b.shape[0]
    else:
        (k, m), n = a.shape, b.shape[1]
    tm, tn, tk = _pick(m, (tm, 512, 256, 128)), _pick(n, (tn, 1152, 1024, 512, 384, 256, 128)), _pick(k, (tk, 1152, 1024, 512, 256, 128))
    nk = k // tk
    if mode == "nn":
        a_spec = pl.BlockSpec((tm, tk), lambda i, j, kk: (i, kk))
        b_spec = pl.BlockSpec((tk, tn), lambda i, j, kk: (kk, j))
        dims = (((1,), (0,)), ((), ()))
    elif mode == "nt":
        a_spec = pl.BlockSpec((tm, tk), lambda i, j, kk: (i, kk))
        b_spec = pl.BlockSpec((tn, tk), lambda i, j, kk: (j, kk))
        dims = (((1,), (1,)), ((), ()))
    else:
        a_spec = pl.BlockSpec((tk, tm), lambda i, j, kk: (kk, i))
        b_spec = pl.BlockSpec((tk, tn), lambda i, j, kk: (kk, j))
        dims = (((0,), (0,)), ((), ()))
    o_spec = pl.BlockSpec((tm, tn), lambda i, j, kk: (i, j))
    n_epi, n_out = len(epi_ins), len(out_dtypes)
    n_ci, n_co, n_cs = (len(comm.ins), len(comm.outs), len(comm.sems)) if comm is not None else (0, 0, 0)
    grid = (m // tm, n // tn, nk)

    def body(*refs):
        a_ref, b_ref = refs[0], refs[1]
        e_refs = refs[2:2 + n_epi]
        ci_refs = refs[2 + n_epi:2 + n_epi + n_ci]
        o_refs = refs[2 + n_epi + n_ci:2 + n_epi + n_ci + n_out]
        co_refs = refs[2 + n_epi + n_ci + n_out:2 + n_epi + n_ci + n_out + n_co]
        acc_ref = refs[2 + n_epi + n_ci + n_out + n_co]
        cs_refs = refs[3 + n_epi + n_ci + n_out + n_co:]
        if comm is not None:
            ids = [pl.program_id(d) for d in range(3)]
            pl.when((ids[0] == 0) & (ids[1] == 0) & (ids[2] == 0))(lambda: comm.start(ci_refs, co_refs, cs_refs))
            if comm.middle is not None:
                pl.when((ids[0] == (5 * grid[0]) // 8) & (ids[1] == 0) & (ids[2] == 0))(
                    lambda: comm.middle(ci_refs, co_refs, cs_refs))

        def finish(acc):
            outs = epi(acc, *[e[...] for e in e_refs]) if epi is not None else (acc,)
            for o_ref, val in zip(o_refs, outs):
                o_ref[...] = val.astype(o_ref.dtype)

        part = lax.dot_general(a_ref[...], b_ref[...], dims, preferred_element_type=F32)
        if nk == 1:
            finish(part)
        else:
            kk = pl.program_id(2)

            @pl.when(kk == 0)
            def _():
                acc_ref[...] = part

            @pl.when(kk > 0)
            def _():
                acc_ref[...] += part

            @pl.when(kk == nk - 1)
            def _():
                finish(acc_ref[...])

        if comm is not None:
            pl.when((ids[0] == grid[0] - 1) & (ids[1] == grid[1] - 1) & (ids[2] == grid[2] - 1))(
                lambda: comm.finish(ci_refs, co_refs, cs_refs))

    extra_in = comm.ins if comm is not None else []
    extra_out = comm.outs if comm is not None else []
    extra_sems = comm.sems if comm is not None else []
    return pl.pallas_call(
        body, name=name, grid=grid,
        in_specs=[a_spec, b_spec] + [o_spec] * n_epi + [ANY] * n_ci,
        out_specs=[o_spec] * n_out + [ANY] * n_co,
        out_shape=[jax.ShapeDtypeStruct((m, n), dt) for dt in out_dtypes] + extra_out,
        scratch_shapes=[pltpu.VMEM((tm, tn) if nk > 1 else (8, 128), F32)] + extra_sems,
        compiler_params=pltpu.CompilerParams(
            dimension_semantics=("arbitrary",) * 3 if comm is not None else ("parallel", "parallel", "arbitrary")),
    )(a, b, *epi_ins, *extra_in)


def _row_tile(s):
    return _pick(s, (512, 256, 128))


def _prenorm(x, sc, sh):
    s = x.shape[0]
    tm = _row_tile(s)

    def body(x_ref, sc_ref, sh_ref, h_ref):
        h_ref[...] = _modnorm(x_ref[...], sc_ref[...], sh_ref[...]).astype(BF16)

    row = pl.BlockSpec((tm, D), lambda i: (i, 0))
    return pl.pallas_call(body, name="prenorm", grid=(s // tm,), in_specs=[row, _full((1, D)), _full((1, D))],
                          out_specs=row, out_shape=jax.ShapeDtypeStruct((s, D), BF16))(x, sc, sh)


def _prenorm_bwd(x, sc, sh, dh, dx_res):
    s = x.shape[0]
    tm = _row_tile(s)

    def body(x_ref, sc_ref, sh_ref, dh_ref, dr_ref, dx_ref, dsc_ref, dsh_ref):
        _, vjp = jax.vjp(_modnorm, x_ref[...], sc_ref[...], sh_ref[...])
        dx, dsc, dsh = vjp(dh_ref[...])
        dx_ref[...] = dr_ref[...] + dx

        @pl.when(pl.program_id(0) == 0)
        def _():
            dsc_ref[...] = jnp.zeros_like(dsc_ref)
            dsh_ref[...] = jnp.zeros_like(dsh_ref)

        dsc_ref[...] += dsc
        dsh_ref[...] += dsh

    row = pl.BlockSpec((tm, D), lambda i: (i, 0))
    vec = _full((1, D))
    return pl.pallas_call(
        body, name="prenorm_bwd", grid=(s // tm,), in_specs=[row, vec, vec, row, row], out_specs=[row, vec, vec],
        out_shape=[jax.ShapeDtypeStruct((s, D), F32), jax.ShapeDtypeStruct((1, D), F32), jax.ShapeDtypeStruct((1, D), F32)],
    )(x, sc, sh, dh, dx_res)


def _resid_norm(x, mo, g1, sc, sh):
    s = x.shape[0]
    tm = _row_tile(s)

    def body(x_ref, mo_ref, g_ref, sc_ref, sh_ref, x1_ref, h_ref):
        x1 = x_ref[...] + g_ref[...] * mo_ref[...]
        x1_ref[...] = x1
        h_ref[...] = _modnorm(x1, sc_ref[...], sh_ref[...]).astype(BF16)

    row = pl.BlockSpec((tm, D), lambda i: (i, 0))
    vec = _full((1, D))
    return pl.pallas_call(
        body, name="resid_norm", grid=(s // tm,), in_specs=[row, row, vec, vec, vec], out_specs=[row, row],
        out_shape=[jax.ShapeDtypeStruct((s, D), F32), jax.ShapeDtypeStruct((s, D), BF16)],
    )(x, mo, g1, sc, sh)


def _resid_norm_bwd(x1, mo, g1, sc, sh, dh, dx2):
    s = x1.shape[0]
    tm = _row_tile(s)

    def body(x1_ref, mo_ref, g_ref, sc_ref, sh_ref, dh_ref, dx2_ref, dx1_ref, dmo_ref, dg_ref, dsc_ref, dsh_ref):
        _, vjp = jax.vjp(_modnorm, x1_ref[...], sc_ref[...], sh_ref[...])
        dx, dsc, dsh = vjp(dh_ref[...])
        dx1 = dx2_ref[...] + dx
        dx1_ref[...] = dx1
        dmo_ref[...] = (dx1 * g_ref[...]).astype(BF16)

        @pl.when(pl.program_id(0) == 0)
        def _():
            dg_ref[...] = jnp.zeros_like(dg_ref)
            dsc_ref[...] = jnp.zeros_like(dsc_ref)
            dsh_ref[...] = jnp.zeros_like(dsh_ref)

        dg_ref[...] += jnp.sum(dx1 * mo_ref[...], axis=0, keepdims=True)
        dsc_ref[...] += dsc
        dsh_ref[...] += dsh

    row = pl.BlockSpec((tm, D), lambda i: (i, 0))
    vec = _full((1, D))
    vshape = jax.ShapeDtypeStruct((1, D), F32)
    return pl.pallas_call(
        body, name="resid_norm_bwd", grid=(s // tm,), in_specs=[row, row, vec, vec, vec, row, row],
        out_specs=[row, row, vec, vec, vec],
        out_shape=[jax.ShapeDtypeStruct((s, D), F32), jax.ShapeDtypeStruct((s, D), BF16), vshape, vshape, vshape],
    )(x1, mo, g1, sc, sh, dh, dx2)


def _loss_head(x1, fo, g2, fw, t):
    s = x1.shape[0]
    tm = _row_tile(s)

    def body(x1_ref, fo_ref, g_ref, fw_ref, t_ref, loss_ref, dx_ref, dfo_ref, dg_ref, dfw_ref):
        loss, (dx1, dfo, dg, dfw) = jax.value_and_grad(_loss_tile, argnums=(0, 1, 2, 3))(
            x1_ref[...], fo_ref[...], g_ref[...], fw_ref[...], t_ref[...])
        dx_ref[...] = dx1
        dfo_ref[...] = dfo.astype(BF16)

        @pl.when(pl.program_id(0) == 0)
        def _():
            loss_ref[...] = jnp.zeros_like(loss_ref)
            dg_ref[...] = jnp.zeros_like(dg_ref)
            dfw_ref[...] = jnp.zeros_like(dfw_ref)

        loss_ref[...] += jnp.full(loss_ref.shape, loss, F32)
        dg_ref[...] += dg
        dfw_ref[...] += dfw

    row = pl.BlockSpec((tm, D), lambda i: (i, 0))
    vec = _full((1, D))
    vshape = jax.ShapeDtypeStruct((1, D), F32)
    return pl.pallas_call(
        body, name="loss_head", grid=(s // tm,), in_specs=[row, row, vec, vec, row],
        out_specs=[_full((8, 128)), row, row, vec, vec],
        out_shape=[jax.ShapeDtypeStruct((8, 128), F32), jax.ShapeDtypeStruct((s, D), F32),
                   jax.ShapeDtypeStruct((s, D), BF16), vshape, vshape],
    )(x1, fo, g2, fw, t)


def _sgu_fwd(proj, norm_w, ws_b, bs_t):
    s = proj.shape[0]
    tm = _pick(s, (256, 128))

    def body(u_ref, v_ref, w_ref, ws_ref, bs_ref, y_ref):
        ug, vn = _sgu_pre(u_ref[...], v_ref[...], w_ref[...])
        vnb = vn.astype(BF16)
        for c in range(tm // Q):
            r = slice(c * Q, (c + 1) * Q)
            for g in range(GM_G):
                cs = slice(g * 128, (g + 1) * 128)
                sv = jnp.dot(ws_ref[g], vnb[r, cs], preferred_element_type=F32) + bs_ref[:, g:g + 1]
                y_ref[r, cs] = (ug[r, cs] * sv).astype(BF16)

    return pl.pallas_call(
        body, name="sgu_fwd", grid=(s // tm,),
        in_specs=[pl.BlockSpec((tm, D), lambda i: (i, 0)), pl.BlockSpec((tm, D), lambda i: (i, 1)),
                  _full((1, D)), _full((GM_G, Q, Q)), _full((Q, GM_G))],
        out_specs=pl.BlockSpec((tm, D), lambda i: (i, 0)),
        out_shape=jax.ShapeDtypeStruct((s, D), BF16),
    )(proj, proj, norm_w, ws_b, bs_t)


def _sgu_bwd(proj, norm_w, ws_b, bs_t, dy, dproj):
    s = proj.shape[0]
    tm = _pick(s, (256, 128))

    def body(u_ref, v_ref, w_ref, ws_ref, bs_ref, dy_ref, _, duv_ref, dw_ref, dws_ref, dbs_ref, dug_scr, dvn_scr):
        @pl.when(pl.program_id(0) == 0)
        def _():
            dw_ref[...] = jnp.zeros_like(dw_ref)
            dws_ref[...] = jnp.zeros_like(dws_ref)
            dbs_ref[...] = jnp.zeros_like(dbs_ref)

        (ug, vn), vjp = jax.vjp(_sgu_pre, u_ref[...], v_ref[...], w_ref[...])
        vnb = vn.astype(BF16)
        dy = dy_ref[...].astype(F32)
        causal = _tril(Q).astype(F32)
        for c in range(tm // Q):
            r = slice(c * Q, (c + 1) * Q)
            for g in range(GM_G):
                cs = slice(g * 128, (g + 1) * 128)
                blk = vnb[r, cs]
                sv = jnp.dot(ws_ref[g], blk, preferred_element_type=F32) + bs_ref[:, g:g + 1]
                dug_scr[r, cs] = dy[r, cs] * sv
                dsv = dy[r, cs] * ug[r, cs]
                dsvb = dsv.astype(BF16)
                dws_ref[g] += causal * lax.dot_general(dsvb, blk, (((1,), (1,)), ((), ())), preferred_element_type=F32)
                dbs_ref[:, g:g + 1] += jnp.sum(dsv, axis=1, keepdims=True)
                dvn_scr[r, cs] = lax.dot_general(ws_ref[g], dsvb, (((0,), (0,)), ((), ())), preferred_element_type=F32)
        du, dv, dw = vjp((dug_scr[...], dvn_scr[...]))
        duv_ref[:, :D] = du.astype(BF16)
        duv_ref[:, D:] = dv.astype(BF16)
        dw_ref[...] += dw

    return pl.pallas_call(
        body, name="sgu_bwd", grid=(s // tm,),
        in_specs=[pl.BlockSpec((tm, D), lambda i: (i, 0)), pl.BlockSpec((tm, D), lambda i: (i, 1)),
                  _full((1, D)), _full((GM_G, Q, Q)), _full((Q, GM_G)), pl.BlockSpec((tm, D), lambda i: (i, 0)), ANY],
        out_specs=[pl.BlockSpec((tm, 2 * D), lambda i: (i, 0)), _full((1, D)), _full((GM_G, Q, Q)), _full((Q, GM_G))],
        out_shape=[jax.ShapeDtypeStruct(dproj.shape, BF16), jax.ShapeDtypeStruct((1, D), F32),
                   jax.ShapeDtypeStruct((GM_G, Q, Q), F32), jax.ShapeDtypeStruct((Q, GM_G), F32)],
        scratch_shapes=[pltpu.VMEM((tm, D), F32), pltpu.VMEM((tm, D), F32)],
        input_output_aliases={6: 0},
    )(proj, proj, norm_w, ws_b, bs_t, dy, dproj)


def _gatenorm_fwd(y_ssd, proj, norm_w):
    s = y_ssd.shape[0]
    tm = _pick(s, (256, 128))
    gw = DI // NGRP

    def body(y_ref, z_ref, w_ref, o_ref):
        for g in range(NGRP):
            cs = slice(g * gw, (g + 1) * gw)
            o_ref[:, cs] = _gatenorm(y_ref[:, cs], z_ref[:, cs], w_ref[:, cs]).astype(BF16)

    return pl.pallas_call(
        body, name="gatenorm_fwd", grid=(s // tm,),
        in_specs=[pl.BlockSpec((tm, DI), lambda i: (i, 0)), pl.BlockSpec((tm, DI), lambda i: (i, OFF_Z // DI)), _full((1, DI))],
        out_specs=pl.BlockSpec((tm, DI), lambda i: (i, 0)),
        out_shape=jax.ShapeDtypeStruct((s, DI), BF16),
    )(y_ssd, proj, norm_w)


def _gatenorm_bwd(y_ssd, proj, norm_w, dyb, dproj):
    s = y_ssd.shape[0]
    tm = _pick(s, (256, 128))
    gw = DI // NGRP

    def body(y_ref, z_ref, w_ref, d_ref, _, dy_ref, dz_ref, dw_ref):
        @pl.when(pl.program_id(0) == 0)
        def _():
            dw_ref[...] = jnp.zeros_like(dw_ref)

        for g in range(NGRP):
            cs = slice(g * gw, (g + 1) * gw)
            _, vjp = jax.vjp(_gatenorm, y_ref[:, cs], z_ref[:, cs], w_ref[:, cs])
            dy, dz, dw = vjp(d_ref[:, cs].astype(F32))
            dy_ref[:, cs] = dy.astype(BF16)
            dz_ref[:, cs] = dz.astype(BF16)
            dw_ref[:, cs] += dw

    blk = pl.BlockSpec((tm, DI), lambda i: (i, 0))
    zblk = pl.BlockSpec((tm, DI), lambda i: (i, OFF_Z // DI))
    return pl.pallas_call(
        body, name="gatenorm_bwd", grid=(s // tm,),
        in_specs=[blk, zblk, _full((1, DI)), blk, ANY],
        out_specs=[blk, zblk, _full((1, DI))],
        out_shape=[jax.ShapeDtypeStruct((s, DI), BF16), jax.ShapeDtypeStruct(dproj.shape, BF16), jax.ShapeDtypeStruct((1, DI), F32)],
        input_output_aliases={4: 1},
    )(y_ssd, proj, norm_w, dyb, dproj)


def _mix_fwd(proj, ba, bb):
    s = proj.shape[0]
    tm = _row_tile(s)
    gb0 = OFF_GA // D

    def body(ga_ref, gb_ref, ba_ref, bb_ref, o_ref):
        o_ref[...] = _mix(ga_ref[...], gb_ref[...], ba_ref[...].astype(F32), bb_ref[...].astype(F32)).astype(BF16)

    row = pl.BlockSpec((tm, D), lambda i: (i, 0))
    return pl.pallas_call(
        body, name="mix_fwd", grid=(s // tm,),
        in_specs=[pl.BlockSpec((tm, D), lambda i: (i, gb0)), pl.BlockSpec((tm, D), lambda i: (i, gb0 + 1)), row, row],
        out_specs=row, out_shape=jax.ShapeDtypeStruct((s, D), BF16),
    )(proj, proj, ba, bb)


def _mix_bwd(proj, ba, bb, dmixed):
    s = proj.shape[0]
    tm = _row_tile(s)
    gb0 = OFF_GA // D

    def body(ga_ref, gb_ref, ba_ref, bb_ref, d_ref, dg_ref, dba_ref, dbb_ref):
        _, vjp = jax.vjp(_mix, ga_ref[...], gb_ref[...], ba_ref[...].astype(F32), bb_ref[...].astype(F32))
        dga, dgb, dba, dbb = vjp(d_ref[...].astype(F32))
        dg_ref[:, :D] = dga.astype(BF16)
        dg_ref[:, D:] = dgb.astype(BF16)
        dba_ref[...] = dba.astype(BF16)
        dbb_ref[...] = dbb.astype(BF16)

    row = pl.BlockSpec((tm, D), lambda i: (i, 0))
    return pl.pallas_call(
        body, name="mix_bwd", grid=(s // tm,),
        in_specs=[pl.BlockSpec((tm, D), lambda i: (i, gb0)), pl.BlockSpec((tm, D), lambda i: (i, gb0 + 1)), row, row, row],
        out_specs=[pl.BlockSpec((tm, 2 * D), lambda i: (i, OFF_GA // (2 * D))), row, row],
        out_shape=[jax.ShapeDtypeStruct((s, W_IN_R), BF16), jax.ShapeDtypeStruct((s, D), BF16), jax.ShapeDtypeStruct((s, D), BF16)],
    )(proj, proj, ba, bb, dmixed)


CONV_TC = 1024


def _conv_fwd(proj, conv_w, conv_b):
    s = proj.shape[0]
    tm = _row_tile(s)
    cb0 = OFF_XBC // CONV_TC

    def body(x_ref, halo_ref, w_ref, b_ref, o_ref):
        halo = jnp.where(pl.program_id(0) > 0, halo_ref[...], 0.0)
        ext = jnp.concatenate([halo, x_ref[...]], axis=0)
        acc = jnp.broadcast_to(b_ref[...], (tm, CONV_TC))
        for k in range(4):
            shifted = ext if k == 3 else pltpu.roll(ext, 3 - k, 0)
            acc = acc + w_ref[k:k + 1, :] * shifted[8:, :]
        o_ref[...] = jax.nn.silu(acc)

    return pl.pallas_call(
        body, name="conv_fwd", grid=(s // tm, CONV // CONV_TC),
        in_specs=[pl.BlockSpec((tm, CONV_TC), lambda i, j: (i, cb0 + j)),
                  pl.BlockSpec((8, CONV_TC), lambda i, j: (jnp.maximum(i * (tm // 8) - 1, 0), cb0 + j)),
                  pl.BlockSpec((4, CONV_TC), lambda i, j: (0, j)), pl.BlockSpec((1, CONV_TC), lambda i, j: (0, j))],
        out_specs=pl.BlockSpec((tm, CONV_TC), lambda i, j: (i, j)),
        out_shape=jax.ShapeDtypeStruct((s, CONV), F32),
    )(proj, proj, conv_w, conv_b)


def _conv_bwd(proj, conv_w, conv_b, dact, dproj, col0, name):
    s, width = dact.shape
    tm = _row_tile(s)
    nt = s // tm
    c0 = col0 // CONV_TC
    cb0 = OFF_XBC // CONV_TC + c0

    def body(x_ref, prev_ref, next_ref, d_ref, dnext_ref, w_ref, b_ref, _, dx_ref, dw_ref, db_ref):
        i = pl.program_id(1)
        prev = jnp.where(i > 0, prev_ref[...], 0.0)
        ext = jnp.concatenate([prev, x_ref[...], next_ref[...]], axis=0)
        dext = jnp.concatenate([d_ref[...], jnp.where(i < nt - 1, dnext_ref[...], 0.0)], axis=0)
        pre = jnp.broadcast_to(b_ref[...], (tm + 8, CONV_TC))
        taps = []
        for k in range(4):
            shifted = (ext if k == 3 else pltpu.roll(ext, 3 - k, 0))[8:, :]
            taps.append(shifted)
            pre = pre + w_ref[k:k + 1, :] * shifted
        sig = jax.nn.sigmoid(pre)
        dpre = dext * (sig * (1.0 + pre * (1.0 - sig)))
        dx = jnp.zeros((tm, CONV_TC), F32)
        for k in range(4):
            shifted = dpre if k == 3 else pltpu.roll(dpre, tm + 8 - (3 - k), 0)
            dx = dx + w_ref[k:k + 1, :] * shifted[:tm, :]
        dx_ref[...] = dx.astype(BF16)

        @pl.when(i == 0)
        def _():
            dw_ref[...] = jnp.zeros_like(dw_ref)
            db_ref[...] = jnp.zeros_like(db_ref)

        dtile = dpre[:tm, :]
        for k in range(4):
            dw_ref[k:k + 1, :] += jnp.sum(dtile * taps[k][:tm, :], axis=0, keepdims=True)
        db_ref[...] += jnp.sum(dtile, axis=0, keepdims=True)

    r8 = tm // 8
    return pl.pallas_call(
        body, name=name, grid=(width // CONV_TC, nt),
        in_specs=[pl.BlockSpec((tm, CONV_TC), lambda j, i: (i, cb0 + j)),
                  pl.BlockSpec((8, CONV_TC), lambda j, i: (jnp.maximum(i * r8 - 1, 0), cb0 + j)),
                  pl.BlockSpec((8, CONV_TC), lambda j, i: (jnp.minimum((i + 1) * r8, nt * r8 - 1), cb0 + j)),
                  pl.BlockSpec((tm, CONV_TC), lambda j, i: (i, j)),
                  pl.BlockSpec((8, CONV_TC), lambda j, i: (jnp.minimum((i + 1) * r8, nt * r8 - 1), j)),
                  pl.BlockSpec((4, CONV_TC), lambda j, i: (0, c0 + j)), pl.BlockSpec((1, CONV_TC), lambda j, i: (0, c0 + j)), ANY],
        out_specs=[pl.BlockSpec((tm, CONV_TC), lambda j, i: (i, cb0 + j)),
                   pl.BlockSpec((4, CONV_TC), lambda j, i: (0, j)), pl.BlockSpec((1, CONV_TC), lambda j, i: (0, j))],
        out_shape=[jax.ShapeDtypeStruct(dproj.shape, BF16), jax.ShapeDtypeStruct((4, width), F32), jax.ShapeDtypeStruct((1, width), F32)],
        input_output_aliases={7: 0},
    )(proj, proj, proj, dact, dact, conv_w, conv_b, dproj)


def _dt_fwd(proj, dtb, alog):
    s = proj.shape[0]
    nc = s // Q

    def body(r_ref, b_ref, a_ref, dt_ref, cum_ref, cumt_ref):
        dt, cum, cum_t = _dt_prep(r_ref[...], b_ref[...], a_ref[...])
        dt_ref[...] = dt
        cum_ref[...] = cum
        cumt_ref[...] = cum_t

    blk = pl.BlockSpec((Q, 128), lambda n: (n, 0))
    return pl.pallas_call(
        body, name="dt_fwd", grid=(nc,),
        in_specs=[pl.BlockSpec((Q, 128), lambda n: (n, OFF_DT // 128)), _full((1, 128)), _full((1, 128))],
        out_specs=[blk, blk, pl.BlockSpec((None, 128, Q), lambda n: (n, 0, 0))],
        out_shape=[jax.ShapeDtypeStruct((s, 128), F32), jax.ShapeDtypeStruct((s, 128), F32), jax.ShapeDtypeStruct((nc, 128, Q), F32)],
    )(proj, dtb, alog)


def _dt_bwd(proj, dtb, alog, ddt, dcum, dcumt, dproj):
    s = proj.shape[0]
    nc = s // Q

    def body(r_ref, b_ref, a_ref, ddt_ref, dcum_ref, dcumt_ref, _, dr_ref, db_ref, da_ref):
        _, vjp = jax.vjp(_dt_prep, r_ref[...], b_ref[...], a_ref[...])
        dr, db, da = vjp((ddt_ref[...], dcum_ref[...], dcumt_ref[...]))
        dr_ref[...] = dr.astype(BF16)

        @pl.when(pl.program_id(0) == 0)
        def _():
            db_ref[...] = jnp.zeros_like(db_ref)
            da_ref[...] = jnp.zeros_like(da_ref)

        db_ref[...] += db
        da_ref[...] += da

    blk = pl.BlockSpec((Q, 128), lambda n: (n, 0))
    pblk = pl.BlockSpec((Q, 128), lambda n: (n, OFF_DT // 128))
    return pl.pallas_call(
        body, name="dt_bwd", grid=(nc,),
        in_specs=[pblk, _full((1, 128)), _full((1, 128)), blk, blk, pl.BlockSpec((None, 128, Q), lambda n: (n, 0, 0)), ANY],
        out_specs=[pblk, _full((1, 128)), _full((1, 128))],
        out_shape=[jax.ShapeDtypeStruct(dproj.shape, BF16), jax.ShapeDtypeStruct((1, 128), F32), jax.ShapeDtypeStruct((1, 128), F32)],
        input_output_aliases={6: 0},
    )(proj, dtb, alog, ddt, dcum, dcumt, dproj)


def _ssd_specs(chunk_of):
    gs = SSD_GPS
    xs = pl.BlockSpec((Q, 256 * gs), lambda n, g: (chunk_of(n), g))
    bm = pl.BlockSpec((Q, 128 * gs), lambda n, g: (chunk_of(n), DI // (128 * gs) + g))
    cm = pl.BlockSpec((Q, 128 * gs), lambda n, g: (chunk_of(n), (DI + D) // (128 * gs) + g))
    per_chunk = pl.BlockSpec((Q, 128), lambda n, g: (chunk_of(n), 0))
    cum_t = pl.BlockSpec((None, 128, Q), lambda n, g: (chunk_of(n), 0, 0))
    state = pl.BlockSpec((None, 256 * gs, 128), lambda n, g: (chunk_of(n), g, 0))
    vec = pl.BlockSpec((1, 128), lambda n, g: (0, 0))
    return xs, bm, cm, per_chunk, cum_t, state, vec


SSD_GPS = 8


def _aligned(v, m):
    return v if isinstance(v, int) else pl.multiple_of(v, m)


def _ssd_fwd(xbc, dt, cum, cum_t, dsk):
    s = xbc.shape[0]
    nc = s // Q
    xs, bm, cm, per_chunk, cumt_spec, state_spec, vec = _ssd_specs(lambda n: n)
    gs = SSD_GPS

    def body(x_ref, b_ref, c_ref, dt_ref, cum_ref, cumt_ref, dsk_ref, y_ref, st_ref, carry):
        n, gstep = pl.program_id(0), (0 if gs == NGRP else pl.program_id(1))
        rows = pl.ds(_aligned(gstep * (256 * gs), 256 * gs), 256 * gs)

        @pl.when(n == 0)
        def _():
            carry[rows, :] = jnp.zeros((256 * gs, 128), F32)

        st_ref[...] = carry[rows, :]
        for k in range(gs):
            xo, so, bo = 256 * k, 256 * k, 128 * k
            y0, y1, n0, n1 = _ssd_group(
                x_ref[:, xo:xo + 128], x_ref[:, xo + 128:xo + 256], st_ref[so:so + 128, :], st_ref[so + 128:so + 256, :],
                b_ref[:, bo:bo + 128], c_ref[:, bo:bo + 128], dt_ref[...], cum_ref[...], cumt_ref[...], dsk_ref[...],
                gstep * gs + k)
            y_ref[:, xo:xo + 128] = y0
            y_ref[:, xo + 128:xo + 256] = y1
            base = gstep * (256 * gs) + so
            carry[pl.ds(_aligned(base, 128), 128), :] = n0
            carry[pl.ds(_aligned(base + 128, 128), 128), :] = n1

    return pl.pallas_call(
        body, name="ssd_fwd", grid=(nc, NGRP // gs),
        in_specs=[xs, bm, cm, per_chunk, per_chunk, cumt_spec, vec],
        out_specs=[xs, state_spec],
        out_shape=[jax.ShapeDtypeStruct((s, DI), F32), jax.ShapeDtypeStruct((nc, DI, 128), F32)],
        scratch_shapes=[pltpu.VMEM((DI, 128), F32)],
    )(xbc, xbc, xbc, dt, cum, cum_t, dsk)


def _ssd_bwd(xbc, dt, cum, cum_t, dsk, states, dy, comm=None):
    s = xbc.shape[0]
    nc = s // Q
    xs, bm, cm, per_chunk, cumt_spec, state_spec, vec = _ssd_specs(lambda n: nc - 1 - n)
    gs = SSD_GPS
    n_ci, n_co = (len(comm.ins), len(comm.outs)) if comm is not None else (0, 0)
    steps = (nc, NGRP // gs)

    def body(*refs):
        x_ref, b_ref, c_ref, dt_ref, cum_ref, cumt_ref, dsk_ref, st_ref, dy_ref = refs[:9]
        ci_refs = refs[9:9 + n_ci]
        dx_ref, db_ref, dc_ref, ddt_ref, dcum_ref, dcumt_ref, ddsk_ref = refs[9 + n_ci:16 + n_ci]
        co_refs = refs[16 + n_ci:16 + n_ci + n_co]
        carry = refs[16 + n_ci + n_co]
        cs_refs = refs[17 + n_ci + n_co:]
        n, gstep = pl.program_id(0), (0 if gs == NGRP else pl.program_id(1))
        rows = pl.ds(_aligned(gstep * (256 * gs), 256 * gs), 256 * gs)
        if comm is not None:
            pl.when((pl.program_id(0) == 0) & (pl.program_id(1) == 0))(lambda: comm.start(ci_refs, co_refs, cs_refs))

        @pl.when(n == 0)
        def _():
            carry[rows, :] = jnp.zeros((256 * gs, 128), F32)

        def zero_skip_sum():
            ddsk_ref[...] = jnp.zeros_like(ddsk_ref)

        def zero_chunk_sums():
            ddt_ref[...] = jnp.zeros_like(ddt_ref)
            dcum_ref[...] = jnp.zeros_like(dcum_ref)
            dcumt_ref[...] = jnp.zeros_like(dcumt_ref)

        if isinstance(gstep, int):
            pl.when(n == 0)(zero_skip_sum)
            zero_chunk_sums()
        else:
            pl.when((n == 0) & (gstep == 0))(zero_skip_sum)
            pl.when(gstep == 0)(zero_chunk_sums)

        for k in range(gs):
            xo, so, bo = 256 * k, 256 * k, 128 * k
            base = gstep * (256 * gs) + so
            lo = pl.ds(_aligned(base, 128), 128)
            hi = pl.ds(_aligned(base + 128, 128), 128)
            fn = functools.partial(_ssd_group, grp=gstep * gs + k)
            _, vjp = jax.vjp(fn, x_ref[:, xo:xo + 128], x_ref[:, xo + 128:xo + 256], st_ref[so:so + 128, :],
                             st_ref[so + 128:so + 256, :], b_ref[:, bo:bo + 128], c_ref[:, bo:bo + 128],
                             dt_ref[...], cum_ref[...], cumt_ref[...], dsk_ref[...])
            dx0, dx1, ds0, ds1, dbm, dcm, ddt, dcum, dcumt, ddsk = vjp(
                (dy_ref[:, xo:xo + 128].astype(F32), dy_ref[:, xo + 128:xo + 256].astype(F32), carry[lo, :], carry[hi, :]))
            dx_ref[:, xo:xo + 128] = dx0
            dx_ref[:, xo + 128:xo + 256] = dx1
            db_ref[:, bo:bo + 128] = dbm
            dc_ref[:, bo:bo + 128] = dcm
            ddt_ref[...] += ddt
            dcum_ref[...] += dcum
            dcumt_ref[...] += dcumt
            ddsk_ref[...] += ddsk
            carry[lo, :] = ds0
            carry[hi, :] = ds1

        if comm is not None:
            pl.when((pl.program_id(0) == steps[0] - 1) & (pl.program_id(1) == steps[1] - 1))(
                lambda: comm.finish(ci_refs, co_refs, cs_refs))

    grp_blk = pl.BlockSpec((Q, 128 * gs), lambda n, g: (nc - 1 - n, g))
    extra_in = comm.ins if comm is not None else []
    extra_out = comm.outs if comm is not None else []
    extra_sems = comm.sems if comm is not None else []
    return pl.pallas_call(
        body, name="ssd_bwd", grid=steps,
        in_specs=[xs, bm, cm, per_chunk, per_chunk, cumt_spec, vec, state_spec, xs] + [ANY] * n_ci,
        out_specs=[xs, grp_blk, grp_blk, per_chunk, per_chunk, cumt_spec, vec] + [ANY] * n_co,
        out_shape=[jax.ShapeDtypeStruct((s, DI), F32), jax.ShapeDtypeStruct((s, D), F32), jax.ShapeDtypeStruct((s, D), F32),
                   jax.ShapeDtypeStruct((s, 128), F32), jax.ShapeDtypeStruct((s, 128), F32),
                   jax.ShapeDtypeStruct((nc, 128, Q), F32), jax.ShapeDtypeStruct((1, 128), F32)] + extra_out,
        scratch_shapes=[pltpu.VMEM((DI, 128), F32)] + extra_sems,
    )(xbc, xbc, xbc, dt, cum, cum_t, dsk, states, dy, *extra_in)


def _adam_math(w, g, m, v):
    m2 = ADAM_B1 * m + (1.0 - ADAM_B1) * g
    v2 = ADAM_B2 * v + (1.0 - ADAM_B2) * jnp.square(g)
    m_hat = m2 / (1.0 - ADAM_B1 ** ADAM_STEP)
    v_hat = v2 / (1.0 - ADAM_B2 ** ADAM_STEP)
    delta = -ADAM_LR * (m_hat / (jnp.sqrt(v_hat) + ADAM_EPS) + ADAM_WD * w)
    return delta, m2, v2


def _adam(w, g, m, v, name):
    r, c = w.shape
    tr = r if r * c * 4 <= (1 << 20) else _pick(r, (128, 64, 32, 16, 8))

    def body(w_ref, g_ref, m_ref, v_ref, d_ref, m2_ref, v2_ref):
        d, m2, v2 = _adam_math(w_ref[...], g_ref[...], m_ref[...], v_ref[...])
        d_ref[...] = d
        m2_ref[...] = m2
        v2_ref[...] = v2

    blk = pl.BlockSpec((tr, c), lambda i: (i, 0))
    shp = jax.ShapeDtypeStruct((r, c), F32)
    return pl.pallas_call(body, name=name, grid=(r // tr,), in_specs=[blk] * 4, out_specs=[blk] * 3,
                          out_shape=[shp] * 3)(w, g, m, v)


def _sum_leading(xs, name, out_dtype=F32, tr=256):
    if isinstance(xs, tuple):
        a, b = xs
        n, r, c = a.shape
        tr = _pick(r, (tr, 128, 64, 32, 16, 8))

        def body2(a_ref, b_ref, o_ref):
            o_ref[...] = (a_ref[...].astype(F32) + b_ref[...].astype(F32)).astype(out_dtype)

        blk = pl.BlockSpec((None, tr, c), lambda s, i: (s, i, 0))
        return pl.pallas_call(body2, name=name, grid=(n, r // tr), in_specs=[blk, blk], out_specs=blk,
                              out_shape=jax.ShapeDtypeStruct((n, r, c), out_dtype))(a, b)
    n, r, c = xs.shape
    tr = r if n * r * c * 4 <= (8 << 20) else _pick(r, (tr, 128, 64, 32, 16, 8))

    def body(x_ref, o_ref):
        acc = x_ref[0].astype(F32)
        for s in range(1, n):
            acc = acc + x_ref[s].astype(F32)
        o_ref[...] = acc.astype(out_dtype)

    return pl.pallas_call(body, name=name, grid=(r // tr,), in_specs=[pl.BlockSpec((n, tr, c), lambda i: (0, i, 0))],
                          out_specs=pl.BlockSpec((tr, c), lambda i: (i, 0)),
                          out_shape=jax.ShapeDtypeStruct((r, c), out_dtype))(xs)


def _mod_fwd(c8, w_mod, b_sl):
    def body(c_ref, w_ref, b_ref, o_ref):
        o_ref[...] = jnp.dot(jax.nn.silu(c_ref[...]), w_ref[...], precision=HIGHEST, preferred_element_type=F32) + b_ref[...]

    return pl.pallas_call(body, name="mod_fwd", in_specs=[VMEM, VMEM, VMEM], out_specs=VMEM,
                          out_shape=jax.ShapeDtypeStruct((N_DEV, w_mod.shape[1]), F32))(c8, w_mod, b_sl)


def _mod_wgrad(c8, dmod):
    def body(c_ref, d_ref, o_ref):
        o_ref[...] = lax.dot_general(jax.nn.silu(c_ref[...]), d_ref[...], (((0,), (0,)), ((), ())),
                                     precision=HIGHEST, preferred_element_type=F32)

    return pl.pallas_call(body, name="mod_wgrad", in_specs=[VMEM, VMEM], out_specs=VMEM,
                          out_shape=jax.ShapeDtypeStruct((D, dmod.shape[1]), F32))(c8, dmod)


def _place():
    x, y, c = lax.axis_index("x"), lax.axis_index("y"), lax.axis_index("c")
    chips = [(1 - x, y), (x, 1 - y), (1 - x, 1 - y)]
    return x, y, c, chips


def _all_gather_small(v, name):
    r, w = v.shape

    def body(v_ref, o_ref, send_sems, recv_sems, local_sem):
        x, y, c, _ = _place()
        me = 4 * x + 2 * y + c
        own = pltpu.make_async_copy(v_ref, o_ref.at[me], local_sem)
        own.start()
        sends = []
        for k in range(1, N_DEV):
            tx = 1 - x if k & 4 else x
            ty = 1 - y if k & 2 else y
            tc = 1 - c if k & 1 else c
            peer = 4 * tx + 2 * ty + tc
            cp = pltpu.make_async_remote_copy(src_ref=v_ref, dst_ref=o_ref.at[me], send_sem=send_sems.at[k - 1],
                                              recv_sem=recv_sems.at[k - 1], device_id=(tx, ty, tc), device_id_type=MESH)
            cp.start()
            sends.append((cp, peer, (tx, ty, tc)))
        for k, (cp, peer, dev) in enumerate(sends):
            pltpu.make_async_remote_copy(src_ref=v_ref, dst_ref=o_ref.at[peer], send_sem=send_sems.at[k],
                                         recv_sem=recv_sems.at[k], device_id=dev, device_id_type=MESH).wait_recv()
        for cp, _, _ in sends:
            cp.wait_send()
        own.wait()

    return pl.pallas_call(
        body, name=name, in_specs=[VMEM], out_specs=VMEM, out_shape=jax.ShapeDtypeStruct((N_DEV, r, w), v.dtype),
        scratch_shapes=[pltpu.SemaphoreType.DMA((N_DEV - 1,)), pltpu.SemaphoreType.DMA((N_DEV - 1,)), pltpu.SemaphoreType.DMA],
    )(v)


def _all_gather_weights(shards):
    n = len(shards)
    nsem = n * 3 * AG_CHUNKS

    def body(*refs):
        srcs, dsts = refs[:n], refs[n:2 * n]
        send_sems, recv_sems, fwd_send_sems, fwd_recv_sems = refs[2 * n:]
        x, y, c, chips = _place()
        q = 2 * x + y
        sibling = (x, y, 1 - c)
        sends = []
        for r in range(AG_CHUNKS):
            for t in range(n):
                hc = srcs[t].shape[0] // 2 // AG_CHUNKS
                rows = pl.ds(c * (hc * AG_CHUNKS) + r * hc, hc)
                for j, (cx, cy) in enumerate(chips):
                    k = (t * 3 + j) * AG_CHUNKS + r
                    cp = pltpu.make_async_remote_copy(src_ref=srcs[t].at[rows], dst_ref=dsts[t].at[q, rows],
                                                      send_sem=send_sems.at[k], recv_sem=recv_sems.at[k],
                                                      device_id=(cx, cy, c), device_id_type=MESH)
                    cp.start()
                    sends.append(cp)
        fwds = []
        for r in range(AG_CHUNKS):
            for t in range(n):
                hc = srcs[t].shape[0] // 2 // AG_CHUNKS
                sub = hc // D2D_SPLIT
                for j, (cx, cy) in enumerate(chips):
                    k = (t * 3 + j) * AG_CHUNKS + r
                    base = c * (hc * AG_CHUNKS) + r * hc
                    part = dsts[t].at[2 * cx + cy, pl.ds(base, hc)]
                    pltpu.make_async_remote_copy(src_ref=part, dst_ref=part, send_sem=send_sems.at[k], recv_sem=recv_sems.at[k],
                                                 device_id=(cx, cy, c), device_id_type=MESH).wait_recv()
                    for u in range(D2D_SPLIT):
                        piece = dsts[t].at[2 * cx + cy, pl.ds(base + u * sub, sub)]
                        pltpu.make_async_remote_copy(src_ref=piece, dst_ref=piece, send_sem=fwd_send_sems.at[k],
                                                     recv_sem=fwd_recv_sems.at[k], device_id=sibling, device_id_type=MESH).start()
                    fwds.append((part, k))
        for r in range(AG_CHUNKS):
            for t in range(n):
                hc = srcs[t].shape[0] // 2 // AG_CHUNKS
                for j, (cx, cy) in enumerate(chips):
                    k = (t * 3 + j) * AG_CHUNKS + r
                    part = dsts[t].at[2 * cx + cy, pl.ds((1 - c) * (hc * AG_CHUNKS) + r * hc, hc)]
                    pltpu.make_async_remote_copy(src_ref=part, dst_ref=part, send_sem=fwd_send_sems.at[k],
                                                 recv_sem=fwd_recv_sems.at[k], device_id=sibling, device_id_type=MESH).wait_recv()
        for cp in sends:
            cp.wait_send()
        for part, k in fwds:
            pltpu.make_async_remote_copy(src_ref=part, dst_ref=part, send_sem=fwd_send_sems.at[k], recv_sem=fwd_recv_sems.at[k],
                                         device_id=sibling, device_id_type=MESH).wait_send()

    return pl.pallas_call(
        body, name="all_gather_weights", in_specs=[ANY] * n, out_specs=[ANY] * n,
        out_shape=[jax.ShapeDtypeStruct((N_CHIP,) + s.shape, s.dtype) for s in shards],
        scratch_shapes=[pltpu.SemaphoreType.DMA((nsem,)), pltpu.SemaphoreType.DMA((nsem,)), pltpu.SemaphoreType.DMA((nsem,)),
                        pltpu.SemaphoreType.DMA((nsem,))],
    )(*shards)


def _exchange_halves(grads, tag=""):
    n = len(grads)

    def body(*refs):
        srcs, theirs = refs[:n], refs[n:2 * n]
        send_sems, recv_sems = refs[2 * n:]
        x, y, c, _ = _place()
        sibling = (x, y, 1 - c)
        waits = []
        for t in range(n):
            h = srcs[t].shape[1] // 2
            sub = h // D2D_SPLIT
            for s in range(N_CHIP):
                for u in range(D2D_SPLIT):
                    pltpu.make_async_remote_copy(src_ref=srcs[t].at[s, pl.ds((1 - c) * h + u * sub, sub)],
                                                 dst_ref=theirs[t].at[s, pl.ds(u * sub, sub)],
                                                 send_sem=send_sems.at[t], recv_sem=recv_sems.at[t],
                                                 device_id=sibling, device_id_type=MESH).start()
            waits.append(pltpu.make_async_remote_copy(src_ref=srcs[t].at[:, pl.ds((1 - c) * h, h)], dst_ref=theirs[t],
                                                      send_sem=send_sems.at[t], recv_sem=recv_sems.at[t],
                                                      device_id=sibling, device_id_type=MESH))
        for whole in waits:
            whole.wait()

    half = [jax.ShapeDtypeStruct((N_CHIP, g.shape[1] // 2, g.shape[2]), g.dtype) for g in grads]
    return pl.pallas_call(
        body, name=f"exchange_halves_{tag}", in_specs=[ANY] * n, out_specs=[ANY] * n, out_shape=half,
        scratch_shapes=[pltpu.SemaphoreType.DMA((n,)), pltpu.SemaphoreType.DMA((n,))],
    )(*grads)


def _chip_exchange_comm(parts):
    n = len(parts)

    def copies(srcs, dsts, sems):
        send_sems, recv_sems, local_sems = sems
        x, y, c, chips = _place()
        q = 2 * x + y
        owns = [pltpu.make_async_copy(srcs[t].at[q], dsts[t].at[q], local_sems.at[t]) for t in range(n)]
        sends, recvs = [], []
        for t in range(n):
            for j, (cx, cy) in enumerate(chips):
                args = dict(send_sem=send_sems.at[3 * t + j], recv_sem=recv_sems.at[3 * t + j],
                            device_id=(cx, cy, c), device_id_type=MESH)
                sends.append(pltpu.make_async_remote_copy(src_ref=srcs[t].at[2 * cx + cy], dst_ref=dsts[t].at[q], **args))
                part = dsts[t].at[2 * cx + cy]
                recvs.append(pltpu.make_async_remote_copy(src_ref=part, dst_ref=part, **args))
        return owns, sends, recvs

    def start(srcs, dsts, sems):
        owns, sends, _ = copies(srcs, dsts, sems)
        for cp in owns + sends:
            cp.start()

    def finish(srcs, dsts, sems):
        owns, sends, recvs = copies(srcs, dsts, sems)
        for cp in recvs:
            cp.wait_recv()
        for cp in sends:
            cp.wait_send()
        for cp in owns:
            cp.wait()

    return _Comm(parts, [jax.ShapeDtypeStruct(p.shape, p.dtype) for p in parts],
                 [pltpu.SemaphoreType.DMA((3 * n,)), pltpu.SemaphoreType.DMA((3 * n,)), pltpu.SemaphoreType.DMA((n,))], start, finish)


def _gather_relay_comm(shard):
    h = shard.shape[0] // 2
    sub = h // D2D_SPLIT

    def place_and_copies(srcs, dsts, sems):
        send_sems, recv_sems, fwd_send_sems, fwd_recv_sems = sems
        x, y, c, chips = _place()
        q = 2 * x + y
        mine, theirs = pl.ds(c * h, h), pl.ds((1 - c) * h, h)
        sends, recvs, relays, relayed = [], [], [], []
        for j, (cx, cy) in enumerate(chips):
            ici = dict(send_sem=send_sems.at[j], recv_sem=recv_sems.at[j], device_id=(cx, cy, c), device_id_type=MESH)
            d2d = dict(send_sem=fwd_send_sems.at[j], recv_sem=fwd_recv_sems.at[j], device_id=(x, y, 1 - c), device_id_type=MESH)
            sends.append(pltpu.make_async_remote_copy(src_ref=srcs[0].at[mine], dst_ref=dsts[0].at[q, mine], **ici))
            landed = dsts[0].at[2 * cx + cy, mine]
            recvs.append(pltpu.make_async_remote_copy(src_ref=landed, dst_ref=landed, **ici))
            pieces = [dsts[0].at[2 * cx + cy, pl.ds(c * h + u * sub, sub)] for u in range(D2D_SPLIT)]
            relays.append(([pltpu.make_async_remote_copy(src_ref=p, dst_ref=p, **d2d) for p in pieces],
                           pltpu.make_async_remote_copy(src_ref=landed, dst_ref=landed, **d2d)))
            other = dsts[0].at[2 * cx + cy, theirs]
            relayed.append(pltpu.make_async_remote_copy(src_ref=other, dst_ref=other, **d2d))
        return sends, recvs, relays, relayed

    def start(srcs, dsts, sems):
        for cp in place_and_copies(srcs, dsts, sems)[0]:
            cp.start()

    def middle(srcs, dsts, sems):
        _, recvs, relays, _ = place_and_copies(srcs, dsts, sems)
        for cp, (pieces, _) in zip(recvs, relays):
            cp.wait_recv()
            for piece in pieces:
                piece.start()

    def finish(srcs, dsts, sems):
        sends, _, relays, relayed = place_and_copies(srcs, dsts, sems)
        for cp in relayed:
            cp.wait_recv()
        for cp in sends:
            cp.wait_send()
        for _, whole in relays:
            whole.wait_send()

    comm = _Comm([shard], [jax.ShapeDtypeStruct((N_CHIP,) + shard.shape, shard.dtype)],
                 [pltpu.SemaphoreType.DMA((3,))] * 4, start, finish)
    comm.middle = middle
    return comm


def _run_comm(comm, name):
    n_ci, n_co = len(comm.ins), len(comm.outs)

    def body(*refs):
        ci, co, cs = refs[:n_ci], refs[n_ci:n_ci + n_co], refs[n_ci + n_co:]
        comm.start(ci, co, cs)
        if comm.middle is not None:
            comm.middle(ci, co, cs)
        comm.finish(ci, co, cs)

    return pl.pallas_call(body, name=name, in_specs=[ANY] * n_ci, out_specs=[ANY] * n_co, out_shape=comm.outs,
                          scratch_shapes=comm.sems)(*comm.ins)


def _share_halves(halves, tag=""):
    n = len(halves)

    def body(*refs):
        srcs, dsts = refs[:n], refs[n:2 * n]
        send_sems, recv_sems = refs[2 * n:]
        x, y, c, _ = _place()
        sibling = (x, y, 1 - c)
        for t in range(n):
            h = srcs[t].shape[0]
            sub = h // (2 * D2D_SPLIT)
            for u in range(2 * D2D_SPLIT):
                pltpu.make_async_remote_copy(src_ref=srcs[t].at[pl.ds(u * sub, sub)], dst_ref=dsts[t].at[pl.ds(c * h + u * sub, sub)],
                                             send_sem=send_sems.at[t], recv_sem=recv_sems.at[t],
                                             device_id=sibling, device_id_type=MESH).start()
        for t in range(n):
            h = srcs[t].shape[0]
            pltpu.make_async_remote_copy(src_ref=srcs[t], dst_ref=dsts[t].at[pl.ds((1 - c) * h, h)], send_sem=send_sems.at[t],
                                         recv_sem=recv_sems.at[t], device_id=sibling, device_id_type=MESH).wait()

    return pl.pallas_call(
        body, name=f"share_halves_{tag}", in_specs=[ANY] * n, out_specs=[ANY] * n,
        out_shape=[jax.ShapeDtypeStruct((2 * h.shape[0], h.shape[1]), h.dtype) for h in halves],
        scratch_shapes=[pltpu.SemaphoreType.DMA((n,)), pltpu.SemaphoreType.DMA((n,))],
    )(*halves)


def _gather_weights(shards, chip):
    gathered = _all_gather_weights(shards)
    return [lax.dynamic_update_slice(g, s[None], (chip, 0, 0)) for g, s in zip(gathered, shards)]


def _reduce_scatter_begin(sends, core, tag):
    theirs = _exchange_halves(sends, tag)
    mines = [lax.dynamic_slice(g, (0, core * (g.shape[1] // 2), 0), (N_CHIP, g.shape[1] // 2, g.shape[2])) for g in sends]
    pair = [_sum_leading((m, t), f"pair_sum_{tag}{i}", out_dtype=BF16) for i, (m, t) in enumerate(zip(mines, theirs))]
    return _chip_exchange_comm(pair)


def _reduce_scatter_end(contrib, core, tag):
    halves = [_sum_leading(c, f"chip_sum_{tag}{i}") for i, c in enumerate(contrib)]
    shared = _share_halves(halves, tag)
    return [lax.dynamic_update_slice(full, mine, (core * mine.shape[0], 0)) for full, mine in zip(shared, halves)]


def _reduce_scatter(sends, core, tag=""):
    contrib = _run_comm(_reduce_scatter_begin(sends, core, tag), f"exchange_chips_{tag}")
    return _reduce_scatter_end(contrib, core, tag)


def _pack(arrs):
    rows = []
    for a in arrs:
        flat = a.astype(F32).reshape(-1)
        pad = (-flat.shape[0]) % 1024
        rows.append(jnp.pad(flat, (0, pad)).reshape(-1, 128))
    return jnp.concatenate(rows, axis=0)


def _unpack(buf, shapes):
    out, r = [], 0
    for shp in shapes:
        size = 1
        for d in shp:
            size *= d
        nr = (size + 1023) // 1024 * 8
        out.append(buf[r:r + nr].reshape(-1)[:size].reshape(shp))
        r += nr
    return out


R_BG, R_BS, R_OUT, R_FF = D // N_CHIP, DI // N_CHIP, D // N_CHIP, DFF // N_CHIP


def _pack_rest_shard(w_bg, w_bs, w_out, w_ff2, w_ff1):
    return jnp.concatenate([w_bg, w_bs, w_out, w_ff2, w_ff1], axis=0).astype(BF16)


def _unpack_rest(g):
    o1, o2, o3, o4 = R_BG, R_BG + R_BS, R_BG + R_BS + R_OUT, R_BG + R_BS + R_OUT + R_FF
    return (g[:, :o1].reshape(D, D), g[:, o1:o2].reshape(DI, D), g[:, o2:o3].reshape(D, D),
            jnp.transpose(g[:, o4:], (1, 0, 2)).reshape(D, DFF), g[:, o3:o4].reshape(DFF, D))


def _local_step(x, t, mod, w_in_r, rest, gm_norm_w, gm_ws, gm_bs, conv_w, conv_b, dt_bias, a_log, d_skip, ssm_norm_w,
                final_norm_w, place=None):
    sh1, sc1, g1, sh2, sc2, g2 = [mod[:, i * D:(i + 1) * D] for i in range(6)]
    ws_b = jnp.where(jnp.tril(jnp.ones((Q, Q), bool))[None], gm_ws, 0.0).astype(BF16)
    bs_t = gm_bs.T
    pad32 = lambda v: jnp.pad(v, ((0, 0), (0, 128 - NHEAD)))
    dtb, alog, dsk = pad32(dt_bias), pad32(a_log), pad32(d_skip)

    h1 = _prenorm(x, sc1, sh1)
    gdt = (F32,) if place is None else (BF16,)
    if place is None:
        (proj,) = _matmul(h1, w_in_r, "nn", name="mm_proj", tn=1152)
        w_bg, w_bs, w_out, w_ff1, w_ff2 = rest
    else:
        chip, core = place
        proj, g_rest = _matmul(h1, w_in_r, "nn", name="mm_proj", tn=1152, comm=_gather_relay_comm(rest))
        w_bg, w_bs, w_out, w_ff1, w_ff2 = _unpack_rest(lax.dynamic_update_slice(g_rest, rest[None], (chip, 0, 0)))
    y_a = _sgu_fwd(proj, gm_norm_w, ws_b, bs_t)
    xbc = _conv_fwd(proj, conv_w, conv_b)
    dt, cum, cum_t = _dt_fwd(proj, dtb, alog)
    y_ssd, states = _ssd_fwd(xbc, dt, cum, cum_t, dsk)
    y_b = _gatenorm_fwd(y_ssd, proj, ssm_norm_w)
    (ba,) = _matmul(y_a, w_bg, "nn", name="mm_branch_gm", out_dtypes=(BF16,))
    (bb,) = _matmul(y_b, w_bs, "nn", name="mm_branch_ssm", out_dtypes=(BF16,), tm=512, tk=2048)
    mixed = _mix_fwd(proj, ba, bb)
    (mo,) = _matmul(mixed, w_out, "nn", name="mm_out")
    x1, h2 = _resid_norm(x, mo, g1, sc2, sh2)
    f, act = _matmul(h2, w_ff1, "nn", name="mm_ff1", out_dtypes=(BF16, BF16),
                     epi=lambda acc: (acc, jnp.square(jnp.maximum(acc, 0.0))))
    (fo,) = _matmul(act, w_ff2, "nn", name="mm_ff2", tm=512, tk=4096)
    loss8, dx2, dfo, dg2, dfw = _loss_head(x1, fo, g2, final_norm_w, t)

    (df,) = _matmul(dfo, w_ff2, "nt", name="mm_ff2_dx", out_dtypes=(BF16,), epi_ins=(f,),
                    epi=lambda acc, fv: (acc * (2.0 * jnp.maximum(fv.astype(F32), 0.0)),))
    (g_ff2,) = _matmul(act, dfo, "tn", name="mm_ff2_dw", out_dtypes=gdt, tm=512, tk=4096)
    (dh2,) = _matmul(df, w_ff1, "nt", name="mm_ff1_dx", tm=512, tk=4096)
    (g_ff1,) = _matmul(h2, df, "tn", name="mm_ff1_dw", out_dtypes=gdt, tm=512, tk=4096)
    dx1, dmo, dg1, dsc2, dsh2 = _resid_norm_bwd(x1, mo, g1, sc2, sh2, dh2, dx2)
    (dmixed,) = _matmul(dmo, w_out, "nt", name="mm_out_dx", out_dtypes=(BF16,))
    (g_out,) = _matmul(mixed, dmo, "tn", name="mm_out_dw", out_dtypes=gdt, tm=512, tk=4096)
    early = None
    if place is not None:
        stack = jnp.concatenate([g_out.reshape(N_CHIP, R_OUT, D), g_ff2.reshape(N_CHIP, R_FF, D),
                                 jnp.transpose(g_ff1.reshape(D, N_CHIP, D), (1, 0, 2))], axis=1).astype(BF16)
        early = _reduce_scatter_begin([stack], core, "early")
    dproj, dba, dbb = _mix_bwd(proj, ba, bb, dmixed)
    (dy_a,) = _matmul(dba, w_bg, "nt", name="mm_branch_gm_dx", out_dtypes=(BF16,))
    (g_bg,) = _matmul(y_a, dba, "tn", name="mm_branch_gm_dw", out_dtypes=gdt, tm=512, tk=4096)
    (dy_b,) = _matmul(dbb, w_bs, "nt", name="mm_branch_ssm_dx", out_dtypes=(BF16,))
    (g_bs,) = _matmul(y_b, dbb, "tn", name="mm_branch_ssm_dw", out_dtypes=gdt, tm=512, tk=4096)
    dproj, d_gm_norm, d_ws, d_bs_t = _sgu_bwd(proj, gm_norm_w, ws_b, bs_t, dy_a, dproj)
    dy_ssd, dproj, d_ssm_norm = _gatenorm_bwd(y_ssd, proj, ssm_norm_w, dy_b, dproj)
    dxs, dbm, dcm, ddt, dcum, dcum_t, d_dsk, *early_contrib = _ssd_bwd(xbc, dt, cum, cum_t, dsk, states, dy_ssd, comm=early)
    dproj, d_dtb, d_alog = _dt_bwd(proj, dtb, alog, ddt, dcum, dcum_t, dproj)
    dproj, dw_x, db_x = _conv_bwd(proj, conv_w, conv_b, dxs, dproj, 0, "conv_bwd_x")
    dproj, dw_b, db_b = _conv_bwd(proj, conv_w, conv_b, dbm, dproj, DI, "conv_bwd_b")
    dproj, dw_c, db_c = _conv_bwd(proj, conv_w, conv_b, dcm, dproj, DI + D, "conv_bwd_c")
    d_conv_w = jnp.concatenate([dw_x, dw_b, dw_c], axis=1)
    d_conv_b = jnp.concatenate([db_x, db_b, db_c], axis=1)
    (g_in_r,) = _matmul(h1, dproj, "tn", name="mm_proj_dw", out_dtypes=gdt, tm=512, tn=1152, tk=4096)
    late = None
    if place is not None:
        send_in = jnp.transpose(_restore_w_in(g_in_r).reshape(D, N_CHIP, W_IN // N_CHIP), (1, 0, 2)).astype(BF16)
        stack = jnp.concatenate([g_bg.reshape(N_CHIP, R_BG, D), g_bs.reshape(N_CHIP, R_BS, D)], axis=1).astype(BF16)
        late = _reduce_scatter_begin([send_in, stack], core, "late")
    dh1, *late_contrib = _matmul(dproj, w_in_r, "nt", name="mm_proj_dx", tk=3456, comm=late)
    grad_x, dsc1, dsh1 = _prenorm_bwd(x, sc1, sh1, dh1, dx1)

    dmod = jnp.concatenate([dsh1, dsc1, dg1, dsh2, dsc2, dg2], axis=1)
    small = dict(gm_ws=d_ws, gm_norm_w=d_gm_norm, gm_bs=d_bs_t.T, conv_w=d_conv_w, conv_b=d_conv_b,
                 dt_bias=d_dtb[:, :NHEAD], a_log=d_alog[:, :NHEAD], d_skip=d_dsk[:, :NHEAD],
                 ssm_norm_w=d_ssm_norm, final_norm_w=dfw, dmod=dmod, loss=loss8[:1, :1])
    if place is None:
        return grad_x, small, dict(w_in_r=g_in_r, w_bg=g_bg, w_bs=g_bs, w_out=g_out, w_ff1=g_ff1, w_ff2=g_ff2)
    (s_early,) = _reduce_scatter_end(early_contrib, core, "early")
    s_in, s_late = _reduce_scatter_end(late_contrib, core, "late")
    o1, o2 = R_OUT, R_OUT + R_FF
    big = dict(w_in=s_in, w_bg=s_late[:R_BG], w_bs=s_late[R_BG:], w_out=s_early[:o1], w_ff2=s_early[o1:o2], w_ff1=s_early[o2:])
    return grad_x, small, big


SMALL_KEYS = ("gm_ws", "gm_norm_w", "gm_bs", "conv_w", "conv_b", "dt_bias", "a_log", "d_skip", "ssm_norm_w",
              "final_norm_w", "dmod", "loss")
SMALL_SHAPES = ((GM_G, Q, Q), (1, D), (GM_G, Q), (4, CONV), (1, CONV), (1, NHEAD), (1, NHEAD), (1, NHEAD), (1, DI),
                (1, D), (1, 6 * D), (1, 1))


def _reorder_w_in(w_full):
    k = w_full.shape[0]
    return jnp.concatenate([w_full[:, :8192], w_full[:, 8224:], w_full[:, 8192:8224],
                            jnp.zeros((k, W_IN_R - W_IN), w_full.dtype)], axis=1)


def _restore_w_in(g_r):
    return jnp.concatenate([g_r[:, :8192], g_r[:, OFF_DT:OFF_DT + NHEAD], g_r[:, 8192:OFF_DT]], axis=1)


def kernel(x, c, w_mod, b_mod, w_in, gm_norm_w, gm_ws, gm_bs, conv_w, conv_b, dt_bias, a_log, d_skip, ssm_norm_w, w_branch_gm, w_branch_ssm, w_out, w_ff1, w_ff2, final_norm_w, loss_target, m_w_mod, m_b_mod, m_w_in, m_gm_norm_w, m_gm_ws, m_gm_bs, m_conv_w, m_conv_b, m_dt_bias, m_a_log, m_d_skip, m_ssm_norm_w, m_w_branch_gm, m_w_branch_ssm, m_w_out, m_w_ff1, m_w_ff2, m_final_norm_w, v_w_mod, v_b_mod, v_w_in, v_gm_norm_w, v_gm_ws, v_gm_bs, v_conv_w, v_conv_b, v_dt_bias, v_a_log, v_d_skip, v_ssm_norm_w, v_w_branch_gm, v_w_branch_ssm, v_w_out, v_w_ff1, v_w_ff2, v_final_norm_w):
    ax, ay, ac = lax.axis_index("x"), lax.axis_index("y"), lax.axis_index("c")
    chip = 2 * ax + ay
    dev = 2 * chip + ac
    seq = x.shape[1]
    nmod = w_mod.shape[2]

    first = jnp.concatenate([c, conv_w[0], jnp.zeros((3, D), F32)], axis=0)
    first_all = _all_gather_small(first, "all_gather_cond")
    c8 = first_all[:, 0, :]
    conv_w_full = jnp.concatenate([first_all[2 * k, 1:5, :] for k in range(N_CHIP)], axis=1)
    b_sl = lax.dynamic_slice(b_mod, (0, chip * nmod), (1, nmod))
    mod_part = _mod_fwd(c8, w_mod[0], b_sl)
    mod_all = _all_gather_small(mod_part, "all_gather_mod")
    mod = jnp.concatenate([lax.dynamic_slice(mod_all, (2 * k, dev, 0), (1, 1, nmod))[0] for k in range(N_CHIP)], axis=1)

    (g_in,) = _gather_weights([w_in[0].astype(BF16)], chip)
    w_in_r = _reorder_w_in(jnp.transpose(g_in, (1, 0, 2)).reshape(D, W_IN))
    rest = _pack_rest_shard(w_branch_gm[0], w_branch_ssm[0], w_out[0], w_ff2[0], w_ff1[0])

    grad_x, small, big = _local_step(
        x[0], loss_target[0], mod, w_in_r, rest, gm_norm_w, gm_ws[0], gm_bs[0], conv_w_full, conv_b, dt_bias, a_log,
        d_skip, ssm_norm_w, final_norm_w.reshape(1, D), place=(chip, ac))

    small_all = _all_gather_small(_pack([small[k] for k in SMALL_KEYS]), "all_gather_small_grads")
    small_sum = _sum_leading(small_all, "sum_small_grads")
    s_ws, s_gnw, s_bs, s_cw, s_cb, s_dtb, s_alog, s_dsk, s_snw, s_fnw, s_bmod, s_loss = _unpack(small_sum, SMALL_SHAPES)
    dmod_all = jnp.stack([_unpack(small_all[k], SMALL_SHAPES)[10][0] for k in range(N_DEV)], axis=0)
    g_w_mod = _mod_wgrad(c8, lax.dynamic_slice(dmod_all, (0, chip * nmod), (N_DEV, nmod)))
    g_conv_w = lax.dynamic_slice(s_cw, (0, chip * (CONV // N_CHIP)), (4, CONV // N_CHIP))

    g_w_in, g_w_bg, g_w_bs, g_w_out, g_w_ff2, g_w_ff1 = (big[k] for k in ("w_in", "w_bg", "w_bs", "w_out", "w_ff2", "w_ff1"))

    def adam_big(w, g, m, v, name):
        d, m2, v2 = _adam(w.reshape(g.shape), g, m.reshape(g.shape), v.reshape(g.shape), name)
        return g.reshape(w.shape), d.reshape(w.shape), m2.reshape(w.shape), v2.reshape(w.shape)

    res = {}
    res["w_mod"] = adam_big(w_mod, g_w_mod, m_w_mod, v_w_mod, "adam_w_mod")
    res["w_in"] = adam_big(w_in, g_w_in, m_w_in, v_w_in, "adam_w_in")
    res["w_branch_gm"] = adam_big(w_branch_gm, g_w_bg, m_w_branch_gm, v_w_branch_gm, "adam_w_branch_gm")
    res["w_branch_ssm"] = adam_big(w_branch_ssm, g_w_bs, m_w_branch_ssm, v_w_branch_ssm, "adam_w_branch_ssm")
    res["w_out"] = adam_big(w_out, g_w_out, m_w_out, v_w_out, "adam_w_out")
    res["w_ff1"] = adam_big(w_ff1, g_w_ff1, m_w_ff1, v_w_ff1, "adam_w_ff1")
    res["w_ff2"] = adam_big(w_ff2, g_w_ff2, m_w_ff2, v_w_ff2, "adam_w_ff2")

    names = ("b_mod", "gm_norm_w", "gm_ws", "gm_bs", "conv_w", "conv_b", "dt_bias", "a_log", "d_skip", "ssm_norm_w", "final_norm_w")
    ws = (b_mod, gm_norm_w, gm_ws, gm_bs, conv_w, conv_b, dt_bias, a_log, d_skip, ssm_norm_w, final_norm_w)
    ms = (m_b_mod, m_gm_norm_w, m_gm_ws, m_gm_bs, m_conv_w, m_conv_b, m_dt_bias, m_a_log, m_d_skip, m_ssm_norm_w, m_final_norm_w)
    vs = (v_b_mod, v_gm_norm_w, v_gm_ws, v_gm_bs, v_conv_w, v_conv_b, v_dt_bias, v_a_log, v_d_skip, v_ssm_norm_w, v_final_norm_w)
    gs = (s_bmod, s_gnw, s_ws, s_bs, g_conv_w, s_cb, s_dtb, s_alog, s_dsk, s_snw, s_fnw)
    gs = [g.reshape(w.shape) for g, w in zip(gs, ws)]
    shapes = [w.shape for w in ws]
    d_p, m_p, v_p = _adam(_pack(ws), _pack(gs), _pack(ms), _pack(vs), "adam_small")
    for name, g, d, m2, v2 in zip(names, gs, _unpack(d_p, shapes), _unpack(m_p, shapes), _unpack(v_p, shapes)):
        res[name] = (g, d, m2, v2)

    order = ("w_mod", "b_mod", "w_in", "gm_norm_w", "gm_ws", "gm_bs", "conv_w", "conv_b", "dt_bias", "a_log", "d_skip",
             "ssm_norm_w", "w_branch_gm", "w_branch_ssm", "w_out", "w_ff1", "w_ff2", "final_norm_w")
    loss = s_loss.reshape(())
    return (loss, grad_x.reshape(x.shape), *[res[k][0] for k in order], *[res[k][1] for k in order],
            *[res[k][2] for k in order], *[res[k][3] for k in order])
```

```python
import functools

import jax
import jax.numpy as jnp
from jax import lax
from jax.experimental import pallas as pl
from jax.experimental.pallas import tpu as pltpu

F32 = jnp.float32
BF16 = jnp.bfloat16
MESH = pl.DeviceIdType.MESH
HIGHEST = lax.Precision.HIGHEST

D = 1024
EPS = 1e-6
Q = 128
GM_G = 8
NHEAD = 32
NGRP = 8
DI = 2048
CONV = 4096
DFF = 4096
W_IN = 10272
W_IN_R = 10368
OFF_Z, OFF_XBC, OFF_GA, OFF_DT = 2048, 4096, 8192, 10240
N_CHIP = 4
N_DEV = 8
AG_CHUNKS = 4
D2D_SPLIT = 4

ADAM_LR, ADAM_B1, ADAM_B2, ADAM_EPS, ADAM_WD, ADAM_STEP = 0.001, 0.9, 0.999, 1e-08, 0.01, 10

ANY = pl.BlockSpec(memory_space=pl.ANY)
VMEM = pl.BlockSpec(memory_space=pltpu.VMEM)


def _full(shape):
    return pl.BlockSpec(shape, lambda *_: (0,) * len(shape))


def _pick(n, prefs):
    for p in prefs:
        if n % p == 0:
            return p
    return n


def _rms(x):
    return x * lax.rsqrt(jnp.mean(x * x, axis=-1, keepdims=True) + EPS)


def _modnorm(x, sc, sh):
    return _rms(x) * (1.0 + sc) + sh


def _sgu_pre(u, v, w):
    return jax.nn.gelu(u), _rms(jax.nn.gelu(v)) * w


def _gatenorm(y, z, w):
    g = y * jax.nn.silu(z)
    return _rms(g) * w


def _mix(ga, gb, ba, bb):
    return jax.nn.sigmoid(ga) * ba + jax.nn.sigmoid(gb) * bb


def _loss_tile(x1, fo, g2, fw, t):
    x2 = x1 + g2 * fo
    y = _rms(x2) * fw
    err = jnp.square(y - t)
    return 0.5 * jnp.sum(jnp.mean(err, axis=-1))


def _tril(n):
    r = lax.broadcasted_iota(jnp.int32, (n, n), 0)
    c = lax.broadcasted_iota(jnp.int32, (n, n), 1)
    return r >= c


def _dt_prep(dtr, dtb, alog):
    dt = jax.nn.softplus(dtr + dtb)
    a = dt * (-jnp.exp(alog))
    ones = _tril(Q).astype(F32)
    cum = jnp.dot(ones, a, precision=HIGHEST, preferred_element_type=F32)
    cum_t = lax.dot_general(a, ones, (((0,), (1,)), ((), ())), precision=HIGHEST, preferred_element_type=F32)
    return dt, cum, cum_t


def _ssd_group(x0, x1, s0, s1, bm, cm, dt, cum, cum_t, dsk, grp):
    lane = lax.broadcasted_iota(jnp.int32, (1, 128), 1)
    sub = lax.broadcasted_iota(jnp.int32, (128, 1), 0)
    future = jnp.where(_tril(Q), 0.0, -jnp.inf)
    half = lane < 64
    half_rows = sub < 64
    bmb = bm.astype(BF16)
    cmb = cm.astype(BF16)
    cb = lax.dot_general(cmb, bmb, (((1,), (1,)), ((), ())), preferred_element_type=F32)

    def col(v, h):
        return jnp.sum(jnp.where(lane == h, v, 0.0), axis=1, keepdims=True)

    def row(v, h):
        return jnp.sum(jnp.where(sub == h, v, 0.0), axis=0, keepdims=True)

    def last(c):
        return jnp.sum(jnp.where(sub == Q - 1, c, 0.0), axis=0, keepdims=True)

    outs, states = [], []
    for p, (xp, sp) in enumerate(((x0, s0), (x1, s1))):
        h_a = 4 * grp + 2 * p
        h_b = h_a + 1
        dt_a, dt_b = col(dt, h_a), col(dt, h_b)
        cum_a, cum_b = col(cum, h_a), col(cum, h_b)
        row_a, row_b = row(cum_t, h_a), row(cum_t, h_b)
        last_a, last_b = last(cum_a), last(cum_b)
        xdt = xp * jnp.where(half, dt_a, dt_b)
        xdtb = xdt.astype(BF16)
        m_a = (cb * jnp.exp(cum_a - row_a + future)).astype(BF16)
        m_b = (cb * jnp.exp(cum_b - row_b + future)).astype(BF16)
        y_intra = jnp.where(half, jnp.dot(m_a, xdtb, preferred_element_type=F32),
                            jnp.dot(m_b, xdtb, preferred_element_type=F32))
        y_inter = lax.dot_general(cmb, sp.astype(BF16), (((1,), (1,)), ((), ())), preferred_element_type=F32)
        y_inter = y_inter * jnp.where(half, jnp.exp(cum_a), jnp.exp(cum_b))
        w_end = jnp.where(half, jnp.exp(last_a - cum_a), jnp.exp(last_b - cum_b))
        upd = lax.dot_general((xdt * w_end).astype(BF16), bmb, (((0,), (0,)), ((), ())), preferred_element_type=F32)
        states.append(sp * jnp.where(half_rows, jnp.exp(last_a), jnp.exp(last_b)) + upd)
        outs.append(y_intra + y_inter + xp * jnp.where(half, col(dsk, h_a), col(dsk, h_b)))
    return outs[0], outs[1], states[0], states[1]


def _ssd_group_bwd(x0, x1, s0, s1, bm, cm, dt, cum, cum_t, dsk, grp, dy0, dy1, dyt0, dyt1, dn0, dn1):
    nt = (((1,), (1,)), ((), ()))
    lane = lax.broadcasted_iota(jnp.int32, (1, 128), 1)
    sub = lax.broadcasted_iota(jnp.int32, (128, 1), 0)
    r = lax.broadcasted_iota(jnp.int32, (Q, Q), 0)
    c = lax.broadcasted_iota(jnp.int32, (Q, Q), 1)
    future = jnp.where(r >= c, 0.0, -jnp.inf)
    future_t = jnp.where(c >= r, 0.0, -jnp.inf)
    half = lane < 64
    half_rows = sub < 64
    bmb = bm.astype(BF16)
    cmb = cm.astype(BF16)
    cb = lax.dot_general(cmb, bmb, nt, preferred_element_type=F32)
    cbt = lax.dot_general(bmb, cmb, nt, preferred_element_type=F32)

    def col(v, h):
        return jnp.sum(jnp.where(lane == h, v, 0.0), axis=1, keepdims=True)

    def row(v, h):
        return jnp.sum(jnp.where(sub == h, v, 0.0), axis=0, keepdims=True)

    def rows_of(v, mask):
        return jnp.sum(jnp.where(mask, v, 0.0), axis=1, keepdims=True)

    def total(v):
        return jnp.sum(jnp.sum(v, axis=1, keepdims=True), axis=0, keepdims=True)

    dcb = jnp.zeros((Q, Q), F32)
    dcbt = jnp.zeros((Q, Q), F32)
    dbm = jnp.zeros((Q, 128), F32)
    dcm = jnp.zeros((Q, 128), F32)
    ddt = jnp.zeros((Q, 128), F32)
    dcum = jnp.zeros((Q, 128), F32)
    dcumt = jnp.zeros((128, Q), F32)
    ddsk = jnp.zeros((1, 128), F32)
    dxs, dss = [], []
    for p, (xp, sp, dy, dyt, dns) in enumerate(((x0, s0, dy0, dyt0, dn0), (x1, s1, dy1, dyt1, dn1))):
        h_a = 4 * grp + 2 * p
        h_b = h_a + 1
        dt_a, dt_b = col(dt, h_a), col(dt, h_b)
        cum_a, cum_b = col(cum, h_a), col(cum, h_b)
        row_a, row_b = row(cum_t, h_a), row(cum_t, h_b)
        last_a = jnp.sum(jnp.where(sub == Q - 1, cum_a, 0.0), axis=0, keepdims=True)
        last_b = jnp.sum(jnp.where(sub == Q - 1, cum_b, 0.0), axis=0, keepdims=True)
        dtp = jnp.where(half, dt_a, dt_b)
        xdt = xp * dtp
        xdtb = xdt.astype(BF16)
        l_a, l_b = jnp.exp(cum_a - row_a + future), jnp.exp(cum_b - row_b + future)
        lt_a, lt_b = jnp.exp(row_a - cum_a + future_t), jnp.exp(row_b - cum_b + future_t)
        dya = jnp.where(half, dy, 0.0)
        dya_b, dyb_b = dya.astype(BF16), (dy - dya).astype(BF16)
        dm_a = lax.dot_general(dya_b, xdtb, nt, preferred_element_type=F32)
        dm_b = lax.dot_general(dyb_b, xdtb, nt, preferred_element_type=F32)
        dmt_a = lax.dot_general(xdtb, dya_b, nt, preferred_element_type=F32)
        dmt_b = lax.dot_general(xdtb, dyb_b, nt, preferred_element_type=F32)
        dxdt = (jnp.dot((cbt * lt_a).astype(BF16), dya_b, preferred_element_type=F32)
                + jnp.dot((cbt * lt_b).astype(BF16), dyb_b, preferred_element_type=F32))
        dseg_a, dseg_b = dm_a * (cb * l_a), dm_b * (cb * l_b)
        dcum_a = jnp.sum(dseg_a, axis=1, keepdims=True)
        dcum_b = jnp.sum(dseg_b, axis=1, keepdims=True)
        drow_a = -jnp.sum(dseg_a, axis=0, keepdims=True)
        drow_b = -jnp.sum(dseg_b, axis=0, keepdims=True)
        dcb = dcb + dm_a * l_a + dm_b * l_b
        dcbt = dcbt + dmt_a * lt_a + dmt_b * lt_b
        spb = sp.astype(BF16)
        y0 = lax.dot_general(cmb, spb, nt, preferred_element_type=F32)
        dy0s = dy * jnp.where(half, jnp.exp(cum_a), jnp.exp(cum_b))
        g = dy0s * y0
        g_a = rows_of(g, half)
        dcum_a = dcum_a + g_a
        dcum_b = dcum_b + (jnp.sum(g, axis=1, keepdims=True) - g_a)
        dcm = dcm + jnp.dot(dy0s.astype(BF16), spb, preferred_element_type=F32)
        dsp = jnp.dot((dyt * jnp.where(half_rows, jnp.exp(row_a), jnp.exp(row_b))).astype(BF16), cmb, preferred_element_type=F32)
        wp = jnp.where(half, jnp.exp(last_a - cum_a), jnp.exp(last_b - cum_b))
        xw = xdt * wp
        dnsb = dns.astype(BF16)
        dxw = lax.dot_general(bmb, dnsb, nt, preferred_element_type=F32)
        dbm = dbm + jnp.dot(xw.astype(BF16), dnsb, preferred_element_type=F32)
        dxdt = dxdt + dxw * wp
        gw = dxw * xw
        gw_a = rows_of(gw, half)
        gw_b = jnp.sum(gw, axis=1, keepdims=True) - gw_a
        dcum_a, dcum_b = dcum_a - gw_a, dcum_b - gw_b
        el_a, el_b = jnp.exp(last_a), jnp.exp(last_b)
        dsp = dsp + dns * jnp.where(half_rows, el_a, el_b)
        gs = jnp.sum(dns * sp, axis=1, keepdims=True)
        gs_a = jnp.sum(jnp.where(half_rows, gs, 0.0), axis=0, keepdims=True)
        dlast_a = jnp.sum(gw_a, axis=0, keepdims=True) + el_a * gs_a
        dlast_b = jnp.sum(gw_b, axis=0, keepdims=True) + el_b * (jnp.sum(gs, axis=0, keepdims=True) - gs_a)
        dcum_a = dcum_a + jnp.where(sub == Q - 1, dlast_a, 0.0)
        dcum_b = dcum_b + jnp.where(sub == Q - 1, dlast_b, 0.0)
        gx = dxdt * xp
        ddt_a = rows_of(gx, half)
        ddt_b = jnp.sum(gx, axis=1, keepdims=True) - ddt_a
        gk = dy * xp
        gk_a = total(jnp.where(half, gk, 0.0))
        gk_b = total(gk) - gk_a
        dxs.append(dxdt * dtp + dy * jnp.where(half, col(dsk, h_a), col(dsk, h_b)))
        dss.append(dsp)
        ddt = ddt + jnp.where(lane == h_a, ddt_a, 0.0) + jnp.where(lane == h_b, ddt_b, 0.0)
        dcum = dcum + jnp.where(lane == h_a, dcum_a, 0.0) + jnp.where(lane == h_b, dcum_b, 0.0)
        dcumt = dcumt + jnp.where(sub == h_a, drow_a, 0.0) + jnp.where(sub == h_b, drow_b, 0.0)
        ddsk = ddsk + jnp.where(lane == h_a, gk_a, 0.0) + jnp.where(lane == h_b, gk_b, 0.0)
    dcm = dcm + jnp.dot(dcb.astype(BF16), bmb, preferred_element_type=F32)
    dbm = dbm + jnp.dot(dcbt.astype(BF16), cmb, preferred_element_type=F32)
    return dxs[0], dxs[1], dss[0], dss[1], dbm, dcm, ddt, dcum, dcumt, ddsk


class _Comm:
    def __init__(self, ins, outs, sems, start, finish):
        self.ins, self.outs, self.sems, self.start, self.finish = list(ins), list(outs), list(sems), start, finish
        self.middle = None


def _matmul(a, b, mode, *, name, out_dtypes=(F32,), epi=None, epi_ins=(), tm=1024, tn=1024, tk=1024, comm=None):
    if mode == "nn":
        (m, k), n = a.shape, b.shape[1]
    elif mode == "nt":
        (m, k), n = a.shape, b.shape[0]
    else:
        (k, m), n = a.shape, b.shape[1]
    tm, tn, tk = _pick(m, (tm, 512, 256, 128)), _pick(n, (tn, 1152, 1024, 512, 384, 256, 128)), _pick(k, (tk, 1152, 1024, 512, 256, 128))
    nk = k // tk
    if mode == "nn":
        a_spec = pl.BlockSpec((tm, tk), lambda i, j, kk: (i, kk))
        b_spec = pl.BlockSpec((tk, tn), lambda i, j, kk: (kk, j))
        dims = (((1,), (0,)), ((), ()))
    elif mode == "nt":
        a_spec = pl.BlockSpec((tm, tk), lambda i, j, kk: (i, kk))
        b_spec = pl.BlockSpec((tn, tk), lambda i, j, kk: (j, kk))
        dims = (((1,), (1,)), ((), ()))
    else:
        a_spec = pl.BlockSpec((tk, tm), lambda i, j, kk: (kk, i))
        b_spec = pl.BlockSpec((tk, tn), lambda i, j, kk: (kk, j))
        dims = (((0,), (0,)), ((), ()))
    o_spec = pl.BlockSpec((tm, tn), lambda i, j, kk: (i, j))
    n_epi, n_out = len(epi_ins), len(out_dtypes)
    n_ci, n_co, n_cs = (len(comm.ins), len(comm.outs), len(comm.sems)) if comm is not None else (0, 0, 0)
    grid = (m // tm, n // tn, nk)

    def body(*refs):
        a_ref, b_ref = refs[0], refs[1]
        e_refs = refs[2:2 + n_epi]
        ci_refs = refs[2 + n_epi:2 + n_epi + n_ci]
        o_refs = refs[2 + n_epi + n_ci:2 + n_epi + n_ci + n_out]
        co_refs = refs[2 + n_epi + n_ci + n_out:2 + n_epi + n_ci + n_out + n_co]
        acc_ref = refs[2 + n_epi + n_ci + n_out + n_co]
        cs_refs = refs[3 + n_epi + n_ci + n_out + n_co:]
        if comm is not None:
            ids = [pl.program_id(d) for d in range(3)]
            pl.when((ids[0] == 0) & (ids[1] == 0) & (ids[2] == 0))(lambda: comm.start(ci_refs, co_refs, cs_refs))
            if comm.middle is not None:
                pl.when((ids[0] == (5 * grid[0]) // 8) & (ids[1] == 0) & (ids[2] == 0))(
                    lambda: comm.middle(ci_refs, co_refs, cs_refs))

        def finish(acc):
            outs = epi(acc, *[e[...] for e in e_refs]) if epi is not None else (acc,)
            for o_ref, val in zip(o_refs, outs):
                o_ref[...] = val.astype(o_ref.dtype)

        part = lax.dot_general(a_ref[...], b_ref[...], dims, preferred_element_type=F32)
        if nk == 1:
            finish(part)
        else:
            kk = pl.program_id(2)

            @pl.when(kk == 0)
            def _():
                acc_ref[...] = part

            @pl.when(kk > 0)
            def _():
                acc_ref[...] += part

            @pl.when(kk == nk - 1)
            def _():
                finish(acc_ref[...])

        if comm is not None:
            pl.when((ids[0] == grid[0] - 1) & (ids[1] == grid[1] - 1) & (ids[2] == grid[2] - 1))(
                lambda: comm.finish(ci_refs, co_refs, cs_refs))

    extra_in = comm.ins if comm is not None else []
    extra_out = comm.outs if comm is not None else []
    extra_sems = comm.sems if comm is not None else []
    return pl.pallas_call(
        body, name=name, grid=grid,
        in_specs=[a_spec, b_spec] + [o_spec] * n_epi + [ANY] * n_ci,
        out_specs=[o_spec] * n_out + [ANY] * n_co,
        out_shape=[jax.ShapeDtypeStruct((m, n), dt) for dt in out_dtypes] + extra_out,
        scratch_shapes=[pltpu.VMEM((tm, tn) if nk > 1 else (8, 128), F32)] + extra_sems,
        compiler_params=pltpu.CompilerParams(
            dimension_semantics=("arbitrary",) * 3 if comm is not None else ("parallel", "parallel", "arbitrary")),
    )(a, b, *epi_ins, *extra_in)


def _row_tile(s):
    return _pick(s, (512, 256, 128))


def _prenorm(x, sc, sh):
    s = x.shape[0]
    tm = _row_tile(s)

    def body(x_ref, sc_ref, sh_ref, h_ref):
        h_ref[...] = _modnorm(x_ref[...], sc_ref[...], sh_ref[...]).astype(BF16)

    row = pl.BlockSpec((tm, D), lambda i: (i, 0))
    return pl.pallas_call(body, name="prenorm", grid=(s // tm,), in_specs=[row, _full((1, D)), _full((1, D))],
                          out_specs=row, out_shape=jax.ShapeDtypeStruct((s, D), BF16))(x, sc, sh)


def _prenorm_bwd(x, sc, sh, dh, dx_res):
    s = x.shape[0]
    tm = _row_tile(s)

    def body(x_ref, sc_ref, sh_ref, dh_ref, dr_ref, dx_ref, dsc_ref, dsh_ref):
        _, vjp = jax.vjp(_modnorm, x_ref[...], sc_ref[...], sh_ref[...])
        dx, dsc, dsh = vjp(dh_ref[...])
        dx_ref[...] = dr_ref[...] + dx

        @pl.when(pl.program_id(0) == 0)
        def _():
            dsc_ref[...] = jnp.zeros_like(dsc_ref)
            dsh_ref[...] = jnp.zeros_like(dsh_ref)

        dsc_ref[...] += dsc
        dsh_ref[...] += dsh

    row = pl.BlockSpec((tm, D), lambda i: (i, 0))
    vec = _full((1, D))
    return pl.pallas_call(
        body, name="prenorm_bwd", grid=(s // tm,), in_specs=[row, vec, vec, row, row], out_specs=[row, vec, vec],
        out_shape=[jax.ShapeDtypeStruct((s, D), F32), jax.ShapeDtypeStruct((1, D), F32), jax.ShapeDtypeStruct((1, D), F32)],
    )(x, sc, sh, dh, dx_res)


def _resid_norm(x, mo, g1, sc, sh):
    s = x.shape[0]
    tm = _row_tile(s)

    def body(x_ref, mo_ref, g_ref, sc_ref, sh_ref, x1_ref, h_ref):
        x1 = x_ref[...] + g_ref[...] * mo_ref[...]
        x1_ref[...] = x1
        h_ref[...] = _modnorm(x1, sc_ref[...], sh_ref[...]).astype(BF16)

    row = pl.BlockSpec((tm, D), lambda i: (i, 0))
    vec = _full((1, D))
    return pl.pallas_call(
        body, name="resid_norm", grid=(s // tm,), in_specs=[row, row, vec, vec, vec], out_specs=[row, row],
        out_shape=[jax.ShapeDtypeStruct((s, D), F32), jax.ShapeDtypeStruct((s, D), BF16)],
    )(x, mo, g1, sc, sh)


def _resid_norm_bwd(x1, mo, g1, sc, sh, dh, dx2):
    s = x1.shape[0]
    tm = _row_tile(s)

    def body(x1_ref, mo_ref, g_ref, sc_ref, sh_ref, dh_ref, dx2_ref, dx1_ref, dmo_ref, dg_ref, dsc_ref, dsh_ref):
        _, vjp = jax.vjp(_modnorm, x1_ref[...], sc_ref[...], sh_ref[...])
        dx, dsc, dsh = vjp(dh_ref[...])
        dx1 = dx2_ref[...] + dx
        dx1_ref[...] = dx1
        dmo_ref[...] = (dx1 * g_ref[...]).astype(BF16)

        @pl.when(pl.program_id(0) == 0)
        def _():
            dg_ref[...] = jnp.zeros_like(dg_ref)
            dsc_ref[...] = jnp.zeros_like(dsc_ref)
            dsh_ref[...] = jnp.zeros_like(dsh_ref)

        dg_ref[...] += jnp.sum(dx1 * mo_ref[...], axis=0, keepdims=True)
        dsc_ref[...] += dsc
        dsh_ref[...] += dsh

    row = pl.BlockSpec((tm, D), lambda i: (i, 0))
    vec = _full((1, D))
    vshape = jax.ShapeDtypeStruct((1, D), F32)
    return pl.pallas_call(
        body, name="resid_norm_bwd", grid=(s // tm,), in_specs=[row, row, vec, vec, vec, row, row],
        out_specs=[row, row, vec, vec, vec],
        out_shape=[jax.ShapeDtypeStruct((s, D), F32), jax.ShapeDtypeStruct((s, D), BF16), vshape, vshape, vshape],
    )(x1, mo, g1, sc, sh, dh, dx2)


def _loss_head(x1, fo, g2, fw, t):
    s = x1.shape[0]
    tm = _row_tile(s)

    def body(x1_ref, fo_ref, g_ref, fw_ref, t_ref, loss_ref, dx_ref, dfo_ref, dg_ref, dfw_ref):
        loss, (dx1, dfo, dg, dfw) = jax.value_and_grad(_loss_tile, argnums=(0, 1, 2, 3))(
            x1_ref[...], fo_ref[...], g_ref[...], fw_ref[...], t_ref[...])
        dx_ref[...] = dx1
        dfo_ref[...] = dfo.astype(BF16)

        @pl.when(pl.program_id(0) == 0)
        def _():
            loss_ref[...] = jnp.zeros_like(loss_ref)
            dg_ref[...] = jnp.zeros_like(dg_ref)
            dfw_ref[...] = jnp.zeros_like(dfw_ref)

        loss_ref[...] += jnp.full(loss_ref.shape, loss, F32)
        dg_ref[...] += dg
        dfw_ref[...] += dfw

    row = pl.BlockSpec((tm, D), lambda i: (i, 0))
    vec = _full((1, D))
    vshape = jax.ShapeDtypeStruct((1, D), F32)
    return pl.pallas_call(
        body, name="loss_head", grid=(s // tm,), in_specs=[row, row, vec, vec, row],
        out_specs=[_full((8, 128)), row, row, vec, vec],
        out_shape=[jax.ShapeDtypeStruct((8, 128), F32), jax.ShapeDtypeStruct((s, D), F32),
                   jax.ShapeDtypeStruct((s, D), BF16), vshape, vshape],
    )(x1, fo, g2, fw, t)


def _sgu_fwd(proj, norm_w, ws_b, bs_t):
    s = proj.shape[0]
    tm = _pick(s, (256, 128))

    def body(u_ref, v_ref, w_ref, ws_ref, bs_ref, y_ref):
        ug, vn = _sgu_pre(u_ref[...], v_ref[...], w_ref[...])
        vnb = vn.astype(BF16)
        for c in range(tm // Q):
            r = slice(c * Q, (c + 1) * Q)
            for g in range(GM_G):
                cs = slice(g * 128, (g + 1) * 128)
                sv = jnp.dot(ws_ref[g], vnb[r, cs], preferred_element_type=F32) + bs_ref[:, g:g + 1]
                y_ref[r, cs] = (ug[r, cs] * sv).astype(BF16)

    return pl.pallas_call(
        body, name="sgu_fwd", grid=(s // tm,),
        in_specs=[pl.BlockSpec((tm, D), lambda i: (i, 0)), pl.BlockSpec((tm, D), lambda i: (i, 1)),
                  _full((1, D)), _full((GM_G, Q, Q)), _full((Q, GM_G))],
        out_specs=pl.BlockSpec((tm, D), lambda i: (i, 0)),
        out_shape=jax.ShapeDtypeStruct((s, D), BF16),
    )(proj, proj, norm_w, ws_b, bs_t)


def _sgu_bwd(proj, norm_w, ws_b, bs_t, dy, dproj):
    s = proj.shape[0]
    tm = _pick(s, (256, 128))

    def body(u_ref, v_ref, w_ref, ws_ref, bs_ref, dy_ref, _, duv_ref, dw_ref, dws_ref, dbs_ref, dug_scr, dvn_scr):
        @pl.when(pl.program_id(0) == 0)
        def _():
            dw_ref[...] = jnp.zeros_like(dw_ref)
            dws_ref[...] = jnp.zeros_like(dws_ref)
            dbs_ref[...] = jnp.zeros_like(dbs_ref)

        (ug, vn), vjp = jax.vjp(_sgu_pre, u_ref[...], v_ref[...], w_ref[...])
        vnb = vn.astype(BF16)
        dy = dy_ref[...].astype(F32)
        causal = _tril(Q).astype(F32)
        for c in range(tm // Q):
            r = slice(c * Q, (c + 1) * Q)
            for g in range(GM_G):
                cs = slice(g * 128, (g + 1) * 128)
                blk = vnb[r, cs]
                sv = jnp.dot(ws_ref[g], blk, preferred_element_type=F32) + bs_ref[:, g:g + 1]
                dug_scr[r, cs] = dy[r, cs] * sv
                dsv = dy[r, cs] * ug[r, cs]
                dsvb = dsv.astype(BF16)
                dws_ref[g] += causal * lax.dot_general(dsvb, blk, (((1,), (1,)), ((), ())), preferred_element_type=F32)
                dbs_ref[:, g:g + 1] += jnp.sum(dsv, axis=1, keepdims=True)
                dvn_scr[r, cs] = lax.dot_general(ws_ref[g], dsvb, (((0,), (0,)), ((), ())), preferred_element_type=F32)
        du, dv, dw = vjp((dug_scr[...], dvn_scr[...]))
        duv_ref[:, :D] = du.astype(BF16)
        duv_ref[:, D:] = dv.astype(BF16)
        dw_ref[...] += dw

    return pl.pallas_call(
        body, name="sgu_bwd", grid=(s // tm,),
        in_specs=[pl.BlockSpec((tm, D), lambda i: (i, 0)), pl.BlockSpec((tm, D), lambda i: (i, 1)),
                  _full((1, D)), _full((GM_G, Q, Q)), _full((Q, GM_G)), pl.BlockSpec((tm, D), lambda i: (i, 0)), ANY],
        out_specs=[pl.BlockSpec((tm, 2 * D), lambda i: (i, 0)), _full((1, D)), _full((GM_G, Q, Q)), _full((Q, GM_G))],
        out_shape=[jax.ShapeDtypeStruct(dproj.shape, BF16), jax.ShapeDtypeStruct((1, D), F32),
                   jax.ShapeDtypeStruct((GM_G, Q, Q), F32), jax.ShapeDtypeStruct((Q, GM_G), F32)],
        scratch_shapes=[pltpu.VMEM((tm, D), F32), pltpu.VMEM((tm, D), F32)],
        input_output_aliases={6: 0},
    )(proj, proj, norm_w, ws_b, bs_t, dy, dproj)


def _gatenorm_fwd(y_ssd, proj, norm_w):
    s = y_ssd.shape[0]
    tm = _pick(s, (256, 128))
    gw = DI // NGRP

    def body(y_ref, z_ref, w_ref, o_ref):
        for g in range(NGRP):
            cs = slice(g * gw, (g + 1) * gw)
            o_ref[:, cs] = _gatenorm(y_ref[:, cs], z_ref[:, cs], w_ref[:, cs]).astype(BF16)

    return pl.pallas_call(
        body, name="gatenorm_fwd", grid=(s // tm,),
        in_specs=[pl.BlockSpec((tm, DI), lambda i: (i, 0)), pl.BlockSpec((tm, DI), lambda i: (i, OFF_Z // DI)), _full((1, DI))],
        out_specs=pl.BlockSpec((tm, DI), lambda i: (i, 0)),
        out_shape=jax.ShapeDtypeStruct((s, DI), BF16),
    )(y_ssd, proj, norm_w)


def _gatenorm_bwd(y_ssd, proj, norm_w, dyb, dproj):
    s = y_ssd.shape[0]
    tm = _pick(s, (256, 128))
    gw = DI // NGRP

    def body(y_ref, z_ref, w_ref, d_ref, _, dy_ref, dz_ref, dw_ref):
        @pl.when(pl.program_id(0) == 0)
        def _():
            dw_ref[...] = jnp.zeros_like(dw_ref)

        for g in range(NGRP):
            cs = slice(g * gw, (g + 1) * gw)
            _, vjp = jax.vjp(_gatenorm, y_ref[:, cs], z_ref[:, cs], w_ref[:, cs])
            dy, dz, dw = vjp(d_ref[:, cs].astype(F32))
            dy_ref[:, cs] = dy.astype(BF16)
            dz_ref[:, cs] = dz.astype(BF16)
            dw_ref[:, cs] += dw

    blk = pl.BlockSpec((tm, DI), lambda i: (i, 0))
    zblk = pl.BlockSpec((tm, DI), lambda i: (i, OFF_Z // DI))
    return pl.pallas_call(
        body, name="gatenorm_bwd", grid=(s // tm,),
        in_specs=[blk, zblk, _full((1, DI)), blk, ANY],
        out_specs=[blk, zblk, _full((1, DI))],
        out_shape=[jax.ShapeDtypeStruct((s, DI), BF16), jax.ShapeDtypeStruct(dproj.shape, BF16), jax.ShapeDtypeStruct((1, DI), F32)],
        input_output_aliases={4: 1},
    )(y_ssd, proj, norm_w, dyb, dproj)


def _mix_fwd(proj, ba, bb):
    s = proj.shape[0]
    tm = _row_tile(s)
    gb0 = OFF_GA // D

    def body(ga_ref, gb_ref, ba_ref, bb_ref, o_ref):
        o_ref[...] = _mix(ga_ref[...], gb_ref[...], ba_ref[...].astype(F32), bb_ref[...].astype(F32)).astype(BF16)

    row = pl.BlockSpec((tm, D), lambda i: (i, 0))
    return pl.pallas_call(
        body, name="mix_fwd", grid=(s // tm,),
        in_specs=[pl.BlockSpec((tm, D), lambda i: (i, gb0)), pl.BlockSpec((tm, D), lambda i: (i, gb0 + 1)), row, row],
        out_specs=row, out_shape=jax.ShapeDtypeStruct((s, D), BF16),
    )(proj, proj, ba, bb)


def _mix_bwd(proj, ba, bb, dmixed):
    s = proj.shape[0]
    tm = _row_tile(s)
    gb0 = OFF_GA // D

    def body(ga_ref, gb_ref, ba_ref, bb_ref, d_ref, dg_ref, dba_ref, dbb_ref):
        _, vjp = jax.vjp(_mix, ga_ref[...], gb_ref[...], ba_ref[...].astype(F32), bb_ref[...].astype(F32))
        dga, dgb, dba, dbb = vjp(d_ref[...].astype(F32))
        dg_ref[:, :D] = dga.astype(BF16)
        dg_ref[:, D:] = dgb.astype(BF16)
        dba_ref[...] = dba.astype(BF16)
        dbb_ref[...] = dbb.astype(BF16)

    row = pl.BlockSpec((tm, D), lambda i: (i, 0))
    return pl.pallas_call(
        body, name="mix_bwd", grid=(s // tm,),
        in_specs=[pl.BlockSpec((tm, D), lambda i: (i, gb0)), pl.BlockSpec((tm, D), lambda i: (i, gb0 + 1)), row, row, row],
        out_specs=[pl.BlockSpec((tm, 2 * D), lambda i: (i, OFF_GA // (2 * D))), row, row],
        out_shape=[jax.ShapeDtypeStruct((s, W_IN_R), BF16), jax.ShapeDtypeStruct((s, D), BF16), jax.ShapeDtypeStruct((s, D), BF16)],
    )(proj, proj, ba, bb, dmixed)


CONV_TC = 1024


def _conv_fwd(proj, conv_w, conv_b):
    s = proj.shape[0]
    tm = _row_tile(s)
    cb0 = OFF_XBC // CONV_TC

    def body(x_ref, halo_ref, w_ref, b_ref, o_ref):
        halo = jnp.where(pl.program_id(0) > 0, halo_ref[...], 0.0)
        ext = jnp.concatenate([halo, x_ref[...]], axis=0)
        acc = jnp.broadcast_to(b_ref[...], (tm, CONV_TC))
        for k in range(4):
            shifted = ext if k == 3 else pltpu.roll(ext, 3 - k, 0)
            acc = acc + w_ref[k:k + 1, :] * shifted[8:, :]
        o_ref[...] = jax.nn.silu(acc)

    return pl.pallas_call(
        body, name="conv_fwd", grid=(s // tm, CONV // CONV_TC),
        in_specs=[pl.BlockSpec((tm, CONV_TC), lambda i, j: (i, cb0 + j)),
                  pl.BlockSpec((8, CONV_TC), lambda i, j: (jnp.maximum(i * (tm // 8) - 1, 0), cb0 + j)),
                  pl.BlockSpec((4, CONV_TC), lambda i, j: (0, j)), pl.BlockSpec((1, CONV_TC), lambda i, j: (0, j))],
        out_specs=pl.BlockSpec((tm, CONV_TC), lambda i, j: (i, j)),
        out_shape=jax.ShapeDtypeStruct((s, CONV), F32),
    )(proj, proj, conv_w, conv_b)


def _conv_bwd(proj, conv_w, conv_b, dact, dproj, col0, name):
    s, width = dact.shape
    tm = _row_tile(s)
    nt = s // tm
    c0 = col0 // CONV_TC
    cb0 = OFF_XBC // CONV_TC + c0

    def body(x_ref, prev_ref, next_ref, d_ref, dnext_ref, w_ref, b_ref, _, dx_ref, dw_ref, db_ref):
        i = pl.program_id(1)
        prev = jnp.where(i > 0, prev_ref[...], 0.0)
        ext = jnp.concatenate([prev, x_ref[...], next_ref[...]], axis=0)
        dext = jnp.concatenate([d_ref[...], jnp.where(i < nt - 1, dnext_ref[...], 0.0)], axis=0)
        pre = jnp.broadcast_to(b_ref[...], (tm + 8, CONV_TC))
        taps = []
        for k in range(4):
            shifted = (ext if k == 3 else pltpu.roll(ext, 3 - k, 0))[8:, :]
            taps.append(shifted)
            pre = pre + w_ref[k:k + 1, :] * shifted
        sig = jax.nn.sigmoid(pre)
        dpre = dext * (sig * (1.0 + pre * (1.0 - sig)))
        dx = jnp.zeros((tm, CONV_TC), F32)
        for k in range(4):
            shifted = dpre if k == 3 else pltpu.roll(dpre, tm + 8 - (3 - k), 0)
            dx = dx + w_ref[k:k + 1, :] * shifted[:tm, :]
        dx_ref[...] = dx.astype(BF16)

        @pl.when(i == 0)
        def _():
            dw_ref[...] = jnp.zeros_like(dw_ref)
            db_ref[...] = jnp.zeros_like(db_ref)

        dtile = dpre[:tm, :]
        for k in range(4):
            dw_ref[k:k + 1, :] += jnp.sum(dtile * taps[k][:tm, :], axis=0, keepdims=True)
        db_ref[...] += jnp.sum(dtile, axis=0, keepdims=True)

    r8 = tm // 8
    return pl.pallas_call(
        body, name=name, grid=(width // CONV_TC, nt),
        in_specs=[pl.BlockSpec((tm, CONV_TC), lambda j, i: (i, cb0 + j)),
                  pl.BlockSpec((8, CONV_TC), lambda j, i: (jnp.maximum(i * r8 - 1, 0), cb0 + j)),
                  pl.BlockSpec((8, CONV_TC), lambda j, i: (jnp.minimum((i + 1) * r8, nt * r8 - 1), cb0 + j)),
                  pl.BlockSpec((tm, CONV_TC), lambda j, i: (i, j)),
                  pl.BlockSpec((8, CONV_TC), lambda j, i: (jnp.minimum((i + 1) * r8, nt * r8 - 1), j)),
                  pl.BlockSpec((4, CONV_TC), lambda j, i: (0, c0 + j)), pl.BlockSpec((1, CONV_TC), lambda j, i: (0, c0 + j)), ANY],
        out_specs=[pl.BlockSpec((tm, CONV_TC), lambda j, i: (i, cb0 + j)),
                   pl.BlockSpec((4, CONV_TC), lambda j, i: (0, j)), pl.BlockSpec((1, CONV_TC), lambda j, i: (0, j))],
        out_shape=[jax.ShapeDtypeStruct(dproj.shape, BF16), jax.ShapeDtypeStruct((4, width), F32), jax.ShapeDtypeStruct((1, width), F32)],
        input_output_aliases={7: 0},
    )(proj, proj, proj, dact, dact, conv_w, conv_b, dproj)


def _dt_fwd(proj, dtb, alog):
    s = proj.shape[0]
    nc = s // Q

    def body(r_ref, b_ref, a_ref, dt_ref, cum_ref, cumt_ref):
        dt, cum, cum_t = _dt_prep(r_ref[...], b_ref[...], a_ref[...])
        dt_ref[...] = dt
        cum_ref[...] = cum
        cumt_ref[...] = cum_t

    blk = pl.BlockSpec((Q, 128), lambda n: (n, 0))
    return pl.pallas_call(
        body, name="dt_fwd", grid=(nc,),
        in_specs=[pl.BlockSpec((Q, 128), lambda n: (n, OFF_DT // 128)), _full((1, 128)), _full((1, 128))],
        out_specs=[blk, blk, pl.BlockSpec((None, 128, Q), lambda n: (n, 0, 0))],
        out_shape=[jax.ShapeDtypeStruct((s, 128), F32), jax.ShapeDtypeStruct((s, 128), F32), jax.ShapeDtypeStruct((nc, 128, Q), F32)],
    )(proj, dtb, alog)


def _dt_bwd(proj, dtb, alog, ddt, dcum, dcumt, dproj):
    s = proj.shape[0]
    nc = s // Q

    def body(r_ref, b_ref, a_ref, ddt_ref, dcum_ref, dcumt_ref, _, dr_ref, db_ref, da_ref):
        _, vjp = jax.vjp(_dt_prep, r_ref[...], b_ref[...], a_ref[...])
        dr, db, da = vjp((ddt_ref[...], dcum_ref[...], dcumt_ref[...]))
        dr_ref[...] = dr.astype(BF16)

        @pl.when(pl.program_id(0) == 0)
        def _():
            db_ref[...] = jnp.zeros_like(db_ref)
            da_ref[...] = jnp.zeros_like(da_ref)

        db_ref[...] += db
        da_ref[...] += da

    blk = pl.BlockSpec((Q, 128), lambda n: (n, 0))
    pblk = pl.BlockSpec((Q, 128), lambda n: (n, OFF_DT // 128))
    return pl.pallas_call(
        body, name="dt_bwd", grid=(nc,),
        in_specs=[pblk, _full((1, 128)), _full((1, 128)), blk, blk, pl.BlockSpec((None, 128, Q), lambda n: (n, 0, 0)), ANY],
        out_specs=[pblk, _full((1, 128)), _full((1, 128))],
        out_shape=[jax.ShapeDtypeStruct(dproj.shape, BF16), jax.ShapeDtypeStruct((1, 128), F32), jax.ShapeDtypeStruct((1, 128), F32)],
        input_output_aliases={6: 0},
    )(proj, dtb, alog, ddt, dcum, dcumt, dproj)


def _ssd_specs(chunk_of):
    gs = SSD_GPS
    xs = pl.BlockSpec((Q, 256 * gs), lambda n, g: (chunk_of(n), g))
    bm = pl.BlockSpec((Q, 128 * gs), lambda n, g: (chunk_of(n), DI // (128 * gs) + g))
    cm = pl.BlockSpec((Q, 128 * gs), lambda n, g: (chunk_of(n), (DI + D) // (128 * gs) + g))
    per_chunk = pl.BlockSpec((Q, 128), lambda n, g: (chunk_of(n), 0))
    cum_t = pl.BlockSpec((None, 128, Q), lambda n, g: (chunk_of(n), 0, 0))
    state = pl.BlockSpec((None, 256 * gs, 128), lambda n, g: (chunk_of(n), g, 0))
    vec = pl.BlockSpec((1, 128), lambda n, g: (0, 0))
    return xs, bm, cm, per_chunk, cum_t, state, vec


SSD_GPS = 8


def _aligned(v, m):
    return v if isinstance(v, int) else pl.multiple_of(v, m)


def _ssd_fwd(xbc, dt, cum, cum_t, dsk):
    s = xbc.shape[0]
    nc = s // Q
    xs, bm, cm, per_chunk, cumt_spec, state_spec, vec = _ssd_specs(lambda n: n)
    gs = SSD_GPS

    def body(x_ref, b_ref, c_ref, dt_ref, cum_ref, cumt_ref, dsk_ref, y_ref, st_ref, carry):
        n, gstep = pl.program_id(0), (0 if gs == NGRP else pl.program_id(1))
        rows = pl.ds(_aligned(gstep * (256 * gs), 256 * gs), 256 * gs)

        @pl.when(n == 0)
        def _():
            carry[rows, :] = jnp.zeros((256 * gs, 128), F32)

        st_ref[...] = carry[rows, :]
        for k in range(gs):
            xo, so, bo = 256 * k, 256 * k, 128 * k
            y0, y1, n0, n1 = _ssd_group(
                x_ref[:, xo:xo + 128], x_ref[:, xo + 128:xo + 256], st_ref[so:so + 128, :], st_ref[so + 128:so + 256, :],
                b_ref[:, bo:bo + 128], c_ref[:, bo:bo + 128], dt_ref[...], cum_ref[...], cumt_ref[...], dsk_ref[...],
                gstep * gs + k)
            y_ref[:, xo:xo + 128] = y0
            y_ref[:, xo + 128:xo + 256] = y1
            base = gstep * (256 * gs) + so
            carry[pl.ds(_aligned(base, 128), 128), :] = n0
            carry[pl.ds(_aligned(base + 128, 128), 128), :] = n1

    return pl.pallas_call(
        body, name="ssd_fwd", grid=(nc, NGRP // gs),
        in_specs=[xs, bm, cm, per_chunk, per_chunk, cumt_spec, vec],
        out_specs=[xs, state_spec],
        out_shape=[jax.ShapeDtypeStruct((s, DI), F32), jax.ShapeDtypeStruct((nc, DI, 128), F32)],
        scratch_shapes=[pltpu.VMEM((DI, 128), F32)],
    )(xbc, xbc, xbc, dt, cum, cum_t, dsk)


def _ssd_bwd(xbc, dt, cum, cum_t, dsk, states, dy, dy_t, comm=None):
    s = xbc.shape[0]
    nc = s // Q
    xs, bm, cm, per_chunk, cumt_spec, state_spec, vec = _ssd_specs(lambda n: nc - 1 - n)
    gs = SSD_GPS
    n_ci, n_co = (len(comm.ins), len(comm.outs)) if comm is not None else (0, 0)
    steps = (nc, NGRP // gs)

    def body(*refs):
        x_ref, b_ref, c_ref, dt_ref, cum_ref, cumt_ref, dsk_ref, st_ref, dy_ref, dyt_ref = refs[:10]
        ci_refs = refs[10:10 + n_ci]
        dx_ref, db_ref, dc_ref, ddt_ref, dcum_ref, dcumt_ref, ddsk_ref = refs[10 + n_ci:17 + n_ci]
        co_refs = refs[17 + n_ci:17 + n_ci + n_co]
        carry = refs[17 + n_ci + n_co]
        cs_refs = refs[18 + n_ci + n_co:]
        n, gstep = pl.program_id(0), (0 if gs == NGRP else pl.program_id(1))
        rows = pl.ds(_aligned(gstep * (256 * gs), 256 * gs), 256 * gs)
        if comm is not None:
            pl.when((pl.program_id(0) == 0) & (pl.program_id(1) == 0))(lambda: comm.start(ci_refs, co_refs, cs_refs))

        @pl.when(n == 0)
        def _():
            carry[rows, :] = jnp.zeros((256 * gs, 128), F32)

        def zero_skip_sum():
            ddsk_ref[...] = jnp.zeros_like(ddsk_ref)

        def zero_chunk_sums():
            ddt_ref[...] = jnp.zeros_like(ddt_ref)
            dcum_ref[...] = jnp.zeros_like(dcum_ref)
            dcumt_ref[...] = jnp.zeros_like(dcumt_ref)

        if isinstance(gstep, int):
            pl.when(n == 0)(zero_skip_sum)
            zero_chunk_sums()
        else:
            pl.when((n == 0) & (gstep == 0))(zero_skip_sum)
            pl.when(gstep == 0)(zero_chunk_sums)

        for k in range(gs):
            xo, so, bo = 256 * k, 256 * k, 128 * k
            base = gstep * (256 * gs) + so
            lo = pl.ds(_aligned(base, 128), 128)
            hi = pl.ds(_aligned(base + 128, 128), 128)
            dx0, dx1, ds0, ds1, dbm, dcm, ddt, dcum, dcumt, ddsk = _ssd_group_bwd(
                x_ref[:, xo:xo + 128], x_ref[:, xo + 128:xo + 256], st_ref[so:so + 128, :], st_ref[so + 128:so + 256, :],
                b_ref[:, bo:bo + 128], c_ref[:, bo:bo + 128], dt_ref[...], cum_ref[...], cumt_ref[...], dsk_ref[...],
                gstep * gs + k, dy_ref[:, xo:xo + 128].astype(F32), dy_ref[:, xo + 128:xo + 256].astype(F32),
                dyt_ref[xo:xo + 128, :].astype(F32), dyt_ref[xo + 128:xo + 256, :].astype(F32), carry[lo, :], carry[hi, :])
            dx_ref[:, xo:xo + 128] = dx0
            dx_ref[:, xo + 128:xo + 256] = dx1
            db_ref[:, bo:bo + 128] = dbm
            dc_ref[:, bo:bo + 128] = dcm
            ddt_ref[...] += ddt
            dcum_ref[...] += dcum
            dcumt_ref[...] += dcumt
            ddsk_ref[...] += ddsk
            carry[lo, :] = ds0
            carry[hi, :] = ds1

        if comm is not None:
            pl.when((pl.program_id(0) == steps[0] - 1) & (pl.program_id(1) == steps[1] - 1))(
                lambda: comm.finish(ci_refs, co_refs, cs_refs))

    grp_blk = pl.BlockSpec((Q, 128 * gs), lambda n, g: (nc - 1 - n, g))
    extra_in = comm.ins if comm is not None else []
    extra_out = comm.outs if comm is not None else []
    extra_sems = comm.sems if comm is not None else []
    return pl.pallas_call(
        body, name="ssd_bwd", grid=steps,
        in_specs=[xs, bm, cm, per_chunk, per_chunk, cumt_spec, vec, state_spec, xs,
                  pl.BlockSpec((256 * gs, Q), lambda n, g: (g, nc - 1 - n))] + [ANY] * n_ci,
        out_specs=[xs, grp_blk, grp_blk, per_chunk, per_chunk, cumt_spec, vec] + [ANY] * n_co,
        out_shape=[jax.ShapeDtypeStruct((s, DI), F32), jax.ShapeDtypeStruct((s, D), F32), jax.ShapeDtypeStruct((s, D), F32),
                   jax.ShapeDtypeStruct((s, 128), F32), jax.ShapeDtypeStruct((s, 128), F32),
                   jax.ShapeDtypeStruct((nc, 128, Q), F32), jax.ShapeDtypeStruct((1, 128), F32)] + extra_out,
        scratch_shapes=[pltpu.VMEM((DI, 128), F32)] + extra_sems,
    )(xbc, xbc, xbc, dt, cum, cum_t, dsk, states, dy, dy_t, *extra_in)


def _adam_math(w, g, m, v):
    m2 = ADAM_B1 * m + (1.0 - ADAM_B1) * g
    v2 = ADAM_B2 * v + (1.0 - ADAM_B2) * jnp.square(g)
    m_hat = m2 / (1.0 - ADAM_B1 ** ADAM_STEP)
    v_hat = v2 / (1.0 - ADAM_B2 ** ADAM_STEP)
    delta = -ADAM_LR * (m_hat / (jnp.sqrt(v_hat) + ADAM_EPS) + ADAM_WD * w)
    return delta, m2, v2


def _adam(w, g, m, v, name):
    r, c = w.shape
    tr = r if r * c * 4 <= (1 << 20) else _pick(r, (128, 64, 32, 16, 8))

    def body(w_ref, g_ref, m_ref, v_ref, d_ref, m2_ref, v2_ref):
        d, m2, v2 = _adam_math(w_ref[...], g_ref[...], m_ref[...], v_ref[...])
        d_ref[...] = d
        m2_ref[...] = m2
        v2_ref[...] = v2

    blk = pl.BlockSpec((tr, c), lambda i: (i, 0))
    shp = jax.ShapeDtypeStruct((r, c), F32)
    return pl.pallas_call(body, name=name, grid=(r // tr,), in_specs=[blk] * 4, out_specs=[blk] * 3,
                          out_shape=[shp] * 3)(w, g, m, v)


def _sum_leading(xs, name, out_dtype=F32, tr=256):
    if isinstance(xs, tuple):
        a, b = xs
        n, r, c = a.shape
        tr = _pick(r, (tr, 128, 64, 32, 16, 8))

        def body2(a_ref, b_ref, o_ref):
            o_ref[...] = (a_ref[...].astype(F32) + b_ref[...].astype(F32)).astype(out_dtype)

        blk = pl.BlockSpec((None, tr, c), lambda s, i: (s, i, 0))
        return pl.pallas_call(body2, name=name, grid=(n, r // tr), in_specs=[blk, blk], out_specs=blk,
                              out_shape=jax.ShapeDtypeStruct((n, r, c), out_dtype))(a, b)
    n, r, c = xs.shape
    tr = r if n * r * c * 4 <= (8 << 20) else _pick(r, (tr, 128, 64, 32, 16, 8))

    def body(x_ref, o_ref):
        acc = x_ref[0].astype(F32)
        for s in range(1, n):
            acc = acc + x_ref[s].astype(F32)
        o_ref[...] = acc.astype(out_dtype)

    return pl.pallas_call(body, name=name, grid=(r // tr,), in_specs=[pl.BlockSpec((n, tr, c), lambda i: (0, i, 0))],
                          out_specs=pl.BlockSpec((tr, c), lambda i: (i, 0)),
                          out_shape=jax.ShapeDtypeStruct((r, c), out_dtype))(xs)


def _mod_fwd(c8, w_mod, b_sl):
    def body(c_ref, w_ref, b_ref, o_ref):
        o_ref[...] = jnp.dot(jax.nn.silu(c_ref[...]), w_ref[...], precision=HIGHEST, preferred_element_type=F32) + b_ref[...]

    return pl.pallas_call(body, name="mod_fwd", in_specs=[VMEM, VMEM, VMEM], out_specs=VMEM,
                          out_shape=jax.ShapeDtypeStruct((N_DEV, w_mod.shape[1]), F32))(c8, w_mod, b_sl)


def _mod_wgrad(c8, dmod):
    def body(c_ref, d_ref, o_ref):
        o_ref[...] = lax.dot_general(jax.nn.silu(c_ref[...]), d_ref[...], (((0,), (0,)), ((), ())),
                                     precision=HIGHEST, preferred_element_type=F32)

    return pl.pallas_call(body, name="mod_wgrad", in_specs=[VMEM, VMEM], out_specs=VMEM,
                          out_shape=jax.ShapeDtypeStruct((D, dmod.shape[1]), F32))(c8, dmod)


def _place():
    x, y, c = lax.axis_index("x"), lax.axis_index("y"), lax.axis_index("c")
    chips = [(1 - x, y), (x, 1 - y), (1 - x, 1 - y)]
    return x, y, c, chips


def _all_gather_small(v, name):
    r, w = v.shape

    def body(v_ref, o_ref, send_sems, recv_sems, local_sem):
        x, y, c, _ = _place()
        me = 4 * x + 2 * y + c
        own = pltpu.make_async_copy(v_ref, o_ref.at[me], local_sem)
        own.start()
        sends = []
        for k in range(1, N_DEV):
            tx = 1 - x if k & 4 else x
            ty = 1 - y if k & 2 else y
            tc = 1 - c if k & 1 else c
            peer = 4 * tx + 2 * ty + tc
            cp = pltpu.make_async_remote_copy(src_ref=v_ref, dst_ref=o_ref.at[me], send_sem=send_sems.at[k - 1],
                                              recv_sem=recv_sems.at[k - 1], device_id=(tx, ty, tc), device_id_type=MESH)
            cp.start()
            sends.append((cp, peer, (tx, ty, tc)))
        for k, (cp, peer, dev) in enumerate(sends):
            pltpu.make_async_remote_copy(src_ref=v_ref, dst_ref=o_ref.at[peer], send_sem=send_sems.at[k],
                                         recv_sem=recv_sems.at[k], device_id=dev, device_id_type=MESH).wait_recv()
        for cp, _, _ in sends:
            cp.wait_send()
        own.wait()

    return pl.pallas_call(
        body, name=name, in_specs=[VMEM], out_specs=VMEM, out_shape=jax.ShapeDtypeStruct((N_DEV, r, w), v.dtype),
        scratch_shapes=[pltpu.SemaphoreType.DMA((N_DEV - 1,)), pltpu.SemaphoreType.DMA((N_DEV - 1,)), pltpu.SemaphoreType.DMA],
    )(v)


def _all_gather_weights(shards):
    n = len(shards)
    nsem = n * 3 * AG_CHUNKS

    def body(*refs):
        srcs, dsts = refs[:n], refs[n:2 * n]
        send_sems, recv_sems, fwd_send_sems, fwd_recv_sems = refs[2 * n:]
        x, y, c, chips = _place()
        q = 2 * x + y
        sibling = (x, y, 1 - c)
        sends = []
        for r in range(AG_CHUNKS):
            for t in range(n):
                hc = srcs[t].shape[0] // 2 // AG_CHUNKS
                rows = pl.ds(c * (hc * AG_CHUNKS) + r * hc, hc)
                for j, (cx, cy) in enumerate(chips):
                    k = (t * 3 + j) * AG_CHUNKS + r
                    cp = pltpu.make_async_remote_copy(src_ref=srcs[t].at[rows], dst_ref=dsts[t].at[q, rows],
                                                      send_sem=send_sems.at[k], recv_sem=recv_sems.at[k],
                                                      device_id=(cx, cy, c), device_id_type=MESH)
                    cp.start()
                    sends.append(cp)
        fwds = []
        for r in range(AG_CHUNKS):
            for t in range(n):
                hc = srcs[t].shape[0] // 2 // AG_CHUNKS
                sub = hc // D2D_SPLIT
                for j, (cx, cy) in enumerate(chips):
                    k = (t * 3 + j) * AG_CHUNKS + r
                    base = c * (hc * AG_CHUNKS) + r * hc
                    part = dsts[t].at[2 * cx + cy, pl.ds(base, hc)]
                    pltpu.make_async_remote_copy(src_ref=part, dst_ref=part, send_sem=send_sems.at[k], recv_sem=recv_sems.at[k],
                                                 device_id=(cx, cy, c), device_id_type=MESH).wait_recv()
                    for u in range(D2D_SPLIT):
                        piece = dsts[t].at[2 * cx + cy, pl.ds(base + u * sub, sub)]
                        pltpu.make_async_remote_copy(src_ref=piece, dst_ref=piece, send_sem=fwd_send_sems.at[k],
                                                     recv_sem=fwd_recv_sems.at[k], device_id=sibling, device_id_type=MESH).start()
                    fwds.append((part, k))
        for r in range(AG_CHUNKS):
            for t in range(n):
                hc = srcs[t].shape[0] // 2 // AG_CHUNKS
                for j, (cx, cy) in enumerate(chips):
                    k = (t * 3 + j) * AG_CHUNKS + r
                    part = dsts[t].at[2 * cx + cy, pl.ds((1 - c) * (hc * AG_CHUNKS) + r * hc, hc)]
                    pltpu.make_async_remote_copy(src_ref=part, dst_ref=part, send_sem=fwd_send_sems.at[k],
                                                 recv_sem=fwd_recv_sems.at[k], device_id=sibling, device_id_type=MESH).wait_recv()
        for cp in sends:
            cp.wait_send()
        for part, k in fwds:
            pltpu.make_async_remote_copy(src_ref=part, dst_ref=part, send_sem=fwd_send_sems.at[k], recv_sem=fwd_recv_sems.at[k],
                                         device_id=sibling, device_id_type=MESH).wait_send()

    return pl.pallas_call(
        body, name="all_gather_weights", in_specs=[ANY] * n, out_specs=[ANY] * n,
        out_shape=[jax.ShapeDtypeStruct((N_CHIP,) + s.shape, s.dtype) for s in shards],
        scratch_shapes=[pltpu.SemaphoreType.DMA((nsem,)), pltpu.SemaphoreType.DMA((nsem,)), pltpu.SemaphoreType.DMA((nsem,)),
                        pltpu.SemaphoreType.DMA((nsem,))],
    )(*shards)


def _exchange_halves(grads, tag=""):
    n = len(grads)

    def body(*refs):
        srcs, theirs = refs[:n], refs[n:2 * n]
        send_sems, recv_sems = refs[2 * n:]
        x, y, c, _ = _place()
        sibling = (x, y, 1 - c)
        waits = []
        for t in range(n):
            h = srcs[t].shape[1] // 2
            sub = h // D2D_SPLIT
            for s in range(N_CHIP):
                for u in range(D2D_SPLIT):
                    pltpu.make_async_remote_copy(src_ref=srcs[t].at[s, pl.ds((1 - c) * h + u * sub, sub)],
                                                 dst_ref=theirs[t].at[s, pl.ds(u * sub, sub)],
                                                 send_sem=send_sems.at[t], recv_sem=recv_sems.at[t],
                                                 device_id=sibling, device_id_type=MESH).start()
            waits.append(pltpu.make_async_remote_copy(src_ref=srcs[t].at[:, pl.ds((1 - c) * h, h)], dst_ref=theirs[t],
                                                      send_sem=send_sems.at[t], recv_sem=recv_sems.at[t],
                                                      device_id=sibling, device_id_type=MESH))
        for whole in waits:
            whole.wait()

    half = [jax.ShapeDtypeStruct((N_CHIP, g.shape[1] // 2, g.shape[2]), g.dtype) for g in grads]
    return pl.pallas_call(
        body, name=f"exchange_halves_{tag}", in_specs=[ANY] * n, out_specs=[ANY] * n, out_shape=half,
        scratch_shapes=[pltpu.SemaphoreType.DMA((n,)), pltpu.SemaphoreType.DMA((n,))],
    )(*grads)


def _chip_exchange_comm(parts):
    n = len(parts)

    def copies(srcs, dsts, sems):
        send_sems, recv_sems, local_sems = sems
        x, y, c, chips = _place()
        q = 2 * x + y
        owns = [pltpu.make_async_copy(srcs[t].at[q], dsts[t].at[q], local_sems.at[t]) for t in range(n)]
        sends, recvs = [], []
        for t in range(n):
            for j, (cx, cy) in enumerate(chips):
                args = dict(send_sem=send_sems.at[3 * t + j], recv_sem=recv_sems.at[3 * t + j],
                            device_id=(cx, cy, c), device_id_type=MESH)
                sends.append(pltpu.make_async_remote_copy(src_ref=srcs[t].at[2 * cx + cy], dst_ref=dsts[t].at[q], **args))
                part = dsts[t].at[2 * cx + cy]
                recvs.append(pltpu.make_async_remote_copy(src_ref=part, dst_ref=part, **args))
        return owns, sends, recvs

    def start(srcs, dsts, sems):
        owns, sends, _ = copies(srcs, dsts, sems)
        for cp in owns + sends:
            cp.start()

    def finish(srcs, dsts, sems):
        owns, sends, recvs = copies(srcs, dsts, sems)
        for cp in recvs:
            cp.wait_recv()
        for cp in sends:
            cp.wait_send()
        for cp in owns:
            cp.wait()

    return _Comm(parts, [jax.ShapeDtypeStruct(p.shape, p.dtype) for p in parts],
                 [pltpu.SemaphoreType.DMA((3 * n,)), pltpu.SemaphoreType.DMA((3 * n,)), pltpu.SemaphoreType.DMA((n,))], start, finish)


def _gather_relay_comm(shard):
    h = shard.shape[0] // 2
    sub = h // D2D_SPLIT

    def place_and_copies(srcs, dsts, sems):
        send_sems, recv_sems, fwd_send_sems, fwd_recv_sems = sems
        x, y, c, chips = _place()
        q = 2 * x + y
        mine, theirs = pl.ds(c * h, h), pl.ds((1 - c) * h, h)
        sends, recvs, relays, relayed = [], [], [], []
        for j, (cx, cy) in enumerate(chips):
            ici = dict(send_sem=send_sems.at[j], recv_sem=recv_sems.at[j], device_id=(cx, cy, c), device_id_type=MESH)
            d2d = dict(send_sem=fwd_send_sems.at[j], recv_sem=fwd_recv_sems.at[j], device_id=(x, y, 1 - c), device_id_type=MESH)
            sends.append(pltpu.make_async_remote_copy(src_ref=srcs[0].at[mine], dst_ref=dsts[0].at[q, mine], **ici))
            landed = dsts[0].at[2 * cx + cy, mine]
            recvs.append(pltpu.make_async_remote_copy(src_ref=landed, dst_ref=landed, **ici))
            pieces = [dsts[0].at[2 * cx + cy, pl.ds(c * h + u * sub, sub)] for u in range(D2D_SPLIT)]
            relays.append(([pltpu.make_async_remote_copy(src_ref=p, dst_ref=p, **d2d) for p in pieces],
                           pltpu.make_async_remote_copy(src_ref=landed, dst_ref=landed, **d2d)))
            other = dsts[0].at[2 * cx + cy, theirs]
            relayed.append(pltpu.make_async_remote_copy(src_ref=other, dst_ref=other, **d2d))
        return sends, recvs, relays, relayed

    def start(srcs, dsts, sems):
        for cp in place_and_copies(srcs, dsts, sems)[0]:
            cp.start()

    def middle(srcs, dsts, sems):
        _, recvs, relays, _ = place_and_copies(srcs, dsts, sems)
        for cp, (pieces, _) in zip(recvs, relays):
            cp.wait_recv()
            for piece in pieces:
                piece.start()

    def finish(srcs, dsts, sems):
        sends, _, relays, relayed = place_and_copies(srcs, dsts, sems)
        for cp in relayed:
            cp.wait_recv()
        for cp in sends:
            cp.wait_send()
        for _, whole in relays:
            whole.wait_send()

    comm = _Comm([shard], [jax.ShapeDtypeStruct((N_CHIP,) + shard.shape, shard.dtype)],
                 [pltpu.SemaphoreType.DMA((3,))] * 4, start, finish)
    comm.middle = middle
    return comm


def _run_comm(comm, name):
    n_ci, n_co = len(comm.ins), len(comm.outs)

    def body(*refs):
        ci, co, cs = refs[:n_ci], refs[n_ci:n_ci + n_co], refs[n_ci + n_co:]
        comm.start(ci, co, cs)
        if comm.middle is not None:
            comm.middle(ci, co, cs)
        comm.finish(ci, co, cs)

    return pl.pallas_call(body, name=name, in_specs=[ANY] * n_ci, out_specs=[ANY] * n_co, out_shape=comm.outs,
                          scratch_shapes=comm.sems)(*comm.ins)


def _share_halves(halves, tag=""):
    n = len(halves)

    def body(*refs):
        srcs, dsts = refs[:n], refs[n:2 * n]
        send_sems, recv_sems = refs[2 * n:]
        x, y, c, _ = _place()
        sibling = (x, y, 1 - c)
        for t in range(n):
            h = srcs[t].shape[0]
            sub = h // (2 * D2D_SPLIT)
            for u in range(2 * D2D_SPLIT):
                pltpu.make_async_remote_copy(src_ref=srcs[t].at[pl.ds(u * sub, sub)], dst_ref=dsts[t].at[pl.ds(c * h + u * sub, sub)],
                                             send_sem=send_sems.at[t], recv_sem=recv_sems.at[t],
                                             device_id=sibling, device_id_type=MESH).start()
        for t in range(n):
            h = srcs[t].shape[0]
            pltpu.make_async_remote_copy(src_ref=srcs[t], dst_ref=dsts[t].at[pl.ds((1 - c) * h, h)], send_sem=send_sems.at[t],
                                         recv_sem=recv_sems.at[t], device_id=sibling, device_id_type=MESH).wait()

    return pl.pallas_call(
        body, name=f"share_halves_{tag}", in_specs=[ANY] * n, out_specs=[ANY] * n,
        out_shape=[jax.ShapeDtypeStruct((2 * h.shape[0], h.shape[1]), h.dtype) for h in halves],
        scratch_shapes=[pltpu.SemaphoreType.DMA((n,)), pltpu.SemaphoreType.DMA((n,))],
    )(*halves)


def _gather_weights(shards, chip):
    gathered = _all_gather_weights(shards)
    return [lax.dynamic_update_slice(g, s[None], (chip, 0, 0)) for g, s in zip(gathered, shards)]


def _reduce_scatter_begin(sends, core, tag):
    theirs = _exchange_halves(sends, tag)
    mines = [lax.dynamic_slice(g, (0, core * (g.shape[1] // 2), 0), (N_CHIP, g.shape[1] // 2, g.shape[2])) for g in sends]
    pair = [_sum_leading((m, t), f"pair_sum_{tag}{i}", out_dtype=BF16) for i, (m, t) in enumerate(zip(mines, theirs))]
    return _chip_exchange_comm(pair)


def _reduce_scatter_end(contrib, core, tag):
    halves = [_sum_leading(c, f"chip_sum_{tag}{i}") for i, c in enumerate(contrib)]
    shared = _share_halves(halves, tag)
    return [lax.dynamic_update_slice(full, mine, (core * mine.shape[0], 0)) for full, mine in zip(shared, halves)]


def _reduce_scatter(sends, core, tag=""):
    contrib = _run_comm(_reduce_scatter_begin(sends, core, tag), f"exchange_chips_{tag}")
    return _reduce_scatter_end(contrib, core, tag)


def _pack(arrs):
    rows = []
    for a in arrs:
        flat = a.astype(F32).reshape(-1)
        pad = (-flat.shape[0]) % 1024
        rows.append(jnp.pad(flat, (0, pad)).reshape(-1, 128))
    return jnp.concatenate(rows, axis=0)


def _unpack(buf, shapes):
    out, r = [], 0
    for shp in shapes:
        size = 1
        for d in shp:
            size *= d
        nr = (size + 1023) // 1024 * 8
        out.append(buf[r:r + nr].reshape(-1)[:size].reshape(shp))
        r += nr
    return out


R_BG, R_BS, R_OUT, R_FF = D // N_CHIP, DI // N_CHIP, D // N_CHIP, DFF // N_CHIP


def _pack_rest_shard(w_bg, w_bs, w_out, w_ff2, w_ff1):
    return jnp.concatenate([w_bg, w_bs, w_out, w_ff2, w_ff1], axis=0).astype(BF16)


def _unpack_rest(g):
    o1, o2, o3, o4 = R_BG, R_BG + R_BS, R_BG + R_BS + R_OUT, R_BG + R_BS + R_OUT + R_FF
    return (g[:, :o1].reshape(D, D), g[:, o1:o2].reshape(DI, D), g[:, o2:o3].reshape(D, D),
            jnp.transpose(g[:, o4:], (1, 0, 2)).reshape(D, DFF), g[:, o3:o4].reshape(DFF, D))


def _local_step(x, t, mod, w_in_r, rest, gm_norm_w, gm_ws, gm_bs, conv_w, conv_b, dt_bias, a_log, d_skip, ssm_norm_w,
                final_norm_w, place=None):
    sh1, sc1, g1, sh2, sc2, g2 = [mod[:, i * D:(i + 1) * D] for i in range(6)]
    ws_b = jnp.where(jnp.tril(jnp.ones((Q, Q), bool))[None], gm_ws, 0.0).astype(BF16)
    bs_t = gm_bs.T
    pad32 = lambda v: jnp.pad(v, ((0, 0), (0, 128 - NHEAD)))
    dtb, alog, dsk = pad32(dt_bias), pad32(a_log), pad32(d_skip)

    h1 = _prenorm(x, sc1, sh1)
    gdt = (F32,) if place is None else (BF16,)
    if place is None:
        (proj,) = _matmul(h1, w_in_r, "nn", name="mm_proj", tn=1152)
        w_bg, w_bs, w_out, w_ff1, w_ff2 = rest
    else:
        chip, core = place
        proj, g_rest = _matmul(h1, w_in_r, "nn", name="mm_proj", tn=1152, comm=_gather_relay_comm(rest))
        w_bg, w_bs, w_out, w_ff1, w_ff2 = _unpack_rest(lax.dynamic_update_slice(g_rest, rest[None], (chip, 0, 0)))
    y_a = _sgu_fwd(proj, gm_norm_w, ws_b, bs_t)
    xbc = _conv_fwd(proj, conv_w, conv_b)
    dt, cum, cum_t = _dt_fwd(proj, dtb, alog)
    y_ssd, states = _ssd_fwd(xbc, dt, cum, cum_t, dsk)
    y_b = _gatenorm_fwd(y_ssd, proj, ssm_norm_w)
    (ba,) = _matmul(y_a, w_bg, "nn", name="mm_branch_gm", out_dtypes=(BF16,))
    (bb,) = _matmul(y_b, w_bs, "nn", name="mm_branch_ssm", out_dtypes=(BF16,), tm=512, tk=2048)
    mixed = _mix_fwd(proj, ba, bb)
    (mo,) = _matmul(mixed, w_out, "nn", name="mm_out")
    x1, h2 = _resid_norm(x, mo, g1, sc2, sh2)
    f, act = _matmul(h2, w_ff1, "nn", name="mm_ff1", out_dtypes=(BF16, BF16),
                     epi=lambda acc: (acc, jnp.square(jnp.maximum(acc, 0.0))))
    (fo,) = _matmul(act, w_ff2, "nn", name="mm_ff2", tm=512, tk=4096)
    loss8, dx2, dfo, dg2, dfw = _loss_head(x1, fo, g2, final_norm_w, t)

    (df,) = _matmul(dfo, w_ff2, "nt", name="mm_ff2_dx", out_dtypes=(BF16,), epi_ins=(f,),
                    epi=lambda acc, fv: (acc * (2.0 * jnp.maximum(fv.astype(F32), 0.0)),))
    (g_ff2,) = _matmul(act, dfo, "tn", name="mm_ff2_dw", out_dtypes=gdt, tm=512, tk=4096)
    (dh2,) = _matmul(df, w_ff1, "nt", name="mm_ff1_dx", tm=512, tk=4096)
    (g_ff1,) = _matmul(h2, df, "tn", name="mm_ff1_dw", out_dtypes=gdt, tm=512, tk=4096)
    dx1, dmo, dg1, dsc2, dsh2 = _resid_norm_bwd(x1, mo, g1, sc2, sh2, dh2, dx2)
    (dmixed,) = _matmul(dmo, w_out, "nt", name="mm_out_dx", out_dtypes=(BF16,))
    (g_out,) = _matmul(mixed, dmo, "tn", name="mm_out_dw", out_dtypes=gdt, tm=512, tk=4096)
    early = None
    if place is not None:
        stack = jnp.concatenate([g_out.reshape(N_CHIP, R_OUT, D), g_ff2.reshape(N_CHIP, R_FF, D),
                                 jnp.transpose(g_ff1.reshape(D, N_CHIP, D), (1, 0, 2))], axis=1).astype(BF16)
        early = _reduce_scatter_begin([stack], core, "early")
    dproj, dba, dbb = _mix_bwd(proj, ba, bb, dmixed)
    (dy_a,) = _matmul(dba, w_bg, "nt", name="mm_branch_gm_dx", out_dtypes=(BF16,))
    (g_bg,) = _matmul(y_a, dba, "tn", name="mm_branch_gm_dw", out_dtypes=gdt, tm=512, tk=4096)
    (dy_b,) = _matmul(dbb, w_bs, "nt", name="mm_branch_ssm_dx", out_dtypes=(BF16,))
    (g_bs,) = _matmul(y_b, dbb, "tn", name="mm_branch_ssm_dw", out_dtypes=gdt, tm=512, tk=4096)
    dproj, d_gm_norm, d_ws, d_bs_t = _sgu_bwd(proj, gm_norm_w, ws_b, bs_t, dy_a, dproj)
    dy_ssd, dproj, d_ssm_norm = _gatenorm_bwd(y_ssd, proj, ssm_norm_w, dy_b, dproj)
    dxs, dbm, dcm, ddt, dcum, dcum_t, d_dsk, *early_contrib = _ssd_bwd(xbc, dt, cum, cum_t, dsk, states, dy_ssd, dy_ssd.T,
                                                                       comm=early)
    dproj, d_dtb, d_alog = _dt_bwd(proj, dtb, alog, ddt, dcum, dcum_t, dproj)
    dproj, dw_x, db_x = _conv_bwd(proj, conv_w, conv_b, dxs, dproj, 0, "conv_bwd_x")
    dproj, dw_b, db_b = _conv_bwd(proj, conv_w, conv_b, dbm, dproj, DI, "conv_bwd_b")
    dproj, dw_c, db_c = _conv_bwd(proj, conv_w, conv_b, dcm, dproj, DI + D, "conv_bwd_c")
    d_conv_w = jnp.concatenate([dw_x, dw_b, dw_c], axis=1)
    d_conv_b = jnp.concatenate([db_x, db_b, db_c], axis=1)
    (g_in_r,) = _matmul(h1, dproj, "tn", name="mm_proj_dw", out_dtypes=gdt, tm=512, tn=1152, tk=4096)
    late = None
    if place is not None:
        send_in = jnp.transpose(_restore_w_in(g_in_r).reshape(D, N_CHIP, W_IN // N_CHIP), (1, 0, 2)).astype(BF16)
        stack = jnp.concatenate([g_bg.reshape(N_CHIP, R_BG, D), g_bs.reshape(N_CHIP, R_BS, D)], axis=1).astype(BF16)
        late = _reduce_scatter_begin([send_in, stack], core, "late")
    dh1, *late_contrib = _matmul(dproj, w_in_r, "nt", name="mm_proj_dx", tk=3456, comm=late)
    grad_x, dsc1, dsh1 = _prenorm_bwd(x, sc1, sh1, dh1, dx1)

    dmod = jnp.concatenate([dsh1, dsc1, dg1, dsh2, dsc2, dg2], axis=1)
    small = dict(gm_ws=d_ws, gm_norm_w=d_gm_norm, gm_bs=d_bs_t.T, conv_w=d_conv_w, conv_b=d_conv_b,
                 dt_bias=d_dtb[:, :NHEAD], a_log=d_alog[:, :NHEAD], d_skip=d_dsk[:, :NHEAD],
                 ssm_norm_w=d_ssm_norm, final_norm_w=dfw, dmod=dmod, loss=loss8[:1, :1])
    if place is None:
        return grad_x, small, dict(w_in_r=g_in_r, w_bg=g_bg, w_bs=g_bs, w_out=g_out, w_ff1=g_ff1, w_ff2=g_ff2)
    (s_early,) = _reduce_scatter_end(early_contrib, core, "early")
    s_in, s_late = _reduce_scatter_end(late_contrib, core, "late")
    o1, o2 = R_OUT, R_OUT + R_FF
    big = dict(w_in=s_in, w_bg=s_late[:R_BG], w_bs=s_late[R_BG:], w_out=s_early[:o1], w_ff2=s_early[o1:o2], w_ff1=s_early[o2:])
    return grad_x, small, big


SMALL_KEYS = ("gm_ws", "gm_norm_w", "gm_bs", "conv_w", "conv_b", "dt_bias", "a_log", "d_skip", "ssm_norm_w",
              "final_norm_w", "dmod", "loss")
SMALL_SHAPES = ((GM_G, Q, Q), (1, D), (GM_G, Q), (4, CONV), (1, CONV), (1, NHEAD), (1, NHEAD), (1, NHEAD), (1, DI),
                (1, D), (1, 6 * D), (1, 1))


def _reorder_w_in(w_full):
    k = w_full.shape[0]
    return jnp.concatenate([w_full[:, :8192], w_full[:, 8224:], w_full[:, 8192:8224],
                            jnp.zeros((k, W_IN_R - W_IN), w_full.dtype)], axis=1)


def _restore_w_in(g_r):
    return jnp.concatenate([g_r[:, :8192], g_r[:, OFF_DT:OFF_DT + NHEAD], g_r[:, 8192:OFF_DT]], axis=1)


def kernel(x, c, w_mod, b_mod, w_in, gm_norm_w, gm_ws, gm_bs, conv_w, conv_b, dt_bias, a_log, d_skip, ssm_norm_w, w_branch_gm, w_branch_ssm, w_out, w_ff1, w_ff2, final_norm_w, loss_target, m_w_mod, m_b_mod, m_w_in, m_gm_norm_w, m_gm_ws, m_gm_bs, m_conv_w, m_conv_b, m_dt_bias, m_a_log, m_d_skip, m_ssm_norm_w, m_w_branch_gm, m_w_branch_ssm, m_w_out, m_w_ff1, m_w_ff2, m_final_norm_w, v_w_mod, v_b_mod, v_w_in, v_gm_norm_w, v_gm_ws, v_gm_bs, v_conv_w, v_conv_b, v_dt_bias, v_a_log, v_d_skip, v_ssm_norm_w, v_w_branch_gm, v_w_branch_ssm, v_w_out, v_w_ff1, v_w_ff2, v_final_norm_w):
    ax, ay, ac = lax.axis_index("x"), lax.axis_index("y"), lax.axis_index("c")
    chip = 2 * ax + ay
    dev = 2 * chip + ac
    seq = x.shape[1]
    nmod = w_mod.shape[2]

    first = jnp.concatenate([c, conv_w[0], jnp.zeros((3, D), F32)], axis=0)
    first_all = _all_gather_small(first, "all_gather_cond")
    c8 = first_all[:, 0, :]
    conv_w_full = jnp.concatenate([first_all[2 * k, 1:5, :] for k in range(N_CHIP)], axis=1)
    b_sl = lax.dynamic_slice(b_mod, (0, chip * nmod), (1, nmod))
    mod_part = _mod_fwd(c8, w_mod[0], b_sl)
    mod_all = _all_gather_small(mod_part, "all_gather_mod")
    mod = jnp.concatenate([lax.dynamic_slice(mod_all, (2 * k, dev, 0), (1, 1, nmod))[0] for k in range(N_CHIP)], axis=1)

    (g_in,) = _gather_weights([w_in[0].astype(BF16)], chip)
    w_in_r = _reorder_w_in(jnp.transpose(g_in, (1, 0, 2)).reshape(D, W_IN))
    rest = _pack_rest_shard(w_branch_gm[0], w_branch_ssm[0], w_out[0], w_ff2[0], w_ff1[0])

    grad_x, small, big = _local_step(
        x[0], loss_target[0], mod, w_in_r, rest, gm_norm_w, gm_ws[0], gm_bs[0], conv_w_full, conv_b, dt_bias, a_log,
        d_skip, ssm_norm_w, final_norm_w.reshape(1, D), place=(chip, ac))

    small_all = _all_gather_small(_pack([small[k] for k in SMALL_KEYS]), "all_gather_small_grads")
    small_sum = _sum_leading(small_all, "sum_small_grads")
    s_ws, s_gnw, s_bs, s_cw, s_cb, s_dtb, s_alog, s_dsk, s_snw, s_fnw, s_bmod, s_loss = _unpack(small_sum, SMALL_SHAPES)
    dmod_all = jnp.stack([_unpack(small_all[k], SMALL_SHAPES)[10][0] for k in range(N_DEV)], axis=0)
    g_w_mod = _mod_wgrad(c8, lax.dynamic_slice(dmod_all, (0, chip * nmod), (N_DEV, nmod)))
    g_conv_w = lax.dynamic_slice(s_cw, (0, chip * (CONV // N_CHIP)), (4, CONV // N_CHIP))

    g_w_in, g_w_bg, g_w_bs, g_w_out, g_w_ff2, g_w_ff1 = (big[k] for k in ("w_in", "w_bg", "w_bs", "w_out", "w_ff2", "w_ff1"))

    def adam_big(w, g, m, v, name):
        d, m2, v2 = _adam(w.reshape(g.shape), g, m.reshape(g.shape), v.reshape(g.shape), name)
        return g.reshape(w.shape), d.reshape(w.shape), m2.reshape(w.shape), v2.reshape(w.shape)

    res = {}
    res["w_mod"] = adam_big(w_mod, g_w_mod, m_w_mod, v_w_mod, "adam_w_mod")
    res["w_in"] = adam_big(w_in, g_w_in, m_w_in, v_w_in, "adam_w_in")
    res["w_branch_gm"] = adam_big(w_branch_gm, g_w_bg, m_w_branch_gm, v_w_branch_gm, "adam_w_branch_gm")
    res["w_branch_ssm"] = adam_big(w_branch_ssm, g_w_bs, m_w_branch_ssm, v_w_branch_ssm, "adam_w_branch_ssm")
    res["w_out"] = adam_big(w_out, g_w_out, m_w_out, v_w_out, "adam_w_out")
    res["w_ff1"] = adam_big(w_ff1, g_w_ff1, m_w_ff1, v_w_ff1, "adam_w_ff1")
    res["w_ff2"] = adam_big(w_ff2, g_w_ff2, m_w_ff2, v_w_ff2, "adam_w_ff2")

    names = ("b_mod", "gm_norm_w", "gm_ws", "gm_bs", "conv_w", "conv_b", "dt_bias", "a_log", "d_skip", "ssm_norm_w", "final_norm_w")
    ws = (b_mod, gm_norm_w, gm_ws, gm_bs, conv_w, conv_b, dt_bias, a_log, d_skip, ssm_norm_w, final_norm_w)
    ms = (m_b_mod, m_gm_norm_w, m_gm_ws, m_gm_bs, m_conv_w, m_conv_b, m_dt_bias, m_a_log, m_d_skip, m_ssm_norm_w, m_final_norm_w)
    vs = (v_b_mod, v_gm_norm_w, v_gm_ws, v_gm_bs, v_conv_w, v_conv_b, v_dt_bias, v_a_log, v_d_skip, v_ssm_norm_w, v_final_norm_w)
    gs = (s_bmod, s_gnw, s_ws, s_bs, g_conv_w, s_cb, s_dtb, s_alog, s_dsk, s_snw, s_fnw)
    gs = [g.reshape(w.shape) for g, w in zip(gs, ws)]
    shapes = [w.shape for w in ws]
    d_p, m_p, v_p = _adam(_pack(ws), _pack(gs), _pack(ms), _pack(vs), "adam_small")
    for name, g, d, m2, v2 in zip(names, gs, _unpack(d_p, shapes), _unpack(m_p, shapes), _unpack(v_p, shapes)):
        res[name] = (g, d, m2, v2)

    order = ("w_mod", "b_mod", "w_in", "gm_norm_w", "gm_ws", "gm_bs", "conv_w", "conv_b", "dt_bias", "a_log", "d_skip",
             "ssm_norm_w", "w_branch_gm", "w_branch_ssm", "w_out", "w_ff1", "w_ff2", "final_norm_w")
    loss = s_loss.reshape(())
    return (loss, grad_x.reshape(x.shape), *[res[k][0] for k in order], *[res[k][1] for k in order],
            *[res[k][2] for k in order], *[res[k][3] for k in order])
```

```python
import functools

import jax
import jax.numpy as jnp
from jax import lax
from jax.experimental import pallas as pl
from jax.experimental.pallas import tpu as pltpu

F32 = jnp.float32
BF16 = jnp.bfloat16
MESH = pl.DeviceIdType.MESH
HIGHEST = lax.Precision.HIGHEST

D = 1024
EPS = 1e-6
Q = 128
GM_G = 8
NHEAD = 32
NGRP = 8
DI = 2048
CONV = 4096
DFF = 4096
W_IN = 10272
W_IN_R = 10368
OFF_Z, OFF_XBC, OFF_GA, OFF_DT = 2048, 4096, 8192, 10240
N_CHIP = 4
N_DEV = 8
AG_CHUNKS = 4
D2D_SPLIT = 4

ADAM_LR, ADAM_B1, ADAM_B2, ADAM_EPS, ADAM_WD, ADAM_STEP = 0.001, 0.9, 0.999, 1e-08, 0.01, 10

ANY = pl.BlockSpec(memory_space=pl.ANY)
VMEM = pl.BlockSpec(memory_space=pltpu.VMEM)


def _full(shape):
    return pl.BlockSpec(shape, lambda *_: (0,) * len(shape))


def _pick(n, prefs):
    for p in prefs:
        if n % p == 0:
            return p
    return n


def _rms(x):
    return x * lax.rsqrt(jnp.mean(x * x, axis=-1, keepdims=True) + EPS)


def _modnorm(x, sc, sh):
    return _rms(x) * (1.0 + sc) + sh


def _sgu_pre(u, v, w):
    return jax.nn.gelu(u), _rms(jax.nn.gelu(v)) * w


def _gatenorm(y, z, w):
    g = y * jax.nn.silu(z)
    return _rms(g) * w


def _mix(ga, gb, ba, bb):
    return jax.nn.sigmoid(ga) * ba + jax.nn.sigmoid(gb) * bb


def _loss_tile(x1, fo, g2, fw, t):
    x2 = x1 + g2 * fo
    y = _rms(x2) * fw
    err = jnp.square(y - t)
    return 0.5 * jnp.sum(jnp.mean(err, axis=-1))


def _tril(n):
    r = lax.broadcasted_iota(jnp.int32, (n, n), 0)
    c = lax.broadcasted_iota(jnp.int32, (n, n), 1)
    return r >= c


def _dt_prep(dtr, dtb, alog):
    dt = jax.nn.softplus(dtr + dtb)
    a = dt * (-jnp.exp(alog))
    ones = _tril(Q).astype(F32)
    cum = jnp.dot(ones, a, precision=HIGHEST, preferred_element_type=F32)
    cum_t = lax.dot_general(a, ones, (((0,), (1,)), ((), ())), precision=HIGHEST, preferred_element_type=F32)
    return dt, cum, cum_t


def _ssd_group(x0, x1, s0, s1, bm, cm, dt, cum, cum_t, dsk, grp):
    lane = lax.broadcasted_iota(jnp.int32, (1, 128), 1)
    sub = lax.broadcasted_iota(jnp.int32, (128, 1), 0)
    future = jnp.where(_tril(Q), 0.0, -jnp.inf)
    half = lane < 64
    half_rows = sub < 64
    bmb = bm.astype(BF16)
    cmb = cm.astype(BF16)
    cb = lax.dot_general(cmb, bmb, (((1,), (1,)), ((), ())), preferred_element_type=F32)

    def col(v, h):
        return jnp.sum(jnp.where(lane == h, v, 0.0), axis=1, keepdims=True)

    def row(v, h):
        return jnp.sum(jnp.where(sub == h, v, 0.0), axis=0, keepdims=True)

    def last(c):
        return jnp.sum(jnp.where(sub == Q - 1, c, 0.0), axis=0, keepdims=True)

    outs, states = [], []
    for p, (xp, sp) in enumerate(((x0, s0), (x1, s1))):
        h_a = 4 * grp + 2 * p
        h_b = h_a + 1
        dt_a, dt_b = col(dt, h_a), col(dt, h_b)
        cum_a, cum_b = col(cum, h_a), col(cum, h_b)
        row_a, row_b = row(cum_t, h_a), row(cum_t, h_b)
        last_a, last_b = last(cum_a), last(cum_b)
        xdt = xp * jnp.where(half, dt_a, dt_b)
        xdtb = xdt.astype(BF16)
        m_a = (cb * jnp.exp(cum_a - row_a + future)).astype(BF16)
        m_b = (cb * jnp.exp(cum_b - row_b + future)).astype(BF16)
        y_intra = jnp.where(half, jnp.dot(m_a, xdtb, preferred_element_type=F32),
                            jnp.dot(m_b, xdtb, preferred_element_type=F32))
        y_inter = lax.dot_general(cmb, sp.astype(BF16), (((1,), (1,)), ((), ())), preferred_element_type=F32)
        y_inter = y_inter * jnp.where(half, jnp.exp(cum_a), jnp.exp(cum_b))
        w_end = jnp.where(half, jnp.exp(last_a - cum_a), jnp.exp(last_b - cum_b))
        upd = lax.dot_general((xdt * w_end).astype(BF16), bmb, (((0,), (0,)), ((), ())), preferred_element_type=F32)
        states.append(sp * jnp.where(half_rows, jnp.exp(last_a), jnp.exp(last_b)) + upd)
        outs.append(y_intra + y_inter + xp * jnp.where(half, col(dsk, h_a), col(dsk, h_b)))
    return outs[0], outs[1], states[0], states[1]


def _ssd_group_bwd(x0, x1, s0, s1, bm, cm, dt, cum, cum_t, dsk, grp, dy0, dy1, dyt0, dyt1, dn0, dn1):
    nt = (((1,), (1,)), ((), ()))
    lane = lax.broadcasted_iota(jnp.int32, (1, 128), 1)
    sub = lax.broadcasted_iota(jnp.int32, (128, 1), 0)
    r = lax.broadcasted_iota(jnp.int32, (Q, Q), 0)
    c = lax.broadcasted_iota(jnp.int32, (Q, Q), 1)
    future = jnp.where(r >= c, 0.0, -jnp.inf)
    future_t = jnp.where(c >= r, 0.0, -jnp.inf)
    half = lane < 64
    half_rows = sub < 64
    bmb = bm.astype(BF16)
    cmb = cm.astype(BF16)
    cb = lax.dot_general(cmb, bmb, nt, preferred_element_type=F32)
    cbt = lax.dot_general(bmb, cmb, nt, preferred_element_type=F32)

    def col(v, h):
        return jnp.sum(jnp.where(lane == h, v, 0.0), axis=1, keepdims=True)

    def row(v, h):
        return jnp.sum(jnp.where(sub == h, v, 0.0), axis=0, keepdims=True)

    def lane_sums(v, sel):
        hi = v.astype(BF16)
        lo = (v - hi.astype(F32)).astype(BF16)
        return jnp.dot(hi, sel, preferred_element_type=F32) + jnp.dot(lo, sel, preferred_element_type=F32)

    def head_sum(v, mask):
        return jnp.sum(jnp.where(mask, v, 0.0), axis=1, keepdims=True)

    src = lax.broadcasted_iota(jnp.int32, (128, 128), 0)
    dst = lax.broadcasted_iota(jnp.int32, (128, 128), 1)

    dcb = jnp.zeros((Q, Q), F32)
    dcbt = jnp.zeros((Q, Q), F32)
    dbm = jnp.zeros((Q, 128), F32)
    dcm = jnp.zeros((Q, 128), F32)
    ddt = jnp.zeros((Q, 128), F32)
    dcum = jnp.zeros((Q, 128), F32)
    dcumt = jnp.zeros((128, Q), F32)
    ddsk = jnp.zeros((1, 128), F32)
    dlast = jnp.zeros((1, 128), F32)
    dxs, dss = [], []
    for p, (xp, sp, dy, dyt, dns) in enumerate(((x0, s0, dy0, dyt0, dn0), (x1, s1, dy1, dyt1, dn1))):
        h_a = 4 * grp + 2 * p
        h_b = h_a + 1
        dt_a, dt_b = col(dt, h_a), col(dt, h_b)
        cum_a, cum_b = col(cum, h_a), col(cum, h_b)
        row_a, row_b = row(cum_t, h_a), row(cum_t, h_b)
        last_a = jnp.sum(jnp.where(sub == Q - 1, cum_a, 0.0), axis=0, keepdims=True)
        last_b = jnp.sum(jnp.where(sub == Q - 1, cum_b, 0.0), axis=0, keepdims=True)
        dtp = jnp.where(half, dt_a, dt_b)
        xdt = xp * dtp
        xdtb = xdt.astype(BF16)
        l_a, l_b = jnp.exp(cum_a - row_a + future), jnp.exp(cum_b - row_b + future)
        lt_a, lt_b = jnp.exp(row_a - cum_a + future_t), jnp.exp(row_b - cum_b + future_t)
        dya = jnp.where(half, dy, 0.0)
        dya_b, dyb_b = dya.astype(BF16), (dy - dya).astype(BF16)
        dm_a = lax.dot_general(dya_b, xdtb, nt, preferred_element_type=F32)
        dm_b = lax.dot_general(dyb_b, xdtb, nt, preferred_element_type=F32)
        dmt_a = lax.dot_general(xdtb, dya_b, nt, preferred_element_type=F32)
        dmt_b = lax.dot_general(xdtb, dyb_b, nt, preferred_element_type=F32)
        dxdt = (jnp.dot((cbt * lt_a).astype(BF16), dya_b, preferred_element_type=F32)
                + jnp.dot((cbt * lt_b).astype(BF16), dyb_b, preferred_element_type=F32))
        pair_sel = jnp.where(dst == jnp.where(src < 64, h_a, h_b), 1.0, 0.0).astype(BF16)
        to_a = jnp.where(dst == h_a, 1.0, 0.0).astype(BF16)
        to_b = jnp.where(dst == h_b, 1.0, 0.0).astype(BF16)
        dseg_a, dseg_b = dm_a * (cb * l_a), dm_b * (cb * l_b)
        dcum = dcum + lane_sums(dseg_a, to_a) + lane_sums(dseg_b, to_b)
        drow_a = -jnp.sum(dseg_a, axis=0, keepdims=True)
        drow_b = -jnp.sum(dseg_b, axis=0, keepdims=True)
        dcb = dcb + dm_a * l_a + dm_b * l_b
        dcbt = dcbt + dmt_a * lt_a + dmt_b * lt_b
        spb = sp.astype(BF16)
        y0 = lax.dot_general(cmb, spb, nt, preferred_element_type=F32)
        dy0s = dy * jnp.where(half, jnp.exp(cum_a), jnp.exp(cum_b))
        g = dy0s * y0
        dcm = dcm + jnp.dot(dy0s.astype(BF16), spb, preferred_element_type=F32)
        dsp = jnp.dot((dyt * jnp.where(half_rows, jnp.exp(row_a), jnp.exp(row_b))).astype(BF16), cmb, preferred_element_type=F32)
        wp = jnp.where(half, jnp.exp(last_a - cum_a), jnp.exp(last_b - cum_b))
        xw = xdt * wp
        dnsb = dns.astype(BF16)
        dxw = lax.dot_general(bmb, dnsb, nt, preferred_element_type=F32)
        dbm = dbm + jnp.dot(xw.astype(BF16), dnsb, preferred_element_type=F32)
        dxdt = dxdt + dxw * wp
        gw = dxw * xw
        dcum = dcum + lane_sums(g - gw, pair_sel)
        el_a, el_b = jnp.exp(last_a), jnp.exp(last_b)
        dsp = dsp + dns * jnp.where(half_rows, el_a, el_b)
        gs = dns * sp
        gs_all = jnp.sum(gs, axis=0, keepdims=True)
        gs_a = jnp.sum(jnp.sum(jnp.where(half_rows, gs, 0.0), axis=0, keepdims=True), axis=1, keepdims=True)
        gs_b = jnp.sum(gs_all, axis=1, keepdims=True) - gs_a
        gw_cols = jnp.sum(gw, axis=0, keepdims=True)
        dlast_a = head_sum(gw_cols, half) + el_a * gs_a
        dlast_b = head_sum(gw_cols, ~half) + el_b * gs_b
        dlast = dlast + jnp.where(lane == h_a, dlast_a, 0.0) + jnp.where(lane == h_b, dlast_b, 0.0)
        ddt = ddt + lane_sums(dxdt * xp, pair_sel)
        gk_cols = jnp.sum(dy * xp, axis=0, keepdims=True)
        dxs.append(dxdt * dtp + dy * jnp.where(half, col(dsk, h_a), col(dsk, h_b)))
        dss.append(dsp)
        dcumt = dcumt + jnp.where(sub == h_a, drow_a, 0.0) + jnp.where(sub == h_b, drow_b, 0.0)
        ddsk = ddsk + jnp.where(lane == h_a, head_sum(gk_cols, half), 0.0) + jnp.where(lane == h_b, head_sum(gk_cols, ~half), 0.0)
    dcum = dcum + jnp.where(sub == Q - 1, dlast, 0.0)
    dcm = dcm + jnp.dot(dcb.astype(BF16), bmb, preferred_element_type=F32)
    dbm = dbm + jnp.dot(dcbt.astype(BF16), cmb, preferred_element_type=F32)
    return dxs[0], dxs[1], dss[0], dss[1], dbm, dcm, ddt, dcum, dcumt, ddsk


class _Comm:
    def __init__(self, ins, outs, sems, start, finish):
        self.ins, self.outs, self.sems, self.start, self.finish = list(ins), list(outs), list(sems), start, finish
        self.middle = None


def _matmul(a, b, mode, *, name, out_dtypes=(F32,), epi=None, epi_ins=(), tm=1024, tn=1024, tk=1024, comm=None):
    if mode == "nn":
        (m, k), n = a.shape, b.shape[1]
    elif mode == "nt":
        (m, k), n = a.shape, b.shape[0]
    else:
        (k, m), n = a.shape, b.shape[1]
    tm, tn, tk = _pick(m, (tm, 512, 256, 128)), _pick(n, (tn, 1152, 1024, 512, 384, 256, 128)), _pick(k, (tk, 1152, 1024, 512, 256, 128))
    nk = k // tk
    if mode == "nn":
        a_spec = pl.BlockSpec((tm, tk), lambda i, j, kk: (i, kk))
        b_spec = pl.BlockSpec((tk, tn), lambda i, j, kk: (kk, j))
        dims = (((1,), (0,)), ((), ()))
    elif mode == "nt":
        a_spec = pl.BlockSpec((tm, tk), lambda i, j, kk: (i, kk))
        b_spec = pl.BlockSpec((tn, tk), lambda i, j, kk: (j, kk))
        dims = (((1,), (1,)), ((), ()))
    else:
        a_spec = pl.BlockSpec((tk, tm), lambda i, j, kk: (kk, i))
        b_spec = pl.BlockSpec((tk, tn), lambda i, j, kk: (kk, j))
        dims = (((0,), (0,)), ((), ()))
    o_spec = pl.BlockSpec((tm, tn), lambda i, j, kk: (i, j))
    n_epi, n_out = len(epi_ins), len(out_dtypes)
    n_ci, n_co, n_cs = (len(comm.ins), len(comm.outs), len(comm.sems)) if comm is not None else (0, 0, 0)
    grid = (m // tm, n // tn, nk)

    def body(*refs):
        a_ref, b_ref = refs[0], refs[1]
        e_refs = refs[2:2 + n_epi]
        ci_refs = refs[2 + n_epi:2 + n_epi + n_ci]
        o_refs = refs[2 + n_epi + n_ci:2 + n_epi + n_ci + n_out]
        co_refs = refs[2 + n_epi + n_ci + n_out:2 + n_epi + n_ci + n_out + n_co]
        acc_ref = refs[2 + n_epi + n_ci + n_out + n_co]
        cs_refs = refs[3 + n_epi + n_ci + n_out + n_co:]
        if comm is not None:
            ids = [pl.program_id(d) for d in range(3)]
            pl.when((ids[0] == 0) & (ids[1] == 0) & (ids[2] == 0))(lambda: comm.start(ci_refs, co_refs, cs_refs))
            if comm.middle is not None:
                pl.when((ids[0] == (5 * grid[0]) // 8) & (ids[1] == 0) & (ids[2] == 0))(
                    lambda: comm.middle(ci_refs, co_refs, cs_refs))

        def finish(acc):
            outs = epi(acc, *[e[...] for e in e_refs]) if epi is not None else (acc,)
            for o_ref, val in zip(o_refs, outs):
                o_ref[...] = val.astype(o_ref.dtype)

        part = lax.dot_general(a_ref[...], b_ref[...], dims, preferred_element_type=F32)
        if nk == 1:
            finish(part)
        else:
            kk = pl.program_id(2)

            @pl.when(kk == 0)
            def _():
                acc_ref[...] = part

            @pl.when(kk > 0)
            def _():
                acc_ref[...] += part

            @pl.when(kk == nk - 1)
            def _():
                finish(acc_ref[...])

        if comm is not None:
            pl.when((ids[0] == grid[0] - 1) & (ids[1] == grid[1] - 1) & (ids[2] == grid[2] - 1))(
                lambda: comm.finish(ci_refs, co_refs, cs_refs))

    extra_in = comm.ins if comm is not None else []
    extra_out = comm.outs if comm is not None else []
    extra_sems = comm.sems if comm is not None else []
    return pl.pallas_call(
        body, name=name, grid=grid,
        in_specs=[a_spec, b_spec] + [o_spec] * n_epi + [ANY] * n_ci,
        out_specs=[o_spec] * n_out + [ANY] * n_co,
        out_shape=[jax.ShapeDtypeStruct((m, n), dt) for dt in out_dtypes] + extra_out,
        scratch_shapes=[pltpu.VMEM((tm, tn) if nk > 1 else (8, 128), F32)] + extra_sems,
        compiler_params=pltpu.CompilerParams(
            dimension_semantics=("arbitrary",) * 3 if comm is not None else ("parallel", "parallel", "arbitrary")),
    )(a, b, *epi_ins, *extra_in)


def _row_tile(s):
    return _pick(s, (512, 256, 128))


def _prenorm(x, sc, sh):
    s = x.shape[0]
    tm = _row_tile(s)

    def body(x_ref, sc_ref, sh_ref, h_ref):
        h_ref[...] = _modnorm(x_ref[...], sc_ref[...], sh_ref[...]).astype(BF16)

    row = pl.BlockSpec((tm, D), lambda i: (i, 0))
    return pl.pallas_call(body, name="prenorm", grid=(s // tm,), in_specs=[row, _full((1, D)), _full((1, D))],
                          out_specs=row, out_shape=jax.ShapeDtypeStruct((s, D), BF16))(x, sc, sh)


def _prenorm_bwd(x, sc, sh, dh, dx_res):
    s = x.shape[0]
    tm = _row_tile(s)

    def body(x_ref, sc_ref, sh_ref, dh_ref, dr_ref, dx_ref, dsc_ref, dsh_ref):
        _, vjp = jax.vjp(_modnorm, x_ref[...], sc_ref[...], sh_ref[...])
        dx, dsc, dsh = vjp(dh_ref[...])
        dx_ref[...] = dr_ref[...] + dx

        @pl.when(pl.program_id(0) == 0)
        def _():
            dsc_ref[...] = jnp.zeros_like(dsc_ref)
            dsh_ref[...] = jnp.zeros_like(dsh_ref)

        dsc_ref[...] += dsc
        dsh_ref[...] += dsh

    row = pl.BlockSpec((tm, D), lambda i: (i, 0))
    vec = _full((1, D))
    return pl.pallas_call(
        body, name="prenorm_bwd", grid=(s // tm,), in_specs=[row, vec, vec, row, row], out_specs=[row, vec, vec],
        out_shape=[jax.ShapeDtypeStruct((s, D), F32), jax.ShapeDtypeStruct((1, D), F32), jax.ShapeDtypeStruct((1, D), F32)],
    )(x, sc, sh, dh, dx_res)


def _resid_norm(x, mo, g1, sc, sh):
    s = x.shape[0]
    tm = _row_tile(s)

    def body(x_ref, mo_ref, g_ref, sc_ref, sh_ref, x1_ref, h_ref):
        x1 = x_ref[...] + g_ref[...] * mo_ref[...]
        x1_ref[...] = x1
        h_ref[...] = _modnorm(x1, sc_ref[...], sh_ref[...]).astype(BF16)

    row = pl.BlockSpec((tm, D), lambda i: (i, 0))
    vec = _full((1, D))
    return pl.pallas_call(
        body, name="resid_norm", grid=(s // tm,), in_specs=[row, row, vec, vec, vec], out_specs=[row, row],
        out_shape=[jax.ShapeDtypeStruct((s, D), F32), jax.ShapeDtypeStruct((s, D), BF16)],
    )(x, mo, g1, sc, sh)


def _resid_norm_bwd(x1, mo, g1, sc, sh, dh, dx2):
    s = x1.shape[0]
    tm = _row_tile(s)

    def body(x1_ref, mo_ref, g_ref, sc_ref, sh_ref, dh_ref, dx2_ref, dx1_ref, dmo_ref, dg_ref, dsc_ref, dsh_ref):
        _, vjp = jax.vjp(_modnorm, x1_ref[...], sc_ref[...], sh_ref[...])
        dx, dsc, dsh = vjp(dh_ref[...])
        dx1 = dx2_ref[...] + dx
        dx1_ref[...] = dx1
        dmo_ref[...] = (dx1 * g_ref[...]).astype(BF16)

        @pl.when(pl.program_id(0) == 0)
        def _():
            dg_ref[...] = jnp.zeros_like(dg_ref)
            dsc_ref[...] = jnp.zeros_like(dsc_ref)
            dsh_ref[...] = jnp.zeros_like(dsh_ref)

        dg_ref[...] += jnp.sum(dx1 * mo_ref[...], axis=0, keepdims=True)
        dsc_ref[...] += dsc
        dsh_ref[...] += dsh

    row = pl.BlockSpec((tm, D), lambda i: (i, 0))
    vec = _full((1, D))
    vshape = jax.ShapeDtypeStruct((1, D), F32)
    return pl.pallas_call(
        body, name="resid_norm_bwd", grid=(s // tm,), in_specs=[row, row, vec, vec, vec, row, row],
        out_specs=[row, row, vec, vec, vec],
        out_shape=[jax.ShapeDtypeStruct((s, D), F32), jax.ShapeDtypeStruct((s, D), BF16), vshape, vshape, vshape],
    )(x1, mo, g1, sc, sh, dh, dx2)


def _loss_head(x1, fo, g2, fw, t):
    s = x1.shape[0]
    tm = _row_tile(s)

    def body(x1_ref, fo_ref, g_ref, fw_ref, t_ref, loss_ref, dx_ref, dfo_ref, dg_ref, dfw_ref):
        loss, (dx1, dfo, dg, dfw) = jax.value_and_grad(_loss_tile, argnums=(0, 1, 2, 3))(
            x1_ref[...], fo_ref[...], g_ref[...], fw_ref[...], t_ref[...])
        dx_ref[...] = dx1
        dfo_ref[...] = dfo.astype(BF16)

        @pl.when(pl.program_id(0) == 0)
        def _():
            loss_ref[...] = jnp.zeros_like(loss_ref)
            dg_ref[...] = jnp.zeros_like(dg_ref)
            dfw_ref[...] = jnp.zeros_like(dfw_ref)

        loss_ref[...] += jnp.full(loss_ref.shape, loss, F32)
        dg_ref[...] += dg
        dfw_ref[...] += dfw

    row = pl.BlockSpec((tm, D), lambda i: (i, 0))
    vec = _full((1, D))
    vshape = jax.ShapeDtypeStruct((1, D), F32)
    return pl.pallas_call(
        body, name="loss_head", grid=(s // tm,), in_specs=[row, row, vec, vec, row],
        out_specs=[_full((8, 128)), row, row, vec, vec],
        out_shape=[jax.ShapeDtypeStruct((8, 128), F32), jax.ShapeDtypeStruct((s, D), F32),
                   jax.ShapeDtypeStruct((s, D), BF16), vshape, vshape],
    )(x1, fo, g2, fw, t)


def _sgu_fwd(proj, norm_w, ws_b, bs_t):
    s = proj.shape[0]
    tm = _pick(s, (256, 128))

    def body(u_ref, v_ref, w_ref, ws_ref, bs_ref, y_ref):
        ug, vn = _sgu_pre(u_ref[...], v_ref[...], w_ref[...])
        vnb = vn.astype(BF16)
        for c in range(tm // Q):
            r = slice(c * Q, (c + 1) * Q)
            for g in range(GM_G):
                cs = slice(g * 128, (g + 1) * 128)
                sv = jnp.dot(ws_ref[g], vnb[r, cs], preferred_element_type=F32) + bs_ref[:, g:g + 1]
                y_ref[r, cs] = (ug[r, cs] * sv).astype(BF16)

    return pl.pallas_call(
        body, name="sgu_fwd", grid=(s // tm,),
        in_specs=[pl.BlockSpec((tm, D), lambda i: (i, 0)), pl.BlockSpec((tm, D), lambda i: (i, 1)),
                  _full((1, D)), _full((GM_G, Q, Q)), _full((Q, GM_G))],
        out_specs=pl.BlockSpec((tm, D), lambda i: (i, 0)),
        out_shape=jax.ShapeDtypeStruct((s, D), BF16),
    )(proj, proj, norm_w, ws_b, bs_t)


def _sgu_bwd(proj, norm_w, ws_b, bs_t, dy, dproj):
    s = proj.shape[0]
    tm = _pick(s, (256, 128))

    def body(u_ref, v_ref, w_ref, ws_ref, bs_ref, dy_ref, _, duv_ref, dw_ref, dws_ref, dbs_ref, dug_scr, dvn_scr):
        @pl.when(pl.program_id(0) == 0)
        def _():
            dw_ref[...] = jnp.zeros_like(dw_ref)
            dws_ref[...] = jnp.zeros_like(dws_ref)
            dbs_ref[...] = jnp.zeros_like(dbs_ref)

        (ug, vn), vjp = jax.vjp(_sgu_pre, u_ref[...], v_ref[...], w_ref[...])
        vnb = vn.astype(BF16)
        dy = dy_ref[...].astype(F32)
        causal = _tril(Q).astype(F32)
        for c in range(tm // Q):
            r = slice(c * Q, (c + 1) * Q)
            for g in range(GM_G):
                cs = slice(g * 128, (g + 1) * 128)
                blk = vnb[r, cs]
                sv = jnp.dot(ws_ref[g], blk, preferred_element_type=F32) + bs_ref[:, g:g + 1]
                dug_scr[r, cs] = dy[r, cs] * sv
                dsv = dy[r, cs] * ug[r, cs]
                dsvb = dsv.astype(BF16)
                dws_ref[g] += causal * lax.dot_general(dsvb, blk, (((1,), (1,)), ((), ())), preferred_element_type=F32)
                dbs_ref[:, g:g + 1] += jnp.sum(dsv, axis=1, keepdims=True)
                dvn_scr[r, cs] = lax.dot_general(ws_ref[g], dsvb, (((0,), (0,)), ((), ())), preferred_element_type=F32)
        du, dv, dw = vjp((dug_scr[...], dvn_scr[...]))
        duv_ref[:, :D] = du.astype(BF16)
        duv_ref[:, D:] = dv.astype(BF16)
        dw_ref[...] += dw

    return pl.pallas_call(
        body, name="sgu_bwd", grid=(s // tm,),
        in_specs=[pl.BlockSpec((tm, D), lambda i: (i, 0)), pl.BlockSpec((tm, D), lambda i: (i, 1)),
                  _full((1, D)), _full((GM_G, Q, Q)), _full((Q, GM_G)), pl.BlockSpec((tm, D), lambda i: (i, 0)), ANY],
        out_specs=[pl.BlockSpec((tm, 2 * D), lambda i: (i, 0)), _full((1, D)), _full((GM_G, Q, Q)), _full((Q, GM_G))],
        out_shape=[jax.ShapeDtypeStruct(dproj.shape, BF16), jax.ShapeDtypeStruct((1, D), F32),
                   jax.ShapeDtypeStruct((GM_G, Q, Q), F32), jax.ShapeDtypeStruct((Q, GM_G), F32)],
        scratch_shapes=[pltpu.VMEM((tm, D), F32), pltpu.VMEM((tm, D), F32)],
        input_output_aliases={6: 0},
    )(proj, proj, norm_w, ws_b, bs_t, dy, dproj)


def _gatenorm_fwd(y_ssd, proj, norm_w):
    s = y_ssd.shape[0]
    tm = _pick(s, (256, 128))
    gw = DI // NGRP

    def body(y_ref, z_ref, w_ref, o_ref):
        for g in range(NGRP):
            cs = slice(g * gw, (g + 1) * gw)
            o_ref[:, cs] = _gatenorm(y_ref[:, cs], z_ref[:, cs], w_ref[:, cs]).astype(BF16)

    return pl.pallas_call(
        body, name="gatenorm_fwd", grid=(s // tm,),
        in_specs=[pl.BlockSpec((tm, DI), lambda i: (i, 0)), pl.BlockSpec((tm, DI), lambda i: (i, OFF_Z // DI)), _full((1, DI))],
        out_specs=pl.BlockSpec((tm, DI), lambda i: (i, 0)),
        out_shape=jax.ShapeDtypeStruct((s, DI), BF16),
    )(y_ssd, proj, norm_w)


def _gatenorm_bwd(y_ssd, proj, norm_w, dyb, dproj):
    s = y_ssd.shape[0]
    tm = _pick(s, (256, 128))
    gw = DI // NGRP

    def body(y_ref, z_ref, w_ref, d_ref, _, dy_ref, dyt_ref, dz_ref, dw_ref):
        @pl.when(pl.program_id(0) == 0)
        def _():
            dw_ref[...] = jnp.zeros_like(dw_ref)

        for g in range(NGRP):
            cs = slice(g * gw, (g + 1) * gw)
            _, vjp = jax.vjp(_gatenorm, y_ref[:, cs], z_ref[:, cs], w_ref[:, cs])
            dy, dz, dw = vjp(d_ref[:, cs].astype(F32))
            dy_ref[:, cs] = dy.astype(BF16)
            dyt_ref[cs, :] = dy.T.astype(BF16)
            dz_ref[:, cs] = dz.astype(BF16)
            dw_ref[:, cs] += dw

    blk = pl.BlockSpec((tm, DI), lambda i: (i, 0))
    zblk = pl.BlockSpec((tm, DI), lambda i: (i, OFF_Z // DI))
    return pl.pallas_call(
        body, name="gatenorm_bwd", grid=(s // tm,),
        in_specs=[blk, zblk, _full((1, DI)), blk, ANY],
        out_specs=[blk, pl.BlockSpec((DI, tm), lambda i: (0, i)), zblk, _full((1, DI))],
        out_shape=[jax.ShapeDtypeStruct((s, DI), BF16), jax.ShapeDtypeStruct((DI, s), BF16),
                   jax.ShapeDtypeStruct(dproj.shape, BF16), jax.ShapeDtypeStruct((1, DI), F32)],
        input_output_aliases={4: 2},
    )(y_ssd, proj, norm_w, dyb, dproj)


def _mix_fwd(proj, ba, bb):
    s = proj.shape[0]
    tm = _row_tile(s)
    gb0 = OFF_GA // D

    def body(ga_ref, gb_ref, ba_ref, bb_ref, o_ref):
        o_ref[...] = _mix(ga_ref[...], gb_ref[...], ba_ref[...].astype(F32), bb_ref[...].astype(F32)).astype(BF16)

    row = pl.BlockSpec((tm, D), lambda i: (i, 0))
    return pl.pallas_call(
        body, name="mix_fwd", grid=(s // tm,),
        in_specs=[pl.BlockSpec((tm, D), lambda i: (i, gb0)), pl.BlockSpec((tm, D), lambda i: (i, gb0 + 1)), row, row],
        out_specs=row, out_shape=jax.ShapeDtypeStruct((s, D), BF16),
    )(proj, proj, ba, bb)


def _mix_bwd(proj, ba, bb, dmixed):
    s = proj.shape[0]
    tm = _row_tile(s)
    gb0 = OFF_GA // D

    def body(ga_ref, gb_ref, ba_ref, bb_ref, d_ref, dg_ref, dba_ref, dbb_ref):
        _, vjp = jax.vjp(_mix, ga_ref[...], gb_ref[...], ba_ref[...].astype(F32), bb_ref[...].astype(F32))
        dga, dgb, dba, dbb = vjp(d_ref[...].astype(F32))
        dg_ref[:, :D] = dga.astype(BF16)
        dg_ref[:, D:] = dgb.astype(BF16)
        dba_ref[...] = dba.astype(BF16)
        dbb_ref[...] = dbb.astype(BF16)

    row = pl.BlockSpec((tm, D), lambda i: (i, 0))
    return pl.pallas_call(
        body, name="mix_bwd", grid=(s // tm,),
        in_specs=[pl.BlockSpec((tm, D), lambda i: (i, gb0)), pl.BlockSpec((tm, D), lambda i: (i, gb0 + 1)), row, row, row],
        out_specs=[pl.BlockSpec((tm, 2 * D), lambda i: (i, OFF_GA // (2 * D))), row, row],
        out_shape=[jax.ShapeDtypeStruct((s, W_IN_R), BF16), jax.ShapeDtypeStruct((s, D), BF16), jax.ShapeDtypeStruct((s, D), BF16)],
    )(proj, proj, ba, bb, dmixed)


CONV_TC = 1024


def _conv_fwd(proj, conv_w, conv_b):
    s = proj.shape[0]
    tm = _row_tile(s)
    cb0 = OFF_XBC // CONV_TC

    def body(x_ref, halo_ref, w_ref, b_ref, o_ref):
        halo = jnp.where(pl.program_id(0) > 0, halo_ref[...], 0.0)
        ext = jnp.concatenate([halo, x_ref[...]], axis=0)
        acc = jnp.broadcast_to(b_ref[...], (tm, CONV_TC))
        for k in range(4):
            shifted = ext if k == 3 else pltpu.roll(ext, 3 - k, 0)
            acc = acc + w_ref[k:k + 1, :] * shifted[8:, :]
        o_ref[...] = jax.nn.silu(acc)

    return pl.pallas_call(
        body, name="conv_fwd", grid=(s // tm, CONV // CONV_TC),
        in_specs=[pl.BlockSpec((tm, CONV_TC), lambda i, j: (i, cb0 + j)),
                  pl.BlockSpec((8, CONV_TC), lambda i, j: (jnp.maximum(i * (tm // 8) - 1, 0), cb0 + j)),
                  pl.BlockSpec((4, CONV_TC), lambda i, j: (0, j)), pl.BlockSpec((1, CONV_TC), lambda i, j: (0, j))],
        out_specs=pl.BlockSpec((tm, CONV_TC), lambda i, j: (i, j)),
        out_shape=jax.ShapeDtypeStruct((s, CONV), F32),
    )(proj, proj, conv_w, conv_b)


def _conv_bwd(proj, conv_w, conv_b, dact, dproj, col0, name):
    s, width = dact.shape
    tm = _row_tile(s)
    nt = s // tm
    c0 = col0 // CONV_TC
    cb0 = OFF_XBC // CONV_TC + c0

    def body(x_ref, prev_ref, next_ref, d_ref, dnext_ref, w_ref, b_ref, _, dx_ref, dw_ref, db_ref):
        i = pl.program_id(1)
        prev = jnp.where(i > 0, prev_ref[...], 0.0)
        ext = jnp.concatenate([prev, x_ref[...], next_ref[...]], axis=0)
        dext = jnp.concatenate([d_ref[...], jnp.where(i < nt - 1, dnext_ref[...], 0.0)], axis=0)
        pre = jnp.broadcast_to(b_ref[...], (tm + 8, CONV_TC))
        taps = []
        for k in range(4):
            shifted = (ext if k == 3 else pltpu.roll(ext, 3 - k, 0))[8:, :]
            taps.append(shifted)
            pre = pre + w_ref[k:k + 1, :] * shifted
        sig = jax.nn.sigmoid(pre)
        dpre = dext * (sig * (1.0 + pre * (1.0 - sig)))
        dx = jnp.zeros((tm, CONV_TC), F32)
        for k in range(4):
            shifted = dpre if k == 3 else pltpu.roll(dpre, tm + 8 - (3 - k), 0)
            dx = dx + w_ref[k:k + 1, :] * shifted[:tm, :]
        dx_ref[...] = dx.astype(BF16)

        @pl.when(i == 0)
        def _():
            dw_ref[...] = jnp.zeros_like(dw_ref)
            db_ref[...] = jnp.zeros_like(db_ref)

        dtile = dpre[:tm, :]
        for k in range(4):
            dw_ref[k:k + 1, :] += jnp.sum(dtile * taps[k][:tm, :], axis=0, keepdims=True)
        db_ref[...] += jnp.sum(dtile, axis=0, keepdims=True)

    r8 = tm // 8
    return pl.pallas_call(
        body, name=name, grid=(width // CONV_TC, nt),
        in_specs=[pl.BlockSpec((tm, CONV_TC), lambda j, i: (i, cb0 + j)),
                  pl.BlockSpec((8, CONV_TC), lambda j, i: (jnp.maximum(i * r8 - 1, 0), cb0 + j)),
                  pl.BlockSpec((8, CONV_TC), lambda j, i: (jnp.minimum((i + 1) * r8, nt * r8 - 1), cb0 + j)),
                  pl.BlockSpec((tm, CONV_TC), lambda j, i: (i, j)),
                  pl.BlockSpec((8, CONV_TC), lambda j, i: (jnp.minimum((i + 1) * r8, nt * r8 - 1), j)),
                  pl.BlockSpec((4, CONV_TC), lambda j, i: (0, c0 + j)), pl.BlockSpec((1, CONV_TC), lambda j, i: (0, c0 + j)), ANY],
        out_specs=[pl.BlockSpec((tm, CONV_TC), lambda j, i: (i, cb0 + j)),
                   pl.BlockSpec((4, CONV_TC), lambda j, i: (0, j)), pl.BlockSpec((1, CONV_TC), lambda j, i: (0, j))],
        out_shape=[jax.ShapeDtypeStruct(dproj.shape, BF16), jax.ShapeDtypeStruct((4, width), F32), jax.ShapeDtypeStruct((1, width), F32)],
        input_output_aliases={7: 0},
    )(proj, proj, proj, dact, dact, conv_w, conv_b, dproj)


def _dt_fwd(proj, dtb, alog):
    s = proj.shape[0]
    nc = s // Q

    def body(r_ref, b_ref, a_ref, dt_ref, cum_ref, cumt_ref):
        dt, cum, cum_t = _dt_prep(r_ref[...], b_ref[...], a_ref[...])
        dt_ref[...] = dt
        cum_ref[...] = cum
        cumt_ref[...] = cum_t

    blk = pl.BlockSpec((Q, 128), lambda n: (n, 0))
    return pl.pallas_call(
        body, name="dt_fwd", grid=(nc,),
        in_specs=[pl.BlockSpec((Q, 128), lambda n: (n, OFF_DT // 128)), _full((1, 128)), _full((1, 128))],
        out_specs=[blk, blk, pl.BlockSpec((None, 128, Q), lambda n: (n, 0, 0))],
        out_shape=[jax.ShapeDtypeStruct((s, 128), F32), jax.ShapeDtypeStruct((s, 128), F32), jax.ShapeDtypeStruct((nc, 128, Q), F32)],
    )(proj, dtb, alog)


def _dt_bwd(proj, dtb, alog, ddt, dcum, dcumt, dproj):
    s = proj.shape[0]
    nc = s // Q

    def body(r_ref, b_ref, a_ref, ddt_ref, dcum_ref, dcumt_ref, _, dr_ref, db_ref, da_ref):
        _, vjp = jax.vjp(_dt_prep, r_ref[...], b_ref[...], a_ref[...])
        dr, db, da = vjp((ddt_ref[...], dcum_ref[...], dcumt_ref[...]))
        dr_ref[...] = dr.astype(BF16)

        @pl.when(pl.program_id(0) == 0)
        def _():
            db_ref[...] = jnp.zeros_like(db_ref)
            da_ref[...] = jnp.zeros_like(da_ref)

        db_ref[...] += db
        da_ref[...] += da

    blk = pl.BlockSpec((Q, 128), lambda n: (n, 0))
    pblk = pl.BlockSpec((Q, 128), lambda n: (n, OFF_DT // 128))
    return pl.pallas_call(
        body, name="dt_bwd", grid=(nc,),
        in_specs=[pblk, _full((1, 128)), _full((1, 128)), blk, blk, pl.BlockSpec((None, 128, Q), lambda n: (n, 0, 0)), ANY],
        out_specs=[pblk, _full((1, 128)), _full((1, 128))],
        out_shape=[jax.ShapeDtypeStruct(dproj.shape, BF16), jax.ShapeDtypeStruct((1, 128), F32), jax.ShapeDtypeStruct((1, 128), F32)],
        input_output_aliases={6: 0},
    )(proj, dtb, alog, ddt, dcum, dcumt, dproj)


def _ssd_specs(chunk_of):
    gs = SSD_GPS
    xs = pl.BlockSpec((Q, 256 * gs), lambda n, g: (chunk_of(n), g))
    bm = pl.BlockSpec((Q, 128 * gs), lambda n, g: (chunk_of(n), DI // (128 * gs) + g))
    cm = pl.BlockSpec((Q, 128 * gs), lambda n, g: (chunk_of(n), (DI + D) // (128 * gs) + g))
    per_chunk = pl.BlockSpec((Q, 128), lambda n, g: (chunk_of(n), 0))
    cum_t = pl.BlockSpec((None, 128, Q), lambda n, g: (chunk_of(n), 0, 0))
    state = pl.BlockSpec((None, 256 * gs, 128), lambda n, g: (chunk_of(n), g, 0))
    vec = pl.BlockSpec((1, 128), lambda n, g: (0, 0))
    return xs, bm, cm, per_chunk, cum_t, state, vec


SSD_GPS = 8


def _aligned(v, m):
    return v if isinstance(v, int) else pl.multiple_of(v, m)


def _ssd_fwd(xbc, dt, cum, cum_t, dsk):
    s = xbc.shape[0]
    nc = s // Q
    xs, bm, cm, per_chunk, cumt_spec, state_spec, vec = _ssd_specs(lambda n: n)
    gs = SSD_GPS

    def body(x_ref, b_ref, c_ref, dt_ref, cum_ref, cumt_ref, dsk_ref, y_ref, st_ref, carry):
        n, gstep = pl.program_id(0), (0 if gs == NGRP else pl.program_id(1))
        rows = pl.ds(_aligned(gstep * (256 * gs), 256 * gs), 256 * gs)

        @pl.when(n == 0)
        def _():
            carry[rows, :] = jnp.zeros((256 * gs, 128), F32)

        st_ref[...] = carry[rows, :]
        for k in range(gs):
            xo, so, bo = 256 * k, 256 * k, 128 * k
            y0, y1, n0, n1 = _ssd_group(
                x_ref[:, xo:xo + 128], x_ref[:, xo + 128:xo + 256], st_ref[so:so + 128, :], st_ref[so + 128:so + 256, :],
                b_ref[:, bo:bo + 128], c_ref[:, bo:bo + 128], dt_ref[...], cum_ref[...], cumt_ref[...], dsk_ref[...],
                gstep * gs + k)
            y_ref[:, xo:xo + 128] = y0
            y_ref[:, xo + 128:xo + 256] = y1
            base = gstep * (256 * gs) + so
            carry[pl.ds(_aligned(base, 128), 128), :] = n0
            carry[pl.ds(_aligned(base + 128, 128), 128), :] = n1

    return pl.pallas_call(
        body, name="ssd_fwd", grid=(nc, NGRP // gs),
        in_specs=[xs, bm, cm, per_chunk, per_chunk, cumt_spec, vec],
        out_specs=[xs, state_spec],
        out_shape=[jax.ShapeDtypeStruct((s, DI), F32), jax.ShapeDtypeStruct((nc, DI, 128), F32)],
        scratch_shapes=[pltpu.VMEM((DI, 128), F32)],
    )(xbc, xbc, xbc, dt, cum, cum_t, dsk)


def _ssd_bwd(xbc, dt, cum, cum_t, dsk, states, dy, dy_t, comm=None):
    s = xbc.shape[0]
    nc = s // Q
    xs, bm, cm, per_chunk, cumt_spec, state_spec, vec = _ssd_specs(lambda n: nc - 1 - n)
    gs = SSD_GPS
    n_ci, n_co = (len(comm.ins), len(comm.outs)) if comm is not None else (0, 0)
    steps = (nc, NGRP // gs)

    def body(*refs):
        x_ref, b_ref, c_ref, dt_ref, cum_ref, cumt_ref, dsk_ref, st_ref, dy_ref, dyt_ref = refs[:10]
        ci_refs = refs[10:10 + n_ci]
        dx_ref, db_ref, dc_ref, ddt_ref, dcum_ref, dcumt_ref, ddsk_ref = refs[10 + n_ci:17 + n_ci]
        co_refs = refs[17 + n_ci:17 + n_ci + n_co]
        carry = refs[17 + n_ci + n_co]
        cs_refs = refs[18 + n_ci + n_co:]
        n, gstep = pl.program_id(0), (0 if gs == NGRP else pl.program_id(1))
        rows = pl.ds(_aligned(gstep * (256 * gs), 256 * gs), 256 * gs)
        if comm is not None:
            pl.when((pl.program_id(0) == 0) & (pl.program_id(1) == 0))(lambda: comm.start(ci_refs, co_refs, cs_refs))

        @pl.when(n == 0)
        def _():
            carry[rows, :] = jnp.zeros((256 * gs, 128), F32)

        def zero_skip_sum():
            ddsk_ref[...] = jnp.zeros_like(ddsk_ref)

        def zero_chunk_sums():
            ddt_ref[...] = jnp.zeros_like(ddt_ref)
            dcum_ref[...] = jnp.zeros_like(dcum_ref)
            dcumt_ref[...] = jnp.zeros_like(dcumt_ref)

        if isinstance(gstep, int):
            pl.when(n == 0)(zero_skip_sum)
            zero_chunk_sums()
        else:
            pl.when((n == 0) & (gstep == 0))(zero_skip_sum)
            pl.when(gstep == 0)(zero_chunk_sums)

        for k in range(gs):
            xo, so, bo = 256 * k, 256 * k, 128 * k
            base = gstep * (256 * gs) + so
            lo = pl.ds(_aligned(base, 128), 128)
            hi = pl.ds(_aligned(base + 128, 128), 128)
            dx0, dx1, ds0, ds1, dbm, dcm, ddt, dcum, dcumt, ddsk = _ssd_group_bwd(
                x_ref[:, xo:xo + 128], x_ref[:, xo + 128:xo + 256], st_ref[so:so + 128, :], st_ref[so + 128:so + 256, :],
                b_ref[:, bo:bo + 128], c_ref[:, bo:bo + 128], dt_ref[...], cum_ref[...], cumt_ref[...], dsk_ref[...],
                gstep * gs + k, dy_ref[:, xo:xo + 128].astype(F32), dy_ref[:, xo + 128:xo + 256].astype(F32),
                dyt_ref[xo:xo + 128, :].astype(F32), dyt_ref[xo + 128:xo + 256, :].astype(F32), carry[lo, :], carry[hi, :])
            dx_ref[:, xo:xo + 128] = dx0
            dx_ref[:, xo + 128:xo + 256] = dx1
            db_ref[:, bo:bo + 128] = dbm
            dc_ref[:, bo:bo + 128] = dcm
            ddt_ref[...] += ddt
            dcum_ref[...] += dcum
            dcumt_ref[...] += dcumt
            ddsk_ref[...] += ddsk
            carry[lo, :] = ds0
            carry[hi, :] = ds1

        if comm is not None:
            pl.when((pl.program_id(0) == steps[0] - 1) & (pl.program_id(1) == steps[1] - 1))(
                lambda: comm.finish(ci_refs, co_refs, cs_refs))

    grp_blk = pl.BlockSpec((Q, 128 * gs), lambda n, g: (nc - 1 - n, g))
    extra_in = comm.ins if comm is not None else []
    extra_out = comm.outs if comm is not None else []
    extra_sems = comm.sems if comm is not None else []
    return pl.pallas_call(
        body, name="ssd_bwd", grid=steps,
        in_specs=[xs, bm, cm, per_chunk, per_chunk, cumt_spec, vec, state_spec, xs,
                  pl.BlockSpec((256 * gs, Q), lambda n, g: (g, nc - 1 - n))] + [ANY] * n_ci,
        out_specs=[xs, grp_blk, grp_blk, per_chunk, per_chunk, cumt_spec, vec] + [ANY] * n_co,
        out_shape=[jax.ShapeDtypeStruct((s, DI), F32), jax.ShapeDtypeStruct((s, D), F32), jax.ShapeDtypeStruct((s, D), F32),
                   jax.ShapeDtypeStruct((s, 128), F32), jax.ShapeDtypeStruct((s, 128), F32),
                   jax.ShapeDtypeStruct((nc, 128, Q), F32), jax.ShapeDtypeStruct((1, 128), F32)] + extra_out,
        scratch_shapes=[pltpu.VMEM((DI, 128), F32)] + extra_sems,
    )(xbc, xbc, xbc, dt, cum, cum_t, dsk, states, dy, dy_t, *extra_in)


def _adam_math(w, g, m, v):
    m2 = ADAM_B1 * m + (1.0 - ADAM_B1) * g
    v2 = ADAM_B2 * v + (1.0 - ADAM_B2) * jnp.square(g)
    m_hat = m2 / (1.0 - ADAM_B1 ** ADAM_STEP)
    v_hat = v2 / (1.0 - ADAM_B2 ** ADAM_STEP)
    delta = -ADAM_LR * (m_hat / (jnp.sqrt(v_hat) + ADAM_EPS) + ADAM_WD * w)
    return delta, m2, v2


def _adam(w, g, m, v, name):
    r, c = w.shape
    tr = r if r * c * 4 <= (1 << 20) else _pick(r, (128, 64, 32, 16, 8))

    def body(w_ref, g_ref, m_ref, v_ref, d_ref, m2_ref, v2_ref):
        d, m2, v2 = _adam_math(w_ref[...], g_ref[...], m_ref[...], v_ref[...])
        d_ref[...] = d
        m2_ref[...] = m2
        v2_ref[...] = v2

    blk = pl.BlockSpec((tr, c), lambda i: (i, 0))
    shp = jax.ShapeDtypeStruct((r, c), F32)
    return pl.pallas_call(body, name=name, grid=(r // tr,), in_specs=[blk] * 4, out_specs=[blk] * 3,
                          out_shape=[shp] * 3)(w, g, m, v)


def _sum_leading(xs, name, out_dtype=F32, tr=256):
    if isinstance(xs, tuple):
        a, b = xs
        n, r, c = a.shape
        tr = _pick(r, (tr, 128, 64, 32, 16, 8))

        def body2(a_ref, b_ref, o_ref):
            o_ref[...] = (a_ref[...].astype(F32) + b_ref[...].astype(F32)).astype(out_dtype)

        blk = pl.BlockSpec((None, tr, c), lambda s, i: (s, i, 0))
        return pl.pallas_call(body2, name=name, grid=(n, r // tr), in_specs=[blk, blk], out_specs=blk,
                              out_shape=jax.ShapeDtypeStruct((n, r, c), out_dtype))(a, b)
    n, r, c = xs.shape
    tr = r if n * r * c * 4 <= (8 << 20) else _pick(r, (tr, 128, 64, 32, 16, 8))

    def body(x_ref, o_ref):
        acc = x_ref[0].astype(F32)
        for s in range(1, n):
            acc = acc + x_ref[s].astype(F32)
        o_ref[...] = acc.astype(out_dtype)

    return pl.pallas_call(body, name=name, grid=(r // tr,), in_specs=[pl.BlockSpec((n, tr, c), lambda i: (0, i, 0))],
                          out_specs=pl.BlockSpec((tr, c), lambda i: (i, 0)),
                          out_shape=jax.ShapeDtypeStruct((r, c), out_dtype))(xs)


def _mod_fwd(c8, w_mod, b_sl):
    def body(c_ref, w_ref, b_ref, o_ref):
        o_ref[...] = jnp.dot(jax.nn.silu(c_ref[...]), w_ref[...], precision=HIGHEST, preferred_element_type=F32) + b_ref[...]

    return pl.pallas_call(body, name="mod_fwd", in_specs=[VMEM, VMEM, VMEM], out_specs=VMEM,
                          out_shape=jax.ShapeDtypeStruct((N_DEV, w_mod.shape[1]), F32))(c8, w_mod, b_sl)


def _mod_wgrad(c8, dmod):
    def body(c_ref, d_ref, o_ref):
        o_ref[...] = lax.dot_general(jax.nn.silu(c_ref[...]), d_ref[...], (((0,), (0,)), ((), ())),
                                     precision=HIGHEST, preferred_element_type=F32)

    return pl.pallas_call(body, name="mod_wgrad", in_specs=[VMEM, VMEM], out_specs=VMEM,
                          out_shape=jax.ShapeDtypeStruct((D, dmod.shape[1]), F32))(c8, dmod)


def _place():
    x, y, c = lax.axis_index("x"), lax.axis_index("y"), lax.axis_index("c")
    chips = [(1 - x, y), (x, 1 - y), (1 - x, 1 - y)]
    return x, y, c, chips


def _all_gather_small(v, name):
    r, w = v.shape

    def body(v_ref, o_ref, send_sems, recv_sems, local_sem):
        x, y, c, _ = _place()
        me = 4 * x + 2 * y + c
        own = pltpu.make_async_copy(v_ref, o_ref.at[me], local_sem)
        own.start()
        sends = []
        for k in range(1, N_DEV):
            tx = 1 - x if k & 4 else x
            ty = 1 - y if k & 2 else y
            tc = 1 - c if k & 1 else c
            peer = 4 * tx + 2 * ty + tc
            cp = pltpu.make_async_remote_copy(src_ref=v_ref, dst_ref=o_ref.at[me], send_sem=send_sems.at[k - 1],
                                              recv_sem=recv_sems.at[k - 1], device_id=(tx, ty, tc), device_id_type=MESH)
            cp.start()
            sends.append((cp, peer, (tx, ty, tc)))
        for k, (cp, peer, dev) in enumerate(sends):
            pltpu.make_async_remote_copy(src_ref=v_ref, dst_ref=o_ref.at[peer], send_sem=send_sems.at[k],
                                         recv_sem=recv_sems.at[k], device_id=dev, device_id_type=MESH).wait_recv()
        for cp, _, _ in sends:
            cp.wait_send()
        own.wait()

    return pl.pallas_call(
        body, name=name, in_specs=[VMEM], out_specs=VMEM, out_shape=jax.ShapeDtypeStruct((N_DEV, r, w), v.dtype),
        scratch_shapes=[pltpu.SemaphoreType.DMA((N_DEV - 1,)), pltpu.SemaphoreType.DMA((N_DEV - 1,)), pltpu.SemaphoreType.DMA],
    )(v)


def _all_gather_weights(shards):
    n = len(shards)
    nsem = n * 3 * AG_CHUNKS

    def body(*refs):
        srcs, dsts = refs[:n], refs[n:2 * n]
        send_sems, recv_sems, fwd_send_sems, fwd_recv_sems = refs[2 * n:]
        x, y, c, chips = _place()
        q = 2 * x + y
        sibling = (x, y, 1 - c)
        sends = []
        for r in range(AG_CHUNKS):
            for t in range(n):
                hc = srcs[t].shape[0] // 2 // AG_CHUNKS
                rows = pl.ds(c * (hc * AG_CHUNKS) + r * hc, hc)
                for j, (cx, cy) in enumerate(chips):
                    k = (t * 3 + j) * AG_CHUNKS + r
                    cp = pltpu.make_async_remote_copy(src_ref=srcs[t].at[rows], dst_ref=dsts[t].at[q, rows],
                                                      send_sem=send_sems.at[k], recv_sem=recv_sems.at[k],
                                                      device_id=(cx, cy, c), device_id_type=MESH)
                    cp.start()
                    sends.append(cp)
        fwds = []
        for r in range(AG_CHUNKS):
            for t in range(n):
                hc = srcs[t].shape[0] // 2 // AG_CHUNKS
                sub = hc // D2D_SPLIT
                for j, (cx, cy) in enumerate(chips):
                    k = (t * 3 + j) * AG_CHUNKS + r
                    base = c * (hc * AG_CHUNKS) + r * hc
                    part = dsts[t].at[2 * cx + cy, pl.ds(base, hc)]
                    pltpu.make_async_remote_copy(src_ref=part, dst_ref=part, send_sem=send_sems.at[k], recv_sem=recv_sems.at[k],
                                                 device_id=(cx, cy, c), device_id_type=MESH).wait_recv()
                    for u in range(D2D_SPLIT):
                        piece = dsts[t].at[2 * cx + cy, pl.ds(base + u * sub, sub)]
                        pltpu.make_async_remote_copy(src_ref=piece, dst_ref=piece, send_sem=fwd_send_sems.at[k],
                                                     recv_sem=fwd_recv_sems.at[k], device_id=sibling, device_id_type=MESH).start()
                    fwds.append((part, k))
        for r in range(AG_CHUNKS):
            for t in range(n):
                hc = srcs[t].shape[0] // 2 // AG_CHUNKS
                for j, (cx, cy) in enumerate(chips):
                    k = (t * 3 + j) * AG_CHUNKS + r
                    part = dsts[t].at[2 * cx + cy, pl.ds((1 - c) * (hc * AG_CHUNKS) + r * hc, hc)]
                    pltpu.make_async_remote_copy(src_ref=part, dst_ref=part, send_sem=fwd_send_sems.at[k],
                                                 recv_sem=fwd_recv_sems.at[k], device_id=sibling, device_id_type=MESH).wait_recv()
        for cp in sends:
            cp.wait_send()
        for part, k in fwds:
            pltpu.make_async_remote_copy(src_ref=part, dst_ref=part, send_sem=fwd_send_sems.at[k], recv_sem=fwd_recv_sems.at[k],
                                         device_id=sibling, device_id_type=MESH).wait_send()

    return pl.pallas_call(
        body, name="all_gather_weights", in_specs=[ANY] * n, out_specs=[ANY] * n,
        out_shape=[jax.ShapeDtypeStruct((N_CHIP,) + s.shape, s.dtype) for s in shards],
        scratch_shapes=[pltpu.SemaphoreType.DMA((nsem,)), pltpu.SemaphoreType.DMA((nsem,)), pltpu.SemaphoreType.DMA((nsem,)),
                        pltpu.SemaphoreType.DMA((nsem,))],
    )(*shards)


def _exchange_halves(grads, tag=""):
    n = len(grads)

    def body(*refs):
        srcs, theirs = refs[:n], refs[n:2 * n]
        send_sems, recv_sems = refs[2 * n:]
        x, y, c, _ = _place()
        sibling = (x, y, 1 - c)
        waits = []
        for t in range(n):
            h = srcs[t].shape[1] // 2
            sub = h // D2D_SPLIT
            for s in range(N_CHIP):
                for u in range(D2D_SPLIT):
                    pltpu.make_async_remote_copy(src_ref=srcs[t].at[s, pl.ds((1 - c) * h + u * sub, sub)],
                                                 dst_ref=theirs[t].at[s, pl.ds(u * sub, sub)],
                                                 send_sem=send_sems.at[t], recv_sem=recv_sems.at[t],
                                                 device_id=sibling, device_id_type=MESH).start()
            waits.append(pltpu.make_async_remote_copy(src_ref=srcs[t].at[:, pl.ds((1 - c) * h, h)], dst_ref=theirs[t],
                                                      send_sem=send_sems.at[t], recv_sem=recv_sems.at[t],
                                                      device_id=sibling, device_id_type=MESH))
        for whole in waits:
            whole.wait()

    half = [jax.ShapeDtypeStruct((N_CHIP, g.shape[1] // 2, g.shape[2]), g.dtype) for g in grads]
    return pl.pallas_call(
        body, name=f"exchange_halves_{tag}", in_specs=[ANY] * n, out_specs=[ANY] * n, out_shape=half,
        scratch_shapes=[pltpu.SemaphoreType.DMA((n,)), pltpu.SemaphoreType.DMA((n,))],
    )(*grads)


def _chip_exchange_comm(parts):
    n = len(parts)

    def copies(srcs, dsts, sems):
        send_sems, recv_sems, local_sems = sems
        x, y, c, chips = _place()
        q = 2 * x + y
        owns = [pltpu.make_async_copy(srcs[t].at[q], dsts[t].at[q], local_sems.at[t]) for t in range(n)]
        sends, recvs = [], []
        for t in range(n):
            for j, (cx, cy) in enumerate(chips):
                args = dict(send_sem=send_sems.at[3 * t + j], recv_sem=recv_sems.at[3 * t + j],
                            device_id=(cx, cy, c), device_id_type=MESH)
                sends.append(pltpu.make_async_remote_copy(src_ref=srcs[t].at[2 * cx + cy], dst_ref=dsts[t].at[q], **args))
                part = dsts[t].at[2 * cx + cy]
                recvs.append(pltpu.make_async_remote_copy(src_ref=part, dst_ref=part, **args))
        return owns, sends, recvs

    def start(srcs, dsts, sems):
        owns, sends, _ = copies(srcs, dsts, sems)
        for cp in owns + sends:
            cp.start()

    def finish(srcs, dsts, sems):
        owns, sends, recvs = copies(srcs, dsts, sems)
        for cp in recvs:
            cp.wait_recv()
        for cp in sends:
            cp.wait_send()
        for cp in owns:
            cp.wait()

    return _Comm(parts, [jax.ShapeDtypeStruct(p.shape, p.dtype) for p in parts],
                 [pltpu.SemaphoreType.DMA((3 * n,)), pltpu.SemaphoreType.DMA((3 * n,)), pltpu.SemaphoreType.DMA((n,))], start, finish)


def _gather_relay_comm(shard):
    h = shard.shape[0] // 2
    sub = h // D2D_SPLIT

    def place_and_copies(srcs, dsts, sems):
        send_sems, recv_sems, fwd_send_sems, fwd_recv_sems = sems
        x, y, c, chips = _place()
        q = 2 * x + y
        mine, theirs = pl.ds(c * h, h), pl.ds((1 - c) * h, h)
        sends, recvs, relays, relayed = [], [], [], []
        for j, (cx, cy) in enumerate(chips):
            ici = dict(send_sem=send_sems.at[j], recv_sem=recv_sems.at[j], device_id=(cx, cy, c), device_id_type=MESH)
            d2d = dict(send_sem=fwd_send_sems.at[j], recv_sem=fwd_recv_sems.at[j], device_id=(x, y, 1 - c), device_id_type=MESH)
            sends.append(pltpu.make_async_remote_copy(src_ref=srcs[0].at[mine], dst_ref=dsts[0].at[q, mine], **ici))
            landed = dsts[0].at[2 * cx + cy, mine]
            recvs.append(pltpu.make_async_remote_copy(src_ref=landed, dst_ref=landed, **ici))
            pieces = [dsts[0].at[2 * cx + cy, pl.ds(c * h + u * sub, sub)] for u in range(D2D_SPLIT)]
            relays.append(([pltpu.make_async_remote_copy(src_ref=p, dst_ref=p, **d2d) for p in pieces],
                           pltpu.make_async_remote_copy(src_ref=landed, dst_ref=landed, **d2d)))
            other = dsts[0].at[2 * cx + cy, theirs]
            relayed.append(pltpu.make_async_remote_copy(src_ref=other, dst_ref=other, **d2d))
        return sends, recvs, relays, relayed

    def start(srcs, dsts, sems):
        for cp in place_and_copies(srcs, dsts, sems)[0]:
            cp.start()

    def middle(srcs, dsts, sems):
        _, recvs, relays, _ = place_and_copies(srcs, dsts, sems)
        for cp, (pieces, _) in zip(recvs, relays):
            cp.wait_recv()
            for piece in pieces:
                piece.start()

    def finish(srcs, dsts, sems):
        sends, _, relays, relayed = place_and_copies(srcs, dsts, sems)
        for cp in relayed:
            cp.wait_recv()
        for cp in sends:
            cp.wait_send()
        for _, whole in relays:
            whole.wait_send()

    comm = _Comm([shard], [jax.ShapeDtypeStruct((N_CHIP,) + shard.shape, shard.dtype)],
                 [pltpu.SemaphoreType.DMA((3,))] * 4, start, finish)
    comm.middle = middle
    return comm


def _run_comm(comm, name):
    n_ci, n_co = len(comm.ins), len(comm.outs)

    def body(*refs):
        ci, co, cs = refs[:n_ci], refs[n_ci:n_ci + n_co], refs[n_ci + n_co:]
        comm.start(ci, co, cs)
        if comm.middle is not None:
            comm.middle(ci, co, cs)
        comm.finish(ci, co, cs)

    return pl.pallas_call(body, name=name, in_specs=[ANY] * n_ci, out_specs=[ANY] * n_co, out_shape=comm.outs,
                          scratch_shapes=comm.sems)(*comm.ins)


def _share_halves(halves, tag=""):
    n = len(halves)

    def body(*refs):
        srcs, dsts = refs[:n], refs[n:2 * n]
        send_sems, recv_sems = refs[2 * n:]
        x, y, c, _ = _place()
        sibling = (x, y, 1 - c)
        for t in range(n):
            h = srcs[t].shape[0]
            sub = h // (2 * D2D_SPLIT)
            for u in range(2 * D2D_SPLIT):
                pltpu.make_async_remote_copy(src_ref=srcs[t].at[pl.ds(u * sub, sub)], dst_ref=dsts[t].at[pl.ds(c * h + u * sub, sub)],
                                             send_sem=send_sems.at[t], recv_sem=recv_sems.at[t],
                                             device_id=sibling, device_id_type=MESH).start()
        for t in range(n):
            h = srcs[t].shape[0]
            pltpu.make_async_remote_copy(src_ref=srcs[t], dst_ref=dsts[t].at[pl.ds((1 - c) * h, h)], send_sem=send_sems.at[t],
                                         recv_sem=recv_sems.at[t], device_id=sibling, device_id_type=MESH).wait()

    return pl.pallas_call(
        body, name=f"share_halves_{tag}", in_specs=[ANY] * n, out_specs=[ANY] * n,
        out_shape=[jax.ShapeDtypeStruct((2 * h.shape[0], h.shape[1]), h.dtype) for h in halves],
        scratch_shapes=[pltpu.SemaphoreType.DMA((n,)), pltpu.SemaphoreType.DMA((n,))],
    )(*halves)


def _gather_weights(shards, chip):
    gathered = _all_gather_weights(shards)
    return [lax.dynamic_update_slice(g, s[None], (chip, 0, 0)) for g, s in zip(gathered, shards)]


def _reduce_scatter_begin(sends, core, tag):
    theirs = _exchange_halves(sends, tag)
    mines = [lax.dynamic_slice(g, (0, core * (g.shape[1] // 2), 0), (N_CHIP, g.shape[1] // 2, g.shape[2])) for g in sends]
    pair = [_sum_leading((m, t), f"pair_sum_{tag}{i}", out_dtype=BF16) for i, (m, t) in enumerate(zip(mines, theirs))]
    return _chip_exchange_comm(pair)


def _reduce_scatter_end(contrib, core, tag):
    halves = [_sum_leading(c, f"chip_sum_{tag}{i}") for i, c in enumerate(contrib)]
    shared = _share_halves(halves, tag)
    return [lax.dynamic_update_slice(full, mine, (core * mine.shape[0], 0)) for full, mine in zip(shared, halves)]


def _reduce_scatter(sends, core, tag=""):
    contrib = _run_comm(_reduce_scatter_begin(sends, core, tag), f"exchange_chips_{tag}")
    return _reduce_scatter_end(contrib, core, tag)


def _pack(arrs):
    rows = []
    for a in arrs:
        flat = a.astype(F32).reshape(-1)
        pad = (-flat.shape[0]) % 1024
        rows.append(jnp.pad(flat, (0, pad)).reshape(-1, 128))
    return jnp.concatenate(rows, axis=0)


def _unpack(buf, shapes):
    out, r = [], 0
    for shp in shapes:
        size = 1
        for d in shp:
            size *= d
        nr = (size + 1023) // 1024 * 8
        out.append(buf[r:r + nr].reshape(-1)[:size].reshape(shp))
        r += nr
    return out


R_BG, R_BS, R_OUT, R_FF = D // N_CHIP, DI // N_CHIP, D // N_CHIP, DFF // N_CHIP


def _pack_rest_shard(w_bg, w_bs, w_out, w_ff2, w_ff1):
    return jnp.concatenate([w_bg, w_bs, w_out, w_ff2, w_ff1], axis=0).astype(BF16)


def _unpack_rest(g):
    o1, o2, o3, o4 = R_BG, R_BG + R_BS, R_BG + R_BS + R_OUT, R_BG + R_BS + R_OUT + R_FF
    return (g[:, :o1].reshape(D, D), g[:, o1:o2].reshape(DI, D), g[:, o2:o3].reshape(D, D),
            jnp.transpose(g[:, o4:], (1, 0, 2)).reshape(D, DFF), g[:, o3:o4].reshape(DFF, D))


def _local_step(x, t, mod, w_in_r, rest, gm_norm_w, gm_ws, gm_bs, conv_w, conv_b, dt_bias, a_log, d_skip, ssm_norm_w,
                final_norm_w, place=None):
    sh1, sc1, g1, sh2, sc2, g2 = [mod[:, i * D:(i + 1) * D] for i in range(6)]
    ws_b = jnp.where(jnp.tril(jnp.ones((Q, Q), bool))[None], gm_ws, 0.0).astype(BF16)
    bs_t = gm_bs.T
    pad32 = lambda v: jnp.pad(v, ((0, 0), (0, 128 - NHEAD)))
    dtb, alog, dsk = pad32(dt_bias), pad32(a_log), pad32(d_skip)

    h1 = _prenorm(x, sc1, sh1)
    gdt = (F32,) if place is None else (BF16,)
    if place is None:
        (proj,) = _matmul(h1, w_in_r, "nn", name="mm_proj", tn=1152)
        w_bg, w_bs, w_out, w_ff1, w_ff2 = rest
    else:
        chip, core = place
        proj, g_rest = _matmul(h1, w_in_r, "nn", name="mm_proj", tn=1152, comm=_gather_relay_comm(rest))
        w_bg, w_bs, w_out, w_ff1, w_ff2 = _unpack_rest(lax.dynamic_update_slice(g_rest, rest[None], (chip, 0, 0)))
    y_a = _sgu_fwd(proj, gm_norm_w, ws_b, bs_t)
    xbc = _conv_fwd(proj, conv_w, conv_b)
    dt, cum, cum_t = _dt_fwd(proj, dtb, alog)
    y_ssd, states = _ssd_fwd(xbc, dt, cum, cum_t, dsk)
    y_b = _gatenorm_fwd(y_ssd, proj, ssm_norm_w)
    (ba,) = _matmul(y_a, w_bg, "nn", name="mm_branch_gm", out_dtypes=(BF16,))
    (bb,) = _matmul(y_b, w_bs, "nn", name="mm_branch_ssm", out_dtypes=(BF16,), tm=512, tk=2048)
    mixed = _mix_fwd(proj, ba, bb)
    (mo,) = _matmul(mixed, w_out, "nn", name="mm_out")
    x1, h2 = _resid_norm(x, mo, g1, sc2, sh2)
    f, act = _matmul(h2, w_ff1, "nn", name="mm_ff1", out_dtypes=(BF16, BF16),
                     epi=lambda acc: (acc, jnp.square(jnp.maximum(acc, 0.0))))
    (fo,) = _matmul(act, w_ff2, "nn", name="mm_ff2", tm=512, tk=4096)
    loss8, dx2, dfo, dg2, dfw = _loss_head(x1, fo, g2, final_norm_w, t)

    (df,) = _matmul(dfo, w_ff2, "nt", name="mm_ff2_dx", out_dtypes=(BF16,), epi_ins=(f,),
                    epi=lambda acc, fv: (acc * (2.0 * jnp.maximum(fv.astype(F32), 0.0)),))
    (g_ff2,) = _matmul(act, dfo, "tn", name="mm_ff2_dw", out_dtypes=gdt, tm=512, tk=4096)
    (dh2,) = _matmul(df, w_ff1, "nt", name="mm_ff1_dx", tm=512, tk=4096)
    (g_ff1,) = _matmul(h2, df, "tn", name="mm_ff1_dw", out_dtypes=gdt, tm=512, tk=4096)
    dx1, dmo, dg1, dsc2, dsh2 = _resid_norm_bwd(x1, mo, g1, sc2, sh2, dh2, dx2)
    (dmixed,) = _matmul(dmo, w_out, "nt", name="mm_out_dx", out_dtypes=(BF16,))
    (g_out,) = _matmul(mixed, dmo, "tn", name="mm_out_dw", out_dtypes=gdt, tm=512, tk=4096)
    early = None
    if place is not None:
        stack = jnp.concatenate([g_out.reshape(N_CHIP, R_OUT, D), g_ff2.reshape(N_CHIP, R_FF, D),
                                 jnp.transpose(g_ff1.reshape(D, N_CHIP, D), (1, 0, 2))], axis=1).astype(BF16)
        early = _reduce_scatter_begin([stack], core, "early")
    dproj, dba, dbb = _mix_bwd(proj, ba, bb, dmixed)
    (dy_a,) = _matmul(dba, w_bg, "nt", name="mm_branch_gm_dx", out_dtypes=(BF16,))
    (g_bg,) = _matmul(y_a, dba, "tn", name="mm_branch_gm_dw", out_dtypes=gdt, tm=512, tk=4096)
    (dy_b,) = _matmul(dbb, w_bs, "nt", name="mm_branch_ssm_dx", out_dtypes=(BF16,))
    (g_bs,) = _matmul(y_b, dbb, "tn", name="mm_branch_ssm_dw", out_dtypes=gdt, tm=512, tk=4096)
    dproj, d_gm_norm, d_ws, d_bs_t = _sgu_bwd(proj, gm_norm_w, ws_b, bs_t, dy_a, dproj)
    dy_ssd, dy_ssd_t, dproj, d_ssm_norm = _gatenorm_bwd(y_ssd, proj, ssm_norm_w, dy_b, dproj)
    dxs, dbm, dcm, ddt, dcum, dcum_t, d_dsk, *early_contrib = _ssd_bwd(xbc, dt, cum, cum_t, dsk, states, dy_ssd, dy_ssd_t,
                                                                       comm=early)
    dproj, d_dtb, d_alog = _dt_bwd(proj, dtb, alog, ddt, dcum, dcum_t, dproj)
    dproj, dw_x, db_x = _conv_bwd(proj, conv_w, conv_b, dxs, dproj, 0, "conv_bwd_x")
    dproj, dw_b, db_b = _conv_bwd(proj, conv_w, conv_b, dbm, dproj, DI, "conv_bwd_b")
    dproj, dw_c, db_c = _conv_bwd(proj, conv_w, conv_b, dcm, dproj, DI + D, "conv_bwd_c")
    d_conv_w = jnp.concatenate([dw_x, dw_b, dw_c], axis=1)
    d_conv_b = jnp.concatenate([db_x, db_b, db_c], axis=1)
    (g_in_r,) = _matmul(h1, dproj, "tn", name="mm_proj_dw", out_dtypes=gdt, tm=512, tn=1152, tk=4096)
    late = None
    if place is not None:
        send_in = _w_in_to_blocks(g_in_r)
        stack = jnp.concatenate([g_bg.reshape(N_CHIP, R_BG, D), g_bs.reshape(N_CHIP, R_BS, D)], axis=1).astype(BF16)
        late = _reduce_scatter_begin([send_in, stack], core, "late")
    dh1, *late_contrib = _matmul(dproj, w_in_r, "nt", name="mm_proj_dx", tk=3456, comm=late)
    grad_x, dsc1, dsh1 = _prenorm_bwd(x, sc1, sh1, dh1, dx1)

    dmod = jnp.concatenate([dsh1, dsc1, dg1, dsh2, dsc2, dg2], axis=1)
    small = dict(gm_ws=d_ws, gm_norm_w=d_gm_norm, gm_bs=d_bs_t.T, conv_w=d_conv_w, conv_b=d_conv_b,
                 dt_bias=d_dtb[:, :NHEAD], a_log=d_alog[:, :NHEAD], d_skip=d_dsk[:, :NHEAD],
                 ssm_norm_w=d_ssm_norm, final_norm_w=dfw, dmod=dmod, loss=loss8[:1, :1])
    if place is None:
        return grad_x, small, dict(w_in_r=g_in_r, w_bg=g_bg, w_bs=g_bs, w_out=g_out, w_ff1=g_ff1, w_ff2=g_ff2)
    (s_early,) = _reduce_scatter_end(early_contrib, core, "early")
    s_in, s_late = _reduce_scatter_end(late_contrib, core, "late")
    o1, o2 = R_OUT, R_OUT + R_FF
    big = dict(w_in=s_in, w_bg=s_late[:R_BG], w_bs=s_late[R_BG:], w_out=s_early[:o1], w_ff2=s_early[o1:o2], w_ff1=s_early[o2:])
    return grad_x, small, big


SMALL_KEYS = ("gm_ws", "gm_norm_w", "gm_bs", "conv_w", "conv_b", "dt_bias", "a_log", "d_skip", "ssm_norm_w",
              "final_norm_w", "dmod", "loss")
SMALL_SHAPES = ((GM_G, Q, Q), (1, D), (GM_G, Q), (4, CONV), (1, CONV), (1, NHEAD), (1, NHEAD), (1, NHEAD), (1, DI),
                (1, D), (1, 6 * D), (1, 1))


def _reorder_w_in(w_full):
    k = w_full.shape[0]
    return jnp.concatenate([w_full[:, :8192], w_full[:, 8224:], w_full[:, 8192:8224],
                            jnp.zeros((k, W_IN_R - W_IN), w_full.dtype)], axis=1)


W_SH = W_IN // N_CHIP
TAIL = 8192 - (N_CHIP - 1) * W_SH


def _w_in_from_blocks(g):
    last = g[N_CHIP - 1]
    return jnp.concatenate([g[0], g[1], g[2], last[:, :TAIL], last[:, TAIL + NHEAD:], last[:, TAIL:TAIL + NHEAD],
                            jnp.zeros((g.shape[1], W_IN_R - W_IN), g.dtype)], axis=1)


def _w_in_to_blocks(g_r):
    cut = (N_CHIP - 1) * W_SH
    last = jnp.concatenate([g_r[:, cut:8192], g_r[:, OFF_DT:OFF_DT + NHEAD], g_r[:, 8192:OFF_DT]], axis=1)
    return jnp.stack([g_r[:, :W_SH], g_r[:, W_SH:2 * W_SH], g_r[:, 2 * W_SH:cut], last], axis=0)


def _restore_w_in(g_r):
    return jnp.concatenate([g_r[:, :8192], g_r[:, OFF_DT:OFF_DT + NHEAD], g_r[:, 8192:OFF_DT]], axis=1)


def kernel(x, c, w_mod, b_mod, w_in, gm_norm_w, gm_ws, gm_bs, conv_w, conv_b, dt_bias, a_log, d_skip, ssm_norm_w, w_branch_gm, w_branch_ssm, w_out, w_ff1, w_ff2, final_norm_w, loss_target, m_w_mod, m_b_mod, m_w_in, m_gm_norm_w, m_gm_ws, m_gm_bs, m_conv_w, m_conv_b, m_dt_bias, m_a_log, m_d_skip, m_ssm_norm_w, m_w_branch_gm, m_w_branch_ssm, m_w_out, m_w_ff1, m_w_ff2, m_final_norm_w, v_w_mod, v_b_mod, v_w_in, v_gm_norm_w, v_gm_ws, v_gm_bs, v_conv_w, v_conv_b, v_dt_bias, v_a_log, v_d_skip, v_ssm_norm_w, v_w_branch_gm, v_w_branch_ssm, v_w_out, v_w_ff1, v_w_ff2, v_final_norm_w):
    ax, ay, ac = lax.axis_index("x"), lax.axis_index("y"), lax.axis_index("c")
    chip = 2 * ax + ay
    dev = 2 * chip + ac
    seq = x.shape[1]
    nmod = w_mod.shape[2]

    first = jnp.concatenate([c, conv_w[0], jnp.zeros((3, D), F32)], axis=0)
    first_all = _all_gather_small(first, "all_gather_cond")
    c8 = first_all[:, 0, :]
    conv_w_full = jnp.concatenate([first_all[2 * k, 1:5, :] for k in range(N_CHIP)], axis=1)
    b_sl = lax.dynamic_slice(b_mod, (0, chip * nmod), (1, nmod))
    mod_part = _mod_fwd(c8, w_mod[0], b_sl)
    mod_all = _all_gather_small(mod_part, "all_gather_mod")
    mod = jnp.concatenate([lax.dynamic_slice(mod_all, (2 * k, dev, 0), (1, 1, nmod))[0] for k in range(N_CHIP)], axis=1)

    (g_in,) = _gather_weights([w_in[0].astype(BF16)], chip)
    w_in_r = _w_in_from_blocks(g_in)
    rest = _pack_rest_shard(w_branch_gm[0], w_branch_ssm[0], w_out[0], w_ff2[0], w_ff1[0])

    grad_x, small, big = _local_step(
        x[0], loss_target[0], mod, w_in_r, rest, gm_norm_w, gm_ws[0], gm_bs[0], conv_w_full, conv_b, dt_bias, a_log,
        d_skip, ssm_norm_w, final_norm_w.reshape(1, D), place=(chip, ac))

    small_all = _all_gather_small(_pack([small[k] for k in SMALL_KEYS]), "all_gather_small_grads")
    small_sum = _sum_leading(small_all, "sum_small_grads")
    s_ws, s_gnw, s_bs, s_cw, s_cb, s_dtb, s_alog, s_dsk, s_snw, s_fnw, s_bmod, s_loss = _unpack(small_sum, SMALL_SHAPES)
    dmod_all = jnp.stack([_unpack(small_all[k], SMALL_SHAPES)[10][0] for k in range(N_DEV)], axis=0)
    g_w_mod = _mod_wgrad(c8, lax.dynamic_slice(dmod_all, (0, chip * nmod), (N_DEV, nmod)))
    g_conv_w = lax.dynamic_slice(s_cw, (0, chip * (CONV // N_CHIP)), (4, CONV // N_CHIP))

    g_w_in, g_w_bg, g_w_bs, g_w_out, g_w_ff2, g_w_ff1 = (big[k] for k in ("w_in", "w_bg", "w_bs", "w_out", "w_ff2", "w_ff1"))

    def adam_big(w, g, m, v, name):
        d, m2, v2 = _adam(w.reshape(g.shape), g, m.reshape(g.shape), v.reshape(g.shape), name)
        return g.reshape(w.shape), d.reshape(w.shape), m2.reshape(w.shape), v2.reshape(w.shape)

    res = {}
    res["w_mod"] = adam_big(w_mod, g_w_mod, m_w_mod, v_w_mod, "adam_w_mod")
    res["w_in"] = adam_big(w_in, g_w_in, m_w_in, v_w_in, "adam_w_in")
    res["w_branch_gm"] = adam_big(w_branch_gm, g_w_bg, m_w_branch_gm, v_w_branch_gm, "adam_w_branch_gm")
    res["w_branch_ssm"] = adam_big(w_branch_ssm, g_w_bs, m_w_branch_ssm, v_w_branch_ssm, "adam_w_branch_ssm")
    res["w_out"] = adam_big(w_out, g_w_out, m_w_out, v_w_out, "adam_w_out")
    res["w_ff1"] = adam_big(w_ff1, g_w_ff1, m_w_ff1, v_w_ff1, "adam_w_ff1")
    res["w_ff2"] = adam_big(w_ff2, g_w_ff2, m_w_ff2, v_w_ff2, "adam_w_ff2")

    names = ("b_mod", "gm_norm_w", "gm_ws", "gm_bs", "conv_w", "conv_b", "dt_bias", "a_log", "d_skip", "ssm_norm_w", "final_norm_w")
    ws = (b_mod, gm_norm_w, gm_ws, gm_bs, conv_w, conv_b, dt_bias, a_log, d_skip, ssm_norm_w, final_norm_w)
    ms = (m_b_mod, m_gm_norm_w, m_gm_ws, m_gm_bs, m_conv_w, m_conv_b, m_dt_bias, m_a_log, m_d_skip, m_ssm_norm_w, m_final_norm_w)
    vs = (v_b_mod, v_gm_norm_w, v_gm_ws, v_gm_bs, v_conv_w, v_conv_b, v_dt_bias, v_a_log, v_d_skip, v_ssm_norm_w, v_final_norm_w)
    gs = (s_bmod, s_gnw, s_ws, s_bs, g_conv_w, s_cb, s_dtb, s_alog, s_dsk, s_snw, s_fnw)
    gs = [g.reshape(w.shape) for g, w in zip(gs, ws)]
    shapes = [w.shape for w in ws]
    d_p, m_p, v_p = _adam(_pack(ws), _pack(gs), _pack(ms), _pack(vs), "adam_small")
    for name, g, d, m2, v2 in zip(names, gs, _unpack(d_p, shapes), _unpack(m_p, shapes), _unpack(v_p, shapes)):
        res[name] = (g, d, m2, v2)

    order = ("w_mod", "b_mod", "w_in", "gm_norm_w", "gm_ws", "gm_bs", "conv_w", "conv_b", "dt_bias", "a_log", "d_skip",
             "ssm_norm_w", "w_branch_gm", "w_branch_ssm", "w_out", "w_ff1", "w_ff2", "final_norm_w")
    loss = s_loss.reshape(())
    return (loss, grad_x.reshape(x.shape), *[res[k][0] for k in order], *[res[k][1] for k in order],
            *[res[k][2] for k in order], *[res[k][3] for k in order])
```

```python
import functools

import jax
import jax.numpy as jnp
from jax import lax
from jax.experimental import pallas as pl
from jax.experimental.pallas import tpu as pltpu

F32 = jnp.float32
BF16 = jnp.bfloat16
MESH = pl.DeviceIdType.MESH
HIGHEST = lax.Precision.HIGHEST

D = 1024
EPS = 1e-6
Q = 128
GM_G = 8
NHEAD = 32
NGRP = 8
DI = 2048
CONV = 4096
DFF = 4096
W_IN = 10272
W_IN_R = 10368
OFF_Z, OFF_XBC, OFF_GA, OFF_DT = 2048, 4096, 8192, 10240
N_CHIP = 4
N_DEV = 8
AG_CHUNKS = 4
D2D_SPLIT = 4

ADAM_LR, ADAM_B1, ADAM_B2, ADAM_EPS, ADAM_WD, ADAM_STEP = 0.001, 0.9, 0.999, 1e-08, 0.01, 10

ANY = pl.BlockSpec(memory_space=pl.ANY)
VMEM = pl.BlockSpec(memory_space=pltpu.VMEM)


def _full(shape):
    return pl.BlockSpec(shape, lambda *_: (0,) * len(shape))


def _pick(n, prefs):
    for p in prefs:
        if n % p == 0:
            return p
    return n


def _rms(x):
    return x * lax.rsqrt(jnp.mean(x * x, axis=-1, keepdims=True) + EPS)


def _modnorm(x, sc, sh):
    return _rms(x) * (1.0 + sc) + sh


def _sgu_pre(u, v, w):
    return jax.nn.gelu(u), _rms(jax.nn.gelu(v)) * w


def _gatenorm(y, z, w):
    g = y * jax.nn.silu(z)
    return _rms(g) * w


def _mix(ga, gb, ba, bb):
    return jax.nn.sigmoid(ga) * ba + jax.nn.sigmoid(gb) * bb


def _loss_tile(x1, fo, g2, fw, t):
    x2 = x1 + g2 * fo
    y = _rms(x2) * fw
    err = jnp.square(y - t)
    return 0.5 * jnp.sum(jnp.mean(err, axis=-1))


def _tril(n):
    r = lax.broadcasted_iota(jnp.int32, (n, n), 0)
    c = lax.broadcasted_iota(jnp.int32, (n, n), 1)
    return r >= c


def _dt_prep(dtr, dtb, alog):
    dt = jax.nn.softplus(dtr + dtb)
    a = dt * (-jnp.exp(alog))
    ones = _tril(Q).astype(F32)
    cum = jnp.dot(ones, a, precision=HIGHEST, preferred_element_type=F32)
    cum_t = lax.dot_general(a, ones, (((0,), (1,)), ((), ())), precision=HIGHEST, preferred_element_type=F32)
    return dt, cum, cum_t


def _ssd_group(x0, x1, s0, s1, bm, cm, dt, cum, cum_t, dsk, grp):
    lane = lax.broadcasted_iota(jnp.int32, (1, 128), 1)
    sub = lax.broadcasted_iota(jnp.int32, (128, 1), 0)
    future = jnp.where(_tril(Q), 0.0, -jnp.inf)
    half = lane < 64
    half_rows = sub < 64
    bmb = bm.astype(BF16)
    cmb = cm.astype(BF16)
    cb = lax.dot_general(cmb, bmb, (((1,), (1,)), ((), ())), preferred_element_type=F32)

    def col(v, h):
        return jnp.sum(jnp.where(lane == h, v, 0.0), axis=1, keepdims=True)

    def row(v, h):
        return jnp.sum(jnp.where(sub == h, v, 0.0), axis=0, keepdims=True)

    def last(c):
        return jnp.sum(jnp.where(sub == Q - 1, c, 0.0), axis=0, keepdims=True)

    outs, states = [], []
    for p, (xp, sp) in enumerate(((x0, s0), (x1, s1))):
        h_a = 4 * grp + 2 * p
        h_b = h_a + 1
        dt_a, dt_b = col(dt, h_a), col(dt, h_b)
        cum_a, cum_b = col(cum, h_a), col(cum, h_b)
        row_a, row_b = row(cum_t, h_a), row(cum_t, h_b)
        last_a, last_b = last(cum_a), last(cum_b)
        xdt = xp * jnp.where(half, dt_a, dt_b)
        xdtb = xdt.astype(BF16)
        m_a = (cb * jnp.exp(cum_a - row_a + future)).astype(BF16)
        m_b = (cb * jnp.exp(cum_b - row_b + future)).astype(BF16)
        y_intra = jnp.where(half, jnp.dot(m_a, xdtb, preferred_element_type=F32),
                            jnp.dot(m_b, xdtb, preferred_element_type=F32))
        y_inter = lax.dot_general(cmb, sp.astype(BF16), (((1,), (1,)), ((), ())), preferred_element_type=F32)
        y_inter = y_inter * jnp.where(half, jnp.exp(cum_a), jnp.exp(cum_b))
        w_end = jnp.where(half, jnp.exp(last_a - cum_a), jnp.exp(last_b - cum_b))
        upd = lax.dot_general((xdt * w_end).astype(BF16), bmb, (((0,), (0,)), ((), ())), preferred_element_type=F32)
        states.append(sp * jnp.where(half_rows, jnp.exp(last_a), jnp.exp(last_b)) + upd)
        outs.append(y_intra + y_inter + xp * jnp.where(half, col(dsk, h_a), col(dsk, h_b)))
    return outs[0], outs[1], states[0], states[1]


def _ssd_group_bwd(x0, x1, s0, s1, bm, cm, dt, cum, cum_t, dsk, grp, dy0, dy1, dyt0, dyt1, dn0, dn1):
    nt = (((1,), (1,)), ((), ()))
    lane = lax.broadcasted_iota(jnp.int32, (1, 128), 1)
    sub = lax.broadcasted_iota(jnp.int32, (128, 1), 0)
    r = lax.broadcasted_iota(jnp.int32, (Q, Q), 0)
    c = lax.broadcasted_iota(jnp.int32, (Q, Q), 1)
    future = jnp.where(r >= c, 0.0, -jnp.inf)
    future_t = jnp.where(c >= r, 0.0, -jnp.inf)
    half = lane < 64
    half_rows = sub < 64
    bmb = bm.astype(BF16)
    cmb = cm.astype(BF16)
    cb = lax.dot_general(cmb, bmb, nt, preferred_element_type=F32)
    cbt = lax.dot_general(bmb, cmb, nt, preferred_element_type=F32)

    def col(v, h):
        return jnp.sum(jnp.where(lane == h, v, 0.0), axis=1, keepdims=True)

    def row(v, h):
        return jnp.sum(jnp.where(sub == h, v, 0.0), axis=0, keepdims=True)

    def lane_sums(v, sel):
        hi = v.astype(BF16)
        lo = (v - hi.astype(F32)).astype(BF16)
        return jnp.dot(hi, sel, preferred_element_type=F32) + jnp.dot(lo, sel, preferred_element_type=F32)

    def head_sum(v, mask):
        return jnp.sum(jnp.where(mask, v, 0.0), axis=1, keepdims=True)

    src = lax.broadcasted_iota(jnp.int32, (128, 128), 0)
    dst = lax.broadcasted_iota(jnp.int32, (128, 128), 1)

    dcb = jnp.zeros((Q, Q), F32)
    dcbt = jnp.zeros((Q, Q), F32)
    dbm = jnp.zeros((Q, 128), F32)
    dcm = jnp.zeros((Q, 128), F32)
    ddt = jnp.zeros((Q, 128), F32)
    dcum = jnp.zeros((Q, 128), F32)
    dcumt = jnp.zeros((128, Q), F32)
    ddsk = jnp.zeros((1, 128), F32)
    dlast = jnp.zeros((1, 128), F32)
    dxs, dss = [], []
    for p, (xp, sp, dy, dyt, dns) in enumerate(((x0, s0, dy0, dyt0, dn0), (x1, s1, dy1, dyt1, dn1))):
        h_a = 4 * grp + 2 * p
        h_b = h_a + 1
        dt_a, dt_b = col(dt, h_a), col(dt, h_b)
        cum_a, cum_b = col(cum, h_a), col(cum, h_b)
        row_a, row_b = row(cum_t, h_a), row(cum_t, h_b)
        last_a = jnp.sum(jnp.where(sub == Q - 1, cum_a, 0.0), axis=0, keepdims=True)
        last_b = jnp.sum(jnp.where(sub == Q - 1, cum_b, 0.0), axis=0, keepdims=True)
        dtp = jnp.where(half, dt_a, dt_b)
        xdt = xp * dtp
        xdtb = xdt.astype(BF16)
        l_a, l_b = jnp.exp(cum_a - row_a + future), jnp.exp(cum_b - row_b + future)
        lt_a, lt_b = jnp.exp(row_a - cum_a + future_t), jnp.exp(row_b - cum_b + future_t)
        dya = jnp.where(half, dy, 0.0)
        dya_b, dyb_b = dya.astype(BF16), (dy - dya).astype(BF16)
        dm_a = lax.dot_general(dya_b, xdtb, nt, preferred_element_type=F32)
        dm_b = lax.dot_general(dyb_b, xdtb, nt, preferred_element_type=F32)
        dmt_a = lax.dot_general(xdtb, dya_b, nt, preferred_element_type=F32)
        dmt_b = lax.dot_general(xdtb, dyb_b, nt, preferred_element_type=F32)
        dxdt = (jnp.dot((cbt * lt_a).astype(BF16), dya_b, preferred_element_type=F32)
                + jnp.dot((cbt * lt_b).astype(BF16), dyb_b, preferred_element_type=F32))
        pair_sel = jnp.where(dst == jnp.where(src < 64, h_a, h_b), 1.0, 0.0).astype(BF16)
        to_a = jnp.where(dst == h_a, 1.0, 0.0).astype(BF16)
        to_b = jnp.where(dst == h_b, 1.0, 0.0).astype(BF16)
        dseg_a, dseg_b = dm_a * (cb * l_a), dm_b * (cb * l_b)
        dcum = dcum + lane_sums(dseg_a, to_a) + lane_sums(dseg_b, to_b)
        drow_a = -jnp.sum(dseg_a, axis=0, keepdims=True)
        drow_b = -jnp.sum(dseg_b, axis=0, keepdims=True)
        dcb = dcb + dm_a * l_a + dm_b * l_b
        dcbt = dcbt + dmt_a * lt_a + dmt_b * lt_b
        spb = sp.astype(BF16)
        y0 = lax.dot_general(cmb, spb, nt, preferred_element_type=F32)
        dy0s = dy * jnp.where(half, jnp.exp(cum_a), jnp.exp(cum_b))
        g = dy0s * y0
        dcm = dcm + jnp.dot(dy0s.astype(BF16), spb, preferred_element_type=F32)
        dsp = jnp.dot((dyt * jnp.where(half_rows, jnp.exp(row_a), jnp.exp(row_b))).astype(BF16), cmb, preferred_element_type=F32)
        wp = jnp.where(half, jnp.exp(last_a - cum_a), jnp.exp(last_b - cum_b))
        xw = xdt * wp
        dnsb = dns.astype(BF16)
        dxw = lax.dot_general(bmb, dnsb, nt, preferred_element_type=F32)
        dbm = dbm + jnp.dot(xw.astype(BF16), dnsb, preferred_element_type=F32)
        dxdt = dxdt + dxw * wp
        gw = dxw * xw
        dcum = dcum + lane_sums(g - gw, pair_sel)
        el_a, el_b = jnp.exp(last_a), jnp.exp(last_b)
        dsp = dsp + dns * jnp.where(half_rows, el_a, el_b)
        gs = dns * sp
        gs_all = jnp.sum(gs, axis=0, keepdims=True)
        gs_a = jnp.sum(jnp.sum(jnp.where(half_rows, gs, 0.0), axis=0, keepdims=True), axis=1, keepdims=True)
        gs_b = jnp.sum(gs_all, axis=1, keepdims=True) - gs_a
        gw_cols = jnp.sum(gw, axis=0, keepdims=True)
        dlast_a = head_sum(gw_cols, half) + el_a * gs_a
        dlast_b = head_sum(gw_cols, ~half) + el_b * gs_b
        dlast = dlast + jnp.where(lane == h_a, dlast_a, 0.0) + jnp.where(lane == h_b, dlast_b, 0.0)
        ddt = ddt + lane_sums(dxdt * xp, pair_sel)
        gk_cols = jnp.sum(dy * xp, axis=0, keepdims=True)
        dxs.append(dxdt * dtp + dy * jnp.where(half, col(dsk, h_a), col(dsk, h_b)))
        dss.append(dsp)
        dcumt = dcumt + jnp.where(sub == h_a, drow_a, 0.0) + jnp.where(sub == h_b, drow_b, 0.0)
        ddsk = ddsk + jnp.where(lane == h_a, head_sum(gk_cols, half), 0.0) + jnp.where(lane == h_b, head_sum(gk_cols, ~half), 0.0)
    dcum = dcum + jnp.where(sub == Q - 1, dlast, 0.0)
    dcm = dcm + jnp.dot(dcb.astype(BF16), bmb, preferred_element_type=F32)
    dbm = dbm + jnp.dot(dcbt.astype(BF16), cmb, preferred_element_type=F32)
    return dxs[0], dxs[1], dss[0], dss[1], dbm, dcm, ddt, dcum, dcumt, ddsk


class _Comm:
    def __init__(self, ins, outs, sems, start, finish):
        self.ins, self.outs, self.sems, self.start, self.finish = list(ins), list(outs), list(sems), start, finish
        self.middle = None


def _matmul(a, b, mode, *, name, out_dtypes=(F32,), epi=None, epi_ins=(), tm=1024, tn=1024, tk=1024, comm=None):
    if mode == "nn":
        (m, k), n = a.shape, b.shape[1]
    elif mode == "nt":
        (m, k), n = a.shape, b.shape[0]
    else:
        (k, m), n = a.shape, b.shape[1]
    tm, tn, tk = _pick(m, (tm, 512, 256, 128)), _pick(n, (tn, 1152, 1024, 512, 384, 256, 128)), _pick(k, (tk, 1152, 1024, 512, 256, 128))
    nk = k // tk
    if mode == "nn":
        a_spec = pl.BlockSpec((tm, tk), lambda i, j, kk: (i, kk))
        b_spec = pl.BlockSpec((tk, tn), lambda i, j, kk: (kk, j))
        dims = (((1,), (0,)), ((), ()))
    elif mode == "nt":
        a_spec = pl.BlockSpec((tm, tk), lambda i, j, kk: (i, kk))
        b_spec = pl.BlockSpec((tn, tk), lambda i, j, kk: (j, kk))
        dims = (((1,), (1,)), ((), ()))
    else:
        a_spec = pl.BlockSpec((tk, tm), lambda i, j, kk: (kk, i))
        b_spec = pl.BlockSpec((tk, tn), lambda i, j, kk: (kk, j))
        dims = (((0,), (0,)), ((), ()))
    o_spec = pl.BlockSpec((tm, tn), lambda i, j, kk: (i, j))
    n_epi, n_out = len(epi_ins), len(out_dtypes)
    n_ci, n_co, n_cs = (len(comm.ins), len(comm.outs), len(comm.sems)) if comm is not None else (0, 0, 0)
    grid = (m // tm, n // tn, nk)

    def body(*refs):
        a_ref, b_ref = refs[0], refs[1]
        e_refs = refs[2:2 + n_epi]
        ci_refs = refs[2 + n_epi:2 + n_epi + n_ci]
        o_refs = refs[2 + n_epi + n_ci:2 + n_epi + n_ci + n_out]
        co_refs = refs[2 + n_epi + n_ci + n_out:2 + n_epi + n_ci + n_out + n_co]
        acc_ref = refs[2 + n_epi + n_ci + n_out + n_co]
        cs_refs = refs[3 + n_epi + n_ci + n_out + n_co:]
        if comm is not None:
            ids = [pl.program_id(d) for d in range(3)]
            pl.when((ids[0] == 0) & (ids[1] == 0) & (ids[2] == 0))(lambda: comm.start(ci_refs, co_refs, cs_refs))
            if comm.middle is not None:
                pl.when((ids[0] == (5 * grid[0]) // 8) & (ids[1] == 0) & (ids[2] == 0))(
                    lambda: comm.middle(ci_refs, co_refs, cs_refs))

        def finish(acc):
            outs = epi(acc, *[e[...] for e in e_refs]) if epi is not None else (acc,)
            for o_ref, val in zip(o_refs, outs):
                o_ref[...] = val.astype(o_ref.dtype)

        part = lax.dot_general(a_ref[...], b_ref[...], dims, preferred_element_type=F32)
        if nk == 1:
            finish(part)
        else:
            kk = pl.program_id(2)

            @pl.when(kk == 0)
            def _():
                acc_ref[...] = part

            @pl.when(kk > 0)
            def _():
                acc_ref[...] += part

            @pl.when(kk == nk - 1)
            def _():
                finish(acc_ref[...])

        if comm is not None:
            pl.when((ids[0] == grid[0] - 1) & (ids[1] == grid[1] - 1) & (ids[2] == grid[2] - 1))(
                lambda: comm.finish(ci_refs, co_refs, cs_refs))

    extra_in = comm.ins if comm is not None else []
    extra_out = comm.outs if comm is not None else []
    extra_sems = comm.sems if comm is not None else []
    return pl.pallas_call(
        body, name=name, grid=grid,
        in_specs=[a_spec, b_spec] + [o_spec] * n_epi + [ANY] * n_ci,
        out_specs=[o_spec] * n_out + [ANY] * n_co,
        out_shape=[jax.ShapeDtypeStruct((m, n), dt) for dt in out_dtypes] + extra_out,
        scratch_shapes=[pltpu.VMEM((tm, tn) if nk > 1 else (8, 128), F32)] + extra_sems,
        compiler_params=pltpu.CompilerParams(
            dimension_semantics=("arbitrary",) * 3 if comm is not None else ("parallel", "parallel", "arbitrary")),
    )(a, b, *epi_ins, *extra_in)


def _row_tile(s):
    return _pick(s, (512, 256, 128))


def _prenorm(x, sc, sh):
    s = x.shape[0]
    tm = _row_tile(s)

    def body(x_ref, sc_ref, sh_ref, h_ref):
        h_ref[...] = _modnorm(x_ref[...], sc_ref[...], sh_ref[...]).astype(BF16)

    row = pl.BlockSpec((tm, D), lambda i: (i, 0))
    return pl.pallas_call(body, name="prenorm", grid=(s // tm,), in_specs=[row, _full((1, D)), _full((1, D))],
                          out_specs=row, out_shape=jax.ShapeDtypeStruct((s, D), BF16))(x, sc, sh)


def _prenorm_bwd(x, sc, sh, dh, dx_res):
    s = x.shape[0]
    tm = _row_tile(s)

    def body(x_ref, sc_ref, sh_ref, dh_ref, dr_ref, dx_ref, dsc_ref, dsh_ref):
        _, vjp = jax.vjp(_modnorm, x_ref[...], sc_ref[...], sh_ref[...])
        dx, dsc, dsh = vjp(dh_ref[...].astype(F32))
        dx_ref[...] = dr_ref[...] + dx

        @pl.when(pl.program_id(0) == 0)
        def _():
            dsc_ref[...] = jnp.zeros_like(dsc_ref)
            dsh_ref[...] = jnp.zeros_like(dsh_ref)

        dsc_ref[...] += dsc
        dsh_ref[...] += dsh

    row = pl.BlockSpec((tm, D), lambda i: (i, 0))
    vec = _full((1, D))
    return pl.pallas_call(
        body, name="prenorm_bwd", grid=(s // tm,), in_specs=[row, vec, vec, row, row], out_specs=[row, vec, vec],
        out_shape=[jax.ShapeDtypeStruct((s, D), F32), jax.ShapeDtypeStruct((1, D), F32), jax.ShapeDtypeStruct((1, D), F32)],
    )(x, sc, sh, dh, dx_res)


def _resid_norm(x, mo, g1, sc, sh):
    s = x.shape[0]
    tm = _row_tile(s)

    def body(x_ref, mo_ref, g_ref, sc_ref, sh_ref, x1_ref, h_ref):
        x1 = x_ref[...] + g_ref[...] * mo_ref[...].astype(F32)
        x1_ref[...] = x1
        h_ref[...] = _modnorm(x1, sc_ref[...], sh_ref[...]).astype(BF16)

    row = pl.BlockSpec((tm, D), lambda i: (i, 0))
    vec = _full((1, D))
    return pl.pallas_call(
        body, name="resid_norm", grid=(s // tm,), in_specs=[row, row, vec, vec, vec], out_specs=[row, row],
        out_shape=[jax.ShapeDtypeStruct((s, D), F32), jax.ShapeDtypeStruct((s, D), BF16)],
    )(x, mo, g1, sc, sh)


def _resid_norm_bwd(x1, mo, g1, sc, sh, dh, dx2):
    s = x1.shape[0]
    tm = _row_tile(s)

    def body(x1_ref, mo_ref, g_ref, sc_ref, sh_ref, dh_ref, dx2_ref, dx1_ref, dmo_ref, dg_ref, dsc_ref, dsh_ref):
        _, vjp = jax.vjp(_modnorm, x1_ref[...], sc_ref[...], sh_ref[...])
        dx, dsc, dsh = vjp(dh_ref[...].astype(F32))
        dx1 = dx2_ref[...] + dx
        dx1_ref[...] = dx1
        dmo_ref[...] = (dx1 * g_ref[...]).astype(BF16)

        @pl.when(pl.program_id(0) == 0)
        def _():
            dg_ref[...] = jnp.zeros_like(dg_ref)
            dsc_ref[...] = jnp.zeros_like(dsc_ref)
            dsh_ref[...] = jnp.zeros_like(dsh_ref)

        dg_ref[...] += jnp.sum(dx1 * mo_ref[...].astype(F32), axis=0, keepdims=True)
        dsc_ref[...] += dsc
        dsh_ref[...] += dsh

    row = pl.BlockSpec((tm, D), lambda i: (i, 0))
    vec = _full((1, D))
    vshape = jax.ShapeDtypeStruct((1, D), F32)
    return pl.pallas_call(
        body, name="resid_norm_bwd", grid=(s // tm,), in_specs=[row, row, vec, vec, vec, row, row],
        out_specs=[row, row, vec, vec, vec],
        out_shape=[jax.ShapeDtypeStruct((s, D), F32), jax.ShapeDtypeStruct((s, D), BF16), vshape, vshape, vshape],
    )(x1, mo, g1, sc, sh, dh, dx2)


def _loss_head(x1, fo, g2, fw, t):
    s = x1.shape[0]
    tm = _row_tile(s)

    def body(x1_ref, fo_ref, g_ref, fw_ref, t_ref, loss_ref, dx_ref, dfo_ref, dg_ref, dfw_ref):
        loss, (dx1, dfo, dg, dfw) = jax.value_and_grad(_loss_tile, argnums=(0, 1, 2, 3))(
            x1_ref[...], fo_ref[...].astype(F32), g_ref[...], fw_ref[...], t_ref[...])
        dx_ref[...] = dx1
        dfo_ref[...] = dfo.astype(BF16)

        @pl.when(pl.program_id(0) == 0)
        def _():
            loss_ref[...] = jnp.zeros_like(loss_ref)
            dg_ref[...] = jnp.zeros_like(dg_ref)
            dfw_ref[...] = jnp.zeros_like(dfw_ref)

        loss_ref[...] += jnp.full(loss_ref.shape, loss, F32)
        dg_ref[...] += dg
        dfw_ref[...] += dfw

    row = pl.BlockSpec((tm, D), lambda i: (i, 0))
    vec = _full((1, D))
    vshape = jax.ShapeDtypeStruct((1, D), F32)
    return pl.pallas_call(
        body, name="loss_head", grid=(s // tm,), in_specs=[row, row, vec, vec, row],
        out_specs=[_full((8, 128)), row, row, vec, vec],
        out_shape=[jax.ShapeDtypeStruct((8, 128), F32), jax.ShapeDtypeStruct((s, D), F32),
                   jax.ShapeDtypeStruct((s, D), BF16), vshape, vshape],
    )(x1, fo, g2, fw, t)


def _sgu_fwd(proj, norm_w, ws_b, bs_t):
    s = proj.shape[0]
    tm = _pick(s, (256, 128))

    def body(u_ref, v_ref, w_ref, ws_ref, bs_ref, y_ref):
        ug, vn = _sgu_pre(u_ref[...].astype(F32), v_ref[...].astype(F32), w_ref[...])
        vnb = vn.astype(BF16)
        for c in range(tm // Q):
            r = slice(c * Q, (c + 1) * Q)
            for g in range(GM_G):
                cs = slice(g * 128, (g + 1) * 128)
                sv = jnp.dot(ws_ref[g], vnb[r, cs], preferred_element_type=F32) + bs_ref[:, g:g + 1]
                y_ref[r, cs] = (ug[r, cs] * sv).astype(BF16)

    return pl.pallas_call(
        body, name="sgu_fwd", grid=(s // tm,),
        in_specs=[pl.BlockSpec((tm, D), lambda i: (i, 0)), pl.BlockSpec((tm, D), lambda i: (i, 1)),
                  _full((1, D)), _full((GM_G, Q, Q)), _full((Q, GM_G))],
        out_specs=pl.BlockSpec((tm, D), lambda i: (i, 0)),
        out_shape=jax.ShapeDtypeStruct((s, D), BF16),
    )(proj, proj, norm_w, ws_b, bs_t)


def _sgu_bwd(proj, norm_w, ws_b, bs_t, dy, dproj):
    s = proj.shape[0]
    tm = _pick(s, (256, 128))

    def body(u_ref, v_ref, w_ref, ws_ref, bs_ref, dy_ref, _, duv_ref, dw_ref, dws_ref, dbs_ref, dug_scr, dvn_scr):
        @pl.when(pl.program_id(0) == 0)
        def _():
            dw_ref[...] = jnp.zeros_like(dw_ref)
            dws_ref[...] = jnp.zeros_like(dws_ref)
            dbs_ref[...] = jnp.zeros_like(dbs_ref)

        (ug, vn), vjp = jax.vjp(_sgu_pre, u_ref[...].astype(F32), v_ref[...].astype(F32), w_ref[...])
        vnb = vn.astype(BF16)
        dy = dy_ref[...].astype(F32)
        causal = _tril(Q).astype(F32)
        for c in range(tm // Q):
            r = slice(c * Q, (c + 1) * Q)
            for g in range(GM_G):
                cs = slice(g * 128, (g + 1) * 128)
                blk = vnb[r, cs]
                sv = jnp.dot(ws_ref[g], blk, preferred_element_type=F32) + bs_ref[:, g:g + 1]
                dug_scr[r, cs] = dy[r, cs] * sv
                dsv = dy[r, cs] * ug[r, cs]
                dsvb = dsv.astype(BF16)
                dws_ref[g] += causal * lax.dot_general(dsvb, blk, (((1,), (1,)), ((), ())), preferred_element_type=F32)
                dbs_ref[:, g:g + 1] += jnp.sum(dsv, axis=1, keepdims=True)
                dvn_scr[r, cs] = lax.dot_general(ws_ref[g], dsvb, (((0,), (0,)), ((), ())), preferred_element_type=F32)
        du, dv, dw = vjp((dug_scr[...], dvn_scr[...]))
        duv_ref[:, :D] = du.astype(BF16)
        duv_ref[:, D:] = dv.astype(BF16)
        dw_ref[...] += dw

    return pl.pallas_call(
        body, name="sgu_bwd", grid=(s // tm,),
        in_specs=[pl.BlockSpec((tm, D), lambda i: (i, 0)), pl.BlockSpec((tm, D), lambda i: (i, 1)),
                  _full((1, D)), _full((GM_G, Q, Q)), _full((Q, GM_G)), pl.BlockSpec((tm, D), lambda i: (i, 0)), ANY],
        out_specs=[pl.BlockSpec((tm, 2 * D), lambda i: (i, 0)), _full((1, D)), _full((GM_G, Q, Q)), _full((Q, GM_G))],
        out_shape=[jax.ShapeDtypeStruct(dproj.shape, BF16), jax.ShapeDtypeStruct((1, D), F32),
                   jax.ShapeDtypeStruct((GM_G, Q, Q), F32), jax.ShapeDtypeStruct((Q, GM_G), F32)],
        scratch_shapes=[pltpu.VMEM((tm, D), F32), pltpu.VMEM((tm, D), F32)],
        input_output_aliases={6: 0},
    )(proj, proj, norm_w, ws_b, bs_t, dy, dproj)


def _gatenorm_fwd(y_ssd, proj, norm_w):
    s = y_ssd.shape[0]
    tm = _pick(s, (256, 128))
    gw = DI // NGRP

    def body(y_ref, z_ref, w_ref, o_ref):
        for g in range(NGRP):
            cs = slice(g * gw, (g + 1) * gw)
            o_ref[:, cs] = _gatenorm(y_ref[:, cs], z_ref[:, cs].astype(F32), w_ref[:, cs]).astype(BF16)

    return pl.pallas_call(
        body, name="gatenorm_fwd", grid=(s // tm,),
        in_specs=[pl.BlockSpec((tm, DI), lambda i: (i, 0)), pl.BlockSpec((tm, DI), lambda i: (i, OFF_Z // DI)), _full((1, DI))],
        out_specs=pl.BlockSpec((tm, DI), lambda i: (i, 0)),
        out_shape=jax.ShapeDtypeStruct((s, DI), BF16),
    )(y_ssd, proj, norm_w)


def _gatenorm_bwd(y_ssd, proj, norm_w, dyb, dproj):
    s = y_ssd.shape[0]
    tm = _pick(s, (256, 128))
    gw = DI // NGRP

    def body(y_ref, z_ref, w_ref, d_ref, _, dy_ref, dyt_ref, dz_ref, dw_ref):
        @pl.when(pl.program_id(0) == 0)
        def _():
            dw_ref[...] = jnp.zeros_like(dw_ref)

        for g in range(NGRP):
            cs = slice(g * gw, (g + 1) * gw)
            _, vjp = jax.vjp(_gatenorm, y_ref[:, cs], z_ref[:, cs].astype(F32), w_ref[:, cs])
            dy, dz, dw = vjp(d_ref[:, cs].astype(F32))
            dy_ref[:, cs] = dy.astype(BF16)
            dyt_ref[cs, :] = dy.T.astype(BF16)
            dz_ref[:, cs] = dz.astype(BF16)
            dw_ref[:, cs] += dw

    blk = pl.BlockSpec((tm, DI), lambda i: (i, 0))
    zblk = pl.BlockSpec((tm, DI), lambda i: (i, OFF_Z // DI))
    return pl.pallas_call(
        body, name="gatenorm_bwd", grid=(s // tm,),
        in_specs=[blk, zblk, _full((1, DI)), blk, ANY],
        out_specs=[blk, pl.BlockSpec((DI, tm), lambda i: (0, i)), zblk, _full((1, DI))],
        out_shape=[jax.ShapeDtypeStruct((s, DI), BF16), jax.ShapeDtypeStruct((DI, s), BF16),
                   jax.ShapeDtypeStruct(dproj.shape, BF16), jax.ShapeDtypeStruct((1, DI), F32)],
        input_output_aliases={4: 2},
    )(y_ssd, proj, norm_w, dyb, dproj)


def _mix_fwd(proj, ba, bb):
    s = proj.shape[0]
    tm = _row_tile(s)
    gb0 = OFF_GA // D

    def body(ga_ref, gb_ref, ba_ref, bb_ref, o_ref):
        o_ref[...] = _mix(ga_ref[...].astype(F32), gb_ref[...].astype(F32), ba_ref[...].astype(F32),
                          bb_ref[...].astype(F32)).astype(BF16)

    row = pl.BlockSpec((tm, D), lambda i: (i, 0))
    return pl.pallas_call(
        body, name="mix_fwd", grid=(s // tm,),
        in_specs=[pl.BlockSpec((tm, D), lambda i: (i, gb0)), pl.BlockSpec((tm, D), lambda i: (i, gb0 + 1)), row, row],
        out_specs=row, out_shape=jax.ShapeDtypeStruct((s, D), BF16),
    )(proj, proj, ba, bb)


def _mix_bwd(proj, ba, bb, dmixed):
    s = proj.shape[0]
    tm = _row_tile(s)
    gb0 = OFF_GA // D

    def body(ga_ref, gb_ref, ba_ref, bb_ref, d_ref, dg_ref, dba_ref, dbb_ref):
        _, vjp = jax.vjp(_mix, ga_ref[...].astype(F32), gb_ref[...].astype(F32), ba_ref[...].astype(F32),
                         bb_ref[...].astype(F32))
        dga, dgb, dba, dbb = vjp(d_ref[...].astype(F32))
        dg_ref[:, :D] = dga.astype(BF16)
        dg_ref[:, D:] = dgb.astype(BF16)
        dba_ref[...] = dba.astype(BF16)
        dbb_ref[...] = dbb.astype(BF16)

    row = pl.BlockSpec((tm, D), lambda i: (i, 0))
    return pl.pallas_call(
        body, name="mix_bwd", grid=(s // tm,),
        in_specs=[pl.BlockSpec((tm, D), lambda i: (i, gb0)), pl.BlockSpec((tm, D), lambda i: (i, gb0 + 1)), row, row, row],
        out_specs=[pl.BlockSpec((tm, 2 * D), lambda i: (i, OFF_GA // (2 * D))), row, row],
        out_shape=[jax.ShapeDtypeStruct((s, W_IN_R), BF16), jax.ShapeDtypeStruct((s, D), BF16), jax.ShapeDtypeStruct((s, D), BF16)],
    )(proj, proj, ba, bb, dmixed)


CONV_TC = 1024


def _conv_fwd(proj, conv_w, conv_b):
    s = proj.shape[0]
    tm = _row_tile(s)
    cb0 = OFF_XBC // CONV_TC

    def body(x_ref, halo_ref, w_ref, b_ref, o_ref):
        halo = jnp.where(pl.program_id(0) > 0, halo_ref[...].astype(F32)[8:, :], 0.0)
        ext = jnp.concatenate([halo, x_ref[...].astype(F32)], axis=0)
        acc = jnp.broadcast_to(b_ref[...], (tm, CONV_TC))
        for k in range(4):
            shifted = ext if k == 3 else pltpu.roll(ext, 3 - k, 0)
            acc = acc + w_ref[k:k + 1, :] * shifted[8:, :]
        o_ref[...] = jax.nn.silu(acc)

    return pl.pallas_call(
        body, name="conv_fwd", grid=(s // tm, CONV // CONV_TC),
        in_specs=[pl.BlockSpec((tm, CONV_TC), lambda i, j: (i, cb0 + j)),
                  pl.BlockSpec((16, CONV_TC), lambda i, j: (jnp.maximum(i * (tm // 16) - 1, 0), cb0 + j)),
                  pl.BlockSpec((4, CONV_TC), lambda i, j: (0, j)), pl.BlockSpec((1, CONV_TC), lambda i, j: (0, j))],
        out_specs=pl.BlockSpec((tm, CONV_TC), lambda i, j: (i, j)),
        out_shape=jax.ShapeDtypeStruct((s, CONV), F32),
    )(proj, proj, conv_w, conv_b)


def _conv_bwd(proj, conv_w, conv_b, dact, dproj, col0, name):
    s, width = dact.shape
    tm = _row_tile(s)
    nt = s // tm
    c0 = col0 // CONV_TC
    cb0 = OFF_XBC // CONV_TC + c0

    def body(x_ref, prev_ref, next_ref, d_ref, dnext_ref, w_ref, b_ref, _, dx_ref, dw_ref, db_ref):
        i = pl.program_id(1)
        prev = jnp.where(i > 0, prev_ref[...].astype(F32)[8:, :], 0.0)
        ext = jnp.concatenate([prev, x_ref[...].astype(F32), next_ref[...].astype(F32)[:8, :]], axis=0)
        dext = jnp.concatenate([d_ref[...], jnp.where(i < nt - 1, dnext_ref[...], 0.0)], axis=0)
        pre = jnp.broadcast_to(b_ref[...], (tm + 8, CONV_TC))
        taps = []
        for k in range(4):
            shifted = (ext if k == 3 else pltpu.roll(ext, 3 - k, 0))[8:, :]
            taps.append(shifted)
            pre = pre + w_ref[k:k + 1, :] * shifted
        sig = jax.nn.sigmoid(pre)
        dpre = dext * (sig * (1.0 + pre * (1.0 - sig)))
        dx = jnp.zeros((tm, CONV_TC), F32)
        for k in range(4):
            shifted = dpre if k == 3 else pltpu.roll(dpre, tm + 8 - (3 - k), 0)
            dx = dx + w_ref[k:k + 1, :] * shifted[:tm, :]
        dx_ref[...] = dx.astype(BF16)

        @pl.when(i == 0)
        def _():
            dw_ref[...] = jnp.zeros_like(dw_ref)
            db_ref[...] = jnp.zeros_like(db_ref)

        dtile = dpre[:tm, :]
        for k in range(4):
            dw_ref[k:k + 1, :] += jnp.sum(dtile * taps[k][:tm, :], axis=0, keepdims=True)
        db_ref[...] += jnp.sum(dtile, axis=0, keepdims=True)

    r8, r16 = tm // 8, tm // 16
    return pl.pallas_call(
        body, name=name, grid=(width // CONV_TC, nt),
        in_specs=[pl.BlockSpec((tm, CONV_TC), lambda j, i: (i, cb0 + j)),
                  pl.BlockSpec((16, CONV_TC), lambda j, i: (jnp.maximum(i * r16 - 1, 0), cb0 + j)),
                  pl.BlockSpec((16, CONV_TC), lambda j, i: (jnp.minimum((i + 1) * r16, nt * r16 - 1), cb0 + j)),
                  pl.BlockSpec((tm, CONV_TC), lambda j, i: (i, j)),
                  pl.BlockSpec((8, CONV_TC), lambda j, i: (jnp.minimum((i + 1) * r8, nt * r8 - 1), j)),
                  pl.BlockSpec((4, CONV_TC), lambda j, i: (0, c0 + j)), pl.BlockSpec((1, CONV_TC), lambda j, i: (0, c0 + j)), ANY],
        out_specs=[pl.BlockSpec((tm, CONV_TC), lambda j, i: (i, cb0 + j)),
                   pl.BlockSpec((4, CONV_TC), lambda j, i: (0, j)), pl.BlockSpec((1, CONV_TC), lambda j, i: (0, j))],
        out_shape=[jax.ShapeDtypeStruct(dproj.shape, BF16), jax.ShapeDtypeStruct((4, width), F32), jax.ShapeDtypeStruct((1, width), F32)],
        input_output_aliases={7: 0},
    )(proj, proj, proj, dact, dact, conv_w, conv_b, dproj)


def _dt_fwd(dt_raw, dtb, alog):
    s = dt_raw.shape[0]
    nc = s // Q

    def body(r_ref, b_ref, a_ref, dt_ref, cum_ref, cumt_ref):
        dt, cum, cum_t = _dt_prep(r_ref[...], b_ref[...], a_ref[...])
        dt_ref[...] = dt
        cum_ref[...] = cum
        cumt_ref[...] = cum_t

    blk = pl.BlockSpec((Q, 128), lambda n: (n, 0))
    return pl.pallas_call(
        body, name="dt_fwd", grid=(nc,),
        in_specs=[blk, _full((1, 128)), _full((1, 128))],
        out_specs=[blk, blk, pl.BlockSpec((None, 128, Q), lambda n: (n, 0, 0))],
        out_shape=[jax.ShapeDtypeStruct((s, 128), F32), jax.ShapeDtypeStruct((s, 128), F32), jax.ShapeDtypeStruct((nc, 128, Q), F32)],
    )(dt_raw, dtb, alog)


def _dt_bwd(dt_raw, dtb, alog, ddt, dcum, dcumt, dproj):
    s = dt_raw.shape[0]
    nc = s // Q

    def body(r_ref, b_ref, a_ref, ddt_ref, dcum_ref, dcumt_ref, _, dr_ref, db_ref, da_ref):
        _, vjp = jax.vjp(_dt_prep, r_ref[...], b_ref[...], a_ref[...])
        dr, db, da = vjp((ddt_ref[...], dcum_ref[...], dcumt_ref[...]))
        dr_ref[...] = dr.astype(BF16)

        @pl.when(pl.program_id(0) == 0)
        def _():
            db_ref[...] = jnp.zeros_like(db_ref)
            da_ref[...] = jnp.zeros_like(da_ref)

        db_ref[...] += db
        da_ref[...] += da

    blk = pl.BlockSpec((Q, 128), lambda n: (n, 0))
    pblk = pl.BlockSpec((Q, 128), lambda n: (n, OFF_DT // 128))
    return pl.pallas_call(
        body, name="dt_bwd", grid=(nc,),
        in_specs=[blk, _full((1, 128)), _full((1, 128)), blk, blk, pl.BlockSpec((None, 128, Q), lambda n: (n, 0, 0)), ANY],
        out_specs=[pblk, _full((1, 128)), _full((1, 128))],
        out_shape=[jax.ShapeDtypeStruct(dproj.shape, BF16), jax.ShapeDtypeStruct((1, 128), F32), jax.ShapeDtypeStruct((1, 128), F32)],
        input_output_aliases={6: 0},
    )(dt_raw, dtb, alog, ddt, dcum, dcumt, dproj)


def _ssd_specs(chunk_of):
    gs = SSD_GPS
    xs = pl.BlockSpec((Q, 256 * gs), lambda n, g: (chunk_of(n), g))
    bm = pl.BlockSpec((Q, 128 * gs), lambda n, g: (chunk_of(n), DI // (128 * gs) + g))
    cm = pl.BlockSpec((Q, 128 * gs), lambda n, g: (chunk_of(n), (DI + D) // (128 * gs) + g))
    per_chunk = pl.BlockSpec((Q, 128), lambda n, g: (chunk_of(n), 0))
    cum_t = pl.BlockSpec((None, 128, Q), lambda n, g: (chunk_of(n), 0, 0))
    state = pl.BlockSpec((None, 256 * gs, 128), lambda n, g: (chunk_of(n), g, 0))
    vec = pl.BlockSpec((1, 128), lambda n, g: (0, 0))
    return xs, bm, cm, per_chunk, cum_t, state, vec


SSD_GPS = 8


def _aligned(v, m):
    return v if isinstance(v, int) else pl.multiple_of(v, m)


def _ssd_fwd(xbc, dt, cum, cum_t, dsk):
    s = xbc.shape[0]
    nc = s // Q
    xs, bm, cm, per_chunk, cumt_spec, state_spec, vec = _ssd_specs(lambda n: n)
    gs = SSD_GPS

    def body(x_ref, b_ref, c_ref, dt_ref, cum_ref, cumt_ref, dsk_ref, y_ref, st_ref, carry):
        n, gstep = pl.program_id(0), (0 if gs == NGRP else pl.program_id(1))
        rows = pl.ds(_aligned(gstep * (256 * gs), 256 * gs), 256 * gs)

        @pl.when(n == 0)
        def _():
            carry[rows, :] = jnp.zeros((256 * gs, 128), F32)

        st_ref[...] = carry[rows, :]
        for k in range(gs):
            xo, so, bo = 256 * k, 256 * k, 128 * k
            y0, y1, n0, n1 = _ssd_group(
                x_ref[:, xo:xo + 128], x_ref[:, xo + 128:xo + 256], st_ref[so:so + 128, :], st_ref[so + 128:so + 256, :],
                b_ref[:, bo:bo + 128], c_ref[:, bo:bo + 128], dt_ref[...], cum_ref[...], cumt_ref[...], dsk_ref[...],
                gstep * gs + k)
            y_ref[:, xo:xo + 128] = y0
            y_ref[:, xo + 128:xo + 256] = y1
            base = gstep * (256 * gs) + so
            carry[pl.ds(_aligned(base, 128), 128), :] = n0
            carry[pl.ds(_aligned(base + 128, 128), 128), :] = n1

    return pl.pallas_call(
        body, name="ssd_fwd", grid=(nc, NGRP // gs),
        in_specs=[xs, bm, cm, per_chunk, per_chunk, cumt_spec, vec],
        out_specs=[xs, state_spec],
        out_shape=[jax.ShapeDtypeStruct((s, DI), F32), jax.ShapeDtypeStruct((nc, DI, 128), F32)],
        scratch_shapes=[pltpu.VMEM((DI, 128), F32)],
    )(xbc, xbc, xbc, dt, cum, cum_t, dsk)


def _ssd_bwd(xbc, dt, cum, cum_t, dsk, states, dy, dy_t, comm=None):
    s = xbc.shape[0]
    nc = s // Q
    xs, bm, cm, per_chunk, cumt_spec, state_spec, vec = _ssd_specs(lambda n: nc - 1 - n)
    gs = SSD_GPS
    n_ci, n_co = (len(comm.ins), len(comm.outs)) if comm is not None else (0, 0)
    steps = (nc, NGRP // gs)

    def body(*refs):
        x_ref, b_ref, c_ref, dt_ref, cum_ref, cumt_ref, dsk_ref, st_ref, dy_ref, dyt_ref = refs[:10]
        ci_refs = refs[10:10 + n_ci]
        dx_ref, db_ref, dc_ref, ddt_ref, dcum_ref, dcumt_ref, ddsk_ref = refs[10 + n_ci:17 + n_ci]
        co_refs = refs[17 + n_ci:17 + n_ci + n_co]
        carry = refs[17 + n_ci + n_co]
        cs_refs = refs[18 + n_ci + n_co:]
        n, gstep = pl.program_id(0), (0 if gs == NGRP else pl.program_id(1))
        rows = pl.ds(_aligned(gstep * (256 * gs), 256 * gs), 256 * gs)
        if comm is not None:
            pl.when((pl.program_id(0) == 0) & (pl.program_id(1) == 0))(lambda: comm.start(ci_refs, co_refs, cs_refs))

        @pl.when(n == 0)
        def _():
            carry[rows, :] = jnp.zeros((256 * gs, 128), F32)

        def zero_skip_sum():
            ddsk_ref[...] = jnp.zeros_like(ddsk_ref)

        def zero_chunk_sums():
            ddt_ref[...] = jnp.zeros_like(ddt_ref)
            dcum_ref[...] = jnp.zeros_like(dcum_ref)
            dcumt_ref[...] = jnp.zeros_like(dcumt_ref)

        if isinstance(gstep, int):
            pl.when(n == 0)(zero_skip_sum)
            zero_chunk_sums()
        else:
            pl.when((n == 0) & (gstep == 0))(zero_skip_sum)
            pl.when(gstep == 0)(zero_chunk_sums)

        for k in range(gs):
            xo, so, bo = 256 * k, 256 * k, 128 * k
            base = gstep * (256 * gs) + so
            lo = pl.ds(_aligned(base, 128), 128)
            hi = pl.ds(_aligned(base + 128, 128), 128)
            dx0, dx1, ds0, ds1, dbm, dcm, ddt, dcum, dcumt, ddsk = _ssd_group_bwd(
                x_ref[:, xo:xo + 128], x_ref[:, xo + 128:xo + 256], st_ref[so:so + 128, :], st_ref[so + 128:so + 256, :],
                b_ref[:, bo:bo + 128], c_ref[:, bo:bo + 128], dt_ref[...], cum_ref[...], cumt_ref[...], dsk_ref[...],
                gstep * gs + k, dy_ref[:, xo:xo + 128].astype(F32), dy_ref[:, xo + 128:xo + 256].astype(F32),
                dyt_ref[xo:xo + 128, :].astype(F32), dyt_ref[xo + 128:xo + 256, :].astype(F32), carry[lo, :], carry[hi, :])
            dx_ref[:, xo:xo + 128] = dx0
            dx_ref[:, xo + 128:xo + 256] = dx1
            db_ref[:, bo:bo + 128] = dbm
            dc_ref[:, bo:bo + 128] = dcm
            ddt_ref[...] += ddt
            dcum_ref[...] += dcum
            dcumt_ref[...] += dcumt
            ddsk_ref[...] += ddsk
            carry[lo, :] = ds0
            carry[hi, :] = ds1

        if comm is not None:
            pl.when((pl.program_id(0) == steps[0] - 1) & (pl.program_id(1) == steps[1] - 1))(
                lambda: comm.finish(ci_refs, co_refs, cs_refs))

    grp_blk = pl.BlockSpec((Q, 128 * gs), lambda n, g: (nc - 1 - n, g))
    extra_in = comm.ins if comm is not None else []
    extra_out = comm.outs if comm is not None else []
    extra_sems = comm.sems if comm is not None else []
    return pl.pallas_call(
        body, name="ssd_bwd", grid=steps,
        in_specs=[xs, bm, cm, per_chunk, per_chunk, cumt_spec, vec, state_spec, xs,
                  pl.BlockSpec((256 * gs, Q), lambda n, g: (g, nc - 1 - n))] + [ANY] * n_ci,
        out_specs=[xs, grp_blk, grp_blk, per_chunk, per_chunk, cumt_spec, vec] + [ANY] * n_co,
        out_shape=[jax.ShapeDtypeStruct((s, DI), F32), jax.ShapeDtypeStruct((s, D), F32), jax.ShapeDtypeStruct((s, D), F32),
                   jax.ShapeDtypeStruct((s, 128), F32), jax.ShapeDtypeStruct((s, 128), F32),
                   jax.ShapeDtypeStruct((nc, 128, Q), F32), jax.ShapeDtypeStruct((1, 128), F32)] + extra_out,
        scratch_shapes=[pltpu.VMEM((DI, 128), F32)] + extra_sems,
    )(xbc, xbc, xbc, dt, cum, cum_t, dsk, states, dy, dy_t, *extra_in)


def _adam_math(w, g, m, v):
    m2 = ADAM_B1 * m + (1.0 - ADAM_B1) * g
    v2 = ADAM_B2 * v + (1.0 - ADAM_B2) * jnp.square(g)
    m_hat = m2 / (1.0 - ADAM_B1 ** ADAM_STEP)
    v_hat = v2 / (1.0 - ADAM_B2 ** ADAM_STEP)
    delta = -ADAM_LR * (m_hat / (jnp.sqrt(v_hat) + ADAM_EPS) + ADAM_WD * w)
    return delta, m2, v2


def _adam(w, g, m, v, name):
    r, c = w.shape
    tr = r if r * c * 4 <= (1 << 20) else _pick(r, (128, 64, 32, 16, 8))

    def body(w_ref, g_ref, m_ref, v_ref, d_ref, m2_ref, v2_ref):
        d, m2, v2 = _adam_math(w_ref[...], g_ref[...], m_ref[...], v_ref[...])
        d_ref[...] = d
        m2_ref[...] = m2
        v2_ref[...] = v2

    blk = pl.BlockSpec((tr, c), lambda i: (i, 0))
    shp = jax.ShapeDtypeStruct((r, c), F32)
    return pl.pallas_call(body, name=name, grid=(r // tr,), in_specs=[blk] * 4, out_specs=[blk] * 3,
                          out_shape=[shp] * 3)(w, g, m, v)


def _sum_leading(xs, name, out_dtype=F32, tr=256):
    if isinstance(xs, tuple):
        a, b = xs
        n, r, c = a.shape
        tr = _pick(r, (tr, 128, 64, 32, 16, 8))

        def body2(a_ref, b_ref, o_ref):
            o_ref[...] = (a_ref[...].astype(F32) + b_ref[...].astype(F32)).astype(out_dtype)

        blk = pl.BlockSpec((None, tr, c), lambda s, i: (s, i, 0))
        return pl.pallas_call(body2, name=name, grid=(n, r // tr), in_specs=[blk, blk], out_specs=blk,
                              out_shape=jax.ShapeDtypeStruct((n, r, c), out_dtype))(a, b)
    n, r, c = xs.shape
    tr = r if n * r * c * 4 <= (8 << 20) else _pick(r, (tr, 128, 64, 32, 16, 8))

    def body(x_ref, o_ref):
        acc = x_ref[0].astype(F32)
        for s in range(1, n):
            acc = acc + x_ref[s].astype(F32)
        o_ref[...] = acc.astype(out_dtype)

    return pl.pallas_call(body, name=name, grid=(r // tr,), in_specs=[pl.BlockSpec((n, tr, c), lambda i: (0, i, 0))],
                          out_specs=pl.BlockSpec((tr, c), lambda i: (i, 0)),
                          out_shape=jax.ShapeDtypeStruct((r, c), out_dtype))(xs)


def _mod_fwd(c8, w_mod, b_sl):
    def body(c_ref, w_ref, b_ref, o_ref):
        o_ref[...] = jnp.dot(jax.nn.silu(c_ref[...]), w_ref[...], precision=HIGHEST, preferred_element_type=F32) + b_ref[...]

    return pl.pallas_call(body, name="mod_fwd", in_specs=[VMEM, VMEM, VMEM], out_specs=VMEM,
                          out_shape=jax.ShapeDtypeStruct((N_DEV, w_mod.shape[1]), F32))(c8, w_mod, b_sl)


def _mod_wgrad(c8, dmod):
    def body(c_ref, d_ref, o_ref):
        o_ref[...] = lax.dot_general(jax.nn.silu(c_ref[...]), d_ref[...], (((0,), (0,)), ((), ())),
                                     precision=HIGHEST, preferred_element_type=F32)

    return pl.pallas_call(body, name="mod_wgrad", in_specs=[VMEM, VMEM], out_specs=VMEM,
                          out_shape=jax.ShapeDtypeStruct((D, dmod.shape[1]), F32))(c8, dmod)


def _place():
    x, y, c = lax.axis_index("x"), lax.axis_index("y"), lax.axis_index("c")
    chips = [(1 - x, y), (x, 1 - y), (1 - x, 1 - y)]
    return x, y, c, chips


def _all_gather_small(v, name):
    r, w = v.shape

    def body(v_ref, o_ref, send_sems, recv_sems, local_sem):
        x, y, c, _ = _place()
        me = 4 * x + 2 * y + c
        own = pltpu.make_async_copy(v_ref, o_ref.at[me], local_sem)
        own.start()
        sends = []
        for k in range(1, N_DEV):
            tx = 1 - x if k & 4 else x
            ty = 1 - y if k & 2 else y
            tc = 1 - c if k & 1 else c
            peer = 4 * tx + 2 * ty + tc
            cp = pltpu.make_async_remote_copy(src_ref=v_ref, dst_ref=o_ref.at[me], send_sem=send_sems.at[k - 1],
                                              recv_sem=recv_sems.at[k - 1], device_id=(tx, ty, tc), device_id_type=MESH)
            cp.start()
            sends.append((cp, peer, (tx, ty, tc)))
        for k, (cp, peer, dev) in enumerate(sends):
            pltpu.make_async_remote_copy(src_ref=v_ref, dst_ref=o_ref.at[peer], send_sem=send_sems.at[k],
                                         recv_sem=recv_sems.at[k], device_id=dev, device_id_type=MESH).wait_recv()
        for cp, _, _ in sends:
            cp.wait_send()
        own.wait()

    return pl.pallas_call(
        body, name=name, in_specs=[VMEM], out_specs=VMEM, out_shape=jax.ShapeDtypeStruct((N_DEV, r, w), v.dtype),
        scratch_shapes=[pltpu.SemaphoreType.DMA((N_DEV - 1,)), pltpu.SemaphoreType.DMA((N_DEV - 1,)), pltpu.SemaphoreType.DMA],
    )(v)


def _all_gather_weights(shards):
    n = len(shards)
    nsem = n * 3 * AG_CHUNKS

    def body(*refs):
        srcs, dsts = refs[:n], refs[n:2 * n]
        send_sems, recv_sems, fwd_send_sems, fwd_recv_sems = refs[2 * n:]
        x, y, c, chips = _place()
        q = 2 * x + y
        sibling = (x, y, 1 - c)
        sends = []
        for r in range(AG_CHUNKS):
            for t in range(n):
                hc = srcs[t].shape[0] // 2 // AG_CHUNKS
                rows = pl.ds(c * (hc * AG_CHUNKS) + r * hc, hc)
                for j, (cx, cy) in enumerate(chips):
                    k = (t * 3 + j) * AG_CHUNKS + r
                    cp = pltpu.make_async_remote_copy(src_ref=srcs[t].at[rows], dst_ref=dsts[t].at[q, rows],
                                                      send_sem=send_sems.at[k], recv_sem=recv_sems.at[k],
                                                      device_id=(cx, cy, c), device_id_type=MESH)
                    cp.start()
                    sends.append(cp)
        fwds = []
        for r in range(AG_CHUNKS):
            for t in range(n):
                hc = srcs[t].shape[0] // 2 // AG_CHUNKS
                sub = hc // D2D_SPLIT
                for j, (cx, cy) in enumerate(chips):
                    k = (t * 3 + j) * AG_CHUNKS + r
                    base = c * (hc * AG_CHUNKS) + r * hc
                    part = dsts[t].at[2 * cx + cy, pl.ds(base, hc)]
                    pltpu.make_async_remote_copy(src_ref=part, dst_ref=part, send_sem=send_sems.at[k], recv_sem=recv_sems.at[k],
                                                 device_id=(cx, cy, c), device_id_type=MESH).wait_recv()
                    for u in range(D2D_SPLIT):
                        piece = dsts[t].at[2 * cx + cy, pl.ds(base + u * sub, sub)]
                        pltpu.make_async_remote_copy(src_ref=piece, dst_ref=piece, send_sem=fwd_send_sems.at[k],
                                                     recv_sem=fwd_recv_sems.at[k], device_id=sibling, device_id_type=MESH).start()
                    fwds.append((part, k))
        for r in range(AG_CHUNKS):
            for t in range(n):
                hc = srcs[t].shape[0] // 2 // AG_CHUNKS
                for j, (cx, cy) in enumerate(chips):
                    k = (t * 3 + j) * AG_CHUNKS + r
                    part = dsts[t].at[2 * cx + cy, pl.ds((1 - c) * (hc * AG_CHUNKS) + r * hc, hc)]
                    pltpu.make_async_remote_copy(src_ref=part, dst_ref=part, send_sem=fwd_send_sems.at[k],
                                                 recv_sem=fwd_recv_sems.at[k], device_id=sibling, device_id_type=MESH).wait_recv()
        for cp in sends:
            cp.wait_send()
        for part, k in fwds:
            pltpu.make_async_remote_copy(src_ref=part, dst_ref=part, send_sem=fwd_send_sems.at[k], recv_sem=fwd_recv_sems.at[k],
                                         device_id=sibling, device_id_type=MESH).wait_send()

    return pl.pallas_call(
        body, name="all_gather_weights", in_specs=[ANY] * n, out_specs=[ANY] * n,
        out_shape=[jax.ShapeDtypeStruct((N_CHIP,) + s.shape, s.dtype) for s in shards],
        scratch_shapes=[pltpu.SemaphoreType.DMA((nsem,)), pltpu.SemaphoreType.DMA((nsem,)), pltpu.SemaphoreType.DMA((nsem,)),
                        pltpu.SemaphoreType.DMA((nsem,))],
    )(*shards)


def _exchange_halves(grads, tag=""):
    n = len(grads)

    def body(*refs):
        srcs, theirs = refs[:n], refs[n:2 * n]
        send_sems, recv_sems = refs[2 * n:]
        x, y, c, _ = _place()
        sibling = (x, y, 1 - c)
        waits = []
        for t in range(n):
            h = srcs[t].shape[1] // 2
            sub = h // D2D_SPLIT
            for s in range(N_CHIP):
                for u in range(D2D_SPLIT):
                    pltpu.make_async_remote_copy(src_ref=srcs[t].at[s, pl.ds((1 - c) * h + u * sub, sub)],
                                                 dst_ref=theirs[t].at[s, pl.ds(u * sub, sub)],
                                                 send_sem=send_sems.at[t], recv_sem=recv_sems.at[t],
                                                 device_id=sibling, device_id_type=MESH).start()
            waits.append(pltpu.make_async_remote_copy(src_ref=srcs[t].at[:, pl.ds((1 - c) * h, h)], dst_ref=theirs[t],
                                                      send_sem=send_sems.at[t], recv_sem=recv_sems.at[t],
                                                      device_id=sibling, device_id_type=MESH))
        for whole in waits:
            whole.wait()

    half = [jax.ShapeDtypeStruct((N_CHIP, g.shape[1] // 2, g.shape[2]), g.dtype) for g in grads]
    return pl.pallas_call(
        body, name=f"exchange_halves_{tag}", in_specs=[ANY] * n, out_specs=[ANY] * n, out_shape=half,
        scratch_shapes=[pltpu.SemaphoreType.DMA((n,)), pltpu.SemaphoreType.DMA((n,))],
    )(*grads)


def _chip_exchange_comm(parts):
    n = len(parts)

    def copies(srcs, dsts, sems):
        send_sems, recv_sems, local_sems = sems
        x, y, c, chips = _place()
        q = 2 * x + y
        owns = [pltpu.make_async_copy(srcs[t].at[q], dsts[t].at[q], local_sems.at[t]) for t in range(n)]
        sends, recvs = [], []
        for t in range(n):
            for j, (cx, cy) in enumerate(chips):
                args = dict(send_sem=send_sems.at[3 * t + j], recv_sem=recv_sems.at[3 * t + j],
                            device_id=(cx, cy, c), device_id_type=MESH)
                sends.append(pltpu.make_async_remote_copy(src_ref=srcs[t].at[2 * cx + cy], dst_ref=dsts[t].at[q], **args))
                part = dsts[t].at[2 * cx + cy]
                recvs.append(pltpu.make_async_remote_copy(src_ref=part, dst_ref=part, **args))
        return owns, sends, recvs

    def start(srcs, dsts, sems):
        owns, sends, _ = copies(srcs, dsts, sems)
        for cp in owns + sends:
            cp.start()

    def finish(srcs, dsts, sems):
        owns, sends, recvs = copies(srcs, dsts, sems)
        for cp in recvs:
            cp.wait_recv()
        for cp in sends:
            cp.wait_send()
        for cp in owns:
            cp.wait()

    return _Comm(parts, [jax.ShapeDtypeStruct(p.shape, p.dtype) for p in parts],
                 [pltpu.SemaphoreType.DMA((3 * n,)), pltpu.SemaphoreType.DMA((3 * n,)), pltpu.SemaphoreType.DMA((n,))], start, finish)


def _gather_relay_comm(shard):
    h = shard.shape[0] // 2
    sub = h // D2D_SPLIT

    def place_and_copies(srcs, dsts, sems):
        send_sems, recv_sems, fwd_send_sems, fwd_recv_sems = sems
        x, y, c, chips = _place()
        q = 2 * x + y
        mine, theirs = pl.ds(c * h, h), pl.ds((1 - c) * h, h)
        sends, recvs, relays, relayed = [], [], [], []
        for j, (cx, cy) in enumerate(chips):
            ici = dict(send_sem=send_sems.at[j], recv_sem=recv_sems.at[j], device_id=(cx, cy, c), device_id_type=MESH)
            d2d = dict(send_sem=fwd_send_sems.at[j], recv_sem=fwd_recv_sems.at[j], device_id=(x, y, 1 - c), device_id_type=MESH)
            sends.append(pltpu.make_async_remote_copy(src_ref=srcs[0].at[mine], dst_ref=dsts[0].at[q, mine], **ici))
            landed = dsts[0].at[2 * cx + cy, mine]
            recvs.append(pltpu.make_async_remote_copy(src_ref=landed, dst_ref=landed, **ici))
            pieces = [dsts[0].at[2 * cx + cy, pl.ds(c * h + u * sub, sub)] for u in range(D2D_SPLIT)]
            relays.append(([pltpu.make_async_remote_copy(src_ref=p, dst_ref=p, **d2d) for p in pieces],
                           pltpu.make_async_remote_copy(src_ref=landed, dst_ref=landed, **d2d)))
            other = dsts[0].at[2 * cx + cy, theirs]
            relayed.append(pltpu.make_async_remote_copy(src_ref=other, dst_ref=other, **d2d))
        return sends, recvs, relays, relayed

    def start(srcs, dsts, sems):
        for cp in place_and_copies(srcs, dsts, sems)[0]:
            cp.start()

    def middle(srcs, dsts, sems):
        _, recvs, relays, _ = place_and_copies(srcs, dsts, sems)
        for cp, (pieces, _) in zip(recvs, relays):
            cp.wait_recv()
            for piece in pieces:
                piece.start()

    def finish(srcs, dsts, sems):
        sends, _, relays, relayed = place_and_copies(srcs, dsts, sems)
        for cp in relayed:
            cp.wait_recv()
        for cp in sends:
            cp.wait_send()
        for _, whole in relays:
            whole.wait_send()

    comm = _Comm([shard], [jax.ShapeDtypeStruct((N_CHIP,) + shard.shape, shard.dtype)],
                 [pltpu.SemaphoreType.DMA((3,))] * 4, start, finish)
    comm.middle = middle
    return comm


def _run_comm(comm, name):
    n_ci, n_co = len(comm.ins), len(comm.outs)

    def body(*refs):
        ci, co, cs = refs[:n_ci], refs[n_ci:n_ci + n_co], refs[n_ci + n_co:]
        comm.start(ci, co, cs)
        if comm.middle is not None:
            comm.middle(ci, co, cs)
        comm.finish(ci, co, cs)

    return pl.pallas_call(body, name=name, in_specs=[ANY] * n_ci, out_specs=[ANY] * n_co, out_shape=comm.outs,
                          scratch_shapes=comm.sems)(*comm.ins)


def _share_halves(halves, tag=""):
    n = len(halves)

    def body(*refs):
        srcs, dsts = refs[:n], refs[n:2 * n]
        send_sems, recv_sems = refs[2 * n:]
        x, y, c, _ = _place()
        sibling = (x, y, 1 - c)
        for t in range(n):
            h = srcs[t].shape[0]
            sub = h // (2 * D2D_SPLIT)
            for u in range(2 * D2D_SPLIT):
                pltpu.make_async_remote_copy(src_ref=srcs[t].at[pl.ds(u * sub, sub)], dst_ref=dsts[t].at[pl.ds(c * h + u * sub, sub)],
                                             send_sem=send_sems.at[t], recv_sem=recv_sems.at[t],
                                             device_id=sibling, device_id_type=MESH).start()
        for t in range(n):
            h = srcs[t].shape[0]
            pltpu.make_async_remote_copy(src_ref=srcs[t], dst_ref=dsts[t].at[pl.ds((1 - c) * h, h)], send_sem=send_sems.at[t],
                                         recv_sem=recv_sems.at[t], device_id=sibling, device_id_type=MESH).wait()

    return pl.pallas_call(
        body, name=f"share_halves_{tag}", in_specs=[ANY] * n, out_specs=[ANY] * n,
        out_shape=[jax.ShapeDtypeStruct((2 * h.shape[0], h.shape[1]), h.dtype) for h in halves],
        scratch_shapes=[pltpu.SemaphoreType.DMA((n,)), pltpu.SemaphoreType.DMA((n,))],
    )(*halves)


def _gather_weights(shards, chip):
    gathered = _all_gather_weights(shards)
    return [lax.dynamic_update_slice(g, s[None], (chip, 0, 0)) for g, s in zip(gathered, shards)]


def _reduce_scatter_begin(sends, core, tag):
    theirs = _exchange_halves(sends, tag)
    mines = [lax.dynamic_slice(g, (0, core * (g.shape[1] // 2), 0), (N_CHIP, g.shape[1] // 2, g.shape[2])) for g in sends]
    pair = [_sum_leading((m, t), f"pair_sum_{tag}{i}", out_dtype=BF16) for i, (m, t) in enumerate(zip(mines, theirs))]
    return _chip_exchange_comm(pair)


def _reduce_scatter_end(contrib, core, tag):
    halves = [_sum_leading(c, f"chip_sum_{tag}{i}") for i, c in enumerate(contrib)]
    shared = _share_halves(halves, tag)
    return [lax.dynamic_update_slice(full, mine, (core * mine.shape[0], 0)) for full, mine in zip(shared, halves)]


def _reduce_scatter(sends, core, tag=""):
    contrib = _run_comm(_reduce_scatter_begin(sends, core, tag), f"exchange_chips_{tag}")
    return _reduce_scatter_end(contrib, core, tag)


def _pack(arrs):
    rows = []
    for a in arrs:
        flat = a.astype(F32).reshape(-1)
        pad = (-flat.shape[0]) % 1024
        rows.append(jnp.pad(flat, (0, pad)).reshape(-1, 128))
    return jnp.concatenate(rows, axis=0)


def _unpack(buf, shapes):
    out, r = [], 0
    for shp in shapes:
        size = 1
        for d in shp:
            size *= d
        nr = (size + 1023) // 1024 * 8
        out.append(buf[r:r + nr].reshape(-1)[:size].reshape(shp))
        r += nr
    return out


R_BG, R_BS, R_OUT, R_FF = D // N_CHIP, DI // N_CHIP, D // N_CHIP, DFF // N_CHIP


def _pack_rest_shard(w_bg, w_bs, w_out, w_ff2, w_ff1):
    return jnp.concatenate([w_bg, w_bs, w_out, w_ff2, w_ff1], axis=0).astype(BF16)


def _unpack_rest(g):
    o1, o2, o3, o4 = R_BG, R_BG + R_BS, R_BG + R_BS + R_OUT, R_BG + R_BS + R_OUT + R_FF
    return (g[:, :o1].reshape(D, D), g[:, o1:o2].reshape(DI, D), g[:, o2:o3].reshape(D, D),
            jnp.transpose(g[:, o4:], (1, 0, 2)).reshape(D, DFF), g[:, o3:o4].reshape(DFF, D))


def _local_step(x, t, mod, w_in_r, rest, gm_norm_w, gm_ws, gm_bs, conv_w, conv_b, dt_bias, a_log, d_skip, ssm_norm_w,
                final_norm_w, place=None):
    sh1, sc1, g1, sh2, sc2, g2 = [mod[:, i * D:(i + 1) * D] for i in range(6)]
    ws_b = jnp.where(jnp.tril(jnp.ones((Q, Q), bool))[None], gm_ws, 0.0).astype(BF16)
    bs_t = gm_bs.T
    pad32 = lambda v: jnp.pad(v, ((0, 0), (0, 128 - NHEAD)))
    dtb, alog, dsk = pad32(dt_bias), pad32(a_log), pad32(d_skip)

    h1 = _prenorm(x, sc1, sh1)
    gdt = (F32,) if place is None else (BF16,)
    if place is None:
        (proj,) = _matmul(h1, w_in_r, "nn", name="mm_proj", out_dtypes=(BF16,), tn=1152)
        w_bg, w_bs, w_out, w_ff1, w_ff2 = rest
    else:
        chip, core = place
        proj, g_rest = _matmul(h1, w_in_r, "nn", name="mm_proj", out_dtypes=(BF16,), tn=1152, comm=_gather_relay_comm(rest))
        w_bg, w_bs, w_out, w_ff1, w_ff2 = _unpack_rest(lax.dynamic_update_slice(g_rest, rest[None], (chip, 0, 0)))
    (dt_raw,) = _matmul(h1, w_in_r[:, OFF_DT:OFF_DT + 128], "nn", name="mm_dt", tn=128)
    y_a = _sgu_fwd(proj, gm_norm_w, ws_b, bs_t)
    xbc = _conv_fwd(proj, conv_w, conv_b)
    dt, cum, cum_t = _dt_fwd(dt_raw, dtb, alog)
    y_ssd, states = _ssd_fwd(xbc, dt, cum, cum_t, dsk)
    y_b = _gatenorm_fwd(y_ssd, proj, ssm_norm_w)
    (ba,) = _matmul(y_a, w_bg, "nn", name="mm_branch_gm", out_dtypes=(BF16,))
    (bb,) = _matmul(y_b, w_bs, "nn", name="mm_branch_ssm", out_dtypes=(BF16,), tm=512, tk=2048)
    mixed = _mix_fwd(proj, ba, bb)
    (mo,) = _matmul(mixed, w_out, "nn", name="mm_out", out_dtypes=(BF16,))
    x1, h2 = _resid_norm(x, mo, g1, sc2, sh2)
    f, act = _matmul(h2, w_ff1, "nn", name="mm_ff1", out_dtypes=(BF16, BF16),
                     epi=lambda acc: (acc, jnp.square(jnp.maximum(acc, 0.0))))
    (fo,) = _matmul(act, w_ff2, "nn", name="mm_ff2", out_dtypes=(BF16,), tm=512, tk=4096)
    loss8, dx2, dfo, dg2, dfw = _loss_head(x1, fo, g2, final_norm_w, t)

    (df,) = _matmul(dfo, w_ff2, "nt", name="mm_ff2_dx", out_dtypes=(BF16,), epi_ins=(f,),
                    epi=lambda acc, fv: (acc * (2.0 * jnp.maximum(fv.astype(F32), 0.0)),))
    (g_ff2,) = _matmul(act, dfo, "tn", name="mm_ff2_dw", out_dtypes=gdt, tm=512, tk=4096)
    (dh2,) = _matmul(df, w_ff1, "nt", name="mm_ff1_dx", out_dtypes=(BF16,), tm=512, tk=4096)
    (g_ff1,) = _matmul(h2, df, "tn", name="mm_ff1_dw", out_dtypes=gdt, tm=512, tk=4096)
    dx1, dmo, dg1, dsc2, dsh2 = _resid_norm_bwd(x1, mo, g1, sc2, sh2, dh2, dx2)
    (dmixed,) = _matmul(dmo, w_out, "nt", name="mm_out_dx", out_dtypes=(BF16,))
    (g_out,) = _matmul(mixed, dmo, "tn", name="mm_out_dw", out_dtypes=gdt, tm=512, tk=4096)
    early = None
    if place is not None:
        stack = jnp.concatenate([g_out.reshape(N_CHIP, R_OUT, D), g_ff2.reshape(N_CHIP, R_FF, D),
                                 jnp.transpose(g_ff1.reshape(D, N_CHIP, D), (1, 0, 2))], axis=1).astype(BF16)
        early = _reduce_scatter_begin([stack], core, "early")
    dproj, dba, dbb = _mix_bwd(proj, ba, bb, dmixed)
    (dy_a,) = _matmul(dba, w_bg, "nt", name="mm_branch_gm_dx", out_dtypes=(BF16,))
    (g_bg,) = _matmul(y_a, dba, "tn", name="mm_branch_gm_dw", out_dtypes=gdt, tm=512, tk=4096)
    (dy_b,) = _matmul(dbb, w_bs, "nt", name="mm_branch_ssm_dx", out_dtypes=(BF16,))
    (g_bs,) = _matmul(y_b, dbb, "tn", name="mm_branch_ssm_dw", out_dtypes=gdt, tm=512, tk=4096)
    dproj, d_gm_norm, d_ws, d_bs_t = _sgu_bwd(proj, gm_norm_w, ws_b, bs_t, dy_a, dproj)
    dy_ssd, dy_ssd_t, dproj, d_ssm_norm = _gatenorm_bwd(y_ssd, proj, ssm_norm_w, dy_b, dproj)
    dxs, dbm, dcm, ddt, dcum, dcum_t, d_dsk, *early_contrib = _ssd_bwd(xbc, dt, cum, cum_t, dsk, states, dy_ssd, dy_ssd_t,
                                                                       comm=early)
    dproj, d_dtb, d_alog = _dt_bwd(dt_raw, dtb, alog, ddt, dcum, dcum_t, dproj)
    dproj, dw_x, db_x = _conv_bwd(proj, conv_w, conv_b, dxs, dproj, 0, "conv_bwd_x")
    dproj, dw_b, db_b = _conv_bwd(proj, conv_w, conv_b, dbm, dproj, DI, "conv_bwd_b")
    dproj, dw_c, db_c = _conv_bwd(proj, conv_w, conv_b, dcm, dproj, DI + D, "conv_bwd_c")
    d_conv_w = jnp.concatenate([dw_x, dw_b, dw_c], axis=1)
    d_conv_b = jnp.concatenate([db_x, db_b, db_c], axis=1)
    (g_in_r,) = _matmul(h1, dproj, "tn", name="mm_proj_dw", out_dtypes=gdt, tm=512, tn=1152, tk=4096)
    late = None
    if place is not None:
        send_in = _w_in_to_blocks(g_in_r)
        stack = jnp.concatenate([g_bg.reshape(N_CHIP, R_BG, D), g_bs.reshape(N_CHIP, R_BS, D)], axis=1).astype(BF16)
        late = _reduce_scatter_begin([send_in, stack], core, "late")
    dh1, *late_contrib = _matmul(dproj, w_in_r, "nt", name="mm_proj_dx", out_dtypes=(BF16,), tk=3456, comm=late)
    grad_x, dsc1, dsh1 = _prenorm_bwd(x, sc1, sh1, dh1, dx1)

    dmod = jnp.concatenate([dsh1, dsc1, dg1, dsh2, dsc2, dg2], axis=1)
    small = dict(gm_ws=d_ws, gm_norm_w=d_gm_norm, gm_bs=d_bs_t.T, conv_w=d_conv_w, conv_b=d_conv_b,
                 dt_bias=d_dtb[:, :NHEAD], a_log=d_alog[:, :NHEAD], d_skip=d_dsk[:, :NHEAD],
                 ssm_norm_w=d_ssm_norm, final_norm_w=dfw, dmod=dmod, loss=loss8[:1, :1])
    if place is None:
        return grad_x, small, dict(w_in_r=g_in_r, w_bg=g_bg, w_bs=g_bs, w_out=g_out, w_ff1=g_ff1, w_ff2=g_ff2)
    (s_early,) = _reduce_scatter_end(early_contrib, core, "early")
    s_in, s_late = _reduce_scatter_end(late_contrib, core, "late")
    o1, o2 = R_OUT, R_OUT + R_FF
    big = dict(w_in=s_in, w_bg=s_late[:R_BG], w_bs=s_late[R_BG:], w_out=s_early[:o1], w_ff2=s_early[o1:o2], w_ff1=s_early[o2:])
    return grad_x, small, big


SMALL_KEYS = ("gm_ws", "gm_norm_w", "gm_bs", "conv_w", "conv_b", "dt_bias", "a_log", "d_skip", "ssm_norm_w",
              "final_norm_w", "dmod", "loss")
SMALL_SHAPES = ((GM_G, Q, Q), (1, D), (GM_G, Q), (4, CONV), (1, CONV), (1, NHEAD), (1, NHEAD), (1, NHEAD), (1, DI),
                (1, D), (1, 6 * D), (1, 1))


def _reorder_w_in(w_full):
    k = w_full.shape[0]
    return jnp.concatenate([w_full[:, :8192], w_full[:, 8224:], w_full[:, 8192:8224],
                            jnp.zeros((k, W_IN_R - W_IN), w_full.dtype)], axis=1)


W_SH = W_IN // N_CHIP
TAIL = 8192 - (N_CHIP - 1) * W_SH


def _w_in_from_blocks(g):
    last = g[N_CHIP - 1]
    return jnp.concatenate([g[0], g[1], g[2], last[:, :TAIL], last[:, TAIL + NHEAD:], last[:, TAIL:TAIL + NHEAD],
                            jnp.zeros((g.shape[1], W_IN_R - W_IN), g.dtype)], axis=1)


def _w_in_to_blocks(g_r):
    cut = (N_CHIP - 1) * W_SH
    last = jnp.concatenate([g_r[:, cut:8192], g_r[:, OFF_DT:OFF_DT + NHEAD], g_r[:, 8192:OFF_DT]], axis=1)
    return jnp.stack([g_r[:, :W_SH], g_r[:, W_SH:2 * W_SH], g_r[:, 2 * W_SH:cut], last], axis=0)


def _restore_w_in(g_r):
    return jnp.concatenate([g_r[:, :8192], g_r[:, OFF_DT:OFF_DT + NHEAD], g_r[:, 8192:OFF_DT]], axis=1)


def kernel(x, c, w_mod, b_mod, w_in, gm_norm_w, gm_ws, gm_bs, conv_w, conv_b, dt_bias, a_log, d_skip, ssm_norm_w, w_branch_gm, w_branch_ssm, w_out, w_ff1, w_ff2, final_norm_w, loss_target, m_w_mod, m_b_mod, m_w_in, m_gm_norm_w, m_gm_ws, m_gm_bs, m_conv_w, m_conv_b, m_dt_bias, m_a_log, m_d_skip, m_ssm_norm_w, m_w_branch_gm, m_w_branch_ssm, m_w_out, m_w_ff1, m_w_ff2, m_final_norm_w, v_w_mod, v_b_mod, v_w_in, v_gm_norm_w, v_gm_ws, v_gm_bs, v_conv_w, v_conv_b, v_dt_bias, v_a_log, v_d_skip, v_ssm_norm_w, v_w_branch_gm, v_w_branch_ssm, v_w_out, v_w_ff1, v_w_ff2, v_final_norm_w):
    ax, ay, ac = lax.axis_index("x"), lax.axis_index("y"), lax.axis_index("c")
    chip = 2 * ax + ay
    dev = 2 * chip + ac
    seq = x.shape[1]
    nmod = w_mod.shape[2]

    first = jnp.concatenate([c, conv_w[0], jnp.zeros((3, D), F32)], axis=0)
    first_all = _all_gather_small(first, "all_gather_cond")
    c8 = first_all[:, 0, :]
    conv_w_full = jnp.concatenate([first_all[2 * k, 1:5, :] for k in range(N_CHIP)], axis=1)
    b_sl = lax.dynamic_slice(b_mod, (0, chip * nmod), (1, nmod))
    mod_part = _mod_fwd(c8, w_mod[0], b_sl)
    mod_all = _all_gather_small(mod_part, "all_gather_mod")
    mod = jnp.concatenate([lax.dynamic_slice(mod_all, (2 * k, dev, 0), (1, 1, nmod))[0] for k in range(N_CHIP)], axis=1)

    (g_in,) = _gather_weights([w_in[0].astype(BF16)], chip)
    w_in_r = _w_in_from_blocks(g_in)
    rest = _pack_rest_shard(w_branch_gm[0], w_branch_ssm[0], w_out[0], w_ff2[0], w_ff1[0])

    grad_x, small, big = _local_step(
        x[0], loss_target[0], mod, w_in_r, rest, gm_norm_w, gm_ws[0], gm_bs[0], conv_w_full, conv_b, dt_bias, a_log,
        d_skip, ssm_norm_w, final_norm_w.reshape(1, D), place=(chip, ac))

    small_all = _all_gather_small(_pack([small[k] for k in SMALL_KEYS]), "all_gather_small_grads")
    small_sum = _sum_leading(small_all, "sum_small_grads")
    s_ws, s_gnw, s_bs, s_cw, s_cb, s_dtb, s_alog, s_dsk, s_snw, s_fnw, s_bmod, s_loss = _unpack(small_sum, SMALL_SHAPES)
    dmod_all = jnp.stack([_unpack(small_all[k], SMALL_SHAPES)[10][0] for k in range(N_DEV)], axis=0)
    g_w_mod = _mod_wgrad(c8, lax.dynamic_slice(dmod_all, (0, chip * nmod), (N_DEV, nmod)))
    g_conv_w = lax.dynamic_slice(s_cw, (0, chip * (CONV // N_CHIP)), (4, CONV // N_CHIP))

    g_w_in, g_w_bg, g_w_bs, g_w_out, g_w_ff2, g_w_ff1 = (big[k] for k in ("w_in", "w_bg", "w_bs", "w_out", "w_ff2", "w_ff1"))

    def adam_big(w, g, m, v, name):
        d, m2, v2 = _adam(w.reshape(g.shape), g, m.reshape(g.shape), v.reshape(g.shape), name)
        return g.reshape(w.shape), d.reshape(w.shape), m2.reshape(w.shape), v2.reshape(w.shape)

    res = {}
    res["w_mod"] = adam_big(w_mod, g_w_mod, m_w_mod, v_w_mod, "adam_w_mod")
    res["w_in"] = adam_big(w_in, g_w_in, m_w_in, v_w_in, "adam_w_in")
    res["w_branch_gm"] = adam_big(w_branch_gm, g_w_bg, m_w_branch_gm, v_w_branch_gm, "adam_w_branch_gm")
    res["w_branch_ssm"] = adam_big(w_branch_ssm, g_w_bs, m_w_branch_ssm, v_w_branch_ssm, "adam_w_branch_ssm")
    res["w_out"] = adam_big(w_out, g_w_out, m_w_out, v_w_out, "adam_w_out")
    res["w_ff1"] = adam_big(w_ff1, g_w_ff1, m_w_ff1, v_w_ff1, "adam_w_ff1")
    res["w_ff2"] = adam_big(w_ff2, g_w_ff2, m_w_ff2, v_w_ff2, "adam_w_ff2")

    names = ("b_mod", "gm_norm_w", "gm_ws", "gm_bs", "conv_w", "conv_b", "dt_bias", "a_log", "d_skip", "ssm_norm_w", "final_norm_w")
    ws = (b_mod, gm_norm_w, gm_ws, gm_bs, conv_w, conv_b, dt_bias, a_log, d_skip, ssm_norm_w, final_norm_w)
    ms = (m_b_mod, m_gm_norm_w, m_gm_ws, m_gm_bs, m_conv_w, m_conv_b, m_dt_bias, m_a_log, m_d_skip, m_ssm_norm_w, m_final_norm_w)
    vs = (v_b_mod, v_gm_norm_w, v_gm_ws, v_gm_bs, v_conv_w, v_conv_b, v_dt_bias, v_a_log, v_d_skip, v_ssm_norm_w, v_final_norm_w)
    gs = (s_bmod, s_gnw, s_ws, s_bs, g_conv_w, s_cb, s_dtb, s_alog, s_dsk, s_snw, s_fnw)
    gs = [g.reshape(w.shape) for g, w in zip(gs, ws)]
    shapes = [w.shape for w in ws]
    d_p, m_p, v_p = _adam(_pack(ws), _pack(gs), _pack(ms), _pack(vs), "adam_small")
    for name, g, d, m2, v2 in zip(names, gs, _unpack(d_p, shapes), _unpack(m_p, shapes), _unpack(v_p, shapes)):
        res[name] = (g, d, m2, v2)

    order = ("w_mod", "b_mod", "w_in", "gm_norm_w", "gm_ws", "gm_bs", "conv_w", "conv_b", "dt_bias", "a_log", "d_skip",
             "ssm_norm_w", "w_branch_gm", "w_branch_ssm", "w_out", "w_ff1", "w_ff2", "final_norm_w")
    loss = s_loss.reshape(())
    return (loss, grad_x.reshape(x.shape), *[res[k][0] for k in order], *[res[k][1] for k in order],
            *[res[k][2] for k in order], *[res[k][3] for k in order])
```

```python
import functools

import jax
import jax.numpy as jnp
from jax import lax
from jax.experimental import pallas as pl
from jax.experimental.pallas import tpu as pltpu

F32 = jnp.float32
BF16 = jnp.bfloat16
MESH = pl.DeviceIdType.MESH
HIGHEST = lax.Precision.HIGHEST

D = 1024
EPS = 1e-6
Q = 128
GM_G = 8
NHEAD = 32
NGRP = 8
DI = 2048
CONV = 4096
DFF = 4096
W_IN = 10272
W_IN_R = 10368
OFF_Z, OFF_XBC, OFF_GA, OFF_DT = 2048, 4096, 8192, 10240
N_CHIP = 4
N_DEV = 8
AG_CHUNKS = 4
D2D_SPLIT = 4

ADAM_LR, ADAM_B1, ADAM_B2, ADAM_EPS, ADAM_WD, ADAM_STEP = 0.001, 0.9, 0.999, 1e-08, 0.01, 10

ANY = pl.BlockSpec(memory_space=pl.ANY)
VMEM = pl.BlockSpec(memory_space=pltpu.VMEM)


def _full(shape):
    return pl.BlockSpec(shape, lambda *_: (0,) * len(shape))


def _pick(n, prefs):
    for p in prefs:
        if n % p == 0:
            return p
    return n


def _rms(x):
    return x * lax.rsqrt(jnp.mean(x * x, axis=-1, keepdims=True) + EPS)


def _modnorm(x, sc, sh):
    return _rms(x) * (1.0 + sc) + sh


def _sgu_pre(u, v, w):
    return jax.nn.gelu(u), _rms(jax.nn.gelu(v)) * w


def _gatenorm(y, z, w):
    g = y * jax.nn.silu(z)
    return _rms(g) * w


def _mix(ga, gb, ba, bb):
    return jax.nn.sigmoid(ga) * ba + jax.nn.sigmoid(gb) * bb


def _loss_tile(x1, fo, g2, fw, t):
    x2 = x1 + g2 * fo
    y = _rms(x2) * fw
    err = jnp.square(y - t)
    return 0.5 * jnp.sum(jnp.mean(err, axis=-1))


def _tril(n):
    r = lax.broadcasted_iota(jnp.int32, (n, n), 0)
    c = lax.broadcasted_iota(jnp.int32, (n, n), 1)
    return r >= c


def _dt_prep(dtr, dtb, alog):
    dt = jax.nn.softplus(dtr + dtb)
    a = dt * (-jnp.exp(alog))
    ones = _tril(Q).astype(F32)
    cum = jnp.dot(ones, a, precision=HIGHEST, preferred_element_type=F32)
    cum_t = lax.dot_general(a, ones, (((0,), (1,)), ((), ())), precision=HIGHEST, preferred_element_type=F32)
    return dt, cum, cum_t


def _ssd_group(x0, x1, s0, s1, bm, cm, dt, cum, cum_t, dsk, grp):
    lane = lax.broadcasted_iota(jnp.int32, (1, 128), 1)
    sub = lax.broadcasted_iota(jnp.int32, (128, 1), 0)
    future = jnp.where(_tril(Q), 0.0, -jnp.inf)
    half = lane < 64
    half_rows = sub < 64
    bmb = bm.astype(BF16)
    cmb = cm.astype(BF16)
    cb = lax.dot_general(cmb, bmb, (((1,), (1,)), ((), ())), preferred_element_type=F32)

    def col(v, h):
        return jnp.sum(jnp.where(lane == h, v, 0.0), axis=1, keepdims=True)

    def row(v, h):
        return jnp.sum(jnp.where(sub == h, v, 0.0), axis=0, keepdims=True)

    def last(c):
        return jnp.sum(jnp.where(sub == Q - 1, c, 0.0), axis=0, keepdims=True)

    outs, states = [], []
    for p, (xp, sp) in enumerate(((x0, s0), (x1, s1))):
        h_a = 4 * grp + 2 * p
        h_b = h_a + 1
        dt_a, dt_b = col(dt, h_a), col(dt, h_b)
        cum_a, cum_b = col(cum, h_a), col(cum, h_b)
        row_a, row_b = row(cum_t, h_a), row(cum_t, h_b)
        last_a, last_b = last(cum_a), last(cum_b)
        xdt = xp * jnp.where(half, dt_a, dt_b)
        xdtb = xdt.astype(BF16)
        m_a = (cb * jnp.exp(cum_a - row_a + future)).astype(BF16)
        m_b = (cb * jnp.exp(cum_b - row_b + future)).astype(BF16)
        y_intra = jnp.where(half, jnp.dot(m_a, xdtb, preferred_element_type=F32),
                            jnp.dot(m_b, xdtb, preferred_element_type=F32))
        y_inter = lax.dot_general(cmb, sp.astype(BF16), (((1,), (1,)), ((), ())), preferred_element_type=F32)
        y_inter = y_inter * jnp.where(half, jnp.exp(cum_a), jnp.exp(cum_b))
        w_end = jnp.where(half, jnp.exp(last_a - cum_a), jnp.exp(last_b - cum_b))
        upd = lax.dot_general((xdt * w_end).astype(BF16), bmb, (((0,), (0,)), ((), ())), preferred_element_type=F32)
        states.append(sp * jnp.where(half_rows, jnp.exp(last_a), jnp.exp(last_b)) + upd)
        outs.append(y_intra + y_inter + xp * jnp.where(half, col(dsk, h_a), col(dsk, h_b)))
    return outs[0], outs[1], states[0], states[1]


def _ssd_group_bwd(x0, x1, s0, s1, bm, cm, dt, cum, cum_t, dsk, grp, dy0, dy1, dyt0, dyt1, dn0, dn1):
    nt = (((1,), (1,)), ((), ()))
    lane = lax.broadcasted_iota(jnp.int32, (1, 128), 1)
    sub = lax.broadcasted_iota(jnp.int32, (128, 1), 0)
    r = lax.broadcasted_iota(jnp.int32, (Q, Q), 0)
    c = lax.broadcasted_iota(jnp.int32, (Q, Q), 1)
    future = jnp.where(r >= c, 0.0, -jnp.inf)
    future_t = jnp.where(c >= r, 0.0, -jnp.inf)
    half = lane < 64
    half_rows = sub < 64
    bmb = bm.astype(BF16)
    cmb = cm.astype(BF16)
    cb = lax.dot_general(cmb, bmb, nt, preferred_element_type=F32)
    cbt = lax.dot_general(bmb, cmb, nt, preferred_element_type=F32)

    def col(v, h):
        return jnp.sum(jnp.where(lane == h, v, 0.0), axis=1, keepdims=True)

    def row(v, h):
        return jnp.sum(jnp.where(sub == h, v, 0.0), axis=0, keepdims=True)

    def lane_sums(v, sel):
        hi = v.astype(BF16)
        lo = (v - hi.astype(F32)).astype(BF16)
        return jnp.dot(hi, sel, preferred_element_type=F32) + jnp.dot(lo, sel, preferred_element_type=F32)

    def head_sum(v, mask):
        return jnp.sum(jnp.where(mask, v, 0.0), axis=1, keepdims=True)

    src = lax.broadcasted_iota(jnp.int32, (128, 128), 0)
    dst = lax.broadcasted_iota(jnp.int32, (128, 128), 1)

    dcb = jnp.zeros((Q, Q), F32)
    dcbt = jnp.zeros((Q, Q), F32)
    dbm = jnp.zeros((Q, 128), F32)
    dcm = jnp.zeros((Q, 128), F32)
    ddt = jnp.zeros((Q, 128), F32)
    dcum = jnp.zeros((Q, 128), F32)
    dcumt = jnp.zeros((128, Q), F32)
    ddsk = jnp.zeros((1, 128), F32)
    dlast = jnp.zeros((1, 128), F32)
    dxs, dss = [], []
    for p, (xp, sp, dy, dyt, dns) in enumerate(((x0, s0, dy0, dyt0, dn0), (x1, s1, dy1, dyt1, dn1))):
        h_a = 4 * grp + 2 * p
        h_b = h_a + 1
        dt_a, dt_b = col(dt, h_a), col(dt, h_b)
        cum_a, cum_b = col(cum, h_a), col(cum, h_b)
        row_a, row_b = row(cum_t, h_a), row(cum_t, h_b)
        last_a = jnp.sum(jnp.where(sub == Q - 1, cum_a, 0.0), axis=0, keepdims=True)
        last_b = jnp.sum(jnp.where(sub == Q - 1, cum_b, 0.0), axis=0, keepdims=True)
        dtp = jnp.where(half, dt_a, dt_b)
        xdt = xp * dtp
        xdtb = xdt.astype(BF16)
        l_a, l_b = jnp.exp(cum_a - row_a + future), jnp.exp(cum_b - row_b + future)
        lt_a, lt_b = jnp.exp(row_a - cum_a + future_t), jnp.exp(row_b - cum_b + future_t)
        dya = jnp.where(half, dy, 0.0)
        dya_b, dyb_b = dya.astype(BF16), (dy - dya).astype(BF16)
        dm_a = lax.dot_general(dya_b, xdtb, nt, preferred_element_type=F32)
        dm_b = lax.dot_general(dyb_b, xdtb, nt, preferred_element_type=F32)
        dmt_a = lax.dot_general(xdtb, dya_b, nt, preferred_element_type=F32)
        dmt_b = lax.dot_general(xdtb, dyb_b, nt, preferred_element_type=F32)
        dxdt = (jnp.dot((cbt * lt_a).astype(BF16), dya_b, preferred_element_type=F32)
                + jnp.dot((cbt * lt_b).astype(BF16), dyb_b, preferred_element_type=F32))
        pair_sel = jnp.where(dst == jnp.where(src < 64, h_a, h_b), 1.0, 0.0).astype(BF16)
        to_a = jnp.where(dst == h_a, 1.0, 0.0).astype(BF16)
        to_b = jnp.where(dst == h_b, 1.0, 0.0).astype(BF16)
        dseg_a, dseg_b = dm_a * (cb * l_a), dm_b * (cb * l_b)
        dcum = dcum + lane_sums(dseg_a, to_a) + lane_sums(dseg_b, to_b)
        drow_a = -jnp.sum(dseg_a, axis=0, keepdims=True)
        drow_b = -jnp.sum(dseg_b, axis=0, keepdims=True)
        dcb = dcb + dm_a * l_a + dm_b * l_b
        dcbt = dcbt + dmt_a * lt_a + dmt_b * lt_b
        spb = sp.astype(BF16)
        y0 = lax.dot_general(cmb, spb, nt, preferred_element_type=F32)
        dy0s = dy * jnp.where(half, jnp.exp(cum_a), jnp.exp(cum_b))
        g = dy0s * y0
        dcm = dcm + jnp.dot(dy0s.astype(BF16), spb, preferred_element_type=F32)
        dsp = jnp.dot((dyt * jnp.where(half_rows, jnp.exp(row_a), jnp.exp(row_b))).astype(BF16), cmb, preferred_element_type=F32)
        wp = jnp.where(half, jnp.exp(last_a - cum_a), jnp.exp(last_b - cum_b))
        xw = xdt * wp
        dnsb = dns.astype(BF16)
        dxw = lax.dot_general(bmb, dnsb, nt, preferred_element_type=F32)
        dbm = dbm + jnp.dot(xw.astype(BF16), dnsb, preferred_element_type=F32)
        dxdt = dxdt + dxw * wp
        gw = dxw * xw
        dcum = dcum + lane_sums(g - gw, pair_sel)
        el_a, el_b = jnp.exp(last_a), jnp.exp(last_b)
        dsp = dsp + dns * jnp.where(half_rows, el_a, el_b)
        gs = dns * sp
        gs_all = jnp.sum(gs, axis=0, keepdims=True)
        gs_a = jnp.sum(jnp.sum(jnp.where(half_rows, gs, 0.0), axis=0, keepdims=True), axis=1, keepdims=True)
        gs_b = jnp.sum(gs_all, axis=1, keepdims=True) - gs_a
        gw_cols = jnp.sum(gw, axis=0, keepdims=True)
        dlast_a = head_sum(gw_cols, half) + el_a * gs_a
        dlast_b = head_sum(gw_cols, ~half) + el_b * gs_b
        dlast = dlast + jnp.where(lane == h_a, dlast_a, 0.0) + jnp.where(lane == h_b, dlast_b, 0.0)
        ddt = ddt + lane_sums(dxdt * xp, pair_sel)
        gk_cols = jnp.sum(dy * xp, axis=0, keepdims=True)
        dxs.append(dxdt * dtp + dy * jnp.where(half, col(dsk, h_a), col(dsk, h_b)))
        dss.append(dsp)
        dcumt = dcumt + jnp.where(sub == h_a, drow_a, 0.0) + jnp.where(sub == h_b, drow_b, 0.0)
        ddsk = ddsk + jnp.where(lane == h_a, head_sum(gk_cols, half), 0.0) + jnp.where(lane == h_b, head_sum(gk_cols, ~half), 0.0)
    dcum = dcum + jnp.where(sub == Q - 1, dlast, 0.0)
    dcm = dcm + jnp.dot(dcb.astype(BF16), bmb, preferred_element_type=F32)
    dbm = dbm + jnp.dot(dcbt.astype(BF16), cmb, preferred_element_type=F32)
    return dxs[0], dxs[1], dss[0], dss[1], dbm, dcm, ddt, dcum, dcumt, ddsk


class _Comm:
    def __init__(self, ins, outs, sems, start, finish):
        self.ins, self.outs, self.sems, self.start, self.finish = list(ins), list(outs), list(sems), start, finish
        self.middle = None


def _matmul(a, b, mode, *, name, out_dtypes=(F32,), epi=None, epi_ins=(), tm=1024, tn=1024, tk=1024, comm=None):
    if mode == "nn":
        (m, k), n = a.shape, b.shape[1]
    elif mode == "nt":
        (m, k), n = a.shape, b.shape[0]
    else:
        (k, m), n = a.shape, b.shape[1]
    tm, tn, tk = _pick(m, (tm, 512, 256, 128)), _pick(n, (tn, 1152, 1024, 512, 384, 256, 128)), _pick(k, (tk, 1152, 1024, 512, 256, 128))
    nk = k // tk
    if mode == "nn":
        a_spec = pl.BlockSpec((tm, tk), lambda i, j, kk: (i, kk))
        b_spec = pl.BlockSpec((tk, tn), lambda i, j, kk: (kk, j))
        dims = (((1,), (0,)), ((), ()))
    elif mode == "nt":
        a_spec = pl.BlockSpec((tm, tk), lambda i, j, kk: (i, kk))
        b_spec = pl.BlockSpec((tn, tk), lambda i, j, kk: (j, kk))
        dims = (((1,), (1,)), ((), ()))
    else:
        a_spec = pl.BlockSpec((tk, tm), lambda i, j, kk: (kk, i))
        b_spec = pl.BlockSpec((tk, tn), lambda i, j, kk: (kk, j))
        dims = (((0,), (0,)), ((), ()))
    o_spec = pl.BlockSpec((tm, tn), lambda i, j, kk: (i, j))
    n_epi, n_out = len(epi_ins), len(out_dtypes)
    n_ci, n_co, n_cs = (len(comm.ins), len(comm.outs), len(comm.sems)) if comm is not None else (0, 0, 0)
    grid = (m // tm, n // tn, nk)

    def body(*refs):
        a_ref, b_ref = refs[0], refs[1]
        e_refs = refs[2:2 + n_epi]
        ci_refs = refs[2 + n_epi:2 + n_epi + n_ci]
        o_refs = refs[2 + n_epi + n_ci:2 + n_epi + n_ci + n_out]
        co_refs = refs[2 + n_epi + n_ci + n_out:2 + n_epi + n_ci + n_out + n_co]
        acc_ref = refs[2 + n_epi + n_ci + n_out + n_co]
        cs_refs = refs[3 + n_epi + n_ci + n_out + n_co:]
        if comm is not None:
            ids = [pl.program_id(d) for d in range(3)]
            pl.when((ids[0] == 0) & (ids[1] == 0) & (ids[2] == 0))(lambda: comm.start(ci_refs, co_refs, cs_refs))
            if comm.middle is not None:
                pl.when((ids[0] == (5 * grid[0]) // 8) & (ids[1] == 0) & (ids[2] == 0))(
                    lambda: comm.middle(ci_refs, co_refs, cs_refs))

        def finish(acc):
            outs = epi(acc, *[e[...] for e in e_refs]) if epi is not None else (acc,)
            for o_ref, val in zip(o_refs, outs):
                o_ref[...] = val.astype(o_ref.dtype)

        part = lax.dot_general(a_ref[...], b_ref[...], dims, preferred_element_type=F32)
        if nk == 1:
            finish(part)
        else:
            kk = pl.program_id(2)

            @pl.when(kk == 0)
            def _():
                acc_ref[...] = part

            @pl.when(kk > 0)
            def _():
                acc_ref[...] += part

            @pl.when(kk == nk - 1)
            def _():
                finish(acc_ref[...])

        if comm is not None:
            pl.when((ids[0] == grid[0] - 1) & (ids[1] == grid[1] - 1) & (ids[2] == grid[2] - 1))(
                lambda: comm.finish(ci_refs, co_refs, cs_refs))

    extra_in = comm.ins if comm is not None else []
    extra_out = comm.outs if comm is not None else []
    extra_sems = comm.sems if comm is not None else []
    return pl.pallas_call(
        body, name=name, grid=grid,
        in_specs=[a_spec, b_spec] + [o_spec] * n_epi + [ANY] * n_ci,
        out_specs=[o_spec] * n_out + [ANY] * n_co,
        out_shape=[jax.ShapeDtypeStruct((m, n), dt) for dt in out_dtypes] + extra_out,
        scratch_shapes=[pltpu.VMEM((tm, tn) if nk > 1 else (8, 128), F32)] + extra_sems,
        compiler_params=pltpu.CompilerParams(
            dimension_semantics=("arbitrary",) * 3 if comm is not None else ("parallel", "parallel", "arbitrary")),
    )(a, b, *epi_ins, *extra_in)


def _row_tile(s):
    return _pick(s, (512, 256, 128))


def _prenorm(x, sc, sh):
    s = x.shape[0]
    tm = _row_tile(s)

    def body(x_ref, sc_ref, sh_ref, h_ref):
        h_ref[...] = _modnorm(x_ref[...], sc_ref[...], sh_ref[...]).astype(BF16)

    row = pl.BlockSpec((tm, D), lambda i: (i, 0))
    return pl.pallas_call(body, name="prenorm", grid=(s // tm,), in_specs=[row, _full((1, D)), _full((1, D))],
                          out_specs=row, out_shape=jax.ShapeDtypeStruct((s, D), BF16))(x, sc, sh)


def _prenorm_bwd(x, sc, sh, dh, dx_res):
    s = x.shape[0]
    tm = _row_tile(s)

    def body(x_ref, sc_ref, sh_ref, dh_ref, dr_ref, dx_ref, dsc_ref, dsh_ref):
        _, vjp = jax.vjp(_modnorm, x_ref[...], sc_ref[...], sh_ref[...])
        dx, dsc, dsh = vjp(dh_ref[...].astype(F32))
        dx_ref[...] = dr_ref[...] + dx

        @pl.when(pl.program_id(0) == 0)
        def _():
            dsc_ref[...] = jnp.zeros_like(dsc_ref)
            dsh_ref[...] = jnp.zeros_like(dsh_ref)

        dsc_ref[...] += dsc
        dsh_ref[...] += dsh

    row = pl.BlockSpec((tm, D), lambda i: (i, 0))
    vec = _full((1, D))
    return pl.pallas_call(
        body, name="prenorm_bwd", grid=(s // tm,), in_specs=[row, vec, vec, row, row], out_specs=[row, vec, vec],
        out_shape=[jax.ShapeDtypeStruct((s, D), F32), jax.ShapeDtypeStruct((1, D), F32), jax.ShapeDtypeStruct((1, D), F32)],
    )(x, sc, sh, dh, dx_res)


def _resid_norm(x, mo, g1, sc, sh):
    s = x.shape[0]
    tm = _row_tile(s)

    def body(x_ref, mo_ref, g_ref, sc_ref, sh_ref, x1_ref, h_ref):
        x1 = x_ref[...] + g_ref[...] * mo_ref[...].astype(F32)
        x1_ref[...] = x1
        h_ref[...] = _modnorm(x1, sc_ref[...], sh_ref[...]).astype(BF16)

    row = pl.BlockSpec((tm, D), lambda i: (i, 0))
    vec = _full((1, D))
    return pl.pallas_call(
        body, name="resid_norm", grid=(s // tm,), in_specs=[row, row, vec, vec, vec], out_specs=[row, row],
        out_shape=[jax.ShapeDtypeStruct((s, D), F32), jax.ShapeDtypeStruct((s, D), BF16)],
    )(x, mo, g1, sc, sh)


def _resid_norm_bwd(x1, mo, g1, sc, sh, dh, dx2):
    s = x1.shape[0]
    tm = _row_tile(s)

    def body(x1_ref, mo_ref, g_ref, sc_ref, sh_ref, dh_ref, dx2_ref, dx1_ref, dmo_ref, dg_ref, dsc_ref, dsh_ref):
        _, vjp = jax.vjp(_modnorm, x1_ref[...], sc_ref[...], sh_ref[...])
        dx, dsc, dsh = vjp(dh_ref[...].astype(F32))
        dx1 = dx2_ref[...] + dx
        dx1_ref[...] = dx1
        dmo_ref[...] = (dx1 * g_ref[...]).astype(BF16)

        @pl.when(pl.program_id(0) == 0)
        def _():
            dg_ref[...] = jnp.zeros_like(dg_ref)
            dsc_ref[...] = jnp.zeros_like(dsc_ref)
            dsh_ref[...] = jnp.zeros_like(dsh_ref)

        dg_ref[...] += jnp.sum(dx1 * mo_ref[...].astype(F32), axis=0, keepdims=True)
        dsc_ref[...] += dsc
        dsh_ref[...] += dsh

    row = pl.BlockSpec((tm, D), lambda i: (i, 0))
    vec = _full((1, D))
    vshape = jax.ShapeDtypeStruct((1, D), F32)
    return pl.pallas_call(
        body, name="resid_norm_bwd", grid=(s // tm,), in_specs=[row, row, vec, vec, vec, row, row],
        out_specs=[row, row, vec, vec, vec],
        out_shape=[jax.ShapeDtypeStruct((s, D), F32), jax.ShapeDtypeStruct((s, D), BF16), vshape, vshape, vshape],
    )(x1, mo, g1, sc, sh, dh, dx2)


def _loss_head(x1, fo, g2, fw, t):
    s = x1.shape[0]
    tm = _row_tile(s)

    def body(x1_ref, fo_ref, g_ref, fw_ref, t_ref, loss_ref, dx_ref, dfo_ref, dg_ref, dfw_ref):
        loss, (dx1, dfo, dg, dfw) = jax.value_and_grad(_loss_tile, argnums=(0, 1, 2, 3))(
            x1_ref[...], fo_ref[...].astype(F32), g_ref[...], fw_ref[...], t_ref[...])
        dx_ref[...] = dx1
        dfo_ref[...] = dfo.astype(BF16)

        @pl.when(pl.program_id(0) == 0)
        def _():
            loss_ref[...] = jnp.zeros_like(loss_ref)
            dg_ref[...] = jnp.zeros_like(dg_ref)
            dfw_ref[...] = jnp.zeros_like(dfw_ref)

        loss_ref[...] += jnp.full(loss_ref.shape, loss, F32)
        dg_ref[...] += dg
        dfw_ref[...] += dfw

    row = pl.BlockSpec((tm, D), lambda i: (i, 0))
    vec = _full((1, D))
    vshape = jax.ShapeDtypeStruct((1, D), F32)
    return pl.pallas_call(
        body, name="loss_head", grid=(s // tm,), in_specs=[row, row, vec, vec, row],
        out_specs=[_full((8, 128)), row, row, vec, vec],
        out_shape=[jax.ShapeDtypeStruct((8, 128), F32), jax.ShapeDtypeStruct((s, D), F32),
                   jax.ShapeDtypeStruct((s, D), BF16), vshape, vshape],
    )(x1, fo, g2, fw, t)


def _sgu_fwd(proj, norm_w, ws_b, bs_t):
    s = proj.shape[0]
    tm = _pick(s, (256, 128))

    def body(u_ref, v_ref, w_ref, ws_ref, bs_ref, y_ref):
        ug, vn = _sgu_pre(u_ref[...].astype(F32), v_ref[...].astype(F32), w_ref[...])
        vnb = vn.astype(BF16)
        for c in range(tm // Q):
            r = slice(c * Q, (c + 1) * Q)
            for g in range(GM_G):
                cs = slice(g * 128, (g + 1) * 128)
                sv = jnp.dot(ws_ref[g], vnb[r, cs], preferred_element_type=F32) + bs_ref[:, g:g + 1]
                y_ref[r, cs] = (ug[r, cs] * sv).astype(BF16)

    return pl.pallas_call(
        body, name="sgu_fwd", grid=(s // tm,),
        in_specs=[pl.BlockSpec((tm, D), lambda i: (i, 0)), pl.BlockSpec((tm, D), lambda i: (i, 1)),
                  _full((1, D)), _full((GM_G, Q, Q)), _full((Q, GM_G))],
        out_specs=pl.BlockSpec((tm, D), lambda i: (i, 0)),
        out_shape=jax.ShapeDtypeStruct((s, D), BF16),
    )(proj, proj, norm_w, ws_b, bs_t)


def _sgu_bwd(proj, norm_w, ws_b, bs_t, dy, dproj):
    s = proj.shape[0]
    tm = _pick(s, (256, 128))

    def body(u_ref, v_ref, w_ref, ws_ref, bs_ref, dy_ref, _, duv_ref, dw_ref, dws_ref, dbs_ref, dug_scr, dvn_scr):
        @pl.when(pl.program_id(0) == 0)
        def _():
            dw_ref[...] = jnp.zeros_like(dw_ref)
            dws_ref[...] = jnp.zeros_like(dws_ref)
            dbs_ref[...] = jnp.zeros_like(dbs_ref)

        (ug, vn), vjp = jax.vjp(_sgu_pre, u_ref[...].astype(F32), v_ref[...].astype(F32), w_ref[...])
        vnb = vn.astype(BF16)
        dy = dy_ref[...].astype(F32)
        causal = _tril(Q).astype(F32)
        for c in range(tm // Q):
            r = slice(c * Q, (c + 1) * Q)
            for g in range(GM_G):
                cs = slice(g * 128, (g + 1) * 128)
                blk = vnb[r, cs]
                sv = jnp.dot(ws_ref[g], blk, preferred_element_type=F32) + bs_ref[:, g:g + 1]
                dug_scr[r, cs] = dy[r, cs] * sv
                dsv = dy[r, cs] * ug[r, cs]
                dsvb = dsv.astype(BF16)
                dws_ref[g] += causal * lax.dot_general(dsvb, blk, (((1,), (1,)), ((), ())), preferred_element_type=F32)
                dbs_ref[:, g:g + 1] += jnp.sum(dsv, axis=1, keepdims=True)
                dvn_scr[r, cs] = lax.dot_general(ws_ref[g], dsvb, (((0,), (0,)), ((), ())), preferred_element_type=F32)
        du, dv, dw = vjp((dug_scr[...], dvn_scr[...]))
        duv_ref[:, :D] = du.astype(BF16)
        duv_ref[:, D:] = dv.astype(BF16)
        dw_ref[...] += dw

    return pl.pallas_call(
        body, name="sgu_bwd", grid=(s // tm,),
        in_specs=[pl.BlockSpec((tm, D), lambda i: (i, 0)), pl.BlockSpec((tm, D), lambda i: (i, 1)),
                  _full((1, D)), _full((GM_G, Q, Q)), _full((Q, GM_G)), pl.BlockSpec((tm, D), lambda i: (i, 0)), ANY],
        out_specs=[pl.BlockSpec((tm, 2 * D), lambda i: (i, 0)), _full((1, D)), _full((GM_G, Q, Q)), _full((Q, GM_G))],
        out_shape=[jax.ShapeDtypeStruct(dproj.shape, BF16), jax.ShapeDtypeStruct((1, D), F32),
                   jax.ShapeDtypeStruct((GM_G, Q, Q), F32), jax.ShapeDtypeStruct((Q, GM_G), F32)],
        scratch_shapes=[pltpu.VMEM((tm, D), F32), pltpu.VMEM((tm, D), F32)],
        input_output_aliases={6: 0},
    )(proj, proj, norm_w, ws_b, bs_t, dy, dproj)


def _gatenorm_fwd(y_ssd, proj, norm_w):
    s = y_ssd.shape[0]
    tm = _pick(s, (256, 128))
    gw = DI // NGRP

    def body(y_ref, z_ref, w_ref, o_ref):
        for g in range(NGRP):
            cs = slice(g * gw, (g + 1) * gw)
            o_ref[:, cs] = _gatenorm(y_ref[:, cs], z_ref[:, cs].astype(F32), w_ref[:, cs]).astype(BF16)

    return pl.pallas_call(
        body, name="gatenorm_fwd", grid=(s // tm,),
        in_specs=[pl.BlockSpec((tm, DI), lambda i: (i, 0)), pl.BlockSpec((tm, DI), lambda i: (i, OFF_Z // DI)), _full((1, DI))],
        out_specs=pl.BlockSpec((tm, DI), lambda i: (i, 0)),
        out_shape=jax.ShapeDtypeStruct((s, DI), BF16),
    )(y_ssd, proj, norm_w)


def _gatenorm_bwd(y_ssd, proj, norm_w, dyb, dproj):
    s = y_ssd.shape[0]
    tm = _pick(s, (256, 128))
    gw = DI // NGRP

    def body(y_ref, z_ref, w_ref, d_ref, _, dy_ref, dyt_ref, dz_ref, dw_ref):
        @pl.when(pl.program_id(0) == 0)
        def _():
            dw_ref[...] = jnp.zeros_like(dw_ref)

        for g in range(NGRP):
            cs = slice(g * gw, (g + 1) * gw)
            _, vjp = jax.vjp(_gatenorm, y_ref[:, cs], z_ref[:, cs].astype(F32), w_ref[:, cs])
            dy, dz, dw = vjp(d_ref[:, cs].astype(F32))
            dy_ref[:, cs] = dy.astype(BF16)
            dyt_ref[cs, :] = dy.T.astype(BF16)
            dz_ref[:, cs] = dz.astype(BF16)
            dw_ref[:, cs] += dw

    blk = pl.BlockSpec((tm, DI), lambda i: (i, 0))
    zblk = pl.BlockSpec((tm, DI), lambda i: (i, OFF_Z // DI))
    return pl.pallas_call(
        body, name="gatenorm_bwd", grid=(s // tm,),
        in_specs=[blk, zblk, _full((1, DI)), blk, ANY],
        out_specs=[blk, pl.BlockSpec((DI, tm), lambda i: (0, i)), zblk, _full((1, DI))],
        out_shape=[jax.ShapeDtypeStruct((s, DI), BF16), jax.ShapeDtypeStruct((DI, s), BF16),
                   jax.ShapeDtypeStruct(dproj.shape, BF16), jax.ShapeDtypeStruct((1, DI), F32)],
        input_output_aliases={4: 2},
    )(y_ssd, proj, norm_w, dyb, dproj)


def _mix_fwd(proj, ba, bb):
    s = proj.shape[0]
    tm = _row_tile(s)
    gb0 = OFF_GA // D

    def body(ga_ref, gb_ref, ba_ref, bb_ref, o_ref):
        o_ref[...] = _mix(ga_ref[...].astype(F32), gb_ref[...].astype(F32), ba_ref[...].astype(F32),
                          bb_ref[...].astype(F32)).astype(BF16)

    row = pl.BlockSpec((tm, D), lambda i: (i, 0))
    return pl.pallas_call(
        body, name="mix_fwd", grid=(s // tm,),
        in_specs=[pl.BlockSpec((tm, D), lambda i: (i, gb0)), pl.BlockSpec((tm, D), lambda i: (i, gb0 + 1)), row, row],
        out_specs=row, out_shape=jax.ShapeDtypeStruct((s, D), BF16),
    )(proj, proj, ba, bb)


def _mix_bwd(proj, ba, bb, dmixed):
    s = proj.shape[0]
    tm = _row_tile(s)
    gb0 = OFF_GA // D

    def body(ga_ref, gb_ref, ba_ref, bb_ref, d_ref, dg_ref, dba_ref, dbb_ref):
        _, vjp = jax.vjp(_mix, ga_ref[...].astype(F32), gb_ref[...].astype(F32), ba_ref[...].astype(F32),
                         bb_ref[...].astype(F32))
        dga, dgb, dba, dbb = vjp(d_ref[...].astype(F32))
        dg_ref[:, :D] = dga.astype(BF16)
        dg_ref[:, D:] = dgb.astype(BF16)
        dba_ref[...] = dba.astype(BF16)
        dbb_ref[...] = dbb.astype(BF16)

    row = pl.BlockSpec((tm, D), lambda i: (i, 0))
    return pl.pallas_call(
        body, name="mix_bwd", grid=(s // tm,),
        in_specs=[pl.BlockSpec((tm, D), lambda i: (i, gb0)), pl.BlockSpec((tm, D), lambda i: (i, gb0 + 1)), row, row, row],
        out_specs=[pl.BlockSpec((tm, 2 * D), lambda i: (i, OFF_GA // (2 * D))), row, row],
        out_shape=[jax.ShapeDtypeStruct((s, W_IN_R), BF16), jax.ShapeDtypeStruct((s, D), BF16), jax.ShapeDtypeStruct((s, D), BF16)],
    )(proj, proj, ba, bb, dmixed)


CONV_TC = 1024


def _conv_fwd(proj, conv_w, conv_b):
    s = proj.shape[0]
    tm = _row_tile(s)
    cb0 = OFF_XBC // CONV_TC

    def body(x_ref, halo_ref, w_ref, b_ref, o_ref):
        halo = jnp.where(pl.program_id(0) > 0, halo_ref[...].astype(F32)[8:, :], 0.0)
        ext = jnp.concatenate([halo, x_ref[...].astype(F32)], axis=0)
        acc = jnp.broadcast_to(b_ref[...], (tm, CONV_TC))
        for k in range(4):
            shifted = ext if k == 3 else pltpu.roll(ext, 3 - k, 0)
            acc = acc + w_ref[k:k + 1, :] * shifted[8:, :]
        o_ref[...] = jax.nn.silu(acc)

    return pl.pallas_call(
        body, name="conv_fwd", grid=(s // tm, CONV // CONV_TC),
        in_specs=[pl.BlockSpec((tm, CONV_TC), lambda i, j: (i, cb0 + j)),
                  pl.BlockSpec((16, CONV_TC), lambda i, j: (jnp.maximum(i * (tm // 16) - 1, 0), cb0 + j)),
                  pl.BlockSpec((4, CONV_TC), lambda i, j: (0, j)), pl.BlockSpec((1, CONV_TC), lambda i, j: (0, j))],
        out_specs=pl.BlockSpec((tm, CONV_TC), lambda i, j: (i, j)),
        out_shape=jax.ShapeDtypeStruct((s, CONV), F32),
    )(proj, proj, conv_w, conv_b)


def _conv_bwd(proj, conv_w, conv_b, dact, dproj, col0, name):
    s, width = dact.shape
    tm = _row_tile(s)
    nt = s // tm
    c0 = col0 // CONV_TC
    cb0 = OFF_XBC // CONV_TC + c0

    def body(x_ref, prev_ref, next_ref, d_ref, dnext_ref, w_ref, b_ref, _, dx_ref, dw_ref, db_ref):
        i = pl.program_id(1)
        prev = jnp.where(i > 0, prev_ref[...].astype(F32)[8:, :], 0.0)
        ext = jnp.concatenate([prev, x_ref[...].astype(F32), next_ref[...].astype(F32)[:8, :]], axis=0)
        dext = jnp.concatenate([d_ref[...], jnp.where(i < nt - 1, dnext_ref[...], 0.0)], axis=0)
        pre = jnp.broadcast_to(b_ref[...], (tm + 8, CONV_TC))
        taps = []
        for k in range(4):
            shifted = (ext if k == 3 else pltpu.roll(ext, 3 - k, 0))[8:, :]
            taps.append(shifted)
            pre = pre + w_ref[k:k + 1, :] * shifted
        sig = jax.nn.sigmoid(pre)
        dpre = dext * (sig * (1.0 + pre * (1.0 - sig)))
        dx = jnp.zeros((tm, CONV_TC), F32)
        for k in range(4):
            shifted = dpre if k == 3 else pltpu.roll(dpre, tm + 8 - (3 - k), 0)
            dx = dx + w_ref[k:k + 1, :] * shifted[:tm, :]
        dx_ref[...] = dx.astype(BF16)

        @pl.when(i == 0)
        def _():
            dw_ref[...] = jnp.zeros_like(dw_ref)
            db_ref[...] = jnp.zeros_like(db_ref)

        dtile = dpre[:tm, :]
        for k in range(4):
            dw_ref[k:k + 1, :] += jnp.sum(dtile * taps[k][:tm, :], axis=0, keepdims=True)
        db_ref[...] += jnp.sum(dtile, axis=0, keepdims=True)

    r8, r16 = tm // 8, tm // 16
    return pl.pallas_call(
        body, name=name, grid=(width // CONV_TC, nt),
        in_specs=[pl.BlockSpec((tm, CONV_TC), lambda j, i: (i, cb0 + j)),
                  pl.BlockSpec((16, CONV_TC), lambda j, i: (jnp.maximum(i * r16 - 1, 0), cb0 + j)),
                  pl.BlockSpec((16, CONV_TC), lambda j, i: (jnp.minimum((i + 1) * r16, nt * r16 - 1), cb0 + j)),
                  pl.BlockSpec((tm, CONV_TC), lambda j, i: (i, j)),
                  pl.BlockSpec((8, CONV_TC), lambda j, i: (jnp.minimum((i + 1) * r8, nt * r8 - 1), j)),
                  pl.BlockSpec((4, CONV_TC), lambda j, i: (0, c0 + j)), pl.BlockSpec((1, CONV_TC), lambda j, i: (0, c0 + j)), ANY],
        out_specs=[pl.BlockSpec((tm, CONV_TC), lambda j, i: (i, cb0 + j)),
                   pl.BlockSpec((4, CONV_TC), lambda j, i: (0, j)), pl.BlockSpec((1, CONV_TC), lambda j, i: (0, j))],
        out_shape=[jax.ShapeDtypeStruct(dproj.shape, BF16), jax.ShapeDtypeStruct((4, width), F32), jax.ShapeDtypeStruct((1, width), F32)],
        input_output_aliases={7: 0},
    )(proj, proj, proj, dact, dact, conv_w, conv_b, dproj)


def _dt_fwd(dt_raw, dtb, alog):
    s = dt_raw.shape[0]
    nc = s // Q

    def body(r_ref, b_ref, a_ref, dt_ref, cum_ref, cumt_ref):
        dt, cum, cum_t = _dt_prep(r_ref[...], b_ref[...], a_ref[...])
        dt_ref[...] = dt
        cum_ref[...] = cum
        cumt_ref[...] = cum_t

    blk = pl.BlockSpec((Q, 128), lambda n: (n, 0))
    return pl.pallas_call(
        body, name="dt_fwd", grid=(nc,),
        in_specs=[blk, _full((1, 128)), _full((1, 128))],
        out_specs=[blk, blk, pl.BlockSpec((None, 128, Q), lambda n: (n, 0, 0))],
        out_shape=[jax.ShapeDtypeStruct((s, 128), F32), jax.ShapeDtypeStruct((s, 128), F32), jax.ShapeDtypeStruct((nc, 128, Q), F32)],
    )(dt_raw, dtb, alog)


def _dt_bwd(dt_raw, dtb, alog, ddt, dcum, dcumt, dproj):
    s = dt_raw.shape[0]
    nc = s // Q

    def body(r_ref, b_ref, a_ref, ddt_ref, dcum_ref, dcumt_ref, _, dr_ref, db_ref, da_ref):
        _, vjp = jax.vjp(_dt_prep, r_ref[...], b_ref[...], a_ref[...])
        dr, db, da = vjp((ddt_ref[...], dcum_ref[...], dcumt_ref[...]))
        dr_ref[...] = dr.astype(BF16)

        @pl.when(pl.program_id(0) == 0)
        def _():
            db_ref[...] = jnp.zeros_like(db_ref)
            da_ref[...] = jnp.zeros_like(da_ref)

        db_ref[...] += db
        da_ref[...] += da

    blk = pl.BlockSpec((Q, 128), lambda n: (n, 0))
    pblk = pl.BlockSpec((Q, 128), lambda n: (n, OFF_DT // 128))
    return pl.pallas_call(
        body, name="dt_bwd", grid=(nc,),
        in_specs=[blk, _full((1, 128)), _full((1, 128)), blk, blk, pl.BlockSpec((None, 128, Q), lambda n: (n, 0, 0)), ANY],
        out_specs=[pblk, _full((1, 128)), _full((1, 128))],
        out_shape=[jax.ShapeDtypeStruct(dproj.shape, BF16), jax.ShapeDtypeStruct((1, 128), F32), jax.ShapeDtypeStruct((1, 128), F32)],
        input_output_aliases={6: 0},
    )(dt_raw, dtb, alog, ddt, dcum, dcumt, dproj)


def _ssd_specs(chunk_of):
    gs = SSD_GPS
    xs = pl.BlockSpec((Q, 256 * gs), lambda n, g: (chunk_of(n), g))
    bm = pl.BlockSpec((Q, 128 * gs), lambda n, g: (chunk_of(n), DI // (128 * gs) + g))
    cm = pl.BlockSpec((Q, 128 * gs), lambda n, g: (chunk_of(n), (DI + D) // (128 * gs) + g))
    per_chunk = pl.BlockSpec((Q, 128), lambda n, g: (chunk_of(n), 0))
    cum_t = pl.BlockSpec((None, 128, Q), lambda n, g: (chunk_of(n), 0, 0))
    state = pl.BlockSpec((None, 256 * gs, 128), lambda n, g: (chunk_of(n), g, 0))
    vec = pl.BlockSpec((1, 128), lambda n, g: (0, 0))
    return xs, bm, cm, per_chunk, cum_t, state, vec


SSD_GPS = 8


def _aligned(v, m):
    return v if isinstance(v, int) else pl.multiple_of(v, m)


def _ssd_fwd(xbc, dt, cum, cum_t, dsk):
    s = xbc.shape[0]
    nc = s // Q
    xs, bm, cm, per_chunk, cumt_spec, state_spec, vec = _ssd_specs(lambda n: n)
    gs = SSD_GPS

    def body(x_ref, b_ref, c_ref, dt_ref, cum_ref, cumt_ref, dsk_ref, y_ref, st_ref, carry):
        n, gstep = pl.program_id(0), (0 if gs == NGRP else pl.program_id(1))
        rows = pl.ds(_aligned(gstep * (256 * gs), 256 * gs), 256 * gs)

        @pl.when(n == 0)
        def _():
            carry[rows, :] = jnp.zeros((256 * gs, 128), F32)

        st_ref[...] = carry[rows, :]
        for k in range(gs):
            xo, so, bo = 256 * k, 256 * k, 128 * k
            y0, y1, n0, n1 = _ssd_group(
                x_ref[:, xo:xo + 128], x_ref[:, xo + 128:xo + 256], st_ref[so:so + 128, :], st_ref[so + 128:so + 256, :],
                b_ref[:, bo:bo + 128], c_ref[:, bo:bo + 128], dt_ref[...], cum_ref[...], cumt_ref[...], dsk_ref[...],
                gstep * gs + k)
            y_ref[:, xo:xo + 128] = y0
            y_ref[:, xo + 128:xo + 256] = y1
            base = gstep * (256 * gs) + so
            carry[pl.ds(_aligned(base, 128), 128), :] = n0
            carry[pl.ds(_aligned(base + 128, 128), 128), :] = n1

    return pl.pallas_call(
        body, name="ssd_fwd", grid=(nc, NGRP // gs),
        in_specs=[xs, bm, cm, per_chunk, per_chunk, cumt_spec, vec],
        out_specs=[xs, state_spec],
        out_shape=[jax.ShapeDtypeStruct((s, DI), F32), jax.ShapeDtypeStruct((nc, DI, 128), F32)],
        scratch_shapes=[pltpu.VMEM((DI, 128), F32)],
    )(xbc, xbc, xbc, dt, cum, cum_t, dsk)


def _ssd_bwd(xbc, dt, cum, cum_t, dsk, states, dy, dy_t, comm=None):
    s = xbc.shape[0]
    nc = s // Q
    xs, bm, cm, per_chunk, cumt_spec, state_spec, vec = _ssd_specs(lambda n: nc - 1 - n)
    gs = SSD_GPS
    n_ci, n_co = (len(comm.ins), len(comm.outs)) if comm is not None else (0, 0)
    steps = (nc, NGRP // gs)

    def body(*refs):
        x_ref, b_ref, c_ref, dt_ref, cum_ref, cumt_ref, dsk_ref, st_ref, dy_ref, dyt_ref = refs[:10]
        ci_refs = refs[10:10 + n_ci]
        dx_ref, db_ref, dc_ref, ddt_ref, dcum_ref, dcumt_ref, ddsk_ref = refs[10 + n_ci:17 + n_ci]
        co_refs = refs[17 + n_ci:17 + n_ci + n_co]
        carry = refs[17 + n_ci + n_co]
        cs_refs = refs[18 + n_ci + n_co:]
        n, gstep = pl.program_id(0), (0 if gs == NGRP else pl.program_id(1))
        rows = pl.ds(_aligned(gstep * (256 * gs), 256 * gs), 256 * gs)
        if comm is not None:
            pl.when((pl.program_id(0) == 0) & (pl.program_id(1) == 0))(lambda: comm.start(ci_refs, co_refs, cs_refs))

        @pl.when(n == 0)
        def _():
            carry[rows, :] = jnp.zeros((256 * gs, 128), F32)

        def zero_skip_sum():
            ddsk_ref[...] = jnp.zeros_like(ddsk_ref)

        def zero_chunk_sums():
            ddt_ref[...] = jnp.zeros_like(ddt_ref)
            dcum_ref[...] = jnp.zeros_like(dcum_ref)
            dcumt_ref[...] = jnp.zeros_like(dcumt_ref)

        if isinstance(gstep, int):
            pl.when(n == 0)(zero_skip_sum)
            zero_chunk_sums()
        else:
            pl.when((n == 0) & (gstep == 0))(zero_skip_sum)
            pl.when(gstep == 0)(zero_chunk_sums)

        for k in range(gs):
            xo, so, bo = 256 * k, 256 * k, 128 * k
            base = gstep * (256 * gs) + so
            lo = pl.ds(_aligned(base, 128), 128)
            hi = pl.ds(_aligned(base + 128, 128), 128)
            dx0, dx1, ds0, ds1, dbm, dcm, ddt, dcum, dcumt, ddsk = _ssd_group_bwd(
                x_ref[:, xo:xo + 128], x_ref[:, xo + 128:xo + 256], st_ref[so:so + 128, :], st_ref[so + 128:so + 256, :],
                b_ref[:, bo:bo + 128], c_ref[:, bo:bo + 128], dt_ref[...], cum_ref[...], cumt_ref[...], dsk_ref[...],
                gstep * gs + k, dy_ref[:, xo:xo + 128].astype(F32), dy_ref[:, xo + 128:xo + 256].astype(F32),
                dyt_ref[xo:xo + 128, :].astype(F32), dyt_ref[xo + 128:xo + 256, :].astype(F32), carry[lo, :], carry[hi, :])
            dx_ref[:, xo:xo + 128] = dx0
            dx_ref[:, xo + 128:xo + 256] = dx1
            db_ref[:, bo:bo + 128] = dbm
            dc_ref[:, bo:bo + 128] = dcm
            ddt_ref[...] += ddt
            dcum_ref[...] += dcum
            dcumt_ref[...] += dcumt
            ddsk_ref[...] += ddsk
            carry[lo, :] = ds0
            carry[hi, :] = ds1

        if comm is not None:
            pl.when((pl.program_id(0) == steps[0] - 1) & (pl.program_id(1) == steps[1] - 1))(
                lambda: comm.finish(ci_refs, co_refs, cs_refs))

    grp_blk = pl.BlockSpec((Q, 128 * gs), lambda n, g: (nc - 1 - n, g))
    extra_in = comm.ins if comm is not None else []
    extra_out = comm.outs if comm is not None else []
    extra_sems = comm.sems if comm is not None else []
    return pl.pallas_call(
        body, name="ssd_bwd", grid=steps,
        in_specs=[xs, bm, cm, per_chunk, per_chunk, cumt_spec, vec, state_spec, xs,
                  pl.BlockSpec((256 * gs, Q), lambda n, g: (g, nc - 1 - n))] + [ANY] * n_ci,
        out_specs=[xs, grp_blk, grp_blk, per_chunk, per_chunk, cumt_spec, vec] + [ANY] * n_co,
        out_shape=[jax.ShapeDtypeStruct((s, DI), F32), jax.ShapeDtypeStruct((s, D), F32), jax.ShapeDtypeStruct((s, D), F32),
                   jax.ShapeDtypeStruct((s, 128), F32), jax.ShapeDtypeStruct((s, 128), F32),
                   jax.ShapeDtypeStruct((nc, 128, Q), F32), jax.ShapeDtypeStruct((1, 128), F32)] + extra_out,
        scratch_shapes=[pltpu.VMEM((DI, 128), F32)] + extra_sems,
    )(xbc, xbc, xbc, dt, cum, cum_t, dsk, states, dy, dy_t, *extra_in)


def _adam_math(w, g, m, v):
    m2 = ADAM_B1 * m + (1.0 - ADAM_B1) * g
    v2 = ADAM_B2 * v + (1.0 - ADAM_B2) * jnp.square(g)
    m_hat = m2 / (1.0 - ADAM_B1 ** ADAM_STEP)
    v_hat = v2 / (1.0 - ADAM_B2 ** ADAM_STEP)
    delta = -ADAM_LR * (m_hat / (jnp.sqrt(v_hat) + ADAM_EPS) + ADAM_WD * w)
    return delta, m2, v2


def _adam(w, g, m, v, name):
    r, c = w.shape
    tr = r if r * c * 4 <= (1 << 20) else _pick(r, (128, 64, 32, 16, 8))

    def body(w_ref, g_ref, m_ref, v_ref, d_ref, m2_ref, v2_ref):
        d, m2, v2 = _adam_math(w_ref[...], g_ref[...], m_ref[...], v_ref[...])
        d_ref[...] = d
        m2_ref[...] = m2
        v2_ref[...] = v2

    blk = pl.BlockSpec((tr, c), lambda i: (i, 0))
    shp = jax.ShapeDtypeStruct((r, c), F32)
    return pl.pallas_call(body, name=name, grid=(r // tr,), in_specs=[blk] * 4, out_specs=[blk] * 3,
                          out_shape=[shp] * 3)(w, g, m, v)


def _sum_leading(xs, name, out_dtype=F32, tr=256):
    if isinstance(xs, tuple):
        a, b = xs
        n, r, c = a.shape
        tr = _pick(r, (tr, 128, 64, 32, 16, 8))

        def body2(a_ref, b_ref, o_ref):
            o_ref[...] = (a_ref[...].astype(F32) + b_ref[...].astype(F32)).astype(out_dtype)

        blk = pl.BlockSpec((None, tr, c), lambda s, i: (s, i, 0))
        return pl.pallas_call(body2, name=name, grid=(n, r // tr), in_specs=[blk, blk], out_specs=blk,
                              out_shape=jax.ShapeDtypeStruct((n, r, c), out_dtype))(a, b)
    n, r, c = xs.shape
    tr = r if n * r * c * 4 <= (8 << 20) else _pick(r, (tr, 128, 64, 32, 16, 8))

    def body(x_ref, o_ref):
        acc = x_ref[0].astype(F32)
        for s in range(1, n):
            acc = acc + x_ref[s].astype(F32)
        o_ref[...] = acc.astype(out_dtype)

    return pl.pallas_call(body, name=name, grid=(r // tr,), in_specs=[pl.BlockSpec((n, tr, c), lambda i: (0, i, 0))],
                          out_specs=pl.BlockSpec((tr, c), lambda i: (i, 0)),
                          out_shape=jax.ShapeDtypeStruct((r, c), out_dtype))(xs)


def _mod_fwd(c8, w_mod, b_sl):
    def body(c_ref, w_ref, b_ref, o_ref):
        o_ref[...] = jnp.dot(jax.nn.silu(c_ref[...]), w_ref[...], precision=HIGHEST, preferred_element_type=F32) + b_ref[...]

    return pl.pallas_call(body, name="mod_fwd", in_specs=[VMEM, VMEM, VMEM], out_specs=VMEM,
                          out_shape=jax.ShapeDtypeStruct((N_DEV, w_mod.shape[1]), F32))(c8, w_mod, b_sl)


def _mod_wgrad(c8, dmod):
    def body(c_ref, d_ref, o_ref):
        o_ref[...] = lax.dot_general(jax.nn.silu(c_ref[...]), d_ref[...], (((0,), (0,)), ((), ())),
                                     precision=HIGHEST, preferred_element_type=F32)

    return pl.pallas_call(body, name="mod_wgrad", in_specs=[VMEM, VMEM], out_specs=VMEM,
                          out_shape=jax.ShapeDtypeStruct((D, dmod.shape[1]), F32))(c8, dmod)


def _place():
    x, y, c = lax.axis_index("x"), lax.axis_index("y"), lax.axis_index("c")
    chips = [(1 - x, y), (x, 1 - y), (1 - x, 1 - y)]
    return x, y, c, chips


def _all_gather_small(v, name):
    r, w = v.shape

    def body(v_ref, o_ref, send_sems, recv_sems, local_sem):
        x, y, c, _ = _place()
        me = 4 * x + 2 * y + c
        own = pltpu.make_async_copy(v_ref, o_ref.at[me], local_sem)
        own.start()
        sends = []
        for k in range(1, N_DEV):
            tx = 1 - x if k & 4 else x
            ty = 1 - y if k & 2 else y
            tc = 1 - c if k & 1 else c
            peer = 4 * tx + 2 * ty + tc
            cp = pltpu.make_async_remote_copy(src_ref=v_ref, dst_ref=o_ref.at[me], send_sem=send_sems.at[k - 1],
                                              recv_sem=recv_sems.at[k - 1], device_id=(tx, ty, tc), device_id_type=MESH)
            cp.start()
            sends.append((cp, peer, (tx, ty, tc)))
        for k, (cp, peer, dev) in enumerate(sends):
            pltpu.make_async_remote_copy(src_ref=v_ref, dst_ref=o_ref.at[peer], send_sem=send_sems.at[k],
                                         recv_sem=recv_sems.at[k], device_id=dev, device_id_type=MESH).wait_recv()
        for cp, _, _ in sends:
            cp.wait_send()
        own.wait()

    return pl.pallas_call(
        body, name=name, in_specs=[VMEM], out_specs=VMEM, out_shape=jax.ShapeDtypeStruct((N_DEV, r, w), v.dtype),
        scratch_shapes=[pltpu.SemaphoreType.DMA((N_DEV - 1,)), pltpu.SemaphoreType.DMA((N_DEV - 1,)), pltpu.SemaphoreType.DMA],
    )(v)


def _all_gather_weights(shards):
    n = len(shards)
    nsem = n * 3 * AG_CHUNKS

    def body(*refs):
        srcs, dsts = refs[:n], refs[n:2 * n]
        send_sems, recv_sems, fwd_send_sems, fwd_recv_sems = refs[2 * n:]
        x, y, c, chips = _place()
        q = 2 * x + y
        sibling = (x, y, 1 - c)
        sends = []
        for r in range(AG_CHUNKS):
            for t in range(n):
                hc = srcs[t].shape[0] // 2 // AG_CHUNKS
                rows = pl.ds(c * (hc * AG_CHUNKS) + r * hc, hc)
                for j, (cx, cy) in enumerate(chips):
                    k = (t * 3 + j) * AG_CHUNKS + r
                    cp = pltpu.make_async_remote_copy(src_ref=srcs[t].at[rows], dst_ref=dsts[t].at[q, rows],
                                                      send_sem=send_sems.at[k], recv_sem=recv_sems.at[k],
                                                      device_id=(cx, cy, c), device_id_type=MESH)
                    cp.start()
                    sends.append(cp)
        fwds = []
        for r in range(AG_CHUNKS):
            for t in range(n):
                hc = srcs[t].shape[0] // 2 // AG_CHUNKS
                sub = hc // D2D_SPLIT
                for j, (cx, cy) in enumerate(chips):
                    k = (t * 3 + j) * AG_CHUNKS + r
                    base = c * (hc * AG_CHUNKS) + r * hc
                    part = dsts[t].at[2 * cx + cy, pl.ds(base, hc)]
                    pltpu.make_async_remote_copy(src_ref=part, dst_ref=part, send_sem=send_sems.at[k], recv_sem=recv_sems.at[k],
                                                 device_id=(cx, cy, c), device_id_type=MESH).wait_recv()
                    for u in range(D2D_SPLIT):
                        piece = dsts[t].at[2 * cx + cy, pl.ds(base + u * sub, sub)]
                        pltpu.make_async_remote_copy(src_ref=piece, dst_ref=piece, send_sem=fwd_send_sems.at[k],
                                                     recv_sem=fwd_recv_sems.at[k], device_id=sibling, device_id_type=MESH).start()
                    fwds.append((part, k))
        for r in range(AG_CHUNKS):
            for t in range(n):
                hc = srcs[t].shape[0] // 2 // AG_CHUNKS
                for j, (cx, cy) in enumerate(chips):
                    k = (t * 3 + j) * AG_CHUNKS + r
                    part = dsts[t].at[2 * cx + cy, pl.ds((1 - c) * (hc * AG_CHUNKS) + r * hc, hc)]
                    pltpu.make_async_remote_copy(src_ref=part, dst_ref=part, send_sem=fwd_send_sems.at[k],
                                                 recv_sem=fwd_recv_sems.at[k], device_id=sibling, device_id_type=MESH).wait_recv()
        for cp in sends:
            cp.wait_send()
        for part, k in fwds:
            pltpu.make_async_remote_copy(src_ref=part, dst_ref=part, send_sem=fwd_send_sems.at[k], recv_sem=fwd_recv_sems.at[k],
                                         device_id=sibling, device_id_type=MESH).wait_send()

    return pl.pallas_call(
        body, name="all_gather_weights", in_specs=[ANY] * n, out_specs=[ANY] * n,
        out_shape=[jax.ShapeDtypeStruct((N_CHIP,) + s.shape, s.dtype) for s in shards],
        scratch_shapes=[pltpu.SemaphoreType.DMA((nsem,)), pltpu.SemaphoreType.DMA((nsem,)), pltpu.SemaphoreType.DMA((nsem,)),
                        pltpu.SemaphoreType.DMA((nsem,))],
    )(*shards)


def _exchange_halves(grads, tag=""):
    n = len(grads)

    def body(*refs):
        srcs, theirs = refs[:n], refs[n:2 * n]
        send_sems, recv_sems = refs[2 * n:]
        x, y, c, _ = _place()
        sibling = (x, y, 1 - c)
        waits = []
        for t in range(n):
            h = srcs[t].shape[1] // 2
            sub = h // D2D_SPLIT
            for s in range(N_CHIP):
                for u in range(D2D_SPLIT):
                    pltpu.make_async_remote_copy(src_ref=srcs[t].at[s, pl.ds((1 - c) * h + u * sub, sub)],
                                                 dst_ref=theirs[t].at[s, pl.ds(u * sub, sub)],
                                                 send_sem=send_sems.at[t], recv_sem=recv_sems.at[t],
                                                 device_id=sibling, device_id_type=MESH).start()
            waits.append(pltpu.make_async_remote_copy(src_ref=srcs[t].at[:, pl.ds((1 - c) * h, h)], dst_ref=theirs[t],
                                                      send_sem=send_sems.at[t], recv_sem=recv_sems.at[t],
                                                      device_id=sibling, device_id_type=MESH))
        for whole in waits:
            whole.wait()

    half = [jax.ShapeDtypeStruct((N_CHIP, g.shape[1] // 2, g.shape[2]), g.dtype) for g in grads]
    return pl.pallas_call(
        body, name=f"exchange_halves_{tag}", in_specs=[ANY] * n, out_specs=[ANY] * n, out_shape=half,
        scratch_shapes=[pltpu.SemaphoreType.DMA((n,)), pltpu.SemaphoreType.DMA((n,))],
    )(*grads)


def _chip_exchange_comm(parts):
    n = len(parts)

    def copies(srcs, dsts, sems):
        send_sems, recv_sems, local_sems = sems
        x, y, c, chips = _place()
        q = 2 * x + y
        owns = [pltpu.make_async_copy(srcs[t].at[q], dsts[t].at[q], local_sems.at[t]) for t in range(n)]
        sends, recvs = [], []
        for t in range(n):
            for j, (cx, cy) in enumerate(chips):
                args = dict(send_sem=send_sems.at[3 * t + j], recv_sem=recv_sems.at[3 * t + j],
                            device_id=(cx, cy, c), device_id_type=MESH)
                sends.append(pltpu.make_async_remote_copy(src_ref=srcs[t].at[2 * cx + cy], dst_ref=dsts[t].at[q], **args))
                part = dsts[t].at[2 * cx + cy]
                recvs.append(pltpu.make_async_remote_copy(src_ref=part, dst_ref=part, **args))
        return owns, sends, recvs

    def start(srcs, dsts, sems):
        owns, sends, _ = copies(srcs, dsts, sems)
        for cp in owns + sends:
            cp.start()

    def finish(srcs, dsts, sems):
        owns, sends, recvs = copies(srcs, dsts, sems)
        for cp in recvs:
            cp.wait_recv()
        for cp in sends:
            cp.wait_send()
        for cp in owns:
            cp.wait()

    return _Comm(parts, [jax.ShapeDtypeStruct(p.shape, p.dtype) for p in parts],
                 [pltpu.SemaphoreType.DMA((3 * n,)), pltpu.SemaphoreType.DMA((3 * n,)), pltpu.SemaphoreType.DMA((n,))], start, finish)


def _gather_relay_comm(shard):
    h = shard.shape[0] // 2
    sub = h // D2D_SPLIT

    def place_and_copies(srcs, dsts, sems):
        send_sems, recv_sems, fwd_send_sems, fwd_recv_sems = sems
        x, y, c, chips = _place()
        q = 2 * x + y
        mine, theirs = pl.ds(c * h, h), pl.ds((1 - c) * h, h)
        sends, recvs, relays, relayed = [], [], [], []
        for j, (cx, cy) in enumerate(chips):
            ici = dict(send_sem=send_sems.at[j], recv_sem=recv_sems.at[j], device_id=(cx, cy, c), device_id_type=MESH)
            d2d = dict(send_sem=fwd_send_sems.at[j], recv_sem=fwd_recv_sems.at[j], device_id=(x, y, 1 - c), device_id_type=MESH)
            sends.append(pltpu.make_async_remote_copy(src_ref=srcs[0].at[mine], dst_ref=dsts[0].at[q, mine], **ici))
            landed = dsts[0].at[2 * cx + cy, mine]
            recvs.append(pltpu.make_async_remote_copy(src_ref=landed, dst_ref=landed, **ici))
            pieces = [dsts[0].at[2 * cx + cy, pl.ds(c * h + u * sub, sub)] for u in range(D2D_SPLIT)]
            relays.append(([pltpu.make_async_remote_copy(src_ref=p, dst_ref=p, **d2d) for p in pieces],
                           pltpu.make_async_remote_copy(src_ref=landed, dst_ref=landed, **d2d)))
            other = dsts[0].at[2 * cx + cy, theirs]
            relayed.append(pltpu.make_async_remote_copy(src_ref=other, dst_ref=other, **d2d))
        return sends, recvs, relays, relayed

    def start(srcs, dsts, sems):
        for cp in place_and_copies(srcs, dsts, sems)[0]:
            cp.start()

    def middle(srcs, dsts, sems):
        _, recvs, relays, _ = place_and_copies(srcs, dsts, sems)
        for cp, (pieces, _) in zip(recvs, relays):
            cp.wait_recv()
            for piece in pieces:
                piece.start()

    def finish(srcs, dsts, sems):
        sends, _, relays, relayed = place_and_copies(srcs, dsts, sems)
        for cp in relayed:
            cp.wait_recv()
        for cp in sends:
            cp.wait_send()
        for _, whole in relays:
            whole.wait_send()

    comm = _Comm([shard], [jax.ShapeDtypeStruct((N_CHIP,) + shard.shape, shard.dtype)],
                 [pltpu.SemaphoreType.DMA((3,))] * 4, start, finish)
    comm.middle = middle
    return comm


def _all_gather_small_comm(v):
    def copies(srcs, dsts, sems):
        send_sems, recv_sems, local_sems = sems
        x, y, c, _ = _place()
        me = 4 * x + 2 * y + c
        own = pltpu.make_async_copy(srcs[0], dsts[0].at[me], local_sems.at[0])
        sends, recvs = [], []
        for k in range(1, N_DEV):
            tx = 1 - x if k & 4 else x
            ty = 1 - y if k & 2 else y
            tc = 1 - c if k & 1 else c
            args = dict(send_sem=send_sems.at[k - 1], recv_sem=recv_sems.at[k - 1], device_id=(tx, ty, tc), device_id_type=MESH)
            sends.append(pltpu.make_async_remote_copy(src_ref=srcs[0], dst_ref=dsts[0].at[me], **args))
            landed = dsts[0].at[4 * tx + 2 * ty + tc]
            recvs.append(pltpu.make_async_remote_copy(src_ref=landed, dst_ref=landed, **args))
        return own, sends, recvs

    def start(srcs, dsts, sems):
        own, sends, _ = copies(srcs, dsts, sems)
        own.start()
        for cp in sends:
            cp.start()

    def finish(srcs, dsts, sems):
        own, sends, recvs = copies(srcs, dsts, sems)
        for cp in recvs:
            cp.wait_recv()
        for cp in sends:
            cp.wait_send()
        own.wait()

    return _Comm([v], [jax.ShapeDtypeStruct((N_DEV,) + v.shape, v.dtype)],
                 [pltpu.SemaphoreType.DMA((N_DEV - 1,)), pltpu.SemaphoreType.DMA((N_DEV - 1,)), pltpu.SemaphoreType.DMA((1,))],
                 start, finish)


def _merge_comms(a, b):
    na, nao, nas = len(a.ins), len(a.outs), len(a.sems)

    def start(ci, co, cs):
        a.start(ci[:na], co[:nao], cs[:nas])
        b.start(ci[na:], co[nao:], cs[nas:])

    def finish(ci, co, cs):
        a.finish(ci[:na], co[:nao], cs[:nas])
        b.finish(ci[na:], co[nao:], cs[nas:])

    return _Comm(a.ins + b.ins, a.outs + b.outs, a.sems + b.sems, start, finish)


def _run_comm(comm, name):
    n_ci, n_co = len(comm.ins), len(comm.outs)

    def body(*refs):
        ci, co, cs = refs[:n_ci], refs[n_ci:n_ci + n_co], refs[n_ci + n_co:]
        comm.start(ci, co, cs)
        if comm.middle is not None:
            comm.middle(ci, co, cs)
        comm.finish(ci, co, cs)

    return pl.pallas_call(body, name=name, in_specs=[ANY] * n_ci, out_specs=[ANY] * n_co, out_shape=comm.outs,
                          scratch_shapes=comm.sems)(*comm.ins)


def _share_halves(halves, tag=""):
    n = len(halves)

    def body(*refs):
        srcs, dsts = refs[:n], refs[n:2 * n]
        send_sems, recv_sems = refs[2 * n:]
        x, y, c, _ = _place()
        sibling = (x, y, 1 - c)
        for t in range(n):
            h = srcs[t].shape[0]
            sub = h // (2 * D2D_SPLIT)
            for u in range(2 * D2D_SPLIT):
                pltpu.make_async_remote_copy(src_ref=srcs[t].at[pl.ds(u * sub, sub)], dst_ref=dsts[t].at[pl.ds(c * h + u * sub, sub)],
                                             send_sem=send_sems.at[t], recv_sem=recv_sems.at[t],
                                             device_id=sibling, device_id_type=MESH).start()
        for t in range(n):
            h = srcs[t].shape[0]
            pltpu.make_async_remote_copy(src_ref=srcs[t], dst_ref=dsts[t].at[pl.ds((1 - c) * h, h)], send_sem=send_sems.at[t],
                                         recv_sem=recv_sems.at[t], device_id=sibling, device_id_type=MESH).wait()

    return pl.pallas_call(
        body, name=f"share_halves_{tag}", in_specs=[ANY] * n, out_specs=[ANY] * n,
        out_shape=[jax.ShapeDtypeStruct((2 * h.shape[0], h.shape[1]), h.dtype) for h in halves],
        scratch_shapes=[pltpu.SemaphoreType.DMA((n,)), pltpu.SemaphoreType.DMA((n,))],
    )(*halves)


def _gather_weights(shards, chip):
    gathered = _all_gather_weights(shards)
    return [lax.dynamic_update_slice(g, s[None], (chip, 0, 0)) for g, s in zip(gathered, shards)]


def _reduce_scatter_begin(sends, core, tag):
    theirs = _exchange_halves(sends, tag)
    mines = [lax.dynamic_slice(g, (0, core * (g.shape[1] // 2), 0), (N_CHIP, g.shape[1] // 2, g.shape[2])) for g in sends]
    pair = [_sum_leading((m, t), f"pair_sum_{tag}{i}", out_dtype=BF16) for i, (m, t) in enumerate(zip(mines, theirs))]
    return _chip_exchange_comm(pair)


def _reduce_scatter_end(contrib, core, tag):
    halves = [_sum_leading(c, f"chip_sum_{tag}{i}") for i, c in enumerate(contrib)]
    shared = _share_halves(halves, tag)
    return [lax.dynamic_update_slice(full, mine, (core * mine.shape[0], 0)) for full, mine in zip(shared, halves)]


def _reduce_scatter(sends, core, tag=""):
    contrib = _run_comm(_reduce_scatter_begin(sends, core, tag), f"exchange_chips_{tag}")
    return _reduce_scatter_end(contrib, core, tag)


def _pack(arrs):
    rows = []
    for a in arrs:
        flat = a.astype(F32).reshape(-1)
        pad = (-flat.shape[0]) % 1024
        rows.append(jnp.pad(flat, (0, pad)).reshape(-1, 128))
    return jnp.concatenate(rows, axis=0)


def _unpack(buf, shapes):
    out, r = [], 0
    for shp in shapes:
        size = 1
        for d in shp:
            size *= d
        nr = (size + 1023) // 1024 * 8
        out.append(buf[r:r + nr].reshape(-1)[:size].reshape(shp))
        r += nr
    return out


R_BG, R_BS, R_OUT, R_FF = D // N_CHIP, DI // N_CHIP, D // N_CHIP, DFF // N_CHIP


def _pack_rest_shard(w_bg, w_bs, w_out, w_ff2, w_ff1):
    return jnp.concatenate([w_bg, w_bs, w_out, w_ff2, w_ff1], axis=0).astype(BF16)


def _unpack_rest(g):
    o1, o2, o3, o4 = R_BG, R_BG + R_BS, R_BG + R_BS + R_OUT, R_BG + R_BS + R_OUT + R_FF
    return (g[:, :o1].reshape(D, D), g[:, o1:o2].reshape(DI, D), g[:, o2:o3].reshape(D, D),
            jnp.transpose(g[:, o4:], (1, 0, 2)).reshape(D, DFF), g[:, o3:o4].reshape(DFF, D))


def _local_step(x, t, mod, w_in_r, rest, gm_norm_w, gm_ws, gm_bs, conv_w, conv_b, dt_bias, a_log, d_skip, ssm_norm_w,
                final_norm_w, place=None):
    sh1, sc1, g1, sh2, sc2, g2 = [mod[:, i * D:(i + 1) * D] for i in range(6)]
    ws_b = jnp.where(jnp.tril(jnp.ones((Q, Q), bool))[None], gm_ws, 0.0).astype(BF16)
    bs_t = gm_bs.T
    pad32 = lambda v: jnp.pad(v, ((0, 0), (0, 128 - NHEAD)))
    dtb, alog, dsk = pad32(dt_bias), pad32(a_log), pad32(d_skip)

    h1 = _prenorm(x, sc1, sh1)
    gdt = (F32,) if place is None else (BF16,)
    if place is None:
        (proj,) = _matmul(h1, w_in_r, "nn", name="mm_proj", out_dtypes=(BF16,), tn=1152)
        w_bg, w_bs, w_out, w_ff1, w_ff2 = rest
    else:
        chip, core = place
        proj, g_rest = _matmul(h1, w_in_r, "nn", name="mm_proj", out_dtypes=(BF16,), tn=1152, comm=_gather_relay_comm(rest))
        w_bg, w_bs, w_out, w_ff1, w_ff2 = _unpack_rest(lax.dynamic_update_slice(g_rest, rest[None], (chip, 0, 0)))
    (dt_raw,) = _matmul(h1, w_in_r[:, OFF_DT:OFF_DT + 128], "nn", name="mm_dt", tn=128)
    y_a = _sgu_fwd(proj, gm_norm_w, ws_b, bs_t)
    xbc = _conv_fwd(proj, conv_w, conv_b)
    dt, cum, cum_t = _dt_fwd(dt_raw, dtb, alog)
    y_ssd, states = _ssd_fwd(xbc, dt, cum, cum_t, dsk)
    y_b = _gatenorm_fwd(y_ssd, proj, ssm_norm_w)
    (ba,) = _matmul(y_a, w_bg, "nn", name="mm_branch_gm", out_dtypes=(BF16,))
    (bb,) = _matmul(y_b, w_bs, "nn", name="mm_branch_ssm", out_dtypes=(BF16,), tm=512, tk=2048)
    mixed = _mix_fwd(proj, ba, bb)
    (mo,) = _matmul(mixed, w_out, "nn", name="mm_out", out_dtypes=(BF16,))
    x1, h2 = _resid_norm(x, mo, g1, sc2, sh2)
    f, act = _matmul(h2, w_ff1, "nn", name="mm_ff1", out_dtypes=(BF16, BF16),
                     epi=lambda acc: (acc, jnp.square(jnp.maximum(acc, 0.0))))
    (fo,) = _matmul(act, w_ff2, "nn", name="mm_ff2", out_dtypes=(BF16,), tm=512, tk=4096)
    loss8, dx2, dfo, dg2, dfw = _loss_head(x1, fo, g2, final_norm_w, t)

    (df,) = _matmul(dfo, w_ff2, "nt", name="mm_ff2_dx", out_dtypes=(BF16,), epi_ins=(f,),
                    epi=lambda acc, fv: (acc * (2.0 * jnp.maximum(fv.astype(F32), 0.0)),))
    (g_ff2,) = _matmul(act, dfo, "tn", name="mm_ff2_dw", out_dtypes=gdt, tm=512, tk=4096)
    (dh2,) = _matmul(df, w_ff1, "nt", name="mm_ff1_dx", out_dtypes=(BF16,), tm=512, tk=4096)
    (g_ff1,) = _matmul(h2, df, "tn", name="mm_ff1_dw", out_dtypes=gdt, tm=512, tk=4096)
    dx1, dmo, dg1, dsc2, dsh2 = _resid_norm_bwd(x1, mo, g1, sc2, sh2, dh2, dx2)
    (dmixed,) = _matmul(dmo, w_out, "nt", name="mm_out_dx", out_dtypes=(BF16,))
    (g_out,) = _matmul(mixed, dmo, "tn", name="mm_out_dw", out_dtypes=gdt, tm=512, tk=4096)
    early = None
    if place is not None:
        stack = jnp.concatenate([g_out.reshape(N_CHIP, R_OUT, D), g_ff2.reshape(N_CHIP, R_FF, D),
                                 jnp.transpose(g_ff1.reshape(D, N_CHIP, D), (1, 0, 2))], axis=1).astype(BF16)
        early = _reduce_scatter_begin([stack], core, "early")
    dproj, dba, dbb = _mix_bwd(proj, ba, bb, dmixed)
    (dy_a,) = _matmul(dba, w_bg, "nt", name="mm_branch_gm_dx", out_dtypes=(BF16,))
    (g_bg,) = _matmul(y_a, dba, "tn", name="mm_branch_gm_dw", out_dtypes=gdt, tm=512, tk=4096)
    (dy_b,) = _matmul(dbb, w_bs, "nt", name="mm_branch_ssm_dx", out_dtypes=(BF16,))
    (g_bs,) = _matmul(y_b, dbb, "tn", name="mm_branch_ssm_dw", out_dtypes=gdt, tm=512, tk=4096)
    dproj, d_gm_norm, d_ws, d_bs_t = _sgu_bwd(proj, gm_norm_w, ws_b, bs_t, dy_a, dproj)
    dy_ssd, dy_ssd_t, dproj, d_ssm_norm = _gatenorm_bwd(y_ssd, proj, ssm_norm_w, dy_b, dproj)
    dxs, dbm, dcm, ddt, dcum, dcum_t, d_dsk, *early_contrib = _ssd_bwd(xbc, dt, cum, cum_t, dsk, states, dy_ssd, dy_ssd_t,
                                                                       comm=early)
    dproj, d_dtb, d_alog = _dt_bwd(dt_raw, dtb, alog, ddt, dcum, dcum_t, dproj)
    dproj, dw_x, db_x = _conv_bwd(proj, conv_w, conv_b, dxs, dproj, 0, "conv_bwd_x")
    dproj, dw_b, db_b = _conv_bwd(proj, conv_w, conv_b, dbm, dproj, DI, "conv_bwd_b")
    dproj, dw_c, db_c = _conv_bwd(proj, conv_w, conv_b, dcm, dproj, DI + D, "conv_bwd_c")
    d_conv_w = jnp.concatenate([dw_x, dw_b, dw_c], axis=1)
    d_conv_b = jnp.concatenate([db_x, db_b, db_c], axis=1)
    (g_in_r,) = _matmul(h1, dproj, "tn", name="mm_proj_dw", out_dtypes=gdt, tm=512, tn=1152, tk=4096)
    late = None
    if place is not None:
        send_in = _w_in_to_blocks(g_in_r)
        stack = jnp.concatenate([g_bg.reshape(N_CHIP, R_BG, D), g_bs.reshape(N_CHIP, R_BS, D)], axis=1).astype(BF16)
        late = _reduce_scatter_begin([send_in, stack], core, "late")
    small = dict(gm_ws=d_ws, gm_norm_w=d_gm_norm, gm_bs=d_bs_t.T, conv_w=d_conv_w, conv_b=d_conv_b,
                 dt_bias=d_dtb[:, :NHEAD], a_log=d_alog[:, :NHEAD], d_skip=d_dsk[:, :NHEAD],
                 ssm_norm_w=d_ssm_norm, final_norm_w=dfw, loss=loss8[:1, :1],
                 dmod=jnp.concatenate([jnp.zeros((1, 2 * D), F32), dg1, dsh2, dsc2, dg2], axis=1))
    if place is not None:
        late = _merge_comms(late, _all_gather_small_comm(_pack([small[k] for k in SMALL_KEYS])))
    dh1, *rode = _matmul(dproj, w_in_r, "nt", name="mm_proj_dx", out_dtypes=(BF16,), tk=3456, comm=late)
    grad_x, dsc1, dsh1 = _prenorm_bwd(x, sc1, sh1, dh1, dx1)
    first = jnp.concatenate([dsh1, dsc1], axis=1)
    if place is None:
        small["dmod"] = jnp.concatenate([first, small["dmod"][:, 2 * D:]], axis=1)
        return grad_x, small, dict(w_in_r=g_in_r, w_bg=g_bg, w_bs=g_bs, w_out=g_out, w_ff1=g_ff1, w_ff2=g_ff2)
    late_contrib, small_all = rode[:2], rode[2]
    first_all = _all_gather_small(_pack([first]), "all_gather_mod_grads")
    (s_early,) = _reduce_scatter_end(early_contrib, core, "early")
    s_in, s_late = _reduce_scatter_end(late_contrib, core, "late")
    o1, o2 = R_OUT, R_OUT + R_FF
    big = dict(w_in=s_in, w_bg=s_late[:R_BG], w_bs=s_late[R_BG:], w_out=s_early[:o1], w_ff2=s_early[o1:o2], w_ff1=s_early[o2:])
    return grad_x, (small_all, first_all), big


SMALL_KEYS = ("gm_ws", "gm_norm_w", "gm_bs", "conv_w", "conv_b", "dt_bias", "a_log", "d_skip", "ssm_norm_w",
              "final_norm_w", "dmod", "loss")
SMALL_SHAPES = ((GM_G, Q, Q), (1, D), (GM_G, Q), (4, CONV), (1, CONV), (1, NHEAD), (1, NHEAD), (1, NHEAD), (1, DI),
                (1, D), (1, 6 * D), (1, 1))


def _reorder_w_in(w_full):
    k = w_full.shape[0]
    return jnp.concatenate([w_full[:, :8192], w_full[:, 8224:], w_full[:, 8192:8224],
                            jnp.zeros((k, W_IN_R - W_IN), w_full.dtype)], axis=1)


W_SH = W_IN // N_CHIP
TAIL = 8192 - (N_CHIP - 1) * W_SH


def _w_in_from_blocks(g):
    last = g[N_CHIP - 1]
    return jnp.concatenate([g[0], g[1], g[2], last[:, :TAIL], last[:, TAIL + NHEAD:], last[:, TAIL:TAIL + NHEAD],
                            jnp.zeros((g.shape[1], W_IN_R - W_IN), g.dtype)], axis=1)


def _w_in_to_blocks(g_r):
    cut = (N_CHIP - 1) * W_SH
    last = jnp.concatenate([g_r[:, cut:8192], g_r[:, OFF_DT:OFF_DT + NHEAD], g_r[:, 8192:OFF_DT]], axis=1)
    return jnp.stack([g_r[:, :W_SH], g_r[:, W_SH:2 * W_SH], g_r[:, 2 * W_SH:cut], last], axis=0)


def _restore_w_in(g_r):
    return jnp.concatenate([g_r[:, :8192], g_r[:, OFF_DT:OFF_DT + NHEAD], g_r[:, 8192:OFF_DT]], axis=1)


def kernel(x, c, w_mod, b_mod, w_in, gm_norm_w, gm_ws, gm_bs, conv_w, conv_b, dt_bias, a_log, d_skip, ssm_norm_w, w_branch_gm, w_branch_ssm, w_out, w_ff1, w_ff2, final_norm_w, loss_target, m_w_mod, m_b_mod, m_w_in, m_gm_norm_w, m_gm_ws, m_gm_bs, m_conv_w, m_conv_b, m_dt_bias, m_a_log, m_d_skip, m_ssm_norm_w, m_w_branch_gm, m_w_branch_ssm, m_w_out, m_w_ff1, m_w_ff2, m_final_norm_w, v_w_mod, v_b_mod, v_w_in, v_gm_norm_w, v_gm_ws, v_gm_bs, v_conv_w, v_conv_b, v_dt_bias, v_a_log, v_d_skip, v_ssm_norm_w, v_w_branch_gm, v_w_branch_ssm, v_w_out, v_w_ff1, v_w_ff2, v_final_norm_w):
    ax, ay, ac = lax.axis_index("x"), lax.axis_index("y"), lax.axis_index("c")
    chip = 2 * ax + ay
    dev = 2 * chip + ac
    seq = x.shape[1]
    nmod = w_mod.shape[2]

    first = jnp.concatenate([c, conv_w[0], jnp.zeros((3, D), F32)], axis=0)
    first_all = _all_gather_small(first, "all_gather_cond")
    c8 = first_all[:, 0, :]
    conv_w_full = jnp.concatenate([first_all[2 * k, 1:5, :] for k in range(N_CHIP)], axis=1)
    b_sl = lax.dynamic_slice(b_mod, (0, chip * nmod), (1, nmod))
    mod_part = _mod_fwd(c8, w_mod[0], b_sl)
    mod_all = _all_gather_small(mod_part, "all_gather_mod")
    mod = jnp.concatenate([lax.dynamic_slice(mod_all, (2 * k, dev, 0), (1, 1, nmod))[0] for k in range(N_CHIP)], axis=1)

    (g_in,) = _gather_weights([w_in[0].astype(BF16)], chip)
    w_in_r = _w_in_from_blocks(g_in)
    rest = _pack_rest_shard(w_branch_gm[0], w_branch_ssm[0], w_out[0], w_ff2[0], w_ff1[0])

    grad_x, small, big = _local_step(
        x[0], loss_target[0], mod, w_in_r, rest, gm_norm_w, gm_ws[0], gm_bs[0], conv_w_full, conv_b, dt_bias, a_log,
        d_skip, ssm_norm_w, final_norm_w.reshape(1, D), place=(chip, ac))

    small_all, first_all = small
    small_sum = _sum_leading(small_all, "sum_small_grads")
    first_sum = _sum_leading(first_all, "sum_mod_grads")
    s_ws, s_gnw, s_bs, s_cw, s_cb, s_dtb, s_alog, s_dsk, s_snw, s_fnw, s_bmod, s_loss = _unpack(small_sum, SMALL_SHAPES)
    head = (1, 2 * D)
    s_bmod = jnp.concatenate([_unpack(first_sum, [head])[0], s_bmod[:, 2 * D:]], axis=1)
    dmod_all = jnp.stack([jnp.concatenate([_unpack(first_all[k], [head])[0][0], _unpack(small_all[k], SMALL_SHAPES)[10][0][2 * D:]])
                          for k in range(N_DEV)], axis=0)
    g_w_mod = _mod_wgrad(c8, lax.dynamic_slice(dmod_all, (0, chip * nmod), (N_DEV, nmod)))
    g_conv_w = lax.dynamic_slice(s_cw, (0, chip * (CONV // N_CHIP)), (4, CONV // N_CHIP))

    g_w_in, g_w_bg, g_w_bs, g_w_out, g_w_ff2, g_w_ff1 = (big[k] for k in ("w_in", "w_bg", "w_bs", "w_out", "w_ff2", "w_ff1"))

    def adam_big(w, g, m, v, name):
        d, m2, v2 = _adam(w.reshape(g.shape), g, m.reshape(g.shape), v.reshape(g.shape), name)
        return g.reshape(w.shape), d.reshape(w.shape), m2.reshape(w.shape), v2.reshape(w.shape)

    res = {}
    res["w_mod"] = adam_big(w_mod, g_w_mod, m_w_mod, v_w_mod, "adam_w_mod")
    res["w_in"] = adam_big(w_in, g_w_in, m_w_in, v_w_in, "adam_w_in")
    res["w_branch_gm"] = adam_big(w_branch_gm, g_w_bg, m_w_branch_gm, v_w_branch_gm, "adam_w_branch_gm")
    res["w_branch_ssm"] = adam_big(w_branch_ssm, g_w_bs, m_w_branch_ssm, v_w_branch_ssm, "adam_w_branch_ssm")
    res["w_out"] = adam_big(w_out, g_w_out, m_w_out, v_w_out, "adam_w_out")
    res["w_ff1"] = adam_big(w_ff1, g_w_ff1, m_w_ff1, v_w_ff1, "adam_w_ff1")
    res["w_ff2"] = adam_big(w_ff2, g_w_ff2, m_w_ff2, v_w_ff2, "adam_w_ff2")

    names = ("b_mod", "gm_norm_w", "gm_ws", "gm_bs", "conv_w", "conv_b", "dt_bias", "a_log", "d_skip", "ssm_norm_w", "final_norm_w")
    ws = (b_mod, gm_norm_w, gm_ws, gm_bs, conv_w, conv_b, dt_bias, a_log, d_skip, ssm_norm_w, final_norm_w)
    ms = (m_b_mod, m_gm_norm_w, m_gm_ws, m_gm_bs, m_conv_w, m_conv_b, m_dt_bias, m_a_log, m_d_skip, m_ssm_norm_w, m_final_norm_w)
    vs = (v_b_mod, v_gm_norm_w, v_gm_ws, v_gm_bs, v_conv_w, v_conv_b, v_dt_bias, v_a_log, v_d_skip, v_ssm_norm_w, v_final_norm_w)
    gs = (s_bmod, s_gnw, s_ws, s_bs, g_conv_w, s_cb, s_dtb, s_alog, s_dsk, s_snw, s_fnw)
    gs = [g.reshape(w.shape) for g, w in zip(gs, ws)]
    shapes = [w.shape for w in ws]
    d_p, m_p, v_p = _adam(_pack(ws), _pack(gs), _pack(ms), _pack(vs), "adam_small")
    for name, g, d, m2, v2 in zip(names, gs, _unpack(d_p, shapes), _unpack(m_p, shapes), _unpack(v_p, shapes)):
        res[name] = (g, d, m2, v2)

    order = ("w_mod", "b_mod", "w_in", "gm_norm_w", "gm_ws", "gm_bs", "conv_w", "conv_b", "dt_bias", "a_log", "d_skip",
             "ssm_norm_w", "w_branch_gm", "w_branch_ssm", "w_out", "w_ff1", "w_ff2", "final_norm_w")
    loss = s_loss.reshape(())
    return (loss, grad_x.reshape(x.shape), *[res[k][0] for k in order], *[res[k][1] for k in order],
            *[res[k][2] for k in order], *[res[k][3] for k in order])
```

```python
import functools

import jax
import jax.numpy as jnp
from jax import lax
from jax.experimental import pallas as pl
from jax.experimental.pallas import tpu as pltpu

F32 = jnp.float32
BF16 = jnp.bfloat16
MESH = pl.DeviceIdType.MESH
HIGHEST = lax.Precision.HIGHEST

D = 1024
EPS = 1e-6
Q = 128
GM_G = 8
NHEAD = 32
NGRP = 8
DI = 2048
CONV = 4096
DFF = 4096
W_IN = 10272
W_IN_R = 10368
OFF_Z, OFF_XBC, OFF_GA, OFF_DT = 2048, 4096, 8192, 10240
N_CHIP = 4
N_DEV = 8
AG_CHUNKS = 8
D2D_SPLIT = 4

ADAM_LR, ADAM_B1, ADAM_B2, ADAM_EPS, ADAM_WD, ADAM_STEP = 0.001, 0.9, 0.999, 1e-08, 0.01, 10

ANY = pl.BlockSpec(memory_space=pl.ANY)
VMEM = pl.BlockSpec(memory_space=pltpu.VMEM)


def _full(shape):
    return pl.BlockSpec(shape, lambda *_: (0,) * len(shape))


def _pick(n, prefs):
    for p in prefs:
        if n % p == 0:
            return p
    return n


def _rms(x):
    return x * lax.rsqrt(jnp.mean(x * x, axis=-1, keepdims=True) + EPS)


def _modnorm(x, sc, sh):
    return _rms(x) * (1.0 + sc) + sh


def _sgu_pre(u, v, w):
    return jax.nn.gelu(u), _rms(jax.nn.gelu(v)) * w


def _gatenorm(y, z, w):
    g = y * jax.nn.silu(z)
    return _rms(g) * w


def _mix(ga, gb, ba, bb):
    return jax.nn.sigmoid(ga) * ba + jax.nn.sigmoid(gb) * bb


def _loss_tile(x1, fo, g2, fw, t):
    x2 = x1 + g2 * fo
    y = _rms(x2) * fw
    err = jnp.square(y - t)
    return 0.5 * jnp.sum(jnp.mean(err, axis=-1))


def _tril(n):
    r = lax.broadcasted_iota(jnp.int32, (n, n), 0)
    c = lax.broadcasted_iota(jnp.int32, (n, n), 1)
    return r >= c


def _dt_prep(dtr, dtb, alog):
    dt = jax.nn.softplus(dtr + dtb)
    a = dt * (-jnp.exp(alog))
    ones = _tril(Q).astype(F32)
    cum = jnp.dot(ones, a, precision=HIGHEST, preferred_element_type=F32)
    cum_t = lax.dot_general(a, ones, (((0,), (1,)), ((), ())), precision=HIGHEST, preferred_element_type=F32)
    return dt, cum, cum_t


def _ssd_group(x0, x1, s0, s1, bm, cm, dt, cum, cum_t, dsk, grp):
    lane = lax.broadcasted_iota(jnp.int32, (1, 128), 1)
    sub = lax.broadcasted_iota(jnp.int32, (128, 1), 0)
    future = jnp.where(_tril(Q), 0.0, -jnp.inf)
    half = lane < 64
    half_rows = sub < 64
    bmb = bm.astype(BF16)
    cmb = cm.astype(BF16)
    cb = lax.dot_general(cmb, bmb, (((1,), (1,)), ((), ())), preferred_element_type=F32)

    def col(v, h):
        return jnp.sum(jnp.where(lane == h, v, 0.0), axis=1, keepdims=True)

    def row(v, h):
        return jnp.sum(jnp.where(sub == h, v, 0.0), axis=0, keepdims=True)

    def last(c):
        return jnp.sum(jnp.where(sub == Q - 1, c, 0.0), axis=0, keepdims=True)

    outs, states = [], []
    for p, (xp, sp) in enumerate(((x0, s0), (x1, s1))):
        h_a = 4 * grp + 2 * p
        h_b = h_a + 1
        dt_a, dt_b = col(dt, h_a), col(dt, h_b)
        cum_a, cum_b = col(cum, h_a), col(cum, h_b)
        row_a, row_b = row(cum_t, h_a), row(cum_t, h_b)
        last_a, last_b = last(cum_a), last(cum_b)
        xdt = xp * jnp.where(half, dt_a, dt_b)
        xdtb = xdt.astype(BF16)
        m_a = (cb * jnp.exp(cum_a - row_a + future)).astype(BF16)
        m_b = (cb * jnp.exp(cum_b - row_b + future)).astype(BF16)
        y_intra = jnp.where(half, jnp.dot(m_a, xdtb, preferred_element_type=F32),
                            jnp.dot(m_b, xdtb, preferred_element_type=F32))
        y_inter = lax.dot_general(cmb, sp.astype(BF16), (((1,), (1,)), ((), ())), preferred_element_type=F32)
        y_inter = y_inter * jnp.where(half, jnp.exp(cum_a), jnp.exp(cum_b))
        w_end = jnp.where(half, jnp.exp(last_a - cum_a), jnp.exp(last_b - cum_b))
        upd = lax.dot_general((xdt * w_end).astype(BF16), bmb, (((0,), (0,)), ((), ())), preferred_element_type=F32)
        states.append(sp * jnp.where(half_rows, jnp.exp(last_a), jnp.exp(last_b)) + upd)
        outs.append(y_intra + y_inter + xp * jnp.where(half, col(dsk, h_a), col(dsk, h_b)))
    return outs[0], outs[1], states[0], states[1]


def _ssd_group_bwd(x0, x1, s0, s1, bm, cm, dt, cum, cum_t, dsk, grp, dy0, dy1, dyt0, dyt1, dn0, dn1):
    nt = (((1,), (1,)), ((), ()))
    lane = lax.broadcasted_iota(jnp.int32, (1, 128), 1)
    sub = lax.broadcasted_iota(jnp.int32, (128, 1), 0)
    r = lax.broadcasted_iota(jnp.int32, (Q, Q), 0)
    c = lax.broadcasted_iota(jnp.int32, (Q, Q), 1)
    future = jnp.where(r >= c, 0.0, -jnp.inf)
    future_t = jnp.where(c >= r, 0.0, -jnp.inf)
    half = lane < 64
    half_rows = sub < 64
    bmb = bm.astype(BF16)
    cmb = cm.astype(BF16)
    cb = lax.dot_general(cmb, bmb, nt, preferred_element_type=F32)
    cbt = lax.dot_general(bmb, cmb, nt, preferred_element_type=F32)

    def col(v, h):
        return jnp.sum(jnp.where(lane == h, v, 0.0), axis=1, keepdims=True)

    def row(v, h):
        return jnp.sum(jnp.where(sub == h, v, 0.0), axis=0, keepdims=True)

    def lane_sums(v, sel):
        hi = v.astype(BF16)
        lo = (v - hi.astype(F32)).astype(BF16)
        return jnp.dot(hi, sel, preferred_element_type=F32) + jnp.dot(lo, sel, preferred_element_type=F32)

    def head_sum(v, mask):
        return jnp.sum(jnp.where(mask, v, 0.0), axis=1, keepdims=True)

    src = lax.broadcasted_iota(jnp.int32, (128, 128), 0)
    dst = lax.broadcasted_iota(jnp.int32, (128, 128), 1)

    dcb = jnp.zeros((Q, Q), F32)
    dcbt = jnp.zeros((Q, Q), F32)
    dbm = jnp.zeros((Q, 128), F32)
    dcm = jnp.zeros((Q, 128), F32)
    ddt = jnp.zeros((Q, 128), F32)
    dcum = jnp.zeros((Q, 128), F32)
    dcumt = jnp.zeros((128, Q), F32)
    ddsk = jnp.zeros((1, 128), F32)
    dlast = jnp.zeros((1, 128), F32)
    dxs, dss = [], []
    for p, (xp, sp, dy, dyt, dns) in enumerate(((x0, s0, dy0, dyt0, dn0), (x1, s1, dy1, dyt1, dn1))):
        h_a = 4 * grp + 2 * p
        h_b = h_a + 1
        dt_a, dt_b = col(dt, h_a), col(dt, h_b)
        cum_a, cum_b = col(cum, h_a), col(cum, h_b)
        row_a, row_b = row(cum_t, h_a), row(cum_t, h_b)
        last_a = jnp.sum(jnp.where(sub == Q - 1, cum_a, 0.0), axis=0, keepdims=True)
        last_b = jnp.sum(jnp.where(sub == Q - 1, cum_b, 0.0), axis=0, keepdims=True)
        dtp = jnp.where(half, dt_a, dt_b)
        xdt = xp * dtp
        xdtb = xdt.astype(BF16)
        l_a, l_b = jnp.exp(cum_a - row_a + future), jnp.exp(cum_b - row_b + future)
        lt_a, lt_b = jnp.exp(row_a - cum_a + future_t), jnp.exp(row_b - cum_b + future_t)
        dya = jnp.where(half, dy, 0.0)
        dya_b, dyb_b = dya.astype(BF16), (dy - dya).astype(BF16)
        dm_a = lax.dot_general(dya_b, xdtb, nt, preferred_element_type=F32)
        dm_b = lax.dot_general(dyb_b, xdtb, nt, preferred_element_type=F32)
        dmt_a = lax.dot_general(xdtb, dya_b, nt, preferred_element_type=F32)
        dmt_b = lax.dot_general(xdtb, dyb_b, nt, preferred_element_type=F32)
        dxdt = (jnp.dot((cbt * lt_a).astype(BF16), dya_b, preferred_element_type=F32)
                + jnp.dot((cbt * lt_b).astype(BF16), dyb_b, preferred_element_type=F32))
        pair_sel = jnp.where(dst == jnp.where(src < 64, h_a, h_b), 1.0, 0.0).astype(BF16)
        to_a = jnp.where(dst == h_a, 1.0, 0.0).astype(BF16)
        to_b = jnp.where(dst == h_b, 1.0, 0.0).astype(BF16)
        dseg_a, dseg_b = dm_a * (cb * l_a), dm_b * (cb * l_b)
        dcum = dcum + lane_sums(dseg_a, to_a) + lane_sums(dseg_b, to_b)
        drow_a = -jnp.sum(dseg_a, axis=0, keepdims=True)
        drow_b = -jnp.sum(dseg_b, axis=0, keepdims=True)
        dcb = dcb + dm_a * l_a + dm_b * l_b
        dcbt = dcbt + dmt_a * lt_a + dmt_b * lt_b
        spb = sp.astype(BF16)
        y0 = lax.dot_general(cmb, spb, nt, preferred_element_type=F32)
        dy0s = dy * jnp.where(half, jnp.exp(cum_a), jnp.exp(cum_b))
        g = dy0s * y0
        dcm = dcm + jnp.dot(dy0s.astype(BF16), spb, preferred_element_type=F32)
        dsp = jnp.dot((dyt * jnp.where(half_rows, jnp.exp(row_a), jnp.exp(row_b))).astype(BF16), cmb, preferred_element_type=F32)
        wp = jnp.where(half, jnp.exp(last_a - cum_a), jnp.exp(last_b - cum_b))
        xw = xdt * wp
        dnsb = dns.astype(BF16)
        dxw = lax.dot_general(bmb, dnsb, nt, preferred_element_type=F32)
        dbm = dbm + jnp.dot(xw.astype(BF16), dnsb, preferred_element_type=F32)
        dxdt = dxdt + dxw * wp
        gw = dxw * xw
        dcum = dcum + lane_sums(g - gw, pair_sel)
        el_a, el_b = jnp.exp(last_a), jnp.exp(last_b)
        dsp = dsp + dns * jnp.where(half_rows, el_a, el_b)
        gs = dns * sp
        gs_all = jnp.sum(gs, axis=0, keepdims=True)
        gs_a = jnp.sum(jnp.sum(jnp.where(half_rows, gs, 0.0), axis=0, keepdims=True), axis=1, keepdims=True)
        gs_b = jnp.sum(gs_all, axis=1, keepdims=True) - gs_a
        gw_cols = jnp.sum(gw, axis=0, keepdims=True)
        dlast_a = head_sum(gw_cols, half) + el_a * gs_a
        dlast_b = head_sum(gw_cols, ~half) + el_b * gs_b
        dlast = dlast + jnp.where(lane == h_a, dlast_a, 0.0) + jnp.where(lane == h_b, dlast_b, 0.0)
        ddt = ddt + lane_sums(dxdt * xp, pair_sel)
        gk_cols = jnp.sum(dy * xp, axis=0, keepdims=True)
        dxs.append(dxdt * dtp + dy * jnp.where(half, col(dsk, h_a), col(dsk, h_b)))
        dss.append(dsp)
        dcumt = dcumt + jnp.where(sub == h_a, drow_a, 0.0) + jnp.where(sub == h_b, drow_b, 0.0)
        ddsk = ddsk + jnp.where(lane == h_a, head_sum(gk_cols, half), 0.0) + jnp.where(lane == h_b, head_sum(gk_cols, ~half), 0.0)
    dcum = dcum + jnp.where(sub == Q - 1, dlast, 0.0)
    dcm = dcm + jnp.dot(dcb.astype(BF16), bmb, preferred_element_type=F32)
    dbm = dbm + jnp.dot(dcbt.astype(BF16), cmb, preferred_element_type=F32)
    return dxs[0], dxs[1], dss[0], dss[1], dbm, dcm, ddt, dcum, dcumt, ddsk


class _Comm:
    def __init__(self, ins, outs, sems, start, finish):
        self.ins, self.outs, self.sems, self.start, self.finish = list(ins), list(outs), list(sems), start, finish
        self.middle = None


def _matmul(a, b, mode, *, name, out_dtypes=(F32,), epi=None, epi_ins=(), tm=1024, tn=1024, tk=1024, comm=None):
    if mode == "nn":
        (m, k), n = a.shape, b.shape[1]
    elif mode == "nt":
        (m, k), n = a.shape, b.shape[0]
    else:
        (k, m), n = a.shape, b.shape[1]
    tm, tn, tk = _pick(m, (tm, 512, 256, 128)), _pick(n, (tn, 1152, 1024, 512, 384, 256, 128)), _pick(k, (tk, 1152, 1024, 512, 256, 128))
    nk = k // tk
    if mode == "nn":
        a_spec = pl.BlockSpec((tm, tk), lambda i, j, kk: (i, kk))
        b_spec = pl.BlockSpec((tk, tn), lambda i, j, kk: (kk, j))
        dims = (((1,), (0,)), ((), ()))
    elif mode == "nt":
        a_spec = pl.BlockSpec((tm, tk), lambda i, j, kk: (i, kk))
        b_spec = pl.BlockSpec((tn, tk), lambda i, j, kk: (j, kk))
        dims = (((1,), (1,)), ((), ()))
    else:
        a_spec = pl.BlockSpec((tk, tm), lambda i, j, kk: (kk, i))
        b_spec = pl.BlockSpec((tk, tn), lambda i, j, kk: (kk, j))
        dims = (((0,), (0,)), ((), ()))
    o_spec = pl.BlockSpec((tm, tn), lambda i, j, kk: (i, j))
    n_epi, n_out = len(epi_ins), len(out_dtypes)
    n_ci, n_co, n_cs = (len(comm.ins), len(comm.outs), len(comm.sems)) if comm is not None else (0, 0, 0)
    grid = (m // tm, n // tn, nk)

    def body(*refs):
        a_ref, b_ref = refs[0], refs[1]
        e_refs = refs[2:2 + n_epi]
        ci_refs = refs[2 + n_epi:2 + n_epi + n_ci]
        o_refs = refs[2 + n_epi + n_ci:2 + n_epi + n_ci + n_out]
        co_refs = refs[2 + n_epi + n_ci + n_out:2 + n_epi + n_ci + n_out + n_co]
        acc_ref = refs[2 + n_epi + n_ci + n_out + n_co]
        cs_refs = refs[3 + n_epi + n_ci + n_out + n_co:]
        if comm is not None:
            ids = [pl.program_id(d) for d in range(3)]
            pl.when((ids[0] == 0) & (ids[1] == 0) & (ids[2] == 0))(lambda: comm.start(ci_refs, co_refs, cs_refs))
            if comm.middle is not None:
                pl.when((ids[0] == (5 * grid[0]) // 8) & (ids[1] == 0) & (ids[2] == 0))(
                    lambda: comm.middle(ci_refs, co_refs, cs_refs))

        def finish(acc):
            outs = epi(acc, *[e[...] for e in e_refs]) if epi is not None else (acc,)
            for o_ref, val in zip(o_refs, outs):
                o_ref[...] = val.astype(o_ref.dtype)

        part = lax.dot_general(a_ref[...], b_ref[...], dims, preferred_element_type=F32)
        if nk == 1:
            finish(part)
        else:
            kk = pl.program_id(2)

            @pl.when(kk == 0)
            def _():
                acc_ref[...] = part

            @pl.when(kk > 0)
            def _():
                acc_ref[...] += part

            @pl.when(kk == nk - 1)
            def _():
                finish(acc_ref[...])

        if comm is not None:
            pl.when((ids[0] == grid[0] - 1) & (ids[1] == grid[1] - 1) & (ids[2] == grid[2] - 1))(
                lambda: comm.finish(ci_refs, co_refs, cs_refs))

    extra_in = comm.ins if comm is not None else []
    extra_out = comm.outs if comm is not None else []
    extra_sems = comm.sems if comm is not None else []
    return pl.pallas_call(
        body, name=name, grid=grid,
        in_specs=[a_spec, b_spec] + [o_spec] * n_epi + [ANY] * n_ci,
        out_specs=[o_spec] * n_out + [ANY] * n_co,
        out_shape=[jax.ShapeDtypeStruct((m, n), dt) for dt in out_dtypes] + extra_out,
        scratch_shapes=[pltpu.VMEM((tm, tn) if nk > 1 else (8, 128), F32)] + extra_sems,
        compiler_params=pltpu.CompilerParams(
            dimension_semantics=("arbitrary",) * 3 if comm is not None else ("parallel", "parallel", "arbitrary")),
    )(a, b, *epi_ins, *extra_in)


def _row_tile(s):
    return _pick(s, (512, 256, 128))


def _prenorm(x, sc, sh):
    s = x.shape[0]
    tm = _row_tile(s)

    def body(x_ref, sc_ref, sh_ref, h_ref):
        h_ref[...] = _modnorm(x_ref[...], sc_ref[...], sh_ref[...]).astype(BF16)

    row = pl.BlockSpec((tm, D), lambda i: (i, 0))
    return pl.pallas_call(body, name="prenorm", grid=(s // tm,), in_specs=[row, _full((1, D)), _full((1, D))],
                          out_specs=row, out_shape=jax.ShapeDtypeStruct((s, D), BF16))(x, sc, sh)


def _prenorm_bwd(x, sc, sh, dh, dx_res):
    s = x.shape[0]
    tm = _row_tile(s)

    def body(x_ref, sc_ref, sh_ref, dh_ref, dr_ref, dx_ref, dsc_ref, dsh_ref):
        _, vjp = jax.vjp(_modnorm, x_ref[...], sc_ref[...], sh_ref[...])
        dx, dsc, dsh = vjp(dh_ref[...].astype(F32))
        dx_ref[...] = dr_ref[...] + dx

        @pl.when(pl.program_id(0) == 0)
        def _():
            dsc_ref[...] = jnp.zeros_like(dsc_ref)
            dsh_ref[...] = jnp.zeros_like(dsh_ref)

        dsc_ref[...] += dsc
        dsh_ref[...] += dsh

    row = pl.BlockSpec((tm, D), lambda i: (i, 0))
    vec = _full((1, D))
    return pl.pallas_call(
        body, name="prenorm_bwd", grid=(s // tm,), in_specs=[row, vec, vec, row, row], out_specs=[row, vec, vec],
        out_shape=[jax.ShapeDtypeStruct((s, D), F32), jax.ShapeDtypeStruct((1, D), F32), jax.ShapeDtypeStruct((1, D), F32)],
    )(x, sc, sh, dh, dx_res)


def _resid_norm(x, mo, g1, sc, sh):
    s = x.shape[0]
    tm = _row_tile(s)

    def body(x_ref, mo_ref, g_ref, sc_ref, sh_ref, x1_ref, h_ref):
        x1 = x_ref[...] + g_ref[...] * mo_ref[...].astype(F32)
        x1_ref[...] = x1
        h_ref[...] = _modnorm(x1, sc_ref[...], sh_ref[...]).astype(BF16)

    row = pl.BlockSpec((tm, D), lambda i: (i, 0))
    vec = _full((1, D))
    return pl.pallas_call(
        body, name="resid_norm", grid=(s // tm,), in_specs=[row, row, vec, vec, vec], out_specs=[row, row],
        out_shape=[jax.ShapeDtypeStruct((s, D), F32), jax.ShapeDtypeStruct((s, D), BF16)],
    )(x, mo, g1, sc, sh)


def _resid_norm_bwd(x1, mo, g1, sc, sh, dh, dx2):
    s = x1.shape[0]
    tm = _row_tile(s)

    def body(x1_ref, mo_ref, g_ref, sc_ref, sh_ref, dh_ref, dx2_ref, dx1_ref, dmo_ref, dg_ref, dsc_ref, dsh_ref):
        _, vjp = jax.vjp(_modnorm, x1_ref[...], sc_ref[...], sh_ref[...])
        dx, dsc, dsh = vjp(dh_ref[...].astype(F32))
        dx1 = dx2_ref[...] + dx
        dx1_ref[...] = dx1
        dmo_ref[...] = (dx1 * g_ref[...]).astype(BF16)

        @pl.when(pl.program_id(0) == 0)
        def _():
            dg_ref[...] = jnp.zeros_like(dg_ref)
            dsc_ref[...] = jnp.zeros_like(dsc_ref)
            dsh_ref[...] = jnp.zeros_like(dsh_ref)

        dg_ref[...] += jnp.sum(dx1 * mo_ref[...].astype(F32), axis=0, keepdims=True)
        dsc_ref[...] += dsc
        dsh_ref[...] += dsh

    row = pl.BlockSpec((tm, D), lambda i: (i, 0))
    vec = _full((1, D))
    vshape = jax.ShapeDtypeStruct((1, D), F32)
    return pl.pallas_call(
        body, name="resid_norm_bwd", grid=(s // tm,), in_specs=[row, row, vec, vec, vec, row, row],
        out_specs=[row, row, vec, vec, vec],
        out_shape=[jax.ShapeDtypeStruct((s, D), F32), jax.ShapeDtypeStruct((s, D), BF16), vshape, vshape, vshape],
    )(x1, mo, g1, sc, sh, dh, dx2)


def _loss_head(x1, fo, g2, fw, t):
    s = x1.shape[0]
    tm = _row_tile(s)

    def body(x1_ref, fo_ref, g_ref, fw_ref, t_ref, loss_ref, dx_ref, dfo_ref, dg_ref, dfw_ref):
        loss, (dx1, dfo, dg, dfw) = jax.value_and_grad(_loss_tile, argnums=(0, 1, 2, 3))(
            x1_ref[...], fo_ref[...].astype(F32), g_ref[...], fw_ref[...], t_ref[...])
        dx_ref[...] = dx1
        dfo_ref[...] = dfo.astype(BF16)

        @pl.when(pl.program_id(0) == 0)
        def _():
            loss_ref[...] = jnp.zeros_like(loss_ref)
            dg_ref[...] = jnp.zeros_like(dg_ref)
            dfw_ref[...] = jnp.zeros_like(dfw_ref)

        loss_ref[...] += jnp.full(loss_ref.shape, loss, F32)
        dg_ref[...] += dg
        dfw_ref[...] += dfw

    row = pl.BlockSpec((tm, D), lambda i: (i, 0))
    vec = _full((1, D))
    vshape = jax.ShapeDtypeStruct((1, D), F32)
    return pl.pallas_call(
        body, name="loss_head", grid=(s // tm,), in_specs=[row, row, vec, vec, row],
        out_specs=[_full((8, 128)), row, row, vec, vec],
        out_shape=[jax.ShapeDtypeStruct((8, 128), F32), jax.ShapeDtypeStruct((s, D), F32),
                   jax.ShapeDtypeStruct((s, D), BF16), vshape, vshape],
    )(x1, fo, g2, fw, t)


def _sgu_fwd(proj, norm_w, ws_b, bs_t):
    s = proj.shape[0]
    tm = _pick(s, (256, 128))

    def body(u_ref, v_ref, w_ref, ws_ref, bs_ref, y_ref):
        ug, vn = _sgu_pre(u_ref[...].astype(F32), v_ref[...].astype(F32), w_ref[...])
        vnb = vn.astype(BF16)
        for c in range(tm // Q):
            r = slice(c * Q, (c + 1) * Q)
            for g in range(GM_G):
                cs = slice(g * 128, (g + 1) * 128)
                sv = jnp.dot(ws_ref[g], vnb[r, cs], preferred_element_type=F32) + bs_ref[:, g:g + 1]
                y_ref[r, cs] = (ug[r, cs] * sv).astype(BF16)

    return pl.pallas_call(
        body, name="sgu_fwd", grid=(s // tm,),
        in_specs=[pl.BlockSpec((tm, D), lambda i: (i, 0)), pl.BlockSpec((tm, D), lambda i: (i, 1)),
                  _full((1, D)), _full((GM_G, Q, Q)), _full((Q, GM_G))],
        out_specs=pl.BlockSpec((tm, D), lambda i: (i, 0)),
        out_shape=jax.ShapeDtypeStruct((s, D), BF16),
    )(proj, proj, norm_w, ws_b, bs_t)


def _sgu_bwd(proj, norm_w, ws_b, bs_t, dy, dproj):
    s = proj.shape[0]
    tm = _pick(s, (256, 128))

    def body(u_ref, v_ref, w_ref, ws_ref, bs_ref, dy_ref, _, duv_ref, dw_ref, dws_ref, dbs_ref, dug_scr, dvn_scr):
        @pl.when(pl.program_id(0) == 0)
        def _():
            dw_ref[...] = jnp.zeros_like(dw_ref)
            dws_ref[...] = jnp.zeros_like(dws_ref)
            dbs_ref[...] = jnp.zeros_like(dbs_ref)

        (ug, vn), vjp = jax.vjp(_sgu_pre, u_ref[...].astype(F32), v_ref[...].astype(F32), w_ref[...])
        vnb = vn.astype(BF16)
        dy = dy_ref[...].astype(F32)
        causal = _tril(Q).astype(F32)
        for c in range(tm // Q):
            r = slice(c * Q, (c + 1) * Q)
            for g in range(GM_G):
                cs = slice(g * 128, (g + 1) * 128)
                blk = vnb[r, cs]
                sv = jnp.dot(ws_ref[g], blk, preferred_element_type=F32) + bs_ref[:, g:g + 1]
                dug_scr[r, cs] = dy[r, cs] * sv
                dsv = dy[r, cs] * ug[r, cs]
                dsvb = dsv.astype(BF16)
                dws_ref[g] += causal * lax.dot_general(dsvb, blk, (((1,), (1,)), ((), ())), preferred_element_type=F32)
                dbs_ref[:, g:g + 1] += jnp.sum(dsv, axis=1, keepdims=True)
                dvn_scr[r, cs] = lax.dot_general(ws_ref[g], dsvb, (((0,), (0,)), ((), ())), preferred_element_type=F32)
        du, dv, dw = vjp((dug_scr[...], dvn_scr[...]))
        duv_ref[:, :D] = du.astype(BF16)
        duv_ref[:, D:] = dv.astype(BF16)
        dw_ref[...] += dw

    return pl.pallas_call(
        body, name="sgu_bwd", grid=(s // tm,),
        in_specs=[pl.BlockSpec((tm, D), lambda i: (i, 0)), pl.BlockSpec((tm, D), lambda i: (i, 1)),
                  _full((1, D)), _full((GM_G, Q, Q)), _full((Q, GM_G)), pl.BlockSpec((tm, D), lambda i: (i, 0)), ANY],
        out_specs=[pl.BlockSpec((tm, 2 * D), lambda i: (i, 0)), _full((1, D)), _full((GM_G, Q, Q)), _full((Q, GM_G))],
        out_shape=[jax.ShapeDtypeStruct(dproj.shape, BF16), jax.ShapeDtypeStruct((1, D), F32),
                   jax.ShapeDtypeStruct((GM_G, Q, Q), F32), jax.ShapeDtypeStruct((Q, GM_G), F32)],
        scratch_shapes=[pltpu.VMEM((tm, D), F32), pltpu.VMEM((tm, D), F32)],
        input_output_aliases={6: 0},
    )(proj, proj, norm_w, ws_b, bs_t, dy, dproj)


def _gatenorm_fwd(y_ssd, proj, norm_w):
    s = y_ssd.shape[0]
    tm = _pick(s, (256, 128))
    gw = DI // NGRP

    def body(y_ref, z_ref, w_ref, o_ref):
        for g in range(NGRP):
            cs = slice(g * gw, (g + 1) * gw)
            o_ref[:, cs] = _gatenorm(y_ref[:, cs], z_ref[:, cs].astype(F32), w_ref[:, cs]).astype(BF16)

    return pl.pallas_call(
        body, name="gatenorm_fwd", grid=(s // tm,),
        in_specs=[pl.BlockSpec((tm, DI), lambda i: (i, 0)), pl.BlockSpec((tm, DI), lambda i: (i, OFF_Z // DI)), _full((1, DI))],
        out_specs=pl.BlockSpec((tm, DI), lambda i: (i, 0)),
        out_shape=jax.ShapeDtypeStruct((s, DI), BF16),
    )(y_ssd, proj, norm_w)


def _gatenorm_bwd(y_ssd, proj, norm_w, dyb, dproj):
    s = y_ssd.shape[0]
    tm = _pick(s, (256, 128))
    gw = DI // NGRP

    def body(y_ref, z_ref, w_ref, d_ref, _, dy_ref, dyt_ref, dz_ref, dw_ref):
        @pl.when(pl.program_id(0) == 0)
        def _():
            dw_ref[...] = jnp.zeros_like(dw_ref)

        for g in range(NGRP):
            cs = slice(g * gw, (g + 1) * gw)
            _, vjp = jax.vjp(_gatenorm, y_ref[:, cs], z_ref[:, cs].astype(F32), w_ref[:, cs])
            dy, dz, dw = vjp(d_ref[:, cs].astype(F32))
            dy_ref[:, cs] = dy.astype(BF16)
            dyt_ref[cs, :] = dy.T.astype(BF16)
            dz_ref[:, cs] = dz.astype(BF16)
            dw_ref[:, cs] += dw

    blk = pl.BlockSpec((tm, DI), lambda i: (i, 0))
    zblk = pl.BlockSpec((tm, DI), lambda i: (i, OFF_Z // DI))
    return pl.pallas_call(
        body, name="gatenorm_bwd", grid=(s // tm,),
        in_specs=[blk, zblk, _full((1, DI)), blk, ANY],
        out_specs=[blk, pl.BlockSpec((DI, tm), lambda i: (0, i)), zblk, _full((1, DI))],
        out_shape=[jax.ShapeDtypeStruct((s, DI), BF16), jax.ShapeDtypeStruct((DI, s), BF16),
                   jax.ShapeDtypeStruct(dproj.shape, BF16), jax.ShapeDtypeStruct((1, DI), F32)],
        input_output_aliases={4: 2},
    )(y_ssd, proj, norm_w, dyb, dproj)


def _mix_fwd(proj, ba, bb):
    s = proj.shape[0]
    tm = _row_tile(s)
    gb0 = OFF_GA // D

    def body(ga_ref, gb_ref, ba_ref, bb_ref, o_ref):
        o_ref[...] = _mix(ga_ref[...].astype(F32), gb_ref[...].astype(F32), ba_ref[...].astype(F32),
                          bb_ref[...].astype(F32)).astype(BF16)

    row = pl.BlockSpec((tm, D), lambda i: (i, 0))
    return pl.pallas_call(
        body, name="mix_fwd", grid=(s // tm,),
        in_specs=[pl.BlockSpec((tm, D), lambda i: (i, gb0)), pl.BlockSpec((tm, D), lambda i: (i, gb0 + 1)), row, row],
        out_specs=row, out_shape=jax.ShapeDtypeStruct((s, D), BF16),
    )(proj, proj, ba, bb)


def _mix_bwd(proj, ba, bb, dmixed):
    s = proj.shape[0]
    tm = _row_tile(s)
    gb0 = OFF_GA // D

    def body(ga_ref, gb_ref, ba_ref, bb_ref, d_ref, dg_ref, dba_ref, dbb_ref):
        _, vjp = jax.vjp(_mix, ga_ref[...].astype(F32), gb_ref[...].astype(F32), ba_ref[...].astype(F32),
                         bb_ref[...].astype(F32))
        dga, dgb, dba, dbb = vjp(d_ref[...].astype(F32))
        dg_ref[:, :D] = dga.astype(BF16)
        dg_ref[:, D:] = dgb.astype(BF16)
        dba_ref[...] = dba.astype(BF16)
        dbb_ref[...] = dbb.astype(BF16)

    row = pl.BlockSpec((tm, D), lambda i: (i, 0))
    return pl.pallas_call(
        body, name="mix_bwd", grid=(s // tm,),
        in_specs=[pl.BlockSpec((tm, D), lambda i: (i, gb0)), pl.BlockSpec((tm, D), lambda i: (i, gb0 + 1)), row, row, row],
        out_specs=[pl.BlockSpec((tm, 2 * D), lambda i: (i, OFF_GA // (2 * D))), row, row],
        out_shape=[jax.ShapeDtypeStruct((s, W_IN_R), BF16), jax.ShapeDtypeStruct((s, D), BF16), jax.ShapeDtypeStruct((s, D), BF16)],
    )(proj, proj, ba, bb, dmixed)


CONV_TC = 1024


def _conv_fwd(proj, conv_w, conv_b):
    s = proj.shape[0]
    tm = _row_tile(s)
    cb0 = OFF_XBC // CONV_TC

    def body(x_ref, halo_ref, w_ref, b_ref, o_ref):
        halo = jnp.where(pl.program_id(0) > 0, halo_ref[...].astype(F32)[8:, :], 0.0)
        ext = jnp.concatenate([halo, x_ref[...].astype(F32)], axis=0)
        acc = jnp.broadcast_to(b_ref[...], (tm, CONV_TC))
        for k in range(4):
            shifted = ext if k == 3 else pltpu.roll(ext, 3 - k, 0)
            acc = acc + w_ref[k:k + 1, :] * shifted[8:, :]
        o_ref[...] = jax.nn.silu(acc)

    return pl.pallas_call(
        body, name="conv_fwd", grid=(s // tm, CONV // CONV_TC),
        in_specs=[pl.BlockSpec((tm, CONV_TC), lambda i, j: (i, cb0 + j)),
                  pl.BlockSpec((16, CONV_TC), lambda i, j: (jnp.maximum(i * (tm // 16) - 1, 0), cb0 + j)),
                  pl.BlockSpec((4, CONV_TC), lambda i, j: (0, j)), pl.BlockSpec((1, CONV_TC), lambda i, j: (0, j))],
        out_specs=pl.BlockSpec((tm, CONV_TC), lambda i, j: (i, j)),
        out_shape=jax.ShapeDtypeStruct((s, CONV), F32),
    )(proj, proj, conv_w, conv_b)


def _conv_bwd(proj, conv_w, conv_b, dact, dproj, col0, name):
    s, width = dact.shape
    tm = _row_tile(s)
    nt = s // tm
    c0 = col0 // CONV_TC
    cb0 = OFF_XBC // CONV_TC + c0

    def body(x_ref, prev_ref, next_ref, d_ref, dnext_ref, w_ref, b_ref, _, dx_ref, dw_ref, db_ref):
        i = pl.program_id(1)
        prev = jnp.where(i > 0, prev_ref[...].astype(F32)[8:, :], 0.0)
        ext = jnp.concatenate([prev, x_ref[...].astype(F32), next_ref[...].astype(F32)[:8, :]], axis=0)
        dext = jnp.concatenate([d_ref[...], jnp.where(i < nt - 1, dnext_ref[...], 0.0)], axis=0)
        pre = jnp.broadcast_to(b_ref[...], (tm + 8, CONV_TC))
        taps = []
        for k in range(4):
            shifted = (ext if k == 3 else pltpu.roll(ext, 3 - k, 0))[8:, :]
            taps.append(shifted)
            pre = pre + w_ref[k:k + 1, :] * shifted
        sig = jax.nn.sigmoid(pre)
        dpre = dext * (sig * (1.0 + pre * (1.0 - sig)))
        dx = jnp.zeros((tm, CONV_TC), F32)
        for k in range(4):
            shifted = dpre if k == 3 else pltpu.roll(dpre, tm + 8 - (3 - k), 0)
            dx = dx + w_ref[k:k + 1, :] * shifted[:tm, :]
        dx_ref[...] = dx.astype(BF16)

        @pl.when(i == 0)
        def _():
            dw_ref[...] = jnp.zeros_like(dw_ref)
            db_ref[...] = jnp.zeros_like(db_ref)

        dtile = dpre[:tm, :]
        for k in range(4):
            dw_ref[k:k + 1, :] += jnp.sum(dtile * taps[k][:tm, :], axis=0, keepdims=True)
        db_ref[...] += jnp.sum(dtile, axis=0, keepdims=True)

    r8, r16 = tm // 8, tm // 16
    return pl.pallas_call(
        body, name=name, grid=(width // CONV_TC, nt),
        in_specs=[pl.BlockSpec((tm, CONV_TC), lambda j, i: (i, cb0 + j)),
                  pl.BlockSpec((16, CONV_TC), lambda j, i: (jnp.maximum(i * r16 - 1, 0), cb0 + j)),
                  pl.BlockSpec((16, CONV_TC), lambda j, i: (jnp.minimum((i + 1) * r16, nt * r16 - 1), cb0 + j)),
                  pl.BlockSpec((tm, CONV_TC), lambda j, i: (i, j)),
                  pl.BlockSpec((8, CONV_TC), lambda j, i: (jnp.minimum((i + 1) * r8, nt * r8 - 1), j)),
                  pl.BlockSpec((4, CONV_TC), lambda j, i: (0, c0 + j)), pl.BlockSpec((1, CONV_TC), lambda j, i: (0, c0 + j)), ANY],
        out_specs=[pl.BlockSpec((tm, CONV_TC), lambda j, i: (i, cb0 + j)),
                   pl.BlockSpec((4, CONV_TC), lambda j, i: (0, j)), pl.BlockSpec((1, CONV_TC), lambda j, i: (0, j))],
        out_shape=[jax.ShapeDtypeStruct(dproj.shape, BF16), jax.ShapeDtypeStruct((4, width), F32), jax.ShapeDtypeStruct((1, width), F32)],
        input_output_aliases={7: 0},
    )(proj, proj, proj, dact, dact, conv_w, conv_b, dproj)


def _dt_fwd(dt_raw, dtb, alog):
    s = dt_raw.shape[0]
    nc = s // Q

    def body(r_ref, b_ref, a_ref, dt_ref, cum_ref, cumt_ref):
        dt, cum, cum_t = _dt_prep(r_ref[...], b_ref[...], a_ref[...])
        dt_ref[...] = dt
        cum_ref[...] = cum
        cumt_ref[...] = cum_t

    blk = pl.BlockSpec((Q, 128), lambda n: (n, 0))
    return pl.pallas_call(
        body, name="dt_fwd", grid=(nc,),
        in_specs=[blk, _full((1, 128)), _full((1, 128))],
        out_specs=[blk, blk, pl.BlockSpec((None, 128, Q), lambda n: (n, 0, 0))],
        out_shape=[jax.ShapeDtypeStruct((s, 128), F32), jax.ShapeDtypeStruct((s, 128), F32), jax.ShapeDtypeStruct((nc, 128, Q), F32)],
    )(dt_raw, dtb, alog)


def _dt_bwd(dt_raw, dtb, alog, ddt, dcum, dcumt, dproj):
    s = dt_raw.shape[0]
    nc = s // Q

    def body(r_ref, b_ref, a_ref, ddt_ref, dcum_ref, dcumt_ref, _, dr_ref, db_ref, da_ref):
        _, vjp = jax.vjp(_dt_prep, r_ref[...], b_ref[...], a_ref[...])
        dr, db, da = vjp((ddt_ref[...], dcum_ref[...], dcumt_ref[...]))
        dr_ref[...] = dr.astype(BF16)

        @pl.when(pl.program_id(0) == 0)
        def _():
            db_ref[...] = jnp.zeros_like(db_ref)
            da_ref[...] = jnp.zeros_like(da_ref)

        db_ref[...] += db
        da_ref[...] += da

    blk = pl.BlockSpec((Q, 128), lambda n: (n, 0))
    pblk = pl.BlockSpec((Q, 128), lambda n: (n, OFF_DT // 128))
    return pl.pallas_call(
        body, name="dt_bwd", grid=(nc,),
        in_specs=[blk, _full((1, 128)), _full((1, 128)), blk, blk, pl.BlockSpec((None, 128, Q), lambda n: (n, 0, 0)), ANY],
        out_specs=[pblk, _full((1, 128)), _full((1, 128))],
        out_shape=[jax.ShapeDtypeStruct(dproj.shape, BF16), jax.ShapeDtypeStruct((1, 128), F32), jax.ShapeDtypeStruct((1, 128), F32)],
        input_output_aliases={6: 0},
    )(dt_raw, dtb, alog, ddt, dcum, dcumt, dproj)


def _ssd_specs(chunk_of):
    gs = SSD_GPS
    xs = pl.BlockSpec((Q, 256 * gs), lambda n, g: (chunk_of(n), g))
    bm = pl.BlockSpec((Q, 128 * gs), lambda n, g: (chunk_of(n), DI // (128 * gs) + g))
    cm = pl.BlockSpec((Q, 128 * gs), lambda n, g: (chunk_of(n), (DI + D) // (128 * gs) + g))
    per_chunk = pl.BlockSpec((Q, 128), lambda n, g: (chunk_of(n), 0))
    cum_t = pl.BlockSpec((None, 128, Q), lambda n, g: (chunk_of(n), 0, 0))
    state = pl.BlockSpec((None, 256 * gs, 128), lambda n, g: (chunk_of(n), g, 0))
    vec = pl.BlockSpec((1, 128), lambda n, g: (0, 0))
    return xs, bm, cm, per_chunk, cum_t, state, vec


SSD_GPS = 8


def _aligned(v, m):
    return v if isinstance(v, int) else pl.multiple_of(v, m)


def _ssd_fwd(xbc, dt, cum, cum_t, dsk):
    s = xbc.shape[0]
    nc = s // Q
    xs, bm, cm, per_chunk, cumt_spec, state_spec, vec = _ssd_specs(lambda n: n)
    gs = SSD_GPS

    def body(x_ref, b_ref, c_ref, dt_ref, cum_ref, cumt_ref, dsk_ref, y_ref, st_ref, carry):
        n, gstep = pl.program_id(0), (0 if gs == NGRP else pl.program_id(1))
        rows = pl.ds(_aligned(gstep * (256 * gs), 256 * gs), 256 * gs)

        @pl.when(n == 0)
        def _():
            carry[rows, :] = jnp.zeros((256 * gs, 128), F32)

        st_ref[...] = carry[rows, :]
        for k in range(gs):
            xo, so, bo = 256 * k, 256 * k, 128 * k
            y0, y1, n0, n1 = _ssd_group(
                x_ref[:, xo:xo + 128], x_ref[:, xo + 128:xo + 256], st_ref[so:so + 128, :], st_ref[so + 128:so + 256, :],
                b_ref[:, bo:bo + 128], c_ref[:, bo:bo + 128], dt_ref[...], cum_ref[...], cumt_ref[...], dsk_ref[...],
                gstep * gs + k)
            y_ref[:, xo:xo + 128] = y0
            y_ref[:, xo + 128:xo + 256] = y1
            base = gstep * (256 * gs) + so
            carry[pl.ds(_aligned(base, 128), 128), :] = n0
            carry[pl.ds(_aligned(base + 128, 128), 128), :] = n1

    return pl.pallas_call(
        body, name="ssd_fwd", grid=(nc, NGRP // gs),
        in_specs=[xs, bm, cm, per_chunk, per_chunk, cumt_spec, vec],
        out_specs=[xs, state_spec],
        out_shape=[jax.ShapeDtypeStruct((s, DI), F32), jax.ShapeDtypeStruct((nc, DI, 128), F32)],
        scratch_shapes=[pltpu.VMEM((DI, 128), F32)],
    )(xbc, xbc, xbc, dt, cum, cum_t, dsk)


def _ssd_bwd(xbc, dt, cum, cum_t, dsk, states, dy, dy_t, comm=None):
    s = xbc.shape[0]
    nc = s // Q
    xs, bm, cm, per_chunk, cumt_spec, state_spec, vec = _ssd_specs(lambda n: nc - 1 - n)
    gs = SSD_GPS
    n_ci, n_co = (len(comm.ins), len(comm.outs)) if comm is not None else (0, 0)
    steps = (nc, NGRP // gs)

    def body(*refs):
        x_ref, b_ref, c_ref, dt_ref, cum_ref, cumt_ref, dsk_ref, st_ref, dy_ref, dyt_ref = refs[:10]
        ci_refs = refs[10:10 + n_ci]
        dx_ref, db_ref, dc_ref, ddt_ref, dcum_ref, dcumt_ref, ddsk_ref = refs[10 + n_ci:17 + n_ci]
        co_refs = refs[17 + n_ci:17 + n_ci + n_co]
        carry = refs[17 + n_ci + n_co]
        cs_refs = refs[18 + n_ci + n_co:]
        n, gstep = pl.program_id(0), (0 if gs == NGRP else pl.program_id(1))
        rows = pl.ds(_aligned(gstep * (256 * gs), 256 * gs), 256 * gs)
        if comm is not None:
            pl.when((pl.program_id(0) == 0) & (pl.program_id(1) == 0))(lambda: comm.start(ci_refs, co_refs, cs_refs))

        @pl.when(n == 0)
        def _():
            carry[rows, :] = jnp.zeros((256 * gs, 128), F32)

        def zero_skip_sum():
            ddsk_ref[...] = jnp.zeros_like(ddsk_ref)

        def zero_chunk_sums():
            ddt_ref[...] = jnp.zeros_like(ddt_ref)
            dcum_ref[...] = jnp.zeros_like(dcum_ref)
            dcumt_ref[...] = jnp.zeros_like(dcumt_ref)

        if isinstance(gstep, int):
            pl.when(n == 0)(zero_skip_sum)
            zero_chunk_sums()
        else:
            pl.when((n == 0) & (gstep == 0))(zero_skip_sum)
            pl.when(gstep == 0)(zero_chunk_sums)

        for k in range(gs):
            xo, so, bo = 256 * k, 256 * k, 128 * k
            base = gstep * (256 * gs) + so
            lo = pl.ds(_aligned(base, 128), 128)
            hi = pl.ds(_aligned(base + 128, 128), 128)
            dx0, dx1, ds0, ds1, dbm, dcm, ddt, dcum, dcumt, ddsk = _ssd_group_bwd(
                x_ref[:, xo:xo + 128], x_ref[:, xo + 128:xo + 256], st_ref[so:so + 128, :], st_ref[so + 128:so + 256, :],
                b_ref[:, bo:bo + 128], c_ref[:, bo:bo + 128], dt_ref[...], cum_ref[...], cumt_ref[...], dsk_ref[...],
                gstep * gs + k, dy_ref[:, xo:xo + 128].astype(F32), dy_ref[:, xo + 128:xo + 256].astype(F32),
                dyt_ref[xo:xo + 128, :].astype(F32), dyt_ref[xo + 128:xo + 256, :].astype(F32), carry[lo, :], carry[hi, :])
            dx_ref[:, xo:xo + 128] = dx0
            dx_ref[:, xo + 128:xo + 256] = dx1
            db_ref[:, bo:bo + 128] = dbm
            dc_ref[:, bo:bo + 128] = dcm
            ddt_ref[...] += ddt
            dcum_ref[...] += dcum
            dcumt_ref[...] += dcumt
            ddsk_ref[...] += ddsk
            carry[lo, :] = ds0
            carry[hi, :] = ds1

        if comm is not None:
            pl.when((pl.program_id(0) == steps[0] - 1) & (pl.program_id(1) == steps[1] - 1))(
                lambda: comm.finish(ci_refs, co_refs, cs_refs))

    grp_blk = pl.BlockSpec((Q, 128 * gs), lambda n, g: (nc - 1 - n, g))
    extra_in = comm.ins if comm is not None else []
    extra_out = comm.outs if comm is not None else []
    extra_sems = comm.sems if comm is not None else []
    return pl.pallas_call(
        body, name="ssd_bwd", grid=steps,
        in_specs=[xs, bm, cm, per_chunk, per_chunk, cumt_spec, vec, state_spec, xs,
                  pl.BlockSpec((256 * gs, Q), lambda n, g: (g, nc - 1 - n))] + [ANY] * n_ci,
        out_specs=[xs, grp_blk, grp_blk, per_chunk, per_chunk, cumt_spec, vec] + [ANY] * n_co,
        out_shape=[jax.ShapeDtypeStruct((s, DI), F32), jax.ShapeDtypeStruct((s, D), F32), jax.ShapeDtypeStruct((s, D), F32),
                   jax.ShapeDtypeStruct((s, 128), F32), jax.ShapeDtypeStruct((s, 128), F32),
                   jax.ShapeDtypeStruct((nc, 128, Q), F32), jax.ShapeDtypeStruct((1, 128), F32)] + extra_out,
        scratch_shapes=[pltpu.VMEM((DI, 128), F32)] + extra_sems,
    )(xbc, xbc, xbc, dt, cum, cum_t, dsk, states, dy, dy_t, *extra_in)


def _adam_math(w, g, m, v):
    m2 = ADAM_B1 * m + (1.0 - ADAM_B1) * g
    v2 = ADAM_B2 * v + (1.0 - ADAM_B2) * jnp.square(g)
    m_hat = m2 / (1.0 - ADAM_B1 ** ADAM_STEP)
    v_hat = v2 / (1.0 - ADAM_B2 ** ADAM_STEP)
    delta = -ADAM_LR * (m_hat / (jnp.sqrt(v_hat) + ADAM_EPS) + ADAM_WD * w)
    return delta, m2, v2


def _adam(w, g, m, v, name):
    r, c = w.shape
    tr = r if r * c * 4 <= (1 << 20) else _pick(r, (128, 64, 32, 16, 8))

    def body(w_ref, g_ref, m_ref, v_ref, d_ref, m2_ref, v2_ref):
        d, m2, v2 = _adam_math(w_ref[...], g_ref[...], m_ref[...], v_ref[...])
        d_ref[...] = d
        m2_ref[...] = m2
        v2_ref[...] = v2

    blk = pl.BlockSpec((tr, c), lambda i: (i, 0))
    shp = jax.ShapeDtypeStruct((r, c), F32)
    return pl.pallas_call(body, name=name, grid=(r // tr,), in_specs=[blk] * 4, out_specs=[blk] * 3,
                          out_shape=[shp] * 3)(w, g, m, v)


def _sum_leading(xs, name, out_dtype=F32, tr=256):
    if isinstance(xs, tuple):
        a, b = xs
        n, r, c = a.shape
        tr = _pick(r, (tr, 128, 64, 32, 16, 8))

        def body2(a_ref, b_ref, o_ref):
            o_ref[...] = (a_ref[...].astype(F32) + b_ref[...].astype(F32)).astype(out_dtype)

        blk = pl.BlockSpec((None, tr, c), lambda s, i: (s, i, 0))
        return pl.pallas_call(body2, name=name, grid=(n, r // tr), in_specs=[blk, blk], out_specs=blk,
                              out_shape=jax.ShapeDtypeStruct((n, r, c), out_dtype))(a, b)
    n, r, c = xs.shape
    tr = r if n * r * c * 4 <= (8 << 20) else _pick(r, (tr, 128, 64, 32, 16, 8))

    def body(x_ref, o_ref):
        acc = x_ref[0].astype(F32)
        for s in range(1, n):
            acc = acc + x_ref[s].astype(F32)
        o_ref[...] = acc.astype(out_dtype)

    return pl.pallas_call(body, name=name, grid=(r // tr,), in_specs=[pl.BlockSpec((n, tr, c), lambda i: (0, i, 0))],
                          out_specs=pl.BlockSpec((tr, c), lambda i: (i, 0)),
                          out_shape=jax.ShapeDtypeStruct((r, c), out_dtype))(xs)


def _mod_fwd(c8, w_mod, b_sl):
    def body(c_ref, w_ref, b_ref, o_ref):
        o_ref[...] = jnp.dot(jax.nn.silu(c_ref[...]), w_ref[...], precision=HIGHEST, preferred_element_type=F32) + b_ref[...]

    return pl.pallas_call(body, name="mod_fwd", in_specs=[VMEM, VMEM, VMEM], out_specs=VMEM,
                          out_shape=jax.ShapeDtypeStruct((N_DEV, w_mod.shape[1]), F32))(c8, w_mod, b_sl)


def _mod_wgrad(c8, dmod):
    def body(c_ref, d_ref, o_ref):
        o_ref[...] = lax.dot_general(jax.nn.silu(c_ref[...]), d_ref[...], (((0,), (0,)), ((), ())),
                                     precision=HIGHEST, preferred_element_type=F32)

    return pl.pallas_call(body, name="mod_wgrad", in_specs=[VMEM, VMEM], out_specs=VMEM,
                          out_shape=jax.ShapeDtypeStruct((D, dmod.shape[1]), F32))(c8, dmod)


def _place():
    x, y, c = lax.axis_index("x"), lax.axis_index("y"), lax.axis_index("c")
    chips = [(1 - x, y), (x, 1 - y), (1 - x, 1 - y)]
    return x, y, c, chips


def _all_gather_small(v, name):
    r, w = v.shape

    def body(v_ref, o_ref, send_sems, recv_sems, local_sem):
        x, y, c, _ = _place()
        me = 4 * x + 2 * y + c
        own = pltpu.make_async_copy(v_ref, o_ref.at[me], local_sem)
        own.start()
        sends = []
        for k in range(1, N_DEV):
            tx = 1 - x if k & 4 else x
            ty = 1 - y if k & 2 else y
            tc = 1 - c if k & 1 else c
            peer = 4 * tx + 2 * ty + tc
            cp = pltpu.make_async_remote_copy(src_ref=v_ref, dst_ref=o_ref.at[me], send_sem=send_sems.at[k - 1],
                                              recv_sem=recv_sems.at[k - 1], device_id=(tx, ty, tc), device_id_type=MESH)
            cp.start()
            sends.append((cp, peer, (tx, ty, tc)))
        for k, (cp, peer, dev) in enumerate(sends):
            pltpu.make_async_remote_copy(src_ref=v_ref, dst_ref=o_ref.at[peer], send_sem=send_sems.at[k],
                                         recv_sem=recv_sems.at[k], device_id=dev, device_id_type=MESH).wait_recv()
        for cp, _, _ in sends:
            cp.wait_send()
        own.wait()

    return pl.pallas_call(
        body, name=name, in_specs=[VMEM], out_specs=VMEM, out_shape=jax.ShapeDtypeStruct((N_DEV, r, w), v.dtype),
        scratch_shapes=[pltpu.SemaphoreType.DMA((N_DEV - 1,)), pltpu.SemaphoreType.DMA((N_DEV - 1,)), pltpu.SemaphoreType.DMA],
    )(v)


def _all_gather_weights(shards):
    n = len(shards)
    nsem = n * 3 * AG_CHUNKS

    def body(*refs):
        srcs, dsts = refs[:n], refs[n:2 * n]
        send_sems, recv_sems, fwd_send_sems, fwd_recv_sems = refs[2 * n:]
        x, y, c, chips = _place()
        q = 2 * x + y
        sibling = (x, y, 1 - c)
        sends = []
        for r in range(AG_CHUNKS):
            for t in range(n):
                hc = srcs[t].shape[0] // 2 // AG_CHUNKS
                rows = pl.ds(c * (hc * AG_CHUNKS) + r * hc, hc)
                for j, (cx, cy) in enumerate(chips):
                    k = (t * 3 + j) * AG_CHUNKS + r
                    cp = pltpu.make_async_remote_copy(src_ref=srcs[t].at[rows], dst_ref=dsts[t].at[q, rows],
                                                      send_sem=send_sems.at[k], recv_sem=recv_sems.at[k],
                                                      device_id=(cx, cy, c), device_id_type=MESH)
                    cp.start()
                    sends.append(cp)
        fwds = []
        for r in range(AG_CHUNKS):
            for t in range(n):
                hc = srcs[t].shape[0] // 2 // AG_CHUNKS
                sub = hc // D2D_SPLIT
                for j, (cx, cy) in enumerate(chips):
                    k = (t * 3 + j) * AG_CHUNKS + r
                    base = c * (hc * AG_CHUNKS) + r * hc
                    part = dsts[t].at[2 * cx + cy, pl.ds(base, hc)]
                    pltpu.make_async_remote_copy(src_ref=part, dst_ref=part, send_sem=send_sems.at[k], recv_sem=recv_sems.at[k],
                                                 device_id=(cx, cy, c), device_id_type=MESH).wait_recv()
                    for u in range(D2D_SPLIT):
                        piece = dsts[t].at[2 * cx + cy, pl.ds(base + u * sub, sub)]
                        pltpu.make_async_remote_copy(src_ref=piece, dst_ref=piece, send_sem=fwd_send_sems.at[k],
                                                     recv_sem=fwd_recv_sems.at[k], device_id=sibling, device_id_type=MESH).start()
                    fwds.append((part, k))
        for r in range(AG_CHUNKS):
            for t in range(n):
                hc = srcs[t].shape[0] // 2 // AG_CHUNKS
                for j, (cx, cy) in enumerate(chips):
                    k = (t * 3 + j) * AG_CHUNKS + r
                    part = dsts[t].at[2 * cx + cy, pl.ds((1 - c) * (hc * AG_CHUNKS) + r * hc, hc)]
                    pltpu.make_async_remote_copy(src_ref=part, dst_ref=part, send_sem=fwd_send_sems.at[k],
                                                 recv_sem=fwd_recv_sems.at[k], device_id=sibling, device_id_type=MESH).wait_recv()
        for cp in sends:
            cp.wait_send()
        for part, k in fwds:
            pltpu.make_async_remote_copy(src_ref=part, dst_ref=part, send_sem=fwd_send_sems.at[k], recv_sem=fwd_recv_sems.at[k],
                                         device_id=sibling, device_id_type=MESH).wait_send()

    return pl.pallas_call(
        body, name="all_gather_weights", in_specs=[ANY] * n, out_specs=[ANY] * n,
        out_shape=[jax.ShapeDtypeStruct((N_CHIP,) + s.shape, s.dtype) for s in shards],
        scratch_shapes=[pltpu.SemaphoreType.DMA((nsem,)), pltpu.SemaphoreType.DMA((nsem,)), pltpu.SemaphoreType.DMA((nsem,)),
                        pltpu.SemaphoreType.DMA((nsem,))],
    )(*shards)


def _exchange_halves(grads, tag=""):
    n = len(grads)

    def body(*refs):
        srcs, theirs = refs[:n], refs[n:2 * n]
        send_sems, recv_sems = refs[2 * n:]
        x, y, c, _ = _place()
        sibling = (x, y, 1 - c)
        waits = []
        for t in range(n):
            h = srcs[t].shape[1] // 2
            sub = h // D2D_SPLIT
            for s in range(N_CHIP):
                for u in range(D2D_SPLIT):
                    pltpu.make_async_remote_copy(src_ref=srcs[t].at[s, pl.ds((1 - c) * h + u * sub, sub)],
                                                 dst_ref=theirs[t].at[s, pl.ds(u * sub, sub)],
                                                 send_sem=send_sems.at[t], recv_sem=recv_sems.at[t],
                                                 device_id=sibling, device_id_type=MESH).start()
            waits.append(pltpu.make_async_remote_copy(src_ref=srcs[t].at[:, pl.ds((1 - c) * h, h)], dst_ref=theirs[t],
                                                      send_sem=send_sems.at[t], recv_sem=recv_sems.at[t],
                                                      device_id=sibling, device_id_type=MESH))
        for whole in waits:
            whole.wait()

    half = [jax.ShapeDtypeStruct((N_CHIP, g.shape[1] // 2, g.shape[2]), g.dtype) for g in grads]
    return pl.pallas_call(
        body, name=f"exchange_halves_{tag}", in_specs=[ANY] * n, out_specs=[ANY] * n, out_shape=half,
        scratch_shapes=[pltpu.SemaphoreType.DMA((n,)), pltpu.SemaphoreType.DMA((n,))],
    )(*grads)


def _chip_exchange_comm(parts):
    n = len(parts)

    def copies(srcs, dsts, sems):
        send_sems, recv_sems, local_sems = sems
        x, y, c, chips = _place()
        q = 2 * x + y
        owns = [pltpu.make_async_copy(srcs[t].at[q], dsts[t].at[q], local_sems.at[t]) for t in range(n)]
        sends, recvs = [], []
        for t in range(n):
            for j, (cx, cy) in enumerate(chips):
                args = dict(send_sem=send_sems.at[3 * t + j], recv_sem=recv_sems.at[3 * t + j],
                            device_id=(cx, cy, c), device_id_type=MESH)
                sends.append(pltpu.make_async_remote_copy(src_ref=srcs[t].at[2 * cx + cy], dst_ref=dsts[t].at[q], **args))
                part = dsts[t].at[2 * cx + cy]
                recvs.append(pltpu.make_async_remote_copy(src_ref=part, dst_ref=part, **args))
        return owns, sends, recvs

    def start(srcs, dsts, sems):
        owns, sends, _ = copies(srcs, dsts, sems)
        for cp in owns + sends:
            cp.start()

    def finish(srcs, dsts, sems):
        owns, sends, recvs = copies(srcs, dsts, sems)
        for cp in recvs:
            cp.wait_recv()
        for cp in sends:
            cp.wait_send()
        for cp in owns:
            cp.wait()

    return _Comm(parts, [jax.ShapeDtypeStruct(p.shape, p.dtype) for p in parts],
                 [pltpu.SemaphoreType.DMA((3 * n,)), pltpu.SemaphoreType.DMA((3 * n,)), pltpu.SemaphoreType.DMA((n,))], start, finish)


def _gather_relay_comm(shard):
    h = shard.shape[0] // 2
    sub = h // D2D_SPLIT

    def place_and_copies(srcs, dsts, sems):
        send_sems, recv_sems, fwd_send_sems, fwd_recv_sems = sems
        x, y, c, chips = _place()
        q = 2 * x + y
        mine, theirs = pl.ds(c * h, h), pl.ds((1 - c) * h, h)
        sends, recvs, relays, relayed = [], [], [], []
        for j, (cx, cy) in enumerate(chips):
            ici = dict(send_sem=send_sems.at[j], recv_sem=recv_sems.at[j], device_id=(cx, cy, c), device_id_type=MESH)
            d2d = dict(send_sem=fwd_send_sems.at[j], recv_sem=fwd_recv_sems.at[j], device_id=(x, y, 1 - c), device_id_type=MESH)
            sends.append(pltpu.make_async_remote_copy(src_ref=srcs[0].at[mine], dst_ref=dsts[0].at[q, mine], **ici))
            landed = dsts[0].at[2 * cx + cy, mine]
            recvs.append(pltpu.make_async_remote_copy(src_ref=landed, dst_ref=landed, **ici))
            pieces = [dsts[0].at[2 * cx + cy, pl.ds(c * h + u * sub, sub)] for u in range(D2D_SPLIT)]
            relays.append(([pltpu.make_async_remote_copy(src_ref=p, dst_ref=p, **d2d) for p in pieces],
                           pltpu.make_async_remote_copy(src_ref=landed, dst_ref=landed, **d2d)))
            other = dsts[0].at[2 * cx + cy, theirs]
            relayed.append(pltpu.make_async_remote_copy(src_ref=other, dst_ref=other, **d2d))
        return sends, recvs, relays, relayed

    def start(srcs, dsts, sems):
        for cp in place_and_copies(srcs, dsts, sems)[0]:
            cp.start()

    def middle(srcs, dsts, sems):
        _, recvs, relays, _ = place_and_copies(srcs, dsts, sems)
        for cp, (pieces, _) in zip(recvs, relays):
            cp.wait_recv()
            for piece in pieces:
                piece.start()

    def finish(srcs, dsts, sems):
        sends, _, relays, relayed = place_and_copies(srcs, dsts, sems)
        for cp in relayed:
            cp.wait_recv()
        for cp in sends:
            cp.wait_send()
        for _, whole in relays:
            whole.wait_send()

    comm = _Comm([shard], [jax.ShapeDtypeStruct((N_CHIP,) + shard.shape, shard.dtype)],
                 [pltpu.SemaphoreType.DMA((3,))] * 4, start, finish)
    comm.middle = middle
    return comm


def _all_gather_small_comm(v):
    def copies(srcs, dsts, sems):
        send_sems, recv_sems, local_sems = sems
        x, y, c, _ = _place()
        me = 4 * x + 2 * y + c
        own = pltpu.make_async_copy(srcs[0], dsts[0].at[me], local_sems.at[0])
        sends, recvs = [], []
        for k in range(1, N_DEV):
            tx = 1 - x if k & 4 else x
            ty = 1 - y if k & 2 else y
            tc = 1 - c if k & 1 else c
            args = dict(send_sem=send_sems.at[k - 1], recv_sem=recv_sems.at[k - 1], device_id=(tx, ty, tc), device_id_type=MESH)
            sends.append(pltpu.make_async_remote_copy(src_ref=srcs[0], dst_ref=dsts[0].at[me], **args))
            landed = dsts[0].at[4 * tx + 2 * ty + tc]
            recvs.append(pltpu.make_async_remote_copy(src_ref=landed, dst_ref=landed, **args))
        return own, sends, recvs

    def start(srcs, dsts, sems):
        own, sends, _ = copies(srcs, dsts, sems)
        own.start()
        for cp in sends:
            cp.start()

    def finish(srcs, dsts, sems):
        own, sends, recvs = copies(srcs, dsts, sems)
        for cp in recvs:
            cp.wait_recv()
        for cp in sends:
            cp.wait_send()
        own.wait()

    return _Comm([v], [jax.ShapeDtypeStruct((N_DEV,) + v.shape, v.dtype)],
                 [pltpu.SemaphoreType.DMA((N_DEV - 1,)), pltpu.SemaphoreType.DMA((N_DEV - 1,)), pltpu.SemaphoreType.DMA((1,))],
                 start, finish)


def _merge_comms(a, b):
    na, nao, nas = len(a.ins), len(a.outs), len(a.sems)

    def start(ci, co, cs):
        a.start(ci[:na], co[:nao], cs[:nas])
        b.start(ci[na:], co[nao:], cs[nas:])

    def finish(ci, co, cs):
        a.finish(ci[:na], co[:nao], cs[:nas])
        b.finish(ci[na:], co[nao:], cs[nas:])

    return _Comm(a.ins + b.ins, a.outs + b.outs, a.sems + b.sems, start, finish)


def _run_comm(comm, name):
    n_ci, n_co = len(comm.ins), len(comm.outs)

    def body(*refs):
        ci, co, cs = refs[:n_ci], refs[n_ci:n_ci + n_co], refs[n_ci + n_co:]
        comm.start(ci, co, cs)
        if comm.middle is not None:
            comm.middle(ci, co, cs)
        comm.finish(ci, co, cs)

    return pl.pallas_call(body, name=name, in_specs=[ANY] * n_ci, out_specs=[ANY] * n_co, out_shape=comm.outs,
                          scratch_shapes=comm.sems)(*comm.ins)


def _share_halves(halves, tag=""):
    n = len(halves)

    def body(*refs):
        srcs, dsts = refs[:n], refs[n:2 * n]
        send_sems, recv_sems = refs[2 * n:]
        x, y, c, _ = _place()
        sibling = (x, y, 1 - c)
        for t in range(n):
            h = srcs[t].shape[0]
            sub = h // (2 * D2D_SPLIT)
            for u in range(2 * D2D_SPLIT):
                pltpu.make_async_remote_copy(src_ref=srcs[t].at[pl.ds(u * sub, sub)], dst_ref=dsts[t].at[pl.ds(c * h + u * sub, sub)],
                                             send_sem=send_sems.at[t], recv_sem=recv_sems.at[t],
                                             device_id=sibling, device_id_type=MESH).start()
        for t in range(n):
            h = srcs[t].shape[0]
            pltpu.make_async_remote_copy(src_ref=srcs[t], dst_ref=dsts[t].at[pl.ds((1 - c) * h, h)], send_sem=send_sems.at[t],
                                         recv_sem=recv_sems.at[t], device_id=sibling, device_id_type=MESH).wait()

    return pl.pallas_call(
        body, name=f"share_halves_{tag}", in_specs=[ANY] * n, out_specs=[ANY] * n,
        out_shape=[jax.ShapeDtypeStruct((2 * h.shape[0], h.shape[1]), h.dtype) for h in halves],
        scratch_shapes=[pltpu.SemaphoreType.DMA((n,)), pltpu.SemaphoreType.DMA((n,))],
    )(*halves)


def _gather_weights(shards, chip):
    gathered = _all_gather_weights(shards)
    return [lax.dynamic_update_slice(g, s[None], (chip, 0, 0)) for g, s in zip(gathered, shards)]


def _reduce_scatter_begin(sends, core, tag):
    theirs = _exchange_halves(sends, tag)
    mines = [lax.dynamic_slice(g, (0, core * (g.shape[1] // 2), 0), (N_CHIP, g.shape[1] // 2, g.shape[2])) for g in sends]
    pair = [_sum_leading((m, t), f"pair_sum_{tag}{i}", out_dtype=BF16) for i, (m, t) in enumerate(zip(mines, theirs))]
    return _chip_exchange_comm(pair)


def _reduce_scatter_end(contrib, core, tag):
    halves = [_sum_leading(c, f"chip_sum_{tag}{i}") for i, c in enumerate(contrib)]
    shared = _share_halves(halves, tag)
    return [lax.dynamic_update_slice(full, mine, (core * mine.shape[0], 0)) for full, mine in zip(shared, halves)]


def _reduce_scatter(sends, core, tag=""):
    contrib = _run_comm(_reduce_scatter_begin(sends, core, tag), f"exchange_chips_{tag}")
    return _reduce_scatter_end(contrib, core, tag)


def _pack(arrs):
    rows = []
    for a in arrs:
        flat = a.astype(F32).reshape(-1)
        pad = (-flat.shape[0]) % 1024
        rows.append(jnp.pad(flat, (0, pad)).reshape(-1, 128))
    return jnp.concatenate(rows, axis=0)


def _unpack(buf, shapes):
    out, r = [], 0
    for shp in shapes:
        size = 1
        for d in shp:
            size *= d
        nr = (size + 1023) // 1024 * 8
        out.append(buf[r:r + nr].reshape(-1)[:size].reshape(shp))
        r += nr
    return out


R_BG, R_BS, R_OUT, R_FF = D // N_CHIP, DI // N_CHIP, D // N_CHIP, DFF // N_CHIP


def _pack_rest_shard(w_bg, w_bs, w_out, w_ff2, w_ff1):
    return jnp.concatenate([w_bg, w_bs, w_out, w_ff2, w_ff1], axis=0).astype(BF16)


def _unpack_rest(g):
    o1, o2, o3, o4 = R_BG, R_BG + R_BS, R_BG + R_BS + R_OUT, R_BG + R_BS + R_OUT + R_FF
    return (g[:, :o1].reshape(D, D), g[:, o1:o2].reshape(DI, D), g[:, o2:o3].reshape(D, D),
            jnp.transpose(g[:, o4:], (1, 0, 2)).reshape(D, DFF), g[:, o3:o4].reshape(DFF, D))


def _local_step(x, t, mod, w_in_r, rest, gm_norm_w, gm_ws, gm_bs, conv_w, conv_b, dt_bias, a_log, d_skip, ssm_norm_w,
                final_norm_w, place=None):
    sh1, sc1, g1, sh2, sc2, g2 = [mod[:, i * D:(i + 1) * D] for i in range(6)]
    ws_b = jnp.where(jnp.tril(jnp.ones((Q, Q), bool))[None], gm_ws, 0.0).astype(BF16)
    bs_t = gm_bs.T
    pad32 = lambda v: jnp.pad(v, ((0, 0), (0, 128 - NHEAD)))
    dtb, alog, dsk = pad32(dt_bias), pad32(a_log), pad32(d_skip)

    h1 = _prenorm(x, sc1, sh1)
    gdt = (F32,) if place is None else (BF16,)
    if place is None:
        (proj,) = _matmul(h1, w_in_r, "nn", name="mm_proj", out_dtypes=(BF16,), tn=1152)
        w_bg, w_bs, w_out, w_ff1, w_ff2 = rest
    else:
        chip, core = place
        proj, g_rest = _matmul(h1, w_in_r, "nn", name="mm_proj", out_dtypes=(BF16,), tn=1152, comm=_gather_relay_comm(rest))
        w_bg, w_bs, w_out, w_ff1, w_ff2 = _unpack_rest(lax.dynamic_update_slice(g_rest, rest[None], (chip, 0, 0)))
    (dt_raw,) = _matmul(h1, w_in_r[:, OFF_DT:OFF_DT + 128], "nn", name="mm_dt", tn=128)
    y_a = _sgu_fwd(proj, gm_norm_w, ws_b, bs_t)
    xbc = _conv_fwd(proj, conv_w, conv_b)
    dt, cum, cum_t = _dt_fwd(dt_raw, dtb, alog)
    y_ssd, states = _ssd_fwd(xbc, dt, cum, cum_t, dsk)
    y_b = _gatenorm_fwd(y_ssd, proj, ssm_norm_w)
    (ba,) = _matmul(y_a, w_bg, "nn", name="mm_branch_gm", out_dtypes=(BF16,))
    (bb,) = _matmul(y_b, w_bs, "nn", name="mm_branch_ssm", out_dtypes=(BF16,), tm=512, tk=2048)
    mixed = _mix_fwd(proj, ba, bb)
    (mo,) = _matmul(mixed, w_out, "nn", name="mm_out", out_dtypes=(BF16,))
    x1, h2 = _resid_norm(x, mo, g1, sc2, sh2)
    f, act = _matmul(h2, w_ff1, "nn", name="mm_ff1", out_dtypes=(BF16, BF16),
                     epi=lambda acc: (acc, jnp.square(jnp.maximum(acc, 0.0))))
    (fo,) = _matmul(act, w_ff2, "nn", name="mm_ff2", out_dtypes=(BF16,), tm=512, tk=4096)
    loss8, dx2, dfo, dg2, dfw = _loss_head(x1, fo, g2, final_norm_w, t)

    (df,) = _matmul(dfo, w_ff2, "nt", name="mm_ff2_dx", out_dtypes=(BF16,), epi_ins=(f,),
                    epi=lambda acc, fv: (acc * (2.0 * jnp.maximum(fv.astype(F32), 0.0)),))
    (g_ff2,) = _matmul(act, dfo, "tn", name="mm_ff2_dw", out_dtypes=gdt, tm=512, tk=4096)
    (dh2,) = _matmul(df, w_ff1, "nt", name="mm_ff1_dx", out_dtypes=(BF16,), tm=512, tk=4096)
    (g_ff1,) = _matmul(h2, df, "tn", name="mm_ff1_dw", out_dtypes=gdt, tm=512, tk=4096)
    dx1, dmo, dg1, dsc2, dsh2 = _resid_norm_bwd(x1, mo, g1, sc2, sh2, dh2, dx2)
    (dmixed,) = _matmul(dmo, w_out, "nt", name="mm_out_dx", out_dtypes=(BF16,))
    (g_out,) = _matmul(mixed, dmo, "tn", name="mm_out_dw", out_dtypes=gdt, tm=512, tk=4096)
    early = None
    if place is not None:
        stack = jnp.concatenate([g_out.reshape(N_CHIP, R_OUT, D), g_ff2.reshape(N_CHIP, R_FF, D),
                                 jnp.transpose(g_ff1.reshape(D, N_CHIP, D), (1, 0, 2))], axis=1).astype(BF16)
        early = _reduce_scatter_begin([stack], core, "early")
    dproj, dba, dbb = _mix_bwd(proj, ba, bb, dmixed)
    (dy_a,) = _matmul(dba, w_bg, "nt", name="mm_branch_gm_dx", out_dtypes=(BF16,))
    (g_bg,) = _matmul(y_a, dba, "tn", name="mm_branch_gm_dw", out_dtypes=gdt, tm=512, tk=4096)
    (dy_b,) = _matmul(dbb, w_bs, "nt", name="mm_branch_ssm_dx", out_dtypes=(BF16,))
    (g_bs,) = _matmul(y_b, dbb, "tn", name="mm_branch_ssm_dw", out_dtypes=gdt, tm=512, tk=4096)
    dproj, d_gm_norm, d_ws, d_bs_t = _sgu_bwd(proj, gm_norm_w, ws_b, bs_t, dy_a, dproj)
    dy_ssd, dy_ssd_t, dproj, d_ssm_norm = _gatenorm_bwd(y_ssd, proj, ssm_norm_w, dy_b, dproj)
    dxs, dbm, dcm, ddt, dcum, dcum_t, d_dsk, *early_contrib = _ssd_bwd(xbc, dt, cum, cum_t, dsk, states, dy_ssd, dy_ssd_t,
                                                                       comm=early)
    dproj, d_dtb, d_alog = _dt_bwd(dt_raw, dtb, alog, ddt, dcum, dcum_t, dproj)
    dproj, dw_x, db_x = _conv_bwd(proj, conv_w, conv_b, dxs, dproj, 0, "conv_bwd_x")
    dproj, dw_b, db_b = _conv_bwd(proj, conv_w, conv_b, dbm, dproj, DI, "conv_bwd_b")
    dproj, dw_c, db_c = _conv_bwd(proj, conv_w, conv_b, dcm, dproj, DI + D, "conv_bwd_c")
    d_conv_w = jnp.concatenate([dw_x, dw_b, dw_c], axis=1)
    d_conv_b = jnp.concatenate([db_x, db_b, db_c], axis=1)
    (g_in_r,) = _matmul(h1, dproj, "tn", name="mm_proj_dw", out_dtypes=gdt, tm=512, tn=1152, tk=4096)
    late = None
    if place is not None:
        send_in = _w_in_to_blocks(g_in_r)
        stack = jnp.concatenate([g_bg.reshape(N_CHIP, R_BG, D), g_bs.reshape(N_CHIP, R_BS, D)], axis=1).astype(BF16)
        late = _reduce_scatter_begin([send_in, stack], core, "late")
    small = dict(gm_ws=d_ws, gm_norm_w=d_gm_norm, gm_bs=d_bs_t.T, conv_w=d_conv_w, conv_b=d_conv_b,
                 dt_bias=d_dtb[:, :NHEAD], a_log=d_alog[:, :NHEAD], d_skip=d_dsk[:, :NHEAD],
                 ssm_norm_w=d_ssm_norm, final_norm_w=dfw, loss=loss8[:1, :1],
                 dmod=jnp.concatenate([jnp.zeros((1, 2 * D), F32), dg1, dsh2, dsc2, dg2], axis=1))
    if place is not None:
        late = _merge_comms(late, _all_gather_small_comm(_pack([small[k] for k in SMALL_KEYS])))
    dh1, *rode = _matmul(dproj, w_in_r, "nt", name="mm_proj_dx", out_dtypes=(BF16,), tk=3456, comm=late)
    grad_x, dsc1, dsh1 = _prenorm_bwd(x, sc1, sh1, dh1, dx1)
    first = jnp.concatenate([dsh1, dsc1], axis=1)
    if place is None:
        small["dmod"] = jnp.concatenate([first, small["dmod"][:, 2 * D:]], axis=1)
        return grad_x, small, dict(w_in_r=g_in_r, w_bg=g_bg, w_bs=g_bs, w_out=g_out, w_ff1=g_ff1, w_ff2=g_ff2)
    late_contrib, small_all = rode[:2], rode[2]
    first_all = _all_gather_small(_pack([first]), "all_gather_mod_grads")
    (s_early,) = _reduce_scatter_end(early_contrib, core, "early")
    s_in, s_late = _reduce_scatter_end(late_contrib, core, "late")
    o1, o2 = R_OUT, R_OUT + R_FF
    big = dict(w_in=s_in, w_bg=s_late[:R_BG], w_bs=s_late[R_BG:], w_out=s_early[:o1], w_ff2=s_early[o1:o2], w_ff1=s_early[o2:])
    return grad_x, (small_all, first_all), big


SMALL_KEYS = ("gm_ws", "gm_norm_w", "gm_bs", "conv_w", "conv_b", "dt_bias", "a_log", "d_skip", "ssm_norm_w",
              "final_norm_w", "dmod", "loss")
SMALL_SHAPES = ((GM_G, Q, Q), (1, D), (GM_G, Q), (4, CONV), (1, CONV), (1, NHEAD), (1, NHEAD), (1, NHEAD), (1, DI),
                (1, D), (1, 6 * D), (1, 1))


def _reorder_w_in(w_full):
    k = w_full.shape[0]
    return jnp.concatenate([w_full[:, :8192], w_full[:, 8224:], w_full[:, 8192:8224],
                            jnp.zeros((k, W_IN_R - W_IN), w_full.dtype)], axis=1)


W_SH = W_IN // N_CHIP
TAIL = 8192 - (N_CHIP - 1) * W_SH


def _w_in_from_blocks(g):
    last = g[N_CHIP - 1]
    return jnp.concatenate([g[0], g[1], g[2], last[:, :TAIL], last[:, TAIL + NHEAD:], last[:, TAIL:TAIL + NHEAD],
                            jnp.zeros((g.shape[1], W_IN_R - W_IN), g.dtype)], axis=1)


def _w_in_to_blocks(g_r):
    cut = (N_CHIP - 1) * W_SH
    last = jnp.concatenate([g_r[:, cut:8192], g_r[:, OFF_DT:OFF_DT + NHEAD], g_r[:, 8192:OFF_DT]], axis=1)
    return jnp.stack([g_r[:, :W_SH], g_r[:, W_SH:2 * W_SH], g_r[:, 2 * W_SH:cut], last], axis=0)


def _restore_w_in(g_r):
    return jnp.concatenate([g_r[:, :8192], g_r[:, OFF_DT:OFF_DT + NHEAD], g_r[:, 8192:OFF_DT]], axis=1)


def kernel(x, c, w_mod, b_mod, w_in, gm_norm_w, gm_ws, gm_bs, conv_w, conv_b, dt_bias, a_log, d_skip, ssm_norm_w, w_branch_gm, w_branch_ssm, w_out, w_ff1, w_ff2, final_norm_w, loss_target, m_w_mod, m_b_mod, m_w_in, m_gm_norm_w, m_gm_ws, m_gm_bs, m_conv_w, m_conv_b, m_dt_bias, m_a_log, m_d_skip, m_ssm_norm_w, m_w_branch_gm, m_w_branch_ssm, m_w_out, m_w_ff1, m_w_ff2, m_final_norm_w, v_w_mod, v_b_mod, v_w_in, v_gm_norm_w, v_gm_ws, v_gm_bs, v_conv_w, v_conv_b, v_dt_bias, v_a_log, v_d_skip, v_ssm_norm_w, v_w_branch_gm, v_w_branch_ssm, v_w_out, v_w_ff1, v_w_ff2, v_final_norm_w):
    ax, ay, ac = lax.axis_index("x"), lax.axis_index("y"), lax.axis_index("c")
    chip = 2 * ax + ay
    dev = 2 * chip + ac
    seq = x.shape[1]
    nmod = w_mod.shape[2]

    first = jnp.concatenate([c, conv_w[0], jnp.zeros((3, D), F32)], axis=0)
    first_all = _all_gather_small(first, "all_gather_cond")
    c8 = first_all[:, 0, :]
    conv_w_full = jnp.concatenate([first_all[2 * k, 1:5, :] for k in range(N_CHIP)], axis=1)
    b_sl = lax.dynamic_slice(b_mod, (0, chip * nmod), (1, nmod))
    mod_part = _mod_fwd(c8, w_mod[0], b_sl)
    mod_all = _all_gather_small(mod_part, "all_gather_mod")
    mod = jnp.concatenate([lax.dynamic_slice(mod_all, (2 * k, dev, 0), (1, 1, nmod))[0] for k in range(N_CHIP)], axis=1)

    (g_in,) = _gather_weights([w_in[0].astype(BF16)], chip)
    w_in_r = _w_in_from_blocks(g_in)
    rest = _pack_rest_shard(w_branch_gm[0], w_branch_ssm[0], w_out[0], w_ff2[0], w_ff1[0])

    grad_x, small, big = _local_step(
        x[0], loss_target[0], mod, w_in_r, rest, gm_norm_w, gm_ws[0], gm_bs[0], conv_w_full, conv_b, dt_bias, a_log,
        d_skip, ssm_norm_w, final_norm_w.reshape(1, D), place=(chip, ac))

    small_all, first_all = small
    small_sum = _sum_leading(small_all, "sum_small_grads")
    first_sum = _sum_leading(first_all, "sum_mod_grads")
    s_ws, s_gnw, s_bs, s_cw, s_cb, s_dtb, s_alog, s_dsk, s_snw, s_fnw, s_bmod, s_loss = _unpack(small_sum, SMALL_SHAPES)
    head = (1, 2 * D)
    s_bmod = jnp.concatenate([_unpack(first_sum, [head])[0], s_bmod[:, 2 * D:]], axis=1)
    dmod_all = jnp.stack([jnp.concatenate([_unpack(first_all[k], [head])[0][0], _unpack(small_all[k], SMALL_SHAPES)[10][0][2 * D:]])
                          for k in range(N_DEV)], axis=0)
    g_w_mod = _mod_wgrad(c8, lax.dynamic_slice(dmod_all, (0, chip * nmod), (N_DEV, nmod)))
    g_conv_w = lax.dynamic_slice(s_cw, (0, chip * (CONV // N_CHIP)), (4, CONV // N_CHIP))

    g_w_in, g_w_bg, g_w_bs, g_w_out, g_w_ff2, g_w_ff1 = (big[k] for k in ("w_in", "w_bg", "w_bs", "w_out", "w_ff2", "w_ff1"))

    def adam_big(w, g, m, v, name):
        d, m2, v2 = _adam(w.reshape(g.shape), g, m.reshape(g.shape), v.reshape(g.shape), name)
        return g.reshape(w.shape), d.reshape(w.shape), m2.reshape(w.shape), v2.reshape(w.shape)

    res = {}
    res["w_mod"] = adam_big(w_mod, g_w_mod, m_w_mod, v_w_mod, "adam_w_mod")
    res["w_in"] = adam_big(w_in, g_w_in, m_w_in, v_w_in, "adam_w_in")
    res["w_branch_gm"] = adam_big(w_branch_gm, g_w_bg, m_w_branch_gm, v_w_branch_gm, "adam_w_branch_gm")
    res["w_branch_ssm"] = adam_big(w_branch_ssm, g_w_bs, m_w_branch_ssm, v_w_branch_ssm, "adam_w_branch_ssm")
    res["w_out"] = adam_big(w_out, g_w_out, m_w_out, v_w_out, "adam_w_out")
    res["w_ff1"] = adam_big(w_ff1, g_w_ff1, m_w_ff1, v_w_ff1, "adam_w_ff1")
    res["w_ff2"] = adam_big(w_ff2, g_w_ff2, m_w_ff2, v_w_ff2, "adam_w_ff2")

    names = ("b_mod", "gm_norm_w", "gm_ws", "gm_bs", "conv_w", "conv_b", "dt_bias", "a_log", "d_skip", "ssm_norm_w", "final_norm_w")
    ws = (b_mod, gm_norm_w, gm_ws, gm_bs, conv_w, conv_b, dt_bias, a_log, d_skip, ssm_norm_w, final_norm_w)
    ms = (m_b_mod, m_gm_norm_w, m_gm_ws, m_gm_bs, m_conv_w, m_conv_b, m_dt_bias, m_a_log, m_d_skip, m_ssm_norm_w, m_final_norm_w)
    vs = (v_b_mod, v_gm_norm_w, v_gm_ws, v_gm_bs, v_conv_w, v_conv_b, v_dt_bias, v_a_log, v_d_skip, v_ssm_norm_w, v_final_norm_w)
    gs = (s_bmod, s_gnw, s_ws, s_bs, g_conv_w, s_cb, s_dtb, s_alog, s_dsk, s_snw, s_fnw)
    gs = [g.reshape(w.shape) for g, w in zip(gs, ws)]
    shapes = [w.shape for w in ws]
    d_p, m_p, v_p = _adam(_pack(ws), _pack(gs), _pack(ms), _pack(vs), "adam_small")
    for name, g, d, m2, v2 in zip(names, gs, _unpack(d_p, shapes), _unpack(m_p, shapes), _unpack(v_p, shapes)):
        res[name] = (g, d, m2, v2)

    order = ("w_mod", "b_mod", "w_in", "gm_norm_w", "gm_ws", "gm_bs", "conv_w", "conv_b", "dt_bias", "a_log", "d_skip",
             "ssm_norm_w", "w_branch_gm", "w_branch_ssm", "w_out", "w_ff1", "w_ff2", "final_norm_w")
    loss = s_loss.reshape(())
    return (loss, grad_x.reshape(x.shape), *[res[k][0] for k in order], *[res[k][1] for k in order],
            *[res[k][2] for k in order], *[res[k][3] for k in order])
```
